```python
import math
import jax, jax.numpy as jnp
from jax import lax
import numpy as np

D_MODEL = 1024
BATCH = 8
SEQ = 2048
DEPTH = 2
DEC_BATCH = 128
DEC_SEQ = 1
PAST_LEN = 16384
PAGE_SIZE = 128

A_HEADS = 4
A_DH = 64
A_WIDTH = A_HEADS * A_DH
S5_P = 16
S5_N = 64
S5_WIDTH = D_MODEL // 2
S5_GROUPS = S5_WIDTH // S5_P
C_HEADS = 4
C_DH = 64
C_WIDTH = C_HEADS * C_DH
CONV_K = 4
MIX_WIDTH = A_WIDTH + S5_WIDTH + C_WIDTH
MEM_TOKENS = 256
X_HEADS = 4
X_DH = D_MODEL // X_HEADS
D_FF = (((8 * D_MODEL + 2) // 3 + 255) // 256) * 256
CHUNK = 64
EPS = 1e-6
IN_SIZES = [A_WIDTH, A_WIDTH, A_WIDTH, A_WIDTH, A_HEADS, A_HEADS,
            S5_WIDTH,
            3 * C_WIDTH, C_WIDTH, C_HEADS, C_HEADS]
IN_WIDTH = sum(IN_SIZES)
IN_SPLITS = [int(s) for s in np.cumsum(IN_SIZES)[:-1]]

kernel_name = 'hymba_mlstm_s5_gdn_step'


def rmsnorm(x, g):
    x32 = x.astype(jnp.float32)
    y = x32 * lax.rsqrt(jnp.mean(x32 * x32, axis=-1, keepdims=True) + EPS)
    return (y * g.astype(jnp.float32)).astype(x.dtype)


def l2norm(x):
    return x * lax.rsqrt(jnp.sum(x * x, axis=-1, keepdims=True) + EPS)


def to_chunks(x, L):
    B, S, H = x.shape[:3]
    x = x.reshape((B, S // L, L, H) + x.shape[3:])
    return jnp.moveaxis(x, (1, 2), (0, 3))


def from_chunks(y):
    y = jnp.moveaxis(y, (0, 3), (1, 2))
    return y.reshape((y.shape[0], y.shape[1] * y.shape[2]) + y.shape[3:])


def mlstm_chunked(q, k, v, i_pre, f_pre, c0, n0, m0):
    S = q.shape[1]
    L = math.gcd(S, CHUNK)
    tril = jnp.tril(jnp.ones((L, L), dtype=bool))
    logf = jax.nn.log_sigmoid(f_pre)
    xs = (to_chunks(q, L), to_chunks(k, L), to_chunks(v, L), to_chunks(i_pre, L), to_chunks(logf, L))

    def step(carry, inp):
        c, n, m = carry
        qc, kc, vc, ic, lfc = inp
        b = jnp.cumsum(lfc, axis=-1)
        d_intra = jnp.where(tril, b[..., :, None] - b[..., None, :] + ic[..., None, :], -jnp.inf)
        d_inter = b + m[..., None]
        m_t = jnp.maximum(jnp.max(d_intra, axis=-1), d_inter)
        w_intra = jnp.exp(d_intra - m_t[..., None])
        w_inter = jnp.exp(d_inter - m_t)
        s = jnp.einsum('bhtd,bhjd->bhtj', qc, kc) * w_intra
        num = w_inter[..., None] * jnp.einsum('bhtd,bhde->bhte', qc, c) + jnp.einsum('bhtj,bhje->bhte', s, vc)
        den = w_inter * jnp.einsum('bhtd,bhd->bht', qc, n) + jnp.sum(s, axis=-1)
        h = num / jnp.maximum(jnp.abs(den), jnp.exp(-m_t))[..., None]
        b_last = b[..., -1]
        g = b_last[..., None] - b + ic
        m_new = jnp.maximum(b_last + m, jnp.max(g, axis=-1))
        w_k = jnp.exp(g - m_new[..., None])
        decay = jnp.exp(b_last + m - m_new)
        c_new = decay[..., None, None] * c + jnp.einsum('bhj,bhjd,bhje->bhde', w_k, kc, vc)
        n_new = decay[..., None] * n + jnp.einsum('bhj,bhjd->bhd', w_k, kc)
        return (c_new, n_new, m_new), h

    (c1, n1, m1), h = lax.scan(step, (c0, n0, m0), xs)
    return from_chunks(h), c1, n1, m1


def s5_scan(u, h0_re, h0_im, a_re, a_im, log_dt, b_re, b_im, c_re, c_im, d):
    f32 = jnp.float32
    a_re, a_im, b_re, b_im = a_re.astype(f32), a_im.astype(f32), b_re.astype(f32), b_im.astype(f32)
    dt = jnp.exp(log_dt.astype(f32))[:, None]
    mag = jnp.exp(a_re * dt)
    lam_re, lam_im = mag * jnp.cos(a_im * dt), mag * jnp.sin(a_im * dt)
    nr, ni = lam_re - 1.0, lam_im
    den = a_re * a_re + a_im * a_im
    coef_re = (nr * a_re + ni * a_im) / den
    coef_im = (ni * a_re - nr * a_im) / den
    bb_re = coef_re[..., None] * b_re - coef_im[..., None] * b_im
    bb_im = coef_re[..., None] * b_im + coef_im[..., None] * b_re
    bu_re = jnp.einsum('bsgp,gnp->bsgn', u, bb_re)
    bu_im = jnp.einsum('bsgp,gnp->bsgn', u, bb_im)
    ar = jnp.broadcast_to(lam_re, bu_re.shape)
    ai = jnp.broadcast_to(lam_im, bu_re.shape)

    def combine(e1, e2):
        a1r, a1i, b1r, b1i = e1
        a2r, a2i, b2r, b2i = e2
        return (a2r * a1r - a2i * a1i, a2r * a1i + a2i * a1r,
                a2r * b1r - a2i * b1i + b2r, a2r * b1i + a2i * b1r + b2i)

    Ar, Ai, Hr, Hi = lax.associative_scan(combine, (ar, ai, bu_re, bu_im), axis=1)
    h_re = Ar * h0_re[:, None] - Ai * h0_im[:, None] + Hr
    h_im = Ar * h0_im[:, None] + Ai * h0_re[:, None] + Hi
    y = (jnp.einsum('bsgn,gpn->bsgp', h_re, c_re.astype(f32))
         - jnp.einsum('bsgn,gpn->bsgp', h_im, c_im.astype(f32))
         + d.astype(f32) * u)
    return y, h_re[:, -1], h_im[:, -1]


def causal_conv(x, buf, w):
    S = x.shape[1]
    xp = jnp.concatenate([buf, x], axis=1)
    y = sum(w[j] * xp[:, j:j + S] for j in range(CONV_K))
    return y, xp[:, -(CONV_K - 1):]


def gdn_chunked(q, k, v, beta, log_alpha, S0):
    S = q.shape[1]
    L = math.gcd(S, CHUNK)
    tril = jnp.tril(jnp.ones((L, L), dtype=bool))
    strict = jnp.tril(jnp.ones((L, L), dtype=bool), -1)
    eye = jnp.eye(L, dtype=jnp.float32)
    xs = (to_chunks(q, L), to_chunks(k, L), to_chunks(v, L), to_chunks(beta, L), to_chunks(log_alpha, L))

    def step(st, inp):
        qc, kc, vc, bc, lac = inp
        g = jnp.cumsum(lac, axis=-1)
        diff = g[..., :, None] - g[..., None, :]
        dec_strict = jnp.where(strict, jnp.exp(jnp.where(strict, diff, 0.0)), 0.0)
        dec_incl = jnp.where(tril, jnp.exp(jnp.where(tril, diff, 0.0)), 0.0)
        eg = jnp.exp(g)
        A = dec_strict * jnp.einsum('bhtd,bhjd->bhtj', kc, kc) * bc[..., None, :]
        rhs = vc - eg[..., None] * jnp.einsum('bhtd,bhde->bhte', kc, st)
        U = lax.linalg.triangular_solve(eye + A, rhs, left_side=True, lower=True, unit_diagonal=True)
        qk = jnp.einsum('bhtd,bhjd->bhtj', qc, kc) * dec_incl * bc[..., None, :]
        o = eg[..., None] * jnp.einsum('bhtd,bhde->bhte', qc, st) + jnp.einsum('bhtj,bhje->bhte', qk, U)
        g_last = g[..., -1]
        w_j = jnp.exp(g_last[..., None] - g) * bc
        st_new = jnp.exp(g_last)[..., None, None] * st + jnp.einsum('bhj,bhjd,bhje->bhde', w_j, kc, U)
        return st_new, o

    S1, o = lax.scan(step, S0, xs)
    return from_chunks(o), S1


def zero_state(B):
    f = jnp.float32
    return (jnp.zeros((B, A_HEADS, A_DH, A_DH), f), jnp.zeros((B, A_HEADS, A_DH), f), jnp.zeros((B, A_HEADS), f),
            jnp.zeros((B, S5_GROUPS, S5_N), f), jnp.zeros((B, S5_GROUPS, S5_N), f),
            jnp.zeros((B, C_HEADS, C_DH, C_DH), f), jnp.zeros((B, CONV_K - 1, 3 * C_WIDTH), f))


def mixer_block(x, state, lp):
    f32 = jnp.float32
    c0, n0, m0, s5r0, s5i0, S0, buf0 = [s.astype(f32) for s in state]
    B, S, _ = x.shape
    z = (rmsnorm(x, lp['norm_mix']) @ lp['w_in']).astype(f32)
    a_q, a_k, a_v, a_o, a_i, a_f, s_u, c_qkv, c_g, c_b, c_a = jnp.split(z, IN_SPLITS, axis=-1)
    q = a_q.reshape(B, S, A_HEADS, A_DH) * A_DH ** -0.5
    k = a_k.reshape(B, S, A_HEADS, A_DH)
    v = a_v.reshape(B, S, A_HEADS, A_DH)
    ha, c1, n1, m1 = mlstm_chunked(q, k, v, a_i + lp['mlstm_b_i'], a_f + lp['mlstm_b_f'], c0, n0, m0)
    ha = rmsnorm(ha, lp['mlstm_norm'].reshape(A_HEADS, A_DH)).reshape(B, S, A_WIDTH) * jax.nn.sigmoid(a_o)
    ys, s5r1, s5i1 = s5_scan(s_u.reshape(B, S, S5_GROUPS, S5_P), s5r0, s5i0, lp['s5_a_re'], lp['s5_a_im'],
                             lp['s5_log_dt'], lp['s5_b_re'], lp['s5_b_im'], lp['s5_c_re'], lp['s5_c_im'], lp['s5_d'])
    gy = jax.nn.gelu(ys.reshape(B, S, S5_WIDTH))
    ys = gy * jax.nn.sigmoid(gy @ lp['s5_w_glu'].astype(f32))
    qkv, buf1 = causal_conv(c_qkv, buf0, lp['gdn_conv_w'].astype(f32))
    gq, gk, gv = jnp.split(jax.nn.silu(qkv), 3, axis=-1)
    gq = l2norm(gq.reshape(B, S, C_HEADS, C_DH)) * C_DH ** -0.5
    gk = l2norm(gk.reshape(B, S, C_HEADS, C_DH))
    gv = gv.reshape(B, S, C_HEADS, C_DH)
    beta = jax.nn.sigmoid(c_b)
    log_alpha = -jnp.exp(lp['gdn_a_log'].astype(f32)) * jax.nn.softplus(c_a + lp['gdn_dt_bias'])
    hc, S1 = gdn_chunked(gq, gk, gv, beta, log_alpha, S0)
    hc = (rmsnorm(hc, lp['gdn_norm']) * jax.nn.silu(c_g.reshape(B, S, C_HEADS, C_DH))).reshape(B, S, C_WIDTH)
    mix = jnp.concatenate([ha, ys, hc], axis=-1).astype(x.dtype)
    return x + mix @ lp['w_out'], (c1, n1, m1, s5r1, s5i1, S1, buf1)


def mem_kv(mem, lp):
    B, M, _ = mem.shape
    mn = rmsnorm(mem, lp['norm_mem'])
    return ((mn @ lp['w_mk']).reshape(B, M, X_HEADS, X_DH), (mn @ lp['w_mv']).reshape(B, M, X_HEADS, X_DH))


def cross_attn(x, mk, mv, lp):
    B, S, _ = x.shape
    q = (rmsnorm(x, lp['norm_xattn']) @ lp['w_mq']).reshape(B, S, X_HEADS, X_DH)
    s = jnp.einsum('bshd,bmhd->bhsm', q, mk).astype(jnp.float32) * X_DH ** -0.5
    p = jax.nn.softmax(s, axis=-1).astype(mv.dtype)
    o = jnp.einsum('bhsm,bmhd->bshd', p, mv).reshape(B, S, D_MODEL)
    return x + o @ lp['w_mo']


def ffn(x, lp):
    h = rmsnorm(x, lp['norm_ffn'])
    return x + (jax.nn.silu(h @ lp['w_gate']) * (h @ lp['w_up'])) @ lp['w_down']


def setup_inputs(seed: int = 0) -> dict:
    key = jax.random.key(seed)
    ks = jax.random.split(key, 64)
    it = iter(range(64))
    f32 = jnp.float32

    def nrm(shape, scale=1.0):
        return scale * jax.random.normal(ks[next(it)], shape, f32)

    def unif(shape, lo, hi):
        return jax.random.uniform(ks[next(it)], shape, f32, lo, hi)

    L = DEPTH
    inp = {}
    inp['x_prompt'] = nrm((BATCH, SEQ, D_MODEL))
    inp['x_sample'] = nrm((DEC_BATCH, DEC_SEQ, D_MODEL))
    inp['mem_prompt'] = nrm((BATCH, MEM_TOKENS, D_MODEL))
    inp['cache_mem_k'] = nrm((L, DEC_BATCH, MEM_TOKENS, X_HEADS, X_DH))
    inp['cache_mem_v'] = nrm((L, DEC_BATCH, MEM_TOKENS, X_HEADS, X_DH))
    inp['state_mlstm_c'] = nrm((L, DEC_BATCH, A_HEADS, A_DH, A_DH), 0.5)
    inp['state_mlstm_n'] = nrm((L, DEC_BATCH, A_HEADS, A_DH))
    inp['state_mlstm_m'] = nrm((L, DEC_BATCH, A_HEADS))
    inp['state_s5_re'] = nrm((L, DEC_BATCH, S5_GROUPS, S5_N), 0.5)
    inp['state_s5_im'] = nrm((L, DEC_BATCH, S5_GROUPS, S5_N), 0.5)
    inp['state_gdn'] = nrm((L, DEC_BATCH, C_HEADS, C_DH, C_DH), 0.3)
    inp['state_gdn_conv'] = nrm((L, DEC_BATCH, CONV_K - 1, 3 * C_WIDTH))
    inp['norm_mix'] = 1.0 + nrm((L, D_MODEL), 0.02)
    inp['w_in'] = nrm((L, D_MODEL, IN_WIDTH), D_MODEL ** -0.5)
    inp['w_out'] = nrm((L, MIX_WIDTH, D_MODEL), MIX_WIDTH ** -0.5)
    inp['mlstm_b_i'] = nrm((L, A_HEADS), 0.1)
    inp['mlstm_b_f'] = jnp.linspace(3.0, 6.0, A_HEADS, dtype=f32)[None, :] + nrm((L, A_HEADS), 0.1)
    inp['mlstm_norm'] = 1.0 + nrm((L, A_WIDTH), 0.02)
    n_idx = jnp.arange(S5_N, dtype=f32)
    inp['s5_a_re'] = -0.5 + nrm((L, S5_GROUPS, S5_N), 0.01)
    inp['s5_a_im'] = math.pi * n_idx + nrm((L, S5_GROUPS, S5_N), 0.01)
    inp['s5_log_dt'] = unif((L, S5_GROUPS), math.log(0.001), math.log(0.1))
    inp['s5_b_re'] = nrm((L, S5_GROUPS, S5_N, S5_P), S5_P ** -0.5)
    inp['s5_b_im'] = nrm((L, S5_GROUPS, S5_N, S5_P), S5_P ** -0.5)
    inp['s5_c_re'] = nrm((L, S5_GROUPS, S5_P, S5_N), S5_N ** -0.5)
    inp['s5_c_im'] = nrm((L, S5_GROUPS, S5_P, S5_N), S5_N ** -0.5)
    inp['s5_d'] = nrm((L, S5_GROUPS, S5_P))
    inp['s5_w_glu'] = nrm((L, S5_WIDTH, S5_WIDTH), S5_WIDTH ** -0.5)
    inp['gdn_conv_w'] = nrm((L, CONV_K, 3 * C_WIDTH), CONV_K ** -0.5)
    inp['gdn_a_log'] = jnp.log(unif((L, C_HEADS), 1.0, 16.0))
    dt = jnp.exp(unif((L, C_HEADS), math.log(0.001), math.log(0.1)))
    inp['gdn_dt_bias'] = dt + jnp.log(-jnp.expm1(-dt))
    inp['gdn_norm'] = 1.0 + nrm((L, C_DH), 0.02)
    inp['norm_xattn'] = 1.0 + nrm((L, D_MODEL), 0.02)
    inp['norm_mem'] = 1.0 + nrm((L, D_MODEL), 0.02)
    inp['w_mq'] = nrm((L, D_MODEL, D_MODEL), D_MODEL ** -0.5)
    inp['w_mk'] = nrm((L, D_MODEL, D_MODEL), D_MODEL ** -0.5)
    inp['w_mv'] = nrm((L, D_MODEL, D_MODEL), D_MODEL ** -0.5)
    inp['w_mo'] = nrm((L, D_MODEL, D_MODEL), D_MODEL ** -0.5)
    inp['norm_ffn'] = 1.0 + nrm((L, D_MODEL), 0.02)
    inp['w_gate'] = nrm((L, D_MODEL, D_FF), D_MODEL ** -0.5)
    inp['w_up'] = nrm((L, D_MODEL, D_FF), D_MODEL ** -0.5)
    inp['w_down'] = nrm((L, D_FF, D_MODEL), D_FF ** -0.5)
    inp['norm_final'] = 1.0 + nrm((D_MODEL,), 0.02)
    return inp


def reference(x_prompt, x_sample, mem_prompt, cache_mem_k, cache_mem_v, state_mlstm_c, state_mlstm_n,
              state_mlstm_m, state_s5_re, state_s5_im, state_gdn, state_gdn_conv,
              norm_mix, w_in, w_out, mlstm_b_i, mlstm_b_f, mlstm_norm,
              s5_a_re, s5_a_im, s5_log_dt, s5_b_re, s5_b_im, s5_c_re, s5_c_im, s5_d, s5_w_glu,
              gdn_conv_w, gdn_a_log, gdn_dt_bias, gdn_norm,
              norm_xattn, norm_mem, w_mq, w_mk, w_mv, w_mo,
              norm_ffn, w_gate, w_up, w_down, norm_final):
    xp, xs = x_prompt, x_sample
    zero_p = zero_state(x_prompt.shape[0])
    mem_k_list, mem_v_list, st_p_list, st_s_list = [], [], [], []
    for l in range(DEPTH):
        lp = {'norm_mix': norm_mix[l], 'w_in': w_in[l], 'w_out': w_out[l],
              'mlstm_b_i': mlstm_b_i[l], 'mlstm_b_f': mlstm_b_f[l], 'mlstm_norm': mlstm_norm[l],
              's5_a_re': s5_a_re[l], 's5_a_im': s5_a_im[l], 's5_log_dt': s5_log_dt[l],
              's5_b_re': s5_b_re[l], 's5_b_im': s5_b_im[l], 's5_c_re': s5_c_re[l], 's5_c_im': s5_c_im[l],
              's5_d': s5_d[l], 's5_w_glu': s5_w_glu[l],
              'gdn_conv_w': gdn_conv_w[l], 'gdn_a_log': gdn_a_log[l], 'gdn_dt_bias': gdn_dt_bias[l],
              'gdn_norm': gdn_norm[l],
              'norm_xattn': norm_xattn[l], 'norm_mem': norm_mem[l], 'w_mq': w_mq[l], 'w_mk': w_mk[l],
              'w_mv': w_mv[l], 'w_mo': w_mo[l],
              'norm_ffn': norm_ffn[l], 'w_gate': w_gate[l], 'w_up': w_up[l], 'w_down': w_down[l]}
        xp, st_p = mixer_block(xp, zero_p, lp)
        mk_p, mv_p = mem_kv(mem_prompt, lp)
        xp = cross_attn(xp, mk_p, mv_p, lp)
        xp = ffn(xp, lp)
        st_in = (state_mlstm_c[l], state_mlstm_n[l], state_mlstm_m[l], state_s5_re[l], state_s5_im[l],
                 state_gdn[l], state_gdn_conv[l])
        xs, st_s = mixer_block(xs, st_in, lp)
        xs = cross_attn(xs, cache_mem_k[l], cache_mem_v[l], lp)
        xs = ffn(xs, lp)
        mem_k_list.append(mk_p)
        mem_v_list.append(mv_p)
        st_p_list.append(st_p)
        st_s_list.append(st_s)
    y_prompt = rmsnorm(xp, norm_final)
    y_sample = rmsnorm(xs, norm_final)
    mem_k_prompt = jnp.stack(mem_k_list)
    mem_v_prompt = jnp.stack(mem_v_list)
    mlstm_c_p, mlstm_n_p, mlstm_m_p, s5_re_p, s5_im_p, gdn_p, gdn_conv_p = [
        jnp.stack([st[i] for st in st_p_list]) for i in range(7)]
    mlstm_c_s, mlstm_n_s, mlstm_m_s, s5_re_s, s5_im_s, gdn_s, gdn_conv_s = [
        jnp.stack([st[i] for st in st_s_list]) for i in range(7)]
    return (y_prompt, y_sample, mem_k_prompt, mem_v_prompt,
            mlstm_c_p, mlstm_n_p, mlstm_m_p, s5_re_p, s5_im_p, gdn_p, gdn_conv_p,
            mlstm_c_s, mlstm_n_s, mlstm_m_s, s5_re_s, s5_im_s, gdn_s, gdn_conv_s)
```

```python
import functools
import math

import jax
import jax.numpy as jnp
from jax import lax
from jax.experimental import pallas as pl
from jax.experimental.pallas import tpu as pltpu

F32 = jnp.float32
BF16 = jnp.bfloat16
HI = lax.Precision.HIGHEST
EPS = 1e-6

HEADS = 4
DH = 64
HW = HEADS * DH
CHUNK = 64
S5_P = 16
S5_N = 64
S5_GROUPS = 32
S5_WIDTH = S5_GROUPS * S5_P
S5_STATE = S5_GROUPS * S5_N
S5_GB = 8
S5_NBLK = S5_GROUPS // S5_GB
CONV_K = 4
X_HEADS = 4
GATE_PAD = 128
LANES = 128
SUBLANES = 8
VMEM_LIMIT = 48 * 1024 * 1024


def _cparams(*sem):
    return pltpu.CompilerParams(dimension_semantics=sem, vmem_limit_bytes=VMEM_LIMIT)


def _rms(x, g_row):
    return x * lax.rsqrt(jnp.mean(x * x, axis=-1, keepdims=True) + EPS) * g_row


def _dot(a, b):
    return jnp.dot(a, b, preferred_element_type=F32)


def _dot_hi(a, b):
    return jnp.dot(a, b, preferred_element_type=F32, precision=HI)


def _dot_nt(a, b):
    return lax.dot_general(a, b, (((1,), (1,)), ((), ())), preferred_element_type=F32)


def _sigmoid(x):
    return 1.0 / (1.0 + jnp.exp(-x))


def _silu(x):
    return x * _sigmoid(x)


def _softplus(x):
    return jnp.maximum(x, 0.0) + jnp.log1p(jnp.exp(-jnp.abs(x)))


def _log_sigmoid(x):
    return jnp.minimum(x, 0.0) - jnp.log1p(jnp.exp(-jnp.abs(x)))


def _norm_matmul_kernel(x_ref, g_ref, w_ref, *out_refs, splits):
    xn = _rms(x_ref[...], g_ref[...]).astype(BF16)
    off = 0
    for o_ref, n in zip(out_refs, splits):
        o_ref[...] = _dot(xn, w_ref[:, off:off + n])
        off += n


def norm_matmul(x, g, w_bf16, splits, tm):
    T, D = x.shape
    N = w_bf16.shape[1]
    assert sum(splits) == N and T % tm == 0
    return pl.pallas_call(
        functools.partial(_norm_matmul_kernel, splits=tuple(splits)),
        grid=(T // tm,),
        in_specs=[pl.BlockSpec((tm, D), lambda i: (i, 0)),
                  pl.BlockSpec((1, D), lambda i: (0, 0)),
                  pl.BlockSpec((D, N), lambda i: (0, 0))],
        out_specs=[pl.BlockSpec((tm, n), lambda i: (i, 0)) for n in splits],
        out_shape=[jax.ShapeDtypeStruct((T, n), F32) for n in splits],
        compiler_params=_cparams("parallel"),
    )(x, g.reshape(1, D), w_bf16)


def _matmul_residual_kernel(x_ref, *refs, ksplits):
    a_refs, w_ref, o_ref = refs[:-2], refs[-2], refs[-1]
    acc = x_ref[...]
    off = 0
    for a_ref, k in zip(a_refs, ksplits):
        acc = acc + _dot(a_ref[...].astype(BF16), w_ref[off:off + k, :])
        off += k
    o_ref[...] = acc


def matmul_residual(x, acts, w_bf16, tm):
    T, D = x.shape
    ks = tuple(a.shape[1] for a in acts)
    K = w_bf16.shape[0]
    assert sum(ks) == K and T % tm == 0
    return pl.pallas_call(
        functools.partial(_matmul_residual_kernel, ksplits=ks),
        grid=(T // tm,),
        in_specs=[pl.BlockSpec((tm, D), lambda i: (i, 0))]
                 + [pl.BlockSpec((tm, k), lambda i: (i, 0)) for k in ks]
                 + [pl.BlockSpec((K, D), lambda i: (0, 0))],
        out_specs=pl.BlockSpec((tm, D), lambda i: (i, 0)),
        out_shape=jax.ShapeDtypeStruct((T, D), F32),
        compiler_params=_cparams("parallel"),
    )(x, *acts, w_bf16)


def _ffn_kernel(x_ref, g_ref, wg_ref, wu_ref, wd_ref, gf_ref, o_ref, hn_scr, acc_scr, *, final_norm):
    j = pl.program_id(1)

    @pl.when(j == 0)
    def _():
        x = x_ref[...]
        hn_scr[...] = _rms(x, g_ref[...]).astype(BF16)
        acc_scr[...] = x

    h = hn_scr[...]
    a = _dot(h, wg_ref[...])
    b = _dot(h, wu_ref[...])
    acc_scr[...] += _dot((_silu(a) * b).astype(BF16), wd_ref[...])

    @pl.when(j == pl.num_programs(1) - 1)
    def _():
        y = acc_scr[...]
        if final_norm:
            y = _rms(y, gf_ref[...])
        o_ref[...] = y


def ffn(x, g, wg, wu, wd, g_final, final_norm, tm, tf):
    T, D = x.shape
    F = wg.shape[1]
    assert T % tm == 0 and F % tf == 0
    return pl.pallas_call(
        functools.partial(_ffn_kernel, final_norm=final_norm),
        grid=(T // tm, F // tf),
        in_specs=[pl.BlockSpec((tm, D), lambda i, j: (i, 0)),
                  pl.BlockSpec((1, D), lambda i, j: (0, 0)),
                  pl.BlockSpec((D, tf), lambda i, j: (0, j)),
                  pl.BlockSpec((D, tf), lambda i, j: (0, j)),
                  pl.BlockSpec((tf, D), lambda i, j: (j, 0)),
                  pl.BlockSpec((1, D), lambda i, j: (0, 0))],
        out_specs=pl.BlockSpec((tm, D), lambda i, j: (i, 0)),
        out_shape=jax.ShapeDtypeStruct((T, D), F32),
        scratch_shapes=[pltpu.VMEM((tm, D), BF16), pltpu.VMEM((tm, D), F32)],
        compiler_params=_cparams("parallel", "arbitrary"),
    )(x, g.reshape(1, D), wg, wu, wd, g_final.reshape(1, D))


def _softmax_rows(s):
    e = jnp.exp(s - jnp.max(s, axis=-1, keepdims=True))
    return e / jnp.sum(e, axis=-1, keepdims=True)


def _xattn_prompt_kernel(q_ref, k_ref, v_ref, o_ref, *, dh):
    scale = dh ** -0.5
    for h in range(X_HEADS):
        sl = slice(h * dh, (h + 1) * dh)
        s = _dot_nt(q_ref[:, sl].astype(BF16), k_ref[0, :, sl].astype(BF16)) * scale
        o_ref[:, sl] = _dot(_softmax_rows(s).astype(BF16), v_ref[0, :, sl].astype(BF16))


def xattn_prompt(q, mk, mv, seq, tq):
    T, D = q.shape
    B, M, _ = mk.shape
    nt = seq // tq
    return pl.pallas_call(
        functools.partial(_xattn_prompt_kernel, dh=D // X_HEADS),
        grid=(B, nt),
        in_specs=[pl.BlockSpec((tq, D), lambda b, t: (b * nt + t, 0)),
                  pl.BlockSpec((1, M, D), lambda b, t: (b, 0, 0)),
                  pl.BlockSpec((1, M, D), lambda b, t: (b, 0, 0))],
        out_specs=pl.BlockSpec((tq, D), lambda b, t: (b * nt + t, 0)),
        out_shape=jax.ShapeDtypeStruct((T, D), F32),
        compiler_params=_cparams("parallel", "parallel"),
    )(q, mk, mv)


def _xattn_sample_kernel(q_ref, k_ref, v_ref, o_ref, *, dh, sb):
    D = q_ref.shape[-1]
    scale = dh ** -0.5
    row = lax.broadcasted_iota(jnp.int32, (SUBLANES, D), 0)
    head_of_lane = lax.broadcasted_iota(jnp.int32, (SUBLANES, D), 1) // dh
    head_mask = row == head_of_lane
    for i in range(sb):
        qm = jnp.where(head_mask, jnp.broadcast_to(q_ref[i], (SUBLANES, D)), 0.0).astype(BF16)
        s = _dot_nt(qm, k_ref[i].astype(BF16)) * scale
        o_all = _dot(_softmax_rows(s).astype(BF16), v_ref[i].astype(BF16))
        o_ref[i] = jnp.sum(jnp.where(head_mask, o_all, 0.0), axis=0, keepdims=True)


def xattn_sample(q, ck, cv, sb):
    B, D = q.shape
    M = ck.shape[1]
    out = pl.pallas_call(
        functools.partial(_xattn_sample_kernel, dh=D // X_HEADS, sb=sb),
        grid=(B // sb,),
        in_specs=[pl.BlockSpec((sb, 1, D), lambda i: (i, 0, 0)),
                  pl.BlockSpec((sb, M, D), lambda i: (i, 0, 0)),
                  pl.BlockSpec((sb, M, D), lambda i: (i, 0, 0))],
        out_specs=pl.BlockSpec((sb, 1, D), lambda i: (i, 0, 0)),
        out_shape=jax.ShapeDtypeStruct((B, 1, D), F32),
        compiler_params=_cparams("parallel"),
    )(q.reshape(B, 1, D), ck, cv)
    return out.reshape(B, D)


def _lane_cat_masks(L):
    row = lax.broadcasted_iota(jnp.int32, (L, HW), 0)
    j = lax.broadcasted_iota(jnp.int32, (L, HW), 1) % DH
    r2 = lax.broadcasted_iota(jnp.int32, (HW, HW), 0) // DH
    c2 = lax.broadcasted_iota(jnp.int32, (HW, HW), 1) // DH
    return row >= j, row > j, row == j, r2 == c2


def _expand_bd(x, bd):
    return jnp.where(bd, jnp.concatenate([x] * HEADS, axis=0), jnp.zeros((), x.dtype))


def _seg_reduce(x, op, fill):
    lo = lax.broadcasted_iota(jnp.int32, (x.shape[0], LANES), 1) < DH
    parts = []
    for c in range(HW // LANES):
        xh = x[:, c * LANES:(c + 1) * LANES]
        a = op(jnp.where(lo, xh, fill), axis=-1, keepdims=True)
        b = op(jnp.where(lo, fill, xh), axis=-1, keepdims=True)
        parts.append(jnp.where(lo, a, b))
    return jnp.concatenate(parts, axis=-1)


def _head_expander(first_lane):
    r = lax.broadcasted_iota(jnp.int32, (GATE_PAD, HW), 0)
    c = lax.broadcasted_iota(jnp.int32, (GATE_PAD, HW), 1) // DH
    return (r == c + first_lane).astype(F32)


def _col_to_row(x, eye):
    return jnp.sum(jnp.where(eye, x, 0.0), axis=0, keepdims=True)


def _head_rms(x, ones_bd, g_row):
    ms = _dot_hi(x * x, ones_bd) * (1.0 / DH)
    return x * lax.rsqrt(ms + EPS) * g_row


def _mlstm_prompt_kernel(za_ref, gt_ref, bias_ref, norm_ref, ha_ref, c_ref, n_ref, m_ref,
                         c_scr, n_scr, m_scr, ix_scr, fx_scr, *, nchunks):
    tb = pl.program_id(1)
    L = CHUNK
    tril, _, eye, bd = _lane_cat_masks(L)
    ones_bd = bd.astype(F32)
    tril_f = (lax.broadcasted_iota(jnp.int32, (L, L), 0) >= lax.broadcasted_iota(jnp.int32, (L, L), 1)).astype(F32)

    @pl.when(tb == 0)
    def _():
        c_scr[...] = jnp.zeros_like(c_scr)
        n_scr[...] = jnp.zeros_like(n_scr)
        m_scr[...] = jnp.zeros_like(m_scr)

    gt = gt_ref[...] + bias_ref[...]
    ix_scr[...] = _dot_hi(gt, _head_expander(0))
    fx_scr[...] = _dot_hi(_log_sigmoid(gt), _head_expander(HEADS))

    def chunk(ci, carry):
        rows = pl.ds(pl.multiple_of(ci * L, L), L)
        q = (za_ref[rows, 0:HW] * (DH ** -0.5)).astype(BF16)
        k = za_ref[rows, HW:2 * HW]
        v = za_ref[rows, 2 * HW:3 * HW].astype(BF16)
        og = za_ref[rows, 3 * HW:4 * HW]
        i_x = ix_scr[rows, :]
        b_x = _dot_hi(tril_f, fx_scr[rows, :])
        c_bd = c_scr[...]
        n_row = n_scr[...]
        m_x = m_scr[...]

        d_intra = jnp.where(tril, b_x - _col_to_row(b_x, eye) + _col_to_row(i_x, eye), -jnp.inf)
        d_inter = b_x + m_x
        m_t = jnp.maximum(_seg_reduce(d_intra, jnp.max, -jnp.inf), d_inter)
        w_intra = jnp.exp(d_intra - m_t)
        w_inter = jnp.exp(d_inter - m_t)
        kb = k.astype(BF16)
        s = _dot_nt(q, _expand_bd(kb, bd)) * w_intra
        num = w_inter * _dot(q, c_bd.astype(BF16)) + _dot(s.astype(BF16), _expand_bd(v, bd))
        qn = _dot((q.astype(F32) * n_row).astype(BF16), ones_bd.astype(BF16))
        den = w_inter * qn + _seg_reduce(s, jnp.sum, 0.0)
        hh = num / jnp.maximum(jnp.abs(den), jnp.exp(-m_t))
        ha_ref[rows, :] = _head_rms(hh, ones_bd, norm_ref[...]) * _sigmoid(og)

        b_last = b_x[L - 1:L, :]
        g_x = b_last - b_x + i_x
        m_new = jnp.maximum(b_last + m_x, jnp.max(g_x, axis=0, keepdims=True))
        kw = k * jnp.exp(g_x - m_new)
        decay = jnp.exp(b_last + m_x - m_new)
        c_scr[...] = decay * c_bd + jnp.where(bd, _dot(kw.T.astype(BF16), v), 0.0)
        n_scr[...] = decay * n_row + jnp.sum(kw, axis=0, keepdims=True)
        m_scr[...] = m_new
        return carry

    lax.fori_loop(0, nchunks, chunk, 0)

    @pl.when(tb == pl.num_programs(1) - 1)
    def _():
        for h in range(HEADS):
            c_ref[0, h] = c_scr[h * DH:(h + 1) * DH, h * DH:(h + 1) * DH]
        n_ref[0] = n_scr[...]
        m_ref[0] = m_scr[...]


def mlstm_prompt(za, gates, b_i, b_f, norm, B, S, tt):
    assert S % tt == 0 and tt % CHUNK == 0
    nt = S // tt
    bias = jnp.zeros((1, GATE_PAD), F32).at[0, 0:HEADS].set(b_i).at[0, HEADS:2 * HEADS].set(b_f)
    ha, c, n, m = pl.pallas_call(
        functools.partial(_mlstm_prompt_kernel, nchunks=tt // CHUNK),
        grid=(B, nt),
        in_specs=[pl.BlockSpec((tt, 4 * HW), lambda b, t: (b * nt + t, 0)),
                  pl.BlockSpec((tt, GATE_PAD), lambda b, t: (b * nt + t, 0)),
                  pl.BlockSpec((1, GATE_PAD), lambda b, t: (0, 0)),
                  pl.BlockSpec((1, HW), lambda b, t: (0, 0))],
        out_specs=[pl.BlockSpec((tt, HW), lambda b, t: (b * nt + t, 0)),
                   pl.BlockSpec((1, HEADS, DH, DH), lambda b, t: (b, 0, 0, 0)),
                   pl.BlockSpec((1, 1, HW), lambda b, t: (b, 0, 0)),
                   pl.BlockSpec((1, 1, HW), lambda b, t: (b, 0, 0))],
        out_shape=[jax.ShapeDtypeStruct((B * S, HW), F32),
                   jax.ShapeDtypeStruct((B, HEADS, DH, DH), F32),
                   jax.ShapeDtypeStruct((B, 1, HW), F32),
                   jax.ShapeDtypeStruct((B, 1, HW), F32)],
        scratch_shapes=[pltpu.VMEM((HW, HW), F32), pltpu.VMEM((1, HW), F32), pltpu.VMEM((1, HW), F32),
                        pltpu.VMEM((tt, HW), F32), pltpu.VMEM((tt, HW), F32)],
        compiler_params=_cparams("parallel", "arbitrary"),
    )(za, gates, bias, norm.reshape(1, HW))
    return ha, c, n.reshape(B, HEADS, DH), m[:, 0, ::DH]


def _gdn_prompt_kernel(zc_ref, zg_ref, gt_ref, cw_ref, alog_ref, dtb_ref, norm_ref,
                       hc_ref, s_ref, tail_ref,
                       s_scr, xp_scr, q_scr, k_scr, v_scr, beta_scr, la_scr, *, nchunks):
    tb = pl.program_id(1)
    L = CHUNK
    tt = nchunks * L
    pad = SUBLANES
    tril, strict, eye, bd = _lane_cat_masks(L)
    ones_bd = bd.astype(F32)
    tril_f = (lax.broadcasted_iota(jnp.int32, (L, L), 0) >= lax.broadcasted_iota(jnp.int32, (L, L), 1)).astype(F32)

    @pl.when(tb == 0)
    def _():
        s_scr[...] = jnp.zeros_like(s_scr)
        xp_scr[0:pad, :] = jnp.zeros((pad, 3 * HW), F32)

    x = zc_ref[...]
    xp_scr[pad:pad + tt, :] = x
    y = cw_ref[CONV_K - 1:CONV_K, :] * x
    for j in range(CONV_K - 1):
        y = y + cw_ref[j:j + 1, :] * xp_scr[pl.ds(pad - (CONV_K - 1) + j, tt), :]
    xp_scr[0:pad, :] = x[tt - pad:tt, :]
    y = _silu(y)
    q_raw, k_raw = y[:, 0:HW], y[:, HW:2 * HW]
    q_scr[...] = q_raw * lax.rsqrt(_dot_hi(q_raw * q_raw, ones_bd) + EPS) * (DH ** -0.5)
    k_scr[...] = k_raw * lax.rsqrt(_dot_hi(k_raw * k_raw, ones_bd) + EPS)
    v_scr[...] = y[:, 2 * HW:3 * HW]
    gt = gt_ref[...]
    beta_scr[...] = _dot_hi(_sigmoid(gt), _head_expander(2 * HEADS))
    la_scr[...] = _dot_hi(-jnp.exp(alog_ref[...]) * _softplus(gt + dtb_ref[...]), _head_expander(3 * HEADS))

    def chunk(ci, carry):
        rows = pl.ds(pl.multiple_of(ci * L, L), L)
        q = q_scr[rows, :].astype(BF16)
        k = k_scr[rows, :]
        kb = k.astype(BF16)
        v = v_scr[rows, :]
        beta_x = beta_scr[rows, :]
        g_x = _dot_hi(tril_f, la_scr[rows, :])
        diff = g_x - _col_to_row(g_x, eye)
        dec_incl = jnp.where(tril, jnp.exp(jnp.where(tril, diff, 0.0)), 0.0)
        dec_strict = jnp.where(strict, dec_incl, 0.0)
        eg = jnp.exp(g_x)
        beta_row = _col_to_row(beta_x, eye)
        k_bd = _expand_bd(kb, bd)
        n0 = -(dec_strict * _dot_nt(kb, k_bd) * beta_row)
        qk = _dot_nt(q, k_bd) * dec_incl * beta_row

        p = _dot(n0.astype(BF16), _expand_bd(n0.astype(BF16), bd))
        m = n0
        steps = int(math.log2(L)) - 1
        for i in range(steps):
            p_bd = _expand_bd(p.astype(BF16), bd)
            if i < steps - 1:
                pm = _dot(jnp.concatenate([p, m], axis=0).astype(BF16), p_bd)
                p_next, mp = pm[0:L], pm[L:2 * L]
            else:
                p_next, mp = None, _dot(m.astype(BF16), p_bd)
            m = m + p + mp
            p = p_next

        s_bd = s_scr[...]
        kqs = _dot(jnp.concatenate([kb, q], axis=0), s_bd.astype(BF16))
        rhs = v - eg * kqs[0:L]
        u = rhs + _dot(m.astype(BF16), _expand_bd(rhs.astype(BF16), bd))
        ub = u.astype(BF16)
        o = eg * kqs[L:2 * L] + _dot(qk.astype(BF16), _expand_bd(ub, bd))
        hc_ref[rows, :] = _head_rms(o, ones_bd, norm_ref[...]) * _silu(zg_ref[rows, :])

        g_last = g_x[L - 1:L, :]
        kw = k * (jnp.exp(g_last - g_x) * beta_x)
        s_scr[...] = jnp.exp(g_last) * s_bd + jnp.where(bd, _dot(kw.T.astype(BF16), ub), 0.0)
        return carry

    lax.fori_loop(0, nchunks, chunk, 0)

    @pl.when(tb == pl.num_programs(1) - 1)
    def _():
        for h in range(HEADS):
            s_ref[0, h] = s_scr[h * DH:(h + 1) * DH, h * DH:(h + 1) * DH]
        tail_ref[0] = xp_scr[0:pad, :]


def _gdn_gate_rows(a_log, dt_bias):
    z = jnp.zeros((1, GATE_PAD), F32)
    return (z.at[0, 3 * HEADS:4 * HEADS].set(a_log), z.at[0, 3 * HEADS:4 * HEADS].set(dt_bias))


def gdn_prompt(zc, zg, gates, conv_w, a_log, dt_bias, norm, B, S, tt):
    assert S % tt == 0 and tt % CHUNK == 0
    nt = S // tt
    alog, dtb = _gdn_gate_rows(a_log, dt_bias)
    hc, s, tail = pl.pallas_call(
        functools.partial(_gdn_prompt_kernel, nchunks=tt // CHUNK),
        grid=(B, nt),
        in_specs=[pl.BlockSpec((tt, 3 * HW), lambda b, t: (b * nt + t, 0)),
                  pl.BlockSpec((tt, HW), lambda b, t: (b * nt + t, 0)),
                  pl.BlockSpec((tt, GATE_PAD), lambda b, t: (b * nt + t, 0)),
                  pl.BlockSpec((CONV_K, 3 * HW), lambda b, t: (0, 0)),
                  pl.BlockSpec((1, GATE_PAD), lambda b, t: (0, 0)),
                  pl.BlockSpec((1, GATE_PAD), lambda b, t: (0, 0)),
                  pl.BlockSpec((1, HW), lambda b, t: (0, 0))],
        out_specs=[pl.BlockSpec((tt, HW), lambda b, t: (b * nt + t, 0)),
                   pl.BlockSpec((1, HEADS, DH, DH), lambda b, t: (b, 0, 0, 0)),
                   pl.BlockSpec((1, SUBLANES, 3 * HW), lambda b, t: (b, 0, 0))],
        out_shape=[jax.ShapeDtypeStruct((B * S, HW), F32),
                   jax.ShapeDtypeStruct((B, HEADS, DH, DH), F32),
                   jax.ShapeDtypeStruct((B, SUBLANES, 3 * HW), F32)],
        scratch_shapes=[pltpu.VMEM((HW, HW), F32), pltpu.VMEM((tt + SUBLANES, 3 * HW), F32),
                        pltpu.VMEM((tt, HW), F32), pltpu.VMEM((tt, HW), F32), pltpu.VMEM((tt, HW), F32),
                        pltpu.VMEM((tt, HW), F32), pltpu.VMEM((tt, HW), F32)],
        compiler_params=_cparams("parallel", "arbitrary"),
    )(zc, zg, gates, conv_w, alog, dtb, jnp.tile(norm, HEADS).reshape(1, HW))
    return hc, s, tail[:, SUBLANES - (CONV_K - 1):, :]


def _s5_prep_kernel(are_ref, aim_ref, ldt_ref, bre_ref, bim_ref, lre_ref, lim_ref, bbre_ref, bbim_ref):
    a_re, a_im = are_ref[...], aim_ref[...]
    dt = jnp.exp(ldt_ref[...])
    mag = jnp.exp(a_re * dt)
    lam_re, lam_im = mag * jnp.cos(a_im * dt), mag * jnp.sin(a_im * dt)
    lre_ref[...] = lam_re
    lim_ref[...] = lam_im
    nr, ni = lam_re - 1.0, lam_im
    den = a_re * a_re + a_im * a_im
    coef_re = (nr * a_re + ni * a_im) / den
    coef_im = (ni * a_re - nr * a_im) / den
    b_re, b_im = bre_ref[...], bim_ref[...]
    bbre_ref[...] = coef_re * b_re - coef_im * b_im
    bbim_ref[...] = coef_re * b_im + coef_im * b_re


def s5_params(lp):
    G, N, P = S5_GROUPS, S5_N, S5_P
    row = lambda a: a.astype(F32).reshape(1, G * N)
    to_pn = lambda b: jnp.transpose(b.astype(F32), (2, 0, 1)).reshape(P, G * N)
    shp = [jax.ShapeDtypeStruct((1, G * N), F32)] * 2 + [jax.ShapeDtypeStruct((P, G * N), F32)] * 2
    lam_re, lam_im, bb_re, bb_im = pl.pallas_call(_s5_prep_kernel, out_shape=shp)(
        row(lp['s5_a_re']), row(lp['s5_a_im']), row(jnp.repeat(lp['s5_log_dt'][:, None], N, axis=1)),
        to_pn(lp['s5_b_re']), to_pn(lp['s5_b_im']))
    eye = jnp.eye(S5_GB, dtype=F32)

    def w_in_blocks(bb):
        b4 = bb.reshape(P, S5_NBLK, S5_GB, N)
        return jnp.einsum('pbgn,gh->bgphn', b4, eye).reshape(S5_NBLK, S5_GB * P, S5_GB * N)

    def w_out_blocks(c):
        c4 = c.astype(F32).reshape(S5_NBLK, S5_GB, P, N)
        return jnp.einsum('bgpn,gh->bgnhp', c4, eye).reshape(S5_NBLK, S5_GB * N, S5_GB * P)

    w_in = jnp.concatenate([w_in_blocks(bb_re), w_in_blocks(bb_im)], axis=-1).astype(BF16)
    return {'lam_re': lam_re, 'lam_im': lam_im, 'w_in': w_in,
            'w_out_re': w_out_blocks(lp['s5_c_re']).astype(BF16),
            'w_out_im': (-w_out_blocks(lp['s5_c_im'])).astype(BF16),
            'd': lp['s5_d'].astype(F32).reshape(1, S5_WIDTH), 'w_glu': lp['s5_w_glu'].astype(BF16)}


def _s5_kernel(u_ref, h0r_ref, h0i_ref, lamr_ref, lami_ref, win_ref, wor_ref, woi_ref, d_ref, wglu_ref,
               ys_ref, h1r_ref, h1i_ref, hr_scr, hi_scr, br_scr, bi_scr, *, tt, bb, lane_blk):
    tb = pl.program_id(0)
    rows_total = tt * bb
    blk_in, blk_st = S5_GB * S5_P, S5_GB * S5_N

    @pl.when(tb == 0)
    def _():
        hr_scr[...] = h0r_ref[...]
        hi_scr[...] = h0i_ref[...]

    u = u_ref[...].reshape(rows_total, S5_WIDTH)
    ub = u.astype(BF16)
    for blk in range(S5_NBLK):
        bu = _dot(ub[:, blk * blk_in:(blk + 1) * blk_in], win_ref[blk])
        br_scr[:, blk * blk_st:(blk + 1) * blk_st] = bu[:, 0:blk_st]
        bi_scr[:, blk * blk_st:(blk + 1) * blk_st] = bu[:, blk_st:2 * blk_st]

    for lb in range(S5_STATE // lane_blk):
        ls = slice(lb * lane_blk, (lb + 1) * lane_blk)
        lr = jnp.broadcast_to(lamr_ref[:, ls], (bb, lane_blk))
        li = jnp.broadcast_to(lami_ref[:, ls], (bb, lane_blk))

        def step(t, carry):
            hr, hi = carry
            rows = pl.ds(pl.multiple_of(t * bb, bb), bb)
            nr = lr * hr - li * hi + br_scr[rows, ls]
            ni = lr * hi + li * hr + bi_scr[rows, ls]
            br_scr[rows, ls] = nr
            bi_scr[rows, ls] = ni
            return nr, ni

        hr, hi = lax.fori_loop(0, tt, step, (hr_scr[:, ls], hi_scr[:, ls]), unroll=min(tt, 8))
        hr_scr[:, ls] = hr
        hi_scr[:, ls] = hi

    ys = []
    for blk in range(S5_NBLK):
        st = slice(blk * blk_st, (blk + 1) * blk_st)
        ys.append(_dot(br_scr[:, st].astype(BF16), wor_ref[blk]) + _dot(bi_scr[:, st].astype(BF16), woi_ref[blk]))
    gy = jax.nn.gelu(jnp.concatenate(ys, axis=-1) + d_ref[...] * u)
    out = gy * _sigmoid(_dot(gy.astype(BF16), wglu_ref[...]))
    ys_ref[...] = out.reshape(tt, bb, S5_WIDTH)

    @pl.when(tb == pl.num_programs(0) - 1)
    def _():
        h1r_ref[...] = hr_scr[...]
        h1i_ref[...] = hi_scr[...]


def s5_mixer(u3, h0_re, h0_im, sp, tt):
    S, B, _ = u3.shape
    assert S % tt == 0 and B % SUBLANES == 0
    lane_blk = max(LANES, min(S5_STATE, (SUBLANES * SUBLANES * LANES) // B))
    full = lambda shape: pl.BlockSpec(shape, lambda t: (0,) * len(shape))
    return pl.pallas_call(
        functools.partial(_s5_kernel, tt=tt, bb=B, lane_blk=lane_blk),
        grid=(S // tt,),
        in_specs=[pl.BlockSpec((tt, B, S5_WIDTH), lambda t: (t, 0, 0)),
                  full((B, S5_STATE)), full((B, S5_STATE)), full((1, S5_STATE)), full((1, S5_STATE)),
                  full(sp['w_in'].shape), full(sp['w_out_re'].shape), full(sp['w_out_im'].shape),
                  full((1, S5_WIDTH)), full((S5_WIDTH, S5_WIDTH))],
        out_specs=[pl.BlockSpec((tt, B, S5_WIDTH), lambda t: (t, 0, 0)),
                   full((B, S5_STATE)), full((B, S5_STATE))],
        out_shape=[jax.ShapeDtypeStruct((S, B, S5_WIDTH), F32),
                   jax.ShapeDtypeStruct((B, S5_STATE), F32), jax.ShapeDtypeStruct((B, S5_STATE), F32)],
        scratch_shapes=[pltpu.VMEM((B, S5_STATE), F32), pltpu.VMEM((B, S5_STATE), F32),
                        pltpu.VMEM((tt * B, S5_STATE), F32), pltpu.VMEM((tt * B, S5_STATE), F32)],
        compiler_params=_cparams("arbitrary"),
    )(u3, h0_re, h0_im, sp['lam_re'], sp['lam_im'], sp['w_in'], sp['w_out_re'], sp['w_out_im'], sp['d'], sp['w_glu'])


def _ones_bd():
    r = lax.broadcasted_iota(jnp.int32, (HW, HW), 0) // DH
    c = lax.broadcasted_iota(jnp.int32, (HW, HW), 1) // DH
    return (r == c).astype(F32)


def _mlstm_sample_kernel(za_ref, gt_ref, bias_ref, norm_ref, c_ref, n_ref, m_ref,
                         ha_ref, c_out, n_out, m_out, q_scr, kw_scr, h_scr):
    za = za_ref[...]
    q_scr[...] = (za[:, 0:HW] * (DH ** -0.5)).T
    k_t = za[:, HW:2 * HW].T
    v_t = za[:, 2 * HW:3 * HW].T
    g_t = (gt_ref[...] + bias_ref[...]).T
    m_out[...] = jnp.zeros_like(m_out)
    for h in range(HEADS):
        hs = slice(h * DH, (h + 1) * DH)
        i_h = g_t[h:h + 1, :]
        bm = _log_sigmoid(g_t[HEADS + h:HEADS + h + 1, :]) + m_ref[h:h + 1, :]
        m_t = jnp.maximum(i_h, bm)
        w_in = jnp.exp(i_h - m_t)
        w_st = jnp.exp(bm - m_t)
        q_h, k_h, v_h = q_scr[hs, :], k_t[hs, :], v_t[hs, :]
        s = jnp.sum(q_h * k_h, axis=0, keepdims=True) * w_in
        kw_scr[hs, :] = k_h * w_in

        def body(d, acc):
            r = h * DH + d
            rows = pl.ds(pl.multiple_of(r * DH, DH), DH)
            c_hd = c_ref[rows, :]
            c_out[rows, :] = w_st * c_hd + kw_scr[pl.ds(r, 1), :] * v_h
            return acc + q_scr[pl.ds(r, 1), :] * c_hd

        qc = lax.fori_loop(0, DH, body, jnp.zeros((DH, za.shape[0]), F32), unroll=4)
        n_h = n_ref[hs, :]
        num = w_st * qc + s * v_h
        den = w_st * jnp.sum(q_h * n_h, axis=0, keepdims=True) + s
        h_scr[hs, :] = num / jnp.maximum(jnp.abs(den), jnp.exp(-m_t))
        n_out[hs, :] = w_st * n_h + kw_scr[hs, :]
        m_out[h:h + 1, :] = m_t
    ha_ref[...] = _head_rms(h_scr[...].T, _ones_bd(), norm_ref[...]) * _sigmoid(za[:, 3 * HW:4 * HW])


def mlstm_sample(za, gates, b_i, b_f, norm, c_t, n_t, m_t):
    B = za.shape[0]
    bias = jnp.zeros((1, GATE_PAD), F32).at[0, 0:HEADS].set(b_i).at[0, HEADS:2 * HEADS].set(b_f)
    shp = lambda *s: jax.ShapeDtypeStruct(s, F32)
    return pl.pallas_call(
        _mlstm_sample_kernel,
        out_shape=[shp(B, HW), shp(HW * DH, B), shp(HW, B), shp(SUBLANES, B)],
        scratch_shapes=[pltpu.VMEM((HW, B), F32), pltpu.VMEM((HW, B), F32), pltpu.VMEM((HW, B), F32)],
        compiler_params=pltpu.CompilerParams(vmem_limit_bytes=VMEM_LIMIT),
    )(za, gates, bias, norm.reshape(1, HW), c_t, n_t, m_t)


def _gdn_sample_kernel(zc_ref, zg_ref, gt_ref, buf_ref, cw_ref, alog_ref, dtb_ref, norm_ref, s_ref,
                       hc_ref, s_out, buf_out, q_scr, k_scr, o_scr):
    W3 = 3 * HW
    x = zc_ref[...]
    y = cw_ref[CONV_K - 1:CONV_K, :] * x
    for j in range(CONV_K - 1):
        y = y + cw_ref[j:j + 1, :] * buf_ref[:, j * W3:(j + 1) * W3]
    buf_out[:, 0:(CONV_K - 2) * W3] = buf_ref[:, W3:(CONV_K - 1) * W3]
    buf_out[:, (CONV_K - 2) * W3:(CONV_K - 1) * W3] = x
    y = _silu(y)
    ones_bd = _ones_bd()
    q_raw, k_raw = y[:, 0:HW], y[:, HW:2 * HW]
    q_scr[...] = (q_raw * lax.rsqrt(_dot_hi(q_raw * q_raw, ones_bd) + EPS) * (DH ** -0.5)).T
    k_scr[...] = (k_raw * lax.rsqrt(_dot_hi(k_raw * k_raw, ones_bd) + EPS)).T
    v_t = y[:, 2 * HW:3 * HW].T
    gt = gt_ref[...]
    beta_t = _sigmoid(gt).T
    la_t = (-jnp.exp(alog_ref[...]) * _softplus(gt + dtb_ref[...])).T
    nb = x.shape[0]
    for h in range(HEADS):
        hs = slice(h * DH, (h + 1) * DH)
        beta = beta_t[2 * HEADS + h:2 * HEADS + h + 1, :]
        eg = jnp.exp(la_t[3 * HEADS + h:3 * HEADS + h + 1, :])
        q_h, k_h, v_h = q_scr[hs, :], k_scr[hs, :], v_t[hs, :]

        def read(d, acc):
            ks, qs = acc
            r = h * DH + d
            s_hd = s_ref[pl.ds(pl.multiple_of(r * DH, DH), DH), :]
            return ks + k_scr[pl.ds(r, 1), :] * s_hd, qs + q_scr[pl.ds(r, 1), :] * s_hd

        zero = jnp.zeros((DH, nb), F32)
        ks, qs = lax.fori_loop(0, DH, read, (zero, zero), unroll=4)
        u = v_h - eg * ks
        o_scr[hs, :] = eg * qs + (jnp.sum(q_h * k_h, axis=0, keepdims=True) * beta) * u

        def write(d, carry):
            r = h * DH + d
            rows = pl.ds(pl.multiple_of(r * DH, DH), DH)
            s_out[rows, :] = eg * s_ref[rows, :] + (beta * k_scr[pl.ds(r, 1), :]) * u
            return carry

        lax.fori_loop(0, DH, write, 0, unroll=4)
    hc_ref[...] = _head_rms(o_scr[...].T, ones_bd, norm_ref[...]) * _silu(zg_ref[...])


def gdn_sample(zc, zg, gates, buf, conv_w, a_log, dt_bias, norm, s_t):
    B = zc.shape[0]
    alog, dtb = _gdn_gate_rows(a_log, dt_bias)
    shp = lambda *s: jax.ShapeDtypeStruct(s, F32)
    return pl.pallas_call(
        _gdn_sample_kernel,
        out_shape=[shp(B, HW), shp(HW * DH, B), shp(B, (CONV_K - 1) * 3 * HW)],
        scratch_shapes=[pltpu.VMEM((HW, B), F32), pltpu.VMEM((HW, B), F32), pltpu.VMEM((HW, B), F32)],
        compiler_params=pltpu.CompilerParams(vmem_limit_bytes=VMEM_LIMIT),
    )(zc, zg, gates, buf, conv_w, alog, dtb, jnp.tile(norm, HEADS).reshape(1, HW), s_t)


IN_SEGMENTS = (4 * HW, S5_WIDTH, 3 * HW, HW, GATE_PAD)


def _tile(n, pref):
    return pref if n % pref == 0 else n


def prep_layer_weights(lp):
    w = lp['w_in']
    a, g4 = 4 * HW, HEADS
    o_u = a + 2 * g4
    o_c = o_u + S5_WIDTH
    o_g = o_c + 3 * HW
    o_b = o_g + HW
    gate_cols = jnp.concatenate([w[:, a:a + 2 * g4], w[:, o_b:o_b + 2 * g4],
                                 jnp.zeros((w.shape[0], GATE_PAD - 4 * g4), w.dtype)], axis=1)
    w_in = jnp.concatenate([w[:, 0:a], w[:, o_u:o_c], w[:, o_c:o_g], w[:, o_g:o_b], gate_cols], axis=1)
    bf = lambda n: lp[n].astype(BF16)
    return {'w_in': w_in.astype(BF16), 'w_out': bf('w_out'), 'w_mq': bf('w_mq'), 'w_mo': bf('w_mo'),
            'w_mkv': jnp.concatenate([lp['w_mk'], lp['w_mv']], axis=1).astype(BF16),
            'w_gate': bf('w_gate'), 'w_up': bf('w_up'), 'w_down': bf('w_down'), 's5': s5_params(lp)}


def mixer_prompt(x, lp, W, B, S):
    T = B * S
    tm = _tile(T, 512)
    za, zu, zc, zg, gates = norm_matmul(x, lp['norm_mix'], W['w_in'], IN_SEGMENTS, tm)
    tt = _tile(S, 512)
    ha, c1, n1, m1 = mlstm_prompt(za, gates, lp['mlstm_b_i'], lp['mlstm_b_f'], lp['mlstm_norm'], B, S, tt)
    u3 = jnp.transpose(zu.reshape(B, S, S5_WIDTH), (1, 0, 2))
    h0 = jnp.zeros((B, S5_STATE), F32)
    ys3, r1, i1 = s5_mixer(u3, h0, h0, W['s5'], _tile(S, 128))
    ys = jnp.transpose(ys3, (1, 0, 2)).reshape(T, S5_WIDTH)
    hc, s1, buf1 = gdn_prompt(zc, zg, gates, lp['gdn_conv_w'], lp['gdn_a_log'], lp['gdn_dt_bias'], lp['gdn_norm'], B, S, tt)
    x1 = matmul_residual(x, [ha, ys, hc], W['w_out'], tm)
    return x1, (c1, n1, m1, r1.reshape(B, S5_GROUPS, S5_N), i1.reshape(B, S5_GROUPS, S5_N), s1, buf1)


def mixer_sample(x, st, lp, W):
    B = x.shape[0]
    c0, n0, m0, r0, i0, s0, buf0 = st
    za, zu, zc, zg, gates = norm_matmul(x, lp['norm_mix'], W['w_in'], IN_SEGMENTS, B)
    m_t = jnp.zeros((SUBLANES, B), F32).at[0:HEADS, :].set(m0.T)
    ha, c1t, n1t, m1t = mlstm_sample(za, gates, lp['mlstm_b_i'], lp['mlstm_b_f'], lp['mlstm_norm'],
                                     c0.reshape(B, HW * DH).T, n0.reshape(B, HW).T, m_t)
    ys3, r1, i1 = s5_mixer(zu.reshape(1, B, S5_WIDTH), r0.reshape(B, S5_STATE), i0.reshape(B, S5_STATE), W['s5'], 1)
    hc, s1t, buf1 = gdn_sample(zc, zg, gates, buf0.reshape(B, (CONV_K - 1) * 3 * HW), lp['gdn_conv_w'],
                               lp['gdn_a_log'], lp['gdn_dt_bias'], lp['gdn_norm'], s0.reshape(B, HW * DH).T)
    x1 = matmul_residual(x, [ha, ys3.reshape(B, S5_WIDTH), hc], W['w_out'], B)
    return x1, (c1t.T.reshape(B, HEADS, DH, DH), n1t.T.reshape(B, HEADS, DH), m1t[0:HEADS, :].T,
                r1.reshape(B, S5_GROUPS, S5_N), i1.reshape(B, S5_GROUPS, S5_N),
                s1t.T.reshape(B, HEADS, DH, DH), buf1.reshape(B, CONV_K - 1, 3 * HW))


def mem_kv(mem, lp, W):
    D = mem.shape[1]
    return norm_matmul(mem, lp['norm_mem'], W['w_mkv'], (D, D), _tile(mem.shape[0], 512))


def xattn_ffn_prompt(x, mk, mv, lp, W, S, norm_final, final):
    T, D = x.shape
    tm = _tile(T, 512)
    (q,) = norm_matmul(x, lp['norm_xattn'], W['w_mq'], (D,), tm)
    o = xattn_prompt(q, mk, mv, S, _tile(S, 512))
    x2 = matmul_residual(x, [o], W['w_mo'], tm)
    return ffn(x2, lp['norm_ffn'], W['w_gate'], W['w_up'], W['w_down'], norm_final, final, _tile(T, 1024), 256)


def xattn_ffn_sample(x, ck, cv, lp, W, norm_final, final):
    B, D = x.shape
    (q,) = norm_matmul(x, lp['norm_xattn'], W['w_mq'], (D,), B)
    o = xattn_sample(q, ck, cv, 4)
    x2 = matmul_residual(x, [o], W['w_mo'], B)
    return ffn(x2, lp['norm_ffn'], W['w_gate'], W['w_up'], W['w_down'], norm_final, final, B, 256)


LAYER_PARAMS = ('norm_mix', 'w_in', 'w_out', 'mlstm_b_i', 'mlstm_b_f', 'mlstm_norm', 's5_a_re', 's5_a_im', 's5_log_dt',
                's5_b_re', 's5_b_im', 's5_c_re', 's5_c_im', 's5_d', 's5_w_glu', 'gdn_conv_w', 'gdn_a_log',
                'gdn_dt_bias', 'gdn_norm', 'norm_xattn', 'norm_mem', 'w_mq', 'w_mk', 'w_mv', 'w_mo', 'norm_ffn',
                'w_gate', 'w_up', 'w_down')


def kernel(x_prompt, x_sample, mem_prompt, cache_mem_k, cache_mem_v, state_mlstm_c, state_mlstm_n, state_mlstm_m, state_s5_re, state_s5_im, state_gdn, state_gdn_conv, norm_mix, w_in, w_out, mlstm_b_i, mlstm_b_f, mlstm_norm, s5_a_re, s5_a_im, s5_log_dt, s5_b_re, s5_b_im, s5_c_re, s5_c_im, s5_d, s5_w_glu, gdn_conv_w, gdn_a_log, gdn_dt_bias, gdn_norm, norm_xattn, norm_mem, w_mq, w_mk, w_mv, w_mo, norm_ffn, w_gate, w_up, w_down, norm_final):
    stacked = dict(norm_mix=norm_mix, w_in=w_in, w_out=w_out, mlstm_b_i=mlstm_b_i, mlstm_b_f=mlstm_b_f,
                   mlstm_norm=mlstm_norm, s5_a_re=s5_a_re, s5_a_im=s5_a_im, s5_log_dt=s5_log_dt, s5_b_re=s5_b_re,
                   s5_b_im=s5_b_im, s5_c_re=s5_c_re, s5_c_im=s5_c_im, s5_d=s5_d, s5_w_glu=s5_w_glu,
                   gdn_conv_w=gdn_conv_w, gdn_a_log=gdn_a_log, gdn_dt_bias=gdn_dt_bias, gdn_norm=gdn_norm,
                   norm_xattn=norm_xattn, norm_mem=norm_mem, w_mq=w_mq, w_mk=w_mk, w_mv=w_mv, w_mo=w_mo,
                   norm_ffn=norm_ffn, w_gate=w_gate, w_up=w_up, w_down=w_down)
    B, S, D = x_prompt.shape
    Bs = x_sample.shape[0]
    M = mem_prompt.shape[1]
    depth = w_in.shape[0]
    xp = x_prompt.reshape(B * S, D)
    xs = x_sample.reshape(Bs, D)
    mem = mem_prompt.reshape(B * M, D)
    mem_k, mem_v, st_p, st_s = [], [], [], []
    for l in range(depth):
        lp = {n: stacked[n][l] for n in LAYER_PARAMS}
        W = prep_layer_weights(lp)
        last = l == depth - 1
        xp, sp = mixer_prompt(xp, lp, W, B, S)
        mk, mv = mem_kv(mem, lp, W)
        xp = xattn_ffn_prompt(xp, mk.reshape(B, M, D), mv.reshape(B, M, D), lp, W, S, norm_final, last)
        st_in = (state_mlstm_c[l], state_mlstm_n[l], state_mlstm_m[l], state_s5_re[l], state_s5_im[l],
                 state_gdn[l], state_gdn_conv[l])
        xs, ss = mixer_sample(xs, st_in, lp, W)
        xs = xattn_ffn_sample(xs, cache_mem_k[l].reshape(Bs, M, D), cache_mem_v[l].reshape(Bs, M, D),
                              lp, W, norm_final, last)
        mem_k.append(mk.reshape(B, M, X_HEADS, D // X_HEADS))
        mem_v.append(mv.reshape(B, M, X_HEADS, D // X_HEADS))
        st_p.append(sp)
        st_s.append(ss)
    stack = lambda lst: [jnp.stack([st[i] for st in lst]) for i in range(7)]
    return (xp.reshape(B, S, D), xs.reshape(Bs, 1, D), jnp.stack(mem_k), jnp.stack(mem_v),
            *stack(st_p), *stack(st_s))
```

```python
import functools
import math

import jax
import jax.numpy as jnp
from jax import lax
from jax.experimental import pallas as pl
from jax.experimental.pallas import tpu as pltpu

F32 = jnp.float32
BF16 = jnp.bfloat16
HI = lax.Precision.HIGHEST
EPS = 1e-6

HEADS = 4
DH = 64
HW = HEADS * DH
CHUNK = 64
S5_P = 16
S5_N = 64
S5_GROUPS = 32
S5_WIDTH = S5_GROUPS * S5_P
S5_STATE = S5_GROUPS * S5_N
S5_GB = 8
S5_NBLK = S5_GROUPS // S5_GB
CONV_K = 4
X_HEADS = 4
GATE_PAD = 128
LANES = 128
SUBLANES = 8
VMEM_LIMIT = 48 * 1024 * 1024


def _cparams(*sem):
    return pltpu.CompilerParams(dimension_semantics=sem, vmem_limit_bytes=VMEM_LIMIT)


def _rms(x, g_row):
    return x * lax.rsqrt(jnp.mean(x * x, axis=-1, keepdims=True) + EPS) * g_row


def _dot(a, b):
    return jnp.dot(a, b, preferred_element_type=F32)


def _dot_hi(a, b):
    return jnp.dot(a, b, preferred_element_type=F32, precision=HI)


def _dot_nt(a, b):
    return lax.dot_general(a, b, (((1,), (1,)), ((), ())), preferred_element_type=F32)


def _sigmoid(x):
    return 1.0 / (1.0 + jnp.exp(-x))


def _silu(x):
    return x * _sigmoid(x)


def _softplus(x):
    return jnp.maximum(x, 0.0) + jnp.log1p(jnp.exp(-jnp.abs(x)))


def _log_sigmoid(x):
    return jnp.minimum(x, 0.0) - jnp.log1p(jnp.exp(-jnp.abs(x)))


def _norm_matmul_kernel(x_ref, g_ref, w_ref, *out_refs, splits):
    xn = _rms(x_ref[...], g_ref[...]).astype(BF16)
    off = 0
    for o_ref, n in zip(out_refs, splits):
        o_ref[...] = _dot(xn, w_ref[:, off:off + n])
        off += n


def norm_matmul(x, g, w_bf16, splits, tm):
    T, D = x.shape
    N = w_bf16.shape[1]
    assert sum(splits) == N and T % tm == 0
    return pl.pallas_call(
        functools.partial(_norm_matmul_kernel, splits=tuple(splits)),
        grid=(T // tm,),
        in_specs=[pl.BlockSpec((tm, D), lambda i: (i, 0)),
                  pl.BlockSpec((1, D), lambda i: (0, 0)),
                  pl.BlockSpec((D, N), lambda i: (0, 0))],
        out_specs=[pl.BlockSpec((tm, n), lambda i: (i, 0)) for n in splits],
        out_shape=[jax.ShapeDtypeStruct((T, n), F32) for n in splits],
        compiler_params=_cparams("parallel"),
    )(x, g.reshape(1, D), w_bf16)


def _matmul_residual_kernel(x_ref, *refs, ksplits):
    a_refs, w_ref, o_ref = refs[:-2], refs[-2], refs[-1]
    acc = x_ref[...]
    off = 0
    for a_ref, k in zip(a_refs, ksplits):
        acc = acc + _dot(a_ref[...].astype(BF16), w_ref[off:off + k, :])
        off += k
    o_ref[...] = acc


def matmul_residual(x, acts, w_bf16, tm):
    T, D = x.shape
    ks = tuple(a.shape[1] for a in acts)
    K = w_bf16.shape[0]
    assert sum(ks) == K and T % tm == 0
    return pl.pallas_call(
        functools.partial(_matmul_residual_kernel, ksplits=ks),
        grid=(T // tm,),
        in_specs=[pl.BlockSpec((tm, D), lambda i: (i, 0))]
                 + [pl.BlockSpec((tm, k), lambda i: (i, 0)) for k in ks]
                 + [pl.BlockSpec((K, D), lambda i: (0, 0))],
        out_specs=pl.BlockSpec((tm, D), lambda i: (i, 0)),
        out_shape=jax.ShapeDtypeStruct((T, D), F32),
        compiler_params=_cparams("parallel"),
    )(x, *acts, w_bf16)


def _ffn_kernel(x_ref, g_ref, wg_ref, wu_ref, wd_ref, gf_ref, o_ref, hn_scr, acc_scr, *, final_norm):
    j = pl.program_id(1)

    @pl.when(j == 0)
    def _():
        x = x_ref[...]
        hn_scr[...] = _rms(x, g_ref[...]).astype(BF16)
        acc_scr[...] = x

    h = hn_scr[...]
    a = _dot(h, wg_ref[...])
    b = _dot(h, wu_ref[...])
    acc_scr[...] += _dot((_silu(a) * b).astype(BF16), wd_ref[...])

    @pl.when(j == pl.num_programs(1) - 1)
    def _():
        y = acc_scr[...]
        if final_norm:
            y = _rms(y, gf_ref[...])
        o_ref[...] = y


def ffn(x, g, wg, wu, wd, g_final, final_norm, tm, tf):
    T, D = x.shape
    F = wg.shape[1]
    assert T % tm == 0 and F % tf == 0
    return pl.pallas_call(
        functools.partial(_ffn_kernel, final_norm=final_norm),
        grid=(T // tm, F // tf),
        in_specs=[pl.BlockSpec((tm, D), lambda i, j: (i, 0)),
                  pl.BlockSpec((1, D), lambda i, j: (0, 0)),
                  pl.BlockSpec((D, tf), lambda i, j: (0, j)),
                  pl.BlockSpec((D, tf), lambda i, j: (0, j)),
                  pl.BlockSpec((tf, D), lambda i, j: (j, 0)),
                  pl.BlockSpec((1, D), lambda i, j: (0, 0))],
        out_specs=pl.BlockSpec((tm, D), lambda i, j: (i, 0)),
        out_shape=jax.ShapeDtypeStruct((T, D), F32),
        scratch_shapes=[pltpu.VMEM((tm, D), BF16), pltpu.VMEM((tm, D), F32)],
        compiler_params=_cparams("parallel", "arbitrary"),
    )(x, g.reshape(1, D), wg, wu, wd, g_final.reshape(1, D))


def _softmax_rows(s):
    e = jnp.exp(s - jnp.max(s, axis=-1, keepdims=True))
    return e / jnp.sum(e, axis=-1, keepdims=True)


def _xattn_prompt_kernel(q_ref, k_ref, v_ref, o_ref, *, dh):
    scale = dh ** -0.5
    for h in range(X_HEADS):
        sl = slice(h * dh, (h + 1) * dh)
        s = _dot_nt(q_ref[:, sl].astype(BF16), k_ref[0, :, sl].astype(BF16)) * scale
        o_ref[:, sl] = _dot(_softmax_rows(s).astype(BF16), v_ref[0, :, sl].astype(BF16))


def xattn_prompt(q, mk, mv, seq, tq):
    T, D = q.shape
    B, M, _ = mk.shape
    nt = seq // tq
    return pl.pallas_call(
        functools.partial(_xattn_prompt_kernel, dh=D // X_HEADS),
        grid=(B, nt),
        in_specs=[pl.BlockSpec((tq, D), lambda b, t: (b * nt + t, 0)),
                  pl.BlockSpec((1, M, D), lambda b, t: (b, 0, 0)),
                  pl.BlockSpec((1, M, D), lambda b, t: (b, 0, 0))],
        out_specs=pl.BlockSpec((tq, D), lambda b, t: (b * nt + t, 0)),
        out_shape=jax.ShapeDtypeStruct((T, D), F32),
        compiler_params=_cparams("parallel", "parallel"),
    )(q, mk, mv)


def _xattn_sample_kernel(q_ref, k_ref, v_ref, o_ref, *, dh, sb):
    D = q_ref.shape[-1]
    scale = dh ** -0.5
    row = lax.broadcasted_iota(jnp.int32, (SUBLANES, D), 0)
    head_of_lane = lax.broadcasted_iota(jnp.int32, (SUBLANES, D), 1) // dh
    head_mask = row == head_of_lane
    for i in range(sb):
        qm = jnp.where(head_mask, jnp.broadcast_to(q_ref[i], (SUBLANES, D)), 0.0).astype(BF16)
        s = _dot_nt(qm, k_ref[i].astype(BF16)) * scale
        o_all = _dot(_softmax_rows(s).astype(BF16), v_ref[i].astype(BF16))
        o_ref[i] = jnp.sum(jnp.where(head_mask, o_all, 0.0), axis=0, keepdims=True)


def xattn_sample(q, ck, cv, layer, sb):
    B, D = q.shape
    M = ck.shape[2]
    out = pl.pallas_call(
        functools.partial(_xattn_sample_kernel, dh=D // X_HEADS, sb=sb),
        grid=(B // sb,),
        in_specs=[pl.BlockSpec((sb, 1, D), lambda i: (i, 0, 0)),
                  pl.BlockSpec((None, sb, M, D), lambda i: (layer, i, 0, 0)),
                  pl.BlockSpec((None, sb, M, D), lambda i: (layer, i, 0, 0))],
        out_specs=pl.BlockSpec((sb, 1, D), lambda i: (i, 0, 0)),
        out_shape=jax.ShapeDtypeStruct((B, 1, D), F32),
        compiler_params=_cparams("parallel"),
    )(q.reshape(B, 1, D), ck, cv)
    return out.reshape(B, D)


def _lane_cat_masks(L):
    row = lax.broadcasted_iota(jnp.int32, (L, HW), 0)
    j = lax.broadcasted_iota(jnp.int32, (L, HW), 1) % DH
    r2 = lax.broadcasted_iota(jnp.int32, (HW, HW), 0) // DH
    c2 = lax.broadcasted_iota(jnp.int32, (HW, HW), 1) // DH
    return row >= j, row > j, row == j, r2 == c2


def _expand_bd(x, bd):
    return jnp.where(bd, jnp.concatenate([x] * HEADS, axis=0), jnp.zeros((), x.dtype))


def _seg_reduce(x, op, fill):
    lo = lax.broadcasted_iota(jnp.int32, (x.shape[0], LANES), 1) < DH
    parts = []
    for c in range(HW // LANES):
        xh = x[:, c * LANES:(c + 1) * LANES]
        a = op(jnp.where(lo, xh, fill), axis=-1, keepdims=True)
        b = op(jnp.where(lo, fill, xh), axis=-1, keepdims=True)
        parts.append(jnp.where(lo, a, b))
    return jnp.concatenate(parts, axis=-1)


def _head_expander(first_lane):
    r = lax.broadcasted_iota(jnp.int32, (GATE_PAD, HW), 0)
    c = lax.broadcasted_iota(jnp.int32, (GATE_PAD, HW), 1) // DH
    return (r == c + first_lane).astype(F32)


def _col_to_row(x, eye):
    return jnp.sum(jnp.where(eye, x, 0.0), axis=0, keepdims=True)


def _head_rms(x, ones_bd, g_row):
    ms = _dot_hi(x * x, ones_bd) * (1.0 / DH)
    return x * lax.rsqrt(ms + EPS) * g_row


def _mlstm_prompt_kernel(za_ref, gt_ref, bias_ref, norm_ref, ha_ref, c_ref, n_ref, m_ref,
                         c_scr, n_scr, m_scr, ix_scr, fx_scr, *, nchunks):
    tb = pl.program_id(1)
    L = CHUNK
    tril, _, eye, bd = _lane_cat_masks(L)
    ones_bd = bd.astype(F32)
    tril_f = (lax.broadcasted_iota(jnp.int32, (L, L), 0) >= lax.broadcasted_iota(jnp.int32, (L, L), 1)).astype(F32)

    @pl.when(tb == 0)
    def _():
        c_scr[...] = jnp.zeros_like(c_scr)
        n_scr[...] = jnp.zeros_like(n_scr)
        m_scr[...] = jnp.zeros_like(m_scr)

    gt = gt_ref[...] + bias_ref[...]
    ix_scr[...] = _dot_hi(gt, _head_expander(0))
    fx_scr[...] = _dot_hi(_log_sigmoid(gt), _head_expander(HEADS))

    def chunk(ci, carry):
        rows = pl.ds(pl.multiple_of(ci * L, L), L)
        q = (za_ref[rows, 0:HW] * (DH ** -0.5)).astype(BF16)
        k = za_ref[rows, HW:2 * HW]
        v = za_ref[rows, 2 * HW:3 * HW].astype(BF16)
        og = za_ref[rows, 3 * HW:4 * HW]
        i_x = ix_scr[rows, :]
        b_x = _dot_hi(tril_f, fx_scr[rows, :])
        c_bd = c_scr[...]
        n_row = n_scr[...]
        m_x = m_scr[...]

        d_intra = jnp.where(tril, b_x - _col_to_row(b_x, eye) + _col_to_row(i_x, eye), -jnp.inf)
        d_inter = b_x + m_x
        m_t = jnp.maximum(_seg_reduce(d_intra, jnp.max, -jnp.inf), d_inter)
        w_intra = jnp.exp(d_intra - m_t)
        w_inter = jnp.exp(d_inter - m_t)
        kb = k.astype(BF16)
        s = _dot_nt(q, _expand_bd(kb, bd)) * w_intra
        num = w_inter * _dot(q, c_bd.astype(BF16)) + _dot(s.astype(BF16), _expand_bd(v, bd))
        qn = _dot((q.astype(F32) * n_row).astype(BF16), ones_bd.astype(BF16))
        den = w_inter * qn + _seg_reduce(s, jnp.sum, 0.0)
        hh = num / jnp.maximum(jnp.abs(den), jnp.exp(-m_t))
        ha_ref[rows, :] = _head_rms(hh, ones_bd, norm_ref[...]) * _sigmoid(og)

        b_last = b_x[L - 1:L, :]
        g_x = b_last - b_x + i_x
        m_new = jnp.maximum(b_last + m_x, jnp.max(g_x, axis=0, keepdims=True))
        kw = k * jnp.exp(g_x - m_new)
        decay = jnp.exp(b_last + m_x - m_new)
        c_scr[...] = decay * c_bd + jnp.where(bd, _dot(kw.T.astype(BF16), v), 0.0)
        n_scr[...] = decay * n_row + jnp.sum(kw, axis=0, keepdims=True)
        m_scr[...] = m_new
        return carry

    lax.fori_loop(0, nchunks, chunk, 0)

    @pl.when(tb == pl.num_programs(1) - 1)
    def _():
        for h in range(HEADS):
            c_ref[0, h] = c_scr[h * DH:(h + 1) * DH, h * DH:(h + 1) * DH]
        n_ref[0] = n_scr[...]
        m_ref[0] = m_scr[...]


def mlstm_prompt(za, gates, b_i, b_f, norm, B, S, tt):
    assert S % tt == 0 and tt % CHUNK == 0
    nt = S // tt
    bias = jnp.zeros((1, GATE_PAD), F32).at[0, 0:HEADS].set(b_i).at[0, HEADS:2 * HEADS].set(b_f)
    ha, c, n, m = pl.pallas_call(
        functools.partial(_mlstm_prompt_kernel, nchunks=tt // CHUNK),
        grid=(B, nt),
        in_specs=[pl.BlockSpec((tt, 4 * HW), lambda b, t: (b * nt + t, 0)),
                  pl.BlockSpec((tt, GATE_PAD), lambda b, t: (b * nt + t, 0)),
                  pl.BlockSpec((1, GATE_PAD), lambda b, t: (0, 0)),
                  pl.BlockSpec((1, HW), lambda b, t: (0, 0))],
        out_specs=[pl.BlockSpec((tt, HW), lambda b, t: (b * nt + t, 0)),
                   pl.BlockSpec((1, HEADS, DH, DH), lambda b, t: (b, 0, 0, 0)),
                   pl.BlockSpec((1, 1, HW), lambda b, t: (b, 0, 0)),
                   pl.BlockSpec((1, 1, HW), lambda b, t: (b, 0, 0))],
        out_shape=[jax.ShapeDtypeStruct((B * S, HW), F32),
                   jax.ShapeDtypeStruct((B, HEADS, DH, DH), F32),
                   jax.ShapeDtypeStruct((B, 1, HW), F32),
                   jax.ShapeDtypeStruct((B, 1, HW), F32)],
        scratch_shapes=[pltpu.VMEM((HW, HW), F32), pltpu.VMEM((1, HW), F32), pltpu.VMEM((1, HW), F32),
                        pltpu.VMEM((tt, HW), F32), pltpu.VMEM((tt, HW), F32)],
        compiler_params=_cparams("parallel", "arbitrary"),
    )(za, gates, bias, norm.reshape(1, HW))
    return ha, c, n.reshape(B, HEADS, DH), m[:, 0, ::DH]


def _gdn_prompt_kernel(zc_ref, zg_ref, gt_ref, cw_ref, alog_ref, dtb_ref, norm_ref,
                       hc_ref, s_ref, tail_ref,
                       s_scr, xp_scr, q_scr, k_scr, v_scr, beta_scr, la_scr, *, nchunks):
    tb = pl.program_id(1)
    L = CHUNK
    tt = nchunks * L
    pad = SUBLANES
    tril, strict, eye, bd = _lane_cat_masks(L)
    ones_bd = bd.astype(F32)
    tril_f = (lax.broadcasted_iota(jnp.int32, (L, L), 0) >= lax.broadcasted_iota(jnp.int32, (L, L), 1)).astype(F32)

    @pl.when(tb == 0)
    def _():
        s_scr[...] = jnp.zeros_like(s_scr)
        xp_scr[0:pad, :] = jnp.zeros((pad, 3 * HW), F32)

    x = zc_ref[...]
    xp_scr[pad:pad + tt, :] = x
    y = cw_ref[CONV_K - 1:CONV_K, :] * x
    for j in range(CONV_K - 1):
        y = y + cw_ref[j:j + 1, :] * xp_scr[pl.ds(pad - (CONV_K - 1) + j, tt), :]
    xp_scr[0:pad, :] = x[tt - pad:tt, :]
    y = _silu(y)
    q_raw, k_raw = y[:, 0:HW], y[:, HW:2 * HW]
    q_scr[...] = q_raw * lax.rsqrt(_dot_hi(q_raw * q_raw, ones_bd) + EPS) * (DH ** -0.5)
    k_scr[...] = k_raw * lax.rsqrt(_dot_hi(k_raw * k_raw, ones_bd) + EPS)
    v_scr[...] = y[:, 2 * HW:3 * HW]
    gt = gt_ref[...]
    beta_scr[...] = _dot_hi(_sigmoid(gt), _head_expander(2 * HEADS))
    la_scr[...] = _dot_hi(-jnp.exp(alog_ref[...]) * _softplus(gt + dtb_ref[...]), _head_expander(3 * HEADS))

    def chunk(ci, carry):
        rows = pl.ds(pl.multiple_of(ci * L, L), L)
        q = q_scr[rows, :].astype(BF16)
        k = k_scr[rows, :]
        kb = k.astype(BF16)
        v = v_scr[rows, :]
        beta_x = beta_scr[rows, :]
        g_x = _dot_hi(tril_f, la_scr[rows, :])
        diff = g_x - _col_to_row(g_x, eye)
        dec_incl = jnp.where(tril, jnp.exp(jnp.where(tril, diff, 0.0)), 0.0)
        dec_strict = jnp.where(strict, dec_incl, 0.0)
        eg = jnp.exp(g_x)
        beta_row = _col_to_row(beta_x, eye)
        k_bd = _expand_bd(kb, bd)
        n0 = -(dec_strict * _dot_nt(kb, k_bd) * beta_row)
        qk = _dot_nt(q, k_bd) * dec_incl * beta_row

        p = _dot(n0.astype(BF16), _expand_bd(n0.astype(BF16), bd))
        m = n0
        steps = int(math.log2(L)) - 1
        for i in range(steps):
            p_bd = _expand_bd(p.astype(BF16), bd)
            if i < steps - 1:
                pm = _dot(jnp.concatenate([p, m], axis=0).astype(BF16), p_bd)
                p_next, mp = pm[0:L], pm[L:2 * L]
            else:
                p_next, mp = None, _dot(m.astype(BF16), p_bd)
            m = m + p + mp
            p = p_next

        s_bd = s_scr[...]
        kqs = _dot(jnp.concatenate([kb, q], axis=0), s_bd.astype(BF16))
        rhs = v - eg * kqs[0:L]
        u = rhs + _dot(m.astype(BF16), _expand_bd(rhs.astype(BF16), bd))
        ub = u.astype(BF16)
        o = eg * kqs[L:2 * L] + _dot(qk.astype(BF16), _expand_bd(ub, bd))
        hc_ref[rows, :] = _head_rms(o, ones_bd, norm_ref[...]) * _silu(zg_ref[rows, :])

        g_last = g_x[L - 1:L, :]
        kw = k * (jnp.exp(g_last - g_x) * beta_x)
        s_scr[...] = jnp.exp(g_last) * s_bd + jnp.where(bd, _dot(kw.T.astype(BF16), ub), 0.0)
        return carry

    lax.fori_loop(0, nchunks, chunk, 0)

    @pl.when(tb == pl.num_programs(1) - 1)
    def _():
        for h in range(HEADS):
            s_ref[0, h] = s_scr[h * DH:(h + 1) * DH, h * DH:(h + 1) * DH]
        tail_ref[0] = xp_scr[0:pad, :]


def _gdn_gate_rows(a_log, dt_bias):
    z = jnp.zeros((1, GATE_PAD), F32)
    return (z.at[0, 3 * HEADS:4 * HEADS].set(a_log), z.at[0, 3 * HEADS:4 * HEADS].set(dt_bias))


def gdn_prompt(zc, zg, gates, conv_w, a_log, dt_bias, norm, B, S, tt):
    assert S % tt == 0 and tt % CHUNK == 0
    nt = S // tt
    alog, dtb = _gdn_gate_rows(a_log, dt_bias)
    hc, s, tail = pl.pallas_call(
        functools.partial(_gdn_prompt_kernel, nchunks=tt // CHUNK),
        grid=(B, nt),
        in_specs=[pl.BlockSpec((tt, 3 * HW), lambda b, t: (b * nt + t, 0)),
                  pl.BlockSpec((tt, HW), lambda b, t: (b * nt + t, 0)),
                  pl.BlockSpec((tt, GATE_PAD), lambda b, t: (b * nt + t, 0)),
                  pl.BlockSpec((CONV_K, 3 * HW), lambda b, t: (0, 0)),
                  pl.BlockSpec((1, GATE_PAD), lambda b, t: (0, 0)),
                  pl.BlockSpec((1, GATE_PAD), lambda b, t: (0, 0)),
                  pl.BlockSpec((1, HW), lambda b, t: (0, 0))],
        out_specs=[pl.BlockSpec((tt, HW), lambda b, t: (b * nt + t, 0)),
                   pl.BlockSpec((1, HEADS, DH, DH), lambda b, t: (b, 0, 0, 0)),
                   pl.BlockSpec((1, SUBLANES, 3 * HW), lambda b, t: (b, 0, 0))],
        out_shape=[jax.ShapeDtypeStruct((B * S, HW), F32),
                   jax.ShapeDtypeStruct((B, HEADS, DH, DH), F32),
                   jax.ShapeDtypeStruct((B, SUBLANES, 3 * HW), F32)],
        scratch_shapes=[pltpu.VMEM((HW, HW), F32), pltpu.VMEM((tt + SUBLANES, 3 * HW), F32),
                        pltpu.VMEM((tt, HW), F32), pltpu.VMEM((tt, HW), F32), pltpu.VMEM((tt, HW), F32),
                        pltpu.VMEM((tt, HW), F32), pltpu.VMEM((tt, HW), F32)],
        compiler_params=_cparams("parallel", "arbitrary"),
    )(zc, zg, gates, conv_w, alog, dtb, jnp.tile(norm, HEADS).reshape(1, HW))
    return hc, s, tail[:, SUBLANES - (CONV_K - 1):, :]


def _s5_prep_kernel(are_ref, aim_ref, ldt_ref, bre_ref, bim_ref, lre_ref, lim_ref, bbre_ref, bbim_ref):
    a_re, a_im = are_ref[...], aim_ref[...]
    dt = jnp.exp(ldt_ref[...])
    mag = jnp.exp(a_re * dt)
    lam_re, lam_im = mag * jnp.cos(a_im * dt), mag * jnp.sin(a_im * dt)
    lre_ref[...] = lam_re
    lim_ref[...] = lam_im
    nr, ni = lam_re - 1.0, lam_im
    den = a_re * a_re + a_im * a_im
    coef_re = (nr * a_re + ni * a_im) / den
    coef_im = (ni * a_re - nr * a_im) / den
    b_re, b_im = bre_ref[...], bim_ref[...]
    bbre_ref[...] = coef_re * b_re - coef_im * b_im
    bbim_ref[...] = coef_re * b_im + coef_im * b_re


def s5_params(lp):
    G, N, P = S5_GROUPS, S5_N, S5_P
    row = lambda a: a.astype(F32).reshape(1, G * N)
    to_pn = lambda b: jnp.transpose(b.astype(F32), (2, 0, 1)).reshape(P, G * N)
    shp = [jax.ShapeDtypeStruct((1, G * N), F32)] * 2 + [jax.ShapeDtypeStruct((P, G * N), F32)] * 2
    lam_re, lam_im, bb_re, bb_im = pl.pallas_call(_s5_prep_kernel, out_shape=shp)(
        row(lp['s5_a_re']), row(lp['s5_a_im']), row(jnp.repeat(lp['s5_log_dt'][:, None], N, axis=1)),
        to_pn(lp['s5_b_re']), to_pn(lp['s5_b_im']))
    eye = jnp.eye(S5_GB, dtype=F32)

    def w_in_blocks(bb):
        b4 = bb.reshape(P, S5_NBLK, S5_GB, N)
        return jnp.einsum('pbgn,gh->bgphn', b4, eye).reshape(S5_NBLK, S5_GB * P, S5_GB * N)

    def w_out_blocks(c):
        c4 = c.astype(F32).reshape(S5_NBLK, S5_GB, P, N)
        return jnp.einsum('bgpn,gh->bgnhp', c4, eye).reshape(S5_NBLK, S5_GB * N, S5_GB * P)

    w_in = jnp.concatenate([w_in_blocks(bb_re), w_in_blocks(bb_im)], axis=-1).astype(BF16)
    return {'lam_re': lam_re, 'lam_im': lam_im, 'w_in': w_in,
            'w_out_re': w_out_blocks(lp['s5_c_re']).astype(BF16),
            'w_out_im': (-w_out_blocks(lp['s5_c_im'])).astype(BF16),
            'd': lp['s5_d'].astype(F32).reshape(1, S5_WIDTH), 'w_glu': lp['s5_w_glu'].astype(BF16)}


def _s5_kernel(u_ref, h0r_ref, h0i_ref, lamr_ref, lami_ref, win_ref, wor_ref, woi_ref, d_ref, wglu_ref,
               ys_ref, h1r_ref, h1i_ref, hr_scr, hi_scr, br_scr, bi_scr, *, nseq, rows, bb, lane_blk):
    tb = pl.program_id(0)
    nsteps = nseq * rows // bb
    blk_in, blk_st = S5_GB * S5_P, S5_GB * S5_N
    seq_rows = lambda i: pl.ds(i, rows, stride=nseq) if nseq > 1 else slice(None)

    @pl.when(tb == 0)
    def _():
        hr_scr[...] = h0r_ref[...]
        hi_scr[...] = h0i_ref[...]

    tiles_blk = blk_st // LANES

    def load_tiles(scr, c0, n, r):
        return jnp.concatenate([scr[c0 + c, r, :] for c in range(n)], axis=-1)

    def store_tiles(scr, c0, r, val):
        for c in range(val.shape[-1] // LANES):
            scr[c0 + c, r, :] = val[:, c * LANES:(c + 1) * LANES]

    for i in range(nseq):
        ub = u_ref[i].astype(BF16)
        for blk in range(S5_NBLK):
            bu = _dot(ub[:, blk * blk_in:(blk + 1) * blk_in], win_ref[blk])
            store_tiles(br_scr, blk * tiles_blk, seq_rows(i), bu[:, 0:blk_st])
            store_tiles(bi_scr, blk * tiles_blk, seq_rows(i), bu[:, blk_st:2 * blk_st])

    tiles_lb = lane_blk // LANES
    for lb in range(S5_STATE // lane_blk):
        ls = slice(lb * lane_blk, (lb + 1) * lane_blk)
        lr = jnp.broadcast_to(lamr_ref[:, ls], (bb, lane_blk))
        li = jnp.broadcast_to(lami_ref[:, ls], (bb, lane_blk))

        def step(t, carry):
            hr, hi = carry
            r = pl.ds(pl.multiple_of(t * bb, bb), bb)
            nr = lr * hr - li * hi + load_tiles(br_scr, lb * tiles_lb, tiles_lb, r)
            ni = lr * hi + li * hr + load_tiles(bi_scr, lb * tiles_lb, tiles_lb, r)
            store_tiles(br_scr, lb * tiles_lb, r, nr)
            store_tiles(bi_scr, lb * tiles_lb, r, ni)
            return nr, ni

        hr, hi = lax.fori_loop(0, nsteps, step, (hr_scr[:, ls], hi_scr[:, ls]), unroll=min(nsteps, 8))
        hr_scr[:, ls] = hr
        hi_scr[:, ls] = hi

    for i in range(nseq):
        ys = []
        for blk in range(S5_NBLK):
            h_re = load_tiles(br_scr, blk * tiles_blk, tiles_blk, seq_rows(i))
            h_im = load_tiles(bi_scr, blk * tiles_blk, tiles_blk, seq_rows(i))
            ys.append(_dot(h_re.astype(BF16), wor_ref[blk]) + _dot(h_im.astype(BF16), woi_ref[blk]))
        gy = jax.nn.gelu(jnp.concatenate(ys, axis=-1) + d_ref[...] * u_ref[i])
        ys_ref[i] = gy * _sigmoid(_dot(gy.astype(BF16), wglu_ref[...]))

    @pl.when(tb == pl.num_programs(0) - 1)
    def _():
        h1r_ref[...] = hr_scr[...]
        h1i_ref[...] = hi_scr[...]


def s5_mixer(u, h0_re, h0_im, sp, tt, single_step):
    B = h0_re.shape[0]
    nseq, S = (1, 1) if single_step else (B, u.shape[1])
    rows = B if single_step else tt
    assert S % tt == 0 and B % SUBLANES == 0
    assert u.shape == ((1, B, S5_WIDTH) if single_step else (B, S, S5_WIDTH))
    lane_blk = max(LANES, min(S5_STATE, (SUBLANES * SUBLANES * LANES) // B))
    full = lambda shape: pl.BlockSpec(shape, lambda t: (0,) * len(shape))
    return pl.pallas_call(
        functools.partial(_s5_kernel, nseq=nseq, rows=rows, bb=B, lane_blk=lane_blk),
        grid=(S // tt,),
        in_specs=[pl.BlockSpec((nseq, rows, S5_WIDTH), lambda t: (0, t, 0)),
                  full((B, S5_STATE)), full((B, S5_STATE)), full((1, S5_STATE)), full((1, S5_STATE)),
                  full(sp['w_in'].shape), full(sp['w_out_re'].shape), full(sp['w_out_im'].shape),
                  full((1, S5_WIDTH)), full((S5_WIDTH, S5_WIDTH))],
        out_specs=[pl.BlockSpec((nseq, rows, S5_WIDTH), lambda t: (0, t, 0)),
                   full((B, S5_STATE)), full((B, S5_STATE))],
        out_shape=[jax.ShapeDtypeStruct(u.shape, F32),
                   jax.ShapeDtypeStruct((B, S5_STATE), F32), jax.ShapeDtypeStruct((B, S5_STATE), F32)],
        scratch_shapes=[pltpu.VMEM((B, S5_STATE), F32), pltpu.VMEM((B, S5_STATE), F32),
                        pltpu.VMEM((S5_STATE // LANES, nseq * rows, LANES), F32),
                        pltpu.VMEM((S5_STATE // LANES, nseq * rows, LANES), F32)],
        compiler_params=_cparams("arbitrary"),
    )(u, h0_re, h0_im, sp['lam_re'], sp['lam_im'], sp['w_in'], sp['w_out_re'], sp['w_out_im'], sp['d'], sp['w_glu'])


def _ones_bd():
    r = lax.broadcasted_iota(jnp.int32, (HW, HW), 0) // DH
    c = lax.broadcasted_iota(jnp.int32, (HW, HW), 1) // DH
    return (r == c).astype(F32)


def _mlstm_sample_kernel(za_ref, gt_ref, bias_ref, norm_ref, c_ref, n_ref, m_ref,
                         ha_ref, c_out, n_out, m_out, q_scr, kw_scr, h_scr):
    za = za_ref[...]
    q_scr[...] = (za[:, 0:HW] * (DH ** -0.5)).T
    k_t = za[:, HW:2 * HW].T
    v_t = za[:, 2 * HW:3 * HW].T
    g_t = (gt_ref[...] + bias_ref[...]).T
    m_out[...] = jnp.zeros_like(m_out)
    for h in range(HEADS):
        hs = slice(h * DH, (h + 1) * DH)
        i_h = g_t[h:h + 1, :]
        bm = _log_sigmoid(g_t[HEADS + h:HEADS + h + 1, :]) + m_ref[h:h + 1, :]
        m_t = jnp.maximum(i_h, bm)
        w_in = jnp.exp(i_h - m_t)
        w_st = jnp.exp(bm - m_t)
        q_h, k_h, v_h = q_scr[hs, :], k_t[hs, :], v_t[hs, :]
        s = jnp.sum(q_h * k_h, axis=0, keepdims=True) * w_in
        kw_scr[hs, :] = k_h * w_in

        def body(d, acc):
            r = h * DH + d
            rows = pl.ds(pl.multiple_of(r * DH, DH), DH)
            c_hd = c_ref[rows, :]
            c_out[rows, :] = w_st * c_hd + kw_scr[pl.ds(r, 1), :] * v_h
            return acc + q_scr[pl.ds(r, 1), :] * c_hd

        qc = lax.fori_loop(0, DH, body, jnp.zeros((DH, za.shape[0]), F32), unroll=4)
        n_h = n_ref[hs, :]
        num = w_st * qc + s * v_h
        den = w_st * jnp.sum(q_h * n_h, axis=0, keepdims=True) + s
        h_scr[hs, :] = num / jnp.maximum(jnp.abs(den), jnp.exp(-m_t))
        n_out[hs, :] = w_st * n_h + kw_scr[hs, :]
        m_out[h:h + 1, :] = m_t
    ha_ref[...] = _head_rms(h_scr[...].T, _ones_bd(), norm_ref[...]) * _sigmoid(za[:, 3 * HW:4 * HW])


def mlstm_sample(za, gates, b_i, b_f, norm, c_t, n_t, m_t):
    B = za.shape[0]
    bias = jnp.zeros((1, GATE_PAD), F32).at[0, 0:HEADS].set(b_i).at[0, HEADS:2 * HEADS].set(b_f)
    shp = lambda *s: jax.ShapeDtypeStruct(s, F32)
    return pl.pallas_call(
        _mlstm_sample_kernel,
        out_shape=[shp(B, HW), shp(HW * DH, B), shp(HW, B), shp(SUBLANES, B)],
        scratch_shapes=[pltpu.VMEM((HW, B), F32), pltpu.VMEM((HW, B), F32), pltpu.VMEM((HW, B), F32)],
        compiler_params=pltpu.CompilerParams(vmem_limit_bytes=VMEM_LIMIT),
    )(za, gates, bias, norm.reshape(1, HW), c_t, n_t, m_t)


def _gdn_sample_kernel(zc_ref, zg_ref, gt_ref, buf_ref, cw_ref, alog_ref, dtb_ref, norm_ref, s_ref,
                       hc_ref, s_out, buf_out, q_scr, k_scr, o_scr):
    W3 = 3 * HW
    x = zc_ref[...]
    y = cw_ref[CONV_K - 1:CONV_K, :] * x
    for j in range(CONV_K - 1):
        y = y + cw_ref[j:j + 1, :] * buf_ref[:, j * W3:(j + 1) * W3]
    buf_out[:, 0:(CONV_K - 2) * W3] = buf_ref[:, W3:(CONV_K - 1) * W3]
    buf_out[:, (CONV_K - 2) * W3:(CONV_K - 1) * W3] = x
    y = _silu(y)
    ones_bd = _ones_bd()
    q_raw, k_raw = y[:, 0:HW], y[:, HW:2 * HW]
    q_scr[...] = (q_raw * lax.rsqrt(_dot_hi(q_raw * q_raw, ones_bd) + EPS) * (DH ** -0.5)).T
    k_scr[...] = (k_raw * lax.rsqrt(_dot_hi(k_raw * k_raw, ones_bd) + EPS)).T
    v_t = y[:, 2 * HW:3 * HW].T
    gt = gt_ref[...]
    beta_t = _sigmoid(gt).T
    la_t = (-jnp.exp(alog_ref[...]) * _softplus(gt + dtb_ref[...])).T
    nb = x.shape[0]
    for h in range(HEADS):
        hs = slice(h * DH, (h + 1) * DH)
        beta = beta_t[2 * HEADS + h:2 * HEADS + h + 1, :]
        eg = jnp.exp(la_t[3 * HEADS + h:3 * HEADS + h + 1, :])
        q_h, k_h, v_h = q_scr[hs, :], k_scr[hs, :], v_t[hs, :]

        def read(d, acc):
            ks, qs = acc
            r = h * DH + d
            s_hd = s_ref[pl.ds(pl.multiple_of(r * DH, DH), DH), :]
            return ks + k_scr[pl.ds(r, 1), :] * s_hd, qs + q_scr[pl.ds(r, 1), :] * s_hd

        zero = jnp.zeros((DH, nb), F32)
        ks, qs = lax.fori_loop(0, DH, read, (zero, zero), unroll=4)
        u = v_h - eg * ks
        o_scr[hs, :] = eg * qs + (jnp.sum(q_h * k_h, axis=0, keepdims=True) * beta) * u

        def write(d, carry):
            r = h * DH + d
            rows = pl.ds(pl.multiple_of(r * DH, DH), DH)
            s_out[rows, :] = eg * s_ref[rows, :] + (beta * k_scr[pl.ds(r, 1), :]) * u
            return carry

        lax.fori_loop(0, DH, write, 0, unroll=4)
    hc_ref[...] = _head_rms(o_scr[...].T, ones_bd, norm_ref[...]) * _silu(zg_ref[...])


def gdn_sample(zc, zg, gates, buf, conv_w, a_log, dt_bias, norm, s_t):
    B = zc.shape[0]
    alog, dtb = _gdn_gate_rows(a_log, dt_bias)
    shp = lambda *s: jax.ShapeDtypeStruct(s, F32)
    return pl.pallas_call(
        _gdn_sample_kernel,
        out_shape=[shp(B, HW), shp(HW * DH, B), shp(B, (CONV_K - 1) * 3 * HW)],
        scratch_shapes=[pltpu.VMEM((HW, B), F32), pltpu.VMEM((HW, B), F32), pltpu.VMEM((HW, B), F32)],
        compiler_params=pltpu.CompilerParams(vmem_limit_bytes=VMEM_LIMIT),
    )(zc, zg, gates, buf, conv_w, alog, dtb, jnp.tile(norm, HEADS).reshape(1, HW), s_t)


IN_SEGMENTS = (4 * HW, S5_WIDTH, 3 * HW, HW, GATE_PAD)


def _tile(n, pref):
    return pref if n % pref == 0 else n


def prep_layer_weights(lp):
    w = lp['w_in']
    a, g4 = 4 * HW, HEADS
    o_u = a + 2 * g4
    o_c = o_u + S5_WIDTH
    o_g = o_c + 3 * HW
    o_b = o_g + HW
    gate_cols = jnp.concatenate([w[:, a:a + 2 * g4], w[:, o_b:o_b + 2 * g4],
                                 jnp.zeros((w.shape[0], GATE_PAD - 4 * g4), w.dtype)], axis=1)
    w_in = jnp.concatenate([w[:, 0:a], w[:, o_u:o_c], w[:, o_c:o_g], w[:, o_g:o_b], gate_cols], axis=1)
    bf = lambda n: lp[n].astype(BF16)
    return {'w_in': w_in.astype(BF16), 'w_out': bf('w_out'), 'w_mq': bf('w_mq'), 'w_mo': bf('w_mo'),
            'w_mkv': jnp.concatenate([lp['w_mk'], lp['w_mv']], axis=1).astype(BF16),
            'w_gate': bf('w_gate'), 'w_up': bf('w_up'), 'w_down': bf('w_down'), 's5': s5_params(lp)}


def mixer_prompt(x, lp, W, B, S):
    T = B * S
    tm = _tile(T, 512)
    za, zu, zc, zg, gates = norm_matmul(x, lp['norm_mix'], W['w_in'], IN_SEGMENTS, tm)
    tt = _tile(S, 512)
    ha, c1, n1, m1 = mlstm_prompt(za, gates, lp['mlstm_b_i'], lp['mlstm_b_f'], lp['mlstm_norm'], B, S, tt)
    h0 = jnp.zeros((B, S5_STATE), F32)
    ys3, r1, i1 = s5_mixer(zu.reshape(B, S, S5_WIDTH), h0, h0, W['s5'], _tile(S, 128), False)
    ys = ys3.reshape(T, S5_WIDTH)
    hc, s1, buf1 = gdn_prompt(zc, zg, gates, lp['gdn_conv_w'], lp['gdn_a_log'], lp['gdn_dt_bias'], lp['gdn_norm'], B, S, tt)
    x1 = matmul_residual(x, [ha, ys, hc], W['w_out'], tm)
    return x1, (c1, n1, m1, r1.reshape(B, S5_GROUPS, S5_N), i1.reshape(B, S5_GROUPS, S5_N), s1, buf1)


def mixer_sample(x, st, lp, W):
    B = x.shape[0]
    c0, n0, m0, r0, i0, s0, buf0 = st
    za, zu, zc, zg, gates = norm_matmul(x, lp['norm_mix'], W['w_in'], IN_SEGMENTS, B)
    m_t = jnp.zeros((SUBLANES, B), F32).at[0:HEADS, :].set(m0.T)
    ha, c1t, n1t, m1t = mlstm_sample(za, gates, lp['mlstm_b_i'], lp['mlstm_b_f'], lp['mlstm_norm'],
                                     c0.reshape(B, HW * DH).T, n0.reshape(B, HW).T, m_t)
    ys3, r1, i1 = s5_mixer(zu.reshape(1, B, S5_WIDTH), r0.reshape(B, S5_STATE), i0.reshape(B, S5_STATE), W['s5'], 1, True)
    hc, s1t, buf1 = gdn_sample(zc, zg, gates, buf0.reshape(B, (CONV_K - 1) * 3 * HW), lp['gdn_conv_w'],
                               lp['gdn_a_log'], lp['gdn_dt_bias'], lp['gdn_norm'], s0.reshape(B, HW * DH).T)
    x1 = matmul_residual(x, [ha, ys3.reshape(B, S5_WIDTH), hc], W['w_out'], B)
    return x1, (c1t.T.reshape(B, HEADS, DH, DH), n1t.T.reshape(B, HEADS, DH), m1t[0:HEADS, :].T,
                r1.reshape(B, S5_GROUPS, S5_N), i1.reshape(B, S5_GROUPS, S5_N),
                s1t.T.reshape(B, HEADS, DH, DH), buf1.reshape(B, CONV_K - 1, 3 * HW))


def mem_kv(mem, lp, W):
    D = mem.shape[1]
    return norm_matmul(mem, lp['norm_mem'], W['w_mkv'], (D, D), _tile(mem.shape[0], 512))


def xattn_ffn_prompt(x, mk, mv, lp, W, S, norm_final, final):
    T, D = x.shape
    tm = _tile(T, 512)
    (q,) = norm_matmul(x, lp['norm_xattn'], W['w_mq'], (D,), tm)
    o = xattn_prompt(q, mk, mv, S, _tile(S, 512))
    x2 = matmul_residual(x, [o], W['w_mo'], tm)
    return ffn(x2, lp['norm_ffn'], W['w_gate'], W['w_up'], W['w_down'], norm_final, final, _tile(T, 1024), 256)


def xattn_ffn_sample(x, ck, cv, layer, lp, W, norm_final, final):
    B, D = x.shape
    (q,) = norm_matmul(x, lp['norm_xattn'], W['w_mq'], (D,), B)
    o = xattn_sample(q, ck, cv, layer, 4)
    x2 = matmul_residual(x, [o], W['w_mo'], B)
    return ffn(x2, lp['norm_ffn'], W['w_gate'], W['w_up'], W['w_down'], norm_final, final, B, 256)


LAYER_PARAMS = ('norm_mix', 'w_in', 'w_out', 'mlstm_b_i', 'mlstm_b_f', 'mlstm_norm', 's5_a_re', 's5_a_im', 's5_log_dt',
                's5_b_re', 's5_b_im', 's5_c_re', 's5_c_im', 's5_d', 's5_w_glu', 'gdn_conv_w', 'gdn_a_log',
                'gdn_dt_bias', 'gdn_norm', 'norm_xattn', 'norm_mem', 'w_mq', 'w_mk', 'w_mv', 'w_mo', 'norm_ffn',
                'w_gate', 'w_up', 'w_down')


def kernel(x_prompt, x_sample, mem_prompt, cache_mem_k, cache_mem_v, state_mlstm_c, state_mlstm_n, state_mlstm_m, state_s5_re, state_s5_im, state_gdn, state_gdn_conv, norm_mix, w_in, w_out, mlstm_b_i, mlstm_b_f, mlstm_norm, s5_a_re, s5_a_im, s5_log_dt, s5_b_re, s5_b_im, s5_c_re, s5_c_im, s5_d, s5_w_glu, gdn_conv_w, gdn_a_log, gdn_dt_bias, gdn_norm, norm_xattn, norm_mem, w_mq, w_mk, w_mv, w_mo, norm_ffn, w_gate, w_up, w_down, norm_final):
    stacked = dict(norm_mix=norm_mix, w_in=w_in, w_out=w_out, mlstm_b_i=mlstm_b_i, mlstm_b_f=mlstm_b_f,
                   mlstm_norm=mlstm_norm, s5_a_re=s5_a_re, s5_a_im=s5_a_im, s5_log_dt=s5_log_dt, s5_b_re=s5_b_re,
                   s5_b_im=s5_b_im, s5_c_re=s5_c_re, s5_c_im=s5_c_im, s5_d=s5_d, s5_w_glu=s5_w_glu,
                   gdn_conv_w=gdn_conv_w, gdn_a_log=gdn_a_log, gdn_dt_bias=gdn_dt_bias, gdn_norm=gdn_norm,
                   norm_xattn=norm_xattn, norm_mem=norm_mem, w_mq=w_mq, w_mk=w_mk, w_mv=w_mv, w_mo=w_mo,
                   norm_ffn=norm_ffn, w_gate=w_gate, w_up=w_up, w_down=w_down)
    B, S, D = x_prompt.shape
    Bs = x_sample.shape[0]
    M = mem_prompt.shape[1]
    depth = w_in.shape[0]
    xp = x_prompt.reshape(B * S, D)
    xs = x_sample.reshape(Bs, D)
    mem = mem_prompt.reshape(B * M, D)
    cache_k = cache_mem_k.reshape(depth, Bs, M, D)
    cache_v = cache_mem_v.reshape(depth, Bs, M, D)
    mem_k, mem_v, st_p, st_s = [], [], [], []
    for l in range(depth):
        lp = {n: stacked[n][l] for n in LAYER_PARAMS}
        W = prep_layer_weights(lp)
        last = l == depth - 1
        xp, sp = mixer_prompt(xp, lp, W, B, S)
        mk, mv = mem_kv(mem, lp, W)
        xp = xattn_ffn_prompt(xp, mk.reshape(B, M, D), mv.reshape(B, M, D), lp, W, S, norm_final, last)
        st_in = (state_mlstm_c[l], state_mlstm_n[l], state_mlstm_m[l], state_s5_re[l], state_s5_im[l],
                 state_gdn[l], state_gdn_conv[l])
        xs, ss = mixer_sample(xs, st_in, lp, W)
        xs = xattn_ffn_sample(xs, cache_k, cache_v, l, lp, W, norm_final, last)
        mem_k.append(mk.reshape(B, M, X_HEADS, D // X_HEADS))
        mem_v.append(mv.reshape(B, M, X_HEADS, D // X_HEADS))
        st_p.append(sp)
        st_s.append(ss)
    stack = lambda lst: [jnp.stack([st[i] for st in lst]) for i in range(7)]
    return (xp.reshape(B, S, D), xs.reshape(Bs, 1, D), jnp.stack(mem_k), jnp.stack(mem_v),
            *stack(st_p), *stack(st_s))
```

```python
import functools
import math

import jax
import jax.numpy as jnp
from jax import lax
from jax.experimental import pallas as pl
from jax.experimental.pallas import tpu as pltpu

F32 = jnp.float32
BF16 = jnp.bfloat16
EPS = 1e-6

HEADS = 4
DH = 64
HW = HEADS * DH
CHUNK = 64
S5_P = 16
S5_N = 64
S5_GROUPS = 32
S5_WIDTH = S5_GROUPS * S5_P
S5_STATE = S5_GROUPS * S5_N
S5_GB = 8
S5_NBLK = S5_GROUPS // S5_GB
CONV_K = 4
X_HEADS = 4
GATE_PAD = 128
LANES = 128
SUBLANES = 8
VMEM_LIMIT = 48 * 1024 * 1024


def _cparams(*sem):
    return pltpu.CompilerParams(dimension_semantics=sem, vmem_limit_bytes=VMEM_LIMIT)


def _rms(x, g_row):
    return x * lax.rsqrt(jnp.mean(x * x, axis=-1, keepdims=True) + EPS) * g_row


def _dot(a, b):
    return jnp.dot(a, b, preferred_element_type=F32)


def _dot_nt(a, b):
    return lax.dot_general(a, b, (((1,), (1,)), ((), ())), preferred_element_type=F32)


def _sigmoid(x):
    return 1.0 / (1.0 + jnp.exp(-x))


def _silu(x):
    return x * _sigmoid(x)


def _softplus(x):
    return jnp.maximum(x, 0.0) + jnp.log1p(jnp.exp(-jnp.abs(x)))


def _log_sigmoid(x):
    return jnp.minimum(x, 0.0) - jnp.log1p(jnp.exp(-jnp.abs(x)))


def _norm_matmul_kernel(x_ref, g_ref, w_ref, *out_refs, splits):
    xn = _rms(x_ref[...], g_ref[...]).astype(BF16)
    off = 0
    for o_ref, n in zip(out_refs, splits):
        o_ref[...] = _dot(xn, w_ref[:, off:off + n])
        off += n


def norm_matmul(x, g, w_bf16, splits, tm):
    T, D = x.shape
    N = w_bf16.shape[1]
    assert sum(splits) == N and T % tm == 0
    return pl.pallas_call(
        functools.partial(_norm_matmul_kernel, splits=tuple(splits)),
        grid=(T // tm,),
        in_specs=[pl.BlockSpec((tm, D), lambda i: (i, 0)),
                  pl.BlockSpec((1, D), lambda i: (0, 0)),
                  pl.BlockSpec((D, N), lambda i: (0, 0))],
        out_specs=[pl.BlockSpec((tm, n), lambda i: (i, 0)) for n in splits],
        out_shape=[jax.ShapeDtypeStruct((T, n), F32) for n in splits],
        compiler_params=_cparams("parallel"),
    )(x, g.reshape(1, D), w_bf16)


def _matmul_residual_kernel(x_ref, *refs, ksplits):
    a_refs, w_ref, o_ref = refs[:-2], refs[-2], refs[-1]
    acc = x_ref[...]
    off = 0
    for a_ref, k in zip(a_refs, ksplits):
        acc = acc + _dot(a_ref[...].astype(BF16), w_ref[off:off + k, :])
        off += k
    o_ref[...] = acc


def matmul_residual(x, acts, w_bf16, tm):
    T, D = x.shape
    ks = tuple(a.shape[1] for a in acts)
    K = w_bf16.shape[0]
    assert sum(ks) == K and T % tm == 0
    return pl.pallas_call(
        functools.partial(_matmul_residual_kernel, ksplits=ks),
        grid=(T // tm,),
        in_specs=[pl.BlockSpec((tm, D), lambda i: (i, 0))]
                 + [pl.BlockSpec((tm, k), lambda i: (i, 0)) for k in ks]
                 + [pl.BlockSpec((K, D), lambda i: (0, 0))],
        out_specs=pl.BlockSpec((tm, D), lambda i: (i, 0)),
        out_shape=jax.ShapeDtypeStruct((T, D), F32),
        compiler_params=_cparams("parallel"),
    )(x, *acts, w_bf16)


def _ffn_kernel(x_ref, g_ref, wg_ref, wu_ref, wd_ref, gf_ref, o_ref, hn_scr, acc_scr, *, final_norm):
    j = pl.program_id(1)

    @pl.when(j == 0)
    def _():
        x = x_ref[...]
        hn_scr[...] = _rms(x, g_ref[...]).astype(BF16)
        acc_scr[...] = x

    h = hn_scr[...]
    a = _dot(h, wg_ref[...])
    b = _dot(h, wu_ref[...])
    acc_scr[...] += _dot((_silu(a) * b).astype(BF16), wd_ref[...])

    @pl.when(j == pl.num_programs(1) - 1)
    def _():
        y = acc_scr[...]
        if final_norm:
            y = _rms(y, gf_ref[...])
        o_ref[...] = y


def ffn(x, g, wg, wu, wd, g_final, final_norm, tm, tf):
    T, D = x.shape
    F = wg.shape[1]
    assert T % tm == 0 and F % tf == 0
    return pl.pallas_call(
        functools.partial(_ffn_kernel, final_norm=final_norm),
        grid=(T // tm, F // tf),
        in_specs=[pl.BlockSpec((tm, D), lambda i, j: (i, 0)),
                  pl.BlockSpec((1, D), lambda i, j: (0, 0)),
                  pl.BlockSpec((D, tf), lambda i, j: (0, j)),
                  pl.BlockSpec((D, tf), lambda i, j: (0, j)),
                  pl.BlockSpec((tf, D), lambda i, j: (j, 0)),
                  pl.BlockSpec((1, D), lambda i, j: (0, 0))],
        out_specs=pl.BlockSpec((tm, D), lambda i, j: (i, 0)),
        out_shape=jax.ShapeDtypeStruct((T, D), F32),
        scratch_shapes=[pltpu.VMEM((tm, D), BF16), pltpu.VMEM((tm, D), F32)],
        compiler_params=_cparams("parallel", "arbitrary"),
    )(x, g.reshape(1, D), wg, wu, wd, g_final.reshape(1, D))


def _softmax_rows(s):
    e = jnp.exp(s - jnp.max(s, axis=-1, keepdims=True))
    return e / jnp.sum(e, axis=-1, keepdims=True)


def _xattn_prompt_kernel(q_ref, k_ref, v_ref, o_ref, *, dh):
    scale = dh ** -0.5
    for h in range(X_HEADS):
        sl = slice(h * dh, (h + 1) * dh)
        s = _dot_nt(q_ref[:, sl].astype(BF16), k_ref[0, :, sl].astype(BF16)) * scale
        o_ref[:, sl] = _dot(_softmax_rows(s).astype(BF16), v_ref[0, :, sl].astype(BF16))


def xattn_prompt(q, mk, mv, seq, tq):
    T, D = q.shape
    B, M, _ = mk.shape
    nt = seq // tq
    return pl.pallas_call(
        functools.partial(_xattn_prompt_kernel, dh=D // X_HEADS),
        grid=(B, nt),
        in_specs=[pl.BlockSpec((tq, D), lambda b, t: (b * nt + t, 0)),
                  pl.BlockSpec((1, M, D), lambda b, t: (b, 0, 0)),
                  pl.BlockSpec((1, M, D), lambda b, t: (b, 0, 0))],
        out_specs=pl.BlockSpec((tq, D), lambda b, t: (b * nt + t, 0)),
        out_shape=jax.ShapeDtypeStruct((T, D), F32),
        compiler_params=_cparams("parallel", "parallel"),
    )(q, mk, mv)


def _xattn_sample_kernel(q_ref, k_ref, v_ref, o_ref, *, sb):
    M, H, dh = k_ref.shape[1:]
    scale = dh ** -0.5
    row = lax.broadcasted_iota(jnp.int32, (SUBLANES, M * H), 0)
    col_head = lax.broadcasted_iota(jnp.int32, (SUBLANES, M * H), 1) % H
    own = (row % H) == col_head
    pad = jnp.zeros((SUBLANES - H, dh), F32)
    for i in range(sb):
        q8 = jnp.concatenate([q_ref[i], pad], axis=0).astype(BF16)
        s = _dot_nt(q8, k_ref[i].reshape(M * H, dh).astype(BF16)) * scale
        p = _softmax_rows(jnp.where(own, s, -jnp.inf))
        o_ref[i] = _dot(p.astype(BF16), v_ref[i].reshape(M * H, dh).astype(BF16))[0:H]


def xattn_sample(q, ck, cv, layer, sb):
    B, D = q.shape
    _, _, M, H, dh = ck.shape
    out = pl.pallas_call(
        functools.partial(_xattn_sample_kernel, sb=sb),
        grid=(B // sb,),
        in_specs=[pl.BlockSpec((sb, H, dh), lambda i: (i, 0, 0)),
                  pl.BlockSpec((None, sb, M, H, dh), lambda i: (layer, i, 0, 0, 0)),
                  pl.BlockSpec((None, sb, M, H, dh), lambda i: (layer, i, 0, 0, 0))],
        out_specs=pl.BlockSpec((sb, H, dh), lambda i: (i, 0, 0)),
        out_shape=jax.ShapeDtypeStruct((B, H, dh), F32),
        compiler_params=_cparams("parallel"),
    )(q.reshape(B, H, dh), ck, cv)
    return out.reshape(B, D)


def _lane_cat_masks(L):
    row = lax.broadcasted_iota(jnp.int32, (L, HW), 0)
    j = lax.broadcasted_iota(jnp.int32, (L, HW), 1) % DH
    r2 = lax.broadcasted_iota(jnp.int32, (HW, HW), 0) // DH
    c2 = lax.broadcasted_iota(jnp.int32, (HW, HW), 1) // DH
    return row >= j, row > j, row == j, r2 == c2


def _expand_bd(x, bd):
    return jnp.where(bd, jnp.concatenate([x] * HEADS, axis=0), jnp.zeros((), x.dtype))


def _seg_reduce(x, op, fill):
    lo = lax.broadcasted_iota(jnp.int32, (x.shape[0], LANES), 1) < DH
    parts = []
    for c in range(HW // LANES):
        xh = x[:, c * LANES:(c + 1) * LANES]
        a = op(jnp.where(lo, xh, fill), axis=-1, keepdims=True)
        b = op(jnp.where(lo, fill, xh), axis=-1, keepdims=True)
        parts.append(jnp.where(lo, a, b))
    return jnp.concatenate(parts, axis=-1)


def _head_expander(first_lane):
    r = lax.broadcasted_iota(jnp.int32, (GATE_PAD, HW), 0)
    c = lax.broadcasted_iota(jnp.int32, (GATE_PAD, HW), 1) // DH
    return (r == c + first_lane).astype(BF16)


def _chunk_tril(tt):
    r = lax.broadcasted_iota(jnp.int32, (tt, tt), 0)
    c = lax.broadcasted_iota(jnp.int32, (tt, tt), 1)
    return ((r // CHUNK == c // CHUNK) & (r >= c)).astype(BF16)


def _split3(x):
    hi = x.astype(BF16)
    r = x - hi.astype(F32)
    mid = r.astype(BF16)
    return hi, mid, (r - mid.astype(F32)).astype(BF16)


def _dot_sel_r(x, sel):
    hi, mid, lo = _split3(x)
    return (_dot(lo, sel) + _dot(mid, sel)) + _dot(hi, sel)


def _dot_sel_l(sel, x):
    hi, mid, lo = _split3(x)
    return (_dot(sel, lo) + _dot(sel, mid)) + _dot(sel, hi)


def _col_to_row(x, eye):
    return jnp.sum(jnp.where(eye, x, 0.0), axis=0, keepdims=True)


def _head_rms(x, ones_bd, g_row):
    ms = _dot_sel_r(x * x, ones_bd) * (1.0 / DH)
    return x * lax.rsqrt(ms + EPS) * g_row


def _mlstm_prompt_kernel(za_ref, gt_ref, bias_ref, norm_ref, ha_ref, c_ref, n_ref, m_ref,
                         c_scr, n_scr, m_scr, ix_scr, bx_scr, *, nchunks, unroll):
    tb = pl.program_id(1)
    L = CHUNK
    tril, _, eye, bd = _lane_cat_masks(L)
    ones_bd = bd.astype(BF16)

    @pl.when(tb == 0)
    def _():
        c_scr[...] = jnp.zeros_like(c_scr)
        n_scr[...] = jnp.zeros_like(n_scr)
        m_scr[...] = jnp.zeros_like(m_scr)

    gt = gt_ref[...] + bias_ref[...]
    b_cols = _dot_sel_l(_chunk_tril(nchunks * L), _log_sigmoid(gt))
    ix_scr[...] = _dot_sel_r(gt, _head_expander(0))
    bx_scr[...] = _dot_sel_r(b_cols, _head_expander(HEADS))

    def chunk(ci, carry):
        c_bd, n_row, m_x = carry
        rows = pl.ds(pl.multiple_of(ci * L, L), L)
        q = (za_ref[rows, 0:HW] * (DH ** -0.5)).astype(BF16)
        k = za_ref[rows, HW:2 * HW]
        v = za_ref[rows, 2 * HW:3 * HW].astype(BF16)
        og = za_ref[rows, 3 * HW:4 * HW]
        i_x = ix_scr[rows, :]
        b_x = bx_scr[rows, :]

        d_intra = jnp.where(tril, b_x - _col_to_row(b_x, eye) + _col_to_row(i_x, eye), -jnp.inf)
        d_inter = b_x + m_x
        m_t = jnp.maximum(_seg_reduce(d_intra, jnp.max, -jnp.inf), d_inter)
        w_intra = jnp.exp(d_intra - m_t)
        w_inter = jnp.exp(d_inter - m_t)
        kb = k.astype(BF16)
        s = _dot_nt(q, _expand_bd(kb, bd)) * w_intra
        num = w_inter * _dot(q, c_bd.astype(BF16)) + _dot(s.astype(BF16), _expand_bd(v, bd))
        qn = _dot((q.astype(F32) * n_row).astype(BF16), ones_bd)
        den = w_inter * qn + _seg_reduce(s, jnp.sum, 0.0)
        hh = num / jnp.maximum(jnp.abs(den), jnp.exp(-m_t))
        ha_ref[rows, :] = _head_rms(hh, ones_bd, norm_ref[...]) * _sigmoid(og)

        b_last = b_x[L - 1:L, :]
        g_x = b_last - b_x + i_x
        m_new = jnp.maximum(b_last + m_x, jnp.max(g_x, axis=0, keepdims=True))
        kw = k * jnp.exp(g_x - m_new)
        decay = jnp.exp(b_last + m_x - m_new)
        c_new = decay * c_bd + jnp.where(bd, _dot(kw.T.astype(BF16), v), 0.0)
        return c_new, decay * n_row + jnp.sum(kw, axis=0, keepdims=True), m_new

    c_bd, n_row, m_x = lax.fori_loop(0, nchunks, chunk, (c_scr[...], n_scr[...], m_scr[...]), unroll=unroll)
    c_scr[...] = c_bd
    n_scr[...] = n_row
    m_scr[...] = m_x

    @pl.when(tb == pl.num_programs(1) - 1)
    def _():
        for h in range(HEADS):
            c_ref[0, h] = c_scr[h * DH:(h + 1) * DH, h * DH:(h + 1) * DH]
        n_ref[0] = n_scr[...]
        m_ref[0] = m_scr[...]


def mlstm_prompt(za, gates, b_i, b_f, norm, B, S, tt):
    assert S % tt == 0 and tt % CHUNK == 0
    nt = S // tt
    bias = jnp.zeros((1, GATE_PAD), F32).at[0, 0:HEADS].set(b_i).at[0, HEADS:2 * HEADS].set(b_f)
    ha, c, n, m = pl.pallas_call(
        functools.partial(_mlstm_prompt_kernel, nchunks=tt // CHUNK, unroll=4),
        grid=(B, nt),
        in_specs=[pl.BlockSpec((tt, 4 * HW), lambda b, t: (b * nt + t, 0)),
                  pl.BlockSpec((tt, GATE_PAD), lambda b, t: (b * nt + t, 0)),
                  pl.BlockSpec((1, GATE_PAD), lambda b, t: (0, 0)),
                  pl.BlockSpec((1, HW), lambda b, t: (0, 0))],
        out_specs=[pl.BlockSpec((tt, HW), lambda b, t: (b * nt + t, 0)),
                   pl.BlockSpec((1, HEADS, DH, DH), lambda b, t: (b, 0, 0, 0)),
                   pl.BlockSpec((1, 1, HW), lambda b, t: (b, 0, 0)),
                   pl.BlockSpec((1, 1, HW), lambda b, t: (b, 0, 0))],
        out_shape=[jax.ShapeDtypeStruct((B * S, HW), F32),
                   jax.ShapeDtypeStruct((B, HEADS, DH, DH), F32),
                   jax.ShapeDtypeStruct((B, 1, HW), F32),
                   jax.ShapeDtypeStruct((B, 1, HW), F32)],
        scratch_shapes=[pltpu.VMEM((HW, HW), F32), pltpu.VMEM((1, HW), F32), pltpu.VMEM((1, HW), F32),
                        pltpu.VMEM((tt, HW), F32), pltpu.VMEM((tt, HW), F32)],
        compiler_params=_cparams("parallel", "arbitrary"),
    )(za, gates, bias, norm.reshape(1, HW))
    return ha, c, n.reshape(B, HEADS, DH), m[:, 0, ::DH]


def _gdn_prompt_kernel(zc_ref, zg_ref, gt_ref, cw_ref, alog_ref, dtb_ref, norm_ref,
                       hc_ref, s_ref, tail_ref,
                       s_scr, xp_scr, q_scr, k_scr, v_scr, beta_scr, g_scr, uv_scr, wq_scr, qkm_scr, kwt_scr,
                       *, nchunks, group, unroll_b):
    tb = pl.program_id(1)
    L = CHUNK
    tt = nchunks * L
    pad = SUBLANES
    tril, strict, eye, bd = _lane_cat_masks(L)
    ones_bd = bd.astype(BF16)

    @pl.when(tb == 0)
    def _():
        s_scr[...] = jnp.zeros_like(s_scr)
        xp_scr[0:pad, :] = jnp.zeros((pad, 3 * HW), F32)

    x = zc_ref[...]
    xp_scr[pad:pad + tt, :] = x
    y = cw_ref[CONV_K - 1:CONV_K, :] * x
    for j in range(CONV_K - 1):
        y = y + cw_ref[j:j + 1, :] * xp_scr[pl.ds(pad - (CONV_K - 1) + j, tt), :]
    xp_scr[0:pad, :] = x[tt - pad:tt, :]
    y = _silu(y)
    q_raw, k_raw = y[:, 0:HW], y[:, HW:2 * HW]
    q_scr[...] = (q_raw * lax.rsqrt(_dot_sel_r(q_raw * q_raw, ones_bd) + EPS) * (DH ** -0.5)).astype(BF16)
    k_scr[...] = k_raw * lax.rsqrt(_dot_sel_r(k_raw * k_raw, ones_bd) + EPS)
    v_scr[...] = y[:, 2 * HW:3 * HW]
    gt = gt_ref[...]
    beta_scr[...] = _dot_sel_r(_sigmoid(gt), _head_expander(2 * HEADS))
    la_cols = -jnp.exp(alog_ref[...]) * _softplus(gt + dtb_ref[...])
    g_scr[...] = _dot_sel_r(_dot_sel_l(_chunk_tril(tt), la_cols), _head_expander(3 * HEADS))

    def prepare(gi, carry):
        cis = [gi * group + j for j in range(group)]
        rows = [pl.ds(pl.multiple_of(ci * L, L), L) for ci in cis]
        k = [k_scr[r, :] for r in rows]
        g_x = [g_scr[r, :] for r in rows]
        beta_row = [_col_to_row(beta_scr[r, :], eye) for r in rows]
        dec_incl = [jnp.where(tril, jnp.exp(jnp.where(tril, g - _col_to_row(g, eye), 0.0)), 0.0) for g in g_x]
        k_bd = [_expand_bd(kk.astype(BF16), bd) for kk in k]
        n0 = [-(jnp.where(strict, d, 0.0) * _dot_nt(kk.astype(BF16), kbd) * br)
              for d, kk, kbd, br in zip(dec_incl, k, k_bd, beta_row)]
        for r, d, kbd, br in zip(rows, dec_incl, k_bd, beta_row):
            qkm_scr[r, :] = (_dot_nt(q_scr[r, :], kbd) * d * br).astype(BF16)

        p = [_dot(n.astype(BF16), _expand_bd(n.astype(BF16), bd)) for n in n0]
        m = n0
        steps = int(math.log2(L)) - 1
        for i in range(steps):
            p_bd = [_expand_bd(pp.astype(BF16), bd) for pp in p]
            if i < steps - 1:
                pm = [_dot(jnp.concatenate([pp, mm], axis=0).astype(BF16), pbd) for pp, mm, pbd in zip(p, m, p_bd)]
                p_next, mp = [x[0:L] for x in pm], [x[L:2 * L] for x in pm]
            else:
                p_next, mp = None, [_dot(mm.astype(BF16), pbd) for mm, pbd in zip(m, p_bd)]
            m = [mm + pp + x for mm, pp, x in zip(m, p, mp)]
            p = p_next

        for ci, r, kk, g, mm in zip(cis, rows, k, g_x, m):
            v = v_scr[r, :]
            egk = jnp.exp(g) * kk
            rhs_bd = jnp.concatenate([_expand_bd(v.astype(BF16), bd), _expand_bd(egk.astype(BF16), bd)], axis=1)
            mr = _dot(mm.astype(BF16), rhs_bd)
            uv_scr[r, :] = v + mr[:, 0:HW]
            wq_rows = pl.multiple_of(ci * 2 * L, 2 * L)
            wq_scr[pl.ds(wq_rows, L), :] = (egk + mr[:, HW:2 * HW]).astype(BF16)
            wq_scr[pl.ds(wq_rows + L, L), :] = q_scr[r, :]
            kw = kk * (jnp.exp(g[L - 1:L, :] - g) * beta_scr[r, :])
            kwt_scr[pl.ds(pl.multiple_of(ci * HW, HW), HW), :] = kw.T.astype(BF16)
        return carry

    lax.fori_loop(0, nchunks // group, prepare, 0)

    def advance(ci, s_bd):
        rows = pl.ds(pl.multiple_of(ci * L, L), L)
        g_x = g_scr[rows, :]
        wqs = _dot(wq_scr[pl.ds(pl.multiple_of(ci * 2 * L, 2 * L), 2 * L), :], s_bd.astype(BF16))
        ub = (uv_scr[rows, :] - wqs[0:L]).astype(BF16)
        o = jnp.exp(g_x) * wqs[L:2 * L] + _dot(qkm_scr[rows, :], _expand_bd(ub, bd))
        hc_ref[rows, :] = _head_rms(o, ones_bd, norm_ref[...]) * _silu(zg_ref[rows, :])
        kwt = kwt_scr[pl.ds(pl.multiple_of(ci * HW, HW), HW), :]
        return jnp.exp(g_x[L - 1:L, :]) * s_bd + jnp.where(bd, _dot(kwt, ub), 0.0)

    s_scr[...] = lax.fori_loop(0, nchunks, advance, s_scr[...], unroll=unroll_b)

    @pl.when(tb == pl.num_programs(1) - 1)
    def _():
        for h in range(HEADS):
            s_ref[0, h] = s_scr[h * DH:(h + 1) * DH, h * DH:(h + 1) * DH]
        tail_ref[0] = xp_scr[0:pad, :]


def _gdn_gate_rows(a_log, dt_bias):
    z = jnp.zeros((1, GATE_PAD), F32)
    return (z.at[0, 3 * HEADS:4 * HEADS].set(a_log), z.at[0, 3 * HEADS:4 * HEADS].set(dt_bias))


def gdn_prompt(zc, zg, gates, conv_w, a_log, dt_bias, norm, B, S, tt):
    assert S % tt == 0 and tt % CHUNK == 0
    nt = S // tt
    alog, dtb = _gdn_gate_rows(a_log, dt_bias)
    hc, s, tail = pl.pallas_call(
        functools.partial(_gdn_prompt_kernel, nchunks=tt // CHUNK, group=math.gcd(tt // CHUNK, 4), unroll_b=2),
        grid=(B, nt),
        in_specs=[pl.BlockSpec((tt, 3 * HW), lambda b, t: (b * nt + t, 0)),
                  pl.BlockSpec((tt, HW), lambda b, t: (b * nt + t, 0)),
                  pl.BlockSpec((tt, GATE_PAD), lambda b, t: (b * nt + t, 0)),
                  pl.BlockSpec((CONV_K, 3 * HW), lambda b, t: (0, 0)),
                  pl.BlockSpec((1, GATE_PAD), lambda b, t: (0, 0)),
                  pl.BlockSpec((1, GATE_PAD), lambda b, t: (0, 0)),
                  pl.BlockSpec((1, HW), lambda b, t: (0, 0))],
        out_specs=[pl.BlockSpec((tt, HW), lambda b, t: (b * nt + t, 0)),
                   pl.BlockSpec((1, HEADS, DH, DH), lambda b, t: (b, 0, 0, 0)),
                   pl.BlockSpec((1, SUBLANES, 3 * HW), lambda b, t: (b, 0, 0))],
        out_shape=[jax.ShapeDtypeStruct((B * S, HW), F32),
                   jax.ShapeDtypeStruct((B, HEADS, DH, DH), F32),
                   jax.ShapeDtypeStruct((B, SUBLANES, 3 * HW), F32)],
        scratch_shapes=[pltpu.VMEM((HW, HW), F32), pltpu.VMEM((tt + SUBLANES, 3 * HW), F32),
                        pltpu.VMEM((tt, HW), BF16), pltpu.VMEM((tt, HW), F32), pltpu.VMEM((tt, HW), F32),
                        pltpu.VMEM((tt, HW), F32), pltpu.VMEM((tt, HW), F32), pltpu.VMEM((tt, HW), F32),
                        pltpu.VMEM((2 * tt, HW), BF16), pltpu.VMEM((tt, HW), BF16),
                        pltpu.VMEM((tt // CHUNK * HW, CHUNK), BF16)],
        compiler_params=_cparams("parallel", "arbitrary"),
    )(zc, zg, gates, conv_w, alog, dtb, jnp.tile(norm, HEADS).reshape(1, HW))
    return hc, s, tail[:, SUBLANES - (CONV_K - 1):, :]


def _s5_prep_kernel(are_ref, aim_ref, ldt_ref, bre_ref, bim_ref, lre_ref, lim_ref, bbre_ref, bbim_ref):
    a_re, a_im = are_ref[...], aim_ref[...]
    dt = jnp.exp(ldt_ref[...])
    mag = jnp.exp(a_re * dt)
    lam_re, lam_im = mag * jnp.cos(a_im * dt), mag * jnp.sin(a_im * dt)
    lre_ref[...] = lam_re
    lim_ref[...] = lam_im
    nr, ni = lam_re - 1.0, lam_im
    den = a_re * a_re + a_im * a_im
    coef_re = (nr * a_re + ni * a_im) / den
    coef_im = (ni * a_re - nr * a_im) / den
    b_re, b_im = bre_ref[...], bim_ref[...]
    bbre_ref[...] = coef_re * b_re - coef_im * b_im
    bbim_ref[...] = coef_re * b_im + coef_im * b_re


def s5_params(lp):
    G, N, P = S5_GROUPS, S5_N, S5_P
    row = lambda a: a.astype(F32).reshape(1, G * N)
    to_pn = lambda b: jnp.transpose(b.astype(F32), (2, 0, 1)).reshape(P, G * N)
    shp = [jax.ShapeDtypeStruct((1, G * N), F32)] * 2 + [jax.ShapeDtypeStruct((P, G * N), F32)] * 2
    lam_re, lam_im, bb_re, bb_im = pl.pallas_call(_s5_prep_kernel, out_shape=shp)(
        row(lp['s5_a_re']), row(lp['s5_a_im']), row(jnp.repeat(lp['s5_log_dt'][:, None], N, axis=1)),
        to_pn(lp['s5_b_re']), to_pn(lp['s5_b_im']))
    eye = jnp.eye(S5_GB, dtype=F32)

    def w_in_blocks(bb):
        b4 = bb.reshape(P, S5_NBLK, S5_GB, N)
        return jnp.einsum('pbgn,gh->bgphn', b4, eye).reshape(S5_NBLK, S5_GB * P, S5_GB * N)

    def w_out_blocks(c):
        c4 = c.astype(F32).reshape(S5_NBLK, S5_GB, P, N)
        return jnp.einsum('bgpn,gh->bgnhp', c4, eye).reshape(S5_NBLK, S5_GB * N, S5_GB * P)

    w_in = jnp.concatenate([w_in_blocks(bb_re), w_in_blocks(bb_im)], axis=-1).astype(BF16)
    return {'lam_re': lam_re, 'lam_im': lam_im, 'w_in': w_in,
            'w_out_re': w_out_blocks(lp['s5_c_re']).astype(BF16),
            'w_out_im': (-w_out_blocks(lp['s5_c_im'])).astype(BF16),
            'd': lp['s5_d'].astype(F32).reshape(1, S5_WIDTH), 'w_glu': lp['s5_w_glu'].astype(BF16)}


def _s5_kernel(u_ref, h0r_ref, h0i_ref, lamr_ref, lami_ref, win_ref, wor_ref, woi_ref, d_ref, wglu_ref,
               ys_ref, h1r_ref, h1i_ref, hr_scr, hi_scr, br_scr, bi_scr, *, nseq, rows, bb, lane_blk):
    tb = pl.program_id(0)
    nsteps = nseq * rows // bb
    blk_in, blk_st = S5_GB * S5_P, S5_GB * S5_N
    seq_rows = lambda i: pl.ds(i, rows, stride=nseq) if nseq > 1 else slice(None)

    @pl.when(tb == 0)
    def _():
        hr_scr[...] = h0r_ref[...]
        hi_scr[...] = h0i_ref[...]

    tiles_blk = blk_st // LANES

    def load_tiles(scr, c0, n, r):
        return jnp.concatenate([scr[c0 + c, r, :] for c in range(n)], axis=-1)

    def store_tiles(scr, c0, r, val):
        for c in range(val.shape[-1] // LANES):
            scr[c0 + c, r, :] = val[:, c * LANES:(c + 1) * LANES]

    for i in range(nseq):
        ub = u_ref[i].astype(BF16)
        for blk in range(S5_NBLK):
            bu = _dot(ub[:, blk * blk_in:(blk + 1) * blk_in], win_ref[blk])
            store_tiles(br_scr, blk * tiles_blk, seq_rows(i), bu[:, 0:blk_st])
            store_tiles(bi_scr, blk * tiles_blk, seq_rows(i), bu[:, blk_st:2 * blk_st])

    tiles_lb = lane_blk // LANES
    for lb in range(S5_STATE // lane_blk):
        ls = slice(lb * lane_blk, (lb + 1) * lane_blk)
        lr = jnp.broadcast_to(lamr_ref[:, ls], (bb, lane_blk))
        li = jnp.broadcast_to(lami_ref[:, ls], (bb, lane_blk))

        def step(t, carry):
            hr, hi = carry
            r = pl.ds(pl.multiple_of(t * bb, bb), bb)
            nr = lr * hr - li * hi + load_tiles(br_scr, lb * tiles_lb, tiles_lb, r)
            ni = lr * hi + li * hr + load_tiles(bi_scr, lb * tiles_lb, tiles_lb, r)
            store_tiles(br_scr, lb * tiles_lb, r, nr)
            store_tiles(bi_scr, lb * tiles_lb, r, ni)
            return nr, ni

        hr, hi = lax.fori_loop(0, nsteps, step, (hr_scr[:, ls], hi_scr[:, ls]), unroll=min(nsteps, 8))
        hr_scr[:, ls] = hr
        hi_scr[:, ls] = hi

    for i in range(nseq):
        ys = []
        for blk in range(S5_NBLK):
            h_re = load_tiles(br_scr, blk * tiles_blk, tiles_blk, seq_rows(i))
            h_im = load_tiles(bi_scr, blk * tiles_blk, tiles_blk, seq_rows(i))
            ys.append(_dot(h_re.astype(BF16), wor_ref[blk]) + _dot(h_im.astype(BF16), woi_ref[blk]))
        gy = jax.nn.gelu(jnp.concatenate(ys, axis=-1) + d_ref[...] * u_ref[i])
        ys_ref[i] = gy * _sigmoid(_dot(gy.astype(BF16), wglu_ref[...]))

    @pl.when(tb == pl.num_programs(0) - 1)
    def _():
        h1r_ref[...] = hr_scr[...]
        h1i_ref[...] = hi_scr[...]


def s5_mixer(u, h0_re, h0_im, sp, tt, single_step):
    B = h0_re.shape[0]
    nseq, S = (1, 1) if single_step else (B, u.shape[1])
    rows = B if single_step else tt
    assert S % tt == 0 and B % SUBLANES == 0
    assert u.shape == ((1, B, S5_WIDTH) if single_step else (B, S, S5_WIDTH))
    lane_blk = max(LANES, min(S5_STATE, (SUBLANES * SUBLANES * LANES) // B))
    full = lambda shape: pl.BlockSpec(shape, lambda t: (0,) * len(shape))
    return pl.pallas_call(
        functools.partial(_s5_kernel, nseq=nseq, rows=rows, bb=B, lane_blk=lane_blk),
        grid=(S // tt,),
        in_specs=[pl.BlockSpec((nseq, rows, S5_WIDTH), lambda t: (0, t, 0)),
                  full((B, S5_STATE)), full((B, S5_STATE)), full((1, S5_STATE)), full((1, S5_STATE)),
                  full(sp['w_in'].shape), full(sp['w_out_re'].shape), full(sp['w_out_im'].shape),
                  full((1, S5_WIDTH)), full((S5_WIDTH, S5_WIDTH))],
        out_specs=[pl.BlockSpec((nseq, rows, S5_WIDTH), lambda t: (0, t, 0)),
                   full((B, S5_STATE)), full((B, S5_STATE))],
        out_shape=[jax.ShapeDtypeStruct(u.shape, F32),
                   jax.ShapeDtypeStruct((B, S5_STATE), F32), jax.ShapeDtypeStruct((B, S5_STATE), F32)],
        scratch_shapes=[pltpu.VMEM((B, S5_STATE), F32), pltpu.VMEM((B, S5_STATE), F32),
                        pltpu.VMEM((S5_STATE // LANES, nseq * rows, LANES), F32),
                        pltpu.VMEM((S5_STATE // LANES, nseq * rows, LANES), F32)],
        compiler_params=_cparams("arbitrary"),
    )(u, h0_re, h0_im, sp['lam_re'], sp['lam_im'], sp['w_in'], sp['w_out_re'], sp['w_out_im'], sp['d'], sp['w_glu'])


def _ones_bd():
    r = lax.broadcasted_iota(jnp.int32, (HW, HW), 0) // DH
    c = lax.broadcasted_iota(jnp.int32, (HW, HW), 1) // DH
    return (r == c).astype(BF16)


def _mlstm_sample_kernel(za_ref, gt_ref, bias_ref, norm_ref, c_ref, n_ref, m_ref,
                         ha_ref, c_out, n_out, m_out, q_scr, kw_scr, h_scr):
    za = za_ref[...]
    q_scr[...] = (za[:, 0:HW] * (DH ** -0.5)).T
    k_t = za[:, HW:2 * HW].T
    v_t = za[:, 2 * HW:3 * HW].T
    g_t = (gt_ref[...] + bias_ref[...]).T
    m_out[...] = jnp.zeros_like(m_out)
    for h in range(HEADS):
        hs = slice(h * DH, (h + 1) * DH)
        i_h = g_t[h:h + 1, :]
        bm = _log_sigmoid(g_t[HEADS + h:HEADS + h + 1, :]) + m_ref[h:h + 1, :]
        m_t = jnp.maximum(i_h, bm)
        w_in = jnp.exp(i_h - m_t)
        w_st = jnp.exp(bm - m_t)
        q_h, k_h, v_h = q_scr[hs, :], k_t[hs, :], v_t[hs, :]
        s = jnp.sum(q_h * k_h, axis=0, keepdims=True) * w_in
        kw_scr[hs, :] = k_h * w_in

        def body(d, acc):
            r = h * DH + d
            rows = pl.ds(pl.multiple_of(r * DH, DH), DH)
            c_hd = c_ref[rows, :]
            c_out[rows, :] = w_st * c_hd + kw_scr[pl.ds(r, 1), :] * v_h
            return acc + q_scr[pl.ds(r, 1), :] * c_hd

        qc = lax.fori_loop(0, DH, body, jnp.zeros((DH, za.shape[0]), F32), unroll=4)
        n_h = n_ref[hs, :]
        num = w_st * qc + s * v_h
        den = w_st * jnp.sum(q_h * n_h, axis=0, keepdims=True) + s
        h_scr[hs, :] = num / jnp.maximum(jnp.abs(den), jnp.exp(-m_t))
        n_out[hs, :] = w_st * n_h + kw_scr[hs, :]
        m_out[h:h + 1, :] = m_t
    ha_ref[...] = _head_rms(h_scr[...].T, _ones_bd(), norm_ref[...]) * _sigmoid(za[:, 3 * HW:4 * HW])


def mlstm_sample(za, gates, b_i, b_f, norm, c_t, n_t, m_t):
    B = za.shape[0]
    bias = jnp.zeros((1, GATE_PAD), F32).at[0, 0:HEADS].set(b_i).at[0, HEADS:2 * HEADS].set(b_f)
    shp = lambda *s: jax.ShapeDtypeStruct(s, F32)
    return pl.pallas_call(
        _mlstm_sample_kernel,
        out_shape=[shp(B, HW), shp(HW * DH, B), shp(HW, B), shp(SUBLANES, B)],
        scratch_shapes=[pltpu.VMEM((HW, B), F32), pltpu.VMEM((HW, B), F32), pltpu.VMEM((HW, B), F32)],
        compiler_params=pltpu.CompilerParams(vmem_limit_bytes=VMEM_LIMIT),
    )(za, gates, bias, norm.reshape(1, HW), c_t, n_t, m_t)


def _gdn_sample_kernel(zc_ref, zg_ref, gt_ref, buf_ref, cw_ref, alog_ref, dtb_ref, norm_ref, s_ref,
                       hc_ref, s_out, buf_out, q_scr, k_scr, o_scr):
    W3 = 3 * HW
    x = zc_ref[...]
    y = cw_ref[CONV_K - 1:CONV_K, :] * x
    for j in range(CONV_K - 1):
        y = y + cw_ref[j:j + 1, :] * buf_ref[:, j * W3:(j + 1) * W3]
    buf_out[:, 0:(CONV_K - 2) * W3] = buf_ref[:, W3:(CONV_K - 1) * W3]
    buf_out[:, (CONV_K - 2) * W3:(CONV_K - 1) * W3] = x
    y = _silu(y)
    ones_bd = _ones_bd()
    q_raw, k_raw = y[:, 0:HW], y[:, HW:2 * HW]
    q_scr[...] = (q_raw * lax.rsqrt(_dot_sel_r(q_raw * q_raw, ones_bd) + EPS) * (DH ** -0.5)).T
    k_scr[...] = (k_raw * lax.rsqrt(_dot_sel_r(k_raw * k_raw, ones_bd) + EPS)).T
    v_t = y[:, 2 * HW:3 * HW].T
    gt = gt_ref[...]
    beta_t = _sigmoid(gt).T
    la_t = (-jnp.exp(alog_ref[...]) * _softplus(gt + dtb_ref[...])).T
    nb = x.shape[0]
    for h in range(HEADS):
        hs = slice(h * DH, (h + 1) * DH)
        beta = beta_t[2 * HEADS + h:2 * HEADS + h + 1, :]
        eg = jnp.exp(la_t[3 * HEADS + h:3 * HEADS + h + 1, :])
        q_h, k_h, v_h = q_scr[hs, :], k_scr[hs, :], v_t[hs, :]

        def read(d, acc):
            ks, qs = acc
            r = h * DH + d
            s_hd = s_ref[pl.ds(pl.multiple_of(r * DH, DH), DH), :]
            return ks + k_scr[pl.ds(r, 1), :] * s_hd, qs + q_scr[pl.ds(r, 1), :] * s_hd

        zero = jnp.zeros((DH, nb), F32)
        ks, qs = lax.fori_loop(0, DH, read, (zero, zero), unroll=4)
        u = v_h - eg * ks
        o_scr[hs, :] = eg * qs + (jnp.sum(q_h * k_h, axis=0, keepdims=True) * beta) * u

        def write(d, carry):
            r = h * DH + d
            rows = pl.ds(pl.multiple_of(r * DH, DH), DH)
            s_out[rows, :] = eg * s_ref[rows, :] + (beta * k_scr[pl.ds(r, 1), :]) * u
            return carry

        lax.fori_loop(0, DH, write, 0, unroll=4)
    hc_ref[...] = _head_rms(o_scr[...].T, ones_bd, norm_ref[...]) * _silu(zg_ref[...])


def gdn_sample(zc, zg, gates, buf, conv_w, a_log, dt_bias, norm, s_t):
    B = zc.shape[0]
    alog, dtb = _gdn_gate_rows(a_log, dt_bias)
    shp = lambda *s: jax.ShapeDtypeStruct(s, F32)
    return pl.pallas_call(
        _gdn_sample_kernel,
        out_shape=[shp(B, HW), shp(HW * DH, B), shp(B, (CONV_K - 1) * 3 * HW)],
        scratch_shapes=[pltpu.VMEM((HW, B), F32), pltpu.VMEM((HW, B), F32), pltpu.VMEM((HW, B), F32)],
        compiler_params=pltpu.CompilerParams(vmem_limit_bytes=VMEM_LIMIT),
    )(zc, zg, gates, buf, conv_w, alog, dtb, jnp.tile(norm, HEADS).reshape(1, HW), s_t)


IN_SEGMENTS = (4 * HW, S5_WIDTH, 3 * HW, HW, GATE_PAD)


def _tile(n, pref):
    return pref if n % pref == 0 else n


def prep_layer_weights(lp):
    w = lp['w_in']
    a, g4 = 4 * HW, HEADS
    o_u = a + 2 * g4
    o_c = o_u + S5_WIDTH
    o_g = o_c + 3 * HW
    o_b = o_g + HW
    gate_cols = jnp.concatenate([w[:, a:a + 2 * g4], w[:, o_b:o_b + 2 * g4],
                                 jnp.zeros((w.shape[0], GATE_PAD - 4 * g4), w.dtype)], axis=1)
    w_in = jnp.concatenate([w[:, 0:a], w[:, o_u:o_c], w[:, o_c:o_g], w[:, o_g:o_b], gate_cols], axis=1)
    bf = lambda n: lp[n].astype(BF16)
    return {'w_in': w_in.astype(BF16), 'w_out': bf('w_out'), 'w_mq': bf('w_mq'), 'w_mo': bf('w_mo'),
            'w_mkv': jnp.concatenate([lp['w_mk'], lp['w_mv']], axis=1).astype(BF16),
            'w_gate': bf('w_gate'), 'w_up': bf('w_up'), 'w_down': bf('w_down'), 's5': s5_params(lp)}


def mixer_prompt(x, lp, W, B, S):
    T = B * S
    tm = _tile(T, 512)
    za, zu, zc, zg, gates = norm_matmul(x, lp['norm_mix'], W['w_in'], IN_SEGMENTS, tm)
    tt = _tile(S, 512)
    ha, c1, n1, m1 = mlstm_prompt(za, gates, lp['mlstm_b_i'], lp['mlstm_b_f'], lp['mlstm_norm'], B, S, tt)
    h0 = jnp.zeros((B, S5_STATE), F32)
    ys3, r1, i1 = s5_mixer(zu.reshape(B, S, S5_WIDTH), h0, h0, W['s5'], _tile(S, 128), False)
    ys = ys3.reshape(T, S5_WIDTH)
    hc, s1, buf1 = gdn_prompt(zc, zg, gates, lp['gdn_conv_w'], lp['gdn_a_log'], lp['gdn_dt_bias'], lp['gdn_norm'], B, S, tt)
    x1 = matmul_residual(x, [ha, ys, hc], W['w_out'], tm)
    return x1, (c1, n1, m1, r1.reshape(B, S5_GROUPS, S5_N), i1.reshape(B, S5_GROUPS, S5_N), s1, buf1)


def mixer_sample(x, st, lp, W):
    B = x.shape[0]
    c0, n0, m0, r0, i0, s0, buf0 = st
    za, zu, zc, zg, gates = norm_matmul(x, lp['norm_mix'], W['w_in'], IN_SEGMENTS, B)
    m_t = jnp.zeros((SUBLANES, B), F32).at[0:HEADS, :].set(m0.T)
    ha, c1t, n1t, m1t = mlstm_sample(za, gates, lp['mlstm_b_i'], lp['mlstm_b_f'], lp['mlstm_norm'],
                                     c0.reshape(B, HW * DH).T, n0.reshape(B, HW).T, m_t)
    ys3, r1, i1 = s5_mixer(zu.reshape(1, B, S5_WIDTH), r0.reshape(B, S5_STATE), i0.reshape(B, S5_STATE), W['s5'], 1, True)
    hc, s1t, buf1 = gdn_sample(zc, zg, gates, buf0.reshape(B, (CONV_K - 1) * 3 * HW), lp['gdn_conv_w'],
                               lp['gdn_a_log'], lp['gdn_dt_bias'], lp['gdn_norm'], s0.reshape(B, HW * DH).T)
    x1 = matmul_residual(x, [ha, ys3.reshape(B, S5_WIDTH), hc], W['w_out'], B)
    return x1, (c1t.T.reshape(B, HEADS, DH, DH), n1t.T.reshape(B, HEADS, DH), m1t[0:HEADS, :].T,
                r1.reshape(B, S5_GROUPS, S5_N), i1.reshape(B, S5_GROUPS, S5_N),
                s1t.T.reshape(B, HEADS, DH, DH), buf1.reshape(B, CONV_K - 1, 3 * HW))


def mem_kv(mem, lp, W):
    D = mem.shape[1]
    return norm_matmul(mem, lp['norm_mem'], W['w_mkv'], (D, D), _tile(mem.shape[0], 512))


def xattn_ffn_prompt(x, mk, mv, lp, W, S, norm_final, final):
    T, D = x.shape
    tm = _tile(T, 512)
    (q,) = norm_matmul(x, lp['norm_xattn'], W['w_mq'], (D,), tm)
    o = xattn_prompt(q, mk, mv, S, _tile(S, 512))
    x2 = matmul_residual(x, [o], W['w_mo'], tm)
    return ffn(x2, lp['norm_ffn'], W['w_gate'], W['w_up'], W['w_down'], norm_final, final, _tile(T, 1024), 256)


def xattn_ffn_sample(x, ck, cv, layer, lp, W, norm_final, final):
    B, D = x.shape
    (q,) = norm_matmul(x, lp['norm_xattn'], W['w_mq'], (D,), B)
    o = xattn_sample(q, ck, cv, layer, 4)
    x2 = matmul_residual(x, [o], W['w_mo'], B)
    return ffn(x2, lp['norm_ffn'], W['w_gate'], W['w_up'], W['w_down'], norm_final, final, B, 256)


LAYER_PARAMS = ('norm_mix', 'w_in', 'w_out', 'mlstm_b_i', 'mlstm_b_f', 'mlstm_norm', 's5_a_re', 's5_a_im', 's5_log_dt',
                's5_b_re', 's5_b_im', 's5_c_re', 's5_c_im', 's5_d', 's5_w_glu', 'gdn_conv_w', 'gdn_a_log',
                'gdn_dt_bias', 'gdn_norm', 'norm_xattn', 'norm_mem', 'w_mq', 'w_mk', 'w_mv', 'w_mo', 'norm_ffn',
                'w_gate', 'w_up', 'w_down')


def kernel(x_prompt, x_sample, mem_prompt, cache_mem_k, cache_mem_v, state_mlstm_c, state_mlstm_n, state_mlstm_m, state_s5_re, state_s5_im, state_gdn, state_gdn_conv, norm_mix, w_in, w_out, mlstm_b_i, mlstm_b_f, mlstm_norm, s5_a_re, s5_a_im, s5_log_dt, s5_b_re, s5_b_im, s5_c_re, s5_c_im, s5_d, s5_w_glu, gdn_conv_w, gdn_a_log, gdn_dt_bias, gdn_norm, norm_xattn, norm_mem, w_mq, w_mk, w_mv, w_mo, norm_ffn, w_gate, w_up, w_down, norm_final):
    stacked = dict(norm_mix=norm_mix, w_in=w_in, w_out=w_out, mlstm_b_i=mlstm_b_i, mlstm_b_f=mlstm_b_f,
                   mlstm_norm=mlstm_norm, s5_a_re=s5_a_re, s5_a_im=s5_a_im, s5_log_dt=s5_log_dt, s5_b_re=s5_b_re,
                   s5_b_im=s5_b_im, s5_c_re=s5_c_re, s5_c_im=s5_c_im, s5_d=s5_d, s5_w_glu=s5_w_glu,
                   gdn_conv_w=gdn_conv_w, gdn_a_log=gdn_a_log, gdn_dt_bias=gdn_dt_bias, gdn_norm=gdn_norm,
                   norm_xattn=norm_xattn, norm_mem=norm_mem, w_mq=w_mq, w_mk=w_mk, w_mv=w_mv, w_mo=w_mo,
                   norm_ffn=norm_ffn, w_gate=w_gate, w_up=w_up, w_down=w_down)
    B, S, D = x_prompt.shape
    Bs = x_sample.shape[0]
    M = mem_prompt.shape[1]
    depth = w_in.shape[0]
    xp = x_prompt.reshape(B * S, D)
    xs = x_sample.reshape(Bs, D)
    mem = mem_prompt.reshape(B * M, D)
    cache_k, cache_v = cache_mem_k, cache_mem_v
    mem_k, mem_v, st_p, st_s = [], [], [], []
    for l in range(depth):
        lp = {n: stacked[n][l] for n in LAYER_PARAMS}
        W = prep_layer_weights(lp)
        last = l == depth - 1
        xp, sp = mixer_prompt(xp, lp, W, B, S)
        mk, mv = mem_kv(mem, lp, W)
        xp = xattn_ffn_prompt(xp, mk.reshape(B, M, D), mv.reshape(B, M, D), lp, W, S, norm_final, last)
        st_in = (state_mlstm_c[l], state_mlstm_n[l], state_mlstm_m[l], state_s5_re[l], state_s5_im[l],
                 state_gdn[l], state_gdn_conv[l])
        xs, ss = mixer_sample(xs, st_in, lp, W)
        xs = xattn_ffn_sample(xs, cache_k, cache_v, l, lp, W, norm_final, last)
        mem_k.append(mk.reshape(B, M, X_HEADS, D // X_HEADS))
        mem_v.append(mv.reshape(B, M, X_HEADS, D // X_HEADS))
        st_p.append(sp)
        st_s.append(ss)
    stack = lambda lst: [jnp.stack([st[i] for st in lst]) for i in range(7)]
    return (xp.reshape(B, S, D), xs.reshape(Bs, 1, D), jnp.stack(mem_k), jnp.stack(mem_v),
            *stack(st_p), *stack(st_s))
```

```python
import functools
import math

import jax
import jax.numpy as jnp
from jax import lax
from jax.experimental import pallas as pl
from jax.experimental.pallas import tpu as pltpu

F32 = jnp.float32
BF16 = jnp.bfloat16
EPS = 1e-6

HEADS = 4
DH = 64
HW = HEADS * DH
CHUNK = 64
S5_P = 16
S5_N = 64
S5_GROUPS = 32
S5_WIDTH = S5_GROUPS * S5_P
S5_STATE = S5_GROUPS * S5_N
S5_GB = 8
S5_NBLK = S5_GROUPS // S5_GB
CONV_K = 4
X_HEADS = 4
GATE_PAD = 128
LANES = 128
SUBLANES = 8
VMEM_LIMIT = 48 * 1024 * 1024


def _cparams(*sem):
    return pltpu.CompilerParams(dimension_semantics=sem, vmem_limit_bytes=VMEM_LIMIT)


def _rms(x, g_row):
    return x * lax.rsqrt(jnp.mean(x * x, axis=-1, keepdims=True) + EPS) * g_row


def _dot(a, b):
    return jnp.dot(a, b, preferred_element_type=F32)


def _dot_nt(a, b):
    return lax.dot_general(a, b, (((1,), (1,)), ((), ())), preferred_element_type=F32)


def _sigmoid(x):
    return 1.0 / (1.0 + jnp.exp(-x))


def _silu(x):
    return x * _sigmoid(x)


def _softplus(x):
    return jnp.maximum(x, 0.0) + jnp.log1p(jnp.exp(-jnp.abs(x)))


def _log_sigmoid(x):
    return jnp.minimum(x, 0.0) - jnp.log1p(jnp.exp(-jnp.abs(x)))


def _norm_matmul_kernel(x_ref, g_ref, w_ref, *out_refs, splits):
    xn = _rms(x_ref[...], g_ref[...]).astype(BF16)
    off = 0
    for o_ref, n in zip(out_refs, splits):
        o_ref[...] = _dot(xn, w_ref[:, off:off + n])
        off += n


def norm_matmul(x, g, w_bf16, splits, tm):
    T, D = x.shape
    N = w_bf16.shape[1]
    assert sum(splits) == N and T % tm == 0
    return pl.pallas_call(
        functools.partial(_norm_matmul_kernel, splits=tuple(splits)),
        grid=(T // tm,),
        in_specs=[pl.BlockSpec((tm, D), lambda i: (i, 0)),
                  pl.BlockSpec((1, D), lambda i: (0, 0)),
                  pl.BlockSpec((D, N), lambda i: (0, 0))],
        out_specs=[pl.BlockSpec((tm, n), lambda i: (i, 0)) for n in splits],
        out_shape=[jax.ShapeDtypeStruct((T, n), F32) for n in splits],
        compiler_params=_cparams("parallel"),
    )(x, g.reshape(1, D), w_bf16)


def _matmul_residual_kernel(x_ref, *refs, ksplits):
    a_refs, w_ref, o_ref = refs[:-2], refs[-2], refs[-1]
    acc = x_ref[...]
    off = 0
    for a_ref, k in zip(a_refs, ksplits):
        acc = acc + _dot(a_ref[...].astype(BF16), w_ref[off:off + k, :])
        off += k
    o_ref[...] = acc


def matmul_residual(x, acts, w_bf16, tm):
    T, D = x.shape
    ks = tuple(a.shape[1] for a in acts)
    K = w_bf16.shape[0]
    assert sum(ks) == K and T % tm == 0
    return pl.pallas_call(
        functools.partial(_matmul_residual_kernel, ksplits=ks),
        grid=(T // tm,),
        in_specs=[pl.BlockSpec((tm, D), lambda i: (i, 0))]
                 + [pl.BlockSpec((tm, k), lambda i: (i, 0)) for k in ks]
                 + [pl.BlockSpec((K, D), lambda i: (0, 0))],
        out_specs=pl.BlockSpec((tm, D), lambda i: (i, 0)),
        out_shape=jax.ShapeDtypeStruct((T, D), F32),
        compiler_params=_cparams("parallel"),
    )(x, *acts, w_bf16)


def _ffn_kernel(x_ref, g_ref, wg_ref, wu_ref, wd_ref, gf_ref, o_ref, hn_scr, acc_scr, *, final_norm):
    j = pl.program_id(1)

    @pl.when(j == 0)
    def _():
        x = x_ref[...]
        hn_scr[...] = _rms(x, g_ref[...]).astype(BF16)
        acc_scr[...] = x

    h = hn_scr[...]
    a = _dot(h, wg_ref[...])
    b = _dot(h, wu_ref[...])
    acc_scr[...] += _dot((_silu(a) * b).astype(BF16), wd_ref[...])

    @pl.when(j == pl.num_programs(1) - 1)
    def _():
        y = acc_scr[...]
        if final_norm:
            y = _rms(y, gf_ref[...])
        o_ref[...] = y


def ffn(x, g, wg, wu, wd, g_final, final_norm, tm, tf):
    T, D = x.shape
    F = wg.shape[1]
    assert T % tm == 0 and F % tf == 0
    return pl.pallas_call(
        functools.partial(_ffn_kernel, final_norm=final_norm),
        grid=(T // tm, F // tf),
        in_specs=[pl.BlockSpec((tm, D), lambda i, j: (i, 0)),
                  pl.BlockSpec((1, D), lambda i, j: (0, 0)),
                  pl.BlockSpec((D, tf), lambda i, j: (0, j)),
                  pl.BlockSpec((D, tf), lambda i, j: (0, j)),
                  pl.BlockSpec((tf, D), lambda i, j: (j, 0)),
                  pl.BlockSpec((1, D), lambda i, j: (0, 0))],
        out_specs=pl.BlockSpec((tm, D), lambda i, j: (i, 0)),
        out_shape=jax.ShapeDtypeStruct((T, D), F32),
        scratch_shapes=[pltpu.VMEM((tm, D), BF16), pltpu.VMEM((tm, D), F32)],
        compiler_params=_cparams("parallel", "arbitrary"),
    )(x, g.reshape(1, D), wg, wu, wd, g_final.reshape(1, D))


def _softmax_rows(s):
    e = jnp.exp(s - jnp.max(s, axis=-1, keepdims=True))
    return e / jnp.sum(e, axis=-1, keepdims=True)


def _xattn_prompt_kernel(x_ref, g_ref, wq_ref, k_ref, v_ref, wo_ref, o_ref, *, dh):
    scale = dh ** -0.5
    x = x_ref[...]
    q = _dot(_rms(x, g_ref[...]).astype(BF16), wq_ref[...]).astype(BF16)
    heads = []
    for h in range(X_HEADS):
        sl = slice(h * dh, (h + 1) * dh)
        s = _dot_nt(q[:, sl], k_ref[0, :, sl].astype(BF16)) * scale
        heads.append(_dot(_softmax_rows(s).astype(BF16), v_ref[0, :, sl].astype(BF16)).astype(BF16))
    o_ref[...] = x + _dot(jnp.concatenate(heads, axis=-1), wo_ref[...])


def xattn_prompt(x, g, wq, mk, mv, wo, seq, tq):
    T, D = x.shape
    B, M, _ = mk.shape
    nt = seq // tq
    return pl.pallas_call(
        functools.partial(_xattn_prompt_kernel, dh=D // X_HEADS),
        grid=(B, nt),
        in_specs=[pl.BlockSpec((tq, D), lambda b, t: (b * nt + t, 0)),
                  pl.BlockSpec((1, D), lambda b, t: (0, 0)),
                  pl.BlockSpec((D, D), lambda b, t: (0, 0)),
                  pl.BlockSpec((1, M, D), lambda b, t: (b, 0, 0)),
                  pl.BlockSpec((1, M, D), lambda b, t: (b, 0, 0)),
                  pl.BlockSpec((D, D), lambda b, t: (0, 0))],
        out_specs=pl.BlockSpec((tq, D), lambda b, t: (b * nt + t, 0)),
        out_shape=jax.ShapeDtypeStruct((T, D), F32),
        compiler_params=_cparams("parallel", "parallel"),
    )(x, g.reshape(1, D), wq, mk, mv, wo)


def _xattn_sample_kernel(q_ref, k_ref, v_ref, o_ref, *, sb):
    M, H, dh = k_ref.shape[1:]
    scale = dh ** -0.5
    row = lax.broadcasted_iota(jnp.int32, (SUBLANES, M * H), 0)
    col_head = lax.broadcasted_iota(jnp.int32, (SUBLANES, M * H), 1) % H
    own = (row % H) == col_head
    pad = jnp.zeros((SUBLANES - H, dh), F32)
    for i in range(sb):
        q8 = jnp.concatenate([q_ref[i], pad], axis=0).astype(BF16)
        s = _dot_nt(q8, k_ref[i].reshape(M * H, dh).astype(BF16)) * scale
        p = _softmax_rows(jnp.where(own, s, -jnp.inf))
        o_ref[i] = _dot(p.astype(BF16), v_ref[i].reshape(M * H, dh).astype(BF16))[0:H]


def xattn_sample(q, ck, cv, layer, sb):
    B, D = q.shape
    _, _, M, H, dh = ck.shape
    out = pl.pallas_call(
        functools.partial(_xattn_sample_kernel, sb=sb),
        grid=(B // sb,),
        in_specs=[pl.BlockSpec((sb, H, dh), lambda i: (i, 0, 0)),
                  pl.BlockSpec((None, sb, M, H, dh), lambda i: (layer, i, 0, 0, 0)),
                  pl.BlockSpec((None, sb, M, H, dh), lambda i: (layer, i, 0, 0, 0))],
        out_specs=pl.BlockSpec((sb, H, dh), lambda i: (i, 0, 0)),
        out_shape=jax.ShapeDtypeStruct((B, H, dh), F32),
        compiler_params=_cparams("parallel"),
    )(q.reshape(B, H, dh), ck, cv)
    return out.reshape(B, D)


def _lane_cat_masks(L):
    row = lax.broadcasted_iota(jnp.int32, (L, HW), 0)
    j = lax.broadcasted_iota(jnp.int32, (L, HW), 1) % DH
    r2 = lax.broadcasted_iota(jnp.int32, (HW, HW), 0) // DH
    c2 = lax.broadcasted_iota(jnp.int32, (HW, HW), 1) // DH
    return row >= j, row > j, row == j, r2 == c2


def _expand_bd(x, bd):
    return jnp.where(bd, jnp.concatenate([x] * HEADS, axis=0), jnp.zeros((), x.dtype))


def _seg_reduce(x, op, fill):
    lo = lax.broadcasted_iota(jnp.int32, (x.shape[0], LANES), 1) < DH
    parts = []
    for c in range(HW // LANES):
        xh = x[:, c * LANES:(c + 1) * LANES]
        a = op(jnp.where(lo, xh, fill), axis=-1, keepdims=True)
        b = op(jnp.where(lo, fill, xh), axis=-1, keepdims=True)
        parts.append(jnp.where(lo, a, b))
    return jnp.concatenate(parts, axis=-1)


def _head_expander(first_lane):
    r = lax.broadcasted_iota(jnp.int32, (GATE_PAD, HW), 0)
    c = lax.broadcasted_iota(jnp.int32, (GATE_PAD, HW), 1) // DH
    return (r == c + first_lane).astype(BF16)


def _chunk_tril(tt):
    r = lax.broadcasted_iota(jnp.int32, (tt, tt), 0)
    c = lax.broadcasted_iota(jnp.int32, (tt, tt), 1)
    return ((r // CHUNK == c // CHUNK) & (r >= c)).astype(BF16)


def _split3(x):
    hi = x.astype(BF16)
    r = x - hi.astype(F32)
    mid = r.astype(BF16)
    return hi, mid, (r - mid.astype(F32)).astype(BF16)


def _dot_sel_r(x, sel):
    hi, mid, lo = _split3(x)
    return (_dot(lo, sel) + _dot(mid, sel)) + _dot(hi, sel)


def _dot_sel_l(sel, x):
    hi, mid, lo = _split3(x)
    return (_dot(sel, lo) + _dot(sel, mid)) + _dot(sel, hi)


def _col_to_row(x, eye):
    return jnp.sum(jnp.where(eye, x, 0.0), axis=0, keepdims=True)


def _head_rms(x, ones_bd, g_row):
    ms = _dot_sel_r(x * x, ones_bd) * (1.0 / DH)
    return x * lax.rsqrt(ms + EPS) * g_row


def _mlstm_prompt_kernel(za_ref, gt_ref, bias_ref, norm_ref, ha_ref, c_ref, n_ref, m_ref,
                         c_scr, n_scr, m_scr, ix_scr, bx_scr, *, nchunks, unroll):
    tb = pl.program_id(1)
    L = CHUNK
    tril, _, eye, bd = _lane_cat_masks(L)
    ones_bd = bd.astype(BF16)

    @pl.when(tb == 0)
    def _():
        c_scr[...] = jnp.zeros_like(c_scr)
        n_scr[...] = jnp.zeros_like(n_scr)
        m_scr[...] = jnp.zeros_like(m_scr)

    gt = gt_ref[...] + bias_ref[...]
    b_cols = _dot_sel_l(_chunk_tril(nchunks * L), _log_sigmoid(gt))
    ix_scr[...] = _dot_sel_r(gt, _head_expander(0))
    bx_scr[...] = _dot_sel_r(b_cols, _head_expander(HEADS))

    def chunk(ci, carry):
        c_bd, n_row, m_x = carry
        rows = pl.ds(pl.multiple_of(ci * L, L), L)
        q = (za_ref[rows, 0:HW] * (DH ** -0.5)).astype(BF16)
        k = za_ref[rows, HW:2 * HW]
        v = za_ref[rows, 2 * HW:3 * HW].astype(BF16)
        og = za_ref[rows, 3 * HW:4 * HW]
        i_x = ix_scr[rows, :]
        b_x = bx_scr[rows, :]

        d_intra = jnp.where(tril, b_x - _col_to_row(b_x, eye) + _col_to_row(i_x, eye), -jnp.inf)
        d_inter = b_x + m_x
        m_t = jnp.maximum(_seg_reduce(d_intra, jnp.max, -jnp.inf), d_inter)
        w_intra = jnp.exp(d_intra - m_t)
        w_inter = jnp.exp(d_inter - m_t)
        kb = k.astype(BF16)
        s = _dot_nt(q, _expand_bd(kb, bd)) * w_intra
        num = w_inter * _dot(q, c_bd.astype(BF16)) + _dot(s.astype(BF16), _expand_bd(v, bd))
        qn = _dot((q.astype(F32) * n_row).astype(BF16), ones_bd)
        den = w_inter * qn + _seg_reduce(s, jnp.sum, 0.0)
        hh = num / jnp.maximum(jnp.abs(den), jnp.exp(-m_t))
        ha_ref[rows, :] = _head_rms(hh, ones_bd, norm_ref[...]) * _sigmoid(og)

        b_last = b_x[L - 1:L, :]
        g_x = b_last - b_x + i_x
        m_new = jnp.maximum(b_last + m_x, jnp.max(g_x, axis=0, keepdims=True))
        kw = k * jnp.exp(g_x - m_new)
        decay = jnp.exp(b_last + m_x - m_new)
        c_new = decay * c_bd + jnp.where(bd, _dot(kw.T.astype(BF16), v), 0.0)
        return c_new, decay * n_row + jnp.sum(kw, axis=0, keepdims=True), m_new

    c_bd, n_row, m_x = lax.fori_loop(0, nchunks, chunk, (c_scr[...], n_scr[...], m_scr[...]), unroll=unroll)
    c_scr[...] = c_bd
    n_scr[...] = n_row
    m_scr[...] = m_x

    @pl.when(tb == pl.num_programs(1) - 1)
    def _():
        for h in range(HEADS):
            c_ref[0, h] = c_scr[h * DH:(h + 1) * DH, h * DH:(h + 1) * DH]
        n_ref[0] = n_scr[...]
        m_ref[0] = m_scr[...]


def mlstm_prompt(za, gates, b_i, b_f, norm, B, S, tt):
    assert S % tt == 0 and tt % CHUNK == 0
    nt = S // tt
    bias = jnp.zeros((1, GATE_PAD), F32).at[0, 0:HEADS].set(b_i).at[0, HEADS:2 * HEADS].set(b_f)
    ha, c, n, m = pl.pallas_call(
        functools.partial(_mlstm_prompt_kernel, nchunks=tt // CHUNK, unroll=4),
        grid=(B, nt),
        in_specs=[pl.BlockSpec((tt, 4 * HW), lambda b, t: (b * nt + t, 0)),
                  pl.BlockSpec((tt, GATE_PAD), lambda b, t: (b * nt + t, 0)),
                  pl.BlockSpec((1, GATE_PAD), lambda b, t: (0, 0)),
                  pl.BlockSpec((1, HW), lambda b, t: (0, 0))],
        out_specs=[pl.BlockSpec((tt, HW), lambda b, t: (b * nt + t, 0)),
                   pl.BlockSpec((1, HEADS, DH, DH), lambda b, t: (b, 0, 0, 0)),
                   pl.BlockSpec((1, 1, HW), lambda b, t: (b, 0, 0)),
                   pl.BlockSpec((1, 1, HW), lambda b, t: (b, 0, 0))],
        out_shape=[jax.ShapeDtypeStruct((B * S, HW), F32),
                   jax.ShapeDtypeStruct((B, HEADS, DH, DH), F32),
                   jax.ShapeDtypeStruct((B, 1, HW), F32),
                   jax.ShapeDtypeStruct((B, 1, HW), F32)],
        scratch_shapes=[pltpu.VMEM((HW, HW), F32), pltpu.VMEM((1, HW), F32), pltpu.VMEM((1, HW), F32),
                        pltpu.VMEM((tt, HW), F32), pltpu.VMEM((tt, HW), F32)],
        compiler_params=_cparams("parallel", "arbitrary"),
    )(za, gates, bias, norm.reshape(1, HW))
    return ha, c, n.reshape(B, HEADS, DH), m[:, 0, ::DH]


def _gdn_prompt_kernel(zc_ref, zg_ref, gt_ref, cw_ref, alog_ref, dtb_ref, norm_ref,
                       hc_ref, s_ref, tail_ref,
                       s_scr, xp_scr, q_scr, k_scr, v_scr, beta_scr, g_scr, uv_scr, wq_scr, qkm_scr, kwt_scr,
                       *, nchunks, group, unroll_b):
    tb = pl.program_id(1)
    L = CHUNK
    tt = nchunks * L
    pad = SUBLANES
    tril, strict, eye, bd = _lane_cat_masks(L)
    ones_bd = bd.astype(BF16)

    @pl.when(tb == 0)
    def _():
        s_scr[...] = jnp.zeros_like(s_scr)
        xp_scr[0:pad, :] = jnp.zeros((pad, 3 * HW), F32)

    x = zc_ref[...]
    xp_scr[pad:pad + tt, :] = x
    y = cw_ref[CONV_K - 1:CONV_K, :] * x
    for j in range(CONV_K - 1):
        y = y + cw_ref[j:j + 1, :] * xp_scr[pl.ds(pad - (CONV_K - 1) + j, tt), :]
    xp_scr[0:pad, :] = x[tt - pad:tt, :]
    y = _silu(y)
    q_raw, k_raw = y[:, 0:HW], y[:, HW:2 * HW]
    q_scr[...] = (q_raw * lax.rsqrt(_dot_sel_r(q_raw * q_raw, ones_bd) + EPS) * (DH ** -0.5)).astype(BF16)
    k_scr[...] = k_raw * lax.rsqrt(_dot_sel_r(k_raw * k_raw, ones_bd) + EPS)
    v_scr[...] = y[:, 2 * HW:3 * HW]
    gt = gt_ref[...]
    beta_scr[...] = _dot_sel_r(_sigmoid(gt), _head_expander(2 * HEADS))
    la_cols = -jnp.exp(alog_ref[...]) * _softplus(gt + dtb_ref[...])
    g_scr[...] = _dot_sel_r(_dot_sel_l(_chunk_tril(tt), la_cols), _head_expander(3 * HEADS))

    def prepare(gi, carry):
        cis = [gi * group + j for j in range(group)]
        rows = [pl.ds(pl.multiple_of(ci * L, L), L) for ci in cis]
        k = [k_scr[r, :] for r in rows]
        g_x = [g_scr[r, :] for r in rows]
        beta_row = [_col_to_row(beta_scr[r, :], eye) for r in rows]
        dec_incl = [jnp.where(tril, jnp.exp(jnp.where(tril, g - _col_to_row(g, eye), 0.0)), 0.0) for g in g_x]
        k_bd = [_expand_bd(kk.astype(BF16), bd) for kk in k]
        n0 = [-(jnp.where(strict, d, 0.0) * _dot_nt(kk.astype(BF16), kbd) * br)
              for d, kk, kbd, br in zip(dec_incl, k, k_bd, beta_row)]
        for r, d, kbd, br in zip(rows, dec_incl, k_bd, beta_row):
            qkm_scr[r, :] = (_dot_nt(q_scr[r, :], kbd) * d * br).astype(BF16)

        p = [_dot(n.astype(BF16), _expand_bd(n.astype(BF16), bd)) for n in n0]
        m = n0
        steps = int(math.log2(L)) - 1
        for i in range(steps):
            p_bd = [_expand_bd(pp.astype(BF16), bd) for pp in p]
            if i < steps - 1:
                pm = [_dot(jnp.concatenate([pp, mm], axis=0).astype(BF16), pbd) for pp, mm, pbd in zip(p, m, p_bd)]
                p_next, mp = [x[0:L] for x in pm], [x[L:2 * L] for x in pm]
            else:
                p_next, mp = None, [_dot(mm.astype(BF16), pbd) for mm, pbd in zip(m, p_bd)]
            m = [mm + pp + x for mm, pp, x in zip(m, p, mp)]
            p = p_next

        for ci, r, kk, g, mm in zip(cis, rows, k, g_x, m):
            v = v_scr[r, :]
            egk = jnp.exp(g) * kk
            rhs_bd = jnp.concatenate([_expand_bd(v.astype(BF16), bd), _expand_bd(egk.astype(BF16), bd)], axis=1)
            mr = _dot(mm.astype(BF16), rhs_bd)
            uv_scr[r, :] = v + mr[:, 0:HW]
            wq_rows = pl.multiple_of(ci * 2 * L, 2 * L)
            wq_scr[pl.ds(wq_rows, L), :] = (egk + mr[:, HW:2 * HW]).astype(BF16)
            wq_scr[pl.ds(wq_rows + L, L), :] = q_scr[r, :]
            kw = kk * (jnp.exp(g[L - 1:L, :] - g) * beta_scr[r, :])
            kwt_scr[pl.ds(pl.multiple_of(ci * HW, HW), HW), :] = kw.T.astype(BF16)
        return carry

    lax.fori_loop(0, nchunks // group, prepare, 0)

    def advance(ci, s_bd):
        rows = pl.ds(pl.multiple_of(ci * L, L), L)
        g_x = g_scr[rows, :]
        wqs = _dot(wq_scr[pl.ds(pl.multiple_of(ci * 2 * L, 2 * L), 2 * L), :], s_bd.astype(BF16))
        ub = (uv_scr[rows, :] - wqs[0:L]).astype(BF16)
        o = jnp.exp(g_x) * wqs[L:2 * L] + _dot(qkm_scr[rows, :], _expand_bd(ub, bd))
        hc_ref[rows, :] = _head_rms(o, ones_bd, norm_ref[...]) * _silu(zg_ref[rows, :])
        kwt = kwt_scr[pl.ds(pl.multiple_of(ci * HW, HW), HW), :]
        return jnp.exp(g_x[L - 1:L, :]) * s_bd + jnp.where(bd, _dot(kwt, ub), 0.0)

    s_scr[...] = lax.fori_loop(0, nchunks, advance, s_scr[...], unroll=unroll_b)

    @pl.when(tb == pl.num_programs(1) - 1)
    def _():
        for h in range(HEADS):
            s_ref[0, h] = s_scr[h * DH:(h + 1) * DH, h * DH:(h + 1) * DH]
        tail_ref[0] = xp_scr[0:pad, :]


def _gdn_gate_rows(a_log, dt_bias):
    z = jnp.zeros((1, GATE_PAD), F32)
    return (z.at[0, 3 * HEADS:4 * HEADS].set(a_log), z.at[0, 3 * HEADS:4 * HEADS].set(dt_bias))


def gdn_prompt(zc, zg, gates, conv_w, a_log, dt_bias, norm, B, S, tt):
    assert S % tt == 0 and tt % CHUNK == 0
    nt = S // tt
    alog, dtb = _gdn_gate_rows(a_log, dt_bias)
    hc, s, tail = pl.pallas_call(
        functools.partial(_gdn_prompt_kernel, nchunks=tt // CHUNK, group=math.gcd(tt // CHUNK, 4), unroll_b=2),
        grid=(B, nt),
        in_specs=[pl.BlockSpec((tt, 3 * HW), lambda b, t: (b * nt + t, 0)),
                  pl.BlockSpec((tt, HW), lambda b, t: (b * nt + t, 0)),
                  pl.BlockSpec((tt, GATE_PAD), lambda b, t: (b * nt + t, 0)),
                  pl.BlockSpec((CONV_K, 3 * HW), lambda b, t: (0, 0)),
                  pl.BlockSpec((1, GATE_PAD), lambda b, t: (0, 0)),
                  pl.BlockSpec((1, GATE_PAD), lambda b, t: (0, 0)),
                  pl.BlockSpec((1, HW), lambda b, t: (0, 0))],
        out_specs=[pl.BlockSpec((tt, HW), lambda b, t: (b * nt + t, 0)),
                   pl.BlockSpec((1, HEADS, DH, DH), lambda b, t: (b, 0, 0, 0)),
                   pl.BlockSpec((1, SUBLANES, 3 * HW), lambda b, t: (b, 0, 0))],
        out_shape=[jax.ShapeDtypeStruct((B * S, HW), F32),
                   jax.ShapeDtypeStruct((B, HEADS, DH, DH), F32),
                   jax.ShapeDtypeStruct((B, SUBLANES, 3 * HW), F32)],
        scratch_shapes=[pltpu.VMEM((HW, HW), F32), pltpu.VMEM((tt + SUBLANES, 3 * HW), F32),
                        pltpu.VMEM((tt, HW), BF16), pltpu.VMEM((tt, HW), F32), pltpu.VMEM((tt, HW), F32),
                        pltpu.VMEM((tt, HW), F32), pltpu.VMEM((tt, HW), F32), pltpu.VMEM((tt, HW), F32),
                        pltpu.VMEM((2 * tt, HW), BF16), pltpu.VMEM((tt, HW), BF16),
                        pltpu.VMEM((tt // CHUNK * HW, CHUNK), BF16)],
        compiler_params=_cparams("parallel", "arbitrary"),
    )(zc, zg, gates, conv_w, alog, dtb, jnp.tile(norm, HEADS).reshape(1, HW))
    return hc, s, tail[:, SUBLANES - (CONV_K - 1):, :]


def _s5_prep_kernel(are_ref, aim_ref, ldt_ref, bre_ref, bim_ref, lre_ref, lim_ref, bbre_ref, bbim_ref):
    a_re, a_im = are_ref[...], aim_ref[...]
    dt = jnp.exp(ldt_ref[...])
    mag = jnp.exp(a_re * dt)
    lam_re, lam_im = mag * jnp.cos(a_im * dt), mag * jnp.sin(a_im * dt)
    lre_ref[...] = lam_re
    lim_ref[...] = lam_im
    nr, ni = lam_re - 1.0, lam_im
    den = a_re * a_re + a_im * a_im
    coef_re = (nr * a_re + ni * a_im) / den
    coef_im = (ni * a_re - nr * a_im) / den
    b_re, b_im = bre_ref[...], bim_ref[...]
    bbre_ref[...] = coef_re * b_re - coef_im * b_im
    bbim_ref[...] = coef_re * b_im + coef_im * b_re


def s5_params(lp):
    G, N, P = S5_GROUPS, S5_N, S5_P
    row = lambda a: a.astype(F32).reshape(1, G * N)
    to_pn = lambda b: jnp.transpose(b.astype(F32), (2, 0, 1)).reshape(P, G * N)
    shp = [jax.ShapeDtypeStruct((1, G * N), F32)] * 2 + [jax.ShapeDtypeStruct((P, G * N), F32)] * 2
    lam_re, lam_im, bb_re, bb_im = pl.pallas_call(_s5_prep_kernel, out_shape=shp)(
        row(lp['s5_a_re']), row(lp['s5_a_im']), row(jnp.repeat(lp['s5_log_dt'][:, None], N, axis=1)),
        to_pn(lp['s5_b_re']), to_pn(lp['s5_b_im']))
    eye = jnp.eye(S5_GB, dtype=F32)

    def w_in_blocks(bb):
        b4 = bb.reshape(P, S5_NBLK, S5_GB, N)
        return jnp.einsum('pbgn,gh->bgphn', b4, eye).reshape(S5_NBLK, S5_GB * P, S5_GB * N)

    def w_out_blocks(c):
        c4 = c.astype(F32).reshape(S5_NBLK, S5_GB, P, N)
        return jnp.einsum('bgpn,gh->bgnhp', c4, eye).reshape(S5_NBLK, S5_GB * N, S5_GB * P)

    w_in = jnp.concatenate([w_in_blocks(bb_re), w_in_blocks(bb_im)], axis=-1).astype(BF16)
    return {'lam_re': lam_re, 'lam_im': lam_im, 'w_in': w_in,
            'w_out_re': w_out_blocks(lp['s5_c_re']).astype(BF16),
            'w_out_im': (-w_out_blocks(lp['s5_c_im'])).astype(BF16),
            'd': lp['s5_d'].astype(F32).reshape(1, S5_WIDTH), 'w_glu': lp['s5_w_glu'].astype(BF16)}


def _s5_kernel(u_ref, h0r_ref, h0i_ref, lamr_ref, lami_ref, win_ref, wor_ref, woi_ref, d_ref, wglu_ref,
               ys_ref, h1r_ref, h1i_ref, hr_scr, hi_scr, br_scr, bi_scr, *, nseq, rows, bb, lane_blk):
    tb = pl.program_id(0)
    nsteps = nseq * rows // bb
    blk_in, blk_st = S5_GB * S5_P, S5_GB * S5_N

    @pl.when(tb == 0)
    def _():
        hr_scr[...] = h0r_ref[...]
        hi_scr[...] = h0i_ref[...]

    if nseq > 1:
        u = pltpu.einshape("btd->tbd", u_ref[...]).reshape(nseq * rows, S5_WIDTH)
    else:
        u = u_ref[0]
    ub = u.astype(BF16)
    for blk in range(S5_NBLK):
        bu = _dot(ub[:, blk * blk_in:(blk + 1) * blk_in], win_ref[blk])
        br_scr[:, blk * blk_st:(blk + 1) * blk_st] = bu[:, 0:blk_st]
        bi_scr[:, blk * blk_st:(blk + 1) * blk_st] = bu[:, blk_st:2 * blk_st]

    for lb in range(S5_STATE // lane_blk):
        ls = slice(lb * lane_blk, (lb + 1) * lane_blk)
        lr = jnp.broadcast_to(lamr_ref[:, ls], (bb, lane_blk))
        li = jnp.broadcast_to(lami_ref[:, ls], (bb, lane_blk))

        def step(t, carry):
            hr, hi = carry
            r = pl.ds(pl.multiple_of(t * bb, bb), bb)
            nr = lr * hr - li * hi + br_scr[r, ls]
            ni = lr * hi + li * hr + bi_scr[r, ls]
            br_scr[r, ls] = nr
            bi_scr[r, ls] = ni
            return nr, ni

        hr, hi = lax.fori_loop(0, nsteps, step, (hr_scr[:, ls], hi_scr[:, ls]), unroll=min(nsteps, 8))
        hr_scr[:, ls] = hr
        hi_scr[:, ls] = hi

    ys = []
    for blk in range(S5_NBLK):
        st = slice(blk * blk_st, (blk + 1) * blk_st)
        ys.append(_dot(br_scr[:, st].astype(BF16), wor_ref[blk]) + _dot(bi_scr[:, st].astype(BF16), woi_ref[blk]))
    gy = jax.nn.gelu(jnp.concatenate(ys, axis=-1) + d_ref[...] * u)
    out = gy * _sigmoid(_dot(gy.astype(BF16), wglu_ref[...]))
    if nseq > 1:
        ys_ref[...] = pltpu.einshape("tbd->btd", out.reshape(rows, nseq, S5_WIDTH))
    else:
        ys_ref[0] = out

    @pl.when(tb == pl.num_programs(0) - 1)
    def _():
        h1r_ref[...] = hr_scr[...]
        h1i_ref[...] = hi_scr[...]


def s5_mixer(u, h0_re, h0_im, sp, tt, single_step):
    B = h0_re.shape[0]
    nseq, S = (1, 1) if single_step else (B, u.shape[1])
    rows = B if single_step else tt
    assert S % tt == 0 and B % SUBLANES == 0
    assert u.shape == ((1, B, S5_WIDTH) if single_step else (B, S, S5_WIDTH))
    lane_blk = max(LANES, min(S5_STATE, (SUBLANES * SUBLANES * LANES) // B))
    full = lambda shape: pl.BlockSpec(shape, lambda t: (0,) * len(shape))
    return pl.pallas_call(
        functools.partial(_s5_kernel, nseq=nseq, rows=rows, bb=B, lane_blk=lane_blk),
        grid=(S // tt,),
        in_specs=[pl.BlockSpec((nseq, rows, S5_WIDTH), lambda t: (0, t, 0)),
                  full((B, S5_STATE)), full((B, S5_STATE)), full((1, S5_STATE)), full((1, S5_STATE)),
                  full(sp['w_in'].shape), full(sp['w_out_re'].shape), full(sp['w_out_im'].shape),
                  full((1, S5_WIDTH)), full((S5_WIDTH, S5_WIDTH))],
        out_specs=[pl.BlockSpec((nseq, rows, S5_WIDTH), lambda t: (0, t, 0)),
                   full((B, S5_STATE)), full((B, S5_STATE))],
        out_shape=[jax.ShapeDtypeStruct(u.shape, F32),
                   jax.ShapeDtypeStruct((B, S5_STATE), F32), jax.ShapeDtypeStruct((B, S5_STATE), F32)],
        scratch_shapes=[pltpu.VMEM((B, S5_STATE), F32), pltpu.VMEM((B, S5_STATE), F32),
                        pltpu.VMEM((nseq * rows, S5_STATE), F32), pltpu.VMEM((nseq * rows, S5_STATE), F32)],
        compiler_params=_cparams("arbitrary"),
    )(u, h0_re, h0_im, sp['lam_re'], sp['lam_im'], sp['w_in'], sp['w_out_re'], sp['w_out_im'], sp['d'], sp['w_glu'])


def _ones_bd():
    r = lax.broadcasted_iota(jnp.int32, (HW, HW), 0) // DH
    c = lax.broadcasted_iota(jnp.int32, (HW, HW), 1) // DH
    return (r == c).astype(BF16)


def _mlstm_sample_kernel(za_ref, gt_ref, bias_ref, norm_ref, c_ref, n_ref, m_ref,
                         ha_ref, c_out, n_out, m_out, q_scr, kw_scr, h_scr):
    za = za_ref[...]
    q_scr[...] = (za[:, 0:HW] * (DH ** -0.5)).T
    k_t = za[:, HW:2 * HW].T
    v_t = za[:, 2 * HW:3 * HW].T
    g_t = (gt_ref[...] + bias_ref[...]).T
    m_out[...] = jnp.zeros_like(m_out)
    for h in range(HEADS):
        hs = slice(h * DH, (h + 1) * DH)
        i_h = g_t[h:h + 1, :]
        bm = _log_sigmoid(g_t[HEADS + h:HEADS + h + 1, :]) + m_ref[h:h + 1, :]
        m_t = jnp.maximum(i_h, bm)
        w_in = jnp.exp(i_h - m_t)
        w_st = jnp.exp(bm - m_t)
        q_h, k_h, v_h = q_scr[hs, :], k_t[hs, :], v_t[hs, :]
        s = jnp.sum(q_h * k_h, axis=0, keepdims=True) * w_in
        kw_scr[hs, :] = k_h * w_in

        def body(d, acc):
            r = h * DH + d
            rows = pl.ds(pl.multiple_of(r * DH, DH), DH)
            c_hd = c_ref[rows, :]
            c_out[rows, :] = w_st * c_hd + kw_scr[pl.ds(r, 1), :] * v_h
            return acc + q_scr[pl.ds(r, 1), :] * c_hd

        qc = lax.fori_loop(0, DH, body, jnp.zeros((DH, za.shape[0]), F32), unroll=4)
        n_h = n_ref[hs, :]
        num = w_st * qc + s * v_h
        den = w_st * jnp.sum(q_h * n_h, axis=0, keepdims=True) + s
        h_scr[hs, :] = num / jnp.maximum(jnp.abs(den), jnp.exp(-m_t))
        n_out[hs, :] = w_st * n_h + kw_scr[hs, :]
        m_out[h:h + 1, :] = m_t
    ha_ref[...] = _head_rms(h_scr[...].T, _ones_bd(), norm_ref[...]) * _sigmoid(za[:, 3 * HW:4 * HW])


def mlstm_sample(za, gates, b_i, b_f, norm, c_t, n_t, m_t):
    B = za.shape[0]
    bias = jnp.zeros((1, GATE_PAD), F32).at[0, 0:HEADS].set(b_i).at[0, HEADS:2 * HEADS].set(b_f)
    shp = lambda *s: jax.ShapeDtypeStruct(s, F32)
    return pl.pallas_call(
        _mlstm_sample_kernel,
        out_shape=[shp(B, HW), shp(HW * DH, B), shp(HW, B), shp(SUBLANES, B)],
        scratch_shapes=[pltpu.VMEM((HW, B), F32), pltpu.VMEM((HW, B), F32), pltpu.VMEM((HW, B), F32)],
        compiler_params=pltpu.CompilerParams(vmem_limit_bytes=VMEM_LIMIT),
    )(za, gates, bias, norm.reshape(1, HW), c_t, n_t, m_t)


def _gdn_sample_kernel(zc_ref, zg_ref, gt_ref, buf_ref, cw_ref, alog_ref, dtb_ref, norm_ref, s_ref,
                       hc_ref, s_out, buf_out, q_scr, k_scr, o_scr):
    W3 = 3 * HW
    x = zc_ref[...]
    y = cw_ref[CONV_K - 1:CONV_K, :] * x
    for j in range(CONV_K - 1):
        y = y + cw_ref[j:j + 1, :] * buf_ref[:, j * W3:(j + 1) * W3]
    buf_out[:, 0:(CONV_K - 2) * W3] = buf_ref[:, W3:(CONV_K - 1) * W3]
    buf_out[:, (CONV_K - 2) * W3:(CONV_K - 1) * W3] = x
    y = _silu(y)
    ones_bd = _ones_bd()
    q_raw, k_raw = y[:, 0:HW], y[:, HW:2 * HW]
    q_scr[...] = (q_raw * lax.rsqrt(_dot_sel_r(q_raw * q_raw, ones_bd) + EPS) * (DH ** -0.5)).T
    k_scr[...] = (k_raw * lax.rsqrt(_dot_sel_r(k_raw * k_raw, ones_bd) + EPS)).T
    v_t = y[:, 2 * HW:3 * HW].T
    gt = gt_ref[...]
    beta_t = _sigmoid(gt).T
    la_t = (-jnp.exp(alog_ref[...]) * _softplus(gt + dtb_ref[...])).T
    nb = x.shape[0]
    for h in range(HEADS):
        hs = slice(h * DH, (h + 1) * DH)
        beta = beta_t[2 * HEADS + h:2 * HEADS + h + 1, :]
        eg = jnp.exp(la_t[3 * HEADS + h:3 * HEADS + h + 1, :])
        q_h, k_h, v_h = q_scr[hs, :], k_scr[hs, :], v_t[hs, :]

        def read(d, acc):
            ks, qs = acc
            r = h * DH + d
            s_hd = s_ref[pl.ds(pl.multiple_of(r * DH, DH), DH), :]
            return ks + k_scr[pl.ds(r, 1), :] * s_hd, qs + q_scr[pl.ds(r, 1), :] * s_hd

        zero = jnp.zeros((DH, nb), F32)
        ks, qs = lax.fori_loop(0, DH, read, (zero, zero), unroll=4)
        u = v_h - eg * ks
        o_scr[hs, :] = eg * qs + (jnp.sum(q_h * k_h, axis=0, keepdims=True) * beta) * u

        def write(d, carry):
            r = h * DH + d
            rows = pl.ds(pl.multiple_of(r * DH, DH), DH)
            s_out[rows, :] = eg * s_ref[rows, :] + (beta * k_scr[pl.ds(r, 1), :]) * u
            return carry

        lax.fori_loop(0, DH, write, 0, unroll=4)
    hc_ref[...] = _head_rms(o_scr[...].T, ones_bd, norm_ref[...]) * _silu(zg_ref[...])


def gdn_sample(zc, zg, gates, buf, conv_w, a_log, dt_bias, norm, s_t):
    B = zc.shape[0]
    alog, dtb = _gdn_gate_rows(a_log, dt_bias)
    shp = lambda *s: jax.ShapeDtypeStruct(s, F32)
    return pl.pallas_call(
        _gdn_sample_kernel,
        out_shape=[shp(B, HW), shp(HW * DH, B), shp(B, (CONV_K - 1) * 3 * HW)],
        scratch_shapes=[pltpu.VMEM((HW, B), F32), pltpu.VMEM((HW, B), F32), pltpu.VMEM((HW, B), F32)],
        compiler_params=pltpu.CompilerParams(vmem_limit_bytes=VMEM_LIMIT),
    )(zc, zg, gates, buf, conv_w, alog, dtb, jnp.tile(norm, HEADS).reshape(1, HW), s_t)


IN_SEGMENTS = (4 * HW, S5_WIDTH, 3 * HW, HW, GATE_PAD)


def _tile(n, pref):
    return pref if n % pref == 0 else n


def prep_layer_weights(lp):
    w = lp['w_in']
    a, g4 = 4 * HW, HEADS
    o_u = a + 2 * g4
    o_c = o_u + S5_WIDTH
    o_g = o_c + 3 * HW
    o_b = o_g + HW
    gate_cols = jnp.concatenate([w[:, a:a + 2 * g4], w[:, o_b:o_b + 2 * g4],
                                 jnp.zeros((w.shape[0], GATE_PAD - 4 * g4), w.dtype)], axis=1)
    w_in = jnp.concatenate([w[:, 0:a], w[:, o_u:o_c], w[:, o_c:o_g], w[:, o_g:o_b], gate_cols], axis=1)
    bf = lambda n: lp[n].astype(BF16)
    return {'w_in': w_in.astype(BF16), 'w_out': bf('w_out'), 'w_mq': bf('w_mq'), 'w_mo': bf('w_mo'),
            'w_mkv': jnp.concatenate([lp['w_mk'], lp['w_mv']], axis=1).astype(BF16),
            'w_gate': bf('w_gate'), 'w_up': bf('w_up'), 'w_down': bf('w_down'), 's5': s5_params(lp)}


def mixer_prompt(x, lp, W, B, S):
    T = B * S
    tm = _tile(T, 512)
    za, zu, zc, zg, gates = norm_matmul(x, lp['norm_mix'], W['w_in'], IN_SEGMENTS, tm)
    tt = _tile(S, 512)
    ha, c1, n1, m1 = mlstm_prompt(za, gates, lp['mlstm_b_i'], lp['mlstm_b_f'], lp['mlstm_norm'], B, S, tt)
    h0 = jnp.zeros((B, S5_STATE), F32)
    ys3, r1, i1 = s5_mixer(zu.reshape(B, S, S5_WIDTH), h0, h0, W['s5'], _tile(S, 128), False)
    ys = ys3.reshape(T, S5_WIDTH)
    hc, s1, buf1 = gdn_prompt(zc, zg, gates, lp['gdn_conv_w'], lp['gdn_a_log'], lp['gdn_dt_bias'], lp['gdn_norm'], B, S, tt)
    x1 = matmul_residual(x, [ha, ys, hc], W['w_out'], tm)
    return x1, (c1, n1, m1, r1.reshape(B, S5_GROUPS, S5_N), i1.reshape(B, S5_GROUPS, S5_N), s1, buf1)


def mixer_sample(x, st, lp, W):
    B = x.shape[0]
    c0, n0, m0, r0, i0, s0, buf0 = st
    za, zu, zc, zg, gates = norm_matmul(x, lp['norm_mix'], W['w_in'], IN_SEGMENTS, B)
    m_t = jnp.zeros((SUBLANES, B), F32).at[0:HEADS, :].set(m0.T)
    ha, c1t, n1t, m1t = mlstm_sample(za, gates, lp['mlstm_b_i'], lp['mlstm_b_f'], lp['mlstm_norm'],
                                     c0.reshape(B, HW * DH).T, n0.reshape(B, HW).T, m_t)
    ys3, r1, i1 = s5_mixer(zu.reshape(1, B, S5_WIDTH), r0.reshape(B, S5_STATE), i0.reshape(B, S5_STATE), W['s5'], 1, True)
    hc, s1t, buf1 = gdn_sample(zc, zg, gates, buf0.reshape(B, (CONV_K - 1) * 3 * HW), lp['gdn_conv_w'],
                               lp['gdn_a_log'], lp['gdn_dt_bias'], lp['gdn_norm'], s0.reshape(B, HW * DH).T)
    x1 = matmul_residual(x, [ha, ys3.reshape(B, S5_WIDTH), hc], W['w_out'], B)
    return x1, (c1t.T.reshape(B, HEADS, DH, DH), n1t.T.reshape(B, HEADS, DH), m1t[0:HEADS, :].T,
                r1.reshape(B, S5_GROUPS, S5_N), i1.reshape(B, S5_GROUPS, S5_N),
                s1t.T.reshape(B, HEADS, DH, DH), buf1.reshape(B, CONV_K - 1, 3 * HW))


def mem_kv(mem, lp, W):
    D = mem.shape[1]
    return norm_matmul(mem, lp['norm_mem'], W['w_mkv'], (D, D), _tile(mem.shape[0], 512))


def xattn_ffn_prompt(x, mk, mv, lp, W, S, norm_final, final):
    T, D = x.shape
    x2 = xattn_prompt(x, lp['norm_xattn'], W['w_mq'], mk, mv, W['w_mo'], S, _tile(S, 512))
    return ffn(x2, lp['norm_ffn'], W['w_gate'], W['w_up'], W['w_down'], norm_final, final, _tile(T, 1024), 256)


def xattn_ffn_sample(x, ck, cv, layer, lp, W, norm_final, final):
    B, D = x.shape
    (q,) = norm_matmul(x, lp['norm_xattn'], W['w_mq'], (D,), B)
    o = xattn_sample(q, ck, cv, layer, 4)
    x2 = matmul_residual(x, [o], W['w_mo'], B)
    return ffn(x2, lp['norm_ffn'], W['w_gate'], W['w_up'], W['w_down'], norm_final, final, B, 256)


LAYER_PARAMS = ('norm_mix', 'w_in', 'w_out', 'mlstm_b_i', 'mlstm_b_f', 'mlstm_norm', 's5_a_re', 's5_a_im', 's5_log_dt',
                's5_b_re', 's5_b_im', 's5_c_re', 's5_c_im', 's5_d', 's5_w_glu', 'gdn_conv_w', 'gdn_a_log',
                'gdn_dt_bias', 'gdn_norm', 'norm_xattn', 'norm_mem', 'w_mq', 'w_mk', 'w_mv', 'w_mo', 'norm_ffn',
                'w_gate', 'w_up', 'w_down')


def kernel(x_prompt, x_sample, mem_prompt, cache_mem_k, cache_mem_v, state_mlstm_c, state_mlstm_n, state_mlstm_m, state_s5_re, state_s5_im, state_gdn, state_gdn_conv, norm_mix, w_in, w_out, mlstm_b_i, mlstm_b_f, mlstm_norm, s5_a_re, s5_a_im, s5_log_dt, s5_b_re, s5_b_im, s5_c_re, s5_c_im, s5_d, s5_w_glu, gdn_conv_w, gdn_a_log, gdn_dt_bias, gdn_norm, norm_xattn, norm_mem, w_mq, w_mk, w_mv, w_mo, norm_ffn, w_gate, w_up, w_down, norm_final):
    stacked = dict(norm_mix=norm_mix, w_in=w_in, w_out=w_out, mlstm_b_i=mlstm_b_i, mlstm_b_f=mlstm_b_f,
                   mlstm_norm=mlstm_norm, s5_a_re=s5_a_re, s5_a_im=s5_a_im, s5_log_dt=s5_log_dt, s5_b_re=s5_b_re,
                   s5_b_im=s5_b_im, s5_c_re=s5_c_re, s5_c_im=s5_c_im, s5_d=s5_d, s5_w_glu=s5_w_glu,
                   gdn_conv_w=gdn_conv_w, gdn_a_log=gdn_a_log, gdn_dt_bias=gdn_dt_bias, gdn_norm=gdn_norm,
                   norm_xattn=norm_xattn, norm_mem=norm_mem, w_mq=w_mq, w_mk=w_mk, w_mv=w_mv, w_mo=w_mo,
                   norm_ffn=norm_ffn, w_gate=w_gate, w_up=w_up, w_down=w_down)
    B, S, D = x_prompt.shape
    Bs = x_sample.shape[0]
    M = mem_prompt.shape[1]
    depth = w_in.shape[0]
    xp = x_prompt.reshape(B * S, D)
    xs = x_sample.reshape(Bs, D)
    mem = mem_prompt.reshape(B * M, D)
    cache_k, cache_v = cache_mem_k, cache_mem_v
    mem_k, mem_v, st_p, st_s = [], [], [], []
    for l in range(depth):
        lp = {n: stacked[n][l] for n in LAYER_PARAMS}
        W = prep_layer_weights(lp)
        last = l == depth - 1
        xp, sp = mixer_prompt(xp, lp, W, B, S)
        mk, mv = mem_kv(mem, lp, W)
        xp = xattn_ffn_prompt(xp, mk.reshape(B, M, D), mv.reshape(B, M, D), lp, W, S, norm_final, last)
        st_in = (state_mlstm_c[l], state_mlstm_n[l], state_mlstm_m[l], state_s5_re[l], state_s5_im[l],
                 state_gdn[l], state_gdn_conv[l])
        xs, ss = mixer_sample(xs, st_in, lp, W)
        xs = xattn_ffn_sample(xs, cache_k, cache_v, l, lp, W, norm_final, last)
        mem_k.append(mk.reshape(B, M, X_HEADS, D // X_HEADS))
        mem_v.append(mv.reshape(B, M, X_HEADS, D // X_HEADS))
        st_p.append(sp)
        st_s.append(ss)
    stack = lambda lst: [jnp.stack([st[i] for st in lst]) for i in range(7)]
    return (xp.reshape(B, S, D), xs.reshape(Bs, 1, D), jnp.stack(mem_k), jnp.stack(mem_v),
            *stack(st_p), *stack(st_s))
```

```python
import functools
import math

import jax
import jax.numpy as jnp
from jax import lax
from jax.experimental import pallas as pl
from jax.experimental.pallas import tpu as pltpu

F32 = jnp.float32
BF16 = jnp.bfloat16
EPS = 1e-6

HEADS = 4
DH = 64
HW = HEADS * DH
CHUNK = 64
S5_P = 16
S5_N = 64
S5_GROUPS = 32
S5_WIDTH = S5_GROUPS * S5_P
S5_STATE = S5_GROUPS * S5_N
S5_GB = 8
S5_NBLK = S5_GROUPS // S5_GB
CONV_K = 4
X_HEADS = 4
GATE_PAD = 128
LANES = 128
SUBLANES = 8
VMEM_LIMIT = 48 * 1024 * 1024


def _cparams(*sem):
    return pltpu.CompilerParams(dimension_semantics=sem, vmem_limit_bytes=VMEM_LIMIT)


def _rms(x, g_row):
    return x * lax.rsqrt(jnp.mean(x * x, axis=-1, keepdims=True) + EPS) * g_row


def _dot(a, b):
    return jnp.dot(a, b, preferred_element_type=F32)


def _dot_nt(a, b):
    return lax.dot_general(a, b, (((1,), (1,)), ((), ())), preferred_element_type=F32)


def _sigmoid(x):
    return 1.0 / (1.0 + jnp.exp(-x))


def _silu(x):
    return x * _sigmoid(x)


def _softplus(x):
    return jnp.maximum(x, 0.0) + jnp.log1p(jnp.exp(-jnp.abs(x)))


def _log_sigmoid(x):
    return jnp.minimum(x, 0.0) - jnp.log1p(jnp.exp(-jnp.abs(x)))


def _norm_matmul_kernel(x_ref, g_ref, w_ref, *out_refs, splits):
    xn = _rms(x_ref[...], g_ref[...]).astype(BF16)
    off = 0
    for o_ref, n in zip(out_refs, splits):
        o_ref[...] = _dot(xn, w_ref[:, off:off + n])
        off += n


def norm_matmul(x, g, w_bf16, splits, tm):
    T, D = x.shape
    N = w_bf16.shape[1]
    assert sum(splits) == N and T % tm == 0
    return pl.pallas_call(
        functools.partial(_norm_matmul_kernel, splits=tuple(splits)),
        grid=(T // tm,),
        in_specs=[pl.BlockSpec((tm, D), lambda i: (i, 0)),
                  pl.BlockSpec((1, D), lambda i: (0, 0)),
                  pl.BlockSpec((D, N), lambda i: (0, 0))],
        out_specs=[pl.BlockSpec((tm, n), lambda i: (i, 0)) for n in splits],
        out_shape=[jax.ShapeDtypeStruct((T, n), F32) for n in splits],
        compiler_params=_cparams("parallel"),
    )(x, g.reshape(1, D), w_bf16)


def _matmul_residual_kernel(x_ref, *refs, ksplits):
    a_refs, w_ref, o_ref = refs[:-2], refs[-2], refs[-1]
    acc = x_ref[...]
    off = 0
    for a_ref, k in zip(a_refs, ksplits):
        acc = acc + _dot(a_ref[...].astype(BF16), w_ref[off:off + k, :])
        off += k
    o_ref[...] = acc


def matmul_residual(x, acts, w_bf16, tm):
    T, D = x.shape
    ks = tuple(a.shape[1] for a in acts)
    K = w_bf16.shape[0]
    assert sum(ks) == K and T % tm == 0
    return pl.pallas_call(
        functools.partial(_matmul_residual_kernel, ksplits=ks),
        grid=(T // tm,),
        in_specs=[pl.BlockSpec((tm, D), lambda i: (i, 0))]
                 + [pl.BlockSpec((tm, k), lambda i: (i, 0)) for k in ks]
                 + [pl.BlockSpec((K, D), lambda i: (0, 0))],
        out_specs=pl.BlockSpec((tm, D), lambda i: (i, 0)),
        out_shape=jax.ShapeDtypeStruct((T, D), F32),
        compiler_params=_cparams("parallel"),
    )(x, *acts, w_bf16)


def _ffn_kernel(x_ref, g_ref, wg_ref, wu_ref, wd_ref, gf_ref, o_ref, *, final_norm, tf):
    x = x_ref[...]
    h = _rms(x, g_ref[...]).astype(BF16)
    y = x
    for j in range(wg_ref.shape[1] // tf):
        cols = slice(j * tf, (j + 1) * tf)
        a = _dot(h, wg_ref[:, cols])
        b = _dot(h, wu_ref[:, cols])
        y = y + _dot((_silu(a) * b).astype(BF16), wd_ref[cols, :])
    if final_norm:
        y = _rms(y, gf_ref[...])
    o_ref[...] = y


def _resident(shape):
    return pl.BlockSpec(shape, lambda *_: (0,) * len(shape), pipeline_mode=pl.Buffered(1))


def ffn(x, g, wg, wu, wd, g_final, final_norm, tm, tf):
    T, D = x.shape
    F = wg.shape[1]
    assert T % tm == 0 and F % tf == 0
    return pl.pallas_call(
        functools.partial(_ffn_kernel, final_norm=final_norm, tf=tf),
        grid=(T // tm,),
        in_specs=[pl.BlockSpec((tm, D), lambda i: (i, 0)),
                  _resident((1, D)), _resident((D, F)), _resident((D, F)), _resident((F, D)), _resident((1, D))],
        out_specs=pl.BlockSpec((tm, D), lambda i: (i, 0)),
        out_shape=jax.ShapeDtypeStruct((T, D), F32),
        compiler_params=_cparams("parallel"),
    )(x, g.reshape(1, D), wg, wu, wd, g_final.reshape(1, D))


def _softmax_rows(s):
    e = jnp.exp(s - jnp.max(s, axis=-1, keepdims=True))
    return e / jnp.sum(e, axis=-1, keepdims=True)


def _xattn_prompt_kernel(x_ref, g_ref, wq_ref, k_ref, v_ref, wo_ref, o_ref, *, dh):
    scale = dh ** -0.5
    x = x_ref[...]
    q = _dot(_rms(x, g_ref[...]).astype(BF16), wq_ref[...]).astype(BF16)
    heads = []
    for h in range(X_HEADS):
        sl = slice(h * dh, (h + 1) * dh)
        s = _dot_nt(q[:, sl], k_ref[0, :, sl].astype(BF16)) * scale
        heads.append(_dot(_softmax_rows(s).astype(BF16), v_ref[0, :, sl].astype(BF16)).astype(BF16))
    o_ref[...] = x + _dot(jnp.concatenate(heads, axis=-1), wo_ref[...])


def xattn_prompt(x, g, wq, mk, mv, wo, seq, tq):
    T, D = x.shape
    B, M, _ = mk.shape
    nt = seq // tq
    return pl.pallas_call(
        functools.partial(_xattn_prompt_kernel, dh=D // X_HEADS),
        grid=(B, nt),
        in_specs=[pl.BlockSpec((tq, D), lambda b, t: (b * nt + t, 0)),
                  pl.BlockSpec((1, D), lambda b, t: (0, 0)),
                  pl.BlockSpec((D, D), lambda b, t: (0, 0)),
                  pl.BlockSpec((1, M, D), lambda b, t: (b, 0, 0)),
                  pl.BlockSpec((1, M, D), lambda b, t: (b, 0, 0)),
                  pl.BlockSpec((D, D), lambda b, t: (0, 0))],
        out_specs=pl.BlockSpec((tq, D), lambda b, t: (b * nt + t, 0)),
        out_shape=jax.ShapeDtypeStruct((T, D), F32),
        compiler_params=_cparams("parallel", "parallel"),
    )(x, g.reshape(1, D), wq, mk, mv, wo)


def _xattn_sample_kernel(q_ref, k_ref, v_ref, o_ref, *, sb):
    M, H, dh = k_ref.shape[1:]
    scale = dh ** -0.5
    row = lax.broadcasted_iota(jnp.int32, (SUBLANES, M * H), 0)
    col_head = lax.broadcasted_iota(jnp.int32, (SUBLANES, M * H), 1) % H
    own = (row % H) == col_head
    pad = jnp.zeros((SUBLANES - H, dh), F32)
    for i in range(sb):
        q8 = jnp.concatenate([q_ref[i], pad], axis=0).astype(BF16)
        s = _dot_nt(q8, k_ref[i].reshape(M * H, dh).astype(BF16)) * scale
        p = _softmax_rows(jnp.where(own, s, -jnp.inf))
        o_ref[i] = _dot(p.astype(BF16), v_ref[i].reshape(M * H, dh).astype(BF16))[0:H]


def xattn_sample(q, ck, cv, layer, sb):
    B, D = q.shape
    _, _, M, H, dh = ck.shape
    out = pl.pallas_call(
        functools.partial(_xattn_sample_kernel, sb=sb),
        grid=(B // sb,),
        in_specs=[pl.BlockSpec((sb, H, dh), lambda i: (i, 0, 0)),
                  pl.BlockSpec((None, sb, M, H, dh), lambda i: (layer, i, 0, 0, 0)),
                  pl.BlockSpec((None, sb, M, H, dh), lambda i: (layer, i, 0, 0, 0))],
        out_specs=pl.BlockSpec((sb, H, dh), lambda i: (i, 0, 0)),
        out_shape=jax.ShapeDtypeStruct((B, H, dh), F32),
        compiler_params=_cparams("parallel"),
    )(q.reshape(B, H, dh), ck, cv)
    return out.reshape(B, D)


def _lane_cat_masks(L):
    row = lax.broadcasted_iota(jnp.int32, (L, HW), 0)
    j = lax.broadcasted_iota(jnp.int32, (L, HW), 1) % DH
    r2 = lax.broadcasted_iota(jnp.int32, (HW, HW), 0) // DH
    c2 = lax.broadcasted_iota(jnp.int32, (HW, HW), 1) // DH
    return row >= j, row > j, row == j, r2 == c2


def _expand_bd(x, bd):
    return jnp.where(bd, jnp.concatenate([x] * HEADS, axis=0), jnp.zeros((), x.dtype))


def _seg_reduce(x, op, fill):
    lo = lax.broadcasted_iota(jnp.int32, (x.shape[0], LANES), 1) < DH
    parts = []
    for c in range(HW // LANES):
        xh = x[:, c * LANES:(c + 1) * LANES]
        a = op(jnp.where(lo, xh, fill), axis=-1, keepdims=True)
        b = op(jnp.where(lo, fill, xh), axis=-1, keepdims=True)
        parts.append(jnp.where(lo, a, b))
    return jnp.concatenate(parts, axis=-1)


def _head_expander(first_lane):
    r = lax.broadcasted_iota(jnp.int32, (GATE_PAD, HW), 0)
    c = lax.broadcasted_iota(jnp.int32, (GATE_PAD, HW), 1) // DH
    return (r == c + first_lane).astype(BF16)


def _chunk_tril(tt):
    r = lax.broadcasted_iota(jnp.int32, (tt, tt), 0)
    c = lax.broadcasted_iota(jnp.int32, (tt, tt), 1)
    return ((r // CHUNK == c // CHUNK) & (r >= c)).astype(BF16)


def _split3(x):
    hi = x.astype(BF16)
    r = x - hi.astype(F32)
    mid = r.astype(BF16)
    return hi, mid, (r - mid.astype(F32)).astype(BF16)


def _dot_sel_r(x, sel):
    hi, mid, lo = _split3(x)
    return (_dot(lo, sel) + _dot(mid, sel)) + _dot(hi, sel)


def _dot_sel_l(sel, x):
    hi, mid, lo = _split3(x)
    return (_dot(sel, lo) + _dot(sel, mid)) + _dot(sel, hi)


def _col_to_row(x, eye):
    return jnp.sum(jnp.where(eye, x, 0.0), axis=0, keepdims=True)


def _head_rms(x, ones_bd, g_row):
    ms = _dot_sel_r(x * x, ones_bd) * (1.0 / DH)
    return x * lax.rsqrt(ms + EPS) * g_row


def _mlstm_prompt_kernel(za_ref, gt_ref, bias_ref, norm_ref, ha_ref, c_ref, n_ref, m_ref,
                         c_scr, n_scr, m_scr, ix_scr, bx_scr, *, nchunks, unroll):
    tb = pl.program_id(1)
    L = CHUNK
    tril, _, eye, bd = _lane_cat_masks(L)
    ones_bd = bd.astype(BF16)

    @pl.when(tb == 0)
    def _():
        c_scr[...] = jnp.zeros_like(c_scr)
        n_scr[...] = jnp.zeros_like(n_scr)
        m_scr[...] = jnp.zeros_like(m_scr)

    gt = gt_ref[...] + bias_ref[...]
    b_cols = _dot_sel_l(_chunk_tril(nchunks * L), _log_sigmoid(gt))
    ix_scr[...] = _dot_sel_r(gt, _head_expander(0))
    bx_scr[...] = _dot_sel_r(b_cols, _head_expander(HEADS))

    def chunk(ci, carry):
        c_bd, n_row, m_x = carry
        rows = pl.ds(pl.multiple_of(ci * L, L), L)
        q = (za_ref[rows, 0:HW] * (DH ** -0.5)).astype(BF16)
        k = za_ref[rows, HW:2 * HW]
        v = za_ref[rows, 2 * HW:3 * HW].astype(BF16)
        og = za_ref[rows, 3 * HW:4 * HW]
        i_x = ix_scr[rows, :]
        b_x = bx_scr[rows, :]

        d_intra = jnp.where(tril, b_x - _col_to_row(b_x, eye) + _col_to_row(i_x, eye), -jnp.inf)
        d_inter = b_x + m_x
        m_t = jnp.maximum(_seg_reduce(d_intra, jnp.max, -jnp.inf), d_inter)
        w_intra = jnp.exp(d_intra - m_t)
        w_inter = jnp.exp(d_inter - m_t)
        kb = k.astype(BF16)
        s = _dot_nt(q, _expand_bd(kb, bd)) * w_intra
        num = w_inter * _dot(q, c_bd.astype(BF16)) + _dot(s.astype(BF16), _expand_bd(v, bd))
        qn = _dot((q.astype(F32) * n_row).astype(BF16), ones_bd)
        den = w_inter * qn + _seg_reduce(s, jnp.sum, 0.0)
        hh = num / jnp.maximum(jnp.abs(den), jnp.exp(-m_t))
        ha_ref[rows, :] = _head_rms(hh, ones_bd, norm_ref[...]) * _sigmoid(og)

        b_last = b_x[L - 1:L, :]
        g_x = b_last - b_x + i_x
        m_new = jnp.maximum(b_last + m_x, jnp.max(g_x, axis=0, keepdims=True))
        kw = k * jnp.exp(g_x - m_new)
        decay = jnp.exp(b_last + m_x - m_new)
        c_new = decay * c_bd + jnp.where(bd, _dot(kw.T.astype(BF16), v), 0.0)
        return c_new, decay * n_row + jnp.sum(kw, axis=0, keepdims=True), m_new

    c_bd, n_row, m_x = lax.fori_loop(0, nchunks, chunk, (c_scr[...], n_scr[...], m_scr[...]), unroll=unroll)
    c_scr[...] = c_bd
    n_scr[...] = n_row
    m_scr[...] = m_x

    @pl.when(tb == pl.num_programs(1) - 1)
    def _():
        for h in range(HEADS):
            c_ref[0, h] = c_scr[h * DH:(h + 1) * DH, h * DH:(h + 1) * DH]
        n_ref[0] = n_scr[...]
        m_ref[0] = m_scr[...]


def mlstm_prompt(za, gates, b_i, b_f, norm, B, S, tt):
    assert S % tt == 0 and tt % CHUNK == 0
    nt = S // tt
    bias = jnp.zeros((1, GATE_PAD), F32).at[0, 0:HEADS].set(b_i).at[0, HEADS:2 * HEADS].set(b_f)
    ha, c, n, m = pl.pallas_call(
        functools.partial(_mlstm_prompt_kernel, nchunks=tt // CHUNK, unroll=4),
        grid=(B, nt),
        in_specs=[pl.BlockSpec((tt, 4 * HW), lambda b, t: (b * nt + t, 0)),
                  pl.BlockSpec((tt, GATE_PAD), lambda b, t: (b * nt + t, 0)),
                  pl.BlockSpec((1, GATE_PAD), lambda b, t: (0, 0)),
                  pl.BlockSpec((1, HW), lambda b, t: (0, 0))],
        out_specs=[pl.BlockSpec((tt, HW), lambda b, t: (b * nt + t, 0)),
                   pl.BlockSpec((1, HEADS, DH, DH), lambda b, t: (b, 0, 0, 0)),
                   pl.BlockSpec((1, 1, HW), lambda b, t: (b, 0, 0)),
                   pl.BlockSpec((1, 1, HW), lambda b, t: (b, 0, 0))],
        out_shape=[jax.ShapeDtypeStruct((B * S, HW), F32),
                   jax.ShapeDtypeStruct((B, HEADS, DH, DH), F32),
                   jax.ShapeDtypeStruct((B, 1, HW), F32),
                   jax.ShapeDtypeStruct((B, 1, HW), F32)],
        scratch_shapes=[pltpu.VMEM((HW, HW), F32), pltpu.VMEM((1, HW), F32), pltpu.VMEM((1, HW), F32),
                        pltpu.VMEM((tt, HW), F32), pltpu.VMEM((tt, HW), F32)],
        compiler_params=_cparams("parallel", "arbitrary"),
    )(za, gates, bias, norm.reshape(1, HW))
    return ha, c, n.reshape(B, HEADS, DH), m[:, 0, ::DH]


def _gdn_prompt_kernel(zc_ref, zg_ref, gt_ref, cw_ref, alog_ref, dtb_ref, norm_ref,
                       hc_ref, s_ref, tail_ref,
                       s_scr, xp_scr, q_scr, k_scr, v_scr, beta_scr, g_scr, uv_scr, wq_scr, qkm_scr, kwt_scr,
                       *, nchunks, group, unroll_b):
    tb = pl.program_id(1)
    L = CHUNK
    tt = nchunks * L
    pad = SUBLANES
    tril, strict, eye, bd = _lane_cat_masks(L)
    ones_bd = bd.astype(BF16)

    @pl.when(tb == 0)
    def _():
        s_scr[...] = jnp.zeros_like(s_scr)
        xp_scr[0:pad, :] = jnp.zeros((pad, 3 * HW), F32)

    x = zc_ref[...]
    xp_scr[pad:pad + tt, :] = x
    y = cw_ref[CONV_K - 1:CONV_K, :] * x
    for j in range(CONV_K - 1):
        y = y + cw_ref[j:j + 1, :] * xp_scr[pl.ds(pad - (CONV_K - 1) + j, tt), :]
    xp_scr[0:pad, :] = x[tt - pad:tt, :]
    y = _silu(y)
    q_raw, k_raw = y[:, 0:HW], y[:, HW:2 * HW]
    q_scr[...] = (q_raw * lax.rsqrt(_dot_sel_r(q_raw * q_raw, ones_bd) + EPS) * (DH ** -0.5)).astype(BF16)
    k_scr[...] = k_raw * lax.rsqrt(_dot_sel_r(k_raw * k_raw, ones_bd) + EPS)
    v_scr[...] = y[:, 2 * HW:3 * HW]
    gt = gt_ref[...]
    beta_scr[...] = _dot_sel_r(_sigmoid(gt), _head_expander(2 * HEADS))
    la_cols = -jnp.exp(alog_ref[...]) * _softplus(gt + dtb_ref[...])
    g_scr[...] = _dot_sel_r(_dot_sel_l(_chunk_tril(tt), la_cols), _head_expander(3 * HEADS))

    def prepare(gi, carry):
        cis = [gi * group + j for j in range(group)]
        rows = [pl.ds(pl.multiple_of(ci * L, L), L) for ci in cis]
        k = [k_scr[r, :] for r in rows]
        g_x = [g_scr[r, :] for r in rows]
        beta_row = [_col_to_row(beta_scr[r, :], eye) for r in rows]
        dec_incl = [jnp.where(tril, jnp.exp(jnp.where(tril, g - _col_to_row(g, eye), 0.0)), 0.0) for g in g_x]
        k_bd = [_expand_bd(kk.astype(BF16), bd) for kk in k]
        n0 = [-(jnp.where(strict, d, 0.0) * _dot_nt(kk.astype(BF16), kbd) * br)
              for d, kk, kbd, br in zip(dec_incl, k, k_bd, beta_row)]
        for r, d, kbd, br in zip(rows, dec_incl, k_bd, beta_row):
            qkm_scr[r, :] = (_dot_nt(q_scr[r, :], kbd) * d * br).astype(BF16)

        p = [_dot(n.astype(BF16), _expand_bd(n.astype(BF16), bd)) for n in n0]
        m = n0
        steps = int(math.log2(L)) - 1
        for i in range(steps):
            p_bd = [_expand_bd(pp.astype(BF16), bd) for pp in p]
            if i < steps - 1:
                pm = [_dot(jnp.concatenate([pp, mm], axis=0).astype(BF16), pbd) for pp, mm, pbd in zip(p, m, p_bd)]
                p_next, mp = [x[0:L] for x in pm], [x[L:2 * L] for x in pm]
            else:
                p_next, mp = None, [_dot(mm.astype(BF16), pbd) for mm, pbd in zip(m, p_bd)]
            m = [mm + pp + x for mm, pp, x in zip(m, p, mp)]
            p = p_next

        for ci, r, kk, g, mm in zip(cis, rows, k, g_x, m):
            v = v_scr[r, :]
            egk = jnp.exp(g) * kk
            rhs_bd = jnp.concatenate([_expand_bd(v.astype(BF16), bd), _expand_bd(egk.astype(BF16), bd)], axis=1)
            mr = _dot(mm.astype(BF16), rhs_bd)
            uv_scr[r, :] = v + mr[:, 0:HW]
            wq_rows = pl.multiple_of(ci * 2 * L, 2 * L)
            wq_scr[pl.ds(wq_rows, L), :] = (egk + mr[:, HW:2 * HW]).astype(BF16)
            wq_scr[pl.ds(wq_rows + L, L), :] = q_scr[r, :]
            kw = kk * (jnp.exp(g[L - 1:L, :] - g) * beta_scr[r, :])
            kwt_scr[pl.ds(pl.multiple_of(ci * HW, HW), HW), :] = kw.T.astype(BF16)
        return carry

    lax.fori_loop(0, nchunks // group, prepare, 0)

    def advance(ci, s_bd):
        rows = pl.ds(pl.multiple_of(ci * L, L), L)
        g_x = g_scr[rows, :]
        wqs = _dot(wq_scr[pl.ds(pl.multiple_of(ci * 2 * L, 2 * L), 2 * L), :], s_bd.astype(BF16))
        ub = (uv_scr[rows, :] - wqs[0:L]).astype(BF16)
        o = jnp.exp(g_x) * wqs[L:2 * L] + _dot(qkm_scr[rows, :], _expand_bd(ub, bd))
        hc_ref[rows, :] = _head_rms(o, ones_bd, norm_ref[...]) * _silu(zg_ref[rows, :])
        kwt = kwt_scr[pl.ds(pl.multiple_of(ci * HW, HW), HW), :]
        return jnp.exp(g_x[L - 1:L, :]) * s_bd + jnp.where(bd, _dot(kwt, ub), 0.0)

    s_scr[...] = lax.fori_loop(0, nchunks, advance, s_scr[...], unroll=unroll_b)

    @pl.when(tb == pl.num_programs(1) - 1)
    def _():
        for h in range(HEADS):
            s_ref[0, h] = s_scr[h * DH:(h + 1) * DH, h * DH:(h + 1) * DH]
        tail_ref[0] = xp_scr[0:pad, :]


def _gdn_gate_rows(a_log, dt_bias):
    z = jnp.zeros((1, GATE_PAD), F32)
    return (z.at[0, 3 * HEADS:4 * HEADS].set(a_log), z.at[0, 3 * HEADS:4 * HEADS].set(dt_bias))


def gdn_prompt(zc, zg, gates, conv_w, a_log, dt_bias, norm, B, S, tt):
    assert S % tt == 0 and tt % CHUNK == 0
    nt = S // tt
    alog, dtb = _gdn_gate_rows(a_log, dt_bias)
    hc, s, tail = pl.pallas_call(
        functools.partial(_gdn_prompt_kernel, nchunks=tt // CHUNK, group=math.gcd(tt // CHUNK, 4), unroll_b=2),
        grid=(B, nt),
        in_specs=[pl.BlockSpec((tt, 3 * HW), lambda b, t: (b * nt + t, 0)),
                  pl.BlockSpec((tt, HW), lambda b, t: (b * nt + t, 0)),
                  pl.BlockSpec((tt, GATE_PAD), lambda b, t: (b * nt + t, 0)),
                  pl.BlockSpec((CONV_K, 3 * HW), lambda b, t: (0, 0)),
                  pl.BlockSpec((1, GATE_PAD), lambda b, t: (0, 0)),
                  pl.BlockSpec((1, GATE_PAD), lambda b, t: (0, 0)),
                  pl.BlockSpec((1, HW), lambda b, t: (0, 0))],
        out_specs=[pl.BlockSpec((tt, HW), lambda b, t: (b * nt + t, 0)),
                   pl.BlockSpec((1, HEADS, DH, DH), lambda b, t: (b, 0, 0, 0)),
                   pl.BlockSpec((1, SUBLANES, 3 * HW), lambda b, t: (b, 0, 0))],
        out_shape=[jax.ShapeDtypeStruct((B * S, HW), F32),
                   jax.ShapeDtypeStruct((B, HEADS, DH, DH), F32),
                   jax.ShapeDtypeStruct((B, SUBLANES, 3 * HW), F32)],
        scratch_shapes=[pltpu.VMEM((HW, HW), F32), pltpu.VMEM((tt + SUBLANES, 3 * HW), F32),
                        pltpu.VMEM((tt, HW), BF16), pltpu.VMEM((tt, HW), F32), pltpu.VMEM((tt, HW), F32),
                        pltpu.VMEM((tt, HW), F32), pltpu.VMEM((tt, HW), F32), pltpu.VMEM((tt, HW), F32),
                        pltpu.VMEM((2 * tt, HW), BF16), pltpu.VMEM((tt, HW), BF16),
                        pltpu.VMEM((tt // CHUNK * HW, CHUNK), BF16)],
        compiler_params=_cparams("parallel", "arbitrary"),
    )(zc, zg, gates, conv_w, alog, dtb, jnp.tile(norm, HEADS).reshape(1, HW))
    return hc, s, tail[:, SUBLANES - (CONV_K - 1):, :]


def _s5_prep_kernel(are_ref, aim_ref, ldt_ref, bre_ref, bim_ref, lre_ref, lim_ref, bbre_ref, bbim_ref):
    a_re, a_im = are_ref[...], aim_ref[...]
    dt = jnp.exp(ldt_ref[...])
    mag = jnp.exp(a_re * dt)
    lam_re, lam_im = mag * jnp.cos(a_im * dt), mag * jnp.sin(a_im * dt)
    lre_ref[...] = lam_re
    lim_ref[...] = lam_im
    nr, ni = lam_re - 1.0, lam_im
    den = a_re * a_re + a_im * a_im
    coef_re = (nr * a_re + ni * a_im) / den
    coef_im = (ni * a_re - nr * a_im) / den
    b_re, b_im = bre_ref[...], bim_ref[...]
    bbre_ref[...] = coef_re * b_re - coef_im * b_im
    bbim_ref[...] = coef_re * b_im + coef_im * b_re


def s5_params(lp):
    G, N, P = S5_GROUPS, S5_N, S5_P
    row = lambda a: a.astype(F32).reshape(1, G * N)
    to_pn = lambda b: jnp.transpose(b.astype(F32), (2, 0, 1)).reshape(P, G * N)
    shp = [jax.ShapeDtypeStruct((1, G * N), F32)] * 2 + [jax.ShapeDtypeStruct((P, G * N), F32)] * 2
    lam_re, lam_im, bb_re, bb_im = pl.pallas_call(_s5_prep_kernel, out_shape=shp)(
        row(lp['s5_a_re']), row(lp['s5_a_im']), row(jnp.repeat(lp['s5_log_dt'][:, None], N, axis=1)),
        to_pn(lp['s5_b_re']), to_pn(lp['s5_b_im']))
    eye = jnp.eye(S5_GB, dtype=F32)

    def w_in_blocks(bb):
        b4 = bb.reshape(P, S5_NBLK, S5_GB, N)
        return jnp.einsum('pbgn,gh->bgphn', b4, eye).reshape(S5_NBLK, S5_GB * P, S5_GB * N)

    def w_out_blocks(c):
        c4 = c.astype(F32).reshape(S5_NBLK, S5_GB, P, N)
        return jnp.einsum('bgpn,gh->bgnhp', c4, eye).reshape(S5_NBLK, S5_GB * N, S5_GB * P)

    w_in = jnp.concatenate([w_in_blocks(bb_re), w_in_blocks(bb_im)], axis=-1).astype(BF16)
    return {'lam_re': lam_re, 'lam_im': lam_im, 'w_in': w_in,
            'w_out_re': w_out_blocks(lp['s5_c_re']).astype(BF16),
            'w_out_im': (-w_out_blocks(lp['s5_c_im'])).astype(BF16),
            'd': lp['s5_d'].astype(F32).reshape(1, S5_WIDTH), 'w_glu': lp['s5_w_glu'].astype(BF16)}


def _s5_kernel(u_ref, h0r_ref, h0i_ref, lamr_ref, lami_ref, win_ref, wor_ref, woi_ref, d_ref, wglu_ref,
               ys_ref, h1r_ref, h1i_ref, hr_scr, hi_scr, br_scr, bi_scr, *, nseq, rows, bb, lane_blk):
    tb = pl.program_id(0)
    nsteps = nseq * rows // bb
    blk_in, blk_st = S5_GB * S5_P, S5_GB * S5_N

    @pl.when(tb == 0)
    def _():
        hr_scr[...] = h0r_ref[...]
        hi_scr[...] = h0i_ref[...]

    if nseq > 1:
        u = jnp.swapaxes(u_ref[...], 0, 1).reshape(nseq * rows, S5_WIDTH)
    else:
        u = u_ref[0]
    ub = u.astype(BF16)
    for blk in range(S5_NBLK):
        bu = _dot(ub[:, blk * blk_in:(blk + 1) * blk_in], win_ref[blk])
        br_scr[:, blk * blk_st:(blk + 1) * blk_st] = bu[:, 0:blk_st]
        bi_scr[:, blk * blk_st:(blk + 1) * blk_st] = bu[:, blk_st:2 * blk_st]

    for lb in range(S5_STATE // lane_blk):
        ls = slice(lb * lane_blk, (lb + 1) * lane_blk)
        lr = jnp.broadcast_to(lamr_ref[:, ls], (bb, lane_blk))
        li = jnp.broadcast_to(lami_ref[:, ls], (bb, lane_blk))

        def step(t, carry):
            hr, hi = carry
            r = pl.ds(pl.multiple_of(t * bb, bb), bb)
            nr = lr * hr - li * hi + br_scr[r, ls]
            ni = lr * hi + li * hr + bi_scr[r, ls]
            br_scr[r, ls] = nr
            bi_scr[r, ls] = ni
            return nr, ni

        hr, hi = lax.fori_loop(0, nsteps, step, (hr_scr[:, ls], hi_scr[:, ls]), unroll=min(nsteps, 8))
        hr_scr[:, ls] = hr
        hi_scr[:, ls] = hi

    ys = []
    for blk in range(S5_NBLK):
        st = slice(blk * blk_st, (blk + 1) * blk_st)
        ys.append(_dot(br_scr[:, st].astype(BF16), wor_ref[blk]) + _dot(bi_scr[:, st].astype(BF16), woi_ref[blk]))
    gy = jax.nn.gelu(jnp.concatenate(ys, axis=-1) + d_ref[...] * u)
    out = gy * _sigmoid(_dot(gy.astype(BF16), wglu_ref[...]))
    if nseq > 1:
        ys_ref[...] = jnp.swapaxes(out.reshape(rows, nseq, S5_WIDTH), 0, 1)
    else:
        ys_ref[0] = out

    @pl.when(tb == pl.num_programs(0) - 1)
    def _():
        h1r_ref[...] = hr_scr[...]
        h1i_ref[...] = hi_scr[...]


def s5_mixer(u, h0_re, h0_im, sp, tt, single_step):
    B = h0_re.shape[0]
    nseq, S = (1, 1) if single_step else (B, u.shape[1])
    rows = B if single_step else tt
    assert S % tt == 0 and B % SUBLANES == 0
    assert u.shape == ((1, B, S5_WIDTH) if single_step else (B, S, S5_WIDTH))
    lane_blk = max(LANES, min(S5_STATE, (SUBLANES * SUBLANES * LANES) // B))
    full = lambda shape: pl.BlockSpec(shape, lambda t: (0,) * len(shape))
    return pl.pallas_call(
        functools.partial(_s5_kernel, nseq=nseq, rows=rows, bb=B, lane_blk=lane_blk),
        grid=(S // tt,),
        in_specs=[pl.BlockSpec((nseq, rows, S5_WIDTH), lambda t: (0, t, 0)),
                  full((B, S5_STATE)), full((B, S5_STATE)), full((1, S5_STATE)), full((1, S5_STATE)),
                  full(sp['w_in'].shape), full(sp['w_out_re'].shape), full(sp['w_out_im'].shape),
                  full((1, S5_WIDTH)), full((S5_WIDTH, S5_WIDTH))],
        out_specs=[pl.BlockSpec((nseq, rows, S5_WIDTH), lambda t: (0, t, 0)),
                   full((B, S5_STATE)), full((B, S5_STATE))],
        out_shape=[jax.ShapeDtypeStruct(u.shape, F32),
                   jax.ShapeDtypeStruct((B, S5_STATE), F32), jax.ShapeDtypeStruct((B, S5_STATE), F32)],
        scratch_shapes=[pltpu.VMEM((B, S5_STATE), F32), pltpu.VMEM((B, S5_STATE), F32),
                        pltpu.VMEM((nseq * rows, S5_STATE), F32), pltpu.VMEM((nseq * rows, S5_STATE), F32)],
        compiler_params=_cparams("arbitrary"),
    )(u, h0_re, h0_im, sp['lam_re'], sp['lam_im'], sp['w_in'], sp['w_out_re'], sp['w_out_im'], sp['d'], sp['w_glu'])


def _ones_bd():
    r = lax.broadcasted_iota(jnp.int32, (HW, HW), 0) // DH
    c = lax.broadcasted_iota(jnp.int32, (HW, HW), 1) // DH
    return (r == c).astype(BF16)


def _mlstm_sample_kernel(za_ref, gt_ref, bias_ref, norm_ref, c_ref, n_ref, m_ref,
                         ha_ref, c_out, n_out, m_out, q_scr, kw_scr, h_scr):
    za = za_ref[...]
    q_scr[...] = (za[:, 0:HW] * (DH ** -0.5)).T
    k_t = za[:, HW:2 * HW].T
    v_t = za[:, 2 * HW:3 * HW].T
    g_t = (gt_ref[...] + bias_ref[...]).T
    m_out[...] = jnp.zeros_like(m_out)
    for h in range(HEADS):
        hs = slice(h * DH, (h + 1) * DH)
        i_h = g_t[h:h + 1, :]
        bm = _log_sigmoid(g_t[HEADS + h:HEADS + h + 1, :]) + m_ref[h:h + 1, :]
        m_t = jnp.maximum(i_h, bm)
        w_in = jnp.exp(i_h - m_t)
        w_st = jnp.exp(bm - m_t)
        q_h, k_h, v_h = q_scr[hs, :], k_t[hs, :], v_t[hs, :]
        s = jnp.sum(q_h * k_h, axis=0, keepdims=True) * w_in
        kw_scr[hs, :] = k_h * w_in

        def body(d, acc):
            r = h * DH + d
            rows = pl.ds(pl.multiple_of(r * DH, DH), DH)
            c_hd = c_ref[rows, :]
            c_out[rows, :] = w_st * c_hd + kw_scr[pl.ds(r, 1), :] * v_h
            return acc + q_scr[pl.ds(r, 1), :] * c_hd

        qc = lax.fori_loop(0, DH, body, jnp.zeros((DH, za.shape[0]), F32), unroll=4)
        n_h = n_ref[hs, :]
        num = w_st * qc + s * v_h
        den = w_st * jnp.sum(q_h * n_h, axis=0, keepdims=True) + s
        h_scr[hs, :] = num / jnp.maximum(jnp.abs(den), jnp.exp(-m_t))
        n_out[hs, :] = w_st * n_h + kw_scr[hs, :]
        m_out[h:h + 1, :] = m_t
    ha_ref[...] = _head_rms(h_scr[...].T, _ones_bd(), norm_ref[...]) * _sigmoid(za[:, 3 * HW:4 * HW])


def mlstm_sample(za, gates, b_i, b_f, norm, c_t, n_t, m_t):
    B = za.shape[0]
    bias = jnp.zeros((1, GATE_PAD), F32).at[0, 0:HEADS].set(b_i).at[0, HEADS:2 * HEADS].set(b_f)
    shp = lambda *s: jax.ShapeDtypeStruct(s, F32)
    return pl.pallas_call(
        _mlstm_sample_kernel,
        out_shape=[shp(B, HW), shp(HW * DH, B), shp(HW, B), shp(SUBLANES, B)],
        scratch_shapes=[pltpu.VMEM((HW, B), F32), pltpu.VMEM((HW, B), F32), pltpu.VMEM((HW, B), F32)],
        compiler_params=pltpu.CompilerParams(vmem_limit_bytes=VMEM_LIMIT),
    )(za, gates, bias, norm.reshape(1, HW), c_t, n_t, m_t)


def _gdn_sample_kernel(zc_ref, zg_ref, gt_ref, buf_ref, cw_ref, alog_ref, dtb_ref, norm_ref, s_ref,
                       hc_ref, s_out, buf_out, q_scr, k_scr, o_scr):
    W3 = 3 * HW
    x = zc_ref[...]
    y = cw_ref[CONV_K - 1:CONV_K, :] * x
    for j in range(CONV_K - 1):
        y = y + cw_ref[j:j + 1, :] * buf_ref[:, j * W3:(j + 1) * W3]
    buf_out[:, 0:(CONV_K - 2) * W3] = buf_ref[:, W3:(CONV_K - 1) * W3]
    buf_out[:, (CONV_K - 2) * W3:(CONV_K - 1) * W3] = x
    y = _silu(y)
    ones_bd = _ones_bd()
    q_raw, k_raw = y[:, 0:HW], y[:, HW:2 * HW]
    q_scr[...] = (q_raw * lax.rsqrt(_dot_sel_r(q_raw * q_raw, ones_bd) + EPS) * (DH ** -0.5)).T
    k_scr[...] = (k_raw * lax.rsqrt(_dot_sel_r(k_raw * k_raw, ones_bd) + EPS)).T
    v_t = y[:, 2 * HW:3 * HW].T
    gt = gt_ref[...]
    beta_t = _sigmoid(gt).T
    la_t = (-jnp.exp(alog_ref[...]) * _softplus(gt + dtb_ref[...])).T
    nb = x.shape[0]
    for h in range(HEADS):
        hs = slice(h * DH, (h + 1) * DH)
        beta = beta_t[2 * HEADS + h:2 * HEADS + h + 1, :]
        eg = jnp.exp(la_t[3 * HEADS + h:3 * HEADS + h + 1, :])
        q_h, k_h, v_h = q_scr[hs, :], k_scr[hs, :], v_t[hs, :]

        def read(d, acc):
            ks, qs = acc
            r = h * DH + d
            s_hd = s_ref[pl.ds(pl.multiple_of(r * DH, DH), DH), :]
            return ks + k_scr[pl.ds(r, 1), :] * s_hd, qs + q_scr[pl.ds(r, 1), :] * s_hd

        zero = jnp.zeros((DH, nb), F32)
        ks, qs = lax.fori_loop(0, DH, read, (zero, zero), unroll=4)
        u = v_h - eg * ks
        o_scr[hs, :] = eg * qs + (jnp.sum(q_h * k_h, axis=0, keepdims=True) * beta) * u

        def write(d, carry):
            r = h * DH + d
            rows = pl.ds(pl.multiple_of(r * DH, DH), DH)
            s_out[rows, :] = eg * s_ref[rows, :] + (beta * k_scr[pl.ds(r, 1), :]) * u
            return carry

        lax.fori_loop(0, DH, write, 0, unroll=4)
    hc_ref[...] = _head_rms(o_scr[...].T, ones_bd, norm_ref[...]) * _silu(zg_ref[...])


def gdn_sample(zc, zg, gates, buf, conv_w, a_log, dt_bias, norm, s_t):
    B = zc.shape[0]
    alog, dtb = _gdn_gate_rows(a_log, dt_bias)
    shp = lambda *s: jax.ShapeDtypeStruct(s, F32)
    return pl.pallas_call(
        _gdn_sample_kernel,
        out_shape=[shp(B, HW), shp(HW * DH, B), shp(B, (CONV_K - 1) * 3 * HW)],
        scratch_shapes=[pltpu.VMEM((HW, B), F32), pltpu.VMEM((HW, B), F32), pltpu.VMEM((HW, B), F32)],
        compiler_params=pltpu.CompilerParams(vmem_limit_bytes=VMEM_LIMIT),
    )(zc, zg, gates, buf, conv_w, alog, dtb, jnp.tile(norm, HEADS).reshape(1, HW), s_t)


FFN_CHUNK = 1408
IN_SEGMENTS = (4 * HW, S5_WIDTH, 3 * HW, HW, GATE_PAD)


def _tile(n, pref):
    return pref if n % pref == 0 else n


def prep_layer_weights(lp):
    w = lp['w_in']
    a, g4 = 4 * HW, HEADS
    o_u = a + 2 * g4
    o_c = o_u + S5_WIDTH
    o_g = o_c + 3 * HW
    o_b = o_g + HW
    gate_cols = jnp.concatenate([w[:, a:a + 2 * g4], w[:, o_b:o_b + 2 * g4],
                                 jnp.zeros((w.shape[0], GATE_PAD - 4 * g4), w.dtype)], axis=1)
    w_in = jnp.concatenate([w[:, 0:a], w[:, o_u:o_c], w[:, o_c:o_g], w[:, o_g:o_b], gate_cols], axis=1)
    bf = lambda n: lp[n].astype(BF16)
    return {'w_in': w_in.astype(BF16), 'w_out': bf('w_out'), 'w_mq': bf('w_mq'), 'w_mo': bf('w_mo'),
            'w_mkv': jnp.concatenate([lp['w_mk'], lp['w_mv']], axis=1).astype(BF16),
            'w_gate': bf('w_gate'), 'w_up': bf('w_up'), 'w_down': bf('w_down'), 's5': s5_params(lp)}


def mixer_prompt(x, lp, W, B, S):
    T = B * S
    tm = _tile(T, 512)
    za, zu, zc, zg, gates = norm_matmul(x, lp['norm_mix'], W['w_in'], IN_SEGMENTS, tm)
    tt = _tile(S, 512)
    ha, c1, n1, m1 = mlstm_prompt(za, gates, lp['mlstm_b_i'], lp['mlstm_b_f'], lp['mlstm_norm'], B, S, tt)
    h0 = jnp.zeros((B, S5_STATE), F32)
    ys3, r1, i1 = s5_mixer(zu.reshape(B, S, S5_WIDTH), h0, h0, W['s5'], _tile(S, 128), False)
    ys = ys3.reshape(T, S5_WIDTH)
    hc, s1, buf1 = gdn_prompt(zc, zg, gates, lp['gdn_conv_w'], lp['gdn_a_log'], lp['gdn_dt_bias'], lp['gdn_norm'], B, S, tt)
    x1 = matmul_residual(x, [ha, ys, hc], W['w_out'], tm)
    return x1, (c1, n1, m1, r1.reshape(B, S5_GROUPS, S5_N), i1.reshape(B, S5_GROUPS, S5_N), s1, buf1)


def mixer_sample(x, st, lp, W):
    B = x.shape[0]
    c0, n0, m0, r0, i0, s0, buf0 = st
    za, zu, zc, zg, gates = norm_matmul(x, lp['norm_mix'], W['w_in'], IN_SEGMENTS, B)
    m_t = jnp.zeros((SUBLANES, B), F32).at[0:HEADS, :].set(m0.T)
    ha, c1t, n1t, m1t = mlstm_sample(za, gates, lp['mlstm_b_i'], lp['mlstm_b_f'], lp['mlstm_norm'],
                                     c0.reshape(B, HW * DH).T, n0.reshape(B, HW).T, m_t)
    ys3, r1, i1 = s5_mixer(zu.reshape(1, B, S5_WIDTH), r0.reshape(B, S5_STATE), i0.reshape(B, S5_STATE), W['s5'], 1, True)
    hc, s1t, buf1 = gdn_sample(zc, zg, gates, buf0.reshape(B, (CONV_K - 1) * 3 * HW), lp['gdn_conv_w'],
                               lp['gdn_a_log'], lp['gdn_dt_bias'], lp['gdn_norm'], s0.reshape(B, HW * DH).T)
    x1 = matmul_residual(x, [ha, ys3.reshape(B, S5_WIDTH), hc], W['w_out'], B)
    return x1, (c1t.T.reshape(B, HEADS, DH, DH), n1t.T.reshape(B, HEADS, DH), m1t[0:HEADS, :].T,
                r1.reshape(B, S5_GROUPS, S5_N), i1.reshape(B, S5_GROUPS, S5_N),
                s1t.T.reshape(B, HEADS, DH, DH), buf1.reshape(B, CONV_K - 1, 3 * HW))


def mem_kv(mem, lp, W):
    D = mem.shape[1]
    return norm_matmul(mem, lp['norm_mem'], W['w_mkv'], (D, D), _tile(mem.shape[0], 512))


def xattn_ffn_prompt(x, mk, mv, lp, W, S, norm_final, final):
    T, D = x.shape
    x2 = xattn_prompt(x, lp['norm_xattn'], W['w_mq'], mk, mv, W['w_mo'], S, _tile(S, 512))
    return ffn(x2, lp['norm_ffn'], W['w_gate'], W['w_up'], W['w_down'], norm_final, final, _tile(T, 512), FFN_CHUNK)


def xattn_ffn_sample(x, ck, cv, layer, lp, W, norm_final, final):
    B, D = x.shape
    (q,) = norm_matmul(x, lp['norm_xattn'], W['w_mq'], (D,), B)
    o = xattn_sample(q, ck, cv, layer, 4)
    x2 = matmul_residual(x, [o], W['w_mo'], B)
    return ffn(x2, lp['norm_ffn'], W['w_gate'], W['w_up'], W['w_down'], norm_final, final, B, FFN_CHUNK)


LAYER_PARAMS = ('norm_mix', 'w_in', 'w_out', 'mlstm_b_i', 'mlstm_b_f', 'mlstm_norm', 's5_a_re', 's5_a_im', 's5_log_dt',
                's5_b_re', 's5_b_im', 's5_c_re', 's5_c_im', 's5_d', 's5_w_glu', 'gdn_conv_w', 'gdn_a_log',
                'gdn_dt_bias', 'gdn_norm', 'norm_xattn', 'norm_mem', 'w_mq', 'w_mk', 'w_mv', 'w_mo', 'norm_ffn',
                'w_gate', 'w_up', 'w_down')


def kernel(x_prompt, x_sample, mem_prompt, cache_mem_k, cache_mem_v, state_mlstm_c, state_mlstm_n, state_mlstm_m, state_s5_re, state_s5_im, state_gdn, state_gdn_conv, norm_mix, w_in, w_out, mlstm_b_i, mlstm_b_f, mlstm_norm, s5_a_re, s5_a_im, s5_log_dt, s5_b_re, s5_b_im, s5_c_re, s5_c_im, s5_d, s5_w_glu, gdn_conv_w, gdn_a_log, gdn_dt_bias, gdn_norm, norm_xattn, norm_mem, w_mq, w_mk, w_mv, w_mo, norm_ffn, w_gate, w_up, w_down, norm_final):
    stacked = dict(norm_mix=norm_mix, w_in=w_in, w_out=w_out, mlstm_b_i=mlstm_b_i, mlstm_b_f=mlstm_b_f,
                   mlstm_norm=mlstm_norm, s5_a_re=s5_a_re, s5_a_im=s5_a_im, s5_log_dt=s5_log_dt, s5_b_re=s5_b_re,
                   s5_b_im=s5_b_im, s5_c_re=s5_c_re, s5_c_im=s5_c_im, s5_d=s5_d, s5_w_glu=s5_w_glu,
                   gdn_conv_w=gdn_conv_w, gdn_a_log=gdn_a_log, gdn_dt_bias=gdn_dt_bias, gdn_norm=gdn_norm,
                   norm_xattn=norm_xattn, norm_mem=norm_mem, w_mq=w_mq, w_mk=w_mk, w_mv=w_mv, w_mo=w_mo,
                   norm_ffn=norm_ffn, w_gate=w_gate, w_up=w_up, w_down=w_down)
    B, S, D = x_prompt.shape
    Bs = x_sample.shape[0]
    M = mem_prompt.shape[1]
    depth = w_in.shape[0]
    xp = x_prompt.reshape(B * S, D)
    xs = x_sample.reshape(Bs, D)
    mem = mem_prompt.reshape(B * M, D)
    cache_k, cache_v = cache_mem_k, cache_mem_v
    mem_k, mem_v, st_p, st_s = [], [], [], []
    for l in range(depth):
        lp = {n: stacked[n][l] for n in LAYER_PARAMS}
        W = prep_layer_weights(lp)
        last = l == depth - 1
        xp, sp = mixer_prompt(xp, lp, W, B, S)
        mk, mv = mem_kv(mem, lp, W)
        xp = xattn_ffn_prompt(xp, mk.reshape(B, M, D), mv.reshape(B, M, D), lp, W, S, norm_final, last)
        st_in = (state_mlstm_c[l], state_mlstm_n[l], state_mlstm_m[l], state_s5_re[l], state_s5_im[l],
                 state_gdn[l], state_gdn_conv[l])
        xs, ss = mixer_sample(xs, st_in, lp, W)
        xs = xattn_ffn_sample(xs, cache_k, cache_v, l, lp, W, norm_final, last)
        mem_k.append(mk.reshape(B, M, X_HEADS, D // X_HEADS))
        mem_v.append(mv.reshape(B, M, X_HEADS, D // X_HEADS))
        st_p.append(sp)
        st_s.append(ss)
    stack = lambda lst: [jnp.stack([st[i] for st in lst]) for i in range(7)]
    return (xp.reshape(B, S, D), xs.reshape(Bs, 1, D), jnp.stack(mem_k), jnp.stack(mem_v),
            *stack(st_p), *stack(st_s))
```

```python
import functools
import math

import jax
import jax.numpy as jnp
from jax import lax
from jax.experimental import pallas as pl
from jax.experimental.pallas import tpu as pltpu

F32 = jnp.float32
BF16 = jnp.bfloat16
EPS = 1e-6

HEADS = 4
DH = 64
HW = HEADS * DH
CHUNK = 64
S5_P = 16
S5_N = 64
S5_GROUPS = 32
S5_WIDTH = S5_GROUPS * S5_P
S5_STATE = S5_GROUPS * S5_N
S5_GB = 8
S5_NBLK = S5_GROUPS // S5_GB
CONV_K = 4
GDN_SEQS = 4
X_HEADS = 4
GATE_PAD = 128
LANES = 128
SUBLANES = 8
VMEM_LIMIT = 48 * 1024 * 1024


def _cparams(*sem):
    return pltpu.CompilerParams(dimension_semantics=sem, vmem_limit_bytes=VMEM_LIMIT)


def _rms(x, g_row):
    return x * lax.rsqrt(jnp.mean(x * x, axis=-1, keepdims=True) + EPS) * g_row


def _dot(a, b):
    return jnp.dot(a, b, preferred_element_type=F32)


def _dot_nt(a, b):
    return lax.dot_general(a, b, (((1,), (1,)), ((), ())), preferred_element_type=F32)


def _sigmoid(x):
    return 1.0 / (1.0 + jnp.exp(-x))


def _silu(x):
    return x * _sigmoid(x)


def _softplus(x):
    return jnp.maximum(x, 0.0) + jnp.log1p(jnp.exp(-jnp.abs(x)))


def _log_sigmoid(x):
    return jnp.minimum(x, 0.0) - jnp.log1p(jnp.exp(-jnp.abs(x)))


def _norm_matmul_kernel(x_ref, g_ref, w_ref, *out_refs, splits):
    xn = _rms(x_ref[...], g_ref[...]).astype(BF16)
    off = 0
    for o_ref, n in zip(out_refs, splits):
        o_ref[...] = _dot(xn, w_ref[:, off:off + n])
        off += n


def norm_matmul(x, g, w_bf16, splits, tm):
    T, D = x.shape
    N = w_bf16.shape[1]
    assert sum(splits) == N and T % tm == 0
    return pl.pallas_call(
        functools.partial(_norm_matmul_kernel, splits=tuple(splits)),
        grid=(T // tm,),
        in_specs=[pl.BlockSpec((tm, D), lambda i: (i, 0)),
                  pl.BlockSpec((1, D), lambda i: (0, 0)),
                  pl.BlockSpec((D, N), lambda i: (0, 0))],
        out_specs=[pl.BlockSpec((tm, n), lambda i: (i, 0)) for n in splits],
        out_shape=[jax.ShapeDtypeStruct((T, n), F32) for n in splits],
        compiler_params=_cparams("parallel"),
    )(x, g.reshape(1, D), w_bf16)


def _matmul_residual_kernel(x_ref, *refs, ksplits):
    a_refs, w_ref, o_ref = refs[:-2], refs[-2], refs[-1]
    acc = x_ref[...]
    off = 0
    for a_ref, k in zip(a_refs, ksplits):
        acc = acc + _dot(a_ref[...].astype(BF16), w_ref[off:off + k, :])
        off += k
    o_ref[...] = acc


def matmul_residual(x, acts, w_bf16, tm):
    T, D = x.shape
    ks = tuple(a.shape[1] for a in acts)
    K = w_bf16.shape[0]
    assert sum(ks) == K and T % tm == 0
    return pl.pallas_call(
        functools.partial(_matmul_residual_kernel, ksplits=ks),
        grid=(T // tm,),
        in_specs=[pl.BlockSpec((tm, D), lambda i: (i, 0))]
                 + [pl.BlockSpec((tm, k), lambda i: (i, 0)) for k in ks]
                 + [pl.BlockSpec((K, D), lambda i: (0, 0))],
        out_specs=pl.BlockSpec((tm, D), lambda i: (i, 0)),
        out_shape=jax.ShapeDtypeStruct((T, D), F32),
        compiler_params=_cparams("parallel"),
    )(x, *acts, w_bf16)


def _ffn_kernel(x_ref, g_ref, wg_ref, wu_ref, wd_ref, gf_ref, o_ref, *, final_norm, tf):
    x = x_ref[...]
    h = _rms(x, g_ref[...]).astype(BF16)
    y = x
    for j in range(wg_ref.shape[1] // tf):
        cols = slice(j * tf, (j + 1) * tf)
        a = _dot(h, wg_ref[:, cols])
        b = _dot(h, wu_ref[:, cols])
        y = y + _dot((_silu(a) * b).astype(BF16), wd_ref[cols, :])
    if final_norm:
        y = _rms(y, gf_ref[...])
    o_ref[...] = y


def _resident(shape):
    return pl.BlockSpec(shape, lambda *_: (0,) * len(shape), pipeline_mode=pl.Buffered(1))


def ffn(x, g, wg, wu, wd, g_final, final_norm, tm, tf):
    T, D = x.shape
    F = wg.shape[1]
    assert T % tm == 0 and F % tf == 0
    return pl.pallas_call(
        functools.partial(_ffn_kernel, final_norm=final_norm, tf=tf),
        grid=(T // tm,),
        in_specs=[pl.BlockSpec((tm, D), lambda i: (i, 0)),
                  _resident((1, D)), _resident((D, F)), _resident((D, F)), _resident((F, D)), _resident((1, D))],
        out_specs=pl.BlockSpec((tm, D), lambda i: (i, 0)),
        out_shape=jax.ShapeDtypeStruct((T, D), F32),
        compiler_params=_cparams("parallel"),
    )(x, g.reshape(1, D), wg, wu, wd, g_final.reshape(1, D))


def _softmax_rows(s):
    e = jnp.exp(s - jnp.max(s, axis=-1, keepdims=True))
    return e / jnp.sum(e, axis=-1, keepdims=True)


def _xattn_prompt_kernel(x_ref, g_ref, wq_ref, k_ref, v_ref, wo_ref, o_ref, *, dh):
    scale = dh ** -0.5
    x = x_ref[...]
    q = _dot(_rms(x, g_ref[...]).astype(BF16), wq_ref[...]).astype(BF16)
    heads = []
    for h in range(X_HEADS):
        sl = slice(h * dh, (h + 1) * dh)
        s = _dot_nt(q[:, sl], k_ref[0, :, sl].astype(BF16)) * scale
        heads.append(_dot(_softmax_rows(s).astype(BF16), v_ref[0, :, sl].astype(BF16)).astype(BF16))
    o_ref[...] = x + _dot(jnp.concatenate(heads, axis=-1), wo_ref[...])


def xattn_prompt(x, g, wq, mk, mv, wo, seq, tq):
    T, D = x.shape
    B, M, _ = mk.shape
    nt = seq // tq
    return pl.pallas_call(
        functools.partial(_xattn_prompt_kernel, dh=D // X_HEADS),
        grid=(B, nt),
        in_specs=[pl.BlockSpec((tq, D), lambda b, t: (b * nt + t, 0)),
                  pl.BlockSpec((1, D), lambda b, t: (0, 0)),
                  pl.BlockSpec((D, D), lambda b, t: (0, 0)),
                  pl.BlockSpec((1, M, D), lambda b, t: (b, 0, 0)),
                  pl.BlockSpec((1, M, D), lambda b, t: (b, 0, 0)),
                  pl.BlockSpec((D, D), lambda b, t: (0, 0))],
        out_specs=pl.BlockSpec((tq, D), lambda b, t: (b * nt + t, 0)),
        out_shape=jax.ShapeDtypeStruct((T, D), F32),
        compiler_params=_cparams("parallel", "parallel"),
    )(x, g.reshape(1, D), wq, mk, mv, wo)


def _xattn_sample_kernel(q_ref, k_ref, v_ref, o_ref, *, sb):
    M, H, dh = k_ref.shape[1:]
    scale = dh ** -0.5
    row = lax.broadcasted_iota(jnp.int32, (SUBLANES, M * H), 0)
    col_head = lax.broadcasted_iota(jnp.int32, (SUBLANES, M * H), 1) % H
    own = (row % H) == col_head
    pad = jnp.zeros((SUBLANES - H, dh), F32)
    for i in range(sb):
        q8 = jnp.concatenate([q_ref[i], pad], axis=0).astype(BF16)
        s = _dot_nt(q8, k_ref[i].reshape(M * H, dh).astype(BF16)) * scale
        p = _softmax_rows(jnp.where(own, s, -jnp.inf))
        o_ref[i] = _dot(p.astype(BF16), v_ref[i].reshape(M * H, dh).astype(BF16))[0:H]


def xattn_sample(q, ck, cv, layer, sb):
    B, D = q.shape
    _, _, M, H, dh = ck.shape
    out = pl.pallas_call(
        functools.partial(_xattn_sample_kernel, sb=sb),
        grid=(B // sb,),
        in_specs=[pl.BlockSpec((sb, H, dh), lambda i: (i, 0, 0)),
                  pl.BlockSpec((None, sb, M, H, dh), lambda i: (layer, i, 0, 0, 0)),
                  pl.BlockSpec((None, sb, M, H, dh), lambda i: (layer, i, 0, 0, 0))],
        out_specs=pl.BlockSpec((sb, H, dh), lambda i: (i, 0, 0)),
        out_shape=jax.ShapeDtypeStruct((B, H, dh), F32),
        compiler_params=_cparams("parallel"),
    )(q.reshape(B, H, dh), ck, cv)
    return out.reshape(B, D)


def _lane_cat_masks(L):
    row = lax.broadcasted_iota(jnp.int32, (L, HW), 0)
    j = lax.broadcasted_iota(jnp.int32, (L, HW), 1) % DH
    r2 = lax.broadcasted_iota(jnp.int32, (HW, HW), 0) // DH
    c2 = lax.broadcasted_iota(jnp.int32, (HW, HW), 1) // DH
    return row >= j, row > j, row == j, r2 == c2


def _expand_bd(x, bd):
    return jnp.where(bd, jnp.concatenate([x] * HEADS, axis=0), jnp.zeros((), x.dtype))


def _seg_reduce(x, op, fill):
    lo = lax.broadcasted_iota(jnp.int32, (x.shape[0], LANES), 1) < DH
    parts = []
    for c in range(HW // LANES):
        xh = x[:, c * LANES:(c + 1) * LANES]
        a = op(jnp.where(lo, xh, fill), axis=-1, keepdims=True)
        b = op(jnp.where(lo, fill, xh), axis=-1, keepdims=True)
        parts.append(jnp.where(lo, a, b))
    return jnp.concatenate(parts, axis=-1)


def _head_expander(first_lane):
    r = lax.broadcasted_iota(jnp.int32, (GATE_PAD, HW), 0)
    c = lax.broadcasted_iota(jnp.int32, (GATE_PAD, HW), 1) // DH
    return (r == c + first_lane).astype(BF16)


def _chunk_tril(tt):
    r = lax.broadcasted_iota(jnp.int32, (tt, tt), 0)
    c = lax.broadcasted_iota(jnp.int32, (tt, tt), 1)
    return ((r // CHUNK == c // CHUNK) & (r >= c)).astype(BF16)


def _split3(x):
    hi = x.astype(BF16)
    r = x - hi.astype(F32)
    mid = r.astype(BF16)
    return hi, mid, (r - mid.astype(F32)).astype(BF16)


def _dot_sel_r(x, sel):
    hi, mid, lo = _split3(x)
    return (_dot(lo, sel) + _dot(mid, sel)) + _dot(hi, sel)


def _dot_sel_l(sel, x):
    hi, mid, lo = _split3(x)
    return (_dot(sel, lo) + _dot(sel, mid)) + _dot(sel, hi)


def _col_to_row(x, eye):
    return jnp.sum(jnp.where(eye, x, 0.0), axis=0, keepdims=True)


def _head_rms(x, ones_bd, g_row):
    ms = _dot_sel_r(x * x, ones_bd) * (1.0 / DH)
    return x * lax.rsqrt(ms + EPS) * g_row


def _mlstm_prompt_kernel(za_ref, gt_ref, bias_ref, norm_ref, ha_ref, c_ref, n_ref, m_ref,
                         c_scr, n_scr, m_scr, ix_scr, bx_scr, *, nchunks, unroll):
    tb = pl.program_id(1)
    L = CHUNK
    tril, _, eye, bd = _lane_cat_masks(L)
    ones_bd = bd.astype(BF16)

    @pl.when(tb == 0)
    def _():
        c_scr[...] = jnp.zeros_like(c_scr)
        n_scr[...] = jnp.zeros_like(n_scr)
        m_scr[...] = jnp.zeros_like(m_scr)

    gt = gt_ref[...] + bias_ref[...]
    b_cols = _dot_sel_l(_chunk_tril(nchunks * L), _log_sigmoid(gt))
    ix_scr[...] = _dot_sel_r(gt, _head_expander(0))
    bx_scr[...] = _dot_sel_r(b_cols, _head_expander(HEADS))

    def chunk(ci, carry):
        c_bd, n_row, m_x = carry
        rows = pl.ds(pl.multiple_of(ci * L, L), L)
        q = (za_ref[rows, 0:HW] * (DH ** -0.5)).astype(BF16)
        k = za_ref[rows, HW:2 * HW]
        v = za_ref[rows, 2 * HW:3 * HW].astype(BF16)
        og = za_ref[rows, 3 * HW:4 * HW]
        i_x = ix_scr[rows, :]
        b_x = bx_scr[rows, :]

        d_intra = jnp.where(tril, b_x - _col_to_row(b_x, eye) + _col_to_row(i_x, eye), -jnp.inf)
        d_inter = b_x + m_x
        m_t = jnp.maximum(_seg_reduce(d_intra, jnp.max, -jnp.inf), d_inter)
        w_intra = jnp.exp(d_intra - m_t)
        w_inter = jnp.exp(d_inter - m_t)
        kb = k.astype(BF16)
        s = _dot_nt(q, _expand_bd(kb, bd)) * w_intra
        num = w_inter * _dot(q, c_bd.astype(BF16)) + _dot(s.astype(BF16), _expand_bd(v, bd))
        qn = _dot((q.astype(F32) * n_row).astype(BF16), ones_bd)
        den = w_inter * qn + _seg_reduce(s, jnp.sum, 0.0)
        hh = num / jnp.maximum(jnp.abs(den), jnp.exp(-m_t))
        ha_ref[rows, :] = _head_rms(hh, ones_bd, norm_ref[...]) * _sigmoid(og)

        b_last = b_x[L - 1:L, :]
        g_x = b_last - b_x + i_x
        m_new = jnp.maximum(b_last + m_x, jnp.max(g_x, axis=0, keepdims=True))
        kw = k * jnp.exp(g_x - m_new)
        decay = jnp.exp(b_last + m_x - m_new)
        c_new = decay * c_bd + jnp.where(bd, _dot(kw.T.astype(BF16), v), 0.0)
        return c_new, decay * n_row + jnp.sum(kw, axis=0, keepdims=True), m_new

    c_bd, n_row, m_x = lax.fori_loop(0, nchunks, chunk, (c_scr[...], n_scr[...], m_scr[...]), unroll=unroll)
    c_scr[...] = c_bd
    n_scr[...] = n_row
    m_scr[...] = m_x

    @pl.when(tb == pl.num_programs(1) - 1)
    def _():
        for h in range(HEADS):
            c_ref[0, h] = c_scr[h * DH:(h + 1) * DH, h * DH:(h + 1) * DH]
        n_ref[0] = n_scr[...]
        m_ref[0] = m_scr[...]


def mlstm_prompt(za, gates, b_i, b_f, norm, B, S, tt):
    assert S % tt == 0 and tt % CHUNK == 0
    nt = S // tt
    bias = jnp.zeros((1, GATE_PAD), F32).at[0, 0:HEADS].set(b_i).at[0, HEADS:2 * HEADS].set(b_f)
    ha, c, n, m = pl.pallas_call(
        functools.partial(_mlstm_prompt_kernel, nchunks=tt // CHUNK, unroll=4),
        grid=(B, nt),
        in_specs=[pl.BlockSpec((tt, 4 * HW), lambda b, t: (b * nt + t, 0)),
                  pl.BlockSpec((tt, GATE_PAD), lambda b, t: (b * nt + t, 0)),
                  pl.BlockSpec((1, GATE_PAD), lambda b, t: (0, 0)),
                  pl.BlockSpec((1, HW), lambda b, t: (0, 0))],
        out_specs=[pl.BlockSpec((tt, HW), lambda b, t: (b * nt + t, 0)),
                   pl.BlockSpec((1, HEADS, DH, DH), lambda b, t: (b, 0, 0, 0)),
                   pl.BlockSpec((1, 1, HW), lambda b, t: (b, 0, 0)),
                   pl.BlockSpec((1, 1, HW), lambda b, t: (b, 0, 0))],
        out_shape=[jax.ShapeDtypeStruct((B * S, HW), F32),
                   jax.ShapeDtypeStruct((B, HEADS, DH, DH), F32),
                   jax.ShapeDtypeStruct((B, 1, HW), F32),
                   jax.ShapeDtypeStruct((B, 1, HW), F32)],
        scratch_shapes=[pltpu.VMEM((HW, HW), F32), pltpu.VMEM((1, HW), F32), pltpu.VMEM((1, HW), F32),
                        pltpu.VMEM((tt, HW), F32), pltpu.VMEM((tt, HW), F32)],
        compiler_params=_cparams("parallel", "arbitrary"),
    )(za, gates, bias, norm.reshape(1, HW))
    return ha, c, n.reshape(B, HEADS, DH), m[:, 0, ::DH]


def _gdn_prompt_kernel(zc_ref, zg_ref, gt_ref, cw_ref, alog_ref, dtb_ref, norm_ref,
                       hc_ref, s_ref, tail_ref,
                       s_scr, xp_scr, q_scr, k_scr, v_scr, beta_scr, g_scr, uv_scr, wq_scr, qkm_scr, kwt_scr,
                       *, nseq, nchunks, group):
    tb = pl.program_id(1)
    L = CHUNK
    tt = nchunks * L
    pad = SUBLANES
    tril, strict, eye, bd = _lane_cat_masks(L)
    ones_bd = bd.astype(BF16)

    @pl.when(tb == 0)
    def _():
        s_scr[...] = jnp.zeros_like(s_scr)
        xp_scr[:, 0:pad, :] = jnp.zeros((nseq, pad, 3 * HW), F32)

    for i in range(nseq):
        sr = slice(i * tt, (i + 1) * tt)
        x = zc_ref[i]
        xp_scr[i, pad:pad + tt, :] = x
        y = cw_ref[CONV_K - 1:CONV_K, :] * x
        for j in range(CONV_K - 1):
            y = y + cw_ref[j:j + 1, :] * xp_scr[i, pl.ds(pad - (CONV_K - 1) + j, tt), :]
        xp_scr[i, 0:pad, :] = x[tt - pad:tt, :]
        y = _silu(y)
        q_raw, k_raw = y[:, 0:HW], y[:, HW:2 * HW]
        q_scr[sr, :] = (q_raw * lax.rsqrt(_dot_sel_r(q_raw * q_raw, ones_bd) + EPS) * (DH ** -0.5)).astype(BF16)
        k_scr[sr, :] = k_raw * lax.rsqrt(_dot_sel_r(k_raw * k_raw, ones_bd) + EPS)
        v_scr[sr, :] = y[:, 2 * HW:3 * HW]
        gt = gt_ref[i]
        beta_scr[sr, :] = _dot_sel_r(_sigmoid(gt), _head_expander(2 * HEADS))
        la_cols = -jnp.exp(alog_ref[...]) * _softplus(gt + dtb_ref[...])
        g_scr[sr, :] = _dot_sel_r(_dot_sel_l(_chunk_tril(tt), la_cols), _head_expander(3 * HEADS))

    def prepare(gi, carry):
        cis = [gi * group + j for j in range(group)]
        rows = [pl.ds(pl.multiple_of(ci * L, L), L) for ci in cis]
        k = [k_scr[r, :] for r in rows]
        g_x = [g_scr[r, :] for r in rows]
        beta_row = [_col_to_row(beta_scr[r, :], eye) for r in rows]
        dec_incl = [jnp.where(tril, jnp.exp(jnp.where(tril, g - _col_to_row(g, eye), 0.0)), 0.0) for g in g_x]
        k_bd = [_expand_bd(kk.astype(BF16), bd) for kk in k]
        n0 = [-(jnp.where(strict, d, 0.0) * _dot_nt(kk.astype(BF16), kbd) * br)
              for d, kk, kbd, br in zip(dec_incl, k, k_bd, beta_row)]
        for r, d, kbd, br in zip(rows, dec_incl, k_bd, beta_row):
            qkm_scr[r, :] = (_dot_nt(q_scr[r, :], kbd) * d * br).astype(BF16)

        p = [_dot(n.astype(BF16), _expand_bd(n.astype(BF16), bd)) for n in n0]
        m = n0
        steps = int(math.log2(L)) - 1
        for i in range(steps):
            p_bd = [_expand_bd(pp.astype(BF16), bd) for pp in p]
            if i < steps - 1:
                pm = [_dot(jnp.concatenate([pp, mm], axis=0).astype(BF16), pbd) for pp, mm, pbd in zip(p, m, p_bd)]
                p_next, mp = [x[0:L] for x in pm], [x[L:2 * L] for x in pm]
            else:
                p_next, mp = None, [_dot(mm.astype(BF16), pbd) for mm, pbd in zip(m, p_bd)]
            m = [mm + pp + x for mm, pp, x in zip(m, p, mp)]
            p = p_next

        for ci, r, kk, g, mm in zip(cis, rows, k, g_x, m):
            v = v_scr[r, :]
            egk = jnp.exp(g) * kk
            rhs_bd = jnp.concatenate([_expand_bd(v.astype(BF16), bd), _expand_bd(egk.astype(BF16), bd)], axis=1)
            mr = _dot(mm.astype(BF16), rhs_bd)
            uv_scr[r, :] = v + mr[:, 0:HW]
            wq_rows = pl.multiple_of(ci * 2 * L, 2 * L)
            wq_scr[pl.ds(wq_rows, L), :] = (egk + mr[:, HW:2 * HW]).astype(BF16)
            wq_scr[pl.ds(wq_rows + L, L), :] = q_scr[r, :]
            kw = kk * (jnp.exp(g[L - 1:L, :] - g) * beta_scr[r, :])
            kwt_scr[pl.ds(pl.multiple_of(ci * HW, HW), HW), :] = kw.T.astype(BF16)
        return carry

    lax.fori_loop(0, nseq * nchunks // group, prepare, 0)

    def advance(c, states):
        cis = [i * nchunks + c for i in range(nseq)]
        rows = [pl.ds(pl.multiple_of(ci * L, L), L) for ci in cis]
        wqs = [_dot(wq_scr[pl.ds(pl.multiple_of(ci * 2 * L, 2 * L), 2 * L), :], s.astype(BF16))
               for ci, s in zip(cis, states)]
        ub = [(uv_scr[r, :] - x[0:L]).astype(BF16) for r, x in zip(rows, wqs)]
        new = [jnp.exp(g_scr[pl.ds(pl.multiple_of(ci * L + L - SUBLANES, SUBLANES), SUBLANES), :][SUBLANES - 1:, :]) * s
               + jnp.where(bd, _dot(kwt_scr[pl.ds(pl.multiple_of(ci * HW, HW), HW), :], u), 0.0)
               for ci, s, u in zip(cis, states, ub)]
        for i, (r, x, u) in enumerate(zip(rows, wqs, ub)):
            o = jnp.exp(g_scr[r, :]) * x[L:2 * L] + _dot(qkm_scr[r, :], _expand_bd(u, bd))
            hc_ref[i, pl.ds(pl.multiple_of(c * L, L), L), :] = (
                _head_rms(o, ones_bd, norm_ref[...]) * _silu(zg_ref[i, pl.ds(pl.multiple_of(c * L, L), L), :]))
        return tuple(new)

    states = lax.fori_loop(0, nchunks, advance, tuple(s_scr[i] for i in range(nseq)))
    for i in range(nseq):
        s_scr[i] = states[i]

    @pl.when(tb == pl.num_programs(1) - 1)
    def _():
        for i in range(nseq):
            for h in range(HEADS):
                s_ref[i, h] = s_scr[i, h * DH:(h + 1) * DH, h * DH:(h + 1) * DH]
            tail_ref[i] = xp_scr[i, 0:pad, :]


def _gdn_gate_rows(a_log, dt_bias):
    z = jnp.zeros((1, GATE_PAD), F32)
    return (z.at[0, 3 * HEADS:4 * HEADS].set(a_log), z.at[0, 3 * HEADS:4 * HEADS].set(dt_bias))


def gdn_prompt(zc, zg, gates, conv_w, a_log, dt_bias, norm, B, S, tt):
    assert S % tt == 0 and tt % CHUNK == 0
    nt = S // tt
    nseq = math.gcd(B, GDN_SEQS)
    rows = nseq * tt
    alog, dtb = _gdn_gate_rows(a_log, dt_bias)
    blk = lambda w: pl.BlockSpec((nseq, tt, w), lambda b, t: (b, t, 0))
    row = lambda w: pl.BlockSpec((1, w), lambda b, t: (0, 0))
    hc, s, tail = pl.pallas_call(
        functools.partial(_gdn_prompt_kernel, nseq=nseq, nchunks=tt // CHUNK, group=math.gcd(rows // CHUNK, 8)),
        grid=(B // nseq, nt),
        in_specs=[blk(3 * HW), blk(HW), blk(GATE_PAD),
                  pl.BlockSpec((CONV_K, 3 * HW), lambda b, t: (0, 0)), row(GATE_PAD), row(GATE_PAD), row(HW)],
        out_specs=[blk(HW),
                   pl.BlockSpec((nseq, HEADS, DH, DH), lambda b, t: (b, 0, 0, 0)),
                   pl.BlockSpec((nseq, SUBLANES, 3 * HW), lambda b, t: (b, 0, 0))],
        out_shape=[jax.ShapeDtypeStruct((B, S, HW), F32),
                   jax.ShapeDtypeStruct((B, HEADS, DH, DH), F32),
                   jax.ShapeDtypeStruct((B, SUBLANES, 3 * HW), F32)],
        scratch_shapes=[pltpu.VMEM((nseq, HW, HW), F32), pltpu.VMEM((nseq, tt + SUBLANES, 3 * HW), F32),
                        pltpu.VMEM((rows, HW), BF16), pltpu.VMEM((rows, HW), F32), pltpu.VMEM((rows, HW), F32),
                        pltpu.VMEM((rows, HW), F32), pltpu.VMEM((rows, HW), F32), pltpu.VMEM((rows, HW), F32),
                        pltpu.VMEM((2 * rows, HW), BF16), pltpu.VMEM((rows, HW), BF16),
                        pltpu.VMEM((rows // CHUNK * HW, CHUNK), BF16)],
        compiler_params=_cparams("parallel", "arbitrary"),
    )(zc.reshape(B, S, 3 * HW), zg.reshape(B, S, HW), gates.reshape(B, S, GATE_PAD), conv_w, alog, dtb,
      jnp.tile(norm, HEADS).reshape(1, HW))
    return hc.reshape(B * S, HW), s, tail[:, SUBLANES - (CONV_K - 1):, :]


def _s5_prep_kernel(are_ref, aim_ref, ldt_ref, bre_ref, bim_ref, lre_ref, lim_ref, bbre_ref, bbim_ref):
    a_re, a_im = are_ref[...], aim_ref[...]
    dt = jnp.exp(ldt_ref[...])
    mag = jnp.exp(a_re * dt)
    lam_re, lam_im = mag * jnp.cos(a_im * dt), mag * jnp.sin(a_im * dt)
    lre_ref[...] = lam_re
    lim_ref[...] = lam_im
    nr, ni = lam_re - 1.0, lam_im
    den = a_re * a_re + a_im * a_im
    coef_re = (nr * a_re + ni * a_im) / den
    coef_im = (ni * a_re - nr * a_im) / den
    b_re, b_im = bre_ref[...], bim_ref[...]
    bbre_ref[...] = coef_re * b_re - coef_im * b_im
    bbim_ref[...] = coef_re * b_im + coef_im * b_re


def s5_params(lp):
    G, N, P = S5_GROUPS, S5_N, S5_P
    row = lambda a: a.astype(F32).reshape(1, G * N)
    to_pn = lambda b: jnp.transpose(b.astype(F32), (2, 0, 1)).reshape(P, G * N)
    shp = [jax.ShapeDtypeStruct((1, G * N), F32)] * 2 + [jax.ShapeDtypeStruct((P, G * N), F32)] * 2
    lam_re, lam_im, bb_re, bb_im = pl.pallas_call(_s5_prep_kernel, out_shape=shp)(
        row(lp['s5_a_re']), row(lp['s5_a_im']), row(jnp.repeat(lp['s5_log_dt'][:, None], N, axis=1)),
        to_pn(lp['s5_b_re']), to_pn(lp['s5_b_im']))
    eye = jnp.eye(S5_GB, dtype=F32)

    def w_in_blocks(bb):
        b4 = bb.reshape(P, S5_NBLK, S5_GB, N)
        return jnp.einsum('pbgn,gh->bgphn', b4, eye).reshape(S5_NBLK, S5_GB * P, S5_GB * N)

    def w_out_blocks(c):
        c4 = c.astype(F32).reshape(S5_NBLK, S5_GB, P, N)
        return jnp.einsum('bgpn,gh->bgnhp', c4, eye).reshape(S5_NBLK, S5_GB * N, S5_GB * P)

    w_in = jnp.concatenate([w_in_blocks(bb_re), w_in_blocks(bb_im)], axis=-1).astype(BF16)
    return {'lam_re': lam_re, 'lam_im': lam_im, 'w_in': w_in,
            'w_out_re': w_out_blocks(lp['s5_c_re']).astype(BF16),
            'w_out_im': (-w_out_blocks(lp['s5_c_im'])).astype(BF16),
            'd': lp['s5_d'].astype(F32).reshape(1, S5_WIDTH), 'w_glu': lp['s5_w_glu'].astype(BF16)}


def _s5_kernel(u_ref, h0r_ref, h0i_ref, lamr_ref, lami_ref, win_ref, wor_ref, woi_ref, d_ref, wglu_ref,
               ys_ref, h1r_ref, h1i_ref, hr_scr, hi_scr, br_scr, bi_scr, *, nseq, rows, bb, lane_blk):
    tb = pl.program_id(0)
    nsteps = nseq * rows // bb
    blk_in, blk_st = S5_GB * S5_P, S5_GB * S5_N

    @pl.when(tb == 0)
    def _():
        hr_scr[...] = h0r_ref[...]
        hi_scr[...] = h0i_ref[...]

    if nseq > 1:
        u = jnp.swapaxes(u_ref[...], 0, 1).reshape(nseq * rows, S5_WIDTH)
    else:
        u = u_ref[0]
    ub = u.astype(BF16)
    for blk in range(S5_NBLK):
        bu = _dot(ub[:, blk * blk_in:(blk + 1) * blk_in], win_ref[blk])
        br_scr[:, blk * blk_st:(blk + 1) * blk_st] = bu[:, 0:blk_st]
        bi_scr[:, blk * blk_st:(blk + 1) * blk_st] = bu[:, blk_st:2 * blk_st]

    for lb in range(S5_STATE // lane_blk):
        ls = slice(lb * lane_blk, (lb + 1) * lane_blk)
        lr = jnp.broadcast_to(lamr_ref[:, ls], (bb, lane_blk))
        li = jnp.broadcast_to(lami_ref[:, ls], (bb, lane_blk))

        def step(t, carry):
            hr, hi = carry
            r = pl.ds(pl.multiple_of(t * bb, bb), bb)
            nr = lr * hr - li * hi + br_scr[r, ls]
            ni = lr * hi + li * hr + bi_scr[r, ls]
            br_scr[r, ls] = nr
            bi_scr[r, ls] = ni
            return nr, ni

        hr, hi = lax.fori_loop(0, nsteps, step, (hr_scr[:, ls], hi_scr[:, ls]), unroll=min(nsteps, 8))
        hr_scr[:, ls] = hr
        hi_scr[:, ls] = hi

    ys = []
    for blk in range(S5_NBLK):
        st = slice(blk * blk_st, (blk + 1) * blk_st)
        ys.append(_dot(br_scr[:, st].astype(BF16), wor_ref[blk]) + _dot(bi_scr[:, st].astype(BF16), woi_ref[blk]))
    gy = jax.nn.gelu(jnp.concatenate(ys, axis=-1) + d_ref[...] * u)
    out = gy * _sigmoid(_dot(gy.astype(BF16), wglu_ref[...]))
    if nseq > 1:
        ys_ref[...] = jnp.swapaxes(out.reshape(rows, nseq, S5_WIDTH), 0, 1)
    else:
        ys_ref[0] = out

    @pl.when(tb == pl.num_programs(0) - 1)
    def _():
        h1r_ref[...] = hr_scr[...]
        h1i_ref[...] = hi_scr[...]


def s5_mixer(u, h0_re, h0_im, sp, tt, single_step):
    B = h0_re.shape[0]
    nseq, S = (1, 1) if single_step else (B, u.shape[1])
    rows = B if single_step else tt
    assert S % tt == 0 and B % SUBLANES == 0
    assert u.shape == ((1, B, S5_WIDTH) if single_step else (B, S, S5_WIDTH))
    lane_blk = max(LANES, min(S5_STATE, (SUBLANES * SUBLANES * LANES) // B))
    full = lambda shape: pl.BlockSpec(shape, lambda t: (0,) * len(shape))
    return pl.pallas_call(
        functools.partial(_s5_kernel, nseq=nseq, rows=rows, bb=B, lane_blk=lane_blk),
        grid=(S // tt,),
        in_specs=[pl.BlockSpec((nseq, rows, S5_WIDTH), lambda t: (0, t, 0)),
                  full((B, S5_STATE)), full((B, S5_STATE)), full((1, S5_STATE)), full((1, S5_STATE)),
                  full(sp['w_in'].shape), full(sp['w_out_re'].shape), full(sp['w_out_im'].shape),
                  full((1, S5_WIDTH)), full((S5_WIDTH, S5_WIDTH))],
        out_specs=[pl.BlockSpec((nseq, rows, S5_WIDTH), lambda t: (0, t, 0)),
                   full((B, S5_STATE)), full((B, S5_STATE))],
        out_shape=[jax.ShapeDtypeStruct(u.shape, F32),
                   jax.ShapeDtypeStruct((B, S5_STATE), F32), jax.ShapeDtypeStruct((B, S5_STATE), F32)],
        scratch_shapes=[pltpu.VMEM((B, S5_STATE), F32), pltpu.VMEM((B, S5_STATE), F32),
                        pltpu.VMEM((nseq * rows, S5_STATE), F32), pltpu.VMEM((nseq * rows, S5_STATE), F32)],
        compiler_params=_cparams("arbitrary"),
    )(u, h0_re, h0_im, sp['lam_re'], sp['lam_im'], sp['w_in'], sp['w_out_re'], sp['w_out_im'], sp['d'], sp['w_glu'])


def _ones_bd():
    r = lax.broadcasted_iota(jnp.int32, (HW, HW), 0) // DH
    c = lax.broadcasted_iota(jnp.int32, (HW, HW), 1) // DH
    return (r == c).astype(BF16)


def _mlstm_sample_kernel(za_ref, gt_ref, bias_ref, norm_ref, c_ref, n_ref, m_ref,
                         ha_ref, c_out, n_out, m_out, q_scr, kw_scr, h_scr):
    za = za_ref[...]
    q_scr[...] = (za[:, 0:HW] * (DH ** -0.5)).T
    k_t = za[:, HW:2 * HW].T
    v_t = za[:, 2 * HW:3 * HW].T
    g_t = (gt_ref[...] + bias_ref[...]).T
    m_out[...] = jnp.zeros_like(m_out)
    for h in range(HEADS):
        hs = slice(h * DH, (h + 1) * DH)
        i_h = g_t[h:h + 1, :]
        bm = _log_sigmoid(g_t[HEADS + h:HEADS + h + 1, :]) + m_ref[h:h + 1, :]
        m_t = jnp.maximum(i_h, bm)
        w_in = jnp.exp(i_h - m_t)
        w_st = jnp.exp(bm - m_t)
        q_h, k_h, v_h = q_scr[hs, :], k_t[hs, :], v_t[hs, :]
        s = jnp.sum(q_h * k_h, axis=0, keepdims=True) * w_in
        kw_scr[hs, :] = k_h * w_in

        def body(d, acc):
            r = h * DH + d
            rows = pl.ds(pl.multiple_of(r * DH, DH), DH)
            c_hd = c_ref[rows, :]
            c_out[rows, :] = w_st * c_hd + kw_scr[pl.ds(r, 1), :] * v_h
            return acc + q_scr[pl.ds(r, 1), :] * c_hd

        qc = lax.fori_loop(0, DH, body, jnp.zeros((DH, za.shape[0]), F32), unroll=4)
        n_h = n_ref[hs, :]
        num = w_st * qc + s * v_h
        den = w_st * jnp.sum(q_h * n_h, axis=0, keepdims=True) + s
        h_scr[hs, :] = num / jnp.maximum(jnp.abs(den), jnp.exp(-m_t))
        n_out[hs, :] = w_st * n_h + kw_scr[hs, :]
        m_out[h:h + 1, :] = m_t
    ha_ref[...] = _head_rms(h_scr[...].T, _ones_bd(), norm_ref[...]) * _sigmoid(za[:, 3 * HW:4 * HW])


def mlstm_sample(za, gates, b_i, b_f, norm, c_t, n_t, m_t):
    B = za.shape[0]
    bias = jnp.zeros((1, GATE_PAD), F32).at[0, 0:HEADS].set(b_i).at[0, HEADS:2 * HEADS].set(b_f)
    shp = lambda *s: jax.ShapeDtypeStruct(s, F32)
    return pl.pallas_call(
        _mlstm_sample_kernel,
        out_shape=[shp(B, HW), shp(HW * DH, B), shp(HW, B), shp(SUBLANES, B)],
        scratch_shapes=[pltpu.VMEM((HW, B), F32), pltpu.VMEM((HW, B), F32), pltpu.VMEM((HW, B), F32)],
        compiler_params=pltpu.CompilerParams(vmem_limit_bytes=VMEM_LIMIT),
    )(za, gates, bias, norm.reshape(1, HW), c_t, n_t, m_t)


def _gdn_sample_kernel(zc_ref, zg_ref, gt_ref, buf_ref, cw_ref, alog_ref, dtb_ref, norm_ref, s_ref,
                       hc_ref, s_out, buf_out, q_scr, k_scr, o_scr):
    W3 = 3 * HW
    x = zc_ref[...]
    y = cw_ref[CONV_K - 1:CONV_K, :] * x
    for j in range(CONV_K - 1):
        y = y + cw_ref[j:j + 1, :] * buf_ref[:, j * W3:(j + 1) * W3]
    buf_out[:, 0:(CONV_K - 2) * W3] = buf_ref[:, W3:(CONV_K - 1) * W3]
    buf_out[:, (CONV_K - 2) * W3:(CONV_K - 1) * W3] = x
    y = _silu(y)
    ones_bd = _ones_bd()
    q_raw, k_raw = y[:, 0:HW], y[:, HW:2 * HW]
    q_scr[...] = (q_raw * lax.rsqrt(_dot_sel_r(q_raw * q_raw, ones_bd) + EPS) * (DH ** -0.5)).T
    k_scr[...] = (k_raw * lax.rsqrt(_dot_sel_r(k_raw * k_raw, ones_bd) + EPS)).T
    v_t = y[:, 2 * HW:3 * HW].T
    gt = gt_ref[...]
    beta_t = _sigmoid(gt).T
    la_t = (-jnp.exp(alog_ref[...]) * _softplus(gt + dtb_ref[...])).T
    nb = x.shape[0]
    for h in range(HEADS):
        hs = slice(h * DH, (h + 1) * DH)
        beta = beta_t[2 * HEADS + h:2 * HEADS + h + 1, :]
        eg = jnp.exp(la_t[3 * HEADS + h:3 * HEADS + h + 1, :])
        q_h, k_h, v_h = q_scr[hs, :], k_scr[hs, :], v_t[hs, :]

        def read(d, acc):
            ks, qs = acc
            r = h * DH + d
            s_hd = s_ref[pl.ds(pl.multiple_of(r * DH, DH), DH), :]
            return ks + k_scr[pl.ds(r, 1), :] * s_hd, qs + q_scr[pl.ds(r, 1), :] * s_hd

        zero = jnp.zeros((DH, nb), F32)
        ks, qs = lax.fori_loop(0, DH, read, (zero, zero), unroll=4)
        u = v_h - eg * ks
        o_scr[hs, :] = eg * qs + (jnp.sum(q_h * k_h, axis=0, keepdims=True) * beta) * u

        def write(d, carry):
            r = h * DH + d
            rows = pl.ds(pl.multiple_of(r * DH, DH), DH)
            s_out[rows, :] = eg * s_ref[rows, :] + (beta * k_scr[pl.ds(r, 1), :]) * u
            return carry

        lax.fori_loop(0, DH, write, 0, unroll=4)
    hc_ref[...] = _head_rms(o_scr[...].T, ones_bd, norm_ref[...]) * _silu(zg_ref[...])


def gdn_sample(zc, zg, gates, buf, conv_w, a_log, dt_bias, norm, s_t):
    B = zc.shape[0]
    alog, dtb = _gdn_gate_rows(a_log, dt_bias)
    shp = lambda *s: jax.ShapeDtypeStruct(s, F32)
    return pl.pallas_call(
        _gdn_sample_kernel,
        out_shape=[shp(B, HW), shp(HW * DH, B), shp(B, (CONV_K - 1) * 3 * HW)],
        scratch_shapes=[pltpu.VMEM((HW, B), F32), pltpu.VMEM((HW, B), F32), pltpu.VMEM((HW, B), F32)],
        compiler_params=pltpu.CompilerParams(vmem_limit_bytes=VMEM_LIMIT),
    )(zc, zg, gates, buf, conv_w, alog, dtb, jnp.tile(norm, HEADS).reshape(1, HW), s_t)


FFN_CHUNK = 1408
IN_SEGMENTS = (4 * HW, S5_WIDTH, 3 * HW, HW, GATE_PAD)


def _tile(n, pref):
    return pref if n % pref == 0 else n


def prep_layer_weights(lp):
    w = lp['w_in']
    a, g4 = 4 * HW, HEADS
    o_u = a + 2 * g4
    o_c = o_u + S5_WIDTH
    o_g = o_c + 3 * HW
    o_b = o_g + HW
    gate_cols = jnp.concatenate([w[:, a:a + 2 * g4], w[:, o_b:o_b + 2 * g4],
                                 jnp.zeros((w.shape[0], GATE_PAD - 4 * g4), w.dtype)], axis=1)
    w_in = jnp.concatenate([w[:, 0:a], w[:, o_u:o_c], w[:, o_c:o_g], w[:, o_g:o_b], gate_cols], axis=1)
    bf = lambda n: lp[n].astype(BF16)
    return {'w_in': w_in.astype(BF16), 'w_out': bf('w_out'), 'w_mq': bf('w_mq'), 'w_mo': bf('w_mo'),
            'w_mkv': jnp.concatenate([lp['w_mk'], lp['w_mv']], axis=1).astype(BF16),
            'w_gate': bf('w_gate'), 'w_up': bf('w_up'), 'w_down': bf('w_down'), 's5': s5_params(lp)}


def mixer_prompt(x, lp, W, B, S):
    T = B * S
    tm = _tile(T, 512)
    za, zu, zc, zg, gates = norm_matmul(x, lp['norm_mix'], W['w_in'], IN_SEGMENTS, tm)
    tt = _tile(S, 512)
    ha, c1, n1, m1 = mlstm_prompt(za, gates, lp['mlstm_b_i'], lp['mlstm_b_f'], lp['mlstm_norm'], B, S, tt)
    h0 = jnp.zeros((B, S5_STATE), F32)
    ys3, r1, i1 = s5_mixer(zu.reshape(B, S, S5_WIDTH), h0, h0, W['s5'], _tile(S, 128), False)
    ys = ys3.reshape(T, S5_WIDTH)
    hc, s1, buf1 = gdn_prompt(zc, zg, gates, lp['gdn_conv_w'], lp['gdn_a_log'], lp['gdn_dt_bias'], lp['gdn_norm'], B, S, _tile(S, 256))
    x1 = matmul_residual(x, [ha, ys, hc], W['w_out'], tm)
    return x1, (c1, n1, m1, r1.reshape(B, S5_GROUPS, S5_N), i1.reshape(B, S5_GROUPS, S5_N), s1, buf1)


def mixer_sample(x, st, lp, W):
    B = x.shape[0]
    c0, n0, m0, r0, i0, s0, buf0 = st
    za, zu, zc, zg, gates = norm_matmul(x, lp['norm_mix'], W['w_in'], IN_SEGMENTS, B)
    m_t = jnp.zeros((SUBLANES, B), F32).at[0:HEADS, :].set(m0.T)
    ha, c1t, n1t, m1t = mlstm_sample(za, gates, lp['mlstm_b_i'], lp['mlstm_b_f'], lp['mlstm_norm'],
                                     c0.reshape(B, HW * DH).T, n0.reshape(B, HW).T, m_t)
    ys3, r1, i1 = s5_mixer(zu.reshape(1, B, S5_WIDTH), r0.reshape(B, S5_STATE), i0.reshape(B, S5_STATE), W['s5'], 1, True)
    hc, s1t, buf1 = gdn_sample(zc, zg, gates, buf0.reshape(B, (CONV_K - 1) * 3 * HW), lp['gdn_conv_w'],
                               lp['gdn_a_log'], lp['gdn_dt_bias'], lp['gdn_norm'], s0.reshape(B, HW * DH).T)
    x1 = matmul_residual(x, [ha, ys3.reshape(B, S5_WIDTH), hc], W['w_out'], B)
    return x1, (c1t.T.reshape(B, HEADS, DH, DH), n1t.T.reshape(B, HEADS, DH), m1t[0:HEADS, :].T,
                r1.reshape(B, S5_GROUPS, S5_N), i1.reshape(B, S5_GROUPS, S5_N),
                s1t.T.reshape(B, HEADS, DH, DH), buf1.reshape(B, CONV_K - 1, 3 * HW))


def mem_kv(mem, lp, W):
    D = mem.shape[1]
    return norm_matmul(mem, lp['norm_mem'], W['w_mkv'], (D, D), _tile(mem.shape[0], 512))


def xattn_ffn_prompt(x, mk, mv, lp, W, S, norm_final, final):
    T, D = x.shape
    x2 = xattn_prompt(x, lp['norm_xattn'], W['w_mq'], mk, mv, W['w_mo'], S, _tile(S, 512))
    return ffn(x2, lp['norm_ffn'], W['w_gate'], W['w_up'], W['w_down'], norm_final, final, _tile(T, 512), FFN_CHUNK)


def xattn_ffn_sample(x, ck, cv, layer, lp, W, norm_final, final):
    B, D = x.shape
    (q,) = norm_matmul(x, lp['norm_xattn'], W['w_mq'], (D,), B)
    o = xattn_sample(q, ck, cv, layer, 4)
    x2 = matmul_residual(x, [o], W['w_mo'], B)
    return ffn(x2, lp['norm_ffn'], W['w_gate'], W['w_up'], W['w_down'], norm_final, final, B, FFN_CHUNK)


LAYER_PARAMS = ('norm_mix', 'w_in', 'w_out', 'mlstm_b_i', 'mlstm_b_f', 'mlstm_norm', 's5_a_re', 's5_a_im', 's5_log_dt',
                's5_b_re', 's5_b_im', 's5_c_re', 's5_c_im', 's5_d', 's5_w_glu', 'gdn_conv_w', 'gdn_a_log',
                'gdn_dt_bias', 'gdn_norm', 'norm_xattn', 'norm_mem', 'w_mq', 'w_mk', 'w_mv', 'w_mo', 'norm_ffn',
                'w_gate', 'w_up', 'w_down')


def kernel(x_prompt, x_sample, mem_prompt, cache_mem_k, cache_mem_v, state_mlstm_c, state_mlstm_n, state_mlstm_m, state_s5_re, state_s5_im, state_gdn, state_gdn_conv, norm_mix, w_in, w_out, mlstm_b_i, mlstm_b_f, mlstm_norm, s5_a_re, s5_a_im, s5_log_dt, s5_b_re, s5_b_im, s5_c_re, s5_c_im, s5_d, s5_w_glu, gdn_conv_w, gdn_a_log, gdn_dt_bias, gdn_norm, norm_xattn, norm_mem, w_mq, w_mk, w_mv, w_mo, norm_ffn, w_gate, w_up, w_down, norm_final):
    stacked = dict(norm_mix=norm_mix, w_in=w_in, w_out=w_out, mlstm_b_i=mlstm_b_i, mlstm_b_f=mlstm_b_f,
                   mlstm_norm=mlstm_norm, s5_a_re=s5_a_re, s5_a_im=s5_a_im, s5_log_dt=s5_log_dt, s5_b_re=s5_b_re,
                   s5_b_im=s5_b_im, s5_c_re=s5_c_re, s5_c_im=s5_c_im, s5_d=s5_d, s5_w_glu=s5_w_glu,
                   gdn_conv_w=gdn_conv_w, gdn_a_log=gdn_a_log, gdn_dt_bias=gdn_dt_bias, gdn_norm=gdn_norm,
                   norm_xattn=norm_xattn, norm_mem=norm_mem, w_mq=w_mq, w_mk=w_mk, w_mv=w_mv, w_mo=w_mo,
                   norm_ffn=norm_ffn, w_gate=w_gate, w_up=w_up, w_down=w_down)
    B, S, D = x_prompt.shape
    Bs = x_sample.shape[0]
    M = mem_prompt.shape[1]
    depth = w_in.shape[0]
    xp = x_prompt.reshape(B * S, D)
    xs = x_sample.reshape(Bs, D)
    mem = mem_prompt.reshape(B * M, D)
    cache_k, cache_v = cache_mem_k, cache_mem_v
    mem_k, mem_v, st_p, st_s = [], [], [], []
    for l in range(depth):
        lp = {n: stacked[n][l] for n in LAYER_PARAMS}
        W = prep_layer_weights(lp)
        last = l == depth - 1
        xp, sp = mixer_prompt(xp, lp, W, B, S)
        mk, mv = mem_kv(mem, lp, W)
        xp = xattn_ffn_prompt(xp, mk.reshape(B, M, D), mv.reshape(B, M, D), lp, W, S, norm_final, last)
        st_in = (state_mlstm_c[l], state_mlstm_n[l], state_mlstm_m[l], state_s5_re[l], state_s5_im[l],
                 state_gdn[l], state_gdn_conv[l])
        xs, ss = mixer_sample(xs, st_in, lp, W)
        xs = xattn_ffn_sample(xs, cache_k, cache_v, l, lp, W, norm_final, last)
        mem_k.append(mk.reshape(B, M, X_HEADS, D // X_HEADS))
        mem_v.append(mv.reshape(B, M, X_HEADS, D // X_HEADS))
        st_p.append(sp)
        st_s.append(ss)
    stack = lambda lst: [jnp.stack([st[i] for st in lst]) for i in range(7)]
    return (xp.reshape(B, S, D), xs.reshape(Bs, 1, D), jnp.stack(mem_k), jnp.stack(mem_v),
            *stack(st_p), *stack(st_s))
```

```python
import functools
import math

import jax
import jax.numpy as jnp
from jax import lax
from jax.experimental import pallas as pl
from jax.experimental.pallas import tpu as pltpu

F32 = jnp.float32
BF16 = jnp.bfloat16
EPS = 1e-6

HEADS = 4
DH = 64
HW = HEADS * DH
CHUNK = 64
S5_P = 16
S5_N = 64
S5_GROUPS = 32
S5_WIDTH = S5_GROUPS * S5_P
S5_STATE = S5_GROUPS * S5_N
S5_GB = 8
S5_NBLK = S5_GROUPS // S5_GB
CONV_K = 4
GDN_SEQS = 4
X_HEADS = 4
GATE_PAD = 128
LANES = 128
SUBLANES = 8
VMEM_LIMIT = 48 * 1024 * 1024


def _cparams(*sem):
    return pltpu.CompilerParams(dimension_semantics=sem, vmem_limit_bytes=VMEM_LIMIT)


def _rms(x, g_row):
    return x * lax.rsqrt(jnp.mean(x * x, axis=-1, keepdims=True) + EPS) * g_row


def _dot(a, b):
    return jnp.dot(a, b, preferred_element_type=F32)


def _dot_nt(a, b):
    return lax.dot_general(a, b, (((1,), (1,)), ((), ())), preferred_element_type=F32)


def _sigmoid(x):
    return 1.0 / (1.0 + jnp.exp(-x))


def _silu(x):
    return x * _sigmoid(x)


def _softplus(x):
    return jnp.maximum(x, 0.0) + jnp.log1p(jnp.exp(-jnp.abs(x)))


def _log_sigmoid(x):
    return jnp.minimum(x, 0.0) - jnp.log1p(jnp.exp(-jnp.abs(x)))


def _norm_matmul_kernel(x_ref, g_ref, w_ref, *out_refs, splits):
    xn = _rms(x_ref[...], g_ref[...]).astype(BF16)
    off = 0
    for o_ref, n in zip(out_refs, splits):
        o_ref[...] = _dot(xn, w_ref[:, off:off + n])
        off += n


def _layer_weight(w_all, layer):
    return pl.BlockSpec((None,) + w_all.shape[1:], lambda *_: (layer, 0, 0), pipeline_mode=pl.Buffered(1))


def norm_matmul(x, g, w_all, layer, splits, tm):
    T, D = x.shape
    N = w_all.shape[2]
    assert sum(splits) == N and T % tm == 0
    return pl.pallas_call(
        functools.partial(_norm_matmul_kernel, splits=tuple(splits)),
        grid=(T // tm,),
        in_specs=[pl.BlockSpec((tm, D), lambda i: (i, 0)),
                  pl.BlockSpec((1, D), lambda i: (0, 0)),
                  _layer_weight(w_all, layer)],
        out_specs=[pl.BlockSpec((tm, n), lambda i: (i, 0)) for n in splits],
        out_shape=[jax.ShapeDtypeStruct((T, n), F32) for n in splits],
        compiler_params=_cparams("parallel"),
    )(x, g.reshape(1, D), w_all)


def _matmul_residual_kernel(x_ref, *refs, ksplits):
    a_refs, w_ref, o_ref = refs[:-2], refs[-2], refs[-1]
    acc = x_ref[...]
    off = 0
    for a_ref, k in zip(a_refs, ksplits):
        acc = acc + _dot(a_ref[...].astype(BF16), w_ref[off:off + k, :])
        off += k
    o_ref[...] = acc


def matmul_residual(x, acts, w_all, layer, tm):
    T, D = x.shape
    ks = tuple(a.shape[1] for a in acts)
    K = w_all.shape[1]
    assert sum(ks) == K and T % tm == 0
    return pl.pallas_call(
        functools.partial(_matmul_residual_kernel, ksplits=ks),
        grid=(T // tm,),
        in_specs=[pl.BlockSpec((tm, D), lambda i: (i, 0))]
                 + [pl.BlockSpec((tm, k), lambda i: (i, 0)) for k in ks]
                 + [_layer_weight(w_all, layer)],
        out_specs=pl.BlockSpec((tm, D), lambda i: (i, 0)),
        out_shape=jax.ShapeDtypeStruct((T, D), F32),
        compiler_params=_cparams("parallel"),
    )(x, *acts, w_all)


def _ffn_kernel(x_ref, g_ref, wg_ref, wu_ref, wd_ref, gf_ref, o_ref, *, final_norm, tf):
    x = x_ref[...]
    h = _rms(x, g_ref[...]).astype(BF16)
    y = x
    for j in range(wg_ref.shape[1] // tf):
        cols = slice(j * tf, (j + 1) * tf)
        a = _dot(h, wg_ref[:, cols])
        b = _dot(h, wu_ref[:, cols])
        y = y + _dot((_silu(a) * b).astype(BF16), wd_ref[cols, :])
    if final_norm:
        y = _rms(y, gf_ref[...])
    o_ref[...] = y


def _resident(shape):
    return pl.BlockSpec(shape, lambda *_: (0,) * len(shape), pipeline_mode=pl.Buffered(1))


def ffn(x, g, wg, wu, wd, layer, g_final, final_norm, tm, tf):
    T, D = x.shape
    F = wg.shape[2]
    assert T % tm == 0 and F % tf == 0
    return pl.pallas_call(
        functools.partial(_ffn_kernel, final_norm=final_norm, tf=tf),
        grid=(T // tm,),
        in_specs=[pl.BlockSpec((tm, D), lambda i: (i, 0)),
                  _resident((1, D)), _layer_weight(wg, layer), _layer_weight(wu, layer), _layer_weight(wd, layer),
                  _resident((1, D))],
        out_specs=pl.BlockSpec((tm, D), lambda i: (i, 0)),
        out_shape=jax.ShapeDtypeStruct((T, D), F32),
        compiler_params=_cparams("parallel"),
    )(x, g.reshape(1, D), wg, wu, wd, g_final.reshape(1, D))


def _softmax_rows(s):
    e = jnp.exp(s - jnp.max(s, axis=-1, keepdims=True))
    return e / jnp.sum(e, axis=-1, keepdims=True)


def _xattn_prompt_kernel(x_ref, *refs, dh, n_acts):
    a_refs = refs[:n_acts]
    wmix_ref, g_ref, wq_ref, k_ref, v_ref, wo_ref, o_ref = refs[n_acts:]
    scale = dh ** -0.5
    x = x_ref[...]
    off = 0
    for a_ref in a_refs:
        x = x + _dot(a_ref[...].astype(BF16), wmix_ref[off:off + a_ref.shape[1], :])
        off += a_ref.shape[1]
    q = _dot(_rms(x, g_ref[...]).astype(BF16), wq_ref[...]).astype(BF16)
    heads = []
    for h in range(X_HEADS):
        sl = slice(h * dh, (h + 1) * dh)
        s = _dot_nt(q[:, sl], k_ref[0, :, sl].astype(BF16)) * scale
        heads.append(_dot(_softmax_rows(s).astype(BF16), v_ref[0, :, sl].astype(BF16)).astype(BF16))
    o_ref[...] = x + _dot(jnp.concatenate(heads, axis=-1), wo_ref[...])


def xattn_prompt(x, acts, wmix, g, wq, mk, mv, wo, layer, seq, tq):
    T, D = x.shape
    B, M, _ = mk.shape
    nt = seq // tq
    rows = lambda w: pl.BlockSpec((tq, w), lambda b, t: (b * nt + t, 0))
    return pl.pallas_call(
        functools.partial(_xattn_prompt_kernel, dh=D // X_HEADS, n_acts=len(acts)),
        grid=(B, nt),
        in_specs=[rows(D)] + [rows(a.shape[1]) for a in acts]
                 + [_layer_weight(wmix, layer),
                    pl.BlockSpec((1, D), lambda b, t: (0, 0)),
                    _layer_weight(wq, layer),
                    pl.BlockSpec((1, M, D), lambda b, t: (b, 0, 0)),
                    pl.BlockSpec((1, M, D), lambda b, t: (b, 0, 0)),
                    _layer_weight(wo, layer)],
        out_specs=rows(D),
        out_shape=jax.ShapeDtypeStruct((T, D), F32),
        compiler_params=_cparams("parallel", "parallel"),
    )(x, *acts, wmix, g.reshape(1, D), wq, mk, mv, wo)


def _xattn_sample_kernel(q_ref, k_ref, v_ref, o_ref, *, sb):
    M, H, dh = k_ref.shape[1:]
    scale = dh ** -0.5
    row = lax.broadcasted_iota(jnp.int32, (SUBLANES, M * H), 0)
    col_head = lax.broadcasted_iota(jnp.int32, (SUBLANES, M * H), 1) % H
    own = (row % H) == col_head
    pad = jnp.zeros((SUBLANES - H, dh), F32)
    for i in range(sb):
        q8 = jnp.concatenate([q_ref[i], pad], axis=0).astype(BF16)
        s = _dot_nt(q8, k_ref[i].reshape(M * H, dh).astype(BF16)) * scale
        p = _softmax_rows(jnp.where(own, s, -jnp.inf))
        o_ref[i] = _dot(p.astype(BF16), v_ref[i].reshape(M * H, dh).astype(BF16))[0:H]


def xattn_sample(q, ck, cv, layer, sb):
    B, D = q.shape
    _, _, M, H, dh = ck.shape
    out = pl.pallas_call(
        functools.partial(_xattn_sample_kernel, sb=sb),
        grid=(B // sb,),
        in_specs=[pl.BlockSpec((sb, H, dh), lambda i: (i, 0, 0)),
                  pl.BlockSpec((None, sb, M, H, dh), lambda i: (layer, i, 0, 0, 0)),
                  pl.BlockSpec((None, sb, M, H, dh), lambda i: (layer, i, 0, 0, 0))],
        out_specs=pl.BlockSpec((sb, H, dh), lambda i: (i, 0, 0)),
        out_shape=jax.ShapeDtypeStruct((B, H, dh), F32),
        compiler_params=_cparams("parallel"),
    )(q.reshape(B, H, dh), ck, cv)
    return out.reshape(B, D)


def _lane_cat_masks(L):
    row = lax.broadcasted_iota(jnp.int32, (L, HW), 0)
    j = lax.broadcasted_iota(jnp.int32, (L, HW), 1) % DH
    r2 = lax.broadcasted_iota(jnp.int32, (HW, HW), 0) // DH
    c2 = lax.broadcasted_iota(jnp.int32, (HW, HW), 1) // DH
    return row >= j, row > j, row == j, r2 == c2


def _expand_bd(x, bd):
    return jnp.where(bd, jnp.concatenate([x] * HEADS, axis=0), jnp.zeros((), x.dtype))


def _seg_reduce(x, op, fill):
    lo = lax.broadcasted_iota(jnp.int32, (x.shape[0], LANES), 1) < DH
    parts = []
    for c in range(HW // LANES):
        xh = x[:, c * LANES:(c + 1) * LANES]
        a = op(jnp.where(lo, xh, fill), axis=-1, keepdims=True)
        b = op(jnp.where(lo, fill, xh), axis=-1, keepdims=True)
        parts.append(jnp.where(lo, a, b))
    return jnp.concatenate(parts, axis=-1)


def _head_expander(first_lane):
    r = lax.broadcasted_iota(jnp.int32, (GATE_PAD, HW), 0)
    c = lax.broadcasted_iota(jnp.int32, (GATE_PAD, HW), 1) // DH
    return (r == c + first_lane).astype(BF16)


def _chunk_tril(tt):
    r = lax.broadcasted_iota(jnp.int32, (tt, tt), 0)
    c = lax.broadcasted_iota(jnp.int32, (tt, tt), 1)
    return ((r // CHUNK == c // CHUNK) & (r >= c)).astype(BF16)


def _split3(x):
    hi = x.astype(BF16)
    r = x - hi.astype(F32)
    mid = r.astype(BF16)
    return hi, mid, (r - mid.astype(F32)).astype(BF16)


def _dot_sel_r(x, sel):
    hi, mid, lo = _split3(x)
    return (_dot(lo, sel) + _dot(mid, sel)) + _dot(hi, sel)


def _dot_sel_l(sel, x):
    hi, mid, lo = _split3(x)
    return (_dot(sel, lo) + _dot(sel, mid)) + _dot(sel, hi)


def _col_to_row(x, eye):
    return jnp.sum(jnp.where(eye, x, 0.0), axis=0, keepdims=True)


def _head_rms(x, ones_bd, g_row):
    ms = _dot_sel_r(x * x, ones_bd) * (1.0 / DH)
    return x * lax.rsqrt(ms + EPS) * g_row


def _mlstm_prompt_kernel(za_ref, gt_ref, bias_ref, norm_ref, ha_ref, c_ref, n_ref, m_ref,
                         c_scr, n_scr, m_scr, ix_scr, bx_scr, *, nchunks, group):
    tb = pl.program_id(1)
    L = CHUNK
    tril, _, eye, bd = _lane_cat_masks(L)
    ones_bd = bd.astype(BF16)

    @pl.when(tb == 0)
    def _():
        c_scr[...] = jnp.zeros_like(c_scr)
        n_scr[...] = jnp.zeros_like(n_scr)
        m_scr[...] = jnp.zeros_like(m_scr)

    gt = gt_ref[...] + bias_ref[...]
    b_cols = _dot_sel_l(_chunk_tril(nchunks * L), _log_sigmoid(gt))
    ix_scr[...] = _dot_sel_r(gt, _head_expander(0))
    bx_scr[...] = _dot_sel_r(b_cols, _head_expander(HEADS))

    def chunks(gi, carry):
        c_bd, n_row, m_x = carry
        rows = [pl.ds(pl.multiple_of((gi * group + j) * L, L), L) for j in range(group)]
        q = [(za_ref[r, 0:HW] * (DH ** -0.5)).astype(BF16) for r in rows]
        k = [za_ref[r, HW:2 * HW] for r in rows]
        v = [za_ref[r, 2 * HW:3 * HW].astype(BF16) for r in rows]
        i_x = [ix_scr[r, :] for r in rows]
        b_x = [bx_scr[r, :] for r in rows]

        m_in, n_in, kw, decay = [], [], [], []
        for kk, ii, bb in zip(k, i_x, b_x):
            b_last = bb[L - 1:L, :]
            g_x = b_last - bb + ii
            m_new = jnp.maximum(b_last + m_x, jnp.max(g_x, axis=0, keepdims=True))
            kw.append(kk * jnp.exp(g_x - m_new))
            decay.append(jnp.exp(b_last + m_x - m_new))
            m_in.append(m_x)
            n_in.append(n_row)
            n_row = decay[-1] * n_row + jnp.sum(kw[-1], axis=0, keepdims=True)
            m_x = m_new

        d_intra = [jnp.where(tril, bb - _col_to_row(bb, eye) + _col_to_row(ii, eye), -jnp.inf) for bb, ii in zip(b_x, i_x)]
        d_inter = [bb + mm for bb, mm in zip(b_x, m_in)]
        m_t = [jnp.maximum(_seg_reduce(di, jnp.max, -jnp.inf), de) for di, de in zip(d_intra, d_inter)]
        w_inter = [jnp.exp(de - mt) for de, mt in zip(d_inter, m_t)]
        s = [_dot_nt(qq, _expand_bd(kk.astype(BF16), bd)) * jnp.exp(di - mt) for qq, kk, di, mt in zip(q, k, d_intra, m_t)]
        kv = [jnp.where(bd, _dot(kwj.T.astype(BF16), vv), 0.0) for kwj, vv in zip(kw, v)]
        c_in = []
        for dj, kvj in zip(decay, kv):
            c_in.append(c_bd)
            c_bd = dj * c_bd + kvj
        num = [wi * _dot(qq, cc.astype(BF16)) + _dot(ss.astype(BF16), _expand_bd(vv, bd))
               for wi, qq, cc, ss, vv in zip(w_inter, q, c_in, s, v)]
        den = [wi * _dot((qq.astype(F32) * nn).astype(BF16), ones_bd) + _seg_reduce(ss, jnp.sum, 0.0)
               for wi, qq, nn, ss in zip(w_inter, q, n_in, s)]
        for r, nu, de, mt in zip(rows, num, den, m_t):
            hh = nu / jnp.maximum(jnp.abs(de), jnp.exp(-mt))
            ha_ref[r, :] = _head_rms(hh, ones_bd, norm_ref[...]) * _sigmoid(za_ref[r, 3 * HW:4 * HW])
        return c_bd, n_row, m_x

    c_bd, n_row, m_x = lax.fori_loop(0, nchunks // group, chunks, (c_scr[...], n_scr[...], m_scr[...]))
    c_scr[...] = c_bd
    n_scr[...] = n_row
    m_scr[...] = m_x

    @pl.when(tb == pl.num_programs(1) - 1)
    def _():
        for h in range(HEADS):
            c_ref[0, h] = c_scr[h * DH:(h + 1) * DH, h * DH:(h + 1) * DH]
        n_ref[0] = n_scr[...]
        m_ref[0] = m_scr[...]


def mlstm_prompt(za, gates, b_i, b_f, norm, B, S, tt):
    assert S % tt == 0 and tt % CHUNK == 0
    nt = S // tt
    bias = jnp.zeros((1, GATE_PAD), F32).at[0, 0:HEADS].set(b_i).at[0, HEADS:2 * HEADS].set(b_f)
    ha, c, n, m = pl.pallas_call(
        functools.partial(_mlstm_prompt_kernel, nchunks=tt // CHUNK, group=math.gcd(tt // CHUNK, 4)),
        grid=(B, nt),
        in_specs=[pl.BlockSpec((tt, 4 * HW), lambda b, t: (b * nt + t, 0)),
                  pl.BlockSpec((tt, GATE_PAD), lambda b, t: (b * nt + t, 0)),
                  pl.BlockSpec((1, GATE_PAD), lambda b, t: (0, 0)),
                  pl.BlockSpec((1, HW), lambda b, t: (0, 0))],
        out_specs=[pl.BlockSpec((tt, HW), lambda b, t: (b * nt + t, 0)),
                   pl.BlockSpec((1, HEADS, DH, DH), lambda b, t: (b, 0, 0, 0)),
                   pl.BlockSpec((1, 1, HW), lambda b, t: (b, 0, 0)),
                   pl.BlockSpec((1, 1, HW), lambda b, t: (b, 0, 0))],
        out_shape=[jax.ShapeDtypeStruct((B * S, HW), F32),
                   jax.ShapeDtypeStruct((B, HEADS, DH, DH), F32),
                   jax.ShapeDtypeStruct((B, 1, HW), F32),
                   jax.ShapeDtypeStruct((B, 1, HW), F32)],
        scratch_shapes=[pltpu.VMEM((HW, HW), F32), pltpu.VMEM((1, HW), F32), pltpu.VMEM((1, HW), F32),
                        pltpu.VMEM((tt, HW), F32), pltpu.VMEM((tt, HW), F32)],
        compiler_params=_cparams("parallel", "arbitrary"),
    )(za, gates, bias, norm.reshape(1, HW))
    return ha, c, n.reshape(B, HEADS, DH), m[:, 0, ::DH]


def _gdn_prompt_kernel(zc_ref, zg_ref, gt_ref, cw_ref, alog_ref, dtb_ref, norm_ref,
                       hc_ref, s_ref, tail_ref,
                       s_scr, xp_scr, q_scr, k_scr, v_scr, beta_scr, g_scr, uv_scr, wq_scr, qkm_scr, kwt_scr,
                       *, nseq, nchunks, group):
    tb = pl.program_id(1)
    L = CHUNK
    tt = nchunks * L
    pad = SUBLANES
    tril, strict, eye, bd = _lane_cat_masks(L)
    ones_bd = bd.astype(BF16)

    @pl.when(tb == 0)
    def _():
        s_scr[...] = jnp.zeros_like(s_scr)
        xp_scr[:, 0:pad, :] = jnp.zeros((nseq, pad, 3 * HW), F32)

    for i in range(nseq):
        sr = slice(i * tt, (i + 1) * tt)
        x = zc_ref[i]
        xp_scr[i, pad:pad + tt, :] = x
        y = cw_ref[CONV_K - 1:CONV_K, :] * x
        for j in range(CONV_K - 1):
            y = y + cw_ref[j:j + 1, :] * xp_scr[i, pl.ds(pad - (CONV_K - 1) + j, tt), :]
        xp_scr[i, 0:pad, :] = x[tt - pad:tt, :]
        y = _silu(y)
        q_raw, k_raw = y[:, 0:HW], y[:, HW:2 * HW]
        q_scr[sr, :] = (q_raw * lax.rsqrt(_dot_sel_r(q_raw * q_raw, ones_bd) + EPS) * (DH ** -0.5)).astype(BF16)
        k_scr[sr, :] = k_raw * lax.rsqrt(_dot_sel_r(k_raw * k_raw, ones_bd) + EPS)
        v_scr[sr, :] = y[:, 2 * HW:3 * HW]
        gt = gt_ref[i]
        beta_scr[sr, :] = _dot_sel_r(_sigmoid(gt), _head_expander(2 * HEADS))
        la_cols = -jnp.exp(alog_ref[...]) * _softplus(gt + dtb_ref[...])
        g_scr[sr, :] = _dot_sel_r(_dot_sel_l(_chunk_tril(tt), la_cols), _head_expander(3 * HEADS))

    def prepare(gi, carry):
        cis = [gi * group + j for j in range(group)]
        rows = [pl.ds(pl.multiple_of(ci * L, L), L) for ci in cis]
        k = [k_scr[r, :] for r in rows]
        g_x = [g_scr[r, :] for r in rows]
        beta_row = [_col_to_row(beta_scr[r, :], eye) for r in rows]
        dec_incl = [jnp.where(tril, jnp.exp(jnp.where(tril, g - _col_to_row(g, eye), 0.0)), 0.0) for g in g_x]
        k_bd = [_expand_bd(kk.astype(BF16), bd) for kk in k]
        n0 = [-(jnp.where(strict, d, 0.0) * _dot_nt(kk.astype(BF16), kbd) * br)
              for d, kk, kbd, br in zip(dec_incl, k, k_bd, beta_row)]
        for r, d, kbd, br in zip(rows, dec_incl, k_bd, beta_row):
            qkm_scr[r, :] = (_dot_nt(q_scr[r, :], kbd) * d * br).astype(BF16)

        p = [_dot(n.astype(BF16), _expand_bd(n.astype(BF16), bd)) for n in n0]
        m = n0
        steps = int(math.log2(L)) - 1
        for i in range(steps):
            p_bd = [_expand_bd(pp.astype(BF16), bd) for pp in p]
            if i < steps - 1:
                pm = [_dot(jnp.concatenate([pp, mm], axis=0).astype(BF16), pbd) for pp, mm, pbd in zip(p, m, p_bd)]
                p_next, mp = [x[0:L] for x in pm], [x[L:2 * L] for x in pm]
            else:
                p_next, mp = None, [_dot(mm.astype(BF16), pbd) for mm, pbd in zip(m, p_bd)]
            m = [mm + pp + x for mm, pp, x in zip(m, p, mp)]
            p = p_next

        for ci, r, kk, g, mm in zip(cis, rows, k, g_x, m):
            v = v_scr[r, :]
            egk = jnp.exp(g) * kk
            rhs_bd = jnp.concatenate([_expand_bd(v.astype(BF16), bd), _expand_bd(egk.astype(BF16), bd)], axis=1)
            mr = _dot(mm.astype(BF16), rhs_bd)
            uv_scr[r, :] = v + mr[:, 0:HW]
            wq_rows = pl.multiple_of(ci * 2 * L, 2 * L)
            wq_scr[pl.ds(wq_rows, L), :] = (egk + mr[:, HW:2 * HW]).astype(BF16)
            wq_scr[pl.ds(wq_rows + L, L), :] = q_scr[r, :]
            kw = kk * (jnp.exp(g[L - 1:L, :] - g) * beta_scr[r, :])
            kwt_scr[pl.ds(pl.multiple_of(ci * HW, HW), HW), :] = kw.T.astype(BF16)
        return carry

    lax.fori_loop(0, nseq * nchunks // group, prepare, 0)

    def advance(c, states):
        cis = [i * nchunks + c for i in range(nseq)]
        rows = [pl.ds(pl.multiple_of(ci * L, L), L) for ci in cis]
        wqs = [_dot(wq_scr[pl.ds(pl.multiple_of(ci * 2 * L, 2 * L), 2 * L), :], s.astype(BF16))
               for ci, s in zip(cis, states)]
        ub = [(uv_scr[r, :] - x[0:L]).astype(BF16) for r, x in zip(rows, wqs)]
        new = [jnp.exp(g_scr[pl.ds(pl.multiple_of(ci * L + L - SUBLANES, SUBLANES), SUBLANES), :][SUBLANES - 1:, :]) * s
               + jnp.where(bd, _dot(kwt_scr[pl.ds(pl.multiple_of(ci * HW, HW), HW), :], u), 0.0)
               for ci, s, u in zip(cis, states, ub)]
        for i, (r, x, u) in enumerate(zip(rows, wqs, ub)):
            o = jnp.exp(g_scr[r, :]) * x[L:2 * L] + _dot(qkm_scr[r, :], _expand_bd(u, bd))
            hc_ref[i, pl.ds(pl.multiple_of(c * L, L), L), :] = (
                _head_rms(o, ones_bd, norm_ref[...]) * _silu(zg_ref[i, pl.ds(pl.multiple_of(c * L, L), L), :]))
        return tuple(new)

    states = lax.fori_loop(0, nchunks, advance, tuple(s_scr[i] for i in range(nseq)))
    for i in range(nseq):
        s_scr[i] = states[i]

    @pl.when(tb == pl.num_programs(1) - 1)
    def _():
        for i in range(nseq):
            for h in range(HEADS):
                s_ref[i, h] = s_scr[i, h * DH:(h + 1) * DH, h * DH:(h + 1) * DH]
            tail_ref[i] = xp_scr[i, 0:pad, :]


def _gdn_gate_rows(a_log, dt_bias):
    z = jnp.zeros((1, GATE_PAD), F32)
    return (z.at[0, 3 * HEADS:4 * HEADS].set(a_log), z.at[0, 3 * HEADS:4 * HEADS].set(dt_bias))


def gdn_prompt(zc, zg, gates, conv_w, a_log, dt_bias, norm, B, S, tt):
    assert S % tt == 0 and tt % CHUNK == 0
    nt = S // tt
    nseq = math.gcd(B, GDN_SEQS)
    rows = nseq * tt
    alog, dtb = _gdn_gate_rows(a_log, dt_bias)
    blk = lambda w: pl.BlockSpec((nseq, tt, w), lambda b, t: (b, t, 0))
    row = lambda w: pl.BlockSpec((1, w), lambda b, t: (0, 0))
    hc, s, tail = pl.pallas_call(
        functools.partial(_gdn_prompt_kernel, nseq=nseq, nchunks=tt // CHUNK, group=math.gcd(rows // CHUNK, 8)),
        grid=(B // nseq, nt),
        in_specs=[blk(3 * HW), blk(HW), blk(GATE_PAD),
                  pl.BlockSpec((CONV_K, 3 * HW), lambda b, t: (0, 0)), row(GATE_PAD), row(GATE_PAD), row(HW)],
        out_specs=[blk(HW),
                   pl.BlockSpec((nseq, HEADS, DH, DH), lambda b, t: (b, 0, 0, 0)),
                   pl.BlockSpec((nseq, SUBLANES, 3 * HW), lambda b, t: (b, 0, 0))],
        out_shape=[jax.ShapeDtypeStruct((B, S, HW), F32),
                   jax.ShapeDtypeStruct((B, HEADS, DH, DH), F32),
                   jax.ShapeDtypeStruct((B, SUBLANES, 3 * HW), F32)],
        scratch_shapes=[pltpu.VMEM((nseq, HW, HW), F32), pltpu.VMEM((nseq, tt + SUBLANES, 3 * HW), F32),
                        pltpu.VMEM((rows, HW), BF16), pltpu.VMEM((rows, HW), F32), pltpu.VMEM((rows, HW), F32),
                        pltpu.VMEM((rows, HW), F32), pltpu.VMEM((rows, HW), F32), pltpu.VMEM((rows, HW), F32),
                        pltpu.VMEM((2 * rows, HW), BF16), pltpu.VMEM((rows, HW), BF16),
                        pltpu.VMEM((rows // CHUNK * HW, CHUNK), BF16)],
        compiler_params=_cparams("parallel", "arbitrary"),
    )(zc.reshape(B, S, 3 * HW), zg.reshape(B, S, HW), gates.reshape(B, S, GATE_PAD), conv_w, alog, dtb,
      jnp.tile(norm, HEADS).reshape(1, HW))
    return hc.reshape(B * S, HW), s, tail[:, SUBLANES - (CONV_K - 1):, :]


def _s5_prep_kernel(are_ref, aim_ref, ldt_ref, bre_ref, bim_ref, lre_ref, lim_ref, bbre_ref, bbim_ref):
    a_re, a_im = are_ref[...], aim_ref[...]
    dt = jnp.exp(ldt_ref[...])
    mag = jnp.exp(a_re * dt)
    lam_re, lam_im = mag * jnp.cos(a_im * dt), mag * jnp.sin(a_im * dt)
    lre_ref[...] = lam_re
    lim_ref[...] = lam_im
    nr, ni = lam_re - 1.0, lam_im
    den = a_re * a_re + a_im * a_im
    coef_re = (nr * a_re + ni * a_im) / den
    coef_im = (ni * a_re - nr * a_im) / den
    b_re, b_im = bre_ref[...], bim_ref[...]
    bbre_ref[...] = coef_re * b_re - coef_im * b_im
    bbim_ref[...] = coef_re * b_im + coef_im * b_re


def s5_params(lp):
    G, N, P = S5_GROUPS, S5_N, S5_P
    row = lambda a: a.astype(F32).reshape(1, G * N)
    to_pn = lambda b: jnp.transpose(b.astype(F32), (2, 0, 1)).reshape(P, G * N)
    shp = [jax.ShapeDtypeStruct((1, G * N), F32)] * 2 + [jax.ShapeDtypeStruct((P, G * N), F32)] * 2
    lam_re, lam_im, bb_re, bb_im = pl.pallas_call(_s5_prep_kernel, out_shape=shp)(
        row(lp['s5_a_re']), row(lp['s5_a_im']), row(jnp.repeat(lp['s5_log_dt'][:, None], N, axis=1)),
        to_pn(lp['s5_b_re']), to_pn(lp['s5_b_im']))
    eye = jnp.eye(S5_GB, dtype=F32)

    def w_in_blocks(bb):
        b4 = bb.reshape(P, S5_NBLK, S5_GB, N)
        return jnp.einsum('pbgn,gh->bgphn', b4, eye).reshape(S5_NBLK, S5_GB * P, S5_GB * N)

    def w_out_blocks(c):
        c4 = c.astype(F32).reshape(S5_NBLK, S5_GB, P, N)
        return jnp.einsum('bgpn,gh->bgnhp', c4, eye).reshape(S5_NBLK, S5_GB * N, S5_GB * P)

    w_in = jnp.concatenate([w_in_blocks(bb_re), w_in_blocks(bb_im)], axis=-1).astype(BF16)
    return {'lam_re': lam_re, 'lam_im': lam_im, 'w_in': w_in,
            'w_out_re': w_out_blocks(lp['s5_c_re']).astype(BF16),
            'w_out_im': (-w_out_blocks(lp['s5_c_im'])).astype(BF16),
            'd': lp['s5_d'].astype(F32).reshape(1, S5_WIDTH), 'w_glu': lp['s5_w_glu'].astype(BF16)}


def _s5_kernel(u_ref, h0r_ref, h0i_ref, lamr_ref, lami_ref, win_ref, wor_ref, woi_ref, d_ref, wglu_ref,
               ys_ref, h1r_ref, h1i_ref, hr_scr, hi_scr, br_scr, bi_scr, *, nseq, rows, bb, lane_blk):
    tb = pl.program_id(0)
    nsteps = nseq * rows // bb
    blk_in, blk_st = S5_GB * S5_P, S5_GB * S5_N

    @pl.when(tb == 0)
    def _():
        hr_scr[...] = h0r_ref[...]
        hi_scr[...] = h0i_ref[...]

    if nseq > 1:
        u = jnp.swapaxes(u_ref[...], 0, 1).reshape(nseq * rows, S5_WIDTH)
    else:
        u = u_ref[0]
    ub = u.astype(BF16)
    for blk in range(S5_NBLK):
        bu = _dot(ub[:, blk * blk_in:(blk + 1) * blk_in], win_ref[blk])
        br_scr[:, blk * blk_st:(blk + 1) * blk_st] = bu[:, 0:blk_st]
        bi_scr[:, blk * blk_st:(blk + 1) * blk_st] = bu[:, blk_st:2 * blk_st]

    for lb in range(S5_STATE // lane_blk):
        ls = slice(lb * lane_blk, (lb + 1) * lane_blk)
        lr = jnp.broadcast_to(lamr_ref[:, ls], (bb, lane_blk))
        li = jnp.broadcast_to(lami_ref[:, ls], (bb, lane_blk))

        def step(t, carry):
            hr, hi = carry
            r = pl.ds(pl.multiple_of(t * bb, bb), bb)
            nr = lr * hr - li * hi + br_scr[r, ls]
            ni = lr * hi + li * hr + bi_scr[r, ls]
            br_scr[r, ls] = nr
            bi_scr[r, ls] = ni
            return nr, ni

        hr, hi = lax.fori_loop(0, nsteps, step, (hr_scr[:, ls], hi_scr[:, ls]), unroll=min(nsteps, 8))
        hr_scr[:, ls] = hr
        hi_scr[:, ls] = hi

    ys = []
    for blk in range(S5_NBLK):
        st = slice(blk * blk_st, (blk + 1) * blk_st)
        ys.append(_dot(br_scr[:, st].astype(BF16), wor_ref[blk]) + _dot(bi_scr[:, st].astype(BF16), woi_ref[blk]))
    gy = jax.nn.gelu(jnp.concatenate(ys, axis=-1) + d_ref[...] * u)
    out = gy * _sigmoid(_dot(gy.astype(BF16), wglu_ref[...]))
    if nseq > 1:
        ys_ref[...] = jnp.swapaxes(out.reshape(rows, nseq, S5_WIDTH), 0, 1)
    else:
        ys_ref[0] = out

    @pl.when(tb == pl.num_programs(0) - 1)
    def _():
        h1r_ref[...] = hr_scr[...]
        h1i_ref[...] = hi_scr[...]


def s5_mixer(u, h0_re, h0_im, sp, tt, single_step):
    B = h0_re.shape[0]
    nseq, S = (1, 1) if single_step else (B, u.shape[1])
    rows = B if single_step else tt
    assert S % tt == 0 and B % SUBLANES == 0
    assert u.shape == ((1, B, S5_WIDTH) if single_step else (B, S, S5_WIDTH))
    lane_blk = max(LANES, min(S5_STATE, (SUBLANES * SUBLANES * LANES) // B))
    full = lambda shape: pl.BlockSpec(shape, lambda t: (0,) * len(shape))
    return pl.pallas_call(
        functools.partial(_s5_kernel, nseq=nseq, rows=rows, bb=B, lane_blk=lane_blk),
        grid=(S // tt,),
        in_specs=[pl.BlockSpec((nseq, rows, S5_WIDTH), lambda t: (0, t, 0)),
                  full((B, S5_STATE)), full((B, S5_STATE)), full((1, S5_STATE)), full((1, S5_STATE)),
                  full(sp['w_in'].shape), full(sp['w_out_re'].shape), full(sp['w_out_im'].shape),
                  full((1, S5_WIDTH)), full((S5_WIDTH, S5_WIDTH))],
        out_specs=[pl.BlockSpec((nseq, rows, S5_WIDTH), lambda t: (0, t, 0)),
                   full((B, S5_STATE)), full((B, S5_STATE))],
        out_shape=[jax.ShapeDtypeStruct(u.shape, F32),
                   jax.ShapeDtypeStruct((B, S5_STATE), F32), jax.ShapeDtypeStruct((B, S5_STATE), F32)],
        scratch_shapes=[pltpu.VMEM((B, S5_STATE), F32), pltpu.VMEM((B, S5_STATE), F32),
                        pltpu.VMEM((nseq * rows, S5_STATE), F32), pltpu.VMEM((nseq * rows, S5_STATE), F32)],
        compiler_params=_cparams("arbitrary"),
    )(u, h0_re, h0_im, sp['lam_re'], sp['lam_im'], sp['w_in'], sp['w_out_re'], sp['w_out_im'], sp['d'], sp['w_glu'])


def _ones_bd():
    r = lax.broadcasted_iota(jnp.int32, (HW, HW), 0) // DH
    c = lax.broadcasted_iota(jnp.int32, (HW, HW), 1) // DH
    return (r == c).astype(BF16)


def _mlstm_sample_kernel(za_ref, gt_ref, bias_ref, norm_ref, c_ref, n_ref, m_ref,
                         ha_ref, c_out, n_out, m_out, q_scr, kw_scr, h_scr):
    za = za_ref[...]
    q_scr[...] = (za[:, 0:HW] * (DH ** -0.5)).T
    k_t = za[:, HW:2 * HW].T
    v_t = za[:, 2 * HW:3 * HW].T
    g_t = (gt_ref[...] + bias_ref[...]).T
    m_out[...] = jnp.zeros_like(m_out)
    for h in range(HEADS):
        hs = slice(h * DH, (h + 1) * DH)
        i_h = g_t[h:h + 1, :]
        bm = _log_sigmoid(g_t[HEADS + h:HEADS + h + 1, :]) + m_ref[h:h + 1, :]
        m_t = jnp.maximum(i_h, bm)
        w_in = jnp.exp(i_h - m_t)
        w_st = jnp.exp(bm - m_t)
        q_h, k_h, v_h = q_scr[hs, :], k_t[hs, :], v_t[hs, :]
        s = jnp.sum(q_h * k_h, axis=0, keepdims=True) * w_in
        kw_scr[hs, :] = k_h * w_in

        def body(d, acc):
            r = h * DH + d
            rows = pl.ds(pl.multiple_of(r * DH, DH), DH)
            c_hd = c_ref[rows, :]
            c_out[rows, :] = w_st * c_hd + kw_scr[pl.ds(r, 1), :] * v_h
            return acc + q_scr[pl.ds(r, 1), :] * c_hd

        qc = lax.fori_loop(0, DH, body, jnp.zeros((DH, za.shape[0]), F32), unroll=4)
        n_h = n_ref[hs, :]
        num = w_st * qc + s * v_h
        den = w_st * jnp.sum(q_h * n_h, axis=0, keepdims=True) + s
        h_scr[hs, :] = num / jnp.maximum(jnp.abs(den), jnp.exp(-m_t))
        n_out[hs, :] = w_st * n_h + kw_scr[hs, :]
        m_out[h:h + 1, :] = m_t
    ha_ref[...] = _head_rms(h_scr[...].T, _ones_bd(), norm_ref[...]) * _sigmoid(za[:, 3 * HW:4 * HW])


def mlstm_sample(za, gates, b_i, b_f, norm, c_t, n_t, m_t):
    B = za.shape[0]
    bias = jnp.zeros((1, GATE_PAD), F32).at[0, 0:HEADS].set(b_i).at[0, HEADS:2 * HEADS].set(b_f)
    shp = lambda *s: jax.ShapeDtypeStruct(s, F32)
    return pl.pallas_call(
        _mlstm_sample_kernel,
        out_shape=[shp(B, HW), shp(HW * DH, B), shp(HW, B), shp(SUBLANES, B)],
        scratch_shapes=[pltpu.VMEM((HW, B), F32), pltpu.VMEM((HW, B), F32), pltpu.VMEM((HW, B), F32)],
        compiler_params=pltpu.CompilerParams(vmem_limit_bytes=VMEM_LIMIT),
    )(za, gates, bias, norm.reshape(1, HW), c_t, n_t, m_t)


def _gdn_sample_kernel(zc_ref, zg_ref, gt_ref, buf_ref, cw_ref, alog_ref, dtb_ref, norm_ref, s_ref,
                       hc_ref, s_out, buf_out, q_scr, k_scr, o_scr):
    W3 = 3 * HW
    x = zc_ref[...]
    y = cw_ref[CONV_K - 1:CONV_K, :] * x
    for j in range(CONV_K - 1):
        y = y + cw_ref[j:j + 1, :] * buf_ref[:, j * W3:(j + 1) * W3]
    buf_out[:, 0:(CONV_K - 2) * W3] = buf_ref[:, W3:(CONV_K - 1) * W3]
    buf_out[:, (CONV_K - 2) * W3:(CONV_K - 1) * W3] = x
    y = _silu(y)
    ones_bd = _ones_bd()
    q_raw, k_raw = y[:, 0:HW], y[:, HW:2 * HW]
    q_scr[...] = (q_raw * lax.rsqrt(_dot_sel_r(q_raw * q_raw, ones_bd) + EPS) * (DH ** -0.5)).T
    k_scr[...] = (k_raw * lax.rsqrt(_dot_sel_r(k_raw * k_raw, ones_bd) + EPS)).T
    v_t = y[:, 2 * HW:3 * HW].T
    gt = gt_ref[...]
    beta_t = _sigmoid(gt).T
    la_t = (-jnp.exp(alog_ref[...]) * _softplus(gt + dtb_ref[...])).T
    nb = x.shape[0]
    for h in range(HEADS):
        hs = slice(h * DH, (h + 1) * DH)
        beta = beta_t[2 * HEADS + h:2 * HEADS + h + 1, :]
        eg = jnp.exp(la_t[3 * HEADS + h:3 * HEADS + h + 1, :])
        q_h, k_h, v_h = q_scr[hs, :], k_scr[hs, :], v_t[hs, :]

        def read(d, acc):
            ks, qs = acc
            r = h * DH + d
            s_hd = s_ref[pl.ds(pl.multiple_of(r * DH, DH), DH), :]
            return ks + k_scr[pl.ds(r, 1), :] * s_hd, qs + q_scr[pl.ds(r, 1), :] * s_hd

        zero = jnp.zeros((DH, nb), F32)
        ks, qs = lax.fori_loop(0, DH, read, (zero, zero), unroll=4)
        u = v_h - eg * ks
        o_scr[hs, :] = eg * qs + (jnp.sum(q_h * k_h, axis=0, keepdims=True) * beta) * u

        def write(d, carry):
            r = h * DH + d
            rows = pl.ds(pl.multiple_of(r * DH, DH), DH)
            s_out[rows, :] = eg * s_ref[rows, :] + (beta * k_scr[pl.ds(r, 1), :]) * u
            return carry

        lax.fori_loop(0, DH, write, 0, unroll=4)
    hc_ref[...] = _head_rms(o_scr[...].T, ones_bd, norm_ref[...]) * _silu(zg_ref[...])


def gdn_sample(zc, zg, gates, buf, conv_w, a_log, dt_bias, norm, s_t):
    B = zc.shape[0]
    alog, dtb = _gdn_gate_rows(a_log, dt_bias)
    shp = lambda *s: jax.ShapeDtypeStruct(s, F32)
    return pl.pallas_call(
        _gdn_sample_kernel,
        out_shape=[shp(B, HW), shp(HW * DH, B), shp(B, (CONV_K - 1) * 3 * HW)],
        scratch_shapes=[pltpu.VMEM((HW, B), F32), pltpu.VMEM((HW, B), F32), pltpu.VMEM((HW, B), F32)],
        compiler_params=pltpu.CompilerParams(vmem_limit_bytes=VMEM_LIMIT),
    )(zc, zg, gates, buf, conv_w, alog, dtb, jnp.tile(norm, HEADS).reshape(1, HW), s_t)


FFN_CHUNK = 1408
IN_SEGMENTS = (4 * HW, S5_WIDTH, 3 * HW, HW, GATE_PAD)


def _tile(n, pref):
    return pref if n % pref == 0 else n


def prep_weights(p):
    w = p['w_in']
    a, g4 = 4 * HW, HEADS
    o_u = a + 2 * g4
    o_c = o_u + S5_WIDTH
    o_g = o_c + 3 * HW
    o_b = o_g + HW
    gate_cols = jnp.concatenate([w[:, :, a:a + 2 * g4], w[:, :, o_b:o_b + 2 * g4],
                                 jnp.zeros(w.shape[:2] + (GATE_PAD - 4 * g4,), w.dtype)], axis=2)
    w_in = jnp.concatenate([w[:, :, 0:a], w[:, :, o_u:o_c], w[:, :, o_c:o_g], w[:, :, o_g:o_b], gate_cols], axis=2)
    bf = lambda n: p[n].astype(BF16)
    return {'w_in': w_in.astype(BF16), 'w_out': bf('w_out'), 'w_mq': bf('w_mq'), 'w_mo': bf('w_mo'),
            'w_mkv': jnp.concatenate([p['w_mk'], p['w_mv']], axis=2).astype(BF16),
            'w_gate': bf('w_gate'), 'w_up': bf('w_up'), 'w_down': bf('w_down')}


def mixer_prompt(x, lp, W, layer, B, S):
    T = B * S
    tm = _tile(T, 512)
    za, zu, zc, zg, gates = norm_matmul(x, lp['norm_mix'], W['w_in'], layer, IN_SEGMENTS, tm)
    tt = _tile(S, 512)
    ha, c1, n1, m1 = mlstm_prompt(za, gates, lp['mlstm_b_i'], lp['mlstm_b_f'], lp['mlstm_norm'], B, S, tt)
    h0 = jnp.zeros((B, S5_STATE), F32)
    ys3, r1, i1 = s5_mixer(zu.reshape(B, S, S5_WIDTH), h0, h0, lp['s5'], _tile(S, 128), False)
    ys = ys3.reshape(T, S5_WIDTH)
    hc, s1, buf1 = gdn_prompt(zc, zg, gates, lp['gdn_conv_w'], lp['gdn_a_log'], lp['gdn_dt_bias'], lp['gdn_norm'], B, S, _tile(S, 256))
    return [ha, ys, hc], (c1, n1, m1, r1.reshape(B, S5_GROUPS, S5_N), i1.reshape(B, S5_GROUPS, S5_N), s1, buf1)


def mixer_sample(x, st, lp, W, layer):
    B = x.shape[0]
    c0, n0, m0, r0, i0, s0, buf0 = st
    za, zu, zc, zg, gates = norm_matmul(x, lp['norm_mix'], W['w_in'], layer, IN_SEGMENTS, B)
    m_t = jnp.zeros((SUBLANES, B), F32).at[0:HEADS, :].set(m0.T)
    ha, c1t, n1t, m1t = mlstm_sample(za, gates, lp['mlstm_b_i'], lp['mlstm_b_f'], lp['mlstm_norm'],
                                     c0.reshape(B, HW * DH).T, n0.reshape(B, HW).T, m_t)
    ys3, r1, i1 = s5_mixer(zu.reshape(1, B, S5_WIDTH), r0.reshape(B, S5_STATE), i0.reshape(B, S5_STATE), lp['s5'], 1, True)
    hc, s1t, buf1 = gdn_sample(zc, zg, gates, buf0.reshape(B, (CONV_K - 1) * 3 * HW), lp['gdn_conv_w'],
                               lp['gdn_a_log'], lp['gdn_dt_bias'], lp['gdn_norm'], s0.reshape(B, HW * DH).T)
    x1 = matmul_residual(x, [ha, ys3.reshape(B, S5_WIDTH), hc], W['w_out'], layer, B)
    return x1, (c1t.T.reshape(B, HEADS, DH, DH), n1t.T.reshape(B, HEADS, DH), m1t[0:HEADS, :].T,
                r1.reshape(B, S5_GROUPS, S5_N), i1.reshape(B, S5_GROUPS, S5_N),
                s1t.T.reshape(B, HEADS, DH, DH), buf1.reshape(B, CONV_K - 1, 3 * HW))


def mem_kv(mem, lp, W, layer):
    D = mem.shape[1]
    return norm_matmul(mem, lp['norm_mem'], W['w_mkv'], layer, (D, D), _tile(mem.shape[0], 512))


def xattn_ffn_prompt(x, acts, mk, mv, lp, W, layer, S, norm_final, final):
    T, D = x.shape
    x2 = xattn_prompt(x, acts, W['w_out'], lp['norm_xattn'], W['w_mq'], mk, mv, W['w_mo'], layer, S, _tile(S, 512))
    return ffn(x2, lp['norm_ffn'], W['w_gate'], W['w_up'], W['w_down'], layer, norm_final, final, _tile(T, 512),
               FFN_CHUNK)


def xattn_ffn_sample(x, ck, cv, lp, W, layer, norm_final, final):
    B, D = x.shape
    (q,) = norm_matmul(x, lp['norm_xattn'], W['w_mq'], layer, (D,), B)
    o = xattn_sample(q, ck, cv, layer, 4)
    x2 = matmul_residual(x, [o], W['w_mo'], layer, B)
    return ffn(x2, lp['norm_ffn'], W['w_gate'], W['w_up'], W['w_down'], layer, norm_final, final, B, FFN_CHUNK)


LAYER_PARAMS = ('norm_mix', 'w_in', 'w_out', 'mlstm_b_i', 'mlstm_b_f', 'mlstm_norm', 's5_a_re', 's5_a_im', 's5_log_dt',
                's5_b_re', 's5_b_im', 's5_c_re', 's5_c_im', 's5_d', 's5_w_glu', 'gdn_conv_w', 'gdn_a_log',
                'gdn_dt_bias', 'gdn_norm', 'norm_xattn', 'norm_mem', 'w_mq', 'w_mk', 'w_mv', 'w_mo', 'norm_ffn',
                'w_gate', 'w_up', 'w_down')


def kernel(x_prompt, x_sample, mem_prompt, cache_mem_k, cache_mem_v, state_mlstm_c, state_mlstm_n, state_mlstm_m, state_s5_re, state_s5_im, state_gdn, state_gdn_conv, norm_mix, w_in, w_out, mlstm_b_i, mlstm_b_f, mlstm_norm, s5_a_re, s5_a_im, s5_log_dt, s5_b_re, s5_b_im, s5_c_re, s5_c_im, s5_d, s5_w_glu, gdn_conv_w, gdn_a_log, gdn_dt_bias, gdn_norm, norm_xattn, norm_mem, w_mq, w_mk, w_mv, w_mo, norm_ffn, w_gate, w_up, w_down, norm_final):
    stacked = dict(norm_mix=norm_mix, w_in=w_in, w_out=w_out, mlstm_b_i=mlstm_b_i, mlstm_b_f=mlstm_b_f,
                   mlstm_norm=mlstm_norm, s5_a_re=s5_a_re, s5_a_im=s5_a_im, s5_log_dt=s5_log_dt, s5_b_re=s5_b_re,
                   s5_b_im=s5_b_im, s5_c_re=s5_c_re, s5_c_im=s5_c_im, s5_d=s5_d, s5_w_glu=s5_w_glu,
                   gdn_conv_w=gdn_conv_w, gdn_a_log=gdn_a_log, gdn_dt_bias=gdn_dt_bias, gdn_norm=gdn_norm,
                   norm_xattn=norm_xattn, norm_mem=norm_mem, w_mq=w_mq, w_mk=w_mk, w_mv=w_mv, w_mo=w_mo,
                   norm_ffn=norm_ffn, w_gate=w_gate, w_up=w_up, w_down=w_down)
    B, S, D = x_prompt.shape
    Bs = x_sample.shape[0]
    M = mem_prompt.shape[1]
    depth = w_in.shape[0]
    xp = x_prompt.reshape(B * S, D)
    xs = x_sample.reshape(Bs, D)
    mem = mem_prompt.reshape(B * M, D)
    cache_k, cache_v = cache_mem_k, cache_mem_v
    mem_k, mem_v, st_p, st_s = [], [], [], []
    W = prep_weights(stacked)
    for l in range(depth):
        lp = {n: stacked[n][l] for n in LAYER_PARAMS if n not in W}
        lp['s5'] = s5_params(lp)
        last = l == depth - 1
        acts, sp = mixer_prompt(xp, lp, W, l, B, S)
        mk, mv = mem_kv(mem, lp, W, l)
        xp = xattn_ffn_prompt(xp, acts, mk.reshape(B, M, D), mv.reshape(B, M, D), lp, W, l, S, norm_final, last)
        st_in = (state_mlstm_c[l], state_mlstm_n[l], state_mlstm_m[l], state_s5_re[l], state_s5_im[l],
                 state_gdn[l], state_gdn_conv[l])
        xs, ss = mixer_sample(xs, st_in, lp, W, l)
        xs = xattn_ffn_sample(xs, cache_k, cache_v, lp, W, l, norm_final, last)
        mem_k.append(mk.reshape(B, M, X_HEADS, D // X_HEADS))
        mem_v.append(mv.reshape(B, M, X_HEADS, D // X_HEADS))
        st_p.append(sp)
        st_s.append(ss)
    stack = lambda lst: [jnp.stack([st[i] for st in lst]) for i in range(7)]
    return (xp.reshape(B, S, D), xs.reshape(Bs, 1, D), jnp.stack(mem_k), jnp.stack(mem_v),
            *stack(st_p), *stack(st_s))
```

```python
import functools
import math

import jax
import jax.numpy as jnp
from jax import lax
from jax.experimental import pallas as pl
from jax.experimental.pallas import tpu as pltpu

F32 = jnp.float32
BF16 = jnp.bfloat16
EPS = 1e-6

HEADS = 4
DH = 64
HW = HEADS * DH
CHUNK = 64
S5_P = 16
S5_N = 64
S5_GROUPS = 32
S5_WIDTH = S5_GROUPS * S5_P
S5_STATE = S5_GROUPS * S5_N
S5_GB = 8
S5_NBLK = S5_GROUPS // S5_GB
CONV_K = 4
GDN_SEQS = 4
X_HEADS = 4
GATE_PAD = 128
LANES = 128
SUBLANES = 8
VMEM_LIMIT = 48 * 1024 * 1024


def _cparams(*sem):
    return pltpu.CompilerParams(dimension_semantics=sem, vmem_limit_bytes=VMEM_LIMIT)


def _rms(x, g_row):
    return x * lax.rsqrt(jnp.mean(x * x, axis=-1, keepdims=True) + EPS) * g_row


def _dot(a, b):
    return jnp.dot(a, b, preferred_element_type=F32)


def _dot_nt(a, b):
    return lax.dot_general(a, b, (((1,), (1,)), ((), ())), preferred_element_type=F32)


def _sigmoid(x):
    return 1.0 / (1.0 + jnp.exp(-x))


def _silu(x):
    return x * _sigmoid(x)


def _softplus(x):
    return jnp.maximum(x, 0.0) + jnp.log1p(jnp.exp(-jnp.abs(x)))


def _log_sigmoid(x):
    return jnp.minimum(x, 0.0) - jnp.log1p(jnp.exp(-jnp.abs(x)))


def _norm_matmul_kernel(x_ref, g_ref, w_ref, *out_refs, splits):
    xn = _rms(x_ref[...], g_ref[...]).astype(BF16)
    off = 0
    for o_ref, n in zip(out_refs, splits):
        o_ref[...] = _dot(xn, w_ref[:, off:off + n])
        off += n


def _layer_weight(w_all, layer):
    return pl.BlockSpec((None,) + w_all.shape[1:], lambda *_: (layer, 0, 0), pipeline_mode=pl.Buffered(1))


def norm_matmul(x, g, w_all, layer, splits, tm):
    T, D = x.shape
    N = w_all.shape[2]
    assert sum(splits) == N and T % tm == 0
    return pl.pallas_call(
        functools.partial(_norm_matmul_kernel, splits=tuple(splits)),
        grid=(T // tm,),
        in_specs=[pl.BlockSpec((tm, D), lambda i: (i, 0)),
                  pl.BlockSpec((1, D), lambda i: (0, 0)),
                  _layer_weight(w_all, layer)],
        out_specs=[pl.BlockSpec((tm, n), lambda i: (i, 0)) for n in splits],
        out_shape=[jax.ShapeDtypeStruct((T, n), F32) for n in splits],
        compiler_params=_cparams("parallel"),
    )(x, g.reshape(1, D), w_all)


def _matmul_residual_kernel(x_ref, *refs, ksplits):
    a_refs, w_ref, o_ref = refs[:-2], refs[-2], refs[-1]
    acc = x_ref[...]
    off = 0
    for a_ref, k in zip(a_refs, ksplits):
        acc = acc + _dot(a_ref[...].astype(BF16), w_ref[off:off + k, :])
        off += k
    o_ref[...] = acc


def matmul_residual(x, acts, w_all, layer, tm):
    T, D = x.shape
    ks = tuple(a.shape[1] for a in acts)
    K = w_all.shape[1]
    assert sum(ks) == K and T % tm == 0
    return pl.pallas_call(
        functools.partial(_matmul_residual_kernel, ksplits=ks),
        grid=(T // tm,),
        in_specs=[pl.BlockSpec((tm, D), lambda i: (i, 0))]
                 + [pl.BlockSpec((tm, k), lambda i: (i, 0)) for k in ks]
                 + [_layer_weight(w_all, layer)],
        out_specs=pl.BlockSpec((tm, D), lambda i: (i, 0)),
        out_shape=jax.ShapeDtypeStruct((T, D), F32),
        compiler_params=_cparams("parallel"),
    )(x, *acts, w_all)


def _ffn_kernel(x_ref, g_ref, wg_ref, wu_ref, wd_ref, gf_ref, o_ref, *, final_norm, tf):
    x = x_ref[...]
    h = _rms(x, g_ref[...]).astype(BF16)
    y = x
    for j in range(wg_ref.shape[1] // tf):
        cols = slice(j * tf, (j + 1) * tf)
        a = _dot(h, wg_ref[:, cols])
        b = _dot(h, wu_ref[:, cols])
        y = y + _dot((_silu(a) * b).astype(BF16), wd_ref[cols, :])
    if final_norm:
        y = _rms(y, gf_ref[...])
    o_ref[...] = y


def _resident(shape):
    return pl.BlockSpec(shape, lambda *_: (0,) * len(shape), pipeline_mode=pl.Buffered(1))


def ffn(x, g, wg, wu, wd, layer, g_final, final_norm, tm, tf):
    T, D = x.shape
    F = wg.shape[2]
    assert T % tm == 0 and F % tf == 0
    return pl.pallas_call(
        functools.partial(_ffn_kernel, final_norm=final_norm, tf=tf),
        grid=(T // tm,),
        in_specs=[pl.BlockSpec((tm, D), lambda i: (i, 0)),
                  _resident((1, D)), _layer_weight(wg, layer), _layer_weight(wu, layer), _layer_weight(wd, layer),
                  _resident((1, D))],
        out_specs=pl.BlockSpec((tm, D), lambda i: (i, 0)),
        out_shape=jax.ShapeDtypeStruct((T, D), F32),
        compiler_params=_cparams("parallel"),
    )(x, g.reshape(1, D), wg, wu, wd, g_final.reshape(1, D))


def _softmax_rows(s):
    e = jnp.exp(s - jnp.max(s, axis=-1, keepdims=True))
    return e / jnp.sum(e, axis=-1, keepdims=True)


def _xattn_prompt_kernel(x_ref, *refs, dh, n_acts):
    a_refs = refs[:n_acts]
    wmix_ref, g_ref, wq_ref, k_ref, v_ref, wo_ref, o_ref = refs[n_acts:]
    scale = dh ** -0.5
    x = x_ref[...]
    off = 0
    for a_ref in a_refs:
        x = x + _dot(a_ref[...].astype(BF16), wmix_ref[off:off + a_ref.shape[1], :])
        off += a_ref.shape[1]
    q = _dot(_rms(x, g_ref[...]).astype(BF16), wq_ref[...]).astype(BF16)
    heads = []
    for h in range(X_HEADS):
        sl = slice(h * dh, (h + 1) * dh)
        s = _dot_nt(q[:, sl], k_ref[0, :, sl].astype(BF16)) * scale
        heads.append(_dot(_softmax_rows(s).astype(BF16), v_ref[0, :, sl].astype(BF16)).astype(BF16))
    o_ref[...] = x + _dot(jnp.concatenate(heads, axis=-1), wo_ref[...])


def xattn_prompt(x, acts, wmix, g, wq, mk, mv, wo, layer, seq, tq):
    T, D = x.shape
    _, B, M, _ = mk.shape
    nt = seq // tq
    rows = lambda w: pl.BlockSpec((tq, w), lambda b, t: (b * nt + t, 0))
    return pl.pallas_call(
        functools.partial(_xattn_prompt_kernel, dh=D // X_HEADS, n_acts=len(acts)),
        grid=(B, nt),
        in_specs=[rows(D)] + [rows(a.shape[1]) for a in acts]
                 + [_layer_weight(wmix, layer),
                    pl.BlockSpec((1, D), lambda b, t: (0, 0)),
                    _layer_weight(wq, layer),
                    pl.BlockSpec((None, 1, M, D), lambda b, t: (layer, b, 0, 0)),
                    pl.BlockSpec((None, 1, M, D), lambda b, t: (layer, b, 0, 0)),
                    _layer_weight(wo, layer)],
        out_specs=rows(D),
        out_shape=jax.ShapeDtypeStruct((T, D), F32),
        compiler_params=_cparams("parallel", "parallel"),
    )(x, *acts, wmix, g.reshape(1, D), wq, mk, mv, wo)


def _xattn_sample_kernel(q_ref, k_ref, v_ref, o_ref, *, sb):
    M, H, dh = k_ref.shape[1:]
    scale = dh ** -0.5
    row = lax.broadcasted_iota(jnp.int32, (SUBLANES, M * H), 0)
    col_head = lax.broadcasted_iota(jnp.int32, (SUBLANES, M * H), 1) % H
    own = (row % H) == col_head
    pad = jnp.zeros((SUBLANES - H, dh), F32)
    for i in range(sb):
        q8 = jnp.concatenate([q_ref[i], pad], axis=0).astype(BF16)
        s = _dot_nt(q8, k_ref[i].reshape(M * H, dh).astype(BF16)) * scale
        p = _softmax_rows(jnp.where(own, s, -jnp.inf))
        o_ref[i] = _dot(p.astype(BF16), v_ref[i].reshape(M * H, dh).astype(BF16))[0:H]


def xattn_sample(q, ck, cv, layer, sb):
    B, D = q.shape
    _, _, M, H, dh = ck.shape
    out = pl.pallas_call(
        functools.partial(_xattn_sample_kernel, sb=sb),
        grid=(B // sb,),
        in_specs=[pl.BlockSpec((sb, H, dh), lambda i: (i, 0, 0)),
                  pl.BlockSpec((None, sb, M, H, dh), lambda i: (layer, i, 0, 0, 0)),
                  pl.BlockSpec((None, sb, M, H, dh), lambda i: (layer, i, 0, 0, 0))],
        out_specs=pl.BlockSpec((sb, H, dh), lambda i: (i, 0, 0)),
        out_shape=jax.ShapeDtypeStruct((B, H, dh), F32),
        compiler_params=_cparams("parallel"),
    )(q.reshape(B, H, dh), ck, cv)
    return out.reshape(B, D)


def _lane_cat_masks(L):
    row = lax.broadcasted_iota(jnp.int32, (L, HW), 0)
    j = lax.broadcasted_iota(jnp.int32, (L, HW), 1) % DH
    r2 = lax.broadcasted_iota(jnp.int32, (HW, HW), 0) // DH
    c2 = lax.broadcasted_iota(jnp.int32, (HW, HW), 1) // DH
    return row >= j, row > j, row == j, r2 == c2


def _expand_bd(x, bd):
    return jnp.where(bd, jnp.concatenate([x] * HEADS, axis=0), jnp.zeros((), x.dtype))


def _seg_reduce(x, op, fill):
    lo = lax.broadcasted_iota(jnp.int32, (x.shape[0], LANES), 1) < DH
    parts = []
    for c in range(HW // LANES):
        xh = x[:, c * LANES:(c + 1) * LANES]
        a = op(jnp.where(lo, xh, fill), axis=-1, keepdims=True)
        b = op(jnp.where(lo, fill, xh), axis=-1, keepdims=True)
        parts.append(jnp.where(lo, a, b))
    return jnp.concatenate(parts, axis=-1)


def _head_expander(first_lane):
    r = lax.broadcasted_iota(jnp.int32, (GATE_PAD, HW), 0)
    c = lax.broadcasted_iota(jnp.int32, (GATE_PAD, HW), 1) // DH
    return (r == c + first_lane).astype(BF16)


def _chunk_tril(tt):
    r = lax.broadcasted_iota(jnp.int32, (tt, tt), 0)
    c = lax.broadcasted_iota(jnp.int32, (tt, tt), 1)
    return ((r // CHUNK == c // CHUNK) & (r >= c)).astype(BF16)


def _split3(x):
    hi = x.astype(BF16)
    r = x - hi.astype(F32)
    mid = r.astype(BF16)
    return hi, mid, (r - mid.astype(F32)).astype(BF16)


def _dot_sel_r(x, sel):
    hi, mid, lo = _split3(x)
    return (_dot(lo, sel) + _dot(mid, sel)) + _dot(hi, sel)


def _dot_sel_l(sel, x):
    hi, mid, lo = _split3(x)
    return (_dot(sel, lo) + _dot(sel, mid)) + _dot(sel, hi)


def _col_to_row(x, eye):
    return jnp.sum(jnp.where(eye, x, 0.0), axis=0, keepdims=True)


def _head_rms(x, ones_bd, g_row):
    ms = _dot_sel_r(x * x, ones_bd) * (1.0 / DH)
    return x * lax.rsqrt(ms + EPS) * g_row


def _mlstm_prompt_kernel(za_ref, gt_ref, bias_ref, norm_ref, ha_ref, c_ref, n_ref, m_ref,
                         c_scr, n_scr, m_scr, ix_scr, bx_scr, *, nchunks, group):
    tb = pl.program_id(1)
    L = CHUNK
    tril, _, eye, bd = _lane_cat_masks(L)
    ones_bd = bd.astype(BF16)

    @pl.when(tb == 0)
    def _():
        c_scr[...] = jnp.zeros_like(c_scr)
        n_scr[...] = jnp.zeros_like(n_scr)
        m_scr[...] = jnp.zeros_like(m_scr)

    gt = gt_ref[...] + bias_ref[...]
    b_cols = _dot_sel_l(_chunk_tril(nchunks * L), _log_sigmoid(gt))
    ix_scr[...] = _dot_sel_r(gt, _head_expander(0))
    bx_scr[...] = _dot_sel_r(b_cols, _head_expander(HEADS))

    def chunks(gi, carry):
        c_bd, n_row, m_x = carry
        rows = [pl.ds(pl.multiple_of((gi * group + j) * L, L), L) for j in range(group)]
        q = [(za_ref[r, 0:HW] * (DH ** -0.5)).astype(BF16) for r in rows]
        k = [za_ref[r, HW:2 * HW] for r in rows]
        v = [za_ref[r, 2 * HW:3 * HW].astype(BF16) for r in rows]
        i_x = [ix_scr[r, :] for r in rows]
        b_x = [bx_scr[r, :] for r in rows]

        m_in, n_in, kw, decay = [], [], [], []
        for kk, ii, bb in zip(k, i_x, b_x):
            b_last = bb[L - 1:L, :]
            g_x = b_last - bb + ii
            m_new = jnp.maximum(b_last + m_x, jnp.max(g_x, axis=0, keepdims=True))
            kw.append(kk * jnp.exp(g_x - m_new))
            decay.append(jnp.exp(b_last + m_x - m_new))
            m_in.append(m_x)
            n_in.append(n_row)
            n_row = decay[-1] * n_row + jnp.sum(kw[-1], axis=0, keepdims=True)
            m_x = m_new

        d_intra = [jnp.where(tril, bb - _col_to_row(bb, eye) + _col_to_row(ii, eye), -jnp.inf) for bb, ii in zip(b_x, i_x)]
        d_inter = [bb + mm for bb, mm in zip(b_x, m_in)]
        m_t = [jnp.maximum(_seg_reduce(di, jnp.max, -jnp.inf), de) for di, de in zip(d_intra, d_inter)]
        w_inter = [jnp.exp(de - mt) for de, mt in zip(d_inter, m_t)]
        s = [_dot_nt(qq, _expand_bd(kk.astype(BF16), bd)) * jnp.exp(di - mt) for qq, kk, di, mt in zip(q, k, d_intra, m_t)]
        kv = [jnp.where(bd, _dot(kwj.T.astype(BF16), vv), 0.0) for kwj, vv in zip(kw, v)]
        c_in = []
        for dj, kvj in zip(decay, kv):
            c_in.append(c_bd)
            c_bd = dj * c_bd + kvj
        num = [wi * _dot(qq, cc.astype(BF16)) + _dot(ss.astype(BF16), _expand_bd(vv, bd))
               for wi, qq, cc, ss, vv in zip(w_inter, q, c_in, s, v)]
        den = [wi * _dot((qq.astype(F32) * nn).astype(BF16), ones_bd) + _seg_reduce(ss, jnp.sum, 0.0)
               for wi, qq, nn, ss in zip(w_inter, q, n_in, s)]
        for r, nu, de, mt in zip(rows, num, den, m_t):
            hh = nu / jnp.maximum(jnp.abs(de), jnp.exp(-mt))
            ha_ref[r, :] = _head_rms(hh, ones_bd, norm_ref[...]) * _sigmoid(za_ref[r, 3 * HW:4 * HW])
        return c_bd, n_row, m_x

    c_bd, n_row, m_x = lax.fori_loop(0, nchunks // group, chunks, (c_scr[...], n_scr[...], m_scr[...]))
    c_scr[...] = c_bd
    n_scr[...] = n_row
    m_scr[...] = m_x

    @pl.when(tb == pl.num_programs(1) - 1)
    def _():
        for h in range(HEADS):
            c_ref[0, h] = c_scr[h * DH:(h + 1) * DH, h * DH:(h + 1) * DH]
        n_ref[0] = n_scr[...]
        m_ref[0] = m_scr[...]


def mlstm_prompt(za, gates, b_i, b_f, norm, B, S, tt):
    assert S % tt == 0 and tt % CHUNK == 0
    nt = S // tt
    bias = jnp.zeros((1, GATE_PAD), F32).at[0, 0:HEADS].set(b_i).at[0, HEADS:2 * HEADS].set(b_f)
    ha, c, n, m = pl.pallas_call(
        functools.partial(_mlstm_prompt_kernel, nchunks=tt // CHUNK, group=math.gcd(tt // CHUNK, 4)),
        grid=(B, nt),
        in_specs=[pl.BlockSpec((tt, 4 * HW), lambda b, t: (b * nt + t, 0)),
                  pl.BlockSpec((tt, GATE_PAD), lambda b, t: (b * nt + t, 0)),
                  pl.BlockSpec((1, GATE_PAD), lambda b, t: (0, 0)),
                  pl.BlockSpec((1, HW), lambda b, t: (0, 0))],
        out_specs=[pl.BlockSpec((tt, HW), lambda b, t: (b * nt + t, 0)),
                   pl.BlockSpec((1, HEADS, DH, DH), lambda b, t: (b, 0, 0, 0)),
                   pl.BlockSpec((1, 1, HW), lambda b, t: (b, 0, 0)),
                   pl.BlockSpec((1, 1, HW), lambda b, t: (b, 0, 0))],
        out_shape=[jax.ShapeDtypeStruct((B * S, HW), F32),
                   jax.ShapeDtypeStruct((B, HEADS, DH, DH), F32),
                   jax.ShapeDtypeStruct((B, 1, HW), F32),
                   jax.ShapeDtypeStruct((B, 1, HW), F32)],
        scratch_shapes=[pltpu.VMEM((HW, HW), F32), pltpu.VMEM((1, HW), F32), pltpu.VMEM((1, HW), F32),
                        pltpu.VMEM((tt, HW), F32), pltpu.VMEM((tt, HW), F32)],
        compiler_params=_cparams("parallel", "arbitrary"),
    )(za, gates, bias, norm.reshape(1, HW))
    return ha, c, n.reshape(B, HEADS, DH), m[:, 0, ::DH]


def _gdn_prompt_kernel(zc_ref, zg_ref, gt_ref, cw_ref, alog_ref, dtb_ref, norm_ref,
                       hc_ref, s_ref, tail_ref,
                       s_scr, xp_scr, q_scr, k_scr, v_scr, beta_scr, g_scr, uv_scr, wq_scr, qkm_scr, kwt_scr,
                       *, nseq, nchunks, group):
    tb = pl.program_id(1)
    L = CHUNK
    tt = nchunks * L
    pad = SUBLANES
    tril, strict, eye, bd = _lane_cat_masks(L)
    ones_bd = bd.astype(BF16)

    @pl.when(tb == 0)
    def _():
        s_scr[...] = jnp.zeros_like(s_scr)
        xp_scr[:, 0:pad, :] = jnp.zeros((nseq, pad, 3 * HW), F32)

    for i in range(nseq):
        sr = slice(i * tt, (i + 1) * tt)
        x = zc_ref[i]
        xp_scr[i, pad:pad + tt, :] = x
        y = cw_ref[CONV_K - 1:CONV_K, :] * x
        for j in range(CONV_K - 1):
            y = y + cw_ref[j:j + 1, :] * xp_scr[i, pl.ds(pad - (CONV_K - 1) + j, tt), :]
        xp_scr[i, 0:pad, :] = x[tt - pad:tt, :]
        y = _silu(y)
        q_raw, k_raw = y[:, 0:HW], y[:, HW:2 * HW]
        q_scr[sr, :] = (q_raw * lax.rsqrt(_dot_sel_r(q_raw * q_raw, ones_bd) + EPS) * (DH ** -0.5)).astype(BF16)
        k_scr[sr, :] = k_raw * lax.rsqrt(_dot_sel_r(k_raw * k_raw, ones_bd) + EPS)
        v_scr[sr, :] = y[:, 2 * HW:3 * HW]
        gt = gt_ref[i]
        beta_scr[sr, :] = _dot_sel_r(_sigmoid(gt), _head_expander(2 * HEADS))
        la_cols = -jnp.exp(alog_ref[...]) * _softplus(gt + dtb_ref[...])
        g_scr[sr, :] = _dot_sel_r(_dot_sel_l(_chunk_tril(tt), la_cols), _head_expander(3 * HEADS))

    def prepare(gi, carry):
        cis = [gi * group + j for j in range(group)]
        rows = [pl.ds(pl.multiple_of(ci * L, L), L) for ci in cis]
        k = [k_scr[r, :] for r in rows]
        g_x = [g_scr[r, :] for r in rows]
        beta_row = [_col_to_row(beta_scr[r, :], eye) for r in rows]
        dec_incl = [jnp.where(tril, jnp.exp(jnp.where(tril, g - _col_to_row(g, eye), 0.0)), 0.0) for g in g_x]
        k_bd = [_expand_bd(kk.astype(BF16), bd) for kk in k]
        n0 = [-(jnp.where(strict, d, 0.0) * _dot_nt(kk.astype(BF16), kbd) * br)
              for d, kk, kbd, br in zip(dec_incl, k, k_bd, beta_row)]
        for r, d, kbd, br in zip(rows, dec_incl, k_bd, beta_row):
            qkm_scr[r, :] = (_dot_nt(q_scr[r, :], kbd) * d * br).astype(BF16)

        p = [_dot(n.astype(BF16), _expand_bd(n.astype(BF16), bd)) for n in n0]
        m = n0
        steps = int(math.log2(L)) - 1
        for i in range(steps):
            p_bd = [_expand_bd(pp.astype(BF16), bd) for pp in p]
            if i < steps - 1:
                pm = [_dot(jnp.concatenate([pp, mm], axis=0).astype(BF16), pbd) for pp, mm, pbd in zip(p, m, p_bd)]
                p_next, mp = [x[0:L] for x in pm], [x[L:2 * L] for x in pm]
            else:
                p_next, mp = None, [_dot(mm.astype(BF16), pbd) for mm, pbd in zip(m, p_bd)]
            m = [mm + pp + x for mm, pp, x in zip(m, p, mp)]
            p = p_next

        for ci, r, kk, g, mm in zip(cis, rows, k, g_x, m):
            v = v_scr[r, :]
            egk = jnp.exp(g) * kk
            rhs_bd = jnp.concatenate([_expand_bd(v.astype(BF16), bd), _expand_bd(egk.astype(BF16), bd)], axis=1)
            mr = _dot(mm.astype(BF16), rhs_bd)
            uv_scr[r, :] = v + mr[:, 0:HW]
            wq_rows = pl.multiple_of(ci * 2 * L, 2 * L)
            wq_scr[pl.ds(wq_rows, L), :] = (egk + mr[:, HW:2 * HW]).astype(BF16)
            wq_scr[pl.ds(wq_rows + L, L), :] = q_scr[r, :]
            kw = kk * (jnp.exp(g[L - 1:L, :] - g) * beta_scr[r, :])
            kwt_scr[pl.ds(pl.multiple_of(ci * HW, HW), HW), :] = kw.T.astype(BF16)
        return carry

    lax.fori_loop(0, nseq * nchunks // group, prepare, 0)

    def advance(c, states):
        cis = [i * nchunks + c for i in range(nseq)]
        rows = [pl.ds(pl.multiple_of(ci * L, L), L) for ci in cis]
        wqs = [_dot(wq_scr[pl.ds(pl.multiple_of(ci * 2 * L, 2 * L), 2 * L), :], s.astype(BF16))
               for ci, s in zip(cis, states)]
        ub = [(uv_scr[r, :] - x[0:L]).astype(BF16) for r, x in zip(rows, wqs)]
        new = [jnp.exp(g_scr[pl.ds(pl.multiple_of(ci * L + L - SUBLANES, SUBLANES), SUBLANES), :][SUBLANES - 1:, :]) * s
               + jnp.where(bd, _dot(kwt_scr[pl.ds(pl.multiple_of(ci * HW, HW), HW), :], u), 0.0)
               for ci, s, u in zip(cis, states, ub)]
        for i, (r, x, u) in enumerate(zip(rows, wqs, ub)):
            o = jnp.exp(g_scr[r, :]) * x[L:2 * L] + _dot(qkm_scr[r, :], _expand_bd(u, bd))
            hc_ref[i, pl.ds(pl.multiple_of(c * L, L), L), :] = (
                _head_rms(o, ones_bd, norm_ref[...]) * _silu(zg_ref[i, pl.ds(pl.multiple_of(c * L, L), L), :]))
        return tuple(new)

    states = lax.fori_loop(0, nchunks, advance, tuple(s_scr[i] for i in range(nseq)))
    for i in range(nseq):
        s_scr[i] = states[i]

    @pl.when(tb == pl.num_programs(1) - 1)
    def _():
        for i in range(nseq):
            for h in range(HEADS):
                s_ref[i, h] = s_scr[i, h * DH:(h + 1) * DH, h * DH:(h + 1) * DH]
            tail_ref[i] = xp_scr[i, 0:pad, :]


def _gdn_gate_rows(a_log, dt_bias):
    z = jnp.zeros((1, GATE_PAD), F32)
    return (z.at[0, 3 * HEADS:4 * HEADS].set(a_log), z.at[0, 3 * HEADS:4 * HEADS].set(dt_bias))


def gdn_prompt(zc, zg, gates, conv_w, a_log, dt_bias, norm, B, S, tt):
    assert S % tt == 0 and tt % CHUNK == 0
    nt = S // tt
    nseq = math.gcd(B, GDN_SEQS)
    rows = nseq * tt
    alog, dtb = _gdn_gate_rows(a_log, dt_bias)
    blk = lambda w: pl.BlockSpec((nseq, tt, w), lambda b, t: (b, t, 0))
    row = lambda w: pl.BlockSpec((1, w), lambda b, t: (0, 0))
    hc, s, tail = pl.pallas_call(
        functools.partial(_gdn_prompt_kernel, nseq=nseq, nchunks=tt // CHUNK, group=math.gcd(rows // CHUNK, 8)),
        grid=(B // nseq, nt),
        in_specs=[blk(3 * HW), blk(HW), blk(GATE_PAD),
                  pl.BlockSpec((CONV_K, 3 * HW), lambda b, t: (0, 0)), row(GATE_PAD), row(GATE_PAD), row(HW)],
        out_specs=[blk(HW),
                   pl.BlockSpec((nseq, HEADS, DH, DH), lambda b, t: (b, 0, 0, 0)),
                   pl.BlockSpec((nseq, SUBLANES, 3 * HW), lambda b, t: (b, 0, 0))],
        out_shape=[jax.ShapeDtypeStruct((B, S, HW), F32),
                   jax.ShapeDtypeStruct((B, HEADS, DH, DH), F32),
                   jax.ShapeDtypeStruct((B, SUBLANES, 3 * HW), F32)],
        scratch_shapes=[pltpu.VMEM((nseq, HW, HW), F32), pltpu.VMEM((nseq, tt + SUBLANES, 3 * HW), F32),
                        pltpu.VMEM((rows, HW), BF16), pltpu.VMEM((rows, HW), F32), pltpu.VMEM((rows, HW), F32),
                        pltpu.VMEM((rows, HW), F32), pltpu.VMEM((rows, HW), F32), pltpu.VMEM((rows, HW), F32),
                        pltpu.VMEM((2 * rows, HW), BF16), pltpu.VMEM((rows, HW), BF16),
                        pltpu.VMEM((rows // CHUNK * HW, CHUNK), BF16)],
        compiler_params=_cparams("parallel", "arbitrary"),
    )(zc.reshape(B, S, 3 * HW), zg.reshape(B, S, HW), gates.reshape(B, S, GATE_PAD), conv_w, alog, dtb,
      jnp.tile(norm, HEADS).reshape(1, HW))
    return hc.reshape(B * S, HW), s, tail[:, SUBLANES - (CONV_K - 1):, :]


def _s5_prep_kernel(are_ref, aim_ref, ldt_ref, bre_ref, bim_ref, lre_ref, lim_ref, bbre_ref, bbim_ref):
    a_re, a_im = are_ref[...], aim_ref[...]
    dt = jnp.exp(ldt_ref[...])
    mag = jnp.exp(a_re * dt)
    lam_re, lam_im = mag * jnp.cos(a_im * dt), mag * jnp.sin(a_im * dt)
    lre_ref[...] = lam_re
    lim_ref[...] = lam_im
    nr, ni = lam_re - 1.0, lam_im
    den = a_re * a_re + a_im * a_im
    coef_re = (nr * a_re + ni * a_im) / den
    coef_im = (ni * a_re - nr * a_im) / den
    b_re, b_im = bre_ref[...], bim_ref[...]
    bbre_ref[...] = coef_re * b_re - coef_im * b_im
    bbim_ref[...] = coef_re * b_im + coef_im * b_re


def s5_params(lp):
    G, N, P = S5_GROUPS, S5_N, S5_P
    row = lambda a: a.astype(F32).reshape(1, G * N)
    to_pn = lambda b: jnp.transpose(b.astype(F32), (2, 0, 1)).reshape(P, G * N)
    shp = [jax.ShapeDtypeStruct((1, G * N), F32)] * 2 + [jax.ShapeDtypeStruct((P, G * N), F32)] * 2
    lam_re, lam_im, bb_re, bb_im = pl.pallas_call(_s5_prep_kernel, out_shape=shp)(
        row(lp['s5_a_re']), row(lp['s5_a_im']), row(jnp.repeat(lp['s5_log_dt'][:, None], N, axis=1)),
        to_pn(lp['s5_b_re']), to_pn(lp['s5_b_im']))
    eye = jnp.eye(S5_GB, dtype=F32)

    def w_in_blocks(bb):
        b4 = bb.reshape(P, S5_NBLK, S5_GB, N)
        return jnp.einsum('pbgn,gh->bgphn', b4, eye).reshape(S5_NBLK, S5_GB * P, S5_GB * N)

    def w_out_blocks(c):
        c4 = c.astype(F32).reshape(S5_NBLK, S5_GB, P, N)
        return jnp.einsum('bgpn,gh->bgnhp', c4, eye).reshape(S5_NBLK, S5_GB * N, S5_GB * P)

    w_in = jnp.concatenate([w_in_blocks(bb_re), w_in_blocks(bb_im)], axis=-1).astype(BF16)
    return {'lam_re': lam_re, 'lam_im': lam_im, 'w_in': w_in,
            'w_out_re': w_out_blocks(lp['s5_c_re']).astype(BF16),
            'w_out_im': (-w_out_blocks(lp['s5_c_im'])).astype(BF16),
            'd': lp['s5_d'].astype(F32).reshape(1, S5_WIDTH), 'w_glu': lp['s5_w_glu'].astype(BF16)}


def _s5_kernel(u_ref, h0r_ref, h0i_ref, lamr_ref, lami_ref, win_ref, wor_ref, woi_ref, d_ref, wglu_ref,
               ys_ref, h1r_ref, h1i_ref, hr_scr, hi_scr, br_scr, bi_scr, *, nseq, rows, bb, lane_blk):
    tb = pl.program_id(0)
    nsteps = nseq * rows // bb
    blk_in, blk_st = S5_GB * S5_P, S5_GB * S5_N

    @pl.when(tb == 0)
    def _():
        hr_scr[...] = h0r_ref[...]
        hi_scr[...] = h0i_ref[...]

    if nseq > 1:
        u = jnp.swapaxes(u_ref[...], 0, 1).reshape(nseq * rows, S5_WIDTH)
    else:
        u = u_ref[0]
    ub = u.astype(BF16)
    for blk in range(S5_NBLK):
        bu = _dot(ub[:, blk * blk_in:(blk + 1) * blk_in], win_ref[blk])
        br_scr[:, blk * blk_st:(blk + 1) * blk_st] = bu[:, 0:blk_st]
        bi_scr[:, blk * blk_st:(blk + 1) * blk_st] = bu[:, blk_st:2 * blk_st]

    for lb in range(S5_STATE // lane_blk):
        ls = slice(lb * lane_blk, (lb + 1) * lane_blk)
        lr = jnp.broadcast_to(lamr_ref[:, ls], (bb, lane_blk))
        li = jnp.broadcast_to(lami_ref[:, ls], (bb, lane_blk))

        def step(t, carry):
            hr, hi = carry
            r = pl.ds(pl.multiple_of(t * bb, bb), bb)
            nr = lr * hr - li * hi + br_scr[r, ls]
            ni = lr * hi + li * hr + bi_scr[r, ls]
            br_scr[r, ls] = nr
            bi_scr[r, ls] = ni
            return nr, ni

        hr, hi = lax.fori_loop(0, nsteps, step, (hr_scr[:, ls], hi_scr[:, ls]), unroll=min(nsteps, 8))
        hr_scr[:, ls] = hr
        hi_scr[:, ls] = hi

    ys = []
    for blk in range(S5_NBLK):
        st = slice(blk * blk_st, (blk + 1) * blk_st)
        ys.append(_dot(br_scr[:, st].astype(BF16), wor_ref[blk]) + _dot(bi_scr[:, st].astype(BF16), woi_ref[blk]))
    gy = jax.nn.gelu(jnp.concatenate(ys, axis=-1) + d_ref[...] * u)
    out = gy * _sigmoid(_dot(gy.astype(BF16), wglu_ref[...]))
    if nseq > 1:
        ys_ref[...] = jnp.swapaxes(out.reshape(rows, nseq, S5_WIDTH), 0, 1)
    else:
        ys_ref[0] = out

    @pl.when(tb == pl.num_programs(0) - 1)
    def _():
        h1r_ref[...] = hr_scr[...]
        h1i_ref[...] = hi_scr[...]


def s5_mixer(u, h0_re, h0_im, sp, tt, single_step):
    B = h0_re.shape[0]
    nseq, S = (1, 1) if single_step else (B, u.shape[1])
    rows = B if single_step else tt
    assert S % tt == 0 and B % SUBLANES == 0
    assert u.shape == ((1, B, S5_WIDTH) if single_step else (B, S, S5_WIDTH))
    lane_blk = max(LANES, min(S5_STATE, (SUBLANES * SUBLANES * LANES) // B))
    full = lambda shape: pl.BlockSpec(shape, lambda t: (0,) * len(shape))
    return pl.pallas_call(
        functools.partial(_s5_kernel, nseq=nseq, rows=rows, bb=B, lane_blk=lane_blk),
        grid=(S // tt,),
        in_specs=[pl.BlockSpec((nseq, rows, S5_WIDTH), lambda t: (0, t, 0)),
                  full((B, S5_STATE)), full((B, S5_STATE)), full((1, S5_STATE)), full((1, S5_STATE)),
                  full(sp['w_in'].shape), full(sp['w_out_re'].shape), full(sp['w_out_im'].shape),
                  full((1, S5_WIDTH)), full((S5_WIDTH, S5_WIDTH))],
        out_specs=[pl.BlockSpec((nseq, rows, S5_WIDTH), lambda t: (0, t, 0)),
                   full((B, S5_STATE)), full((B, S5_STATE))],
        out_shape=[jax.ShapeDtypeStruct(u.shape, F32),
                   jax.ShapeDtypeStruct((B, S5_STATE), F32), jax.ShapeDtypeStruct((B, S5_STATE), F32)],
        scratch_shapes=[pltpu.VMEM((B, S5_STATE), F32), pltpu.VMEM((B, S5_STATE), F32),
                        pltpu.VMEM((nseq * rows, S5_STATE), F32), pltpu.VMEM((nseq * rows, S5_STATE), F32)],
        compiler_params=_cparams("arbitrary"),
    )(u, h0_re, h0_im, sp['lam_re'], sp['lam_im'], sp['w_in'], sp['w_out_re'], sp['w_out_im'], sp['d'], sp['w_glu'])


def _ones_bd():
    r = lax.broadcasted_iota(jnp.int32, (HW, HW), 0) // DH
    c = lax.broadcasted_iota(jnp.int32, (HW, HW), 1) // DH
    return (r == c).astype(BF16)


def _mlstm_sample_kernel(za_ref, gt_ref, bias_ref, norm_ref, c_ref, n_ref, m_ref,
                         ha_ref, c_out, n_out, m_out, q_scr, kw_scr, h_scr):
    za = za_ref[...]
    q_scr[...] = (za[:, 0:HW] * (DH ** -0.5)).T
    k_t = za[:, HW:2 * HW].T
    v_t = za[:, 2 * HW:3 * HW].T
    g_t = (gt_ref[...] + bias_ref[...]).T
    m_out[...] = jnp.zeros_like(m_out)
    for h in range(HEADS):
        hs = slice(h * DH, (h + 1) * DH)
        i_h = g_t[h:h + 1, :]
        bm = _log_sigmoid(g_t[HEADS + h:HEADS + h + 1, :]) + m_ref[h:h + 1, :]
        m_t = jnp.maximum(i_h, bm)
        w_in = jnp.exp(i_h - m_t)
        w_st = jnp.exp(bm - m_t)
        q_h, k_h, v_h = q_scr[hs, :], k_t[hs, :], v_t[hs, :]
        s = jnp.sum(q_h * k_h, axis=0, keepdims=True) * w_in
        kw_scr[hs, :] = k_h * w_in

        def body(d, acc):
            r = h * DH + d
            rows = pl.ds(pl.multiple_of(r * DH, DH), DH)
            c_hd = c_ref[rows, :]
            c_out[rows, :] = w_st * c_hd + kw_scr[pl.ds(r, 1), :] * v_h
            return acc + q_scr[pl.ds(r, 1), :] * c_hd

        qc = lax.fori_loop(0, DH, body, jnp.zeros((DH, za.shape[0]), F32), unroll=4)
        n_h = n_ref[hs, :]
        num = w_st * qc + s * v_h
        den = w_st * jnp.sum(q_h * n_h, axis=0, keepdims=True) + s
        h_scr[hs, :] = num / jnp.maximum(jnp.abs(den), jnp.exp(-m_t))
        n_out[hs, :] = w_st * n_h + kw_scr[hs, :]
        m_out[h:h + 1, :] = m_t
    ha_ref[...] = _head_rms(h_scr[...].T, _ones_bd(), norm_ref[...]) * _sigmoid(za[:, 3 * HW:4 * HW])


def mlstm_sample(za, gates, b_i, b_f, norm, c_t, n_t, m_t):
    B = za.shape[0]
    bias = jnp.zeros((1, GATE_PAD), F32).at[0, 0:HEADS].set(b_i).at[0, HEADS:2 * HEADS].set(b_f)
    shp = lambda *s: jax.ShapeDtypeStruct(s, F32)
    return pl.pallas_call(
        _mlstm_sample_kernel,
        out_shape=[shp(B, HW), shp(HW * DH, B), shp(HW, B), shp(SUBLANES, B)],
        scratch_shapes=[pltpu.VMEM((HW, B), F32), pltpu.VMEM((HW, B), F32), pltpu.VMEM((HW, B), F32)],
        compiler_params=pltpu.CompilerParams(vmem_limit_bytes=VMEM_LIMIT),
    )(za, gates, bias, norm.reshape(1, HW), c_t, n_t, m_t)


def _gdn_sample_kernel(zc_ref, zg_ref, gt_ref, buf_ref, cw_ref, alog_ref, dtb_ref, norm_ref, s_ref,
                       hc_ref, s_out, buf_out, q_scr, k_scr, o_scr):
    W3 = 3 * HW
    x = zc_ref[...]
    y = cw_ref[CONV_K - 1:CONV_K, :] * x
    for j in range(CONV_K - 1):
        y = y + cw_ref[j:j + 1, :] * buf_ref[:, j * W3:(j + 1) * W3]
    buf_out[:, 0:(CONV_K - 2) * W3] = buf_ref[:, W3:(CONV_K - 1) * W3]
    buf_out[:, (CONV_K - 2) * W3:(CONV_K - 1) * W3] = x
    y = _silu(y)
    ones_bd = _ones_bd()
    q_raw, k_raw = y[:, 0:HW], y[:, HW:2 * HW]
    q_scr[...] = (q_raw * lax.rsqrt(_dot_sel_r(q_raw * q_raw, ones_bd) + EPS) * (DH ** -0.5)).T
    k_scr[...] = (k_raw * lax.rsqrt(_dot_sel_r(k_raw * k_raw, ones_bd) + EPS)).T
    v_t = y[:, 2 * HW:3 * HW].T
    gt = gt_ref[...]
    beta_t = _sigmoid(gt).T
    la_t = (-jnp.exp(alog_ref[...]) * _softplus(gt + dtb_ref[...])).T
    nb = x.shape[0]
    for h in range(HEADS):
        hs = slice(h * DH, (h + 1) * DH)
        beta = beta_t[2 * HEADS + h:2 * HEADS + h + 1, :]
        eg = jnp.exp(la_t[3 * HEADS + h:3 * HEADS + h + 1, :])
        q_h, k_h, v_h = q_scr[hs, :], k_scr[hs, :], v_t[hs, :]

        def read(d, acc):
            ks, qs = acc
            r = h * DH + d
            s_hd = s_ref[pl.ds(pl.multiple_of(r * DH, DH), DH), :]
            return ks + k_scr[pl.ds(r, 1), :] * s_hd, qs + q_scr[pl.ds(r, 1), :] * s_hd

        zero = jnp.zeros((DH, nb), F32)
        ks, qs = lax.fori_loop(0, DH, read, (zero, zero), unroll=4)
        u = v_h - eg * ks
        o_scr[hs, :] = eg * qs + (jnp.sum(q_h * k_h, axis=0, keepdims=True) * beta) * u

        def write(d, carry):
            r = h * DH + d
            rows = pl.ds(pl.multiple_of(r * DH, DH), DH)
            s_out[rows, :] = eg * s_ref[rows, :] + (beta * k_scr[pl.ds(r, 1), :]) * u
            return carry

        lax.fori_loop(0, DH, write, 0, unroll=4)
    hc_ref[...] = _head_rms(o_scr[...].T, ones_bd, norm_ref[...]) * _silu(zg_ref[...])


def gdn_sample(zc, zg, gates, buf, conv_w, a_log, dt_bias, norm, s_t):
    B = zc.shape[0]
    alog, dtb = _gdn_gate_rows(a_log, dt_bias)
    shp = lambda *s: jax.ShapeDtypeStruct(s, F32)
    return pl.pallas_call(
        _gdn_sample_kernel,
        out_shape=[shp(B, HW), shp(HW * DH, B), shp(B, (CONV_K - 1) * 3 * HW)],
        scratch_shapes=[pltpu.VMEM((HW, B), F32), pltpu.VMEM((HW, B), F32), pltpu.VMEM((HW, B), F32)],
        compiler_params=pltpu.CompilerParams(vmem_limit_bytes=VMEM_LIMIT),
    )(zc, zg, gates, buf, conv_w, alog, dtb, jnp.tile(norm, HEADS).reshape(1, HW), s_t)


FFN_CHUNK = 1408
IN_SEGMENTS = (4 * HW, S5_WIDTH, 3 * HW, HW, GATE_PAD)


def _tile(n, pref):
    return pref if n % pref == 0 else n


def prep_weights(p):
    w = p['w_in'].astype(BF16)
    a, g4 = 4 * HW, HEADS
    o_u = a + 2 * g4
    o_c = o_u + S5_WIDTH
    o_g = o_c + 3 * HW
    o_b = o_g + HW
    gate_cols = jnp.concatenate([w[:, :, a:a + 2 * g4], w[:, :, o_b:o_b + 2 * g4],
                                 jnp.zeros(w.shape[:2] + (GATE_PAD - 4 * g4,), w.dtype)], axis=2)
    w_in = jnp.concatenate([w[:, :, 0:a], w[:, :, o_u:o_c], w[:, :, o_c:o_g], w[:, :, o_g:o_b], gate_cols], axis=2)
    bf = lambda n: p[n].astype(BF16)
    return {'w_in': w_in, 'w_out': bf('w_out'), 'w_mq': bf('w_mq'), 'w_mo': bf('w_mo'), 'w_mk': bf('w_mk'),
            'w_mv': bf('w_mv'), 'w_gate': bf('w_gate'), 'w_up': bf('w_up'), 'w_down': bf('w_down')}


def mixer_prompt(x, lp, W, layer, B, S):
    T = B * S
    tm = _tile(T, 512)
    za, zu, zc, zg, gates = norm_matmul(x, lp['norm_mix'], W['w_in'], layer, IN_SEGMENTS, tm)
    tt = _tile(S, 512)
    ha, c1, n1, m1 = mlstm_prompt(za, gates, lp['mlstm_b_i'], lp['mlstm_b_f'], lp['mlstm_norm'], B, S, tt)
    h0 = jnp.zeros((B, S5_STATE), F32)
    ys3, r1, i1 = s5_mixer(zu.reshape(B, S, S5_WIDTH), h0, h0, lp['s5'], _tile(S, 128), False)
    ys = ys3.reshape(T, S5_WIDTH)
    hc, s1, buf1 = gdn_prompt(zc, zg, gates, lp['gdn_conv_w'], lp['gdn_a_log'], lp['gdn_dt_bias'], lp['gdn_norm'], B, S, _tile(S, 256))
    return [ha, ys, hc], (c1, n1, m1, r1.reshape(B, S5_GROUPS, S5_N), i1.reshape(B, S5_GROUPS, S5_N), s1, buf1)


def mixer_sample(x, st, lp, W, layer):
    B = x.shape[0]
    c0, n0, m0, r0, i0, s0, buf0 = st
    za, zu, zc, zg, gates = norm_matmul(x, lp['norm_mix'], W['w_in'], layer, IN_SEGMENTS, B)
    m_t = jnp.zeros((SUBLANES, B), F32).at[0:HEADS, :].set(m0.T)
    ha, c1t, n1t, m1t = mlstm_sample(za, gates, lp['mlstm_b_i'], lp['mlstm_b_f'], lp['mlstm_norm'],
                                     c0.reshape(B, HW * DH).T, n0.reshape(B, HW).T, m_t)
    ys3, r1, i1 = s5_mixer(zu.reshape(1, B, S5_WIDTH), r0.reshape(B, S5_STATE), i0.reshape(B, S5_STATE), lp['s5'], 1, True)
    hc, s1t, buf1 = gdn_sample(zc, zg, gates, buf0.reshape(B, (CONV_K - 1) * 3 * HW), lp['gdn_conv_w'],
                               lp['gdn_a_log'], lp['gdn_dt_bias'], lp['gdn_norm'], s0.reshape(B, HW * DH).T)
    x1 = matmul_residual(x, [ha, ys3.reshape(B, S5_WIDTH), hc], W['w_out'], layer, B)
    return x1, (c1t.T.reshape(B, HEADS, DH, DH), n1t.T.reshape(B, HEADS, DH), m1t[0:HEADS, :].T,
                r1.reshape(B, S5_GROUPS, S5_N), i1.reshape(B, S5_GROUPS, S5_N),
                s1t.T.reshape(B, HEADS, DH, DH), buf1.reshape(B, CONV_K - 1, 3 * HW))


def _mem_kv_kernel(x_ref, g_ref, wk_ref, wv_ref, k2_ref, v2_ref, k5_ref, v5_ref):
    xn = _rms(x_ref[...], g_ref[...]).astype(BF16)
    for w_ref, o2_ref, o5_ref in ((wk_ref, k2_ref, k5_ref), (wv_ref, v2_ref, v5_ref)):
        y = _dot(xn, w_ref[...])
        o2_ref[...] = y
        o5_ref[...] = y.reshape(o5_ref.shape)


def mem_kv(mem, norm_mem, wk, wv):
    B, M, D = mem.shape
    L = wk.shape[0]
    dh = D // X_HEADS
    w_spec = pl.BlockSpec((None, D, D), lambda l, b: (l, 0, 0))
    o2_spec = pl.BlockSpec((None, None, M, D), lambda l, b: (l, b, 0, 0))
    o5_spec = pl.BlockSpec((None, None, M, X_HEADS, dh), lambda l, b: (l, b, 0, 0, 0))
    return pl.pallas_call(
        _mem_kv_kernel,
        grid=(L, B),
        in_specs=[pl.BlockSpec((None, M, D), lambda l, b: (b, 0, 0)),
                  pl.BlockSpec((None, 1, D), lambda l, b: (l, 0, 0)), w_spec, w_spec],
        out_specs=[o2_spec, o2_spec, o5_spec, o5_spec],
        out_shape=[jax.ShapeDtypeStruct((L, B, M, D), F32)] * 2
                  + [jax.ShapeDtypeStruct((L, B, M, X_HEADS, dh), F32)] * 2,
        compiler_params=_cparams("parallel", "parallel"),
    )(mem, norm_mem.reshape(L, 1, D), wk, wv)


def xattn_ffn_prompt(x, acts, mk, mv, lp, W, layer, S, norm_final, final):
    T, D = x.shape
    x2 = xattn_prompt(x, acts, W['w_out'], lp['norm_xattn'], W['w_mq'], mk, mv, W['w_mo'], layer, S, _tile(S, 512))
    return ffn(x2, lp['norm_ffn'], W['w_gate'], W['w_up'], W['w_down'], layer, norm_final, final, _tile(T, 512),
               FFN_CHUNK)


def xattn_ffn_sample(x, ck, cv, lp, W, layer, norm_final, final):
    B, D = x.shape
    (q,) = norm_matmul(x, lp['norm_xattn'], W['w_mq'], layer, (D,), B)
    o = xattn_sample(q, ck, cv, layer, 4)
    x2 = matmul_residual(x, [o], W['w_mo'], layer, B)
    return ffn(x2, lp['norm_ffn'], W['w_gate'], W['w_up'], W['w_down'], layer, norm_final, final, B, FFN_CHUNK)


LAYER_PARAMS = ('norm_mix', 'w_in', 'w_out', 'mlstm_b_i', 'mlstm_b_f', 'mlstm_norm', 's5_a_re', 's5_a_im', 's5_log_dt',
                's5_b_re', 's5_b_im', 's5_c_re', 's5_c_im', 's5_d', 's5_w_glu', 'gdn_conv_w', 'gdn_a_log',
                'gdn_dt_bias', 'gdn_norm', 'norm_xattn', 'norm_mem', 'w_mq', 'w_mk', 'w_mv', 'w_mo', 'norm_ffn',
                'w_gate', 'w_up', 'w_down')


def kernel(x_prompt, x_sample, mem_prompt, cache_mem_k, cache_mem_v, state_mlstm_c, state_mlstm_n, state_mlstm_m, state_s5_re, state_s5_im, state_gdn, state_gdn_conv, norm_mix, w_in, w_out, mlstm_b_i, mlstm_b_f, mlstm_norm, s5_a_re, s5_a_im, s5_log_dt, s5_b_re, s5_b_im, s5_c_re, s5_c_im, s5_d, s5_w_glu, gdn_conv_w, gdn_a_log, gdn_dt_bias, gdn_norm, norm_xattn, norm_mem, w_mq, w_mk, w_mv, w_mo, norm_ffn, w_gate, w_up, w_down, norm_final):
    stacked = dict(norm_mix=norm_mix, w_in=w_in, w_out=w_out, mlstm_b_i=mlstm_b_i, mlstm_b_f=mlstm_b_f,
                   mlstm_norm=mlstm_norm, s5_a_re=s5_a_re, s5_a_im=s5_a_im, s5_log_dt=s5_log_dt, s5_b_re=s5_b_re,
                   s5_b_im=s5_b_im, s5_c_re=s5_c_re, s5_c_im=s5_c_im, s5_d=s5_d, s5_w_glu=s5_w_glu,
                   gdn_conv_w=gdn_conv_w, gdn_a_log=gdn_a_log, gdn_dt_bias=gdn_dt_bias, gdn_norm=gdn_norm,
                   norm_xattn=norm_xattn, norm_mem=norm_mem, w_mq=w_mq, w_mk=w_mk, w_mv=w_mv, w_mo=w_mo,
                   norm_ffn=norm_ffn, w_gate=w_gate, w_up=w_up, w_down=w_down)
    B, S, D = x_prompt.shape
    Bs = x_sample.shape[0]
    M = mem_prompt.shape[1]
    depth = w_in.shape[0]
    xp = x_prompt.reshape(B * S, D)
    xs = x_sample.reshape(Bs, D)
    cache_k, cache_v = cache_mem_k, cache_mem_v
    st_p, st_s = [], []
    W = prep_weights(stacked)
    mk, mv, mem_k, mem_v = mem_kv(mem_prompt, norm_mem, W['w_mk'], W['w_mv'])
    for l in range(depth):
        lp = {n: stacked[n][l] for n in LAYER_PARAMS if n not in W}
        lp['s5'] = s5_params(lp)
        last = l == depth - 1
        acts, sp = mixer_prompt(xp, lp, W, l, B, S)
        xp = xattn_ffn_prompt(xp, acts, mk, mv, lp, W, l, S, norm_final, last)
        st_in = (state_mlstm_c[l], state_mlstm_n[l], state_mlstm_m[l], state_s5_re[l], state_s5_im[l],
                 state_gdn[l], state_gdn_conv[l])
        xs, ss = mixer_sample(xs, st_in, lp, W, l)
        xs = xattn_ffn_sample(xs, cache_k, cache_v, lp, W, l, norm_final, last)
        st_p.append(sp)
        st_s.append(ss)
    stack = lambda lst: [jnp.stack([st[i] for st in lst]) for i in range(7)]
    return (xp.reshape(B, S, D), xs.reshape(Bs, 1, D), mem_k, mem_v,
            *stack(st_p), *stack(st_s))
```

```python
import functools
import math

import jax
import jax.numpy as jnp
from jax import lax
from jax.experimental import pallas as pl
from jax.experimental.pallas import tpu as pltpu

F32 = jnp.float32
BF16 = jnp.bfloat16
EPS = 1e-6

HEADS = 4
DH = 64
HW = HEADS * DH
CHUNK = 64
S5_P = 16
S5_N = 64
S5_GROUPS = 32
S5_WIDTH = S5_GROUPS * S5_P
S5_STATE = S5_GROUPS * S5_N
S5_GB = 8
S5_NBLK = S5_GROUPS // S5_GB
CONV_K = 4
GDN_SEQS = 4
X_HEADS = 4
GATE_PAD = 128
LANES = 128
SUBLANES = 8
VMEM_LIMIT = 48 * 1024 * 1024


def _cparams(*sem):
    return pltpu.CompilerParams(dimension_semantics=sem, vmem_limit_bytes=VMEM_LIMIT)


def _rms(x, g_row):
    return x * lax.rsqrt(jnp.mean(x * x, axis=-1, keepdims=True) + EPS) * g_row


def _dot(a, b):
    return jnp.dot(a, b, preferred_element_type=F32)


def _dot_nt(a, b):
    return lax.dot_general(a, b, (((1,), (1,)), ((), ())), preferred_element_type=F32)


def _sigmoid(x):
    return 1.0 / (1.0 + jnp.exp(-x))


def _silu(x):
    return x * _sigmoid(x)


def _softplus(x):
    return jnp.maximum(x, 0.0) + jnp.log1p(jnp.exp(-jnp.abs(x)))


def _log_sigmoid(x):
    return jnp.minimum(x, 0.0) - jnp.log1p(jnp.exp(-jnp.abs(x)))


def _norm_matmul_kernel(x_ref, g_ref, w_ref, *out_refs, splits):
    xn = _rms(x_ref[...], g_ref[...]).astype(BF16)
    off = 0
    for o_ref, n in zip(out_refs, splits):
        o_ref[...] = _dot(xn, w_ref[:, off:off + n])
        off += n


def _layer_weight(w_all, layer):
    return pl.BlockSpec((None,) + w_all.shape[1:], lambda *_: (layer, 0, 0), pipeline_mode=pl.Buffered(1))


def norm_matmul(x, g, w_all, layer, splits, tm):
    T, D = x.shape
    N = w_all.shape[2]
    assert sum(splits) == N and T % tm == 0
    return pl.pallas_call(
        functools.partial(_norm_matmul_kernel, splits=tuple(splits)),
        grid=(T // tm,),
        in_specs=[pl.BlockSpec((tm, D), lambda i: (i, 0)),
                  pl.BlockSpec((1, D), lambda i: (0, 0)),
                  _layer_weight(w_all, layer)],
        out_specs=[pl.BlockSpec((tm, n), lambda i: (i, 0)) for n in splits],
        out_shape=[jax.ShapeDtypeStruct((T, n), F32) for n in splits],
        compiler_params=_cparams("parallel"),
    )(x, g.reshape(1, D), w_all)


def _in_proj_prompt_kernel(x_ref, g_ref, w_ref, cw_ref, za_ref, zu_ref, zc_ref, zg_ref, gt_ref, tail_ref, xp_scr,
                           *, tiles_per_seq):
    tm = x_ref.shape[0]
    pad = SUBLANES
    xn = _rms(x_ref[...], g_ref[...]).astype(BF16)
    offs = [sum(IN_SEGMENTS[:s]) for s in range(len(IN_SEGMENTS))]

    @pl.when(pl.program_id(0) % tiles_per_seq == 0)
    def _():
        xp_scr[0:pad, :] = jnp.zeros((pad, xp_scr.shape[1]), F32)

    xp_scr[pad:pad + tm, :] = _dot(xn, w_ref[:, offs[2]:offs[2] + IN_SEGMENTS[2]])
    others = ((0, za_ref), (1, zu_ref), (3, zg_ref), (4, gt_ref))
    rows = tm // len(others)
    for r, (s, o_ref) in enumerate(others):
        y = cw_ref[CONV_K - 1:CONV_K, :] * xp_scr[pl.ds(pad + r * rows, rows), :]
        for j in range(CONV_K - 1):
            y = y + cw_ref[j:j + 1, :] * xp_scr[pl.ds(pad + r * rows - (CONV_K - 1) + j, rows), :]
        zc_ref[pl.ds(r * rows, rows), :] = _silu(y)
        o_ref[...] = _dot(xn, w_ref[:, offs[s]:offs[s] + IN_SEGMENTS[s]])
    tail = xp_scr[tm:tm + pad, :]
    xp_scr[0:pad, :] = tail
    tail_ref[0] = tail


def in_proj_prompt(x, g, w_all, layer, conv_w, seq, tm):
    T, D = x.shape
    assert seq % tm == 0 and T % seq == 0
    wc = IN_SEGMENTS[2]
    outs = pl.pallas_call(
        functools.partial(_in_proj_prompt_kernel, tiles_per_seq=seq // tm),
        grid=(T // tm,),
        in_specs=[pl.BlockSpec((tm, D), lambda i: (i, 0)),
                  pl.BlockSpec((1, D), lambda i: (0, 0)),
                  _layer_weight(w_all, layer),
                  pl.BlockSpec((CONV_K, wc), lambda i: (0, 0))],
        out_specs=[pl.BlockSpec((tm, n), lambda i: (i, 0)) for n in IN_SEGMENTS]
                  + [pl.BlockSpec((1, SUBLANES, wc), lambda i: (i // (seq // tm), 0, 0))],
        out_shape=[jax.ShapeDtypeStruct((T, n), F32) for n in IN_SEGMENTS]
                  + [jax.ShapeDtypeStruct((T // seq, SUBLANES, wc), F32)],
        scratch_shapes=[pltpu.VMEM((tm + SUBLANES, wc), F32)],
        compiler_params=_cparams("arbitrary"),
    )(x, g.reshape(1, D), w_all, conv_w)
    return outs


def _matmul_residual_kernel(x_ref, *refs, ksplits):
    a_refs, w_ref, o_ref = refs[:-2], refs[-2], refs[-1]
    acc = x_ref[...]
    off = 0
    for a_ref, k in zip(a_refs, ksplits):
        acc = acc + _dot(a_ref[...].astype(BF16), w_ref[off:off + k, :])
        off += k
    o_ref[...] = acc


def matmul_residual(x, acts, w_all, layer, tm):
    T, D = x.shape
    ks = tuple(a.shape[1] for a in acts)
    K = w_all.shape[1]
    assert sum(ks) == K and T % tm == 0
    return pl.pallas_call(
        functools.partial(_matmul_residual_kernel, ksplits=ks),
        grid=(T // tm,),
        in_specs=[pl.BlockSpec((tm, D), lambda i: (i, 0))]
                 + [pl.BlockSpec((tm, k), lambda i: (i, 0)) for k in ks]
                 + [_layer_weight(w_all, layer)],
        out_specs=pl.BlockSpec((tm, D), lambda i: (i, 0)),
        out_shape=jax.ShapeDtypeStruct((T, D), F32),
        compiler_params=_cparams("parallel"),
    )(x, *acts, w_all)


def _ffn_kernel(x_ref, g_ref, wg_ref, wu_ref, wd_ref, gf_ref, o_ref, *, final_norm, tf):
    x = x_ref[...]
    h = _rms(x, g_ref[...]).astype(BF16)
    y = x
    for j in range(wg_ref.shape[1] // tf):
        cols = slice(j * tf, (j + 1) * tf)
        a = _dot(h, wg_ref[:, cols])
        b = _dot(h, wu_ref[:, cols])
        y = y + _dot((_silu(a) * b).astype(BF16), wd_ref[cols, :])
    if final_norm:
        y = _rms(y, gf_ref[...])
    o_ref[...] = y


def _resident(shape):
    return pl.BlockSpec(shape, lambda *_: (0,) * len(shape), pipeline_mode=pl.Buffered(1))


def ffn(x, g, wg, wu, wd, layer, g_final, final_norm, tm, tf):
    T, D = x.shape
    F = wg.shape[2]
    assert T % tm == 0 and F % tf == 0
    return pl.pallas_call(
        functools.partial(_ffn_kernel, final_norm=final_norm, tf=tf),
        grid=(T // tm,),
        in_specs=[pl.BlockSpec((tm, D), lambda i: (i, 0)),
                  _resident((1, D)), _layer_weight(wg, layer), _layer_weight(wu, layer), _layer_weight(wd, layer),
                  _resident((1, D))],
        out_specs=pl.BlockSpec((tm, D), lambda i: (i, 0)),
        out_shape=jax.ShapeDtypeStruct((T, D), F32),
        compiler_params=_cparams("parallel"),
    )(x, g.reshape(1, D), wg, wu, wd, g_final.reshape(1, D))


def _softmax_rows(s):
    e = jnp.exp(s - jnp.max(s, axis=-1, keepdims=True))
    return e / jnp.sum(e, axis=-1, keepdims=True)


def _xattn_prompt_kernel(x_ref, *refs, dh, n_acts):
    a_refs = refs[:n_acts]
    wmix_ref, g_ref, wq_ref, k_ref, v_ref, wo_ref, o_ref = refs[n_acts:]
    scale = dh ** -0.5
    x = x_ref[...]
    off = 0
    for a_ref in a_refs:
        x = x + _dot(a_ref[...].astype(BF16), wmix_ref[off:off + a_ref.shape[1], :])
        off += a_ref.shape[1]
    q = _dot(_rms(x, g_ref[...]).astype(BF16), wq_ref[...]).astype(BF16)
    heads = []
    for h in range(X_HEADS):
        sl = slice(h * dh, (h + 1) * dh)
        s = _dot_nt(q[:, sl], k_ref[0, :, sl].astype(BF16)) * scale
        heads.append(_dot(_softmax_rows(s).astype(BF16), v_ref[0, :, sl].astype(BF16)).astype(BF16))
    o_ref[...] = x + _dot(jnp.concatenate(heads, axis=-1), wo_ref[...])


def xattn_prompt(x, acts, wmix, g, wq, mk, mv, wo, layer, seq, tq):
    T, D = x.shape
    _, B, M, _ = mk.shape
    nt = seq // tq
    rows = lambda w: pl.BlockSpec((tq, w), lambda b, t: (b * nt + t, 0))
    return pl.pallas_call(
        functools.partial(_xattn_prompt_kernel, dh=D // X_HEADS, n_acts=len(acts)),
        grid=(B, nt),
        in_specs=[rows(D)] + [rows(a.shape[1]) for a in acts]
                 + [_layer_weight(wmix, layer),
                    pl.BlockSpec((1, D), lambda b, t: (0, 0)),
                    _layer_weight(wq, layer),
                    pl.BlockSpec((None, 1, M, D), lambda b, t: (layer, b, 0, 0)),
                    pl.BlockSpec((None, 1, M, D), lambda b, t: (layer, b, 0, 0)),
                    _layer_weight(wo, layer)],
        out_specs=rows(D),
        out_shape=jax.ShapeDtypeStruct((T, D), F32),
        compiler_params=_cparams("parallel", "parallel"),
    )(x, *acts, wmix, g.reshape(1, D), wq, mk, mv, wo)


def _xattn_sample_kernel(q_ref, k_ref, v_ref, o_ref, *, sb):
    M, H, dh = k_ref.shape[1:]
    scale = dh ** -0.5
    row = lax.broadcasted_iota(jnp.int32, (SUBLANES, M * H), 0)
    col_head = lax.broadcasted_iota(jnp.int32, (SUBLANES, M * H), 1) % H
    own = (row % H) == col_head
    pad = jnp.zeros((SUBLANES - H, dh), F32)
    for i in range(sb):
        q8 = jnp.concatenate([q_ref[i], pad], axis=0).astype(BF16)
        s = _dot_nt(q8, k_ref[i].reshape(M * H, dh).astype(BF16)) * scale
        p = _softmax_rows(jnp.where(own, s, -jnp.inf))
        o_ref[i] = _dot(p.astype(BF16), v_ref[i].reshape(M * H, dh).astype(BF16))[0:H]


def xattn_sample(q, ck, cv, layer, sb):
    B, D = q.shape
    _, _, M, H, dh = ck.shape
    out = pl.pallas_call(
        functools.partial(_xattn_sample_kernel, sb=sb),
        grid=(B // sb,),
        in_specs=[pl.BlockSpec((sb, H, dh), lambda i: (i, 0, 0)),
                  pl.BlockSpec((None, sb, M, H, dh), lambda i: (layer, i, 0, 0, 0)),
                  pl.BlockSpec((None, sb, M, H, dh), lambda i: (layer, i, 0, 0, 0))],
        out_specs=pl.BlockSpec((sb, H, dh), lambda i: (i, 0, 0)),
        out_shape=jax.ShapeDtypeStruct((B, H, dh), F32),
        compiler_params=_cparams("parallel"),
    )(q.reshape(B, H, dh), ck, cv)
    return out.reshape(B, D)


def _lane_cat_masks(L):
    row = lax.broadcasted_iota(jnp.int32, (L, HW), 0)
    j = lax.broadcasted_iota(jnp.int32, (L, HW), 1) % DH
    r2 = lax.broadcasted_iota(jnp.int32, (HW, HW), 0) // DH
    c2 = lax.broadcasted_iota(jnp.int32, (HW, HW), 1) // DH
    return row >= j, row > j, row == j, r2 == c2


def _expand_bd(x, bd):
    return jnp.where(bd, jnp.concatenate([x] * HEADS, axis=0), jnp.zeros((), x.dtype))


def _seg_reduce(x, op, fill):
    lo = lax.broadcasted_iota(jnp.int32, (x.shape[0], LANES), 1) < DH
    parts = []
    for c in range(HW // LANES):
        xh = x[:, c * LANES:(c + 1) * LANES]
        a = op(jnp.where(lo, xh, fill), axis=-1, keepdims=True)
        b = op(jnp.where(lo, fill, xh), axis=-1, keepdims=True)
        parts.append(jnp.where(lo, a, b))
    return jnp.concatenate(parts, axis=-1)


def _head_expander(first_lane):
    r = lax.broadcasted_iota(jnp.int32, (GATE_PAD, HW), 0)
    c = lax.broadcasted_iota(jnp.int32, (GATE_PAD, HW), 1) // DH
    return (r == c + first_lane).astype(BF16)


def _chunk_tril(tt):
    r = lax.broadcasted_iota(jnp.int32, (tt, tt), 0)
    c = lax.broadcasted_iota(jnp.int32, (tt, tt), 1)
    return ((r // CHUNK == c // CHUNK) & (r >= c)).astype(BF16)


def _split3(x):
    hi = x.astype(BF16)
    r = x - hi.astype(F32)
    mid = r.astype(BF16)
    return hi, mid, (r - mid.astype(F32)).astype(BF16)


def _dot_sel_r(x, sel):
    hi, mid, lo = _split3(x)
    return (_dot(lo, sel) + _dot(mid, sel)) + _dot(hi, sel)


def _dot_sel_l(sel, x):
    hi, mid, lo = _split3(x)
    return (_dot(sel, lo) + _dot(sel, mid)) + _dot(sel, hi)


def _col_to_row(x, eye):
    return jnp.sum(jnp.where(eye, x, 0.0), axis=0, keepdims=True)


def _head_rms(x, ones_bd, g_row):
    ms = _dot_sel_r(x * x, ones_bd) * (1.0 / DH)
    return x * lax.rsqrt(ms + EPS) * g_row


def _mlstm_prompt_kernel(za_ref, gt_ref, bias_ref, norm_ref, ha_ref, c_ref, n_ref, m_ref,
                         c_scr, n_scr, m_scr, ix_scr, bx_scr, *, nchunks, group):
    tb = pl.program_id(1)
    L = CHUNK
    tril, _, eye, bd = _lane_cat_masks(L)
    ones_bd = bd.astype(BF16)

    @pl.when(tb == 0)
    def _():
        c_scr[...] = jnp.zeros_like(c_scr)
        n_scr[...] = jnp.zeros_like(n_scr)
        m_scr[...] = jnp.zeros_like(m_scr)

    gt = gt_ref[...] + bias_ref[...]
    b_cols = _dot_sel_l(_chunk_tril(nchunks * L), _log_sigmoid(gt))
    ix_scr[...] = _dot_sel_r(gt, _head_expander(0))
    bx_scr[...] = _dot_sel_r(b_cols, _head_expander(HEADS))

    def chunks(gi, carry):
        c_bd, n_row, m_x = carry
        rows = [pl.ds(pl.multiple_of((gi * group + j) * L, L), L) for j in range(group)]
        q = [(za_ref[r, 0:HW] * (DH ** -0.5)).astype(BF16) for r in rows]
        k = [za_ref[r, HW:2 * HW] for r in rows]
        v = [za_ref[r, 2 * HW:3 * HW].astype(BF16) for r in rows]
        i_x = [ix_scr[r, :] for r in rows]
        b_x = [bx_scr[r, :] for r in rows]

        m_in, n_in, kw, decay = [], [], [], []
        for kk, ii, bb in zip(k, i_x, b_x):
            b_last = bb[L - 1:L, :]
            g_x = b_last - bb + ii
            m_new = jnp.maximum(b_last + m_x, jnp.max(g_x, axis=0, keepdims=True))
            kw.append(kk * jnp.exp(g_x - m_new))
            decay.append(jnp.exp(b_last + m_x - m_new))
            m_in.append(m_x)
            n_in.append(n_row)
            n_row = decay[-1] * n_row + jnp.sum(kw[-1], axis=0, keepdims=True)
            m_x = m_new

        d_intra = [jnp.where(tril, bb - _col_to_row(bb, eye) + _col_to_row(ii, eye), -jnp.inf) for bb, ii in zip(b_x, i_x)]
        d_inter = [bb + mm for bb, mm in zip(b_x, m_in)]
        m_t = [jnp.maximum(_seg_reduce(di, jnp.max, -jnp.inf), de) for di, de in zip(d_intra, d_inter)]
        w_inter = [jnp.exp(de - mt) for de, mt in zip(d_inter, m_t)]
        s = [_dot_nt(qq, _expand_bd(kk.astype(BF16), bd)) * jnp.exp(di - mt) for qq, kk, di, mt in zip(q, k, d_intra, m_t)]
        kv = [jnp.where(bd, _dot(kwj.T.astype(BF16), vv), 0.0) for kwj, vv in zip(kw, v)]
        c_in = []
        for dj, kvj in zip(decay, kv):
            c_in.append(c_bd)
            c_bd = dj * c_bd + kvj
        num = [wi * _dot(qq, cc.astype(BF16)) + _dot(ss.astype(BF16), _expand_bd(vv, bd))
               for wi, qq, cc, ss, vv in zip(w_inter, q, c_in, s, v)]
        den = [wi * _dot((qq.astype(F32) * nn).astype(BF16), ones_bd) + _seg_reduce(ss, jnp.sum, 0.0)
               for wi, qq, nn, ss in zip(w_inter, q, n_in, s)]
        for r, nu, de, mt in zip(rows, num, den, m_t):
            hh = nu / jnp.maximum(jnp.abs(de), jnp.exp(-mt))
            ha_ref[r, :] = _head_rms(hh, ones_bd, norm_ref[...]) * _sigmoid(za_ref[r, 3 * HW:4 * HW])
        return c_bd, n_row, m_x

    c_bd, n_row, m_x = lax.fori_loop(0, nchunks // group, chunks, (c_scr[...], n_scr[...], m_scr[...]))
    c_scr[...] = c_bd
    n_scr[...] = n_row
    m_scr[...] = m_x

    @pl.when(tb == pl.num_programs(1) - 1)
    def _():
        for h in range(HEADS):
            c_ref[0, h] = c_scr[h * DH:(h + 1) * DH, h * DH:(h + 1) * DH]
        n_ref[0] = n_scr[...]
        m_ref[0] = m_scr[...]


def mlstm_prompt(za, gates, b_i, b_f, norm, B, S, tt):
    assert S % tt == 0 and tt % CHUNK == 0
    nt = S // tt
    bias = jnp.zeros((1, GATE_PAD), F32).at[0, 0:HEADS].set(b_i).at[0, HEADS:2 * HEADS].set(b_f)
    ha, c, n, m = pl.pallas_call(
        functools.partial(_mlstm_prompt_kernel, nchunks=tt // CHUNK, group=math.gcd(tt // CHUNK, 4)),
        grid=(B, nt),
        in_specs=[pl.BlockSpec((tt, 4 * HW), lambda b, t: (b * nt + t, 0)),
                  pl.BlockSpec((tt, GATE_PAD), lambda b, t: (b * nt + t, 0)),
                  pl.BlockSpec((1, GATE_PAD), lambda b, t: (0, 0)),
                  pl.BlockSpec((1, HW), lambda b, t: (0, 0))],
        out_specs=[pl.BlockSpec((tt, HW), lambda b, t: (b * nt + t, 0)),
                   pl.BlockSpec((1, HEADS, DH, DH), lambda b, t: (b, 0, 0, 0)),
                   pl.BlockSpec((1, 1, HW), lambda b, t: (b, 0, 0)),
                   pl.BlockSpec((1, 1, HW), lambda b, t: (b, 0, 0))],
        out_shape=[jax.ShapeDtypeStruct((B * S, HW), F32),
                   jax.ShapeDtypeStruct((B, HEADS, DH, DH), F32),
                   jax.ShapeDtypeStruct((B, 1, HW), F32),
                   jax.ShapeDtypeStruct((B, 1, HW), F32)],
        scratch_shapes=[pltpu.VMEM((HW, HW), F32), pltpu.VMEM((1, HW), F32), pltpu.VMEM((1, HW), F32),
                        pltpu.VMEM((tt, HW), F32), pltpu.VMEM((tt, HW), F32)],
        compiler_params=_cparams("parallel", "arbitrary"),
    )(za, gates, bias, norm.reshape(1, HW))
    return ha, c, n.reshape(B, HEADS, DH), m[:, 0, ::DH]


def _gdn_prompt_kernel(zc_ref, zg_ref, gt_ref, alog_ref, dtb_ref, norm_ref,
                       hc_ref, s_ref,
                       s_scr, q_scr, k_scr, v_scr, beta_scr, g_scr, uv_scr, wq_scr, qkm_scr, kwt_scr,
                       *, nseq, nchunks, group):
    tb = pl.program_id(1)
    L = CHUNK
    tt = nchunks * L
    tril, strict, eye, bd = _lane_cat_masks(L)
    ones_bd = bd.astype(BF16)

    @pl.when(tb == 0)
    def _():
        s_scr[...] = jnp.zeros_like(s_scr)

    for i in range(nseq):
        sr = slice(i * tt, (i + 1) * tt)
        y = zc_ref[i]
        q_raw, k_raw = y[:, 0:HW], y[:, HW:2 * HW]
        q_scr[sr, :] = (q_raw * lax.rsqrt(_dot_sel_r(q_raw * q_raw, ones_bd) + EPS) * (DH ** -0.5)).astype(BF16)
        k_scr[sr, :] = k_raw * lax.rsqrt(_dot_sel_r(k_raw * k_raw, ones_bd) + EPS)
        v_scr[sr, :] = y[:, 2 * HW:3 * HW]
        gt = gt_ref[i]
        beta_scr[sr, :] = _dot_sel_r(_sigmoid(gt), _head_expander(2 * HEADS))
        la_cols = -jnp.exp(alog_ref[...]) * _softplus(gt + dtb_ref[...])
        g_scr[sr, :] = _dot_sel_r(_dot_sel_l(_chunk_tril(tt), la_cols), _head_expander(3 * HEADS))

    def prepare(gi, carry):
        cis = [gi * group + j for j in range(group)]
        rows = [pl.ds(pl.multiple_of(ci * L, L), L) for ci in cis]
        k = [k_scr[r, :] for r in rows]
        g_x = [g_scr[r, :] for r in rows]
        beta_row = [_col_to_row(beta_scr[r, :], eye) for r in rows]
        dec_incl = [jnp.where(tril, jnp.exp(jnp.where(tril, g - _col_to_row(g, eye), 0.0)), 0.0) for g in g_x]
        k_bd = [_expand_bd(kk.astype(BF16), bd) for kk in k]
        n0 = [-(jnp.where(strict, d, 0.0) * _dot_nt(kk.astype(BF16), kbd) * br)
              for d, kk, kbd, br in zip(dec_incl, k, k_bd, beta_row)]
        for r, d, kbd, br in zip(rows, dec_incl, k_bd, beta_row):
            qkm_scr[r, :] = (_dot_nt(q_scr[r, :], kbd) * d * br).astype(BF16)

        p = [_dot(n.astype(BF16), _expand_bd(n.astype(BF16), bd)) for n in n0]
        m = n0
        steps = int(math.log2(L)) - 1
        for i in range(steps):
            p_bd = [_expand_bd(pp.astype(BF16), bd) for pp in p]
            if i < steps - 1:
                pm = [_dot(jnp.concatenate([pp, mm], axis=0).astype(BF16), pbd) for pp, mm, pbd in zip(p, m, p_bd)]
                p_next, mp = [x[0:L] for x in pm], [x[L:2 * L] for x in pm]
            else:
                p_next, mp = None, [_dot(mm.astype(BF16), pbd) for mm, pbd in zip(m, p_bd)]
            m = [mm + pp + x for mm, pp, x in zip(m, p, mp)]
            p = p_next

        for ci, r, kk, g, mm in zip(cis, rows, k, g_x, m):
            v = v_scr[r, :]
            egk = jnp.exp(g) * kk
            rhs_bd = jnp.concatenate([_expand_bd(v.astype(BF16), bd), _expand_bd(egk.astype(BF16), bd)], axis=1)
            mr = _dot(mm.astype(BF16), rhs_bd)
            uv_scr[r, :] = v + mr[:, 0:HW]
            wq_rows = pl.multiple_of(ci * 2 * L, 2 * L)
            wq_scr[pl.ds(wq_rows, L), :] = (egk + mr[:, HW:2 * HW]).astype(BF16)
            wq_scr[pl.ds(wq_rows + L, L), :] = q_scr[r, :]
            kw = kk * (jnp.exp(g[L - 1:L, :] - g) * beta_scr[r, :])
            kwt_scr[pl.ds(pl.multiple_of(ci * HW, HW), HW), :] = kw.T.astype(BF16)
        return carry

    lax.fori_loop(0, nseq * nchunks // group, prepare, 0)

    def advance(c, states):
        cis = [i * nchunks + c for i in range(nseq)]
        rows = [pl.ds(pl.multiple_of(ci * L, L), L) for ci in cis]
        wqs = [_dot(wq_scr[pl.ds(pl.multiple_of(ci * 2 * L, 2 * L), 2 * L), :], s.astype(BF16))
               for ci, s in zip(cis, states)]
        ub = [(uv_scr[r, :] - x[0:L]).astype(BF16) for r, x in zip(rows, wqs)]
        new = [jnp.exp(g_scr[pl.ds(pl.multiple_of(ci * L + L - SUBLANES, SUBLANES), SUBLANES), :][SUBLANES - 1:, :]) * s
               + jnp.where(bd, _dot(kwt_scr[pl.ds(pl.multiple_of(ci * HW, HW), HW), :], u), 0.0)
               for ci, s, u in zip(cis, states, ub)]
        for i, (r, x, u) in enumerate(zip(rows, wqs, ub)):
            o = jnp.exp(g_scr[r, :]) * x[L:2 * L] + _dot(qkm_scr[r, :], _expand_bd(u, bd))
            hc_ref[i, pl.ds(pl.multiple_of(c * L, L), L), :] = (
                _head_rms(o, ones_bd, norm_ref[...]) * _silu(zg_ref[i, pl.ds(pl.multiple_of(c * L, L), L), :]))
        return tuple(new)

    states = lax.fori_loop(0, nchunks, advance, tuple(s_scr[i] for i in range(nseq)))
    for i in range(nseq):
        s_scr[i] = states[i]

    @pl.when(tb == pl.num_programs(1) - 1)
    def _():
        for i in range(nseq):
            for h in range(HEADS):
                s_ref[i, h] = s_scr[i, h * DH:(h + 1) * DH, h * DH:(h + 1) * DH]


def _gdn_gate_rows(a_log, dt_bias):
    z = jnp.zeros((1, GATE_PAD), F32)
    return (z.at[0, 3 * HEADS:4 * HEADS].set(a_log), z.at[0, 3 * HEADS:4 * HEADS].set(dt_bias))


def gdn_prompt(zc, zg, gates, a_log, dt_bias, norm, B, S, tt):
    assert S % tt == 0 and tt % CHUNK == 0
    nt = S // tt
    nseq = math.gcd(B, GDN_SEQS)
    rows = nseq * tt
    alog, dtb = _gdn_gate_rows(a_log, dt_bias)
    blk = lambda w: pl.BlockSpec((nseq, tt, w), lambda b, t: (b, t, 0))
    row = lambda w: pl.BlockSpec((1, w), lambda b, t: (0, 0))
    hc, s = pl.pallas_call(
        functools.partial(_gdn_prompt_kernel, nseq=nseq, nchunks=tt // CHUNK, group=math.gcd(rows // CHUNK, 8)),
        grid=(B // nseq, nt),
        in_specs=[blk(3 * HW), blk(HW), blk(GATE_PAD), row(GATE_PAD), row(GATE_PAD), row(HW)],
        out_specs=[blk(HW), pl.BlockSpec((nseq, HEADS, DH, DH), lambda b, t: (b, 0, 0, 0))],
        out_shape=[jax.ShapeDtypeStruct((B, S, HW), F32), jax.ShapeDtypeStruct((B, HEADS, DH, DH), F32)],
        scratch_shapes=[pltpu.VMEM((nseq, HW, HW), F32),
                        pltpu.VMEM((rows, HW), BF16), pltpu.VMEM((rows, HW), F32), pltpu.VMEM((rows, HW), F32),
                        pltpu.VMEM((rows, HW), F32), pltpu.VMEM((rows, HW), F32), pltpu.VMEM((rows, HW), F32),
                        pltpu.VMEM((2 * rows, HW), BF16), pltpu.VMEM((rows, HW), BF16),
                        pltpu.VMEM((rows // CHUNK * HW, CHUNK), BF16)],
        compiler_params=_cparams("parallel", "arbitrary"),
    )(zc.reshape(B, S, 3 * HW), zg.reshape(B, S, HW), gates.reshape(B, S, GATE_PAD), alog, dtb,
      jnp.tile(norm, HEADS).reshape(1, HW))
    return hc.reshape(B * S, HW), s


def _s5_prep_kernel(are_ref, aim_ref, ldt_ref, bre_ref, bim_ref, lre_ref, lim_ref, bbre_ref, bbim_ref):
    a_re, a_im = are_ref[...], aim_ref[...]
    dt = jnp.exp(ldt_ref[...])
    mag = jnp.exp(a_re * dt)
    lam_re, lam_im = mag * jnp.cos(a_im * dt), mag * jnp.sin(a_im * dt)
    lre_ref[...] = lam_re
    lim_ref[...] = lam_im
    nr, ni = lam_re - 1.0, lam_im
    den = a_re * a_re + a_im * a_im
    coef_re = (nr * a_re + ni * a_im) / den
    coef_im = (ni * a_re - nr * a_im) / den
    b_re, b_im = bre_ref[...], bim_ref[...]
    bbre_ref[...] = coef_re * b_re - coef_im * b_im
    bbim_ref[...] = coef_re * b_im + coef_im * b_re


def s5_params(lp):
    G, N, P = S5_GROUPS, S5_N, S5_P
    row = lambda a: a.astype(F32).reshape(1, G * N)
    to_pn = lambda b: jnp.transpose(b.astype(F32), (2, 0, 1)).reshape(P, G * N)
    shp = [jax.ShapeDtypeStruct((1, G * N), F32)] * 2 + [jax.ShapeDtypeStruct((P, G * N), F32)] * 2
    lam_re, lam_im, bb_re, bb_im = pl.pallas_call(_s5_prep_kernel, out_shape=shp)(
        row(lp['s5_a_re']), row(lp['s5_a_im']), row(jnp.repeat(lp['s5_log_dt'][:, None], N, axis=1)),
        to_pn(lp['s5_b_re']), to_pn(lp['s5_b_im']))
    eye = jnp.eye(S5_GB, dtype=F32)

    def w_in_blocks(bb):
        b4 = bb.reshape(P, S5_NBLK, S5_GB, N)
        return jnp.einsum('pbgn,gh->bgphn', b4, eye).reshape(S5_NBLK, S5_GB * P, S5_GB * N)

    def w_out_blocks(c):
        c4 = c.astype(F32).reshape(S5_NBLK, S5_GB, P, N)
        return jnp.einsum('bgpn,gh->bgnhp', c4, eye).reshape(S5_NBLK, S5_GB * N, S5_GB * P)

    w_in = jnp.concatenate([w_in_blocks(bb_re), w_in_blocks(bb_im)], axis=-1).astype(BF16)
    return {'lam_re': lam_re, 'lam_im': lam_im, 'w_in': w_in,
            'w_out_re': w_out_blocks(lp['s5_c_re']).astype(BF16),
            'w_out_im': (-w_out_blocks(lp['s5_c_im'])).astype(BF16),
            'd': lp['s5_d'].astype(F32).reshape(1, S5_WIDTH), 'w_glu': lp['s5_w_glu'].astype(BF16)}


def _s5_kernel(u_ref, h0r_ref, h0i_ref, lamr_ref, lami_ref, win_ref, wor_ref, woi_ref, d_ref, wglu_ref,
               ys_ref, h1r_ref, h1i_ref, hr_scr, hi_scr, br_scr, bi_scr, *, nseq, rows, bb, lane_blk):
    tb = pl.program_id(0)
    nsteps = nseq * rows // bb
    blk_in, blk_st = S5_GB * S5_P, S5_GB * S5_N

    @pl.when(tb == 0)
    def _():
        hr_scr[...] = h0r_ref[...]
        hi_scr[...] = h0i_ref[...]

    if nseq > 1:
        u = jnp.swapaxes(u_ref[...], 0, 1).reshape(nseq * rows, S5_WIDTH)
    else:
        u = u_ref[0]
    ub = u.astype(BF16)
    for blk in range(S5_NBLK):
        bu = _dot(ub[:, blk * blk_in:(blk + 1) * blk_in], win_ref[blk])
        br_scr[:, blk * blk_st:(blk + 1) * blk_st] = bu[:, 0:blk_st]
        bi_scr[:, blk * blk_st:(blk + 1) * blk_st] = bu[:, blk_st:2 * blk_st]

    for lb in range(S5_STATE // lane_blk):
        ls = slice(lb * lane_blk, (lb + 1) * lane_blk)
        lr = jnp.broadcast_to(lamr_ref[:, ls], (bb, lane_blk))
        li = jnp.broadcast_to(lami_ref[:, ls], (bb, lane_blk))

        def step(t, carry):
            hr, hi = carry
            r = pl.ds(pl.multiple_of(t * bb, bb), bb)
            nr = lr * hr - li * hi + br_scr[r, ls]
            ni = lr * hi + li * hr + bi_scr[r, ls]
            br_scr[r, ls] = nr
            bi_scr[r, ls] = ni
            return nr, ni

        hr, hi = lax.fori_loop(0, nsteps, step, (hr_scr[:, ls], hi_scr[:, ls]), unroll=min(nsteps, 8))
        hr_scr[:, ls] = hr
        hi_scr[:, ls] = hi

    ys = []
    for blk in range(S5_NBLK):
        st = slice(blk * blk_st, (blk + 1) * blk_st)
        ys.append(_dot(br_scr[:, st].astype(BF16), wor_ref[blk]) + _dot(bi_scr[:, st].astype(BF16), woi_ref[blk]))
    gy = jax.nn.gelu(jnp.concatenate(ys, axis=-1) + d_ref[...] * u)
    out = gy * _sigmoid(_dot(gy.astype(BF16), wglu_ref[...]))
    if nseq > 1:
        ys_ref[...] = jnp.swapaxes(out.reshape(rows, nseq, S5_WIDTH), 0, 1)
    else:
        ys_ref[0] = out

    @pl.when(tb == pl.num_programs(0) - 1)
    def _():
        h1r_ref[...] = hr_scr[...]
        h1i_ref[...] = hi_scr[...]


def s5_mixer(u, h0_re, h0_im, sp, tt, single_step):
    B = h0_re.shape[0]
    nseq, S = (1, 1) if single_step else (B, u.shape[1])
    rows = B if single_step else tt
    assert S % tt == 0 and B % SUBLANES == 0
    assert u.shape == ((1, B, S5_WIDTH) if single_step else (B, S, S5_WIDTH))
    lane_blk = max(LANES, min(S5_STATE, (SUBLANES * SUBLANES * LANES) // B))
    full = lambda shape: pl.BlockSpec(shape, lambda t: (0,) * len(shape))
    return pl.pallas_call(
        functools.partial(_s5_kernel, nseq=nseq, rows=rows, bb=B, lane_blk=lane_blk),
        grid=(S // tt,),
        in_specs=[pl.BlockSpec((nseq, rows, S5_WIDTH), lambda t: (0, t, 0)),
                  full((B, S5_STATE)), full((B, S5_STATE)), full((1, S5_STATE)), full((1, S5_STATE)),
                  full(sp['w_in'].shape), full(sp['w_out_re'].shape), full(sp['w_out_im'].shape),
                  full((1, S5_WIDTH)), full((S5_WIDTH, S5_WIDTH))],
        out_specs=[pl.BlockSpec((nseq, rows, S5_WIDTH), lambda t: (0, t, 0)),
                   full((B, S5_STATE)), full((B, S5_STATE))],
        out_shape=[jax.ShapeDtypeStruct(u.shape, F32),
                   jax.ShapeDtypeStruct((B, S5_STATE), F32), jax.ShapeDtypeStruct((B, S5_STATE), F32)],
        scratch_shapes=[pltpu.VMEM((B, S5_STATE), F32), pltpu.VMEM((B, S5_STATE), F32),
                        pltpu.VMEM((nseq * rows, S5_STATE), F32), pltpu.VMEM((nseq * rows, S5_STATE), F32)],
        compiler_params=_cparams("arbitrary"),
    )(u, h0_re, h0_im, sp['lam_re'], sp['lam_im'], sp['w_in'], sp['w_out_re'], sp['w_out_im'], sp['d'], sp['w_glu'])


def _ones_bd():
    r = lax.broadcasted_iota(jnp.int32, (HW, HW), 0) // DH
    c = lax.broadcasted_iota(jnp.int32, (HW, HW), 1) // DH
    return (r == c).astype(BF16)


def _mlstm_sample_kernel(za_ref, gt_ref, bias_ref, norm_ref, c_ref, n_ref, m_ref,
                         ha_ref, c_out, n_out, m_out, q_scr, kw_scr, h_scr):
    za = za_ref[...]
    q_scr[...] = (za[:, 0:HW] * (DH ** -0.5)).T
    k_t = za[:, HW:2 * HW].T
    v_t = za[:, 2 * HW:3 * HW].T
    g_t = (gt_ref[...] + bias_ref[...]).T
    m_out[...] = jnp.zeros_like(m_out)
    for h in range(HEADS):
        hs = slice(h * DH, (h + 1) * DH)
        i_h = g_t[h:h + 1, :]
        bm = _log_sigmoid(g_t[HEADS + h:HEADS + h + 1, :]) + m_ref[h:h + 1, :]
        m_t = jnp.maximum(i_h, bm)
        w_in = jnp.exp(i_h - m_t)
        w_st = jnp.exp(bm - m_t)
        q_h, k_h, v_h = q_scr[hs, :], k_t[hs, :], v_t[hs, :]
        s = jnp.sum(q_h * k_h, axis=0, keepdims=True) * w_in
        kw_scr[hs, :] = k_h * w_in

        def body(d, acc):
            r = h * DH + d
            rows = pl.ds(pl.multiple_of(r * DH, DH), DH)
            c_hd = c_ref[rows, :]
            c_out[rows, :] = w_st * c_hd + kw_scr[pl.ds(r, 1), :] * v_h
            return acc + q_scr[pl.ds(r, 1), :] * c_hd

        qc = lax.fori_loop(0, DH, body, jnp.zeros((DH, za.shape[0]), F32), unroll=4)
        n_h = n_ref[hs, :]
        num = w_st * qc + s * v_h
        den = w_st * jnp.sum(q_h * n_h, axis=0, keepdims=True) + s
        h_scr[hs, :] = num / jnp.maximum(jnp.abs(den), jnp.exp(-m_t))
        n_out[hs, :] = w_st * n_h + kw_scr[hs, :]
        m_out[h:h + 1, :] = m_t
    ha_ref[...] = _head_rms(h_scr[...].T, _ones_bd(), norm_ref[...]) * _sigmoid(za[:, 3 * HW:4 * HW])


def mlstm_sample(za, gates, b_i, b_f, norm, c_t, n_t, m_t):
    B = za.shape[0]
    bias = jnp.zeros((1, GATE_PAD), F32).at[0, 0:HEADS].set(b_i).at[0, HEADS:2 * HEADS].set(b_f)
    shp = lambda *s: jax.ShapeDtypeStruct(s, F32)
    return pl.pallas_call(
        _mlstm_sample_kernel,
        out_shape=[shp(B, HW), shp(HW * DH, B), shp(HW, B), shp(SUBLANES, B)],
        scratch_shapes=[pltpu.VMEM((HW, B), F32), pltpu.VMEM((HW, B), F32), pltpu.VMEM((HW, B), F32)],
        compiler_params=pltpu.CompilerParams(vmem_limit_bytes=VMEM_LIMIT),
    )(za, gates, bias, norm.reshape(1, HW), c_t, n_t, m_t)


def _gdn_sample_kernel(zc_ref, zg_ref, gt_ref, buf_ref, cw_ref, alog_ref, dtb_ref, norm_ref, s_ref,
                       hc_ref, s_out, buf_out, q_scr, k_scr, o_scr):
    W3 = 3 * HW
    x = zc_ref[...]
    y = cw_ref[CONV_K - 1:CONV_K, :] * x
    for j in range(CONV_K - 1):
        y = y + cw_ref[j:j + 1, :] * buf_ref[:, j * W3:(j + 1) * W3]
    buf_out[:, 0:(CONV_K - 2) * W3] = buf_ref[:, W3:(CONV_K - 1) * W3]
    buf_out[:, (CONV_K - 2) * W3:(CONV_K - 1) * W3] = x
    y = _silu(y)
    ones_bd = _ones_bd()
    q_raw, k_raw = y[:, 0:HW], y[:, HW:2 * HW]
    q_scr[...] = (q_raw * lax.rsqrt(_dot_sel_r(q_raw * q_raw, ones_bd) + EPS) * (DH ** -0.5)).T
    k_scr[...] = (k_raw * lax.rsqrt(_dot_sel_r(k_raw * k_raw, ones_bd) + EPS)).T
    v_t = y[:, 2 * HW:3 * HW].T
    gt = gt_ref[...]
    beta_t = _sigmoid(gt).T
    la_t = (-jnp.exp(alog_ref[...]) * _softplus(gt + dtb_ref[...])).T
    nb = x.shape[0]
    for h in range(HEADS):
        hs = slice(h * DH, (h + 1) * DH)
        beta = beta_t[2 * HEADS + h:2 * HEADS + h + 1, :]
        eg = jnp.exp(la_t[3 * HEADS + h:3 * HEADS + h + 1, :])
        q_h, k_h, v_h = q_scr[hs, :], k_scr[hs, :], v_t[hs, :]

        def read(d, acc):
            ks, qs = acc
            r = h * DH + d
            s_hd = s_ref[pl.ds(pl.multiple_of(r * DH, DH), DH), :]
            return ks + k_scr[pl.ds(r, 1), :] * s_hd, qs + q_scr[pl.ds(r, 1), :] * s_hd

        zero = jnp.zeros((DH, nb), F32)
        ks, qs = lax.fori_loop(0, DH, read, (zero, zero), unroll=4)
        u = v_h - eg * ks
        o_scr[hs, :] = eg * qs + (jnp.sum(q_h * k_h, axis=0, keepdims=True) * beta) * u

        def write(d, carry):
            r = h * DH + d
            rows = pl.ds(pl.multiple_of(r * DH, DH), DH)
            s_out[rows, :] = eg * s_ref[rows, :] + (beta * k_scr[pl.ds(r, 1), :]) * u
            return carry

        lax.fori_loop(0, DH, write, 0, unroll=4)
    hc_ref[...] = _head_rms(o_scr[...].T, ones_bd, norm_ref[...]) * _silu(zg_ref[...])


def gdn_sample(zc, zg, gates, buf, conv_w, a_log, dt_bias, norm, s_t):
    B = zc.shape[0]
    alog, dtb = _gdn_gate_rows(a_log, dt_bias)
    shp = lambda *s: jax.ShapeDtypeStruct(s, F32)
    return pl.pallas_call(
        _gdn_sample_kernel,
        out_shape=[shp(B, HW), shp(HW * DH, B), shp(B, (CONV_K - 1) * 3 * HW)],
        scratch_shapes=[pltpu.VMEM((HW, B), F32), pltpu.VMEM((HW, B), F32), pltpu.VMEM((HW, B), F32)],
        compiler_params=pltpu.CompilerParams(vmem_limit_bytes=VMEM_LIMIT),
    )(zc, zg, gates, buf, conv_w, alog, dtb, jnp.tile(norm, HEADS).reshape(1, HW), s_t)


FFN_CHUNK = 1408
IN_SEGMENTS = (4 * HW, S5_WIDTH, 3 * HW, HW, GATE_PAD)


def _tile(n, pref):
    return pref if n % pref == 0 else n


def prep_weights(p):
    w = p['w_in'].astype(BF16)
    a, g4 = 4 * HW, HEADS
    o_u = a + 2 * g4
    o_c = o_u + S5_WIDTH
    o_g = o_c + 3 * HW
    o_b = o_g + HW
    gate_cols = jnp.concatenate([w[:, :, a:a + 2 * g4], w[:, :, o_b:o_b + 2 * g4],
                                 jnp.zeros(w.shape[:2] + (GATE_PAD - 4 * g4,), w.dtype)], axis=2)
    w_in = jnp.concatenate([w[:, :, 0:a], w[:, :, o_u:o_c], w[:, :, o_c:o_g], w[:, :, o_g:o_b], gate_cols], axis=2)
    bf = lambda n: p[n].astype(BF16)
    return {'w_in': w_in, 'w_out': bf('w_out'), 'w_mq': bf('w_mq'), 'w_mo': bf('w_mo'), 'w_mk': bf('w_mk'),
            'w_mv': bf('w_mv'), 'w_gate': bf('w_gate'), 'w_up': bf('w_up'), 'w_down': bf('w_down')}


def mixer_prompt(x, lp, W, layer, B, S):
    T = B * S
    za, zu, zc, zg, gates, tail = in_proj_prompt(x, lp['norm_mix'], W['w_in'], layer, lp['gdn_conv_w'], S,
                                                 _tile(S, 512))
    buf1 = tail[:, SUBLANES - (CONV_K - 1):, :]
    tt = _tile(S, 512)
    ha, c1, n1, m1 = mlstm_prompt(za, gates, lp['mlstm_b_i'], lp['mlstm_b_f'], lp['mlstm_norm'], B, S, tt)
    h0 = jnp.zeros((B, S5_STATE), F32)
    ys3, r1, i1 = s5_mixer(zu.reshape(B, S, S5_WIDTH), h0, h0, lp['s5'], _tile(S, 128), False)
    ys = ys3.reshape(T, S5_WIDTH)
    hc, s1 = gdn_prompt(zc, zg, gates, lp['gdn_a_log'], lp['gdn_dt_bias'], lp['gdn_norm'], B, S, _tile(S, 256))
    return [ha, ys, hc], (c1, n1, m1, r1.reshape(B, S5_GROUPS, S5_N), i1.reshape(B, S5_GROUPS, S5_N), s1, buf1)


def mixer_sample(x, st, lp, W, layer):
    B = x.shape[0]
    c0, n0, m0, r0, i0, s0, buf0 = st
    za, zu, zc, zg, gates = norm_matmul(x, lp['norm_mix'], W['w_in'], layer, IN_SEGMENTS, B)
    m_t = jnp.zeros((SUBLANES, B), F32).at[0:HEADS, :].set(m0.T)
    ha, c1t, n1t, m1t = mlstm_sample(za, gates, lp['mlstm_b_i'], lp['mlstm_b_f'], lp['mlstm_norm'],
                                     c0.reshape(B, HW * DH).T, n0.reshape(B, HW).T, m_t)
    ys3, r1, i1 = s5_mixer(zu.reshape(1, B, S5_WIDTH), r0.reshape(B, S5_STATE), i0.reshape(B, S5_STATE), lp['s5'], 1, True)
    hc, s1t, buf1 = gdn_sample(zc, zg, gates, buf0.reshape(B, (CONV_K - 1) * 3 * HW), lp['gdn_conv_w'],
                               lp['gdn_a_log'], lp['gdn_dt_bias'], lp['gdn_norm'], s0.reshape(B, HW * DH).T)
    x1 = matmul_residual(x, [ha, ys3.reshape(B, S5_WIDTH), hc], W['w_out'], layer, B)
    return x1, (c1t.T.reshape(B, HEADS, DH, DH), n1t.T.reshape(B, HEADS, DH), m1t[0:HEADS, :].T,
                r1.reshape(B, S5_GROUPS, S5_N), i1.reshape(B, S5_GROUPS, S5_N),
                s1t.T.reshape(B, HEADS, DH, DH), buf1.reshape(B, CONV_K - 1, 3 * HW))


def _mem_kv_kernel(x_ref, g_ref, wk_ref, wv_ref, k2_ref, v2_ref, k5_ref, v5_ref):
    xn = _rms(x_ref[...], g_ref[...]).astype(BF16)
    for w_ref, o2_ref, o5_ref in ((wk_ref, k2_ref, k5_ref), (wv_ref, v2_ref, v5_ref)):
        y = _dot(xn, w_ref[...])
        o2_ref[...] = y
        o5_ref[...] = y.reshape(o5_ref.shape)


def mem_kv(mem, norm_mem, wk, wv):
    B, M, D = mem.shape
    L = wk.shape[0]
    dh = D // X_HEADS
    w_spec = pl.BlockSpec((None, D, D), lambda l, b: (l, 0, 0))
    o2_spec = pl.BlockSpec((None, None, M, D), lambda l, b: (l, b, 0, 0))
    o5_spec = pl.BlockSpec((None, None, M, X_HEADS, dh), lambda l, b: (l, b, 0, 0, 0))
    return pl.pallas_call(
        _mem_kv_kernel,
        grid=(L, B),
        in_specs=[pl.BlockSpec((None, M, D), lambda l, b: (b, 0, 0)),
                  pl.BlockSpec((None, 1, D), lambda l, b: (l, 0, 0)), w_spec, w_spec],
        out_specs=[o2_spec, o2_spec, o5_spec, o5_spec],
        out_shape=[jax.ShapeDtypeStruct((L, B, M, D), F32)] * 2
                  + [jax.ShapeDtypeStruct((L, B, M, X_HEADS, dh), F32)] * 2,
        compiler_params=_cparams("parallel", "parallel"),
    )(mem, norm_mem.reshape(L, 1, D), wk, wv)


def xattn_ffn_prompt(x, acts, mk, mv, lp, W, layer, S, norm_final, final):
    T, D = x.shape
    x2 = xattn_prompt(x, acts, W['w_out'], lp['norm_xattn'], W['w_mq'], mk, mv, W['w_mo'], layer, S, _tile(S, 512))
    return ffn(x2, lp['norm_ffn'], W['w_gate'], W['w_up'], W['w_down'], layer, norm_final, final, _tile(T, 512),
               FFN_CHUNK)


def xattn_ffn_sample(x, ck, cv, lp, W, layer, norm_final, final):
    B, D = x.shape
    (q,) = norm_matmul(x, lp['norm_xattn'], W['w_mq'], layer, (D,), B)
    o = xattn_sample(q, ck, cv, layer, 4)
    x2 = matmul_residual(x, [o], W['w_mo'], layer, B)
    return ffn(x2, lp['norm_ffn'], W['w_gate'], W['w_up'], W['w_down'], layer, norm_final, final, B, FFN_CHUNK)


LAYER_PARAMS = ('norm_mix', 'w_in', 'w_out', 'mlstm_b_i', 'mlstm_b_f', 'mlstm_norm', 's5_a_re', 's5_a_im', 's5_log_dt',
                's5_b_re', 's5_b_im', 's5_c_re', 's5_c_im', 's5_d', 's5_w_glu', 'gdn_conv_w', 'gdn_a_log',
                'gdn_dt_bias', 'gdn_norm', 'norm_xattn', 'norm_mem', 'w_mq', 'w_mk', 'w_mv', 'w_mo', 'norm_ffn',
                'w_gate', 'w_up', 'w_down')


def kernel(x_prompt, x_sample, mem_prompt, cache_mem_k, cache_mem_v, state_mlstm_c, state_mlstm_n, state_mlstm_m, state_s5_re, state_s5_im, state_gdn, state_gdn_conv, norm_mix, w_in, w_out, mlstm_b_i, mlstm_b_f, mlstm_norm, s5_a_re, s5_a_im, s5_log_dt, s5_b_re, s5_b_im, s5_c_re, s5_c_im, s5_d, s5_w_glu, gdn_conv_w, gdn_a_log, gdn_dt_bias, gdn_norm, norm_xattn, norm_mem, w_mq, w_mk, w_mv, w_mo, norm_ffn, w_gate, w_up, w_down, norm_final):
    stacked = dict(norm_mix=norm_mix, w_in=w_in, w_out=w_out, mlstm_b_i=mlstm_b_i, mlstm_b_f=mlstm_b_f,
                   mlstm_norm=mlstm_norm, s5_a_re=s5_a_re, s5_a_im=s5_a_im, s5_log_dt=s5_log_dt, s5_b_re=s5_b_re,
                   s5_b_im=s5_b_im, s5_c_re=s5_c_re, s5_c_im=s5_c_im, s5_d=s5_d, s5_w_glu=s5_w_glu,
                   gdn_conv_w=gdn_conv_w, gdn_a_log=gdn_a_log, gdn_dt_bias=gdn_dt_bias, gdn_norm=gdn_norm,
                   norm_xattn=norm_xattn, norm_mem=norm_mem, w_mq=w_mq, w_mk=w_mk, w_mv=w_mv, w_mo=w_mo,
                   norm_ffn=norm_ffn, w_gate=w_gate, w_up=w_up, w_down=w_down)
    B, S, D = x_prompt.shape
    Bs = x_sample.shape[0]
    M = mem_prompt.shape[1]
    depth = w_in.shape[0]
    xp = x_prompt.reshape(B * S, D)
    xs = x_sample.reshape(Bs, D)
    cache_k, cache_v = cache_mem_k, cache_mem_v
    st_p, st_s = [], []
    W = prep_weights(stacked)
    mk, mv, mem_k, mem_v = mem_kv(mem_prompt, norm_mem, W['w_mk'], W['w_mv'])
    for l in range(depth):
        lp = {n: stacked[n][l] for n in LAYER_PARAMS if n not in W}
        lp['s5'] = s5_params(lp)
        last = l == depth - 1
        acts, sp = mixer_prompt(xp, lp, W, l, B, S)
        xp = xattn_ffn_prompt(xp, acts, mk, mv, lp, W, l, S, norm_final, last)
        st_in = (state_mlstm_c[l], state_mlstm_n[l], state_mlstm_m[l], state_s5_re[l], state_s5_im[l],
                 state_gdn[l], state_gdn_conv[l])
        xs, ss = mixer_sample(xs, st_in, lp, W, l)
        xs = xattn_ffn_sample(xs, cache_k, cache_v, lp, W, l, norm_final, last)
        st_p.append(sp)
        st_s.append(ss)
    stack = lambda lst: [jnp.stack([st[i] for st in lst]) for i in range(7)]
    return (xp.reshape(B, S, D), xs.reshape(Bs, 1, D), mem_k, mem_v,
            *stack(st_p), *stack(st_s))
```

```python
import functools
import math

import jax
import jax.numpy as jnp
from jax import lax
from jax.experimental import pallas as pl
from jax.experimental.pallas import tpu as pltpu

F32 = jnp.float32
BF16 = jnp.bfloat16
EPS = 1e-6

HEADS = 4
DH = 64
HW = HEADS * DH
CHUNK = 64
S5_P = 16
S5_N = 64
S5_GROUPS = 32
S5_WIDTH = S5_GROUPS * S5_P
S5_STATE = S5_GROUPS * S5_N
S5_GB = 8
S5_NBLK = S5_GROUPS // S5_GB
CONV_K = 4
GDN_SEQS = 4
X_HEADS = 4
GATE_PAD = 128
LANES = 128
SUBLANES = 8
VMEM_LIMIT = 48 * 1024 * 1024


def _cparams(*sem):
    return pltpu.CompilerParams(dimension_semantics=sem, vmem_limit_bytes=VMEM_LIMIT)


def _rms(x, g_row):
    return x * lax.rsqrt(jnp.mean(x * x, axis=-1, keepdims=True) + EPS) * g_row


def _dot(a, b):
    return jnp.dot(a, b, preferred_element_type=F32)


def _dot_nt(a, b):
    return lax.dot_general(a, b, (((1,), (1,)), ((), ())), preferred_element_type=F32)


def _sigmoid(x):
    return 1.0 / (1.0 + jnp.exp(-x))


def _silu(x):
    return x * _sigmoid(x)


def _softplus(x):
    return jnp.maximum(x, 0.0) + jnp.log1p(jnp.exp(-jnp.abs(x)))


def _log_sigmoid(x):
    return jnp.minimum(x, 0.0) - jnp.log1p(jnp.exp(-jnp.abs(x)))


def _norm_matmul_kernel(x_ref, g_ref, w_ref, *out_refs, splits):
    xn = _rms(x_ref[...], g_ref[...]).astype(BF16)
    off = 0
    for o_ref, n in zip(out_refs, splits):
        o_ref[...] = _dot(xn, w_ref[:, off:off + n])
        off += n


def _layer_weight(w_all, layer):
    return pl.BlockSpec((None,) + w_all.shape[1:], lambda *_: (layer, 0, 0), pipeline_mode=pl.Buffered(1))


def norm_matmul(x, g, w_all, layer, splits, tm):
    T, D = x.shape
    N = w_all.shape[2]
    assert sum(splits) == N and T % tm == 0
    return pl.pallas_call(
        functools.partial(_norm_matmul_kernel, splits=tuple(splits)),
        grid=(T // tm,),
        in_specs=[pl.BlockSpec((tm, D), lambda i: (i, 0)),
                  pl.BlockSpec((1, D), lambda i: (0, 0)),
                  _layer_weight(w_all, layer)],
        out_specs=[pl.BlockSpec((tm, n), lambda i: (i, 0)) for n in splits],
        out_shape=[jax.ShapeDtypeStruct((T, n), F32) for n in splits],
        compiler_params=_cparams("parallel"),
    )(x, g.reshape(1, D), w_all)


def _in_proj_prompt_kernel(x_ref, g_ref, w_ref, cw_ref, za_ref, zu_ref, zc_ref, zg_ref, gt_ref, tail_ref, xp_scr,
                           *, tiles_per_seq):
    tm = x_ref.shape[0]
    pad = SUBLANES
    xn = _rms(x_ref[...], g_ref[...]).astype(BF16)
    offs = [sum(IN_SEGMENTS[:s]) for s in range(len(IN_SEGMENTS))]

    @pl.when(pl.program_id(0) % tiles_per_seq == 0)
    def _():
        xp_scr[0:pad, :] = jnp.zeros((pad, xp_scr.shape[1]), F32)

    xp_scr[pad:pad + tm, :] = _dot(xn, w_ref[:, offs[2]:offs[2] + IN_SEGMENTS[2]])
    others = ((0, za_ref), (1, zu_ref), (3, zg_ref), (4, gt_ref))
    rows = tm // len(others)
    for r, (s, o_ref) in enumerate(others):
        y = cw_ref[CONV_K - 1:CONV_K, :] * xp_scr[pl.ds(pad + r * rows, rows), :]
        for j in range(CONV_K - 1):
            y = y + cw_ref[j:j + 1, :] * xp_scr[pl.ds(pad + r * rows - (CONV_K - 1) + j, rows), :]
        zc_ref[pl.ds(r * rows, rows), :] = _silu(y)
        o_ref[...] = _dot(xn, w_ref[:, offs[s]:offs[s] + IN_SEGMENTS[s]])
    tail = xp_scr[tm:tm + pad, :]
    xp_scr[0:pad, :] = tail
    tail_ref[0] = tail


def in_proj_prompt(x, g, w_all, layer, conv_w, seq, tm):
    T, D = x.shape
    assert seq % tm == 0 and T % seq == 0
    wc = IN_SEGMENTS[2]
    outs = pl.pallas_call(
        functools.partial(_in_proj_prompt_kernel, tiles_per_seq=seq // tm),
        grid=(T // tm,),
        in_specs=[pl.BlockSpec((tm, D), lambda i: (i, 0)),
                  pl.BlockSpec((1, D), lambda i: (0, 0)),
                  _layer_weight(w_all, layer),
                  pl.BlockSpec((CONV_K, wc), lambda i: (0, 0))],
        out_specs=[pl.BlockSpec((tm, n), lambda i: (i, 0)) for n in IN_SEGMENTS]
                  + [pl.BlockSpec((1, SUBLANES, wc), lambda i: (i // (seq // tm), 0, 0))],
        out_shape=[jax.ShapeDtypeStruct((T, n), F32) for n in IN_SEGMENTS]
                  + [jax.ShapeDtypeStruct((T // seq, SUBLANES, wc), F32)],
        scratch_shapes=[pltpu.VMEM((tm + SUBLANES, wc), F32)],
        compiler_params=_cparams("arbitrary"),
    )(x, g.reshape(1, D), w_all, conv_w)
    return outs


def _matmul_residual_kernel(x_ref, *refs, ksplits):
    a_refs, w_ref, o_ref = refs[:-2], refs[-2], refs[-1]
    acc = x_ref[...]
    off = 0
    for a_ref, k in zip(a_refs, ksplits):
        acc = acc + _dot(a_ref[...].astype(BF16), w_ref[off:off + k, :])
        off += k
    o_ref[...] = acc


def matmul_residual(x, acts, w_all, layer, tm):
    T, D = x.shape
    ks = tuple(a.shape[1] for a in acts)
    K = w_all.shape[1]
    assert sum(ks) == K and T % tm == 0
    return pl.pallas_call(
        functools.partial(_matmul_residual_kernel, ksplits=ks),
        grid=(T // tm,),
        in_specs=[pl.BlockSpec((tm, D), lambda i: (i, 0))]
                 + [pl.BlockSpec((tm, k), lambda i: (i, 0)) for k in ks]
                 + [_layer_weight(w_all, layer)],
        out_specs=pl.BlockSpec((tm, D), lambda i: (i, 0)),
        out_shape=jax.ShapeDtypeStruct((T, D), F32),
        compiler_params=_cparams("parallel"),
    )(x, *acts, w_all)


def _ffn_kernel(x_ref, g_ref, wg_ref, wu_ref, wd_ref, gf_ref, o_ref, *, final_norm, tf):
    x = x_ref[...]
    h = _rms(x, g_ref[...]).astype(BF16)
    y = x
    for j in range(wg_ref.shape[1] // tf):
        cols = slice(j * tf, (j + 1) * tf)
        a = _dot(h, wg_ref[:, cols])
        b = _dot(h, wu_ref[:, cols])
        y = y + _dot((_silu(a) * b).astype(BF16), wd_ref[cols, :])
    if final_norm:
        y = _rms(y, gf_ref[...])
    o_ref[...] = y


def _resident(shape):
    return pl.BlockSpec(shape, lambda *_: (0,) * len(shape), pipeline_mode=pl.Buffered(1))


def ffn(x, g, wg, wu, wd, layer, g_final, final_norm, tm, tf):
    T, D = x.shape
    F = wg.shape[2]
    assert T % tm == 0 and F % tf == 0
    return pl.pallas_call(
        functools.partial(_ffn_kernel, final_norm=final_norm, tf=tf),
        grid=(T // tm,),
        in_specs=[pl.BlockSpec((tm, D), lambda i: (i, 0)),
                  _resident((1, D)), _layer_weight(wg, layer), _layer_weight(wu, layer), _layer_weight(wd, layer),
                  _resident((1, D))],
        out_specs=pl.BlockSpec((tm, D), lambda i: (i, 0)),
        out_shape=jax.ShapeDtypeStruct((T, D), F32),
        compiler_params=_cparams("parallel"),
    )(x, g.reshape(1, D), wg, wu, wd, g_final.reshape(1, D))


def _softmax_rows(s):
    e = jnp.exp(s - jnp.max(s, axis=-1, keepdims=True))
    return e / jnp.sum(e, axis=-1, keepdims=True)


def _xattn_prompt_kernel(x_ref, *refs, dh, n_acts):
    a_refs = refs[:n_acts]
    wmix_ref, g_ref, wq_ref, k_ref, v_ref, wo_ref, o_ref = refs[n_acts:]
    scale = dh ** -0.5
    x = x_ref[...]
    off = 0
    for a_ref in a_refs:
        x = x + _dot(a_ref[...].astype(BF16), wmix_ref[off:off + a_ref.shape[1], :])
        off += a_ref.shape[1]
    q = _dot(_rms(x, g_ref[...]).astype(BF16), wq_ref[...]).astype(BF16)
    sl = [slice(h * dh, (h + 1) * dh) for h in range(X_HEADS)]
    s = [_dot_nt(q[:, c], k_ref[0, :, c].astype(BF16)) * scale for c in sl]
    p = [_softmax_rows(sh).astype(BF16) for sh in s]
    heads = [_dot(ph, v_ref[0, :, c].astype(BF16)).astype(BF16) for ph, c in zip(p, sl)]
    o_ref[...] = x + _dot(jnp.concatenate(heads, axis=-1), wo_ref[...])


def xattn_prompt(x, acts, wmix, g, wq, mk, mv, wo, layer, seq, tq):
    T, D = x.shape
    _, B, M, _ = mk.shape
    nt = seq // tq
    rows = lambda w: pl.BlockSpec((tq, w), lambda b, t: (b * nt + t, 0))
    return pl.pallas_call(
        functools.partial(_xattn_prompt_kernel, dh=D // X_HEADS, n_acts=len(acts)),
        grid=(B, nt),
        in_specs=[rows(D)] + [rows(a.shape[1]) for a in acts]
                 + [_layer_weight(wmix, layer),
                    pl.BlockSpec((1, D), lambda b, t: (0, 0)),
                    _layer_weight(wq, layer),
                    pl.BlockSpec((None, 1, M, D), lambda b, t: (layer, b, 0, 0)),
                    pl.BlockSpec((None, 1, M, D), lambda b, t: (layer, b, 0, 0)),
                    _layer_weight(wo, layer)],
        out_specs=rows(D),
        out_shape=jax.ShapeDtypeStruct((T, D), F32),
        compiler_params=_cparams("parallel", "parallel"),
    )(x, *acts, wmix, g.reshape(1, D), wq, mk, mv, wo)


def _xattn_sample_kernel(q_ref, k_ref, v_ref, o_ref, *, sb):
    M, H, dh = k_ref.shape[1:]
    scale = dh ** -0.5
    row = lax.broadcasted_iota(jnp.int32, (SUBLANES, M * H), 0)
    col_head = lax.broadcasted_iota(jnp.int32, (SUBLANES, M * H), 1) % H
    own = (row % H) == col_head
    pad = jnp.zeros((SUBLANES - H, dh), F32)
    q8 = [jnp.concatenate([q_ref[i], pad], axis=0).astype(BF16) for i in range(sb)]
    s = [_dot_nt(q8[i], k_ref[i].reshape(M * H, dh).astype(BF16)) * scale for i in range(sb)]
    p = [_softmax_rows(jnp.where(own, si, -jnp.inf)).astype(BF16) for si in s]
    for i in range(sb):
        o_ref[i] = _dot(p[i], v_ref[i].reshape(M * H, dh).astype(BF16))[0:H]


def xattn_sample(q, ck, cv, layer, sb):
    B, D = q.shape
    _, _, M, H, dh = ck.shape
    out = pl.pallas_call(
        functools.partial(_xattn_sample_kernel, sb=sb),
        grid=(B // sb,),
        in_specs=[pl.BlockSpec((sb, H, dh), lambda i: (i, 0, 0)),
                  pl.BlockSpec((None, sb, M, H, dh), lambda i: (layer, i, 0, 0, 0)),
                  pl.BlockSpec((None, sb, M, H, dh), lambda i: (layer, i, 0, 0, 0))],
        out_specs=pl.BlockSpec((sb, H, dh), lambda i: (i, 0, 0)),
        out_shape=jax.ShapeDtypeStruct((B, H, dh), F32),
        compiler_params=_cparams("parallel"),
    )(q.reshape(B, H, dh), ck, cv)
    return out.reshape(B, D)


def _lane_cat_masks(L):
    row = lax.broadcasted_iota(jnp.int32, (L, HW), 0)
    j = lax.broadcasted_iota(jnp.int32, (L, HW), 1) % DH
    r2 = lax.broadcasted_iota(jnp.int32, (HW, HW), 0) // DH
    c2 = lax.broadcasted_iota(jnp.int32, (HW, HW), 1) // DH
    return row >= j, row > j, row == j, r2 == c2


def _expand_bd(x, bd):
    return jnp.where(bd, jnp.concatenate([x] * HEADS, axis=0), jnp.zeros((), x.dtype))


def _seg_reduce(x, op, fill):
    lo = lax.broadcasted_iota(jnp.int32, (x.shape[0], LANES), 1) < DH
    parts = []
    for c in range(HW // LANES):
        xh = x[:, c * LANES:(c + 1) * LANES]
        a = op(jnp.where(lo, xh, fill), axis=-1, keepdims=True)
        b = op(jnp.where(lo, fill, xh), axis=-1, keepdims=True)
        parts.append(jnp.where(lo, a, b))
    return jnp.concatenate(parts, axis=-1)


def _head_expander(first_lane):
    r = lax.broadcasted_iota(jnp.int32, (GATE_PAD, HW), 0)
    c = lax.broadcasted_iota(jnp.int32, (GATE_PAD, HW), 1) // DH
    return (r == c + first_lane).astype(BF16)


def _chunk_tril(tt):
    r = lax.broadcasted_iota(jnp.int32, (tt, tt), 0)
    c = lax.broadcasted_iota(jnp.int32, (tt, tt), 1)
    return ((r // CHUNK == c // CHUNK) & (r >= c)).astype(BF16)


def _split3(x):
    hi = x.astype(BF16)
    r = x - hi.astype(F32)
    mid = r.astype(BF16)
    return hi, mid, (r - mid.astype(F32)).astype(BF16)


def _dot_sel_r(x, sel):
    hi, mid, lo = _split3(x)
    return (_dot(lo, sel) + _dot(mid, sel)) + _dot(hi, sel)


def _dot_sel_l(sel, x):
    hi, mid, lo = _split3(x)
    return (_dot(sel, lo) + _dot(sel, mid)) + _dot(sel, hi)


def _col_to_row(x, eye):
    return jnp.sum(jnp.where(eye, x, 0.0), axis=0, keepdims=True)


def _head_rms(x, ones_bd, g_row):
    ms = _dot_sel_r(x * x, ones_bd) * (1.0 / DH)
    return x * lax.rsqrt(ms + EPS) * g_row


def _mlstm_prompt_kernel(za_ref, gt_ref, bias_ref, norm_ref, ha_ref, c_ref, n_ref, m_ref,
                         c_scr, n_scr, m_scr, ix_scr, bx_scr, *, nchunks, group):
    tb = pl.program_id(1)
    L = CHUNK
    tril, _, eye, bd = _lane_cat_masks(L)
    ones_bd = bd.astype(BF16)

    @pl.when(tb == 0)
    def _():
        c_scr[...] = jnp.zeros_like(c_scr)
        n_scr[...] = jnp.zeros_like(n_scr)
        m_scr[...] = jnp.zeros_like(m_scr)

    gt = gt_ref[...] + bias_ref[...]
    b_cols = _dot_sel_l(_chunk_tril(nchunks * L), _log_sigmoid(gt))
    ix_scr[...] = _dot_sel_r(gt, _head_expander(0))
    bx_scr[...] = _dot_sel_r(b_cols, _head_expander(HEADS))

    def chunks(gi, carry):
        c_bd, n_row, m_x = carry
        rows = [pl.ds(pl.multiple_of((gi * group + j) * L, L), L) for j in range(group)]
        q = [(za_ref[r, 0:HW] * (DH ** -0.5)).astype(BF16) for r in rows]
        k = [za_ref[r, HW:2 * HW] for r in rows]
        v = [za_ref[r, 2 * HW:3 * HW].astype(BF16) for r in rows]
        i_x = [ix_scr[r, :] for r in rows]
        b_x = [bx_scr[r, :] for r in rows]

        m_in, n_in, kw, decay = [], [], [], []
        for kk, ii, bb in zip(k, i_x, b_x):
            b_last = bb[L - 1:L, :]
            g_x = b_last - bb + ii
            m_new = jnp.maximum(b_last + m_x, jnp.max(g_x, axis=0, keepdims=True))
            kw.append(kk * jnp.exp(g_x - m_new))
            decay.append(jnp.exp(b_last + m_x - m_new))
            m_in.append(m_x)
            n_in.append(n_row)
            n_row = decay[-1] * n_row + jnp.sum(kw[-1], axis=0, keepdims=True)
            m_x = m_new

        d_intra = [jnp.where(tril, bb - _col_to_row(bb, eye) + _col_to_row(ii, eye), -jnp.inf) for bb, ii in zip(b_x, i_x)]
        d_inter = [bb + mm for bb, mm in zip(b_x, m_in)]
        m_t = [jnp.maximum(_seg_reduce(di, jnp.max, -jnp.inf), de) for di, de in zip(d_intra, d_inter)]
        w_inter = [jnp.exp(de - mt) for de, mt in zip(d_inter, m_t)]
        s = [_dot_nt(qq, _expand_bd(kk.astype(BF16), bd)) * jnp.exp(di - mt) for qq, kk, di, mt in zip(q, k, d_intra, m_t)]
        kv = [jnp.where(bd, _dot(kwj.T.astype(BF16), vv), 0.0) for kwj, vv in zip(kw, v)]
        c_in = []
        for dj, kvj in zip(decay, kv):
            c_in.append(c_bd)
            c_bd = dj * c_bd + kvj
        num = [wi * _dot(qq, cc.astype(BF16)) + _dot(ss.astype(BF16), _expand_bd(vv, bd))
               for wi, qq, cc, ss, vv in zip(w_inter, q, c_in, s, v)]
        den = [wi * _dot((qq.astype(F32) * nn).astype(BF16), ones_bd) + _seg_reduce(ss, jnp.sum, 0.0)
               for wi, qq, nn, ss in zip(w_inter, q, n_in, s)]
        for r, nu, de, mt in zip(rows, num, den, m_t):
            hh = nu / jnp.maximum(jnp.abs(de), jnp.exp(-mt))
            ha_ref[r, :] = _head_rms(hh, ones_bd, norm_ref[...]) * _sigmoid(za_ref[r, 3 * HW:4 * HW])
        return c_bd, n_row, m_x

    c_bd, n_row, m_x = lax.fori_loop(0, nchunks // group, chunks, (c_scr[...], n_scr[...], m_scr[...]))
    c_scr[...] = c_bd
    n_scr[...] = n_row
    m_scr[...] = m_x

    @pl.when(tb == pl.num_programs(1) - 1)
    def _():
        for h in range(HEADS):
            c_ref[0, h] = c_scr[h * DH:(h + 1) * DH, h * DH:(h + 1) * DH]
        n_ref[0] = n_scr[...]
        m_ref[0] = m_scr[...]


def mlstm_prompt(za, gates, b_i, b_f, norm, B, S, tt):
    assert S % tt == 0 and tt % CHUNK == 0
    nt = S // tt
    bias = jnp.zeros((1, GATE_PAD), F32).at[0, 0:HEADS].set(b_i).at[0, HEADS:2 * HEADS].set(b_f)
    ha, c, n, m = pl.pallas_call(
        functools.partial(_mlstm_prompt_kernel, nchunks=tt // CHUNK, group=math.gcd(tt // CHUNK, 4)),
        grid=(B, nt),
        in_specs=[pl.BlockSpec((tt, 4 * HW), lambda b, t: (b * nt + t, 0)),
                  pl.BlockSpec((tt, GATE_PAD), lambda b, t: (b * nt + t, 0)),
                  pl.BlockSpec((1, GATE_PAD), lambda b, t: (0, 0)),
                  pl.BlockSpec((1, HW), lambda b, t: (0, 0))],
        out_specs=[pl.BlockSpec((tt, HW), lambda b, t: (b * nt + t, 0)),
                   pl.BlockSpec((1, HEADS, DH, DH), lambda b, t: (b, 0, 0, 0)),
                   pl.BlockSpec((1, 1, HW), lambda b, t: (b, 0, 0)),
                   pl.BlockSpec((1, 1, HW), lambda b, t: (b, 0, 0))],
        out_shape=[jax.ShapeDtypeStruct((B * S, HW), F32),
                   jax.ShapeDtypeStruct((B, HEADS, DH, DH), F32),
                   jax.ShapeDtypeStruct((B, 1, HW), F32),
                   jax.ShapeDtypeStruct((B, 1, HW), F32)],
        scratch_shapes=[pltpu.VMEM((HW, HW), F32), pltpu.VMEM((1, HW), F32), pltpu.VMEM((1, HW), F32),
                        pltpu.VMEM((tt, HW), F32), pltpu.VMEM((tt, HW), F32)],
        compiler_params=_cparams("parallel", "arbitrary"),
    )(za, gates, bias, norm.reshape(1, HW))
    return ha, c, n.reshape(B, HEADS, DH), m[:, 0, ::DH]


def _gdn_prompt_kernel(zc_ref, zg_ref, gt_ref, alog_ref, dtb_ref, norm_ref,
                       hc_ref, s_ref,
                       s_scr, q_scr, k_scr, v_scr, beta_scr, g_scr, uv_scr, wq_scr, qkm_scr, kwt_scr,
                       *, nseq, nchunks, group):
    tb = pl.program_id(1)
    L = CHUNK
    tt = nchunks * L
    tril, strict, eye, bd = _lane_cat_masks(L)
    ones_bd = bd.astype(BF16)

    @pl.when(tb == 0)
    def _():
        s_scr[...] = jnp.zeros_like(s_scr)

    for i in range(nseq):
        sr = slice(i * tt, (i + 1) * tt)
        y = zc_ref[i]
        q_raw, k_raw = y[:, 0:HW], y[:, HW:2 * HW]
        q_scr[sr, :] = (q_raw * lax.rsqrt(_dot_sel_r(q_raw * q_raw, ones_bd) + EPS) * (DH ** -0.5)).astype(BF16)
        k_scr[sr, :] = k_raw * lax.rsqrt(_dot_sel_r(k_raw * k_raw, ones_bd) + EPS)
        v_scr[sr, :] = y[:, 2 * HW:3 * HW]
        gt = gt_ref[i]
        beta_scr[sr, :] = _dot_sel_r(_sigmoid(gt), _head_expander(2 * HEADS))
        la_cols = -jnp.exp(alog_ref[...]) * _softplus(gt + dtb_ref[...])
        g_scr[sr, :] = _dot_sel_r(_dot_sel_l(_chunk_tril(tt), la_cols), _head_expander(3 * HEADS))

    def prepare(gi, carry):
        cis = [gi * group + j for j in range(group)]
        rows = [pl.ds(pl.multiple_of(ci * L, L), L) for ci in cis]
        k = [k_scr[r, :] for r in rows]
        g_x = [g_scr[r, :] for r in rows]
        beta_row = [_col_to_row(beta_scr[r, :], eye) for r in rows]
        dec_incl = [jnp.where(tril, jnp.exp(jnp.where(tril, g - _col_to_row(g, eye), 0.0)), 0.0) for g in g_x]
        k_bd = [_expand_bd(kk.astype(BF16), bd) for kk in k]
        n0 = [-(jnp.where(strict, d, 0.0) * _dot_nt(kk.astype(BF16), kbd) * br)
              for d, kk, kbd, br in zip(dec_incl, k, k_bd, beta_row)]
        for r, d, kbd, br in zip(rows, dec_incl, k_bd, beta_row):
            qkm_scr[r, :] = (_dot_nt(q_scr[r, :], kbd) * d * br).astype(BF16)

        p = [_dot(n.astype(BF16), _expand_bd(n.astype(BF16), bd)) for n in n0]
        m = n0
        steps = int(math.log2(L)) - 1
        for i in range(steps):
            p_bd = [_expand_bd(pp.astype(BF16), bd) for pp in p]
            if i < steps - 1:
                pm = [_dot(jnp.concatenate([pp, mm], axis=0).astype(BF16), pbd) for pp, mm, pbd in zip(p, m, p_bd)]
                p_next, mp = [x[0:L] for x in pm], [x[L:2 * L] for x in pm]
            else:
                p_next, mp = None, [_dot(mm.astype(BF16), pbd) for mm, pbd in zip(m, p_bd)]
            m = [mm + pp + x for mm, pp, x in zip(m, p, mp)]
            p = p_next

        for ci, r, kk, g, mm in zip(cis, rows, k, g_x, m):
            v = v_scr[r, :]
            egk = jnp.exp(g) * kk
            rhs_bd = jnp.concatenate([_expand_bd(v.astype(BF16), bd), _expand_bd(egk.astype(BF16), bd)], axis=1)
            mr = _dot(mm.astype(BF16), rhs_bd)
            uv_scr[r, :] = v + mr[:, 0:HW]
            wq_rows = pl.multiple_of(ci * 2 * L, 2 * L)
            wq_scr[pl.ds(wq_rows, L), :] = (egk + mr[:, HW:2 * HW]).astype(BF16)
            wq_scr[pl.ds(wq_rows + L, L), :] = q_scr[r, :]
            kw = kk * (jnp.exp(g[L - 1:L, :] - g) * beta_scr[r, :])
            kwt_scr[pl.ds(pl.multiple_of(ci * HW, HW), HW), :] = kw.T.astype(BF16)
        return carry

    lax.fori_loop(0, nseq * nchunks // group, prepare, 0)

    def advance(c, states):
        cis = [i * nchunks + c for i in range(nseq)]
        rows = [pl.ds(pl.multiple_of(ci * L, L), L) for ci in cis]
        wqs = [_dot(wq_scr[pl.ds(pl.multiple_of(ci * 2 * L, 2 * L), 2 * L), :], s.astype(BF16))
               for ci, s in zip(cis, states)]
        ub = [(uv_scr[r, :] - x[0:L]).astype(BF16) for r, x in zip(rows, wqs)]
        new = [jnp.exp(g_scr[pl.ds(pl.multiple_of(ci * L + L - SUBLANES, SUBLANES), SUBLANES), :][SUBLANES - 1:, :]) * s
               + jnp.where(bd, _dot(kwt_scr[pl.ds(pl.multiple_of(ci * HW, HW), HW), :], u), 0.0)
               for ci, s, u in zip(cis, states, ub)]
        for i, (r, x, u) in enumerate(zip(rows, wqs, ub)):
            o = jnp.exp(g_scr[r, :]) * x[L:2 * L] + _dot(qkm_scr[r, :], _expand_bd(u, bd))
            hc_ref[i, pl.ds(pl.multiple_of(c * L, L), L), :] = (
                _head_rms(o, ones_bd, norm_ref[...]) * _silu(zg_ref[i, pl.ds(pl.multiple_of(c * L, L), L), :]))
        return tuple(new)

    states = lax.fori_loop(0, nchunks, advance, tuple(s_scr[i] for i in range(nseq)))
    for i in range(nseq):
        s_scr[i] = states[i]

    @pl.when(tb == pl.num_programs(1) - 1)
    def _():
        for i in range(nseq):
            for h in range(HEADS):
                s_ref[i, h] = s_scr[i, h * DH:(h + 1) * DH, h * DH:(h + 1) * DH]


def _gdn_gate_rows(a_log, dt_bias):
    z = jnp.zeros((1, GATE_PAD), F32)
    return (z.at[0, 3 * HEADS:4 * HEADS].set(a_log), z.at[0, 3 * HEADS:4 * HEADS].set(dt_bias))


def gdn_prompt(zc, zg, gates, a_log, dt_bias, norm, B, S, tt):
    assert S % tt == 0 and tt % CHUNK == 0
    nt = S // tt
    nseq = math.gcd(B, GDN_SEQS)
    rows = nseq * tt
    alog, dtb = _gdn_gate_rows(a_log, dt_bias)
    blk = lambda w: pl.BlockSpec((nseq, tt, w), lambda b, t: (b, t, 0))
    row = lambda w: pl.BlockSpec((1, w), lambda b, t: (0, 0))
    hc, s = pl.pallas_call(
        functools.partial(_gdn_prompt_kernel, nseq=nseq, nchunks=tt // CHUNK, group=math.gcd(rows // CHUNK, 8)),
        grid=(B // nseq, nt),
        in_specs=[blk(3 * HW), blk(HW), blk(GATE_PAD), row(GATE_PAD), row(GATE_PAD), row(HW)],
        out_specs=[blk(HW), pl.BlockSpec((nseq, HEADS, DH, DH), lambda b, t: (b, 0, 0, 0))],
        out_shape=[jax.ShapeDtypeStruct((B, S, HW), F32), jax.ShapeDtypeStruct((B, HEADS, DH, DH), F32)],
        scratch_shapes=[pltpu.VMEM((nseq, HW, HW), F32),
                        pltpu.VMEM((rows, HW), BF16), pltpu.VMEM((rows, HW), F32), pltpu.VMEM((rows, HW), F32),
                        pltpu.VMEM((rows, HW), F32), pltpu.VMEM((rows, HW), F32), pltpu.VMEM((rows, HW), F32),
                        pltpu.VMEM((2 * rows, HW), BF16), pltpu.VMEM((rows, HW), BF16),
                        pltpu.VMEM((rows // CHUNK * HW, CHUNK), BF16)],
        compiler_params=_cparams("parallel", "arbitrary"),
    )(zc.reshape(B, S, 3 * HW), zg.reshape(B, S, HW), gates.reshape(B, S, GATE_PAD), alog, dtb,
      jnp.tile(norm, HEADS).reshape(1, HW))
    return hc.reshape(B * S, HW), s


def _s5_prep_kernel(are_ref, aim_ref, ldt_ref, bre_ref, bim_ref, lre_ref, lim_ref, bbre_ref, bbim_ref):
    a_re, a_im = are_ref[...], aim_ref[...]
    dt = jnp.exp(ldt_ref[...])
    mag = jnp.exp(a_re * dt)
    lam_re, lam_im = mag * jnp.cos(a_im * dt), mag * jnp.sin(a_im * dt)
    lre_ref[...] = lam_re
    lim_ref[...] = lam_im
    nr, ni = lam_re - 1.0, lam_im
    den = a_re * a_re + a_im * a_im
    coef_re = (nr * a_re + ni * a_im) / den
    coef_im = (ni * a_re - nr * a_im) / den
    b_re, b_im = bre_ref[...], bim_ref[...]
    bbre_ref[...] = coef_re * b_re - coef_im * b_im
    bbim_ref[...] = coef_re * b_im + coef_im * b_re


def s5_params(lp):
    G, N, P = S5_GROUPS, S5_N, S5_P
    row = lambda a: a.astype(F32).reshape(1, G * N)
    to_pn = lambda b: jnp.transpose(b.astype(F32), (2, 0, 1)).reshape(P, G * N)
    shp = [jax.ShapeDtypeStruct((1, G * N), F32)] * 2 + [jax.ShapeDtypeStruct((P, G * N), F32)] * 2
    lam_re, lam_im, bb_re, bb_im = pl.pallas_call(_s5_prep_kernel, out_shape=shp)(
        row(lp['s5_a_re']), row(lp['s5_a_im']), row(jnp.repeat(lp['s5_log_dt'][:, None], N, axis=1)),
        to_pn(lp['s5_b_re']), to_pn(lp['s5_b_im']))
    eye = jnp.eye(S5_GB, dtype=F32)

    def w_in_blocks(bb):
        b4 = bb.reshape(P, S5_NBLK, S5_GB, N)
        return jnp.einsum('pbgn,gh->bgphn', b4, eye).reshape(S5_NBLK, S5_GB * P, S5_GB * N)

    def w_out_blocks(c):
        c4 = c.astype(F32).reshape(S5_NBLK, S5_GB, P, N)
        return jnp.einsum('bgpn,gh->bgnhp', c4, eye).reshape(S5_NBLK, S5_GB * N, S5_GB * P)

    w_in = jnp.concatenate([w_in_blocks(bb_re), w_in_blocks(bb_im)], axis=-1).astype(BF16)
    return {'lam_re': lam_re, 'lam_im': lam_im, 'w_in': w_in,
            'w_out_re': w_out_blocks(lp['s5_c_re']).astype(BF16),
            'w_out_im': (-w_out_blocks(lp['s5_c_im'])).astype(BF16),
            'd': lp['s5_d'].astype(F32).reshape(1, S5_WIDTH), 'w_glu': lp['s5_w_glu'].astype(BF16)}


def _s5_kernel(u_ref, h0r_ref, h0i_ref, lamr_ref, lami_ref, win_ref, wor_ref, woi_ref, d_ref, wglu_ref,
               ys_ref, h1r_ref, h1i_ref, hr_scr, hi_scr, br_scr, bi_scr, *, nseq, rows, bb, lane_blk):
    tb = pl.program_id(0)
    nsteps = nseq * rows // bb
    blk_in, blk_st = S5_GB * S5_P, S5_GB * S5_N

    @pl.when(tb == 0)
    def _():
        hr_scr[...] = h0r_ref[...]
        hi_scr[...] = h0i_ref[...]

    if nseq > 1:
        u = jnp.swapaxes(u_ref[...], 0, 1).reshape(nseq * rows, S5_WIDTH)
    else:
        u = u_ref[0]
    ub = u.astype(BF16)
    for blk in range(S5_NBLK):
        bu = _dot(ub[:, blk * blk_in:(blk + 1) * blk_in], win_ref[blk])
        br_scr[:, blk * blk_st:(blk + 1) * blk_st] = bu[:, 0:blk_st]
        bi_scr[:, blk * blk_st:(blk + 1) * blk_st] = bu[:, blk_st:2 * blk_st]

    for lb in range(S5_STATE // lane_blk):
        ls = slice(lb * lane_blk, (lb + 1) * lane_blk)
        lr = jnp.broadcast_to(lamr_ref[:, ls], (bb, lane_blk))
        li = jnp.broadcast_to(lami_ref[:, ls], (bb, lane_blk))

        def step(t, carry):
            hr, hi = carry
            r = pl.ds(pl.multiple_of(t * bb, bb), bb)
            nr = lr * hr - li * hi + br_scr[r, ls]
            ni = lr * hi + li * hr + bi_scr[r, ls]
            br_scr[r, ls] = nr
            bi_scr[r, ls] = ni
            return nr, ni

        hr, hi = lax.fori_loop(0, nsteps, step, (hr_scr[:, ls], hi_scr[:, ls]), unroll=min(nsteps, 8))
        hr_scr[:, ls] = hr
        hi_scr[:, ls] = hi

    ys = []
    for blk in range(S5_NBLK):
        st = slice(blk * blk_st, (blk + 1) * blk_st)
        ys.append(_dot(br_scr[:, st].astype(BF16), wor_ref[blk]) + _dot(bi_scr[:, st].astype(BF16), woi_ref[blk]))
    gy = jax.nn.gelu(jnp.concatenate(ys, axis=-1) + d_ref[...] * u)
    out = gy * _sigmoid(_dot(gy.astype(BF16), wglu_ref[...]))
    if nseq > 1:
        ys_ref[...] = jnp.swapaxes(out.reshape(rows, nseq, S5_WIDTH), 0, 1)
    else:
        ys_ref[0] = out

    @pl.when(tb == pl.num_programs(0) - 1)
    def _():
        h1r_ref[...] = hr_scr[...]
        h1i_ref[...] = hi_scr[...]


def s5_mixer(u, h0_re, h0_im, sp, tt, single_step):
    B = h0_re.shape[0]
    nseq, S = (1, 1) if single_step else (B, u.shape[1])
    rows = B if single_step else tt
    assert S % tt == 0 and B % SUBLANES == 0
    assert u.shape == ((1, B, S5_WIDTH) if single_step else (B, S, S5_WIDTH))
    lane_blk = max(LANES, min(S5_STATE, (SUBLANES * SUBLANES * LANES) // B))
    full = lambda shape: pl.BlockSpec(shape, lambda t: (0,) * len(shape))
    return pl.pallas_call(
        functools.partial(_s5_kernel, nseq=nseq, rows=rows, bb=B, lane_blk=lane_blk),
        grid=(S // tt,),
        in_specs=[pl.BlockSpec((nseq, rows, S5_WIDTH), lambda t: (0, t, 0)),
                  full((B, S5_STATE)), full((B, S5_STATE)), full((1, S5_STATE)), full((1, S5_STATE)),
                  full(sp['w_in'].shape), full(sp['w_out_re'].shape), full(sp['w_out_im'].shape),
                  full((1, S5_WIDTH)), full((S5_WIDTH, S5_WIDTH))],
        out_specs=[pl.BlockSpec((nseq, rows, S5_WIDTH), lambda t: (0, t, 0)),
                   full((B, S5_STATE)), full((B, S5_STATE))],
        out_shape=[jax.ShapeDtypeStruct(u.shape, F32),
                   jax.ShapeDtypeStruct((B, S5_STATE), F32), jax.ShapeDtypeStruct((B, S5_STATE), F32)],
        scratch_shapes=[pltpu.VMEM((B, S5_STATE), F32), pltpu.VMEM((B, S5_STATE), F32),
                        pltpu.VMEM((nseq * rows, S5_STATE), F32), pltpu.VMEM((nseq * rows, S5_STATE), F32)],
        compiler_params=_cparams("arbitrary"),
    )(u, h0_re, h0_im, sp['lam_re'], sp['lam_im'], sp['w_in'], sp['w_out_re'], sp['w_out_im'], sp['d'], sp['w_glu'])


def _ones_bd():
    r = lax.broadcasted_iota(jnp.int32, (HW, HW), 0) // DH
    c = lax.broadcasted_iota(jnp.int32, (HW, HW), 1) // DH
    return (r == c).astype(BF16)


def _mlstm_sample_kernel(za_ref, gt_ref, bias_ref, norm_ref, c_ref, n_ref, m_ref,
                         ha_ref, c_out, n_out, m_out, q_scr, kw_scr, h_scr):
    za = za_ref[...]
    q_scr[...] = (za[:, 0:HW] * (DH ** -0.5)).T
    k_t = za[:, HW:2 * HW].T
    v_t = za[:, 2 * HW:3 * HW].T
    g_t = (gt_ref[...] + bias_ref[...]).T
    m_out[...] = jnp.zeros_like(m_out)
    for h in range(HEADS):
        hs = slice(h * DH, (h + 1) * DH)
        i_h = g_t[h:h + 1, :]
        bm = _log_sigmoid(g_t[HEADS + h:HEADS + h + 1, :]) + m_ref[h:h + 1, :]
        m_t = jnp.maximum(i_h, bm)
        w_in = jnp.exp(i_h - m_t)
        w_st = jnp.exp(bm - m_t)
        q_h, k_h, v_h = q_scr[hs, :], k_t[hs, :], v_t[hs, :]
        s = jnp.sum(q_h * k_h, axis=0, keepdims=True) * w_in
        kw_scr[hs, :] = k_h * w_in

        def body(d, acc):
            r = h * DH + d
            rows = pl.ds(pl.multiple_of(r * DH, DH), DH)
            c_hd = c_ref[rows, :]
            c_out[rows, :] = w_st * c_hd + kw_scr[pl.ds(r, 1), :] * v_h
            return acc + q_scr[pl.ds(r, 1), :] * c_hd

        qc = lax.fori_loop(0, DH, body, jnp.zeros((DH, za.shape[0]), F32), unroll=4)
        n_h = n_ref[hs, :]
        num = w_st * qc + s * v_h
        den = w_st * jnp.sum(q_h * n_h, axis=0, keepdims=True) + s
        h_scr[hs, :] = num / jnp.maximum(jnp.abs(den), jnp.exp(-m_t))
        n_out[hs, :] = w_st * n_h + kw_scr[hs, :]
        m_out[h:h + 1, :] = m_t
    ha_ref[...] = _head_rms(h_scr[...].T, _ones_bd(), norm_ref[...]) * _sigmoid(za[:, 3 * HW:4 * HW])


def mlstm_sample(za, gates, b_i, b_f, norm, c_t, n_t, m_t):
    B = za.shape[0]
    bias = jnp.zeros((1, GATE_PAD), F32).at[0, 0:HEADS].set(b_i).at[0, HEADS:2 * HEADS].set(b_f)
    shp = lambda *s: jax.ShapeDtypeStruct(s, F32)
    return pl.pallas_call(
        _mlstm_sample_kernel,
        out_shape=[shp(B, HW), shp(HW * DH, B), shp(HW, B), shp(SUBLANES, B)],
        scratch_shapes=[pltpu.VMEM((HW, B), F32), pltpu.VMEM((HW, B), F32), pltpu.VMEM((HW, B), F32)],
        compiler_params=pltpu.CompilerParams(vmem_limit_bytes=VMEM_LIMIT),
    )(za, gates, bias, norm.reshape(1, HW), c_t, n_t, m_t)


def _gdn_sample_kernel(zc_ref, zg_ref, gt_ref, buf_ref, cw_ref, alog_ref, dtb_ref, norm_ref, s_ref,
                       hc_ref, s_out, buf_out, q_scr, k_scr, o_scr):
    W3 = 3 * HW
    x = zc_ref[...]
    y = cw_ref[CONV_K - 1:CONV_K, :] * x
    for j in range(CONV_K - 1):
        y = y + cw_ref[j:j + 1, :] * buf_ref[:, j * W3:(j + 1) * W3]
    buf_out[:, 0:(CONV_K - 2) * W3] = buf_ref[:, W3:(CONV_K - 1) * W3]
    buf_out[:, (CONV_K - 2) * W3:(CONV_K - 1) * W3] = x
    y = _silu(y)
    ones_bd = _ones_bd()
    q_raw, k_raw = y[:, 0:HW], y[:, HW:2 * HW]
    q_scr[...] = (q_raw * lax.rsqrt(_dot_sel_r(q_raw * q_raw, ones_bd) + EPS) * (DH ** -0.5)).T
    k_scr[...] = (k_raw * lax.rsqrt(_dot_sel_r(k_raw * k_raw, ones_bd) + EPS)).T
    v_t = y[:, 2 * HW:3 * HW].T
    gt = gt_ref[...]
    beta_t = _sigmoid(gt).T
    la_t = (-jnp.exp(alog_ref[...]) * _softplus(gt + dtb_ref[...])).T
    nb = x.shape[0]
    for h in range(HEADS):
        hs = slice(h * DH, (h + 1) * DH)
        beta = beta_t[2 * HEADS + h:2 * HEADS + h + 1, :]
        eg = jnp.exp(la_t[3 * HEADS + h:3 * HEADS + h + 1, :])
        q_h, k_h, v_h = q_scr[hs, :], k_scr[hs, :], v_t[hs, :]

        def read(d, acc):
            ks, qs = acc
            r = h * DH + d
            s_hd = s_ref[pl.ds(pl.multiple_of(r * DH, DH), DH), :]
            return ks + k_scr[pl.ds(r, 1), :] * s_hd, qs + q_scr[pl.ds(r, 1), :] * s_hd

        zero = jnp.zeros((DH, nb), F32)
        ks, qs = lax.fori_loop(0, DH, read, (zero, zero), unroll=4)
        u = v_h - eg * ks
        o_scr[hs, :] = eg * qs + (jnp.sum(q_h * k_h, axis=0, keepdims=True) * beta) * u

        def write(d, carry):
            r = h * DH + d
            rows = pl.ds(pl.multiple_of(r * DH, DH), DH)
            s_out[rows, :] = eg * s_ref[rows, :] + (beta * k_scr[pl.ds(r, 1), :]) * u
            return carry

        lax.fori_loop(0, DH, write, 0, unroll=4)
    hc_ref[...] = _head_rms(o_scr[...].T, ones_bd, norm_ref[...]) * _silu(zg_ref[...])


def gdn_sample(zc, zg, gates, buf, conv_w, a_log, dt_bias, norm, s_t):
    B = zc.shape[0]
    alog, dtb = _gdn_gate_rows(a_log, dt_bias)
    shp = lambda *s: jax.ShapeDtypeStruct(s, F32)
    return pl.pallas_call(
        _gdn_sample_kernel,
        out_shape=[shp(B, HW), shp(HW * DH, B), shp(B, (CONV_K - 1) * 3 * HW)],
        scratch_shapes=[pltpu.VMEM((HW, B), F32), pltpu.VMEM((HW, B), F32), pltpu.VMEM((HW, B), F32)],
        compiler_params=pltpu.CompilerParams(vmem_limit_bytes=VMEM_LIMIT),
    )(zc, zg, gates, buf, conv_w, alog, dtb, jnp.tile(norm, HEADS).reshape(1, HW), s_t)


FFN_CHUNK = 1408
IN_SEGMENTS = (4 * HW, S5_WIDTH, 3 * HW, HW, GATE_PAD)


def _tile(n, pref):
    return pref if n % pref == 0 else n


def prep_weights(p):
    w = p['w_in'].astype(BF16)
    a, g4 = 4 * HW, HEADS
    o_u = a + 2 * g4
    o_c = o_u + S5_WIDTH
    o_g = o_c + 3 * HW
    o_b = o_g + HW
    gate_cols = jnp.concatenate([w[:, :, a:a + 2 * g4], w[:, :, o_b:o_b + 2 * g4],
                                 jnp.zeros(w.shape[:2] + (GATE_PAD - 4 * g4,), w.dtype)], axis=2)
    w_in = jnp.concatenate([w[:, :, 0:a], w[:, :, o_u:o_c], w[:, :, o_c:o_g], w[:, :, o_g:o_b], gate_cols], axis=2)
    bf = lambda n: p[n].astype(BF16)
    return {'w_in': w_in, 'w_out': bf('w_out'), 'w_mq': bf('w_mq'), 'w_mo': bf('w_mo'), 'w_mk': bf('w_mk'),
            'w_mv': bf('w_mv'), 'w_gate': bf('w_gate'), 'w_up': bf('w_up'), 'w_down': bf('w_down')}


def mixer_prompt(x, lp, W, layer, B, S):
    T = B * S
    za, zu, zc, zg, gates, tail = in_proj_prompt(x, lp['norm_mix'], W['w_in'], layer, lp['gdn_conv_w'], S,
                                                 _tile(S, 512))
    buf1 = tail[:, SUBLANES - (CONV_K - 1):, :]
    tt = _tile(S, 512)
    ha, c1, n1, m1 = mlstm_prompt(za, gates, lp['mlstm_b_i'], lp['mlstm_b_f'], lp['mlstm_norm'], B, S, tt)
    h0 = jnp.zeros((B, S5_STATE), F32)
    ys3, r1, i1 = s5_mixer(zu.reshape(B, S, S5_WIDTH), h0, h0, lp['s5'], _tile(S, 128), False)
    ys = ys3.reshape(T, S5_WIDTH)
    hc, s1 = gdn_prompt(zc, zg, gates, lp['gdn_a_log'], lp['gdn_dt_bias'], lp['gdn_norm'], B, S, _tile(S, 256))
    return [ha, ys, hc], (c1, n1, m1, r1.reshape(B, S5_GROUPS, S5_N), i1.reshape(B, S5_GROUPS, S5_N), s1, buf1)


def mixer_sample(x, st, lp, W, layer):
    B = x.shape[0]
    c0, n0, m0, r0, i0, s0, buf0 = st
    za, zu, zc, zg, gates = norm_matmul(x, lp['norm_mix'], W['w_in'], layer, IN_SEGMENTS, B)
    m_t = jnp.zeros((SUBLANES, B), F32).at[0:HEADS, :].set(m0.T)
    ha, c1t, n1t, m1t = mlstm_sample(za, gates, lp['mlstm_b_i'], lp['mlstm_b_f'], lp['mlstm_norm'],
                                     c0.reshape(B, HW * DH).T, n0.reshape(B, HW).T, m_t)
    ys3, r1, i1 = s5_mixer(zu.reshape(1, B, S5_WIDTH), r0.reshape(B, S5_STATE), i0.reshape(B, S5_STATE), lp['s5'], 1, True)
    hc, s1t, buf1 = gdn_sample(zc, zg, gates, buf0.reshape(B, (CONV_K - 1) * 3 * HW), lp['gdn_conv_w'],
                               lp['gdn_a_log'], lp['gdn_dt_bias'], lp['gdn_norm'], s0.reshape(B, HW * DH).T)
    x1 = matmul_residual(x, [ha, ys3.reshape(B, S5_WIDTH), hc], W['w_out'], layer, B)
    return x1, (c1t.T.reshape(B, HEADS, DH, DH), n1t.T.reshape(B, HEADS, DH), m1t[0:HEADS, :].T,
                r1.reshape(B, S5_GROUPS, S5_N), i1.reshape(B, S5_GROUPS, S5_N),
                s1t.T.reshape(B, HEADS, DH, DH), buf1.reshape(B, CONV_K - 1, 3 * HW))


def _mem_kv_kernel(x_ref, g_ref, wk_ref, wv_ref, k2_ref, v2_ref, k5_ref, v5_ref):
    xn = _rms(x_ref[...], g_ref[...]).astype(BF16)
    for w_ref, o2_ref, o5_ref in ((wk_ref, k2_ref, k5_ref), (wv_ref, v2_ref, v5_ref)):
        y = _dot(xn, w_ref[...])
        o2_ref[...] = y
        o5_ref[...] = y.reshape(o5_ref.shape)


def mem_kv(mem, norm_mem, wk, wv):
    B, M, D = mem.shape
    L = wk.shape[0]
    dh = D // X_HEADS
    w_spec = pl.BlockSpec((None, D, D), lambda l, b: (l, 0, 0))
    o2_spec = pl.BlockSpec((None, None, M, D), lambda l, b: (l, b, 0, 0))
    o5_spec = pl.BlockSpec((None, None, M, X_HEADS, dh), lambda l, b: (l, b, 0, 0, 0))
    return pl.pallas_call(
        _mem_kv_kernel,
        grid=(L, B),
        in_specs=[pl.BlockSpec((None, M, D), lambda l, b: (b, 0, 0)),
                  pl.BlockSpec((None, 1, D), lambda l, b: (l, 0, 0)), w_spec, w_spec],
        out_specs=[o2_spec, o2_spec, o5_spec, o5_spec],
        out_shape=[jax.ShapeDtypeStruct((L, B, M, D), F32)] * 2
                  + [jax.ShapeDtypeStruct((L, B, M, X_HEADS, dh), F32)] * 2,
        compiler_params=_cparams("parallel", "parallel"),
    )(mem, norm_mem.reshape(L, 1, D), wk, wv)


def xattn_ffn_prompt(x, acts, mk, mv, lp, W, layer, S, norm_final, final):
    T, D = x.shape
    x2 = xattn_prompt(x, acts, W['w_out'], lp['norm_xattn'], W['w_mq'], mk, mv, W['w_mo'], layer, S, _tile(S, 512))
    return ffn(x2, lp['norm_ffn'], W['w_gate'], W['w_up'], W['w_down'], layer, norm_final, final, _tile(T, 512),
               FFN_CHUNK)


def xattn_ffn_sample(x, ck, cv, lp, W, layer, norm_final, final):
    B, D = x.shape
    (q,) = norm_matmul(x, lp['norm_xattn'], W['w_mq'], layer, (D,), B)
    o = xattn_sample(q, ck, cv, layer, 4)
    x2 = matmul_residual(x, [o], W['w_mo'], layer, B)
    return ffn(x2, lp['norm_ffn'], W['w_gate'], W['w_up'], W['w_down'], layer, norm_final, final, B, FFN_CHUNK)


LAYER_PARAMS = ('norm_mix', 'w_in', 'w_out', 'mlstm_b_i', 'mlstm_b_f', 'mlstm_norm', 's5_a_re', 's5_a_im', 's5_log_dt',
                's5_b_re', 's5_b_im', 's5_c_re', 's5_c_im', 's5_d', 's5_w_glu', 'gdn_conv_w', 'gdn_a_log',
                'gdn_dt_bias', 'gdn_norm', 'norm_xattn', 'norm_mem', 'w_mq', 'w_mk', 'w_mv', 'w_mo', 'norm_ffn',
                'w_gate', 'w_up', 'w_down')


def kernel(x_prompt, x_sample, mem_prompt, cache_mem_k, cache_mem_v, state_mlstm_c, state_mlstm_n, state_mlstm_m, state_s5_re, state_s5_im, state_gdn, state_gdn_conv, norm_mix, w_in, w_out, mlstm_b_i, mlstm_b_f, mlstm_norm, s5_a_re, s5_a_im, s5_log_dt, s5_b_re, s5_b_im, s5_c_re, s5_c_im, s5_d, s5_w_glu, gdn_conv_w, gdn_a_log, gdn_dt_bias, gdn_norm, norm_xattn, norm_mem, w_mq, w_mk, w_mv, w_mo, norm_ffn, w_gate, w_up, w_down, norm_final):
    stacked = dict(norm_mix=norm_mix, w_in=w_in, w_out=w_out, mlstm_b_i=mlstm_b_i, mlstm_b_f=mlstm_b_f,
                   mlstm_norm=mlstm_norm, s5_a_re=s5_a_re, s5_a_im=s5_a_im, s5_log_dt=s5_log_dt, s5_b_re=s5_b_re,
                   s5_b_im=s5_b_im, s5_c_re=s5_c_re, s5_c_im=s5_c_im, s5_d=s5_d, s5_w_glu=s5_w_glu,
                   gdn_conv_w=gdn_conv_w, gdn_a_log=gdn_a_log, gdn_dt_bias=gdn_dt_bias, gdn_norm=gdn_norm,
                   norm_xattn=norm_xattn, norm_mem=norm_mem, w_mq=w_mq, w_mk=w_mk, w_mv=w_mv, w_mo=w_mo,
                   norm_ffn=norm_ffn, w_gate=w_gate, w_up=w_up, w_down=w_down)
    B, S, D = x_prompt.shape
    Bs = x_sample.shape[0]
    M = mem_prompt.shape[1]
    depth = w_in.shape[0]
    xp = x_prompt.reshape(B * S, D)
    xs = x_sample.reshape(Bs, D)
    cache_k, cache_v = cache_mem_k, cache_mem_v
    st_p, st_s = [], []
    W = prep_weights(stacked)
    mk, mv, mem_k, mem_v = mem_kv(mem_prompt, norm_mem, W['w_mk'], W['w_mv'])
    for l in range(depth):
        lp = {n: stacked[n][l] for n in LAYER_PARAMS if n not in W}
        lp['s5'] = s5_params(lp)
        last = l == depth - 1
        acts, sp = mixer_prompt(xp, lp, W, l, B, S)
        xp = xattn_ffn_prompt(xp, acts, mk, mv, lp, W, l, S, norm_final, last)
        st_in = (state_mlstm_c[l], state_mlstm_n[l], state_mlstm_m[l], state_s5_re[l], state_s5_im[l],
                 state_gdn[l], state_gdn_conv[l])
        xs, ss = mixer_sample(xs, st_in, lp, W, l)
        xs = xattn_ffn_sample(xs, cache_k, cache_v, lp, W, l, norm_final, last)
        st_p.append(sp)
        st_s.append(ss)
    stack = lambda lst: [jnp.stack([st[i] for st in lst]) for i in range(7)]
    return (xp.reshape(B, S, D), xs.reshape(Bs, 1, D), mem_k, mem_v,
            *stack(st_p), *stack(st_s))
```

```python
import functools
import math

import jax
import jax.numpy as jnp
from jax import lax
from jax.experimental import pallas as pl
from jax.experimental.pallas import tpu as pltpu

F32 = jnp.float32
BF16 = jnp.bfloat16
EPS = 1e-6

HEADS = 4
DH = 64
HW = HEADS * DH
CHUNK = 64
S5_P = 16
S5_N = 64
S5_GROUPS = 32
S5_WIDTH = S5_GROUPS * S5_P
S5_STATE = S5_GROUPS * S5_N
S5_GB = 8
S5_NBLK = S5_GROUPS // S5_GB
CONV_K = 4
GDN_SEQS = 4
X_HEADS = 4
GATE_PAD = 128
LANES = 128
SUBLANES = 8
VMEM_LIMIT = 48 * 1024 * 1024


def _cparams(*sem):
    return pltpu.CompilerParams(dimension_semantics=sem, vmem_limit_bytes=VMEM_LIMIT)


def _rms(x, g_row):
    return x * lax.rsqrt(jnp.mean(x * x, axis=-1, keepdims=True) + EPS) * g_row


def _dot(a, b):
    return jnp.dot(a, b, preferred_element_type=F32)


def _dot_nt(a, b):
    return lax.dot_general(a, b, (((1,), (1,)), ((), ())), preferred_element_type=F32)


def _sigmoid(x):
    return 1.0 / (1.0 + jnp.exp(-x))


def _silu(x):
    return x * _sigmoid(x)


def _softplus(x):
    return jnp.maximum(x, 0.0) + jnp.log1p(jnp.exp(-jnp.abs(x)))


def _log_sigmoid(x):
    return jnp.minimum(x, 0.0) - jnp.log1p(jnp.exp(-jnp.abs(x)))


def _norm_matmul_kernel(x_ref, g_ref, w_ref, *out_refs, splits):
    xn = _rms(x_ref[...], g_ref[...]).astype(BF16)
    off = 0
    for o_ref, n in zip(out_refs, splits):
        o_ref[...] = _dot(xn, w_ref[:, off:off + n])
        off += n


def _layer_weight(w_all, layer):
    return pl.BlockSpec((None,) + w_all.shape[1:], lambda *_: (layer, 0, 0), pipeline_mode=pl.Buffered(1))


def norm_matmul(x, g, w_all, layer, splits, tm):
    T, D = x.shape
    N = w_all.shape[2]
    assert sum(splits) == N and T % tm == 0
    return pl.pallas_call(
        functools.partial(_norm_matmul_kernel, splits=tuple(splits)),
        grid=(T // tm,),
        in_specs=[pl.BlockSpec((tm, D), lambda i: (i, 0)),
                  pl.BlockSpec((1, D), lambda i: (0, 0)),
                  _layer_weight(w_all, layer)],
        out_specs=[pl.BlockSpec((tm, n), lambda i: (i, 0)) for n in splits],
        out_shape=[jax.ShapeDtypeStruct((T, n), F32) for n in splits],
        compiler_params=_cparams("parallel"),
    )(x, g.reshape(1, D), w_all)


def _in_proj_prompt_kernel(x_ref, g_ref, w_ref, cw_ref, za_ref, zu_ref, zc_ref, zg_ref, gt_ref, tail_ref, xp_scr,
                           *, tiles_per_seq):
    tm = x_ref.shape[0]
    pad = SUBLANES
    xn = _rms(x_ref[...], g_ref[...]).astype(BF16)
    offs = [sum(IN_SEGMENTS[:s]) for s in range(len(IN_SEGMENTS))]

    @pl.when(pl.program_id(0) % tiles_per_seq == 0)
    def _():
        xp_scr[0:pad, :] = jnp.zeros((pad, xp_scr.shape[1]), F32)

    xp_scr[pad:pad + tm, :] = _dot(xn, w_ref[:, offs[2]:offs[2] + IN_SEGMENTS[2]])
    others = ((0, za_ref), (1, zu_ref), (3, zg_ref), (4, gt_ref))
    rows = tm // len(others)
    for r, (s, o_ref) in enumerate(others):
        y = cw_ref[CONV_K - 1:CONV_K, :] * xp_scr[pl.ds(pad + r * rows, rows), :]
        for j in range(CONV_K - 1):
            y = y + cw_ref[j:j + 1, :] * xp_scr[pl.ds(pad + r * rows - (CONV_K - 1) + j, rows), :]
        zc_ref[pl.ds(r * rows, rows), :] = _silu(y)
        o_ref[...] = _dot(xn, w_ref[:, offs[s]:offs[s] + IN_SEGMENTS[s]])
    tail = xp_scr[tm:tm + pad, :]
    xp_scr[0:pad, :] = tail
    tail_ref[0] = tail


def in_proj_prompt(x, g, w_all, layer, conv_w, seq, tm):
    T, D = x.shape
    assert seq % tm == 0 and T % seq == 0
    wc = IN_SEGMENTS[2]
    outs = pl.pallas_call(
        functools.partial(_in_proj_prompt_kernel, tiles_per_seq=seq // tm),
        grid=(T // tm,),
        in_specs=[pl.BlockSpec((tm, D), lambda i: (i, 0)),
                  pl.BlockSpec((1, D), lambda i: (0, 0)),
                  _layer_weight(w_all, layer),
                  pl.BlockSpec((CONV_K, wc), lambda i: (0, 0))],
        out_specs=[pl.BlockSpec((tm, n), lambda i: (i, 0)) for n in IN_SEGMENTS]
                  + [pl.BlockSpec((1, SUBLANES, wc), lambda i: (i // (seq // tm), 0, 0))],
        out_shape=[jax.ShapeDtypeStruct((T, n), F32) for n in IN_SEGMENTS]
                  + [jax.ShapeDtypeStruct((T // seq, SUBLANES, wc), F32)],
        scratch_shapes=[pltpu.VMEM((tm + SUBLANES, wc), F32)],
        compiler_params=_cparams("arbitrary"),
    )(x, g.reshape(1, D), w_all, conv_w)
    return outs


def _matmul_residual_kernel(x_ref, *refs, ksplits):
    a_refs, w_ref, o_ref = refs[:-2], refs[-2], refs[-1]
    acc = x_ref[...]
    off = 0
    for a_ref, k in zip(a_refs, ksplits):
        acc = acc + _dot(a_ref[...].astype(BF16), w_ref[off:off + k, :])
        off += k
    o_ref[...] = acc


def matmul_residual(x, acts, w_all, layer, tm):
    T, D = x.shape
    ks = tuple(a.shape[1] for a in acts)
    K = w_all.shape[1]
    assert sum(ks) == K and T % tm == 0
    return pl.pallas_call(
        functools.partial(_matmul_residual_kernel, ksplits=ks),
        grid=(T // tm,),
        in_specs=[pl.BlockSpec((tm, D), lambda i: (i, 0))]
                 + [pl.BlockSpec((tm, k), lambda i: (i, 0)) for k in ks]
                 + [_layer_weight(w_all, layer)],
        out_specs=pl.BlockSpec((tm, D), lambda i: (i, 0)),
        out_shape=jax.ShapeDtypeStruct((T, D), F32),
        compiler_params=_cparams("parallel"),
    )(x, *acts, w_all)


def _ffn_kernel(x_ref, g_ref, wg_ref, wu_ref, wd_ref, gf_ref, o_ref, *, final_norm, tf):
    x = x_ref[...]
    h = _rms(x, g_ref[...]).astype(BF16)
    y = x
    for j in range(wg_ref.shape[1] // tf):
        cols = slice(j * tf, (j + 1) * tf)
        a = _dot(h, wg_ref[:, cols])
        b = _dot(h, wu_ref[:, cols])
        y = y + _dot((_silu(a) * b).astype(BF16), wd_ref[cols, :])
    if final_norm:
        y = _rms(y, gf_ref[...])
    o_ref[...] = y


def _resident(shape):
    return pl.BlockSpec(shape, lambda *_: (0,) * len(shape), pipeline_mode=pl.Buffered(1))


def ffn(x, g, wg, wu, wd, layer, g_final, final_norm, tm, tf):
    T, D = x.shape
    F = wg.shape[2]
    assert T % tm == 0 and F % tf == 0
    return pl.pallas_call(
        functools.partial(_ffn_kernel, final_norm=final_norm, tf=tf),
        grid=(T // tm,),
        in_specs=[pl.BlockSpec((tm, D), lambda i: (i, 0)),
                  _resident((1, D)), _layer_weight(wg, layer), _layer_weight(wu, layer), _layer_weight(wd, layer),
                  _resident((1, D))],
        out_specs=pl.BlockSpec((tm, D), lambda i: (i, 0)),
        out_shape=jax.ShapeDtypeStruct((T, D), F32),
        compiler_params=_cparams("parallel"),
    )(x, g.reshape(1, D), wg, wu, wd, g_final.reshape(1, D))


def _softmax_rows(s):
    e = jnp.exp(s - jnp.max(s, axis=-1, keepdims=True))
    return e / jnp.sum(e, axis=-1, keepdims=True)


def _xattn_prompt_kernel(x_ref, *refs, dh, n_acts):
    a_refs = refs[:n_acts]
    wmix_ref, g_ref, wq_ref, k_ref, v_ref, wo_ref, o_ref = refs[n_acts:]
    scale = dh ** -0.5
    x = x_ref[...]
    off = 0
    for a_ref in a_refs:
        x = x + _dot(a_ref[...].astype(BF16), wmix_ref[off:off + a_ref.shape[1], :])
        off += a_ref.shape[1]
    q = _dot(_rms(x, g_ref[...]).astype(BF16), wq_ref[...]).astype(BF16)
    sl = [slice(h * dh, (h + 1) * dh) for h in range(X_HEADS)]
    s = [_dot_nt(q[:, c], k_ref[0, :, c].astype(BF16)) * scale for c in sl]
    p = [_softmax_rows(sh).astype(BF16) for sh in s]
    heads = [_dot(ph, v_ref[0, :, c].astype(BF16)).astype(BF16) for ph, c in zip(p, sl)]
    o_ref[...] = x + _dot(jnp.concatenate(heads, axis=-1), wo_ref[...])


def xattn_prompt(x, acts, wmix, g, wq, mk, mv, wo, layer, seq, tq):
    T, D = x.shape
    _, B, M, _ = mk.shape
    nt = seq // tq
    rows = lambda w: pl.BlockSpec((tq, w), lambda b, t: (b * nt + t, 0))
    return pl.pallas_call(
        functools.partial(_xattn_prompt_kernel, dh=D // X_HEADS, n_acts=len(acts)),
        grid=(B, nt),
        in_specs=[rows(D)] + [rows(a.shape[1]) for a in acts]
                 + [_layer_weight(wmix, layer),
                    pl.BlockSpec((1, D), lambda b, t: (0, 0)),
                    _layer_weight(wq, layer),
                    pl.BlockSpec((None, 1, M, D), lambda b, t: (layer, b, 0, 0)),
                    pl.BlockSpec((None, 1, M, D), lambda b, t: (layer, b, 0, 0)),
                    _layer_weight(wo, layer)],
        out_specs=rows(D),
        out_shape=jax.ShapeDtypeStruct((T, D), F32),
        compiler_params=_cparams("parallel", "parallel"),
    )(x, *acts, wmix, g.reshape(1, D), wq, mk, mv, wo)


def _xattn_sample_kernel(q_ref, k_ref, v_ref, o_ref, *, sb):
    M, H, dh = k_ref.shape[1:]
    scale = dh ** -0.5
    row = lax.broadcasted_iota(jnp.int32, (SUBLANES, M * H), 0)
    col_head = lax.broadcasted_iota(jnp.int32, (SUBLANES, M * H), 1) % H
    own = (row % H) == col_head
    pad = jnp.zeros((SUBLANES - H, dh), F32)
    q8 = [jnp.concatenate([q_ref[i], pad], axis=0).astype(BF16) for i in range(sb)]
    s = [_dot_nt(q8[i], k_ref[i].reshape(M * H, dh).astype(BF16)) * scale for i in range(sb)]
    p = [_softmax_rows(jnp.where(own, si, -jnp.inf)).astype(BF16) for si in s]
    for i in range(sb):
        o_ref[i] = _dot(p[i], v_ref[i].reshape(M * H, dh).astype(BF16))[0:H]


def xattn_sample(q, ck, cv, layer, sb):
    B, D = q.shape
    _, _, M, H, dh = ck.shape
    out = pl.pallas_call(
        functools.partial(_xattn_sample_kernel, sb=sb),
        grid=(B // sb,),
        in_specs=[pl.BlockSpec((sb, H, dh), lambda i: (i, 0, 0)),
                  pl.BlockSpec((None, sb, M, H, dh), lambda i: (layer, i, 0, 0, 0)),
                  pl.BlockSpec((None, sb, M, H, dh), lambda i: (layer, i, 0, 0, 0))],
        out_specs=pl.BlockSpec((sb, H, dh), lambda i: (i, 0, 0)),
        out_shape=jax.ShapeDtypeStruct((B, H, dh), F32),
        compiler_params=_cparams("parallel"),
    )(q.reshape(B, H, dh), ck, cv)
    return out.reshape(B, D)


def _lane_cat_masks(L):
    row = lax.broadcasted_iota(jnp.int32, (L, HW), 0)
    j = lax.broadcasted_iota(jnp.int32, (L, HW), 1) % DH
    r2 = lax.broadcasted_iota(jnp.int32, (HW, HW), 0) // DH
    c2 = lax.broadcasted_iota(jnp.int32, (HW, HW), 1) // DH
    return row >= j, row > j, row == j, r2 == c2


def _expand_bd(x, bd):
    return jnp.where(bd, jnp.concatenate([x] * HEADS, axis=0), jnp.zeros((), x.dtype))


def _seg_reduce(x, op, fill):
    lo = lax.broadcasted_iota(jnp.int32, (x.shape[0], LANES), 1) < DH
    parts = []
    for c in range(HW // LANES):
        xh = x[:, c * LANES:(c + 1) * LANES]
        a = op(jnp.where(lo, xh, fill), axis=-1, keepdims=True)
        b = op(jnp.where(lo, fill, xh), axis=-1, keepdims=True)
        parts.append(jnp.where(lo, a, b))
    return jnp.concatenate(parts, axis=-1)


def _head_expander(first_lane):
    r = lax.broadcasted_iota(jnp.int32, (GATE_PAD, HW), 0)
    c = lax.broadcasted_iota(jnp.int32, (GATE_PAD, HW), 1) // DH
    return (r == c + first_lane).astype(BF16)


def _chunk_cumsum(x):
    tt, w = x.shape
    g = SUBLANES
    x3 = x.reshape(tt // g, g, w)
    sub = lax.broadcasted_iota(jnp.int32, x3.shape, 1)
    s = 1
    while s < g:
        x3 = x3 + jnp.where(sub >= s, pltpu.roll(x3, s, 1), 0.0)
        s *= 2
    per = CHUNK // g
    x4 = x3.reshape(tt // CHUNK, per, g, w)
    acc = jnp.zeros((tt // CHUNK, 1, 1, w), F32)
    parts = []
    for i in range(per):
        parts.append(x4[:, i:i + 1] + acc)
        acc = acc + x4[:, i:i + 1, g - 1:g, :]
    return jnp.concatenate(parts, axis=1).reshape(tt, w)


def _split3(x):
    hi = x.astype(BF16)
    r = x - hi.astype(F32)
    mid = r.astype(BF16)
    return hi, mid, (r - mid.astype(F32)).astype(BF16)


def _dot_sel_r(x, sel):
    hi, mid, lo = _split3(x)
    return (_dot(lo, sel) + _dot(mid, sel)) + _dot(hi, sel)


def _col_to_row(x, eye):
    return jnp.sum(jnp.where(eye, x, 0.0), axis=0, keepdims=True)


def _head_rms(x, ones_bd, g_row):
    ms = _dot_sel_r(x * x, ones_bd) * (1.0 / DH)
    return x * lax.rsqrt(ms + EPS) * g_row


def _mlstm_prompt_kernel(za_ref, gt_ref, bias_ref, norm_ref, ha_ref, c_ref, n_ref, m_ref,
                         c_scr, n_scr, m_scr, ix_scr, bx_scr, *, nchunks, group):
    tb = pl.program_id(1)
    L = CHUNK
    tril, _, eye, bd = _lane_cat_masks(L)
    ones_bd = bd.astype(BF16)

    @pl.when(tb == 0)
    def _():
        c_scr[...] = jnp.zeros_like(c_scr)
        n_scr[...] = jnp.zeros_like(n_scr)
        m_scr[...] = jnp.zeros_like(m_scr)

    gt = gt_ref[...] + bias_ref[...]
    b_cols = _chunk_cumsum(_log_sigmoid(gt))
    ix_scr[...] = _dot_sel_r(gt, _head_expander(0))
    bx_scr[...] = _dot_sel_r(b_cols, _head_expander(HEADS))

    def chunks(gi, carry):
        c_bd, n_row, m_x = carry
        rows = [pl.ds(pl.multiple_of((gi * group + j) * L, L), L) for j in range(group)]
        q = [(za_ref[r, 0:HW] * (DH ** -0.5)).astype(BF16) for r in rows]
        k = [za_ref[r, HW:2 * HW] for r in rows]
        v = [za_ref[r, 2 * HW:3 * HW].astype(BF16) for r in rows]
        i_x = [ix_scr[r, :] for r in rows]
        b_x = [bx_scr[r, :] for r in rows]

        m_in, n_in, kw, decay = [], [], [], []
        for kk, ii, bb in zip(k, i_x, b_x):
            b_last = bb[L - 1:L, :]
            g_x = b_last - bb + ii
            m_new = jnp.maximum(b_last + m_x, jnp.max(g_x, axis=0, keepdims=True))
            kw.append(kk * jnp.exp(g_x - m_new))
            decay.append(jnp.exp(b_last + m_x - m_new))
            m_in.append(m_x)
            n_in.append(n_row)
            n_row = decay[-1] * n_row + jnp.sum(kw[-1], axis=0, keepdims=True)
            m_x = m_new

        d_intra = [jnp.where(tril, bb - _col_to_row(bb, eye) + _col_to_row(ii, eye), -jnp.inf) for bb, ii in zip(b_x, i_x)]
        d_inter = [bb + mm for bb, mm in zip(b_x, m_in)]
        m_t = [jnp.maximum(_seg_reduce(di, jnp.max, -jnp.inf), de) for di, de in zip(d_intra, d_inter)]
        w_inter = [jnp.exp(de - mt) for de, mt in zip(d_inter, m_t)]
        s = [_dot_nt(qq, _expand_bd(kk.astype(BF16), bd)) * jnp.exp(di - mt) for qq, kk, di, mt in zip(q, k, d_intra, m_t)]
        kv = [jnp.where(bd, _dot(kwj.T.astype(BF16), vv), 0.0) for kwj, vv in zip(kw, v)]
        c_in = []
        for dj, kvj in zip(decay, kv):
            c_in.append(c_bd)
            c_bd = dj * c_bd + kvj
        num = [wi * _dot(qq, cc.astype(BF16)) + _dot(ss.astype(BF16), _expand_bd(vv, bd))
               for wi, qq, cc, ss, vv in zip(w_inter, q, c_in, s, v)]
        den = [wi * _dot((qq.astype(F32) * nn).astype(BF16), ones_bd) + _seg_reduce(ss, jnp.sum, 0.0)
               for wi, qq, nn, ss in zip(w_inter, q, n_in, s)]
        for r, nu, de, mt in zip(rows, num, den, m_t):
            hh = nu / jnp.maximum(jnp.abs(de), jnp.exp(-mt))
            ha_ref[r, :] = _head_rms(hh, ones_bd, norm_ref[...]) * _sigmoid(za_ref[r, 3 * HW:4 * HW])
        return c_bd, n_row, m_x

    c_bd, n_row, m_x = lax.fori_loop(0, nchunks // group, chunks, (c_scr[...], n_scr[...], m_scr[...]))
    c_scr[...] = c_bd
    n_scr[...] = n_row
    m_scr[...] = m_x

    @pl.when(tb == pl.num_programs(1) - 1)
    def _():
        for h in range(HEADS):
            c_ref[0, h] = c_scr[h * DH:(h + 1) * DH, h * DH:(h + 1) * DH]
        n_ref[0] = n_scr[...]
        m_ref[0] = m_scr[...]


def mlstm_prompt(za, gates, b_i, b_f, norm, B, S, tt):
    assert S % tt == 0 and tt % CHUNK == 0
    nt = S // tt
    bias = jnp.zeros((1, GATE_PAD), F32).at[0, 0:HEADS].set(b_i).at[0, HEADS:2 * HEADS].set(b_f)
    ha, c, n, m = pl.pallas_call(
        functools.partial(_mlstm_prompt_kernel, nchunks=tt // CHUNK, group=math.gcd(tt // CHUNK, 4)),
        grid=(B, nt),
        in_specs=[pl.BlockSpec((tt, 4 * HW), lambda b, t: (b * nt + t, 0)),
                  pl.BlockSpec((tt, GATE_PAD), lambda b, t: (b * nt + t, 0)),
                  pl.BlockSpec((1, GATE_PAD), lambda b, t: (0, 0)),
                  pl.BlockSpec((1, HW), lambda b, t: (0, 0))],
        out_specs=[pl.BlockSpec((tt, HW), lambda b, t: (b * nt + t, 0)),
                   pl.BlockSpec((1, HEADS, DH, DH), lambda b, t: (b, 0, 0, 0)),
                   pl.BlockSpec((1, 1, HW), lambda b, t: (b, 0, 0)),
                   pl.BlockSpec((1, 1, HW), lambda b, t: (b, 0, 0))],
        out_shape=[jax.ShapeDtypeStruct((B * S, HW), F32),
                   jax.ShapeDtypeStruct((B, HEADS, DH, DH), F32),
                   jax.ShapeDtypeStruct((B, 1, HW), F32),
                   jax.ShapeDtypeStruct((B, 1, HW), F32)],
        scratch_shapes=[pltpu.VMEM((HW, HW), F32), pltpu.VMEM((1, HW), F32), pltpu.VMEM((1, HW), F32),
                        pltpu.VMEM((tt, HW), F32), pltpu.VMEM((tt, HW), F32)],
        compiler_params=_cparams("parallel", "arbitrary"),
    )(za, gates, bias, norm.reshape(1, HW))
    return ha, c, n.reshape(B, HEADS, DH), m[:, 0, ::DH]


def _gdn_prompt_kernel(zc_ref, zg_ref, gt_ref, alog_ref, dtb_ref, norm_ref,
                       hc_ref, s_ref,
                       s_scr, q_scr, k_scr, v_scr, beta_scr, g_scr, uv_scr, wq_scr, qkm_scr, kwt_scr,
                       *, nseq, nchunks, group):
    tb = pl.program_id(1)
    L = CHUNK
    tt = nchunks * L
    tril, strict, eye, bd = _lane_cat_masks(L)
    ones_bd = bd.astype(BF16)

    @pl.when(tb == 0)
    def _():
        s_scr[...] = jnp.zeros_like(s_scr)

    for i in range(nseq):
        sr = slice(i * tt, (i + 1) * tt)
        y = zc_ref[i]
        q_raw, k_raw = y[:, 0:HW], y[:, HW:2 * HW]
        q_scr[sr, :] = (q_raw * lax.rsqrt(_dot_sel_r(q_raw * q_raw, ones_bd) + EPS) * (DH ** -0.5)).astype(BF16)
        k_scr[sr, :] = k_raw * lax.rsqrt(_dot_sel_r(k_raw * k_raw, ones_bd) + EPS)
        v_scr[sr, :] = y[:, 2 * HW:3 * HW]
        gt = gt_ref[i]
        beta_scr[sr, :] = _dot_sel_r(_sigmoid(gt), _head_expander(2 * HEADS))
        la_cols = -jnp.exp(alog_ref[...]) * _softplus(gt + dtb_ref[...])
        g_scr[sr, :] = _dot_sel_r(_chunk_cumsum(la_cols), _head_expander(3 * HEADS))

    def prepare(gi, carry):
        cis = [gi * group + j for j in range(group)]
        rows = [pl.ds(pl.multiple_of(ci * L, L), L) for ci in cis]
        k = [k_scr[r, :] for r in rows]
        g_x = [g_scr[r, :] for r in rows]
        beta_row = [_col_to_row(beta_scr[r, :], eye) for r in rows]
        dec_incl = [jnp.where(tril, jnp.exp(jnp.where(tril, g - _col_to_row(g, eye), 0.0)), 0.0) for g in g_x]
        k_bd = [_expand_bd(kk.astype(BF16), bd) for kk in k]
        n0 = [-(jnp.where(strict, d, 0.0) * _dot_nt(kk.astype(BF16), kbd) * br)
              for d, kk, kbd, br in zip(dec_incl, k, k_bd, beta_row)]
        for r, d, kbd, br in zip(rows, dec_incl, k_bd, beta_row):
            qkm_scr[r, :] = (_dot_nt(q_scr[r, :], kbd) * d * br).astype(BF16)

        p = [_dot(n.astype(BF16), _expand_bd(n.astype(BF16), bd)) for n in n0]
        m = n0
        steps = int(math.log2(L)) - 1
        for i in range(steps):
            p_bd = [_expand_bd(pp.astype(BF16), bd) for pp in p]
            if i < steps - 1:
                pm = [_dot(jnp.concatenate([pp, mm], axis=0).astype(BF16), pbd) for pp, mm, pbd in zip(p, m, p_bd)]
                p_next, mp = [x[0:L] for x in pm], [x[L:2 * L] for x in pm]
            else:
                p_next, mp = None, [_dot(mm.astype(BF16), pbd) for mm, pbd in zip(m, p_bd)]
            m = [mm + pp + x for mm, pp, x in zip(m, p, mp)]
            p = p_next

        for ci, r, kk, g, mm in zip(cis, rows, k, g_x, m):
            v = v_scr[r, :]
            egk = jnp.exp(g) * kk
            rhs_bd = jnp.concatenate([_expand_bd(v.astype(BF16), bd), _expand_bd(egk.astype(BF16), bd)], axis=1)
            mr = _dot(mm.astype(BF16), rhs_bd)
            uv_scr[r, :] = v + mr[:, 0:HW]
            wq_rows = pl.multiple_of(ci * 2 * L, 2 * L)
            wq_scr[pl.ds(wq_rows, L), :] = (egk + mr[:, HW:2 * HW]).astype(BF16)
            wq_scr[pl.ds(wq_rows + L, L), :] = q_scr[r, :]
            kw = kk * (jnp.exp(g[L - 1:L, :] - g) * beta_scr[r, :])
            kwt_scr[pl.ds(pl.multiple_of(ci * HW, HW), HW), :] = kw.T.astype(BF16)
        return carry

    lax.fori_loop(0, nseq * nchunks // group, prepare, 0)

    def advance(c, states):
        cis = [i * nchunks + c for i in range(nseq)]
        rows = [pl.ds(pl.multiple_of(ci * L, L), L) for ci in cis]
        wqs = [_dot(wq_scr[pl.ds(pl.multiple_of(ci * 2 * L, 2 * L), 2 * L), :], s.astype(BF16))
               for ci, s in zip(cis, states)]
        ub = [(uv_scr[r, :] - x[0:L]).astype(BF16) for r, x in zip(rows, wqs)]
        new = [jnp.exp(g_scr[pl.ds(pl.multiple_of(ci * L + L - SUBLANES, SUBLANES), SUBLANES), :][SUBLANES - 1:, :]) * s
               + jnp.where(bd, _dot(kwt_scr[pl.ds(pl.multiple_of(ci * HW, HW), HW), :], u), 0.0)
               for ci, s, u in zip(cis, states, ub)]
        for i, (r, x, u) in enumerate(zip(rows, wqs, ub)):
            o = jnp.exp(g_scr[r, :]) * x[L:2 * L] + _dot(qkm_scr[r, :], _expand_bd(u, bd))
            hc_ref[i, pl.ds(pl.multiple_of(c * L, L), L), :] = (
                _head_rms(o, ones_bd, norm_ref[...]) * _silu(zg_ref[i, pl.ds(pl.multiple_of(c * L, L), L), :]))
        return tuple(new)

    states = lax.fori_loop(0, nchunks, advance, tuple(s_scr[i] for i in range(nseq)))
    for i in range(nseq):
        s_scr[i] = states[i]

    @pl.when(tb == pl.num_programs(1) - 1)
    def _():
        for i in range(nseq):
            for h in range(HEADS):
                s_ref[i, h] = s_scr[i, h * DH:(h + 1) * DH, h * DH:(h + 1) * DH]


def _gdn_gate_rows(a_log, dt_bias):
    z = jnp.zeros((1, GATE_PAD), F32)
    return (z.at[0, 3 * HEADS:4 * HEADS].set(a_log), z.at[0, 3 * HEADS:4 * HEADS].set(dt_bias))


def gdn_prompt(zc, zg, gates, a_log, dt_bias, norm, B, S, tt):
    assert S % tt == 0 and tt % CHUNK == 0
    nt = S // tt
    nseq = math.gcd(B, GDN_SEQS)
    rows = nseq * tt
    alog, dtb = _gdn_gate_rows(a_log, dt_bias)
    blk = lambda w: pl.BlockSpec((nseq, tt, w), lambda b, t: (b, t, 0))
    row = lambda w: pl.BlockSpec((1, w), lambda b, t: (0, 0))
    hc, s = pl.pallas_call(
        functools.partial(_gdn_prompt_kernel, nseq=nseq, nchunks=tt // CHUNK, group=math.gcd(rows // CHUNK, 8)),
        grid=(B // nseq, nt),
        in_specs=[blk(3 * HW), blk(HW), blk(GATE_PAD), row(GATE_PAD), row(GATE_PAD), row(HW)],
        out_specs=[blk(HW), pl.BlockSpec((nseq, HEADS, DH, DH), lambda b, t: (b, 0, 0, 0))],
        out_shape=[jax.ShapeDtypeStruct((B, S, HW), F32), jax.ShapeDtypeStruct((B, HEADS, DH, DH), F32)],
        scratch_shapes=[pltpu.VMEM((nseq, HW, HW), F32),
                        pltpu.VMEM((rows, HW), BF16), pltpu.VMEM((rows, HW), F32), pltpu.VMEM((rows, HW), F32),
                        pltpu.VMEM((rows, HW), F32), pltpu.VMEM((rows, HW), F32), pltpu.VMEM((rows, HW), F32),
                        pltpu.VMEM((2 * rows, HW), BF16), pltpu.VMEM((rows, HW), BF16),
                        pltpu.VMEM((rows // CHUNK * HW, CHUNK), BF16)],
        compiler_params=_cparams("parallel", "arbitrary"),
    )(zc.reshape(B, S, 3 * HW), zg.reshape(B, S, HW), gates.reshape(B, S, GATE_PAD), alog, dtb,
      jnp.tile(norm, HEADS).reshape(1, HW))
    return hc.reshape(B * S, HW), s


def _s5_prep_kernel(are_ref, aim_ref, ldt_ref, bre_ref, bim_ref, lre_ref, lim_ref, bbre_ref, bbim_ref):
    a_re, a_im = are_ref[...], aim_ref[...]
    dt = jnp.exp(ldt_ref[...])
    mag = jnp.exp(a_re * dt)
    lam_re, lam_im = mag * jnp.cos(a_im * dt), mag * jnp.sin(a_im * dt)
    lre_ref[...] = lam_re
    lim_ref[...] = lam_im
    nr, ni = lam_re - 1.0, lam_im
    den = a_re * a_re + a_im * a_im
    coef_re = (nr * a_re + ni * a_im) / den
    coef_im = (ni * a_re - nr * a_im) / den
    b_re, b_im = bre_ref[...], bim_ref[...]
    bbre_ref[...] = coef_re * b_re - coef_im * b_im
    bbim_ref[...] = coef_re * b_im + coef_im * b_re


def s5_params(lp):
    G, N, P = S5_GROUPS, S5_N, S5_P
    row = lambda a: a.astype(F32).reshape(1, G * N)
    to_pn = lambda b: jnp.transpose(b.astype(F32), (2, 0, 1)).reshape(P, G * N)
    shp = [jax.ShapeDtypeStruct((1, G * N), F32)] * 2 + [jax.ShapeDtypeStruct((P, G * N), F32)] * 2
    lam_re, lam_im, bb_re, bb_im = pl.pallas_call(_s5_prep_kernel, out_shape=shp)(
        row(lp['s5_a_re']), row(lp['s5_a_im']), row(jnp.repeat(lp['s5_log_dt'][:, None], N, axis=1)),
        to_pn(lp['s5_b_re']), to_pn(lp['s5_b_im']))
    eye = jnp.eye(S5_GB, dtype=F32)

    def w_in_blocks(bb):
        b4 = bb.reshape(P, S5_NBLK, S5_GB, N)
        return jnp.einsum('pbgn,gh->bgphn', b4, eye).reshape(S5_NBLK, S5_GB * P, S5_GB * N)

    def w_out_blocks(c):
        c4 = c.astype(F32).reshape(S5_NBLK, S5_GB, P, N)
        return jnp.einsum('bgpn,gh->bgnhp', c4, eye).reshape(S5_NBLK, S5_GB * N, S5_GB * P)

    w_in = jnp.concatenate([w_in_blocks(bb_re), w_in_blocks(bb_im)], axis=-1).astype(BF16)
    return {'lam_re': lam_re, 'lam_im': lam_im, 'w_in': w_in,
            'w_out_re': w_out_blocks(lp['s5_c_re']).astype(BF16),
            'w_out_im': (-w_out_blocks(lp['s5_c_im'])).astype(BF16),
            'd': lp['s5_d'].astype(F32).reshape(1, S5_WIDTH), 'w_glu': lp['s5_w_glu'].astype(BF16)}


def _s5_kernel(u_ref, h0r_ref, h0i_ref, lamr_ref, lami_ref, win_ref, wor_ref, woi_ref, d_ref, wglu_ref,
               ys_ref, h1r_ref, h1i_ref, hr_scr, hi_scr, br_scr, bi_scr, *, nseq, rows, bb, lane_blk):
    tb = pl.program_id(0)
    nsteps = nseq * rows // bb
    blk_in, blk_st = S5_GB * S5_P, S5_GB * S5_N

    @pl.when(tb == 0)
    def _():
        hr_scr[...] = h0r_ref[...]
        hi_scr[...] = h0i_ref[...]

    if nseq > 1:
        u = jnp.swapaxes(u_ref[...], 0, 1).reshape(nseq * rows, S5_WIDTH)
    else:
        u = u_ref[0]
    ub = u.astype(BF16)
    for blk in range(S5_NBLK):
        bu = _dot(ub[:, blk * blk_in:(blk + 1) * blk_in], win_ref[blk])
        br_scr[:, blk * blk_st:(blk + 1) * blk_st] = bu[:, 0:blk_st]
        bi_scr[:, blk * blk_st:(blk + 1) * blk_st] = bu[:, blk_st:2 * blk_st]

    for lb in range(S5_STATE // lane_blk):
        ls = slice(lb * lane_blk, (lb + 1) * lane_blk)
        lr = jnp.broadcast_to(lamr_ref[:, ls], (bb, lane_blk))
        li = jnp.broadcast_to(lami_ref[:, ls], (bb, lane_blk))

        def step(t, carry):
            hr, hi = carry
            r = pl.ds(pl.multiple_of(t * bb, bb), bb)
            nr = lr * hr - li * hi + br_scr[r, ls]
            ni = lr * hi + li * hr + bi_scr[r, ls]
            br_scr[r, ls] = nr
            bi_scr[r, ls] = ni
            return nr, ni

        hr, hi = lax.fori_loop(0, nsteps, step, (hr_scr[:, ls], hi_scr[:, ls]), unroll=min(nsteps, 8))
        hr_scr[:, ls] = hr
        hi_scr[:, ls] = hi

    ys = []
    for blk in range(S5_NBLK):
        st = slice(blk * blk_st, (blk + 1) * blk_st)
        ys.append(_dot(br_scr[:, st].astype(BF16), wor_ref[blk]) + _dot(bi_scr[:, st].astype(BF16), woi_ref[blk]))
    gy = jax.nn.gelu(jnp.concatenate(ys, axis=-1) + d_ref[...] * u)
    out = gy * _sigmoid(_dot(gy.astype(BF16), wglu_ref[...]))
    if nseq > 1:
        ys_ref[...] = jnp.swapaxes(out.reshape(rows, nseq, S5_WIDTH), 0, 1)
    else:
        ys_ref[0] = out

    @pl.when(tb == pl.num_programs(0) - 1)
    def _():
        h1r_ref[...] = hr_scr[...]
        h1i_ref[...] = hi_scr[...]


def s5_mixer(u, h0_re, h0_im, sp, tt, single_step):
    B = h0_re.shape[0]
    nseq, S = (1, 1) if single_step else (B, u.shape[1])
    rows = B if single_step else tt
    assert S % tt == 0 and B % SUBLANES == 0
    assert u.shape == ((1, B, S5_WIDTH) if single_step else (B, S, S5_WIDTH))
    lane_blk = max(LANES, min(S5_STATE, (SUBLANES * SUBLANES * LANES) // B))
    full = lambda shape: pl.BlockSpec(shape, lambda t: (0,) * len(shape))
    return pl.pallas_call(
        functools.partial(_s5_kernel, nseq=nseq, rows=rows, bb=B, lane_blk=lane_blk),
        grid=(S // tt,),
        in_specs=[pl.BlockSpec((nseq, rows, S5_WIDTH), lambda t: (0, t, 0)),
                  full((B, S5_STATE)), full((B, S5_STATE)), full((1, S5_STATE)), full((1, S5_STATE)),
                  full(sp['w_in'].shape), full(sp['w_out_re'].shape), full(sp['w_out_im'].shape),
                  full((1, S5_WIDTH)), full((S5_WIDTH, S5_WIDTH))],
        out_specs=[pl.BlockSpec((nseq, rows, S5_WIDTH), lambda t: (0, t, 0)),
                   full((B, S5_STATE)), full((B, S5_STATE))],
        out_shape=[jax.ShapeDtypeStruct(u.shape, F32),
                   jax.ShapeDtypeStruct((B, S5_STATE), F32), jax.ShapeDtypeStruct((B, S5_STATE), F32)],
        scratch_shapes=[pltpu.VMEM((B, S5_STATE), F32), pltpu.VMEM((B, S5_STATE), F32),
                        pltpu.VMEM((nseq * rows, S5_STATE), F32), pltpu.VMEM((nseq * rows, S5_STATE), F32)],
        compiler_params=_cparams("arbitrary"),
    )(u, h0_re, h0_im, sp['lam_re'], sp['lam_im'], sp['w_in'], sp['w_out_re'], sp['w_out_im'], sp['d'], sp['w_glu'])


def _ones_bd():
    r = lax.broadcasted_iota(jnp.int32, (HW, HW), 0) // DH
    c = lax.broadcasted_iota(jnp.int32, (HW, HW), 1) // DH
    return (r == c).astype(BF16)


def _mlstm_sample_kernel(za_ref, gt_ref, bias_ref, norm_ref, c_ref, n_ref, m_ref,
                         ha_ref, c_out, n_out, m_out, q_scr, kw_scr, h_scr):
    za = za_ref[...]
    q_scr[...] = (za[:, 0:HW] * (DH ** -0.5)).T
    k_t = za[:, HW:2 * HW].T
    v_t = za[:, 2 * HW:3 * HW].T
    g_t = (gt_ref[...] + bias_ref[...]).T
    m_out[...] = jnp.zeros_like(m_out)
    for h in range(HEADS):
        hs = slice(h * DH, (h + 1) * DH)
        i_h = g_t[h:h + 1, :]
        bm = _log_sigmoid(g_t[HEADS + h:HEADS + h + 1, :]) + m_ref[h:h + 1, :]
        m_t = jnp.maximum(i_h, bm)
        w_in = jnp.exp(i_h - m_t)
        w_st = jnp.exp(bm - m_t)
        q_h, k_h, v_h = q_scr[hs, :], k_t[hs, :], v_t[hs, :]
        s = jnp.sum(q_h * k_h, axis=0, keepdims=True) * w_in
        kw_scr[hs, :] = k_h * w_in

        def body(d, acc):
            r = h * DH + d
            rows = pl.ds(pl.multiple_of(r * DH, DH), DH)
            c_hd = c_ref[rows, :]
            c_out[rows, :] = w_st * c_hd + kw_scr[pl.ds(r, 1), :] * v_h
            return acc + q_scr[pl.ds(r, 1), :] * c_hd

        qc = lax.fori_loop(0, DH, body, jnp.zeros((DH, za.shape[0]), F32), unroll=4)
        n_h = n_ref[hs, :]
        num = w_st * qc + s * v_h
        den = w_st * jnp.sum(q_h * n_h, axis=0, keepdims=True) + s
        h_scr[hs, :] = num / jnp.maximum(jnp.abs(den), jnp.exp(-m_t))
        n_out[hs, :] = w_st * n_h + kw_scr[hs, :]
        m_out[h:h + 1, :] = m_t
    ha_ref[...] = _head_rms(h_scr[...].T, _ones_bd(), norm_ref[...]) * _sigmoid(za[:, 3 * HW:4 * HW])


def mlstm_sample(za, gates, b_i, b_f, norm, c_t, n_t, m_t):
    B = za.shape[0]
    bias = jnp.zeros((1, GATE_PAD), F32).at[0, 0:HEADS].set(b_i).at[0, HEADS:2 * HEADS].set(b_f)
    shp = lambda *s: jax.ShapeDtypeStruct(s, F32)
    return pl.pallas_call(
        _mlstm_sample_kernel,
        out_shape=[shp(B, HW), shp(HW * DH, B), shp(HW, B), shp(SUBLANES, B)],
        scratch_shapes=[pltpu.VMEM((HW, B), F32), pltpu.VMEM((HW, B), F32), pltpu.VMEM((HW, B), F32)],
        compiler_params=pltpu.CompilerParams(vmem_limit_bytes=VMEM_LIMIT),
    )(za, gates, bias, norm.reshape(1, HW), c_t, n_t, m_t)


def _gdn_sample_kernel(zc_ref, zg_ref, gt_ref, buf_ref, cw_ref, alog_ref, dtb_ref, norm_ref, s_ref,
                       hc_ref, s_out, buf_out, q_scr, k_scr, o_scr):
    W3 = 3 * HW
    x = zc_ref[...]
    y = cw_ref[CONV_K - 1:CONV_K, :] * x
    for j in range(CONV_K - 1):
        y = y + cw_ref[j:j + 1, :] * buf_ref[:, j * W3:(j + 1) * W3]
    buf_out[:, 0:(CONV_K - 2) * W3] = buf_ref[:, W3:(CONV_K - 1) * W3]
    buf_out[:, (CONV_K - 2) * W3:(CONV_K - 1) * W3] = x
    y = _silu(y)
    ones_bd = _ones_bd()
    q_raw, k_raw = y[:, 0:HW], y[:, HW:2 * HW]
    q_scr[...] = (q_raw * lax.rsqrt(_dot_sel_r(q_raw * q_raw, ones_bd) + EPS) * (DH ** -0.5)).T
    k_scr[...] = (k_raw * lax.rsqrt(_dot_sel_r(k_raw * k_raw, ones_bd) + EPS)).T
    v_t = y[:, 2 * HW:3 * HW].T
    gt = gt_ref[...]
    beta_t = _sigmoid(gt).T
    la_t = (-jnp.exp(alog_ref[...]) * _softplus(gt + dtb_ref[...])).T
    nb = x.shape[0]
    for h in range(HEADS):
        hs = slice(h * DH, (h + 1) * DH)
        beta = beta_t[2 * HEADS + h:2 * HEADS + h + 1, :]
        eg = jnp.exp(la_t[3 * HEADS + h:3 * HEADS + h + 1, :])
        q_h, k_h, v_h = q_scr[hs, :], k_scr[hs, :], v_t[hs, :]

        def read(d, acc):
            ks, qs = acc
            r = h * DH + d
            s_hd = s_ref[pl.ds(pl.multiple_of(r * DH, DH), DH), :]
            return ks + k_scr[pl.ds(r, 1), :] * s_hd, qs + q_scr[pl.ds(r, 1), :] * s_hd

        zero = jnp.zeros((DH, nb), F32)
        ks, qs = lax.fori_loop(0, DH, read, (zero, zero), unroll=4)
        u = v_h - eg * ks
        o_scr[hs, :] = eg * qs + (jnp.sum(q_h * k_h, axis=0, keepdims=True) * beta) * u

        def write(d, carry):
            r = h * DH + d
            rows = pl.ds(pl.multiple_of(r * DH, DH), DH)
            s_out[rows, :] = eg * s_ref[rows, :] + (beta * k_scr[pl.ds(r, 1), :]) * u
            return carry

        lax.fori_loop(0, DH, write, 0, unroll=4)
    hc_ref[...] = _head_rms(o_scr[...].T, ones_bd, norm_ref[...]) * _silu(zg_ref[...])


def gdn_sample(zc, zg, gates, buf, conv_w, a_log, dt_bias, norm, s_t):
    B = zc.shape[0]
    alog, dtb = _gdn_gate_rows(a_log, dt_bias)
    shp = lambda *s: jax.ShapeDtypeStruct(s, F32)
    return pl.pallas_call(
        _gdn_sample_kernel,
        out_shape=[shp(B, HW), shp(HW * DH, B), shp(B, (CONV_K - 1) * 3 * HW)],
        scratch_shapes=[pltpu.VMEM((HW, B), F32), pltpu.VMEM((HW, B), F32), pltpu.VMEM((HW, B), F32)],
        compiler_params=pltpu.CompilerParams(vmem_limit_bytes=VMEM_LIMIT),
    )(zc, zg, gates, buf, conv_w, alog, dtb, jnp.tile(norm, HEADS).reshape(1, HW), s_t)


FFN_CHUNK = 1408
IN_SEGMENTS = (4 * HW, S5_WIDTH, 3 * HW, HW, GATE_PAD)


def _tile(n, pref):
    return pref if n % pref == 0 else n


def prep_weights(p):
    w = p['w_in'].astype(BF16)
    a, g4 = 4 * HW, HEADS
    o_u = a + 2 * g4
    o_c = o_u + S5_WIDTH
    o_g = o_c + 3 * HW
    o_b = o_g + HW
    gate_cols = jnp.concatenate([w[:, :, a:a + 2 * g4], w[:, :, o_b:o_b + 2 * g4],
                                 jnp.zeros(w.shape[:2] + (GATE_PAD - 4 * g4,), w.dtype)], axis=2)
    w_in = jnp.concatenate([w[:, :, 0:a], w[:, :, o_u:o_c], w[:, :, o_c:o_g], w[:, :, o_g:o_b], gate_cols], axis=2)
    bf = lambda n: p[n].astype(BF16)
    return {'w_in': w_in, 'w_out': bf('w_out'), 'w_mq': bf('w_mq'), 'w_mo': bf('w_mo'), 'w_mk': bf('w_mk'),
            'w_mv': bf('w_mv'), 'w_gate': bf('w_gate'), 'w_up': bf('w_up'), 'w_down': bf('w_down')}


def mixer_prompt(x, lp, W, layer, B, S):
    T = B * S
    za, zu, zc, zg, gates, tail = in_proj_prompt(x, lp['norm_mix'], W['w_in'], layer, lp['gdn_conv_w'], S,
                                                 _tile(S, 1024))
    buf1 = tail[:, SUBLANES - (CONV_K - 1):, :]
    tt = _tile(S, 512)
    ha, c1, n1, m1 = mlstm_prompt(za, gates, lp['mlstm_b_i'], lp['mlstm_b_f'], lp['mlstm_norm'], B, S, tt)
    h0 = jnp.zeros((B, S5_STATE), F32)
    ys3, r1, i1 = s5_mixer(zu.reshape(B, S, S5_WIDTH), h0, h0, lp['s5'], _tile(S, 128), False)
    ys = ys3.reshape(T, S5_WIDTH)
    hc, s1 = gdn_prompt(zc, zg, gates, lp['gdn_a_log'], lp['gdn_dt_bias'], lp['gdn_norm'], B, S, _tile(S, 256))
    return [ha, ys, hc], (c1, n1, m1, r1.reshape(B, S5_GROUPS, S5_N), i1.reshape(B, S5_GROUPS, S5_N), s1, buf1)


def mixer_sample(x, st, lp, W, layer):
    B = x.shape[0]
    c0, n0, m0, r0, i0, s0, buf0 = st
    za, zu, zc, zg, gates = norm_matmul(x, lp['norm_mix'], W['w_in'], layer, IN_SEGMENTS, B)
    m_t = jnp.zeros((SUBLANES, B), F32).at[0:HEADS, :].set(m0.T)
    ha, c1t, n1t, m1t = mlstm_sample(za, gates, lp['mlstm_b_i'], lp['mlstm_b_f'], lp['mlstm_norm'],
                                     c0.reshape(B, HW * DH).T, n0.reshape(B, HW).T, m_t)
    ys3, r1, i1 = s5_mixer(zu.reshape(1, B, S5_WIDTH), r0.reshape(B, S5_STATE), i0.reshape(B, S5_STATE), lp['s5'], 1, True)
    hc, s1t, buf1 = gdn_sample(zc, zg, gates, buf0.reshape(B, (CONV_K - 1) * 3 * HW), lp['gdn_conv_w'],
                               lp['gdn_a_log'], lp['gdn_dt_bias'], lp['gdn_norm'], s0.reshape(B, HW * DH).T)
    x1 = matmul_residual(x, [ha, ys3.reshape(B, S5_WIDTH), hc], W['w_out'], layer, B)
    return x1, (c1t.T.reshape(B, HEADS, DH, DH), n1t.T.reshape(B, HEADS, DH), m1t[0:HEADS, :].T,
                r1.reshape(B, S5_GROUPS, S5_N), i1.reshape(B, S5_GROUPS, S5_N),
                s1t.T.reshape(B, HEADS, DH, DH), buf1.reshape(B, CONV_K - 1, 3 * HW))


def _mem_kv_kernel(x_ref, g_ref, wk_ref, wv_ref, k2_ref, v2_ref, k5_ref, v5_ref):
    xn = _rms(x_ref[...], g_ref[...]).astype(BF16)
    for w_ref, o2_ref, o5_ref in ((wk_ref, k2_ref, k5_ref), (wv_ref, v2_ref, v5_ref)):
        y = _dot(xn, w_ref[...])
        o2_ref[...] = y
        o5_ref[...] = y.reshape(o5_ref.shape)


def mem_kv(mem, norm_mem, wk, wv):
    B, M, D = mem.shape
    L = wk.shape[0]
    dh = D // X_HEADS
    w_spec = pl.BlockSpec((None, D, D), lambda l, b: (l, 0, 0))
    o2_spec = pl.BlockSpec((None, None, M, D), lambda l, b: (l, b, 0, 0))
    o5_spec = pl.BlockSpec((None, None, M, X_HEADS, dh), lambda l, b: (l, b, 0, 0, 0))
    return pl.pallas_call(
        _mem_kv_kernel,
        grid=(L, B),
        in_specs=[pl.BlockSpec((None, M, D), lambda l, b: (b, 0, 0)),
                  pl.BlockSpec((None, 1, D), lambda l, b: (l, 0, 0)), w_spec, w_spec],
        out_specs=[o2_spec, o2_spec, o5_spec, o5_spec],
        out_shape=[jax.ShapeDtypeStruct((L, B, M, D), F32)] * 2
                  + [jax.ShapeDtypeStruct((L, B, M, X_HEADS, dh), F32)] * 2,
        compiler_params=_cparams("parallel", "parallel"),
    )(mem, norm_mem.reshape(L, 1, D), wk, wv)


def xattn_ffn_prompt(x, acts, mk, mv, lp, W, layer, S, norm_final, final):
    T, D = x.shape
    x2 = xattn_prompt(x, acts, W['w_out'], lp['norm_xattn'], W['w_mq'], mk, mv, W['w_mo'], layer, S, _tile(S, 1024))
    return ffn(x2, lp['norm_ffn'], W['w_gate'], W['w_up'], W['w_down'], layer, norm_final, final, _tile(T, 512),
               FFN_CHUNK)


def xattn_ffn_sample(x, ck, cv, lp, W, layer, norm_final, final):
    B, D = x.shape
    (q,) = norm_matmul(x, lp['norm_xattn'], W['w_mq'], layer, (D,), B)
    o = xattn_sample(q, ck, cv, layer, 4)
    x2 = matmul_residual(x, [o], W['w_mo'], layer, B)
    return ffn(x2, lp['norm_ffn'], W['w_gate'], W['w_up'], W['w_down'], layer, norm_final, final, B, FFN_CHUNK)


LAYER_PARAMS = ('norm_mix', 'w_in', 'w_out', 'mlstm_b_i', 'mlstm_b_f', 'mlstm_norm', 's5_a_re', 's5_a_im', 's5_log_dt',
                's5_b_re', 's5_b_im', 's5_c_re', 's5_c_im', 's5_d', 's5_w_glu', 'gdn_conv_w', 'gdn_a_log',
                'gdn_dt_bias', 'gdn_norm', 'norm_xattn', 'norm_mem', 'w_mq', 'w_mk', 'w_mv', 'w_mo', 'norm_ffn',
                'w_gate', 'w_up', 'w_down')


def kernel(x_prompt, x_sample, mem_prompt, cache_mem_k, cache_mem_v, state_mlstm_c, state_mlstm_n, state_mlstm_m, state_s5_re, state_s5_im, state_gdn, state_gdn_conv, norm_mix, w_in, w_out, mlstm_b_i, mlstm_b_f, mlstm_norm, s5_a_re, s5_a_im, s5_log_dt, s5_b_re, s5_b_im, s5_c_re, s5_c_im, s5_d, s5_w_glu, gdn_conv_w, gdn_a_log, gdn_dt_bias, gdn_norm, norm_xattn, norm_mem, w_mq, w_mk, w_mv, w_mo, norm_ffn, w_gate, w_up, w_down, norm_final):
    stacked = dict(norm_mix=norm_mix, w_in=w_in, w_out=w_out, mlstm_b_i=mlstm_b_i, mlstm_b_f=mlstm_b_f,
                   mlstm_norm=mlstm_norm, s5_a_re=s5_a_re, s5_a_im=s5_a_im, s5_log_dt=s5_log_dt, s5_b_re=s5_b_re,
                   s5_b_im=s5_b_im, s5_c_re=s5_c_re, s5_c_im=s5_c_im, s5_d=s5_d, s5_w_glu=s5_w_glu,
                   gdn_conv_w=gdn_conv_w, gdn_a_log=gdn_a_log, gdn_dt_bias=gdn_dt_bias, gdn_norm=gdn_norm,
                   norm_xattn=norm_xattn, norm_mem=norm_mem, w_mq=w_mq, w_mk=w_mk, w_mv=w_mv, w_mo=w_mo,
                   norm_ffn=norm_ffn, w_gate=w_gate, w_up=w_up, w_down=w_down)
    B, S, D = x_prompt.shape
    Bs = x_sample.shape[0]
    M = mem_prompt.shape[1]
    depth = w_in.shape[0]
    xp = x_prompt.reshape(B * S, D)
    xs = x_sample.reshape(Bs, D)
    cache_k, cache_v = cache_mem_k, cache_mem_v
    st_p, st_s = [], []
    W = prep_weights(stacked)
    mk, mv, mem_k, mem_v = mem_kv(mem_prompt, norm_mem, W['w_mk'], W['w_mv'])
    for l in range(depth):
        lp = {n: stacked[n][l] for n in LAYER_PARAMS if n not in W}
        lp['s5'] = s5_params(lp)
        last = l == depth - 1
        acts, sp = mixer_prompt(xp, lp, W, l, B, S)
        xp = xattn_ffn_prompt(xp, acts, mk, mv, lp, W, l, S, norm_final, last)
        st_in = (state_mlstm_c[l], state_mlstm_n[l], state_mlstm_m[l], state_s5_re[l], state_s5_im[l],
                 state_gdn[l], state_gdn_conv[l])
        xs, ss = mixer_sample(xs, st_in, lp, W, l)
        xs = xattn_ffn_sample(xs, cache_k, cache_v, lp, W, l, norm_final, last)
        st_p.append(sp)
        st_s.append(ss)
    stack = lambda lst: [jnp.stack([st[i] for st in lst]) for i in range(7)]
    return (xp.reshape(B, S, D), xs.reshape(Bs, 1, D), mem_k, mem_v,
            *stack(st_p), *stack(st_s))
```

```python
import functools
import math

import jax
import jax.numpy as jnp
from jax import lax
from jax.experimental import pallas as pl
from jax.experimental.pallas import tpu as pltpu

F32 = jnp.float32
BF16 = jnp.bfloat16
EPS = 1e-6

HEADS = 4
DH = 64
HW = HEADS * DH
CHUNK = 64
S5_P = 16
S5_N = 64
S5_GROUPS = 32
S5_WIDTH = S5_GROUPS * S5_P
S5_STATE = S5_GROUPS * S5_N
S5_GB = 8
S5_NBLK = S5_GROUPS // S5_GB
CONV_K = 4
GDN_SEQS = 4
X_HEADS = 4
GATE_PAD = 128
LANES = 128
SUBLANES = 8
VMEM_LIMIT = 48 * 1024 * 1024


def _cparams(*sem):
    return pltpu.CompilerParams(dimension_semantics=sem, vmem_limit_bytes=VMEM_LIMIT)


def _rms(x, g_row):
    return x * lax.rsqrt(jnp.mean(x * x, axis=-1, keepdims=True) + EPS) * g_row


def _dot(a, b):
    return jnp.dot(a, b, preferred_element_type=F32)


def _dot_nt(a, b):
    return lax.dot_general(a, b, (((1,), (1,)), ((), ())), preferred_element_type=F32)


def _sigmoid(x):
    return 1.0 / (1.0 + jnp.exp(-x))


def _silu(x):
    return x * _sigmoid(x)


def _softplus(x):
    return jnp.maximum(x, 0.0) + jnp.log1p(jnp.exp(-jnp.abs(x)))


def _log_sigmoid(x):
    return jnp.minimum(x, 0.0) - jnp.log1p(jnp.exp(-jnp.abs(x)))


def _norm_matmul_kernel(x_ref, g_ref, w_ref, *out_refs, splits):
    xn = _rms(x_ref[...], g_ref[...]).astype(BF16)
    off = 0
    for o_ref, n in zip(out_refs, splits):
        o_ref[...] = _dot(xn, w_ref[:, off:off + n])
        off += n


def _layer_weight(w_all, layer):
    return pl.BlockSpec((None,) + w_all.shape[1:], lambda *_: (layer, 0, 0), pipeline_mode=pl.Buffered(1))


def norm_matmul(x, g, w_all, layer, splits, tm):
    T, D = x.shape
    N = w_all.shape[2]
    assert sum(splits) == N and T % tm == 0
    return pl.pallas_call(
        functools.partial(_norm_matmul_kernel, splits=tuple(splits)),
        grid=(T // tm,),
        in_specs=[pl.BlockSpec((tm, D), lambda i: (i, 0)),
                  pl.BlockSpec((1, D), lambda i: (0, 0)),
                  _layer_weight(w_all, layer)],
        out_specs=[pl.BlockSpec((tm, n), lambda i: (i, 0)) for n in splits],
        out_shape=[jax.ShapeDtypeStruct((T, n), F32) for n in splits],
        compiler_params=_cparams("parallel"),
    )(x, g.reshape(1, D), w_all)


def _in_proj_prompt_kernel(x_ref, g_ref, w_ref, cw_ref, za_ref, zu_ref, zc_ref, zg_ref, gt_ref, tail_ref, xp_scr,
                           *, tiles_per_seq):
    tm = x_ref.shape[0]
    pad = SUBLANES
    xn = _rms(x_ref[...], g_ref[...]).astype(BF16)
    offs = [sum(IN_SEGMENTS[:s]) for s in range(len(IN_SEGMENTS))]

    @pl.when(pl.program_id(0) % tiles_per_seq == 0)
    def _():
        xp_scr[0:pad, :] = jnp.zeros((pad, xp_scr.shape[1]), F32)

    xp_scr[pad:pad + tm, :] = _dot(xn, w_ref[:, offs[2]:offs[2] + IN_SEGMENTS[2]])
    others = ((0, za_ref), (1, zu_ref), (3, zg_ref), (4, gt_ref))
    rows = tm // len(others)
    for r, (s, o_ref) in enumerate(others):
        y = cw_ref[CONV_K - 1:CONV_K, :] * xp_scr[pl.ds(pad + r * rows, rows), :]
        for j in range(CONV_K - 1):
            y = y + cw_ref[j:j + 1, :] * xp_scr[pl.ds(pad + r * rows - (CONV_K - 1) + j, rows), :]
        zc_ref[pl.ds(r * rows, rows), :] = _silu(y)
        o_ref[...] = _dot(xn, w_ref[:, offs[s]:offs[s] + IN_SEGMENTS[s]])
    tail = xp_scr[tm:tm + pad, :]
    xp_scr[0:pad, :] = tail
    tail_ref[0] = tail


def in_proj_prompt(x, g, w_all, layer, conv_w, seq, tm):
    T, D = x.shape
    assert seq % tm == 0 and T % seq == 0
    wc = IN_SEGMENTS[2]
    outs = pl.pallas_call(
        functools.partial(_in_proj_prompt_kernel, tiles_per_seq=seq // tm),
        grid=(T // tm,),
        in_specs=[pl.BlockSpec((tm, D), lambda i: (i, 0)),
                  pl.BlockSpec((1, D), lambda i: (0, 0)),
                  _layer_weight(w_all, layer),
                  pl.BlockSpec((CONV_K, wc), lambda i: (0, 0))],
        out_specs=[pl.BlockSpec((tm, n), lambda i: (i, 0)) for n in IN_SEGMENTS]
                  + [pl.BlockSpec((1, SUBLANES, wc), lambda i: (i // (seq // tm), 0, 0))],
        out_shape=[jax.ShapeDtypeStruct((T, n), F32) for n in IN_SEGMENTS]
                  + [jax.ShapeDtypeStruct((T // seq, SUBLANES, wc), F32)],
        scratch_shapes=[pltpu.VMEM((tm + SUBLANES, wc), F32)],
        compiler_params=_cparams("arbitrary"),
    )(x, g.reshape(1, D), w_all, conv_w)
    return outs


def _matmul_residual_kernel(x_ref, *refs, ksplits):
    a_refs, w_ref, o_ref = refs[:-2], refs[-2], refs[-1]
    acc = x_ref[...]
    off = 0
    for a_ref, k in zip(a_refs, ksplits):
        acc = acc + _dot(a_ref[...].astype(BF16), w_ref[off:off + k, :])
        off += k
    o_ref[...] = acc


def matmul_residual(x, acts, w_all, layer, tm):
    T, D = x.shape
    ks = tuple(a.shape[1] for a in acts)
    K = w_all.shape[1]
    assert sum(ks) == K and T % tm == 0
    return pl.pallas_call(
        functools.partial(_matmul_residual_kernel, ksplits=ks),
        grid=(T // tm,),
        in_specs=[pl.BlockSpec((tm, D), lambda i: (i, 0))]
                 + [pl.BlockSpec((tm, k), lambda i: (i, 0)) for k in ks]
                 + [_layer_weight(w_all, layer)],
        out_specs=pl.BlockSpec((tm, D), lambda i: (i, 0)),
        out_shape=jax.ShapeDtypeStruct((T, D), F32),
        compiler_params=_cparams("parallel"),
    )(x, *acts, w_all)


def _ffn_kernel(x_ref, g_ref, wg_ref, wu_ref, wd_ref, gf_ref, o_ref, *, final_norm, tf):
    x = x_ref[...]
    h = _rms(x, g_ref[...]).astype(BF16)
    y = x
    for j in range(wg_ref.shape[1] // tf):
        cols = slice(j * tf, (j + 1) * tf)
        a = _dot(h, wg_ref[:, cols])
        b = _dot(h, wu_ref[:, cols])
        y = y + _dot((_silu(a) * b).astype(BF16), wd_ref[cols, :])
    if final_norm:
        y = _rms(y, gf_ref[...])
    o_ref[...] = y


def _resident(shape):
    return pl.BlockSpec(shape, lambda *_: (0,) * len(shape), pipeline_mode=pl.Buffered(1))


def ffn(x, g, wg, wu, wd, layer, g_final, final_norm, tm, tf):
    T, D = x.shape
    F = wg.shape[2]
    assert T % tm == 0 and F % tf == 0
    return pl.pallas_call(
        functools.partial(_ffn_kernel, final_norm=final_norm, tf=tf),
        grid=(T // tm,),
        in_specs=[pl.BlockSpec((tm, D), lambda i: (i, 0)),
                  _resident((1, D)), _layer_weight(wg, layer), _layer_weight(wu, layer), _layer_weight(wd, layer),
                  _resident((1, D))],
        out_specs=pl.BlockSpec((tm, D), lambda i: (i, 0)),
        out_shape=jax.ShapeDtypeStruct((T, D), F32),
        compiler_params=_cparams("parallel"),
    )(x, g.reshape(1, D), wg, wu, wd, g_final.reshape(1, D))


def _softmax_rows(s):
    e = jnp.exp(s - jnp.max(s, axis=-1, keepdims=True))
    return e / jnp.sum(e, axis=-1, keepdims=True)


def _xattn_prompt_kernel(x_ref, *refs, dh, n_acts):
    a_refs = refs[:n_acts]
    wmix_ref, g_ref, wq_ref, k_ref, v_ref, wo_ref, o_ref = refs[n_acts:]
    scale = dh ** -0.5
    x = x_ref[...]
    off = 0
    for a_ref in a_refs:
        x = x + _dot(a_ref[...].astype(BF16), wmix_ref[off:off + a_ref.shape[1], :])
        off += a_ref.shape[1]
    q = _dot(_rms(x, g_ref[...]).astype(BF16), wq_ref[...]).astype(BF16)
    sl = [slice(h * dh, (h + 1) * dh) for h in range(X_HEADS)]
    s = [_dot_nt(q[:, c], k_ref[0, :, c].astype(BF16)) * scale for c in sl]
    p = [_softmax_rows(sh).astype(BF16) for sh in s]
    heads = [_dot(ph, v_ref[0, :, c].astype(BF16)).astype(BF16) for ph, c in zip(p, sl)]
    o_ref[...] = x + _dot(jnp.concatenate(heads, axis=-1), wo_ref[...])


def xattn_prompt(x, acts, wmix, g, wq, mk, mv, wo, layer, seq, tq):
    T, D = x.shape
    _, B, M, _ = mk.shape
    nt = seq // tq
    rows = lambda w: pl.BlockSpec((tq, w), lambda b, t: (b * nt + t, 0))
    return pl.pallas_call(
        functools.partial(_xattn_prompt_kernel, dh=D // X_HEADS, n_acts=len(acts)),
        grid=(B, nt),
        in_specs=[rows(D)] + [rows(a.shape[1]) for a in acts]
                 + [_layer_weight(wmix, layer),
                    pl.BlockSpec((1, D), lambda b, t: (0, 0)),
                    _layer_weight(wq, layer),
                    pl.BlockSpec((None, 1, M, D), lambda b, t: (layer, b, 0, 0)),
                    pl.BlockSpec((None, 1, M, D), lambda b, t: (layer, b, 0, 0)),
                    _layer_weight(wo, layer)],
        out_specs=rows(D),
        out_shape=jax.ShapeDtypeStruct((T, D), F32),
        compiler_params=_cparams("parallel", "parallel"),
    )(x, *acts, wmix, g.reshape(1, D), wq, mk, mv, wo)


def _xattn_sample_kernel(q_ref, k_ref, v_ref, o_ref, *, sb):
    M, H, dh = k_ref.shape[1:]
    scale = dh ** -0.5
    row = lax.broadcasted_iota(jnp.int32, (SUBLANES, M * H), 0)
    col_head = lax.broadcasted_iota(jnp.int32, (SUBLANES, M * H), 1) % H
    own = (row % H) == col_head
    pad = jnp.zeros((SUBLANES - H, dh), F32)
    q8 = [jnp.concatenate([q_ref[i], pad], axis=0).astype(BF16) for i in range(sb)]
    s = [_dot_nt(q8[i], k_ref[i].reshape(M * H, dh).astype(BF16)) * scale for i in range(sb)]
    p = [_softmax_rows(jnp.where(own, si, -jnp.inf)).astype(BF16) for si in s]
    for i in range(sb):
        o_ref[i] = _dot(p[i], v_ref[i].reshape(M * H, dh).astype(BF16))[0:H]


def xattn_sample(q, ck, cv, layer, sb):
    B, D = q.shape
    _, _, M, H, dh = ck.shape
    out = pl.pallas_call(
        functools.partial(_xattn_sample_kernel, sb=sb),
        grid=(B // sb,),
        in_specs=[pl.BlockSpec((sb, H, dh), lambda i: (i, 0, 0)),
                  pl.BlockSpec((None, sb, M, H, dh), lambda i: (layer, i, 0, 0, 0)),
                  pl.BlockSpec((None, sb, M, H, dh), lambda i: (layer, i, 0, 0, 0))],
        out_specs=pl.BlockSpec((sb, H, dh), lambda i: (i, 0, 0)),
        out_shape=jax.ShapeDtypeStruct((B, H, dh), F32),
        compiler_params=_cparams("parallel"),
    )(q.reshape(B, H, dh), ck, cv)
    return out.reshape(B, D)


def _lane_cat_masks(L):
    row = lax.broadcasted_iota(jnp.int32, (L, HW), 0)
    j = lax.broadcasted_iota(jnp.int32, (L, HW), 1) % DH
    r2 = lax.broadcasted_iota(jnp.int32, (HW, HW), 0) // DH
    c2 = lax.broadcasted_iota(jnp.int32, (HW, HW), 1) // DH
    return row >= j, row > j, row == j, r2 == c2


def _expand_bd(x, bd):
    return jnp.where(bd, jnp.concatenate([x] * HEADS, axis=0), jnp.zeros((), x.dtype))


def _seg_reduce(x, op, fill):
    lo = lax.broadcasted_iota(jnp.int32, (x.shape[0], LANES), 1) < DH
    parts = []
    for c in range(HW // LANES):
        xh = x[:, c * LANES:(c + 1) * LANES]
        a = op(jnp.where(lo, xh, fill), axis=-1, keepdims=True)
        b = op(jnp.where(lo, fill, xh), axis=-1, keepdims=True)
        parts.append(jnp.where(lo, a, b))
    return jnp.concatenate(parts, axis=-1)


def _head_expander(first_lane):
    r = lax.broadcasted_iota(jnp.int32, (GATE_PAD, HW), 0)
    c = lax.broadcasted_iota(jnp.int32, (GATE_PAD, HW), 1) // DH
    return (r == c + first_lane).astype(BF16)


def _chunk_cumsum(x):
    tt, w = x.shape
    g = SUBLANES
    x3 = x.reshape(tt // g, g, w)
    sub = lax.broadcasted_iota(jnp.int32, x3.shape, 1)
    s = 1
    while s < g:
        x3 = x3 + jnp.where(sub >= s, pltpu.roll(x3, s, 1), 0.0)
        s *= 2
    per = CHUNK // g
    x4 = x3.reshape(tt // CHUNK, per, g, w)
    acc = jnp.zeros((tt // CHUNK, 1, 1, w), F32)
    parts = []
    for i in range(per):
        parts.append(x4[:, i:i + 1] + acc)
        acc = acc + x4[:, i:i + 1, g - 1:g, :]
    return jnp.concatenate(parts, axis=1).reshape(tt, w)


def _split3(x):
    hi = x.astype(BF16)
    r = x - hi.astype(F32)
    mid = r.astype(BF16)
    return hi, mid, (r - mid.astype(F32)).astype(BF16)


def _dot_sel_r(x, sel):
    hi, mid, lo = _split3(x)
    return (_dot(lo, sel) + _dot(mid, sel)) + _dot(hi, sel)


def _col_to_row(x, eye):
    return jnp.sum(jnp.where(eye, x, 0.0), axis=0, keepdims=True)


def _head_rms(x, ones_bd, g_row):
    ms = _dot_sel_r(x * x, ones_bd) * (1.0 / DH)
    return x * lax.rsqrt(ms + EPS) * g_row


def _mlstm_prompt_kernel(za_ref, gt_ref, bias_ref, norm_ref, ha_ref, c_ref, n_ref, m_ref,
                         c_scr, n_scr, m_scr, ix_scr, bx_scr, *, nchunks, group):
    tb = pl.program_id(1)
    L = CHUNK
    tril, _, eye, bd = _lane_cat_masks(L)
    ones_bd = bd.astype(BF16)

    @pl.when(tb == 0)
    def _():
        c_scr[...] = jnp.zeros_like(c_scr)
        n_scr[...] = jnp.zeros_like(n_scr)
        m_scr[...] = jnp.zeros_like(m_scr)

    gt = gt_ref[...] + bias_ref[...]
    b_cols = _chunk_cumsum(_log_sigmoid(gt))
    ix_scr[...] = _dot_sel_r(gt, _head_expander(0))
    bx_scr[...] = _dot_sel_r(b_cols, _head_expander(HEADS))

    def chunks(gi, carry):
        c_bd, n_row, m_x = carry
        rows = [pl.ds(pl.multiple_of((gi * group + j) * L, L), L) for j in range(group)]
        q = [(za_ref[r, 0:HW] * (DH ** -0.5)).astype(BF16) for r in rows]
        k = [za_ref[r, HW:2 * HW] for r in rows]
        v = [za_ref[r, 2 * HW:3 * HW].astype(BF16) for r in rows]
        i_x = [ix_scr[r, :] for r in rows]
        b_x = [bx_scr[r, :] for r in rows]

        m_in, n_in, kw, decay = [], [], [], []
        for kk, ii, bb in zip(k, i_x, b_x):
            b_last = bb[L - 1:L, :]
            g_x = b_last - bb + ii
            m_new = jnp.maximum(b_last + m_x, jnp.max(g_x, axis=0, keepdims=True))
            kw.append(kk * jnp.exp(g_x - m_new))
            decay.append(jnp.exp(b_last + m_x - m_new))
            m_in.append(m_x)
            n_in.append(n_row)
            n_row = decay[-1] * n_row + jnp.sum(kw[-1], axis=0, keepdims=True)
            m_x = m_new

        d_intra = [jnp.where(tril, bb - _col_to_row(bb, eye) + _col_to_row(ii, eye), -jnp.inf) for bb, ii in zip(b_x, i_x)]
        d_inter = [bb + mm for bb, mm in zip(b_x, m_in)]
        m_t = [jnp.maximum(_seg_reduce(di, jnp.max, -jnp.inf), de) for di, de in zip(d_intra, d_inter)]
        w_inter = [jnp.exp(de - mt) for de, mt in zip(d_inter, m_t)]
        s = [_dot_nt(qq, _expand_bd(kk.astype(BF16), bd)) * jnp.exp(di - mt) for qq, kk, di, mt in zip(q, k, d_intra, m_t)]
        kv = [jnp.where(bd, _dot(kwj.T.astype(BF16), vv), 0.0) for kwj, vv in zip(kw, v)]
        c_in = []
        for dj, kvj in zip(decay, kv):
            c_in.append(c_bd)
            c_bd = dj * c_bd + kvj
        num = [wi * _dot(qq, cc.astype(BF16)) + _dot(ss.astype(BF16), _expand_bd(vv, bd))
               for wi, qq, cc, ss, vv in zip(w_inter, q, c_in, s, v)]
        den = [wi * _dot((qq.astype(F32) * nn).astype(BF16), ones_bd) + _seg_reduce(ss, jnp.sum, 0.0)
               for wi, qq, nn, ss in zip(w_inter, q, n_in, s)]
        for r, nu, de, mt in zip(rows, num, den, m_t):
            hh = nu / jnp.maximum(jnp.abs(de), jnp.exp(-mt))
            ha_ref[r, :] = _head_rms(hh, ones_bd, norm_ref[...]) * _sigmoid(za_ref[r, 3 * HW:4 * HW])
        return c_bd, n_row, m_x

    c_bd, n_row, m_x = lax.fori_loop(0, nchunks // group, chunks, (c_scr[...], n_scr[...], m_scr[...]))
    c_scr[...] = c_bd
    n_scr[...] = n_row
    m_scr[...] = m_x

    @pl.when(tb == pl.num_programs(1) - 1)
    def _():
        for h in range(HEADS):
            c_ref[0, h] = c_scr[h * DH:(h + 1) * DH, h * DH:(h + 1) * DH]
        n_ref[0] = n_scr[...]
        m_ref[0] = m_scr[...]


def mlstm_prompt(za, gates, b_i, b_f, norm, B, S, tt):
    assert S % tt == 0 and tt % CHUNK == 0
    nt = S // tt
    bias = jnp.zeros((1, GATE_PAD), F32).at[0, 0:HEADS].set(b_i).at[0, HEADS:2 * HEADS].set(b_f)
    ha, c, n, m = pl.pallas_call(
        functools.partial(_mlstm_prompt_kernel, nchunks=tt // CHUNK, group=math.gcd(tt // CHUNK, 4)),
        grid=(B, nt),
        in_specs=[pl.BlockSpec((tt, 4 * HW), lambda b, t: (b * nt + t, 0)),
                  pl.BlockSpec((tt, GATE_PAD), lambda b, t: (b * nt + t, 0)),
                  pl.BlockSpec((1, GATE_PAD), lambda b, t: (0, 0)),
                  pl.BlockSpec((1, HW), lambda b, t: (0, 0))],
        out_specs=[pl.BlockSpec((tt, HW), lambda b, t: (b * nt + t, 0)),
                   pl.BlockSpec((1, HEADS, DH, DH), lambda b, t: (b, 0, 0, 0)),
                   pl.BlockSpec((1, 1, HW), lambda b, t: (b, 0, 0)),
                   pl.BlockSpec((1, 1, HW), lambda b, t: (b, 0, 0))],
        out_shape=[jax.ShapeDtypeStruct((B * S, HW), F32),
                   jax.ShapeDtypeStruct((B, HEADS, DH, DH), F32),
                   jax.ShapeDtypeStruct((B, 1, HW), F32),
                   jax.ShapeDtypeStruct((B, 1, HW), F32)],
        scratch_shapes=[pltpu.VMEM((HW, HW), F32), pltpu.VMEM((1, HW), F32), pltpu.VMEM((1, HW), F32),
                        pltpu.VMEM((tt, HW), F32), pltpu.VMEM((tt, HW), F32)],
        compiler_params=_cparams("parallel", "arbitrary"),
    )(za, gates, bias, norm.reshape(1, HW))
    return ha, c, n.reshape(B, HEADS, DH), m[:, 0, ::DH]


def _gdn_prompt_kernel(zc_ref, zg_ref, gt_ref, alog_ref, dtb_ref, norm_ref,
                       hc_ref, s_ref,
                       s_scr, q_scr, k_scr, v_scr, beta_scr, g_scr, uv_scr, wq_scr, qkm_scr, kwt_scr,
                       *, nseq, nchunks, group):
    tb = pl.program_id(1)
    L = CHUNK
    tt = nchunks * L
    tril, strict, eye, bd = _lane_cat_masks(L)
    ones_bd = bd.astype(BF16)

    @pl.when(tb == 0)
    def _():
        s_scr[...] = jnp.zeros_like(s_scr)

    for i in range(nseq):
        sr = slice(i * tt, (i + 1) * tt)
        y = zc_ref[i]
        q_raw, k_raw = y[:, 0:HW], y[:, HW:2 * HW]
        q_scr[sr, :] = (q_raw * lax.rsqrt(_dot_sel_r(q_raw * q_raw, ones_bd) + EPS) * (DH ** -0.5)).astype(BF16)
        k_scr[sr, :] = k_raw * lax.rsqrt(_dot_sel_r(k_raw * k_raw, ones_bd) + EPS)
        v_scr[sr, :] = y[:, 2 * HW:3 * HW]
        gt = gt_ref[i]
        beta_scr[sr, :] = _dot_sel_r(_sigmoid(gt), _head_expander(2 * HEADS))
        la_cols = -jnp.exp(alog_ref[...]) * _softplus(gt + dtb_ref[...])
        g_scr[sr, :] = _dot_sel_r(_chunk_cumsum(la_cols), _head_expander(3 * HEADS))

    def prepare(gi, carry):
        cis = [gi * group + j for j in range(group)]
        rows = [pl.ds(pl.multiple_of(ci * L, L), L) for ci in cis]
        k = [k_scr[r, :] for r in rows]
        g_x = [g_scr[r, :] for r in rows]
        beta_row = [_col_to_row(beta_scr[r, :], eye) for r in rows]
        dec_incl = [jnp.where(tril, jnp.exp(jnp.where(tril, g - _col_to_row(g, eye), 0.0)), 0.0) for g in g_x]
        k_bd = [_expand_bd(kk.astype(BF16), bd) for kk in k]
        n0 = [-(jnp.where(strict, d, 0.0) * _dot_nt(kk.astype(BF16), kbd) * br)
              for d, kk, kbd, br in zip(dec_incl, k, k_bd, beta_row)]
        for r, d, kbd, br in zip(rows, dec_incl, k_bd, beta_row):
            qkm_scr[r, :] = (_dot_nt(q_scr[r, :], kbd) * d * br).astype(BF16)

        p = [_dot(n.astype(BF16), _expand_bd(n.astype(BF16), bd)) for n in n0]
        m = n0
        steps = int(math.log2(L)) - 1
        for i in range(steps):
            p_bd = [_expand_bd(pp.astype(BF16), bd) for pp in p]
            if i < steps - 1:
                pm = [_dot(jnp.concatenate([pp, mm], axis=0).astype(BF16), pbd) for pp, mm, pbd in zip(p, m, p_bd)]
                p_next, mp = [x[0:L] for x in pm], [x[L:2 * L] for x in pm]
            else:
                p_next, mp = None, [_dot(mm.astype(BF16), pbd) for mm, pbd in zip(m, p_bd)]
            m = [mm + pp + x for mm, pp, x in zip(m, p, mp)]
            p = p_next

        for ci, r, kk, g, mm in zip(cis, rows, k, g_x, m):
            v = v_scr[r, :]
            egk = jnp.exp(g) * kk
            rhs_bd = jnp.concatenate([_expand_bd(v.astype(BF16), bd), _expand_bd(egk.astype(BF16), bd)], axis=1)
            mr = _dot(mm.astype(BF16), rhs_bd)
            uv_scr[r, :] = v + mr[:, 0:HW]
            wq_rows = pl.multiple_of(ci * 2 * L, 2 * L)
            wq_scr[pl.ds(wq_rows, L), :] = (egk + mr[:, HW:2 * HW]).astype(BF16)
            wq_scr[pl.ds(wq_rows + L, L), :] = q_scr[r, :]
            kw = kk * (jnp.exp(g[L - 1:L, :] - g) * beta_scr[r, :])
            kwt_scr[pl.ds(pl.multiple_of(ci * HW, HW), HW), :] = kw.T.astype(BF16)
        return carry

    lax.fori_loop(0, nseq * nchunks // group, prepare, 0)

    def advance(c, states):
        cis = [i * nchunks + c for i in range(nseq)]
        rows = [pl.ds(pl.multiple_of(ci * L, L), L) for ci in cis]
        wqs = [_dot(wq_scr[pl.ds(pl.multiple_of(ci * 2 * L, 2 * L), 2 * L), :], s.astype(BF16))
               for ci, s in zip(cis, states)]
        ub = [(uv_scr[r, :] - x[0:L]).astype(BF16) for r, x in zip(rows, wqs)]
        new = [jnp.exp(g_scr[pl.ds(pl.multiple_of(ci * L + L - SUBLANES, SUBLANES), SUBLANES), :][SUBLANES - 1:, :]) * s
               + jnp.where(bd, _dot(kwt_scr[pl.ds(pl.multiple_of(ci * HW, HW), HW), :], u), 0.0)
               for ci, s, u in zip(cis, states, ub)]
        for i, (r, x, u) in enumerate(zip(rows, wqs, ub)):
            o = jnp.exp(g_scr[r, :]) * x[L:2 * L] + _dot(qkm_scr[r, :], _expand_bd(u, bd))
            hc_ref[i, pl.ds(pl.multiple_of(c * L, L), L), :] = (
                _head_rms(o, ones_bd, norm_ref[...]) * _silu(zg_ref[i, pl.ds(pl.multiple_of(c * L, L), L), :]))
        return tuple(new)

    states = lax.fori_loop(0, nchunks, advance, tuple(s_scr[i] for i in range(nseq)))
    for i in range(nseq):
        s_scr[i] = states[i]

    @pl.when(tb == pl.num_programs(1) - 1)
    def _():
        for i in range(nseq):
            for h in range(HEADS):
                s_ref[i, h] = s_scr[i, h * DH:(h + 1) * DH, h * DH:(h + 1) * DH]


def _gdn_gate_rows(a_log, dt_bias):
    z = jnp.zeros((1, GATE_PAD), F32)
    return (z.at[0, 3 * HEADS:4 * HEADS].set(a_log), z.at[0, 3 * HEADS:4 * HEADS].set(dt_bias))


def gdn_prompt(zc, zg, gates, a_log, dt_bias, norm, B, S, tt):
    assert S % tt == 0 and tt % CHUNK == 0
    nt = S // tt
    nseq = math.gcd(B, GDN_SEQS)
    rows = nseq * tt
    alog, dtb = _gdn_gate_rows(a_log, dt_bias)
    blk = lambda w: pl.BlockSpec((nseq, tt, w), lambda b, t: (b, t, 0))
    row = lambda w: pl.BlockSpec((1, w), lambda b, t: (0, 0))
    hc, s = pl.pallas_call(
        functools.partial(_gdn_prompt_kernel, nseq=nseq, nchunks=tt // CHUNK, group=math.gcd(rows // CHUNK, 8)),
        grid=(B // nseq, nt),
        in_specs=[blk(3 * HW), blk(HW), blk(GATE_PAD), row(GATE_PAD), row(GATE_PAD), row(HW)],
        out_specs=[blk(HW), pl.BlockSpec((nseq, HEADS, DH, DH), lambda b, t: (b, 0, 0, 0))],
        out_shape=[jax.ShapeDtypeStruct((B, S, HW), F32), jax.ShapeDtypeStruct((B, HEADS, DH, DH), F32)],
        scratch_shapes=[pltpu.VMEM((nseq, HW, HW), F32),
                        pltpu.VMEM((rows, HW), BF16), pltpu.VMEM((rows, HW), F32), pltpu.VMEM((rows, HW), F32),
                        pltpu.VMEM((rows, HW), F32), pltpu.VMEM((rows, HW), F32), pltpu.VMEM((rows, HW), F32),
                        pltpu.VMEM((2 * rows, HW), BF16), pltpu.VMEM((rows, HW), BF16),
                        pltpu.VMEM((rows // CHUNK * HW, CHUNK), BF16)],
        compiler_params=_cparams("parallel", "arbitrary"),
    )(zc.reshape(B, S, 3 * HW), zg.reshape(B, S, HW), gates.reshape(B, S, GATE_PAD), alog, dtb,
      jnp.tile(norm, HEADS).reshape(1, HW))
    return hc.reshape(B * S, HW), s


def _s5_prep_kernel(are_ref, aim_ref, ldt_ref, bre_ref, bim_ref, lre_ref, lim_ref, bbre_ref, bbim_ref):
    a_re, a_im = are_ref[...], aim_ref[...]
    dt = jnp.exp(ldt_ref[...])
    mag = jnp.exp(a_re * dt)
    lam_re, lam_im = mag * jnp.cos(a_im * dt), mag * jnp.sin(a_im * dt)
    lre_ref[...] = lam_re
    lim_ref[...] = lam_im
    nr, ni = lam_re - 1.0, lam_im
    den = a_re * a_re + a_im * a_im
    coef_re = (nr * a_re + ni * a_im) / den
    coef_im = (ni * a_re - nr * a_im) / den
    b_re, b_im = bre_ref[...], bim_ref[...]
    bbre_ref[...] = coef_re * b_re - coef_im * b_im
    bbim_ref[...] = coef_re * b_im + coef_im * b_re


def s5_params(lp):
    G, N, P = S5_GROUPS, S5_N, S5_P
    row = lambda a: a.astype(F32).reshape(1, G * N)
    to_pn = lambda b: jnp.transpose(b.astype(F32), (2, 0, 1)).reshape(P, G * N)
    shp = [jax.ShapeDtypeStruct((1, G * N), F32)] * 2 + [jax.ShapeDtypeStruct((P, G * N), F32)] * 2
    lam_re, lam_im, bb_re, bb_im = pl.pallas_call(_s5_prep_kernel, out_shape=shp)(
        row(lp['s5_a_re']), row(lp['s5_a_im']), row(jnp.repeat(lp['s5_log_dt'][:, None], N, axis=1)),
        to_pn(lp['s5_b_re']), to_pn(lp['s5_b_im']))
    eye = jnp.eye(S5_GB, dtype=F32)

    def w_in_blocks(bb):
        b4 = bb.reshape(P, S5_NBLK, S5_GB, N)
        return jnp.einsum('pbgn,gh->bgphn', b4, eye).reshape(S5_NBLK, S5_GB * P, S5_GB * N)

    def w_out_blocks(c):
        c4 = c.astype(F32).reshape(S5_NBLK, S5_GB, P, N)
        return jnp.einsum('bgpn,gh->bgnhp', c4, eye).reshape(S5_NBLK, S5_GB * N, S5_GB * P)

    w_in = jnp.concatenate([w_in_blocks(bb_re), w_in_blocks(bb_im)], axis=-1).astype(BF16)
    return {'lam_re': lam_re, 'lam_im': lam_im, 'w_in': w_in,
            'w_out_re': w_out_blocks(lp['s5_c_re']).astype(BF16),
            'w_out_im': (-w_out_blocks(lp['s5_c_im'])).astype(BF16),
            'd': lp['s5_d'].astype(F32).reshape(1, S5_WIDTH), 'w_glu': lp['s5_w_glu'].astype(BF16)}


def _s5_kernel(u_ref, h0r_ref, h0i_ref, lamr_ref, lami_ref, win_ref, wor_ref, woi_ref, d_ref, wglu_ref,
               ys_ref, h1r_ref, h1i_ref, hr_scr, hi_scr, br_scr, bi_scr, *, nseq, rows, bb, lane_blk):
    tb = pl.program_id(0)
    nsteps = nseq * rows // bb
    blk_in, blk_st = S5_GB * S5_P, S5_GB * S5_N

    @pl.when(tb == 0)
    def _():
        hr_scr[...] = h0r_ref[...]
        hi_scr[...] = h0i_ref[...]

    if nseq > 1:
        u = jnp.swapaxes(u_ref[...], 0, 1).reshape(nseq * rows, S5_WIDTH)
    else:
        u = u_ref[0]
    ub = u.astype(BF16)
    for blk in range(S5_NBLK):
        bu = _dot(ub[:, blk * blk_in:(blk + 1) * blk_in], win_ref[blk])
        br_scr[:, blk * blk_st:(blk + 1) * blk_st] = bu[:, 0:blk_st]
        bi_scr[:, blk * blk_st:(blk + 1) * blk_st] = bu[:, blk_st:2 * blk_st]

    for lb in range(S5_STATE // lane_blk):
        ls = slice(lb * lane_blk, (lb + 1) * lane_blk)
        lr = jnp.broadcast_to(lamr_ref[:, ls], (bb, lane_blk))
        li = jnp.broadcast_to(lami_ref[:, ls], (bb, lane_blk))

        def step(t, carry):
            hr, hi = carry
            r = pl.ds(pl.multiple_of(t * bb, bb), bb)
            nr = lr * hr - li * hi + br_scr[r, ls]
            ni = lr * hi + li * hr + bi_scr[r, ls]
            br_scr[r, ls] = nr
            bi_scr[r, ls] = ni
            return nr, ni

        hr, hi = lax.fori_loop(0, nsteps, step, (hr_scr[:, ls], hi_scr[:, ls]), unroll=min(nsteps, 8))
        hr_scr[:, ls] = hr
        hi_scr[:, ls] = hi

    ys = []
    for blk in range(S5_NBLK):
        st = slice(blk * blk_st, (blk + 1) * blk_st)
        ys.append(_dot(br_scr[:, st].astype(BF16), wor_ref[blk]) + _dot(bi_scr[:, st].astype(BF16), woi_ref[blk]))
    gy = jax.nn.gelu(jnp.concatenate(ys, axis=-1) + d_ref[...] * u)
    out = gy * _sigmoid(_dot(gy.astype(BF16), wglu_ref[...]))
    if nseq > 1:
        ys_ref[...] = jnp.swapaxes(out.reshape(rows, nseq, S5_WIDTH), 0, 1)
    else:
        ys_ref[0] = out

    @pl.when(tb == pl.num_programs(0) - 1)
    def _():
        h1r_ref[...] = hr_scr[...]
        h1i_ref[...] = hi_scr[...]


def s5_mixer(u, h0_re, h0_im, sp, tt, single_step):
    B = h0_re.shape[0]
    nseq, S = (1, 1) if single_step else (B, u.shape[1])
    rows = B if single_step else tt
    assert S % tt == 0 and B % SUBLANES == 0
    assert u.shape == ((1, B, S5_WIDTH) if single_step else (B, S, S5_WIDTH))
    lane_blk = max(LANES, min(S5_STATE, (SUBLANES * SUBLANES * LANES) // B))
    full = lambda shape: pl.BlockSpec(shape, lambda t: (0,) * len(shape))
    return pl.pallas_call(
        functools.partial(_s5_kernel, nseq=nseq, rows=rows, bb=B, lane_blk=lane_blk),
        grid=(S // tt,),
        in_specs=[pl.BlockSpec((nseq, rows, S5_WIDTH), lambda t: (0, t, 0)),
                  full((B, S5_STATE)), full((B, S5_STATE)), full((1, S5_STATE)), full((1, S5_STATE)),
                  full(sp['w_in'].shape), full(sp['w_out_re'].shape), full(sp['w_out_im'].shape),
                  full((1, S5_WIDTH)), full((S5_WIDTH, S5_WIDTH))],
        out_specs=[pl.BlockSpec((nseq, rows, S5_WIDTH), lambda t: (0, t, 0)),
                   full((B, S5_STATE)), full((B, S5_STATE))],
        out_shape=[jax.ShapeDtypeStruct(u.shape, F32),
                   jax.ShapeDtypeStruct((B, S5_STATE), F32), jax.ShapeDtypeStruct((B, S5_STATE), F32)],
        scratch_shapes=[pltpu.VMEM((B, S5_STATE), F32), pltpu.VMEM((B, S5_STATE), F32),
                        pltpu.VMEM((nseq * rows, S5_STATE), F32), pltpu.VMEM((nseq * rows, S5_STATE), F32)],
        compiler_params=_cparams("arbitrary"),
    )(u, h0_re, h0_im, sp['lam_re'], sp['lam_im'], sp['w_in'], sp['w_out_re'], sp['w_out_im'], sp['d'], sp['w_glu'])


def _ones_bd():
    r = lax.broadcasted_iota(jnp.int32, (HW, HW), 0) // DH
    c = lax.broadcasted_iota(jnp.int32, (HW, HW), 1) // DH
    return (r == c).astype(BF16)


def _mlstm_sample_kernel(za_ref, gt_ref, bias_ref, norm_ref, c_ref, n_ref, m_ref,
                         ha_ref, c_out, n_out, m_out, q_scr, kw_scr, h_scr):
    za = za_ref[...]
    q_scr[...] = (za[:, 0:HW] * (DH ** -0.5)).T
    k_t = za[:, HW:2 * HW].T
    v_t = za[:, 2 * HW:3 * HW].T
    g_t = (gt_ref[...] + bias_ref[...]).T
    m_out[...] = jnp.zeros_like(m_out)
    for h in range(HEADS):
        hs = slice(h * DH, (h + 1) * DH)
        i_h = g_t[h:h + 1, :]
        bm = _log_sigmoid(g_t[HEADS + h:HEADS + h + 1, :]) + m_ref[h:h + 1, :]
        m_t = jnp.maximum(i_h, bm)
        w_in = jnp.exp(i_h - m_t)
        w_st = jnp.exp(bm - m_t)
        q_h, k_h, v_h = q_scr[hs, :], k_t[hs, :], v_t[hs, :]
        s = jnp.sum(q_h * k_h, axis=0, keepdims=True) * w_in
        kw_scr[hs, :] = k_h * w_in

        def body(d, acc):
            r = h * DH + d
            rows = pl.ds(pl.multiple_of(r * DH, DH), DH)
            c_hd = c_ref[rows, :]
            c_out[rows, :] = w_st * c_hd + kw_scr[pl.ds(r, 1), :] * v_h
            return acc + q_scr[pl.ds(r, 1), :] * c_hd

        qc = lax.fori_loop(0, DH, body, jnp.zeros((DH, za.shape[0]), F32), unroll=4)
        n_h = n_ref[hs, :]
        num = w_st * qc + s * v_h
        den = w_st * jnp.sum(q_h * n_h, axis=0, keepdims=True) + s
        h_scr[hs, :] = num / jnp.maximum(jnp.abs(den), jnp.exp(-m_t))
        n_out[hs, :] = w_st * n_h + kw_scr[hs, :]
        m_out[h:h + 1, :] = m_t
    ha_ref[...] = _head_rms(h_scr[...].T, _ones_bd(), norm_ref[...]) * _sigmoid(za[:, 3 * HW:4 * HW])


def mlstm_sample(za, gates, b_i, b_f, norm, c_t, n_t, m_t):
    B = za.shape[0]
    bias = jnp.zeros((1, GATE_PAD), F32).at[0, 0:HEADS].set(b_i).at[0, HEADS:2 * HEADS].set(b_f)
    shp = lambda *s: jax.ShapeDtypeStruct(s, F32)
    return pl.pallas_call(
        _mlstm_sample_kernel,
        out_shape=[shp(B, HW), shp(HW * DH, B), shp(HW, B), shp(SUBLANES, B)],
        scratch_shapes=[pltpu.VMEM((HW, B), F32), pltpu.VMEM((HW, B), F32), pltpu.VMEM((HW, B), F32)],
        compiler_params=pltpu.CompilerParams(vmem_limit_bytes=VMEM_LIMIT),
    )(za, gates, bias, norm.reshape(1, HW), c_t, n_t, m_t)


def _gdn_sample_kernel(zc_ref, zg_ref, gt_ref, buf_ref, cw_ref, alog_ref, dtb_ref, norm_ref, s_ref,
                       hc_ref, s_out, buf_out, q_scr, k_scr, o_scr):
    W3 = 3 * HW
    x = zc_ref[...]
    y = cw_ref[CONV_K - 1:CONV_K, :] * x
    for j in range(CONV_K - 1):
        y = y + cw_ref[j:j + 1, :] * buf_ref[:, j * W3:(j + 1) * W3]
    buf_out[:, 0:(CONV_K - 2) * W3] = buf_ref[:, W3:(CONV_K - 1) * W3]
    buf_out[:, (CONV_K - 2) * W3:(CONV_K - 1) * W3] = x
    y = _silu(y)
    ones_bd = _ones_bd()
    q_raw, k_raw = y[:, 0:HW], y[:, HW:2 * HW]
    q_scr[...] = (q_raw * lax.rsqrt(_dot_sel_r(q_raw * q_raw, ones_bd) + EPS) * (DH ** -0.5)).T
    k_scr[...] = (k_raw * lax.rsqrt(_dot_sel_r(k_raw * k_raw, ones_bd) + EPS)).T
    v_t = y[:, 2 * HW:3 * HW].T
    gt = gt_ref[...]
    beta_t = _sigmoid(gt).T
    la_t = (-jnp.exp(alog_ref[...]) * _softplus(gt + dtb_ref[...])).T
    nb = x.shape[0]
    for h in range(HEADS):
        hs = slice(h * DH, (h + 1) * DH)
        beta = beta_t[2 * HEADS + h:2 * HEADS + h + 1, :]
        eg = jnp.exp(la_t[3 * HEADS + h:3 * HEADS + h + 1, :])
        q_h, k_h, v_h = q_scr[hs, :], k_scr[hs, :], v_t[hs, :]

        def read(d, acc):
            ks, qs = acc
            r = h * DH + d
            s_hd = s_ref[pl.ds(pl.multiple_of(r * DH, DH), DH), :]
            return ks + k_scr[pl.ds(r, 1), :] * s_hd, qs + q_scr[pl.ds(r, 1), :] * s_hd

        zero = jnp.zeros((DH, nb), F32)
        ks, qs = lax.fori_loop(0, DH, read, (zero, zero), unroll=4)
        u = v_h - eg * ks
        o_scr[hs, :] = eg * qs + (jnp.sum(q_h * k_h, axis=0, keepdims=True) * beta) * u

        def write(d, carry):
            r = h * DH + d
            rows = pl.ds(pl.multiple_of(r * DH, DH), DH)
            s_out[rows, :] = eg * s_ref[rows, :] + (beta * k_scr[pl.ds(r, 1), :]) * u
            return carry

        lax.fori_loop(0, DH, write, 0, unroll=4)
    hc_ref[...] = _head_rms(o_scr[...].T, ones_bd, norm_ref[...]) * _silu(zg_ref[...])


def gdn_sample(zc, zg, gates, buf, conv_w, a_log, dt_bias, norm, s_t):
    B = zc.shape[0]
    alog, dtb = _gdn_gate_rows(a_log, dt_bias)
    shp = lambda *s: jax.ShapeDtypeStruct(s, F32)
    return pl.pallas_call(
        _gdn_sample_kernel,
        out_shape=[shp(B, HW), shp(HW * DH, B), shp(B, (CONV_K - 1) * 3 * HW)],
        scratch_shapes=[pltpu.VMEM((HW, B), F32), pltpu.VMEM((HW, B), F32), pltpu.VMEM((HW, B), F32)],
        compiler_params=pltpu.CompilerParams(vmem_limit_bytes=VMEM_LIMIT),
    )(zc, zg, gates, buf, conv_w, alog, dtb, jnp.tile(norm, HEADS).reshape(1, HW), s_t)


FFN_CHUNK = 1408
IN_SEGMENTS = (4 * HW, S5_WIDTH, 3 * HW, HW, GATE_PAD)


def _tile(n, pref):
    return pref if n % pref == 0 else n


def _regroup_w_in_kernel(w_ref, o_ref):
    a, g2 = 4 * HW, 2 * HEADS
    mid = sum(IN_SEGMENTS[1:4])
    w = w_ref[...]
    o_ref[:, 0:a] = w[:, 0:a].astype(BF16)
    o_ref[:, a:a + mid] = w[:, a + g2:a + g2 + mid].astype(BF16)
    gates = jnp.concatenate([w[:, a:a + g2], w[:, a + g2 + mid:a + 2 * g2 + mid],
                             jnp.zeros((w.shape[0], GATE_PAD - 2 * g2), F32)], axis=1)
    o_ref[:, a + mid:a + mid + GATE_PAD] = gates.astype(BF16)


def regroup_w_in(w_in, tk):
    L, D, n_in = w_in.shape
    n_out = sum(IN_SEGMENTS)
    return pl.pallas_call(
        _regroup_w_in_kernel,
        grid=(L, D // tk),
        in_specs=[pl.BlockSpec((None, tk, n_in), lambda l, i: (l, i, 0))],
        out_specs=pl.BlockSpec((None, tk, n_out), lambda l, i: (l, i, 0)),
        out_shape=jax.ShapeDtypeStruct((L, D, n_out), BF16),
        compiler_params=_cparams("parallel", "parallel"),
    )(w_in)


def prep_weights(p):
    w_in = regroup_w_in(p['w_in'], 256)
    bf = lambda n: p[n].astype(BF16)
    return {'w_in': w_in, 'w_out': bf('w_out'), 'w_mq': bf('w_mq'), 'w_mo': bf('w_mo'), 'w_mk': bf('w_mk'),
            'w_mv': bf('w_mv'), 'w_gate': bf('w_gate'), 'w_up': bf('w_up'), 'w_down': bf('w_down')}


def mixer_prompt(x, lp, W, layer, B, S):
    T = B * S
    za, zu, zc, zg, gates, tail = in_proj_prompt(x, lp['norm_mix'], W['w_in'], layer, lp['gdn_conv_w'], S,
                                                 _tile(S, 1024))
    buf1 = tail[:, SUBLANES - (CONV_K - 1):, :]
    tt = _tile(S, 512)
    ha, c1, n1, m1 = mlstm_prompt(za, gates, lp['mlstm_b_i'], lp['mlstm_b_f'], lp['mlstm_norm'], B, S, tt)
    h0 = jnp.zeros((B, S5_STATE), F32)
    ys3, r1, i1 = s5_mixer(zu.reshape(B, S, S5_WIDTH), h0, h0, lp['s5'], _tile(S, 128), False)
    ys = ys3.reshape(T, S5_WIDTH)
    hc, s1 = gdn_prompt(zc, zg, gates, lp['gdn_a_log'], lp['gdn_dt_bias'], lp['gdn_norm'], B, S, _tile(S, 256))
    return [ha, ys, hc], (c1, n1, m1, r1.reshape(B, S5_GROUPS, S5_N), i1.reshape(B, S5_GROUPS, S5_N), s1, buf1)


def mixer_sample(x, st, lp, W, layer):
    B = x.shape[0]
    c0, n0, m0, r0, i0, s0, buf0 = st
    za, zu, zc, zg, gates = norm_matmul(x, lp['norm_mix'], W['w_in'], layer, IN_SEGMENTS, B)
    m_t = jnp.zeros((SUBLANES, B), F32).at[0:HEADS, :].set(m0.T)
    ha, c1t, n1t, m1t = mlstm_sample(za, gates, lp['mlstm_b_i'], lp['mlstm_b_f'], lp['mlstm_norm'],
                                     c0.reshape(B, HW * DH).T, n0.reshape(B, HW).T, m_t)
    ys3, r1, i1 = s5_mixer(zu.reshape(1, B, S5_WIDTH), r0.reshape(B, S5_STATE), i0.reshape(B, S5_STATE), lp['s5'], 1, True)
    hc, s1t, buf1 = gdn_sample(zc, zg, gates, buf0.reshape(B, (CONV_K - 1) * 3 * HW), lp['gdn_conv_w'],
                               lp['gdn_a_log'], lp['gdn_dt_bias'], lp['gdn_norm'], s0.reshape(B, HW * DH).T)
    x1 = matmul_residual(x, [ha, ys3.reshape(B, S5_WIDTH), hc], W['w_out'], layer, B)
    return x1, (c1t.T.reshape(B, HEADS, DH, DH), n1t.T.reshape(B, HEADS, DH), m1t[0:HEADS, :].T,
                r1.reshape(B, S5_GROUPS, S5_N), i1.reshape(B, S5_GROUPS, S5_N),
                s1t.T.reshape(B, HEADS, DH, DH), buf1.reshape(B, CONV_K - 1, 3 * HW))


def _mem_kv_kernel(x_ref, g_ref, wk_ref, wv_ref, k2_ref, v2_ref, k5_ref, v5_ref):
    xn = _rms(x_ref[...], g_ref[...]).astype(BF16)
    for w_ref, o2_ref, o5_ref in ((wk_ref, k2_ref, k5_ref), (wv_ref, v2_ref, v5_ref)):
        y = _dot(xn, w_ref[...])
        o2_ref[...] = y
        o5_ref[...] = y.reshape(o5_ref.shape)


def mem_kv(mem, norm_mem, wk, wv):
    B, M, D = mem.shape
    L = wk.shape[0]
    dh = D // X_HEADS
    w_spec = pl.BlockSpec((None, D, D), lambda l, b: (l, 0, 0))
    o2_spec = pl.BlockSpec((None, None, M, D), lambda l, b: (l, b, 0, 0))
    o5_spec = pl.BlockSpec((None, None, M, X_HEADS, dh), lambda l, b: (l, b, 0, 0, 0))
    return pl.pallas_call(
        _mem_kv_kernel,
        grid=(L, B),
        in_specs=[pl.BlockSpec((None, M, D), lambda l, b: (b, 0, 0)),
                  pl.BlockSpec((None, 1, D), lambda l, b: (l, 0, 0)), w_spec, w_spec],
        out_specs=[o2_spec, o2_spec, o5_spec, o5_spec],
        out_shape=[jax.ShapeDtypeStruct((L, B, M, D), F32)] * 2
                  + [jax.ShapeDtypeStruct((L, B, M, X_HEADS, dh), F32)] * 2,
        compiler_params=_cparams("parallel", "parallel"),
    )(mem, norm_mem.reshape(L, 1, D), wk, wv)


def xattn_ffn_prompt(x, acts, mk, mv, lp, W, layer, S, norm_final, final):
    T, D = x.shape
    x2 = xattn_prompt(x, acts, W['w_out'], lp['norm_xattn'], W['w_mq'], mk, mv, W['w_mo'], layer, S, _tile(S, 1024))
    return ffn(x2, lp['norm_ffn'], W['w_gate'], W['w_up'], W['w_down'], layer, norm_final, final, _tile(T, 512),
               FFN_CHUNK)


def xattn_ffn_sample(x, ck, cv, lp, W, layer, norm_final, final):
    B, D = x.shape
    (q,) = norm_matmul(x, lp['norm_xattn'], W['w_mq'], layer, (D,), B)
    o = xattn_sample(q, ck, cv, layer, 4)
    x2 = matmul_residual(x, [o], W['w_mo'], layer, B)
    return ffn(x2, lp['norm_ffn'], W['w_gate'], W['w_up'], W['w_down'], layer, norm_final, final, B, FFN_CHUNK)


LAYER_PARAMS = ('norm_mix', 'w_in', 'w_out', 'mlstm_b_i', 'mlstm_b_f', 'mlstm_norm', 's5_a_re', 's5_a_im', 's5_log_dt',
                's5_b_re', 's5_b_im', 's5_c_re', 's5_c_im', 's5_d', 's5_w_glu', 'gdn_conv_w', 'gdn_a_log',
                'gdn_dt_bias', 'gdn_norm', 'norm_xattn', 'norm_mem', 'w_mq', 'w_mk', 'w_mv', 'w_mo', 'norm_ffn',
                'w_gate', 'w_up', 'w_down')


def kernel(x_prompt, x_sample, mem_prompt, cache_mem_k, cache_mem_v, state_mlstm_c, state_mlstm_n, state_mlstm_m, state_s5_re, state_s5_im, state_gdn, state_gdn_conv, norm_mix, w_in, w_out, mlstm_b_i, mlstm_b_f, mlstm_norm, s5_a_re, s5_a_im, s5_log_dt, s5_b_re, s5_b_im, s5_c_re, s5_c_im, s5_d, s5_w_glu, gdn_conv_w, gdn_a_log, gdn_dt_bias, gdn_norm, norm_xattn, norm_mem, w_mq, w_mk, w_mv, w_mo, norm_ffn, w_gate, w_up, w_down, norm_final):
    stacked = dict(norm_mix=norm_mix, w_in=w_in, w_out=w_out, mlstm_b_i=mlstm_b_i, mlstm_b_f=mlstm_b_f,
                   mlstm_norm=mlstm_norm, s5_a_re=s5_a_re, s5_a_im=s5_a_im, s5_log_dt=s5_log_dt, s5_b_re=s5_b_re,
                   s5_b_im=s5_b_im, s5_c_re=s5_c_re, s5_c_im=s5_c_im, s5_d=s5_d, s5_w_glu=s5_w_glu,
                   gdn_conv_w=gdn_conv_w, gdn_a_log=gdn_a_log, gdn_dt_bias=gdn_dt_bias, gdn_norm=gdn_norm,
                   norm_xattn=norm_xattn, norm_mem=norm_mem, w_mq=w_mq, w_mk=w_mk, w_mv=w_mv, w_mo=w_mo,
                   norm_ffn=norm_ffn, w_gate=w_gate, w_up=w_up, w_down=w_down)
    B, S, D = x_prompt.shape
    Bs = x_sample.shape[0]
    M = mem_prompt.shape[1]
    depth = w_in.shape[0]
    xp = x_prompt.reshape(B * S, D)
    xs = x_sample.reshape(Bs, D)
    cache_k, cache_v = cache_mem_k, cache_mem_v
    st_p, st_s = [], []
    W = prep_weights(stacked)
    mk, mv, mem_k, mem_v = mem_kv(mem_prompt, norm_mem, W['w_mk'], W['w_mv'])
    for l in range(depth):
        lp = {n: stacked[n][l] for n in LAYER_PARAMS if n not in W}
        lp['s5'] = s5_params(lp)
        last = l == depth - 1
        acts, sp = mixer_prompt(xp, lp, W, l, B, S)
        xp = xattn_ffn_prompt(xp, acts, mk, mv, lp, W, l, S, norm_final, last)
        st_in = (state_mlstm_c[l], state_mlstm_n[l], state_mlstm_m[l], state_s5_re[l], state_s5_im[l],
                 state_gdn[l], state_gdn_conv[l])
        xs, ss = mixer_sample(xs, st_in, lp, W, l)
        xs = xattn_ffn_sample(xs, cache_k, cache_v, lp, W, l, norm_final, last)
        st_p.append(sp)
        st_s.append(ss)
    stack = lambda lst: [jnp.stack([st[i] for st in lst]) for i in range(7)]
    return (xp.reshape(B, S, D), xs.reshape(Bs, 1, D), mem_k, mem_v,
            *stack(st_p), *stack(st_s))
```

```python
import functools
import math

import jax
import jax.numpy as jnp
from jax import lax
from jax.experimental import pallas as pl
from jax.experimental.pallas import tpu as pltpu

F32 = jnp.float32
BF16 = jnp.bfloat16
EPS = 1e-6

HEADS = 4
DH = 64
HW = HEADS * DH
CHUNK = 64
S5_P = 16
S5_N = 64
S5_GROUPS = 32
S5_WIDTH = S5_GROUPS * S5_P
S5_STATE = S5_GROUPS * S5_N
S5_GB = 8
S5_NBLK = S5_GROUPS // S5_GB
S5_SLAB_ROWS = 256
CONV_K = 4
GDN_SEQS = 4
X_HEADS = 4
GATE_PAD = 128
LANES = 128
SUBLANES = 8
VMEM_LIMIT = 48 * 1024 * 1024


def _cparams(*sem):
    return pltpu.CompilerParams(dimension_semantics=sem, vmem_limit_bytes=VMEM_LIMIT)


def _rms(x, g_row):
    return x * lax.rsqrt(jnp.mean(x * x, axis=-1, keepdims=True) + EPS) * g_row


def _dot(a, b):
    return jnp.dot(a, b, preferred_element_type=F32)


def _dot_nt(a, b):
    return lax.dot_general(a, b, (((1,), (1,)), ((), ())), preferred_element_type=F32)


def _sigmoid(x):
    return 1.0 / (1.0 + jnp.exp(-x))


def _silu(x):
    return x * _sigmoid(x)


def _softplus(x):
    return jnp.maximum(x, 0.0) + jnp.log1p(jnp.exp(-jnp.abs(x)))


def _log_sigmoid(x):
    return jnp.minimum(x, 0.0) - jnp.log1p(jnp.exp(-jnp.abs(x)))


def _norm_matmul_kernel(x_ref, g_ref, w_ref, *out_refs, splits):
    xn = _rms(x_ref[...], g_ref[...]).astype(BF16)
    off = 0
    for o_ref, n in zip(out_refs, splits):
        o_ref[...] = _dot(xn, w_ref[:, off:off + n])
        off += n


def _layer_weight(w_all, layer):
    return pl.BlockSpec((None,) + w_all.shape[1:], lambda *_: (layer, 0, 0), pipeline_mode=pl.Buffered(1))


def norm_matmul(x, g, w_all, layer, splits, tm):
    T, D = x.shape
    N = w_all.shape[2]
    assert sum(splits) == N and T % tm == 0
    return pl.pallas_call(
        functools.partial(_norm_matmul_kernel, splits=tuple(splits)),
        grid=(T // tm,),
        in_specs=[pl.BlockSpec((tm, D), lambda i: (i, 0)),
                  pl.BlockSpec((1, D), lambda i: (0, 0)),
                  _layer_weight(w_all, layer)],
        out_specs=[pl.BlockSpec((tm, n), lambda i: (i, 0)) for n in splits],
        out_shape=[jax.ShapeDtypeStruct((T, n), F32) for n in splits],
        compiler_params=_cparams("parallel"),
    )(x, g.reshape(1, D), w_all)


def _in_proj_prompt_kernel(x_ref, g_ref, w_ref, cw_ref, za_ref, zu_ref, zc_ref, zg_ref, gt_ref, tail_ref, xp_scr,
                           *, tiles_per_seq):
    tm = x_ref.shape[0]
    pad = SUBLANES
    xn = _rms(x_ref[...], g_ref[...]).astype(BF16)
    offs = [sum(IN_SEGMENTS[:s]) for s in range(len(IN_SEGMENTS))]

    @pl.when(pl.program_id(0) % tiles_per_seq == 0)
    def _():
        xp_scr[0:pad, :] = jnp.zeros((pad, xp_scr.shape[1]), F32)

    xp_scr[pad:pad + tm, :] = _dot(xn, w_ref[:, offs[2]:offs[2] + IN_SEGMENTS[2]])
    others = ((0, za_ref), (1, zu_ref), (3, zg_ref), (4, gt_ref))
    rows = tm // len(others)
    for r, (s, o_ref) in enumerate(others):
        y = cw_ref[CONV_K - 1:CONV_K, :] * xp_scr[pl.ds(pad + r * rows, rows), :]
        for j in range(CONV_K - 1):
            y = y + cw_ref[j:j + 1, :] * xp_scr[pl.ds(pad + r * rows - (CONV_K - 1) + j, rows), :]
        zc_ref[pl.ds(r * rows, rows), :] = _silu(y)
        o_ref[...] = _dot(xn, w_ref[:, offs[s]:offs[s] + IN_SEGMENTS[s]])
    tail = xp_scr[tm:tm + pad, :]
    xp_scr[0:pad, :] = tail
    tail_ref[0] = tail


def in_proj_prompt(x, g, w_all, layer, conv_w, seq, tm):
    T, D = x.shape
    assert seq % tm == 0 and T % seq == 0
    wc = IN_SEGMENTS[2]
    outs = pl.pallas_call(
        functools.partial(_in_proj_prompt_kernel, tiles_per_seq=seq // tm),
        grid=(T // tm,),
        in_specs=[pl.BlockSpec((tm, D), lambda i: (i, 0)),
                  pl.BlockSpec((1, D), lambda i: (0, 0)),
                  _layer_weight(w_all, layer),
                  pl.BlockSpec((CONV_K, wc), lambda i: (0, 0))],
        out_specs=[pl.BlockSpec((tm, n), lambda i: (i, 0)) for n in IN_SEGMENTS]
                  + [pl.BlockSpec((1, SUBLANES, wc), lambda i: (i // (seq // tm), 0, 0))],
        out_shape=[jax.ShapeDtypeStruct((T, n), F32) for n in IN_SEGMENTS]
                  + [jax.ShapeDtypeStruct((T // seq, SUBLANES, wc), F32)],
        scratch_shapes=[pltpu.VMEM((tm + SUBLANES, wc), F32)],
        compiler_params=_cparams("arbitrary"),
    )(x, g.reshape(1, D), w_all, conv_w)
    return outs


def _matmul_residual_kernel(x_ref, *refs, ksplits):
    a_refs, w_ref, o_ref = refs[:-2], refs[-2], refs[-1]
    acc = x_ref[...]
    off = 0
    for a_ref, k in zip(a_refs, ksplits):
        acc = acc + _dot(a_ref[...].astype(BF16), w_ref[off:off + k, :])
        off += k
    o_ref[...] = acc


def matmul_residual(x, acts, w_all, layer, tm):
    T, D = x.shape
    ks = tuple(a.shape[1] for a in acts)
    K = w_all.shape[1]
    assert sum(ks) == K and T % tm == 0
    return pl.pallas_call(
        functools.partial(_matmul_residual_kernel, ksplits=ks),
        grid=(T // tm,),
        in_specs=[pl.BlockSpec((tm, D), lambda i: (i, 0))]
                 + [pl.BlockSpec((tm, k), lambda i: (i, 0)) for k in ks]
                 + [_layer_weight(w_all, layer)],
        out_specs=pl.BlockSpec((tm, D), lambda i: (i, 0)),
        out_shape=jax.ShapeDtypeStruct((T, D), F32),
        compiler_params=_cparams("parallel"),
    )(x, *acts, w_all)


def _ffn_kernel(x_ref, g_ref, wg_ref, wu_ref, wd_ref, gf_ref, o_ref, *, final_norm, tf):
    x = x_ref[...]
    h = _rms(x, g_ref[...]).astype(BF16)
    y = x
    for j in range(wg_ref.shape[1] // tf):
        cols = slice(j * tf, (j + 1) * tf)
        a = _dot(h, wg_ref[:, cols])
        b = _dot(h, wu_ref[:, cols])
        y = y + _dot((_silu(a) * b).astype(BF16), wd_ref[cols, :])
    if final_norm:
        y = _rms(y, gf_ref[...])
    o_ref[...] = y


def _resident(shape):
    return pl.BlockSpec(shape, lambda *_: (0,) * len(shape), pipeline_mode=pl.Buffered(1))


def ffn(x, g, wg, wu, wd, layer, g_final, final_norm, tm, tf):
    T, D = x.shape
    F = wg.shape[2]
    assert T % tm == 0 and F % tf == 0
    return pl.pallas_call(
        functools.partial(_ffn_kernel, final_norm=final_norm, tf=tf),
        grid=(T // tm,),
        in_specs=[pl.BlockSpec((tm, D), lambda i: (i, 0)),
                  _resident((1, D)), _layer_weight(wg, layer), _layer_weight(wu, layer), _layer_weight(wd, layer),
                  _resident((1, D))],
        out_specs=pl.BlockSpec((tm, D), lambda i: (i, 0)),
        out_shape=jax.ShapeDtypeStruct((T, D), F32),
        compiler_params=_cparams("parallel"),
    )(x, g.reshape(1, D), wg, wu, wd, g_final.reshape(1, D))


def _softmax_rows(s):
    e = jnp.exp(s - jnp.max(s, axis=-1, keepdims=True))
    return e / jnp.sum(e, axis=-1, keepdims=True)


def _xattn_prompt_kernel(x_ref, *refs, dh, n_acts):
    a_refs = refs[:n_acts]
    wmix_ref, g_ref, wq_ref, k_ref, v_ref, wo_ref, o_ref = refs[n_acts:]
    scale = dh ** -0.5
    x = x_ref[...]
    off = 0
    for a_ref in a_refs:
        x = x + _dot(a_ref[...].astype(BF16), wmix_ref[off:off + a_ref.shape[1], :])
        off += a_ref.shape[1]
    q = _dot(_rms(x, g_ref[...]).astype(BF16), wq_ref[...]).astype(BF16)
    sl = [slice(h * dh, (h + 1) * dh) for h in range(X_HEADS)]
    s = [_dot_nt(q[:, c], k_ref[0, :, c].astype(BF16)) * scale for c in sl]
    p = [_softmax_rows(sh).astype(BF16) for sh in s]
    heads = [_dot(ph, v_ref[0, :, c].astype(BF16)).astype(BF16) for ph, c in zip(p, sl)]
    o_ref[...] = x + _dot(jnp.concatenate(heads, axis=-1), wo_ref[...])


def xattn_prompt(x, acts, wmix, g, wq, mk, mv, wo, layer, seq, tq):
    T, D = x.shape
    _, B, M, _ = mk.shape
    nt = seq // tq
    rows = lambda w: pl.BlockSpec((tq, w), lambda b, t: (b * nt + t, 0))
    return pl.pallas_call(
        functools.partial(_xattn_prompt_kernel, dh=D // X_HEADS, n_acts=len(acts)),
        grid=(B, nt),
        in_specs=[rows(D)] + [rows(a.shape[1]) for a in acts]
                 + [_layer_weight(wmix, layer),
                    pl.BlockSpec((1, D), lambda b, t: (0, 0)),
                    _layer_weight(wq, layer),
                    pl.BlockSpec((None, 1, M, D), lambda b, t: (layer, b, 0, 0)),
                    pl.BlockSpec((None, 1, M, D), lambda b, t: (layer, b, 0, 0)),
                    _layer_weight(wo, layer)],
        out_specs=rows(D),
        out_shape=jax.ShapeDtypeStruct((T, D), F32),
        compiler_params=_cparams("parallel", "parallel"),
    )(x, *acts, wmix, g.reshape(1, D), wq, mk, mv, wo)


def _xattn_sample_kernel(q_ref, k_ref, v_ref, o_ref, *, sb):
    M, H, dh = k_ref.shape[1:]
    scale = dh ** -0.5
    row = lax.broadcasted_iota(jnp.int32, (SUBLANES, M * H), 0)
    col_head = lax.broadcasted_iota(jnp.int32, (SUBLANES, M * H), 1) % H
    own = (row % H) == col_head
    pad = jnp.zeros((SUBLANES - H, dh), F32)
    q8 = [jnp.concatenate([q_ref[i], pad], axis=0).astype(BF16) for i in range(sb)]
    s = [_dot_nt(q8[i], k_ref[i].reshape(M * H, dh).astype(BF16)) * scale for i in range(sb)]
    p = [_softmax_rows(jnp.where(own, si, -jnp.inf)).astype(BF16) for si in s]
    for i in range(sb):
        o_ref[i] = _dot(p[i], v_ref[i].reshape(M * H, dh).astype(BF16))[0:H]


def xattn_sample(q, ck, cv, layer, sb):
    B, D = q.shape
    _, _, M, H, dh = ck.shape
    out = pl.pallas_call(
        functools.partial(_xattn_sample_kernel, sb=sb),
        grid=(B // sb,),
        in_specs=[pl.BlockSpec((sb, H, dh), lambda i: (i, 0, 0)),
                  pl.BlockSpec((None, sb, M, H, dh), lambda i: (layer, i, 0, 0, 0)),
                  pl.BlockSpec((None, sb, M, H, dh), lambda i: (layer, i, 0, 0, 0))],
        out_specs=pl.BlockSpec((sb, H, dh), lambda i: (i, 0, 0)),
        out_shape=jax.ShapeDtypeStruct((B, H, dh), F32),
        compiler_params=_cparams("parallel"),
    )(q.reshape(B, H, dh), ck, cv)
    return out.reshape(B, D)


def _unrolled(n, body, carry):
    for i in range(n):
        carry = body(i, carry)
    return carry


def _lane_cat_masks(L):
    row = lax.broadcasted_iota(jnp.int32, (L, HW), 0)
    j = lax.broadcasted_iota(jnp.int32, (L, HW), 1) % DH
    r2 = lax.broadcasted_iota(jnp.int32, (HW, HW), 0) // DH
    c2 = lax.broadcasted_iota(jnp.int32, (HW, HW), 1) // DH
    return row >= j, row > j, row == j, r2 == c2


def _expand_bd(x, bd):
    return jnp.where(bd, jnp.concatenate([x] * HEADS, axis=0), jnp.zeros((), x.dtype))


def _seg_reduce(x, op, fill):
    lo = lax.broadcasted_iota(jnp.int32, (x.shape[0], LANES), 1) < DH
    parts = []
    for c in range(HW // LANES):
        xh = x[:, c * LANES:(c + 1) * LANES]
        a = op(jnp.where(lo, xh, fill), axis=-1, keepdims=True)
        b = op(jnp.where(lo, fill, xh), axis=-1, keepdims=True)
        parts.append(jnp.where(lo, a, b))
    return jnp.concatenate(parts, axis=-1)


def _head_expander(first_lane):
    r = lax.broadcasted_iota(jnp.int32, (GATE_PAD, HW), 0)
    c = lax.broadcasted_iota(jnp.int32, (GATE_PAD, HW), 1) // DH
    return (r == c + first_lane).astype(BF16)


def _chunk_cumsum(x):
    tt, w = x.shape
    g = SUBLANES
    x3 = x.reshape(tt // g, g, w)
    sub = lax.broadcasted_iota(jnp.int32, x3.shape, 1)
    s = 1
    while s < g:
        x3 = x3 + jnp.where(sub >= s, pltpu.roll(x3, s, 1), 0.0)
        s *= 2
    per = CHUNK // g
    x4 = x3.reshape(tt // CHUNK, per, g, w)
    acc = jnp.zeros((tt // CHUNK, 1, 1, w), F32)
    parts = []
    for i in range(per):
        parts.append(x4[:, i:i + 1] + acc)
        acc = acc + x4[:, i:i + 1, g - 1:g, :]
    return jnp.concatenate(parts, axis=1).reshape(tt, w)


def _split3(x):
    hi = x.astype(BF16)
    r = x - hi.astype(F32)
    mid = r.astype(BF16)
    return hi, mid, (r - mid.astype(F32)).astype(BF16)


def _dot_sel_r(x, sel):
    hi, mid, lo = _split3(x)
    return (_dot(lo, sel) + _dot(mid, sel)) + _dot(hi, sel)


def _col_to_row(x, eye):
    return jnp.sum(jnp.where(eye, x, 0.0), axis=0, keepdims=True)


def _head_rms(x, ones_bd, g_row):
    ms = _dot_sel_r(x * x, ones_bd) * (1.0 / DH)
    return x * lax.rsqrt(ms + EPS) * g_row


def _mlstm_prompt_kernel(za_ref, gt_ref, bias_ref, norm_ref, ha_ref, c_ref, n_ref, m_ref,
                         c_scr, n_scr, m_scr, ix_scr, bx_scr, *, nchunks, group):
    tb = pl.program_id(1)
    L = CHUNK
    tril, _, eye, bd = _lane_cat_masks(L)
    ones_bd = bd.astype(BF16)

    @pl.when(tb == 0)
    def _():
        c_scr[...] = jnp.zeros_like(c_scr)
        n_scr[...] = jnp.zeros_like(n_scr)
        m_scr[...] = jnp.zeros_like(m_scr)

    gt = gt_ref[...] + bias_ref[...]
    b_cols = _chunk_cumsum(_log_sigmoid(gt))
    ix_scr[...] = _dot_sel_r(gt, _head_expander(0))
    bx_scr[...] = _dot_sel_r(b_cols, _head_expander(HEADS))

    def chunks(gi, carry):
        c_bd, n_row, m_x = carry
        rows = [pl.ds(pl.multiple_of((gi * group + j) * L, L), L) for j in range(group)]
        q = [(za_ref[r, 0:HW] * (DH ** -0.5)).astype(BF16) for r in rows]
        k = [za_ref[r, HW:2 * HW] for r in rows]
        v = [za_ref[r, 2 * HW:3 * HW].astype(BF16) for r in rows]
        i_x = [ix_scr[r, :] for r in rows]
        b_x = [bx_scr[r, :] for r in rows]

        m_in, n_in, kw, decay = [], [], [], []
        for kk, ii, bb in zip(k, i_x, b_x):
            b_last = bb[L - 1:L, :]
            g_x = b_last - bb + ii
            m_new = jnp.maximum(b_last + m_x, jnp.max(g_x, axis=0, keepdims=True))
            kw.append(kk * jnp.exp(g_x - m_new))
            decay.append(jnp.exp(b_last + m_x - m_new))
            m_in.append(m_x)
            n_in.append(n_row)
            n_row = decay[-1] * n_row + jnp.sum(kw[-1], axis=0, keepdims=True)
            m_x = m_new

        d_intra = [jnp.where(tril, bb - _col_to_row(bb, eye) + _col_to_row(ii, eye), -jnp.inf) for bb, ii in zip(b_x, i_x)]
        d_inter = [bb + mm for bb, mm in zip(b_x, m_in)]
        m_t = [jnp.maximum(_seg_reduce(di, jnp.max, -jnp.inf), de) for di, de in zip(d_intra, d_inter)]
        w_inter = [jnp.exp(de - mt) for de, mt in zip(d_inter, m_t)]
        s = [_dot_nt(qq, _expand_bd(kk.astype(BF16), bd)) * jnp.exp(di - mt) for qq, kk, di, mt in zip(q, k, d_intra, m_t)]
        kv = [jnp.where(bd, _dot(kwj.T.astype(BF16), vv), 0.0) for kwj, vv in zip(kw, v)]
        c_in = []
        for dj, kvj in zip(decay, kv):
            c_in.append(c_bd)
            c_bd = dj * c_bd + kvj
        num = [wi * _dot(qq, cc.astype(BF16)) + _dot(ss.astype(BF16), _expand_bd(vv, bd))
               for wi, qq, cc, ss, vv in zip(w_inter, q, c_in, s, v)]
        den = [wi * _dot((qq.astype(F32) * nn).astype(BF16), ones_bd) + _seg_reduce(ss, jnp.sum, 0.0)
               for wi, qq, nn, ss in zip(w_inter, q, n_in, s)]
        for r, nu, de, mt in zip(rows, num, den, m_t):
            hh = nu / jnp.maximum(jnp.abs(de), jnp.exp(-mt))
            ha_ref[r, :] = _head_rms(hh, ones_bd, norm_ref[...]) * _sigmoid(za_ref[r, 3 * HW:4 * HW])
        return c_bd, n_row, m_x

    c_bd, n_row, m_x = _unrolled(nchunks // group, chunks, (c_scr[...], n_scr[...], m_scr[...]))
    c_scr[...] = c_bd
    n_scr[...] = n_row
    m_scr[...] = m_x

    @pl.when(tb == pl.num_programs(1) - 1)
    def _():
        for h in range(HEADS):
            c_ref[0, h] = c_scr[h * DH:(h + 1) * DH, h * DH:(h + 1) * DH]
        n_ref[0] = n_scr[...]
        m_ref[0] = m_scr[...]


def mlstm_prompt(za, gates, b_i, b_f, norm, B, S, tt):
    assert S % tt == 0 and tt % CHUNK == 0
    nt = S // tt
    bias = jnp.zeros((1, GATE_PAD), F32).at[0, 0:HEADS].set(b_i).at[0, HEADS:2 * HEADS].set(b_f)
    ha, c, n, m = pl.pallas_call(
        functools.partial(_mlstm_prompt_kernel, nchunks=tt // CHUNK, group=math.gcd(tt // CHUNK, 4)),
        grid=(B, nt),
        in_specs=[pl.BlockSpec((tt, 4 * HW), lambda b, t: (b * nt + t, 0)),
                  pl.BlockSpec((tt, GATE_PAD), lambda b, t: (b * nt + t, 0)),
                  pl.BlockSpec((1, GATE_PAD), lambda b, t: (0, 0)),
                  pl.BlockSpec((1, HW), lambda b, t: (0, 0))],
        out_specs=[pl.BlockSpec((tt, HW), lambda b, t: (b * nt + t, 0)),
                   pl.BlockSpec((1, HEADS, DH, DH), lambda b, t: (b, 0, 0, 0)),
                   pl.BlockSpec((1, 1, HW), lambda b, t: (b, 0, 0)),
                   pl.BlockSpec((1, 1, HW), lambda b, t: (b, 0, 0))],
        out_shape=[jax.ShapeDtypeStruct((B * S, HW), F32),
                   jax.ShapeDtypeStruct((B, HEADS, DH, DH), F32),
                   jax.ShapeDtypeStruct((B, 1, HW), F32),
                   jax.ShapeDtypeStruct((B, 1, HW), F32)],
        scratch_shapes=[pltpu.VMEM((HW, HW), F32), pltpu.VMEM((1, HW), F32), pltpu.VMEM((1, HW), F32),
                        pltpu.VMEM((tt, HW), F32), pltpu.VMEM((tt, HW), F32)],
        compiler_params=_cparams("parallel", "arbitrary"),
    )(za, gates, bias, norm.reshape(1, HW))
    return ha, c, n.reshape(B, HEADS, DH), m[:, 0, ::DH]


def _gdn_prompt_kernel(zc_ref, zg_ref, gt_ref, alog_ref, dtb_ref, norm_ref,
                       hc_ref, s_ref,
                       s_scr, q_scr, k_scr, v_scr, beta_scr, g_scr, uv_scr, wq_scr, qkm_scr, kwt_scr,
                       *, nseq, nchunks, group):
    tb = pl.program_id(1)
    L = CHUNK
    tt = nchunks * L
    tril, strict, eye, bd = _lane_cat_masks(L)
    ones_bd = bd.astype(BF16)

    @pl.when(tb == 0)
    def _():
        s_scr[...] = jnp.zeros_like(s_scr)

    for i in range(nseq):
        sr = slice(i * tt, (i + 1) * tt)
        y = zc_ref[i]
        q_raw, k_raw = y[:, 0:HW], y[:, HW:2 * HW]
        q_scr[sr, :] = (q_raw * lax.rsqrt(_dot_sel_r(q_raw * q_raw, ones_bd) + EPS) * (DH ** -0.5)).astype(BF16)
        k_scr[sr, :] = k_raw * lax.rsqrt(_dot_sel_r(k_raw * k_raw, ones_bd) + EPS)
        v_scr[sr, :] = y[:, 2 * HW:3 * HW]
        gt = gt_ref[i]
        beta_scr[sr, :] = _dot_sel_r(_sigmoid(gt), _head_expander(2 * HEADS))
        la_cols = -jnp.exp(alog_ref[...]) * _softplus(gt + dtb_ref[...])
        g_scr[sr, :] = _dot_sel_r(_chunk_cumsum(la_cols), _head_expander(3 * HEADS))

    def prepare(gi, carry):
        cis = [gi * group + j for j in range(group)]
        rows = [pl.ds(pl.multiple_of(ci * L, L), L) for ci in cis]
        k = [k_scr[r, :] for r in rows]
        g_x = [g_scr[r, :] for r in rows]
        beta_row = [_col_to_row(beta_scr[r, :], eye) for r in rows]
        dec_incl = [jnp.where(tril, jnp.exp(jnp.where(tril, g - _col_to_row(g, eye), 0.0)), 0.0) for g in g_x]
        k_bd = [_expand_bd(kk.astype(BF16), bd) for kk in k]
        n0 = [-(jnp.where(strict, d, 0.0) * _dot_nt(kk.astype(BF16), kbd) * br)
              for d, kk, kbd, br in zip(dec_incl, k, k_bd, beta_row)]
        for r, d, kbd, br in zip(rows, dec_incl, k_bd, beta_row):
            qkm_scr[r, :] = (_dot_nt(q_scr[r, :], kbd) * d * br).astype(BF16)

        p = [_dot(n.astype(BF16), _expand_bd(n.astype(BF16), bd)) for n in n0]
        m = n0
        steps = int(math.log2(L)) - 1
        for i in range(steps):
            p_bd = [_expand_bd(pp.astype(BF16), bd) for pp in p]
            if i < steps - 1:
                pm = [_dot(jnp.concatenate([pp, mm], axis=0).astype(BF16), pbd) for pp, mm, pbd in zip(p, m, p_bd)]
                p_next, mp = [x[0:L] for x in pm], [x[L:2 * L] for x in pm]
            else:
                p_next, mp = None, [_dot(mm.astype(BF16), pbd) for mm, pbd in zip(m, p_bd)]
            m = [mm + pp + x for mm, pp, x in zip(m, p, mp)]
            p = p_next

        for ci, r, kk, g, mm in zip(cis, rows, k, g_x, m):
            v = v_scr[r, :]
            egk = jnp.exp(g) * kk
            rhs_bd = jnp.concatenate([_expand_bd(v.astype(BF16), bd), _expand_bd(egk.astype(BF16), bd)], axis=1)
            mr = _dot(mm.astype(BF16), rhs_bd)
            uv_scr[r, :] = v + mr[:, 0:HW]
            wq_rows = pl.multiple_of(ci * 2 * L, 2 * L)
            wq_scr[pl.ds(wq_rows, L), :] = (egk + mr[:, HW:2 * HW]).astype(BF16)
            wq_scr[pl.ds(wq_rows + L, L), :] = q_scr[r, :]
            kw = kk * (jnp.exp(g[L - 1:L, :] - g) * beta_scr[r, :])
            kwt_scr[pl.ds(pl.multiple_of(ci * HW, HW), HW), :] = kw.T.astype(BF16)
        return carry

    _unrolled(nseq * nchunks // group, prepare, 0)

    def advance(c, states):
        cis = [i * nchunks + c for i in range(nseq)]
        rows = [pl.ds(pl.multiple_of(ci * L, L), L) for ci in cis]
        wqs = [_dot(wq_scr[pl.ds(pl.multiple_of(ci * 2 * L, 2 * L), 2 * L), :], s.astype(BF16))
               for ci, s in zip(cis, states)]
        ub = [(uv_scr[r, :] - x[0:L]).astype(BF16) for r, x in zip(rows, wqs)]
        new = [jnp.exp(g_scr[pl.ds(pl.multiple_of(ci * L + L - SUBLANES, SUBLANES), SUBLANES), :][SUBLANES - 1:, :]) * s
               + jnp.where(bd, _dot(kwt_scr[pl.ds(pl.multiple_of(ci * HW, HW), HW), :], u), 0.0)
               for ci, s, u in zip(cis, states, ub)]
        for i, (r, x, u) in enumerate(zip(rows, wqs, ub)):
            o = jnp.exp(g_scr[r, :]) * x[L:2 * L] + _dot(qkm_scr[r, :], _expand_bd(u, bd))
            hc_ref[i, pl.ds(pl.multiple_of(c * L, L), L), :] = (
                _head_rms(o, ones_bd, norm_ref[...]) * _silu(zg_ref[i, pl.ds(pl.multiple_of(c * L, L), L), :]))
        return tuple(new)

    states = _unrolled(nchunks, advance, tuple(s_scr[i] for i in range(nseq)))
    for i in range(nseq):
        s_scr[i] = states[i]

    @pl.when(tb == pl.num_programs(1) - 1)
    def _():
        for i in range(nseq):
            for h in range(HEADS):
                s_ref[i, h] = s_scr[i, h * DH:(h + 1) * DH, h * DH:(h + 1) * DH]


def _gdn_gate_rows(a_log, dt_bias):
    z = jnp.zeros((1, GATE_PAD), F32)
    return (z.at[0, 3 * HEADS:4 * HEADS].set(a_log), z.at[0, 3 * HEADS:4 * HEADS].set(dt_bias))


def gdn_prompt(zc, zg, gates, a_log, dt_bias, norm, B, S, tt):
    assert S % tt == 0 and tt % CHUNK == 0
    nt = S // tt
    nseq = math.gcd(B, GDN_SEQS)
    rows = nseq * tt
    alog, dtb = _gdn_gate_rows(a_log, dt_bias)
    blk = lambda w: pl.BlockSpec((nseq, tt, w), lambda b, t: (b, t, 0))
    row = lambda w: pl.BlockSpec((1, w), lambda b, t: (0, 0))
    hc, s = pl.pallas_call(
        functools.partial(_gdn_prompt_kernel, nseq=nseq, nchunks=tt // CHUNK, group=math.gcd(rows // CHUNK, 8)),
        grid=(B // nseq, nt),
        in_specs=[blk(3 * HW), blk(HW), blk(GATE_PAD), row(GATE_PAD), row(GATE_PAD), row(HW)],
        out_specs=[blk(HW), pl.BlockSpec((nseq, HEADS, DH, DH), lambda b, t: (b, 0, 0, 0))],
        out_shape=[jax.ShapeDtypeStruct((B, S, HW), F32), jax.ShapeDtypeStruct((B, HEADS, DH, DH), F32)],
        scratch_shapes=[pltpu.VMEM((nseq, HW, HW), F32),
                        pltpu.VMEM((rows, HW), BF16), pltpu.VMEM((rows, HW), F32), pltpu.VMEM((rows, HW), F32),
                        pltpu.VMEM((rows, HW), F32), pltpu.VMEM((rows, HW), F32), pltpu.VMEM((rows, HW), F32),
                        pltpu.VMEM((2 * rows, HW), BF16), pltpu.VMEM((rows, HW), BF16),
                        pltpu.VMEM((rows // CHUNK * HW, CHUNK), BF16)],
        compiler_params=_cparams("parallel", "arbitrary"),
    )(zc.reshape(B, S, 3 * HW), zg.reshape(B, S, HW), gates.reshape(B, S, GATE_PAD), alog, dtb,
      jnp.tile(norm, HEADS).reshape(1, HW))
    return hc.reshape(B * S, HW), s


def _s5_prep_kernel(are_ref, aim_ref, ldt_ref, bre_ref, bim_ref, lre_ref, lim_ref, bbre_ref, bbim_ref):
    a_re, a_im = are_ref[...], aim_ref[...]
    dt = jnp.exp(ldt_ref[...])
    mag = jnp.exp(a_re * dt)
    lam_re, lam_im = mag * jnp.cos(a_im * dt), mag * jnp.sin(a_im * dt)
    lre_ref[...] = lam_re
    lim_ref[...] = lam_im
    nr, ni = lam_re - 1.0, lam_im
    den = a_re * a_re + a_im * a_im
    coef_re = (nr * a_re + ni * a_im) / den
    coef_im = (ni * a_re - nr * a_im) / den
    b_re, b_im = bre_ref[...], bim_ref[...]
    bbre_ref[...] = coef_re * b_re - coef_im * b_im
    bbim_ref[...] = coef_re * b_im + coef_im * b_re


def s5_params(lp):
    G, N, P = S5_GROUPS, S5_N, S5_P
    row = lambda a: a.astype(F32).reshape(1, G * N)
    to_pn = lambda b: jnp.transpose(b.astype(F32), (2, 0, 1)).reshape(P, G * N)
    shp = [jax.ShapeDtypeStruct((1, G * N), F32)] * 2 + [jax.ShapeDtypeStruct((P, G * N), F32)] * 2
    lam_re, lam_im, bb_re, bb_im = pl.pallas_call(_s5_prep_kernel, out_shape=shp)(
        row(lp['s5_a_re']), row(lp['s5_a_im']), row(jnp.repeat(lp['s5_log_dt'][:, None], N, axis=1)),
        to_pn(lp['s5_b_re']), to_pn(lp['s5_b_im']))
    eye = jnp.eye(S5_GB, dtype=F32)

    def w_in_blocks(bb):
        b4 = bb.reshape(P, S5_NBLK, S5_GB, N)
        return jnp.einsum('pbgn,gh->bgphn', b4, eye).reshape(S5_NBLK, S5_GB * P, S5_GB * N)

    def w_out_blocks(c):
        c4 = c.astype(F32).reshape(S5_NBLK, S5_GB, P, N)
        return jnp.einsum('bgpn,gh->bgnhp', c4, eye).reshape(S5_NBLK, S5_GB * N, S5_GB * P)

    w_in = jnp.concatenate([w_in_blocks(bb_re), w_in_blocks(bb_im)], axis=-1).astype(BF16)
    return {'lam_re': lam_re, 'lam_im': lam_im, 'w_in': w_in,
            'w_out_re': w_out_blocks(lp['s5_c_re']).astype(BF16),
            'w_out_im': (-w_out_blocks(lp['s5_c_im'])).astype(BF16),
            'd': lp['s5_d'].astype(F32).reshape(1, S5_WIDTH), 'w_glu': lp['s5_w_glu'].astype(BF16)}


def _s5_kernel(u_ref, h0r_ref, h0i_ref, lamr_ref, lami_ref, win_ref, wor_ref, woi_ref, d_ref, wglu_ref,
               ys_ref, h1r_ref, h1i_ref, hr_scr, hi_scr, br_scr, bi_scr, *, nseq, rows, bb, lane_blk):
    tb = pl.program_id(0)
    blk_in, blk_st = S5_GB * S5_P, S5_GB * S5_N

    @pl.when(tb == 0)
    def _():
        hr_scr[...] = h0r_ref[...]
        hi_scr[...] = h0i_ref[...]

    if nseq > 1:
        u = jnp.swapaxes(u_ref[...], 0, 1).reshape(nseq * rows, S5_WIDTH)
    else:
        u = u_ref[0]
    ub = u.astype(BF16)
    lanes = [slice(lb * lane_blk, (lb + 1) * lane_blk) for lb in range(S5_STATE // lane_blk)]
    lam = [(jnp.broadcast_to(lamr_ref[:, ls], (bb, lane_blk)), jnp.broadcast_to(lami_ref[:, ls], (bb, lane_blk)))
           for ls in lanes]
    h = [(hr_scr[:, ls], hi_scr[:, ls]) for ls in lanes]

    slab = min(nseq * rows, S5_SLAB_ROWS)
    for c in range(nseq * rows // slab):
        rs = slice(c * slab, (c + 1) * slab)
        for blk in range(S5_NBLK):
            bu = _dot(ub[rs, blk * blk_in:(blk + 1) * blk_in], win_ref[blk])
            br_scr[rs, blk * blk_st:(blk + 1) * blk_st] = bu[:, 0:blk_st]
            bi_scr[rs, blk * blk_st:(blk + 1) * blk_st] = bu[:, blk_st:2 * blk_st]

        for lb, ls in enumerate(lanes):
            (lr, li), (hr, hi) = lam[lb], h[lb]
            for t in range(slab // bb):
                r = slice(c * slab + t * bb, c * slab + (t + 1) * bb)
                hr, hi = lr * hr - li * hi + br_scr[r, ls], lr * hi + li * hr + bi_scr[r, ls]
                br_scr[r, ls] = hr
                bi_scr[r, ls] = hi
            h[lb] = (hr, hi)

        ys = []
        for blk in range(S5_NBLK):
            st = slice(blk * blk_st, (blk + 1) * blk_st)
            ys.append(_dot(br_scr[rs, st].astype(BF16), wor_ref[blk]) + _dot(bi_scr[rs, st].astype(BF16), woi_ref[blk]))
        gy = jax.nn.gelu(jnp.concatenate(ys, axis=-1) + d_ref[...] * u[rs])
        out = gy * _sigmoid(_dot(gy.astype(BF16), wglu_ref[...]))
        if nseq > 1:
            tpc = slab // nseq
            ys_ref[:, c * tpc:(c + 1) * tpc, :] = jnp.swapaxes(out.reshape(tpc, nseq, S5_WIDTH), 0, 1)
        else:
            ys_ref[0, rs, :] = out

    for ls, (hr, hi) in zip(lanes, h):
        hr_scr[:, ls] = hr
        hi_scr[:, ls] = hi

    @pl.when(tb == pl.num_programs(0) - 1)
    def _():
        h1r_ref[...] = hr_scr[...]
        h1i_ref[...] = hi_scr[...]


def s5_mixer(u, h0_re, h0_im, sp, tt, single_step):
    B = h0_re.shape[0]
    nseq, S = (1, 1) if single_step else (B, u.shape[1])
    rows = B if single_step else tt
    assert S % tt == 0 and B % SUBLANES == 0
    assert u.shape == ((1, B, S5_WIDTH) if single_step else (B, S, S5_WIDTH))
    lane_blk = max(LANES, min(S5_STATE, (SUBLANES * SUBLANES * LANES) // B))
    full = lambda shape: pl.BlockSpec(shape, lambda t: (0,) * len(shape))
    return pl.pallas_call(
        functools.partial(_s5_kernel, nseq=nseq, rows=rows, bb=B, lane_blk=lane_blk),
        grid=(S // tt,),
        in_specs=[pl.BlockSpec((nseq, rows, S5_WIDTH), lambda t: (0, t, 0)),
                  full((B, S5_STATE)), full((B, S5_STATE)), full((1, S5_STATE)), full((1, S5_STATE)),
                  full(sp['w_in'].shape), full(sp['w_out_re'].shape), full(sp['w_out_im'].shape),
                  full((1, S5_WIDTH)), full((S5_WIDTH, S5_WIDTH))],
        out_specs=[pl.BlockSpec((nseq, rows, S5_WIDTH), lambda t: (0, t, 0)),
                   full((B, S5_STATE)), full((B, S5_STATE))],
        out_shape=[jax.ShapeDtypeStruct(u.shape, F32),
                   jax.ShapeDtypeStruct((B, S5_STATE), F32), jax.ShapeDtypeStruct((B, S5_STATE), F32)],
        scratch_shapes=[pltpu.VMEM((B, S5_STATE), F32), pltpu.VMEM((B, S5_STATE), F32),
                        pltpu.VMEM((nseq * rows, S5_STATE), F32), pltpu.VMEM((nseq * rows, S5_STATE), F32)],
        compiler_params=_cparams("arbitrary"),
    )(u, h0_re, h0_im, sp['lam_re'], sp['lam_im'], sp['w_in'], sp['w_out_re'], sp['w_out_im'], sp['d'], sp['w_glu'])


def _ones_bd():
    r = lax.broadcasted_iota(jnp.int32, (HW, HW), 0) // DH
    c = lax.broadcasted_iota(jnp.int32, (HW, HW), 1) // DH
    return (r == c).astype(BF16)


def _mlstm_sample_kernel(za_ref, gt_ref, bias_ref, norm_ref, c_ref, n_ref, m_ref,
                         ha_ref, c_out, n_out, m_out, q_scr, kw_scr, h_scr):
    za = za_ref[...]
    q_scr[...] = (za[:, 0:HW] * (DH ** -0.5)).T
    k_t = za[:, HW:2 * HW].T
    v_t = za[:, 2 * HW:3 * HW].T
    g_t = (gt_ref[...] + bias_ref[...]).T
    m_out[...] = jnp.zeros_like(m_out)
    for h in range(HEADS):
        hs = slice(h * DH, (h + 1) * DH)
        i_h = g_t[h:h + 1, :]
        bm = _log_sigmoid(g_t[HEADS + h:HEADS + h + 1, :]) + m_ref[h:h + 1, :]
        m_t = jnp.maximum(i_h, bm)
        w_in = jnp.exp(i_h - m_t)
        w_st = jnp.exp(bm - m_t)
        q_h, k_h, v_h = q_scr[hs, :], k_t[hs, :], v_t[hs, :]
        s = jnp.sum(q_h * k_h, axis=0, keepdims=True) * w_in
        kw_scr[hs, :] = k_h * w_in

        def body(d, acc):
            r = h * DH + d
            rows = pl.ds(pl.multiple_of(r * DH, DH), DH)
            c_hd = c_ref[rows, :]
            c_out[rows, :] = w_st * c_hd + kw_scr[pl.ds(r, 1), :] * v_h
            return acc + q_scr[pl.ds(r, 1), :] * c_hd

        qc = lax.fori_loop(0, DH, body, jnp.zeros((DH, za.shape[0]), F32), unroll=4)
        n_h = n_ref[hs, :]
        num = w_st * qc + s * v_h
        den = w_st * jnp.sum(q_h * n_h, axis=0, keepdims=True) + s
        h_scr[hs, :] = num / jnp.maximum(jnp.abs(den), jnp.exp(-m_t))
        n_out[hs, :] = w_st * n_h + kw_scr[hs, :]
        m_out[h:h + 1, :] = m_t
    ha_ref[...] = _head_rms(h_scr[...].T, _ones_bd(), norm_ref[...]) * _sigmoid(za[:, 3 * HW:4 * HW])


def mlstm_sample(za, gates, b_i, b_f, norm, c_t, n_t, m_t):
    B = za.shape[0]
    bias = jnp.zeros((1, GATE_PAD), F32).at[0, 0:HEADS].set(b_i).at[0, HEADS:2 * HEADS].set(b_f)
    shp = lambda *s: jax.ShapeDtypeStruct(s, F32)
    return pl.pallas_call(
        _mlstm_sample_kernel,
        out_shape=[shp(B, HW), shp(HW * DH, B), shp(HW, B), shp(SUBLANES, B)],
        scratch_shapes=[pltpu.VMEM((HW, B), F32), pltpu.VMEM((HW, B), F32), pltpu.VMEM((HW, B), F32)],
        compiler_params=pltpu.CompilerParams(vmem_limit_bytes=VMEM_LIMIT),
    )(za, gates, bias, norm.reshape(1, HW), c_t, n_t, m_t)


def _gdn_sample_kernel(zc_ref, zg_ref, gt_ref, buf_ref, cw_ref, alog_ref, dtb_ref, norm_ref, s_ref,
                       hc_ref, s_out, buf_out, q_scr, k_scr, o_scr):
    W3 = 3 * HW
    x = zc_ref[...]
    y = cw_ref[CONV_K - 1:CONV_K, :] * x
    for j in range(CONV_K - 1):
        y = y + cw_ref[j:j + 1, :] * buf_ref[:, j * W3:(j + 1) * W3]
    buf_out[:, 0:(CONV_K - 2) * W3] = buf_ref[:, W3:(CONV_K - 1) * W3]
    buf_out[:, (CONV_K - 2) * W3:(CONV_K - 1) * W3] = x
    y = _silu(y)
    ones_bd = _ones_bd()
    q_raw, k_raw = y[:, 0:HW], y[:, HW:2 * HW]
    q_scr[...] = (q_raw * lax.rsqrt(_dot_sel_r(q_raw * q_raw, ones_bd) + EPS) * (DH ** -0.5)).T
    k_scr[...] = (k_raw * lax.rsqrt(_dot_sel_r(k_raw * k_raw, ones_bd) + EPS)).T
    v_t = y[:, 2 * HW:3 * HW].T
    gt = gt_ref[...]
    beta_t = _sigmoid(gt).T
    la_t = (-jnp.exp(alog_ref[...]) * _softplus(gt + dtb_ref[...])).T
    nb = x.shape[0]
    for h in range(HEADS):
        hs = slice(h * DH, (h + 1) * DH)
        beta = beta_t[2 * HEADS + h:2 * HEADS + h + 1, :]
        eg = jnp.exp(la_t[3 * HEADS + h:3 * HEADS + h + 1, :])
        q_h, k_h, v_h = q_scr[hs, :], k_scr[hs, :], v_t[hs, :]

        def read(d, acc):
            ks, qs = acc
            r = h * DH + d
            s_hd = s_ref[pl.ds(pl.multiple_of(r * DH, DH), DH), :]
            return ks + k_scr[pl.ds(r, 1), :] * s_hd, qs + q_scr[pl.ds(r, 1), :] * s_hd

        zero = jnp.zeros((DH, nb), F32)
        ks, qs = lax.fori_loop(0, DH, read, (zero, zero), unroll=4)
        u = v_h - eg * ks
        o_scr[hs, :] = eg * qs + (jnp.sum(q_h * k_h, axis=0, keepdims=True) * beta) * u

        def write(d, carry):
            r = h * DH + d
            rows = pl.ds(pl.multiple_of(r * DH, DH), DH)
            s_out[rows, :] = eg * s_ref[rows, :] + (beta * k_scr[pl.ds(r, 1), :]) * u
            return carry

        lax.fori_loop(0, DH, write, 0, unroll=4)
    hc_ref[...] = _head_rms(o_scr[...].T, ones_bd, norm_ref[...]) * _silu(zg_ref[...])


def gdn_sample(zc, zg, gates, buf, conv_w, a_log, dt_bias, norm, s_t):
    B = zc.shape[0]
    alog, dtb = _gdn_gate_rows(a_log, dt_bias)
    shp = lambda *s: jax.ShapeDtypeStruct(s, F32)
    return pl.pallas_call(
        _gdn_sample_kernel,
        out_shape=[shp(B, HW), shp(HW * DH, B), shp(B, (CONV_K - 1) * 3 * HW)],
        scratch_shapes=[pltpu.VMEM((HW, B), F32), pltpu.VMEM((HW, B), F32), pltpu.VMEM((HW, B), F32)],
        compiler_params=pltpu.CompilerParams(vmem_limit_bytes=VMEM_LIMIT),
    )(zc, zg, gates, buf, conv_w, alog, dtb, jnp.tile(norm, HEADS).reshape(1, HW), s_t)


FFN_CHUNK = 1408
IN_SEGMENTS = (4 * HW, S5_WIDTH, 3 * HW, HW, GATE_PAD)


def _tile(n, pref):
    return pref if n % pref == 0 else n


def _regroup_w_in_kernel(w_ref, o_ref):
    a, g2 = 4 * HW, 2 * HEADS
    mid = sum(IN_SEGMENTS[1:4])
    w = w_ref[...]
    o_ref[:, 0:a] = w[:, 0:a].astype(BF16)
    o_ref[:, a:a + mid] = w[:, a + g2:a + g2 + mid].astype(BF16)
    gates = jnp.concatenate([w[:, a:a + g2], w[:, a + g2 + mid:a + 2 * g2 + mid],
                             jnp.zeros((w.shape[0], GATE_PAD - 2 * g2), F32)], axis=1)
    o_ref[:, a + mid:a + mid + GATE_PAD] = gates.astype(BF16)


def regroup_w_in(w_in, tk):
    L, D, n_in = w_in.shape
    n_out = sum(IN_SEGMENTS)
    return pl.pallas_call(
        _regroup_w_in_kernel,
        grid=(L, D // tk),
        in_specs=[pl.BlockSpec((None, tk, n_in), lambda l, i: (l, i, 0))],
        out_specs=pl.BlockSpec((None, tk, n_out), lambda l, i: (l, i, 0)),
        out_shape=jax.ShapeDtypeStruct((L, D, n_out), BF16),
        compiler_params=_cparams("parallel", "parallel"),
    )(w_in)


def prep_weights(p):
    w_in = regroup_w_in(p['w_in'], 256)
    bf = lambda n: p[n].astype(BF16)
    return {'w_in': w_in, 'w_out': bf('w_out'), 'w_mq': bf('w_mq'), 'w_mo': bf('w_mo'), 'w_mk': bf('w_mk'),
            'w_mv': bf('w_mv'), 'w_gate': bf('w_gate'), 'w_up': bf('w_up'), 'w_down': bf('w_down')}


def mixer_prompt(x, lp, W, layer, B, S):
    T = B * S
    za, zu, zc, zg, gates, tail = in_proj_prompt(x, lp['norm_mix'], W['w_in'], layer, lp['gdn_conv_w'], S,
                                                 _tile(S, 1024))
    buf1 = tail[:, SUBLANES - (CONV_K - 1):, :]
    tt = _tile(S, 512)
    ha, c1, n1, m1 = mlstm_prompt(za, gates, lp['mlstm_b_i'], lp['mlstm_b_f'], lp['mlstm_norm'], B, S, tt)
    h0 = jnp.zeros((B, S5_STATE), F32)
    ys3, r1, i1 = s5_mixer(zu.reshape(B, S, S5_WIDTH), h0, h0, lp['s5'], _tile(S, 128), False)
    ys = ys3.reshape(T, S5_WIDTH)
    hc, s1 = gdn_prompt(zc, zg, gates, lp['gdn_a_log'], lp['gdn_dt_bias'], lp['gdn_norm'], B, S, _tile(S, 256))
    return [ha, ys, hc], (c1, n1, m1, r1.reshape(B, S5_GROUPS, S5_N), i1.reshape(B, S5_GROUPS, S5_N), s1, buf1)


def mixer_sample(x, st, lp, W, layer):
    B = x.shape[0]
    c0, n0, m0, r0, i0, s0, buf0 = st
    za, zu, zc, zg, gates = norm_matmul(x, lp['norm_mix'], W['w_in'], layer, IN_SEGMENTS, B)
    m_t = jnp.zeros((SUBLANES, B), F32).at[0:HEADS, :].set(m0.T)
    ha, c1t, n1t, m1t = mlstm_sample(za, gates, lp['mlstm_b_i'], lp['mlstm_b_f'], lp['mlstm_norm'],
                                     c0.reshape(B, HW * DH).T, n0.reshape(B, HW).T, m_t)
    ys3, r1, i1 = s5_mixer(zu.reshape(1, B, S5_WIDTH), r0.reshape(B, S5_STATE), i0.reshape(B, S5_STATE), lp['s5'], 1, True)
    hc, s1t, buf1 = gdn_sample(zc, zg, gates, buf0.reshape(B, (CONV_K - 1) * 3 * HW), lp['gdn_conv_w'],
                               lp['gdn_a_log'], lp['gdn_dt_bias'], lp['gdn_norm'], s0.reshape(B, HW * DH).T)
    x1 = matmul_residual(x, [ha, ys3.reshape(B, S5_WIDTH), hc], W['w_out'], layer, B)
    return x1, (c1t.T.reshape(B, HEADS, DH, DH), n1t.T.reshape(B, HEADS, DH), m1t[0:HEADS, :].T,
                r1.reshape(B, S5_GROUPS, S5_N), i1.reshape(B, S5_GROUPS, S5_N),
                s1t.T.reshape(B, HEADS, DH, DH), buf1.reshape(B, CONV_K - 1, 3 * HW))


def _mem_kv_kernel(x_ref, g_ref, wk_ref, wv_ref, k2_ref, v2_ref, k5_ref, v5_ref):
    xn = _rms(x_ref[...], g_ref[...]).astype(BF16)
    for w_ref, o2_ref, o5_ref in ((wk_ref, k2_ref, k5_ref), (wv_ref, v2_ref, v5_ref)):
        y = _dot(xn, w_ref[...])
        o2_ref[...] = y
        o5_ref[...] = y.reshape(o5_ref.shape)


def mem_kv(mem, norm_mem, wk, wv):
    B, M, D = mem.shape
    L = wk.shape[0]
    dh = D // X_HEADS
    w_spec = pl.BlockSpec((None, D, D), lambda l, b: (l, 0, 0))
    o2_spec = pl.BlockSpec((None, None, M, D), lambda l, b: (l, b, 0, 0))
    o5_spec = pl.BlockSpec((None, None, M, X_HEADS, dh), lambda l, b: (l, b, 0, 0, 0))
    return pl.pallas_call(
        _mem_kv_kernel,
        grid=(L, B),
        in_specs=[pl.BlockSpec((None, M, D), lambda l, b: (b, 0, 0)),
                  pl.BlockSpec((None, 1, D), lambda l, b: (l, 0, 0)), w_spec, w_spec],
        out_specs=[o2_spec, o2_spec, o5_spec, o5_spec],
        out_shape=[jax.ShapeDtypeStruct((L, B, M, D), F32)] * 2
                  + [jax.ShapeDtypeStruct((L, B, M, X_HEADS, dh), F32)] * 2,
        compiler_params=_cparams("parallel", "parallel"),
    )(mem, norm_mem.reshape(L, 1, D), wk, wv)


def xattn_ffn_prompt(x, acts, mk, mv, lp, W, layer, S, norm_final, final):
    T, D = x.shape
    x2 = xattn_prompt(x, acts, W['w_out'], lp['norm_xattn'], W['w_mq'], mk, mv, W['w_mo'], layer, S, _tile(S, 1024))
    return ffn(x2, lp['norm_ffn'], W['w_gate'], W['w_up'], W['w_down'], layer, norm_final, final, _tile(T, 512),
               FFN_CHUNK)


def xattn_ffn_sample(x, ck, cv, lp, W, layer, norm_final, final):
    B, D = x.shape
    (q,) = norm_matmul(x, lp['norm_xattn'], W['w_mq'], layer, (D,), B)
    o = xattn_sample(q, ck, cv, layer, 4)
    x2 = matmul_residual(x, [o], W['w_mo'], layer, B)
    return ffn(x2, lp['norm_ffn'], W['w_gate'], W['w_up'], W['w_down'], layer, norm_final, final, B, FFN_CHUNK)


LAYER_PARAMS = ('norm_mix', 'w_in', 'w_out', 'mlstm_b_i', 'mlstm_b_f', 'mlstm_norm', 's5_a_re', 's5_a_im', 's5_log_dt',
                's5_b_re', 's5_b_im', 's5_c_re', 's5_c_im', 's5_d', 's5_w_glu', 'gdn_conv_w', 'gdn_a_log',
                'gdn_dt_bias', 'gdn_norm', 'norm_xattn', 'norm_mem', 'w_mq', 'w_mk', 'w_mv', 'w_mo', 'norm_ffn',
                'w_gate', 'w_up', 'w_down')


def kernel(x_prompt, x_sample, mem_prompt, cache_mem_k, cache_mem_v, state_mlstm_c, state_mlstm_n, state_mlstm_m, state_s5_re, state_s5_im, state_gdn, state_gdn_conv, norm_mix, w_in, w_out, mlstm_b_i, mlstm_b_f, mlstm_norm, s5_a_re, s5_a_im, s5_log_dt, s5_b_re, s5_b_im, s5_c_re, s5_c_im, s5_d, s5_w_glu, gdn_conv_w, gdn_a_log, gdn_dt_bias, gdn_norm, norm_xattn, norm_mem, w_mq, w_mk, w_mv, w_mo, norm_ffn, w_gate, w_up, w_down, norm_final):
    stacked = dict(norm_mix=norm_mix, w_in=w_in, w_out=w_out, mlstm_b_i=mlstm_b_i, mlstm_b_f=mlstm_b_f,
                   mlstm_norm=mlstm_norm, s5_a_re=s5_a_re, s5_a_im=s5_a_im, s5_log_dt=s5_log_dt, s5_b_re=s5_b_re,
                   s5_b_im=s5_b_im, s5_c_re=s5_c_re, s5_c_im=s5_c_im, s5_d=s5_d, s5_w_glu=s5_w_glu,
                   gdn_conv_w=gdn_conv_w, gdn_a_log=gdn_a_log, gdn_dt_bias=gdn_dt_bias, gdn_norm=gdn_norm,
                   norm_xattn=norm_xattn, norm_mem=norm_mem, w_mq=w_mq, w_mk=w_mk, w_mv=w_mv, w_mo=w_mo,
                   norm_ffn=norm_ffn, w_gate=w_gate, w_up=w_up, w_down=w_down)
    B, S, D = x_prompt.shape
    Bs = x_sample.shape[0]
    M = mem_prompt.shape[1]
    depth = w_in.shape[0]
    xp = x_prompt.reshape(B * S, D)
    xs = x_sample.reshape(Bs, D)
    cache_k, cache_v = cache_mem_k, cache_mem_v
    st_p, st_s = [], []
    W = prep_weights(stacked)
    mk, mv, mem_k, mem_v = mem_kv(mem_prompt, norm_mem, W['w_mk'], W['w_mv'])
    for l in range(depth):
        lp = {n: stacked[n][l] for n in LAYER_PARAMS if n not in W}
        lp['s5'] = s5_params(lp)
        last = l == depth - 1
        acts, sp = mixer_prompt(xp, lp, W, l, B, S)
        xp = xattn_ffn_prompt(xp, acts, mk, mv, lp, W, l, S, norm_final, last)
        st_in = (state_mlstm_c[l], state_mlstm_n[l], state_mlstm_m[l], state_s5_re[l], state_s5_im[l],
                 state_gdn[l], state_gdn_conv[l])
        xs, ss = mixer_sample(xs, st_in, lp, W, l)
        xs = xattn_ffn_sample(xs, cache_k, cache_v, lp, W, l, norm_final, last)
        st_p.append(sp)
        st_s.append(ss)
    stack = lambda lst: [jnp.stack([st[i] for st in lst]) for i in range(7)]
    return (xp.reshape(B, S, D), xs.reshape(Bs, 1, D), mem_k, mem_v,
            *stack(st_p), *stack(st_s))
```

```python
import functools
import math

import jax
import jax.numpy as jnp
from jax import lax
from jax.experimental import pallas as pl
from jax.experimental.pallas import tpu as pltpu

F32 = jnp.float32
BF16 = jnp.bfloat16
EPS = 1e-6

HEADS = 4
DH = 64
HW = HEADS * DH
CHUNK = 64
S5_P = 16
S5_N = 64
S5_GROUPS = 32
S5_WIDTH = S5_GROUPS * S5_P
S5_STATE = S5_GROUPS * S5_N
S5_GB = 8
S5_NBLK = S5_GROUPS // S5_GB
S5_SLAB_ROWS = 256
CONV_K = 4
GDN_SEQS = 4
X_HEADS = 4
GATE_PAD = 128
LANES = 128
SUBLANES = 8
VMEM_LIMIT = 48 * 1024 * 1024


def _cparams(*sem):
    return pltpu.CompilerParams(dimension_semantics=sem, vmem_limit_bytes=VMEM_LIMIT)


def _rms(x, g_row):
    return x * lax.rsqrt(jnp.mean(x * x, axis=-1, keepdims=True) + EPS) * g_row


def _dot(a, b):
    return jnp.dot(a, b, preferred_element_type=F32)


def _dot_nt(a, b):
    return lax.dot_general(a, b, (((1,), (1,)), ((), ())), preferred_element_type=F32)


def _sigmoid(x):
    return 1.0 / (1.0 + jnp.exp(-x))


def _silu(x):
    return x * _sigmoid(x)


def _softplus(x):
    return jnp.maximum(x, 0.0) + jnp.log1p(jnp.exp(-jnp.abs(x)))


def _log_sigmoid(x):
    return jnp.minimum(x, 0.0) - jnp.log1p(jnp.exp(-jnp.abs(x)))


def _norm_matmul_kernel(x_ref, g_ref, w_ref, *out_refs, splits):
    xn = _rms(x_ref[...], g_ref[...]).astype(BF16)
    off = 0
    for o_ref, n in zip(out_refs, splits):
        o_ref[...] = _dot(xn, w_ref[:, off:off + n])
        off += n


def _layer_weight(w_all, layer):
    return pl.BlockSpec((None,) + w_all.shape[1:], lambda *_: (layer, 0, 0), pipeline_mode=pl.Buffered(1))


def norm_matmul(x, g, w_all, layer, splits, tm):
    T, D = x.shape
    N = w_all.shape[2]
    assert sum(splits) == N and T % tm == 0
    return pl.pallas_call(
        functools.partial(_norm_matmul_kernel, splits=tuple(splits)),
        grid=(T // tm,),
        in_specs=[pl.BlockSpec((tm, D), lambda i: (i, 0)),
                  pl.BlockSpec((1, D), lambda i: (0, 0)),
                  _layer_weight(w_all, layer)],
        out_specs=[pl.BlockSpec((tm, n), lambda i: (i, 0)) for n in splits],
        out_shape=[jax.ShapeDtypeStruct((T, n), F32) for n in splits],
        compiler_params=_cparams("parallel"),
    )(x, g.reshape(1, D), w_all)


def _in_proj_prompt_kernel(x_ref, g_ref, w_ref, cw_ref, za_ref, zu_ref, zc_ref, zg_ref, gt_ref, tail_ref, xp_scr,
                           *, tiles_per_seq):
    tm = x_ref.shape[0]
    pad = SUBLANES
    xn = _rms(x_ref[...], g_ref[...]).astype(BF16)
    offs = [sum(IN_SEGMENTS[:s]) for s in range(len(IN_SEGMENTS))]

    @pl.when(pl.program_id(0) % tiles_per_seq == 0)
    def _():
        xp_scr[0:pad, :] = jnp.zeros((pad, xp_scr.shape[1]), F32)

    xp_scr[pad:pad + tm, :] = _dot(xn, w_ref[:, offs[2]:offs[2] + IN_SEGMENTS[2]])
    others = ((0, za_ref), (1, zu_ref), (3, zg_ref), (4, gt_ref))
    rows = tm // len(others)
    for r, (s, o_ref) in enumerate(others):
        y = cw_ref[CONV_K - 1:CONV_K, :] * xp_scr[pl.ds(pad + r * rows, rows), :]
        for j in range(CONV_K - 1):
            y = y + cw_ref[j:j + 1, :] * xp_scr[pl.ds(pad + r * rows - (CONV_K - 1) + j, rows), :]
        zc_ref[pl.ds(r * rows, rows), :] = _silu(y)
        o_ref[...] = _dot(xn, w_ref[:, offs[s]:offs[s] + IN_SEGMENTS[s]])
    tail = xp_scr[tm:tm + pad, :]
    xp_scr[0:pad, :] = tail
    tail_ref[0] = tail


def in_proj_prompt(x, g, w_all, layer, conv_w, seq, tm):
    T, D = x.shape
    assert seq % tm == 0 and T % seq == 0
    wc = IN_SEGMENTS[2]
    outs = pl.pallas_call(
        functools.partial(_in_proj_prompt_kernel, tiles_per_seq=seq // tm),
        grid=(T // tm,),
        in_specs=[pl.BlockSpec((tm, D), lambda i: (i, 0)),
                  pl.BlockSpec((1, D), lambda i: (0, 0)),
                  _layer_weight(w_all, layer),
                  pl.BlockSpec((CONV_K, wc), lambda i: (0, 0))],
        out_specs=[pl.BlockSpec((tm, n), lambda i: (i, 0)) for n in IN_SEGMENTS]
                  + [pl.BlockSpec((1, SUBLANES, wc), lambda i: (i // (seq // tm), 0, 0))],
        out_shape=[jax.ShapeDtypeStruct((T, n), F32) for n in IN_SEGMENTS]
                  + [jax.ShapeDtypeStruct((T // seq, SUBLANES, wc), F32)],
        scratch_shapes=[pltpu.VMEM((tm + SUBLANES, wc), F32)],
        compiler_params=_cparams("arbitrary"),
    )(x, g.reshape(1, D), w_all, conv_w)
    return outs


def _matmul_residual_kernel(x_ref, *refs, ksplits):
    a_refs, w_ref, o_ref = refs[:-2], refs[-2], refs[-1]
    acc = x_ref[...]
    off = 0
    for a_ref, k in zip(a_refs, ksplits):
        acc = acc + _dot(a_ref[...].astype(BF16), w_ref[off:off + k, :])
        off += k
    o_ref[...] = acc


def matmul_residual(x, acts, w_all, layer, tm):
    T, D = x.shape
    ks = tuple(a.shape[1] for a in acts)
    K = w_all.shape[1]
    assert sum(ks) == K and T % tm == 0
    return pl.pallas_call(
        functools.partial(_matmul_residual_kernel, ksplits=ks),
        grid=(T // tm,),
        in_specs=[pl.BlockSpec((tm, D), lambda i: (i, 0))]
                 + [pl.BlockSpec((tm, k), lambda i: (i, 0)) for k in ks]
                 + [_layer_weight(w_all, layer)],
        out_specs=pl.BlockSpec((tm, D), lambda i: (i, 0)),
        out_shape=jax.ShapeDtypeStruct((T, D), F32),
        compiler_params=_cparams("parallel"),
    )(x, *acts, w_all)


def _ffn_kernel(x_ref, g_ref, wg_ref, wu_ref, wd_ref, gf_ref, o_ref, *, final_norm, tf):
    x = x_ref[...]
    h = _rms(x, g_ref[...]).astype(BF16)
    y = x
    for j in range(wg_ref.shape[1] // tf):
        cols = slice(j * tf, (j + 1) * tf)
        a = _dot(h, wg_ref[:, cols])
        b = _dot(h, wu_ref[:, cols])
        y = y + _dot((_silu(a) * b).astype(BF16), wd_ref[cols, :])
    if final_norm:
        y = _rms(y, gf_ref[...])
    o_ref[...] = y


def _resident(shape):
    return pl.BlockSpec(shape, lambda *_: (0,) * len(shape), pipeline_mode=pl.Buffered(1))


def ffn(x, g, wg, wu, wd, layer, g_final, final_norm, tm, tf):
    T, D = x.shape
    F = wg.shape[2]
    assert T % tm == 0 and F % tf == 0
    return pl.pallas_call(
        functools.partial(_ffn_kernel, final_norm=final_norm, tf=tf),
        grid=(T // tm,),
        in_specs=[pl.BlockSpec((tm, D), lambda i: (i, 0)),
                  _resident((1, D)), _layer_weight(wg, layer), _layer_weight(wu, layer), _layer_weight(wd, layer),
                  _resident((1, D))],
        out_specs=pl.BlockSpec((tm, D), lambda i: (i, 0)),
        out_shape=jax.ShapeDtypeStruct((T, D), F32),
        compiler_params=_cparams("parallel"),
    )(x, g.reshape(1, D), wg, wu, wd, g_final.reshape(1, D))


def _softmax_rows(s):
    e = jnp.exp(s - jnp.max(s, axis=-1, keepdims=True))
    return e / jnp.sum(e, axis=-1, keepdims=True)


def _xattn_prompt_kernel(x_ref, *refs, dh, n_acts):
    a_refs = refs[:n_acts]
    wmix_ref, g_ref, wq_ref, k_ref, v_ref, wo_ref, o_ref = refs[n_acts:]
    scale = dh ** -0.5
    x = x_ref[...]
    off = 0
    for a_ref in a_refs:
        x = x + _dot(a_ref[...].astype(BF16), wmix_ref[off:off + a_ref.shape[1], :])
        off += a_ref.shape[1]
    q = _dot(_rms(x, g_ref[...]).astype(BF16), wq_ref[...]).astype(BF16)
    sl = [slice(h * dh, (h + 1) * dh) for h in range(X_HEADS)]
    s = [_dot_nt(q[:, c], k_ref[0, :, c].astype(BF16)) * scale for c in sl]
    p = [_softmax_rows(sh).astype(BF16) for sh in s]
    heads = [_dot(ph, v_ref[0, :, c].astype(BF16)).astype(BF16) for ph, c in zip(p, sl)]
    o_ref[...] = x + _dot(jnp.concatenate(heads, axis=-1), wo_ref[...])


def xattn_prompt(x, acts, wmix, g, wq, mk, mv, wo, layer, seq, tq):
    T, D = x.shape
    _, B, M, _ = mk.shape
    nt = seq // tq
    rows = lambda w: pl.BlockSpec((tq, w), lambda b, t: (b * nt + t, 0))
    return pl.pallas_call(
        functools.partial(_xattn_prompt_kernel, dh=D // X_HEADS, n_acts=len(acts)),
        grid=(B, nt),
        in_specs=[rows(D)] + [rows(a.shape[1]) for a in acts]
                 + [_layer_weight(wmix, layer),
                    pl.BlockSpec((1, D), lambda b, t: (0, 0)),
                    _layer_weight(wq, layer),
                    pl.BlockSpec((None, 1, M, D), lambda b, t: (layer, b, 0, 0)),
                    pl.BlockSpec((None, 1, M, D), lambda b, t: (layer, b, 0, 0)),
                    _layer_weight(wo, layer)],
        out_specs=rows(D),
        out_shape=jax.ShapeDtypeStruct((T, D), F32),
        compiler_params=_cparams("parallel", "parallel"),
    )(x, *acts, wmix, g.reshape(1, D), wq, mk, mv, wo)


def _xattn_sample_kernel(q_ref, k_ref, v_ref, o_ref, *, sb):
    M, H, dh = k_ref.shape[1:]
    scale = dh ** -0.5
    row = lax.broadcasted_iota(jnp.int32, (SUBLANES, M * H), 0)
    col_head = lax.broadcasted_iota(jnp.int32, (SUBLANES, M * H), 1) % H
    own = (row % H) == col_head
    pad = jnp.zeros((SUBLANES - H, dh), F32)
    q8 = [jnp.concatenate([q_ref[i], pad], axis=0).astype(BF16) for i in range(sb)]
    s = [_dot_nt(q8[i], k_ref[i].reshape(M * H, dh).astype(BF16)) * scale for i in range(sb)]
    p = [_softmax_rows(jnp.where(own, si, -jnp.inf)).astype(BF16) for si in s]
    for i in range(sb):
        o_ref[i] = _dot(p[i], v_ref[i].reshape(M * H, dh).astype(BF16))[0:H]


def xattn_sample(q, ck, cv, layer, sb):
    B, D = q.shape
    _, _, M, H, dh = ck.shape
    out = pl.pallas_call(
        functools.partial(_xattn_sample_kernel, sb=sb),
        grid=(B // sb,),
        in_specs=[pl.BlockSpec((sb, H, dh), lambda i: (i, 0, 0)),
                  pl.BlockSpec((None, sb, M, H, dh), lambda i: (layer, i, 0, 0, 0)),
                  pl.BlockSpec((None, sb, M, H, dh), lambda i: (layer, i, 0, 0, 0))],
        out_specs=pl.BlockSpec((sb, H, dh), lambda i: (i, 0, 0)),
        out_shape=jax.ShapeDtypeStruct((B, H, dh), F32),
        compiler_params=_cparams("parallel"),
    )(q.reshape(B, H, dh), ck, cv)
    return out.reshape(B, D)


def _unrolled(n, body, carry):
    for i in range(n):
        carry = body(i, carry)
    return carry


def _lane_cat_masks(L):
    row = lax.broadcasted_iota(jnp.int32, (L, HW), 0)
    j = lax.broadcasted_iota(jnp.int32, (L, HW), 1) % DH
    r2 = lax.broadcasted_iota(jnp.int32, (HW, HW), 0) // DH
    c2 = lax.broadcasted_iota(jnp.int32, (HW, HW), 1) // DH
    return row >= j, row > j, row == j, r2 == c2


def _expand_bd(x, bd):
    return jnp.where(bd, jnp.concatenate([x] * HEADS, axis=0), jnp.zeros((), x.dtype))


def _seg_reduce(x, op, fill):
    lo = lax.broadcasted_iota(jnp.int32, (x.shape[0], LANES), 1) < DH
    parts = []
    for c in range(HW // LANES):
        xh = x[:, c * LANES:(c + 1) * LANES]
        a = op(jnp.where(lo, xh, fill), axis=-1, keepdims=True)
        b = op(jnp.where(lo, fill, xh), axis=-1, keepdims=True)
        parts.append(jnp.where(lo, a, b))
    return jnp.concatenate(parts, axis=-1)


def _head_expander(first_lane):
    r = lax.broadcasted_iota(jnp.int32, (GATE_PAD, HW), 0)
    c = lax.broadcasted_iota(jnp.int32, (GATE_PAD, HW), 1) // DH
    return (r == c + first_lane).astype(BF16)


def _chunk_cumsum(x):
    tt, w = x.shape
    g = SUBLANES
    x3 = x.reshape(tt // g, g, w)
    sub = lax.broadcasted_iota(jnp.int32, x3.shape, 1)
    s = 1
    while s < g:
        x3 = x3 + jnp.where(sub >= s, pltpu.roll(x3, s, 1), 0.0)
        s *= 2
    per = CHUNK // g
    x4 = x3.reshape(tt // CHUNK, per, g, w)
    acc = jnp.zeros((tt // CHUNK, 1, 1, w), F32)
    parts = []
    for i in range(per):
        parts.append(x4[:, i:i + 1] + acc)
        acc = acc + x4[:, i:i + 1, g - 1:g, :]
    return jnp.concatenate(parts, axis=1).reshape(tt, w)


def _split3(x):
    hi = x.astype(BF16)
    r = x - hi.astype(F32)
    mid = r.astype(BF16)
    return hi, mid, (r - mid.astype(F32)).astype(BF16)


def _dot_sel_r(x, sel):
    hi, mid, lo = _split3(x)
    return (_dot(lo, sel) + _dot(mid, sel)) + _dot(hi, sel)


def _col_to_row(x, eye):
    return jnp.sum(jnp.where(eye, x, 0.0), axis=0, keepdims=True)


def _head_rms(x, ones_bd, g_row):
    ms = _dot_sel_r(x * x, ones_bd) * (1.0 / DH)
    return x * lax.rsqrt(ms + EPS) * g_row


def _mlstm_prompt_kernel(za_ref, gt_ref, bias_ref, norm_ref, ha_ref, c_ref, n_ref, m_ref,
                         c_scr, n_scr, m_scr, ix_scr, bx_scr, *, nchunks, group):
    tb = pl.program_id(1)
    L = CHUNK
    tril, _, eye, bd = _lane_cat_masks(L)
    ones_bd = bd.astype(BF16)

    @pl.when(tb == 0)
    def _():
        c_scr[...] = jnp.zeros_like(c_scr)
        n_scr[...] = jnp.zeros_like(n_scr)
        m_scr[...] = jnp.zeros_like(m_scr)

    gt = gt_ref[...] + bias_ref[...]
    b_cols = _chunk_cumsum(_log_sigmoid(gt))
    ix_scr[...] = _dot_sel_r(gt, _head_expander(0))
    bx_scr[...] = _dot_sel_r(b_cols, _head_expander(HEADS))

    def chunks(gi, carry):
        c_bd, n_row, m_x = carry
        rows = [pl.ds(pl.multiple_of((gi * group + j) * L, L), L) for j in range(group)]
        q = [(za_ref[r, 0:HW] * (DH ** -0.5)).astype(BF16) for r in rows]
        k = [za_ref[r, HW:2 * HW] for r in rows]
        v = [za_ref[r, 2 * HW:3 * HW].astype(BF16) for r in rows]
        i_x = [ix_scr[r, :] for r in rows]
        b_x = [bx_scr[r, :] for r in rows]

        m_in, n_in, kw, decay = [], [], [], []
        for kk, ii, bb in zip(k, i_x, b_x):
            b_last = bb[L - 1:L, :]
            g_x = b_last - bb + ii
            m_new = jnp.maximum(b_last + m_x, jnp.max(g_x, axis=0, keepdims=True))
            kw.append(kk * jnp.exp(g_x - m_new))
            decay.append(jnp.exp(b_last + m_x - m_new))
            m_in.append(m_x)
            n_in.append(n_row)
            n_row = decay[-1] * n_row + jnp.sum(kw[-1], axis=0, keepdims=True)
            m_x = m_new

        d_intra = [jnp.where(tril, bb - _col_to_row(bb, eye) + _col_to_row(ii, eye), -jnp.inf) for bb, ii in zip(b_x, i_x)]
        d_inter = [bb + mm for bb, mm in zip(b_x, m_in)]
        m_t = [jnp.maximum(_seg_reduce(di, jnp.max, -jnp.inf), de) for di, de in zip(d_intra, d_inter)]
        w_inter = [jnp.exp(de - mt) for de, mt in zip(d_inter, m_t)]
        s = [_dot_nt(qq, _expand_bd(kk.astype(BF16), bd)) * jnp.exp(di - mt) for qq, kk, di, mt in zip(q, k, d_intra, m_t)]
        kv = [jnp.where(bd, _dot(kwj.T.astype(BF16), vv), 0.0) for kwj, vv in zip(kw, v)]
        c_in = []
        for dj, kvj in zip(decay, kv):
            c_in.append(c_bd)
            c_bd = dj * c_bd + kvj
        num = [wi * _dot(qq, cc.astype(BF16)) + _dot(ss.astype(BF16), _expand_bd(vv, bd))
               for wi, qq, cc, ss, vv in zip(w_inter, q, c_in, s, v)]
        den = [wi * _dot((qq.astype(F32) * nn).astype(BF16), ones_bd) + _seg_reduce(ss, jnp.sum, 0.0)
               for wi, qq, nn, ss in zip(w_inter, q, n_in, s)]
        for r, nu, de, mt in zip(rows, num, den, m_t):
            hh = nu / jnp.maximum(jnp.abs(de), jnp.exp(-mt))
            ha_ref[r, :] = _head_rms(hh, ones_bd, norm_ref[...]) * _sigmoid(za_ref[r, 3 * HW:4 * HW])
        return c_bd, n_row, m_x

    c_bd, n_row, m_x = _unrolled(nchunks // group, chunks, (c_scr[...], n_scr[...], m_scr[...]))
    c_scr[...] = c_bd
    n_scr[...] = n_row
    m_scr[...] = m_x

    @pl.when(tb == pl.num_programs(1) - 1)
    def _():
        for h in range(HEADS):
            c_ref[0, h] = c_scr[h * DH:(h + 1) * DH, h * DH:(h + 1) * DH]
        n_ref[0] = n_scr[...]
        m_ref[0] = m_scr[...]


def mlstm_prompt(za, gates, b_i, b_f, norm, B, S, tt):
    assert S % tt == 0 and tt % CHUNK == 0
    nt = S // tt
    bias = jnp.zeros((1, GATE_PAD), F32).at[0, 0:HEADS].set(b_i).at[0, HEADS:2 * HEADS].set(b_f)
    ha, c, n, m = pl.pallas_call(
        functools.partial(_mlstm_prompt_kernel, nchunks=tt // CHUNK, group=math.gcd(tt // CHUNK, 8)),
        grid=(B, nt),
        in_specs=[pl.BlockSpec((tt, 4 * HW), lambda b, t: (b * nt + t, 0)),
                  pl.BlockSpec((tt, GATE_PAD), lambda b, t: (b * nt + t, 0)),
                  pl.BlockSpec((1, GATE_PAD), lambda b, t: (0, 0)),
                  pl.BlockSpec((1, HW), lambda b, t: (0, 0))],
        out_specs=[pl.BlockSpec((tt, HW), lambda b, t: (b * nt + t, 0)),
                   pl.BlockSpec((1, HEADS, DH, DH), lambda b, t: (b, 0, 0, 0)),
                   pl.BlockSpec((1, 1, HW), lambda b, t: (b, 0, 0)),
                   pl.BlockSpec((1, 1, HW), lambda b, t: (b, 0, 0))],
        out_shape=[jax.ShapeDtypeStruct((B * S, HW), F32),
                   jax.ShapeDtypeStruct((B, HEADS, DH, DH), F32),
                   jax.ShapeDtypeStruct((B, 1, HW), F32),
                   jax.ShapeDtypeStruct((B, 1, HW), F32)],
        scratch_shapes=[pltpu.VMEM((HW, HW), F32), pltpu.VMEM((1, HW), F32), pltpu.VMEM((1, HW), F32),
                        pltpu.VMEM((tt, HW), F32), pltpu.VMEM((tt, HW), F32)],
        compiler_params=_cparams("parallel", "arbitrary"),
    )(za, gates, bias, norm.reshape(1, HW))
    return ha, c, n.reshape(B, HEADS, DH), m[:, 0, ::DH]


def _gdn_prompt_kernel(zc_ref, zg_ref, gt_ref, alog_ref, dtb_ref, norm_ref,
                       hc_ref, s_ref,
                       s_scr, q_scr, k_scr, v_scr, beta_scr, g_scr, uv_scr, wq_scr, qkm_scr, kwt_scr,
                       *, nseq, nchunks, group):
    tb = pl.program_id(1)
    L = CHUNK
    tt = nchunks * L
    tril, strict, eye, bd = _lane_cat_masks(L)
    ones_bd = bd.astype(BF16)

    @pl.when(tb == 0)
    def _():
        s_scr[...] = jnp.zeros_like(s_scr)

    for i in range(nseq):
        sr = slice(i * tt, (i + 1) * tt)
        y = zc_ref[i]
        q_raw, k_raw = y[:, 0:HW], y[:, HW:2 * HW]
        q_scr[sr, :] = (q_raw * lax.rsqrt(_dot_sel_r(q_raw * q_raw, ones_bd) + EPS) * (DH ** -0.5)).astype(BF16)
        k_scr[sr, :] = k_raw * lax.rsqrt(_dot_sel_r(k_raw * k_raw, ones_bd) + EPS)
        v_scr[sr, :] = y[:, 2 * HW:3 * HW]
        gt = gt_ref[i]
        beta_scr[sr, :] = _dot_sel_r(_sigmoid(gt), _head_expander(2 * HEADS))
        la_cols = -jnp.exp(alog_ref[...]) * _softplus(gt + dtb_ref[...])
        g_scr[sr, :] = _dot_sel_r(_chunk_cumsum(la_cols), _head_expander(3 * HEADS))

    def prepare(gi, carry):
        cis = [gi * group + j for j in range(group)]
        rows = [pl.ds(pl.multiple_of(ci * L, L), L) for ci in cis]
        k = [k_scr[r, :] for r in rows]
        g_x = [g_scr[r, :] for r in rows]
        beta_row = [_col_to_row(beta_scr[r, :], eye) for r in rows]
        dec_incl = [jnp.where(tril, jnp.exp(jnp.where(tril, g - _col_to_row(g, eye), 0.0)), 0.0) for g in g_x]
        k_bd = [_expand_bd(kk.astype(BF16), bd) for kk in k]
        n0 = [-(jnp.where(strict, d, 0.0) * _dot_nt(kk.astype(BF16), kbd) * br)
              for d, kk, kbd, br in zip(dec_incl, k, k_bd, beta_row)]
        for r, d, kbd, br in zip(rows, dec_incl, k_bd, beta_row):
            qkm_scr[r, :] = (_dot_nt(q_scr[r, :], kbd) * d * br).astype(BF16)

        p = [_dot(n.astype(BF16), _expand_bd(n.astype(BF16), bd)) for n in n0]
        m = n0
        steps = int(math.log2(L)) - 1
        for i in range(steps):
            p_bd = [_expand_bd(pp.astype(BF16), bd) for pp in p]
            if i < steps - 1:
                pm = [_dot(jnp.concatenate([pp, mm], axis=0).astype(BF16), pbd) for pp, mm, pbd in zip(p, m, p_bd)]
                p_next, mp = [x[0:L] for x in pm], [x[L:2 * L] for x in pm]
            else:
                p_next, mp = None, [_dot(mm.astype(BF16), pbd) for mm, pbd in zip(m, p_bd)]
            m = [mm + pp + x for mm, pp, x in zip(m, p, mp)]
            p = p_next

        for ci, r, kk, g, mm in zip(cis, rows, k, g_x, m):
            v = v_scr[r, :]
            egk = jnp.exp(g) * kk
            rhs_bd = jnp.concatenate([_expand_bd(v.astype(BF16), bd), _expand_bd(egk.astype(BF16), bd)], axis=1)
            mr = _dot(mm.astype(BF16), rhs_bd)
            uv_scr[r, :] = v + mr[:, 0:HW]
            wq_rows = pl.multiple_of(ci * 2 * L, 2 * L)
            wq_scr[pl.ds(wq_rows, L), :] = (egk + mr[:, HW:2 * HW]).astype(BF16)
            wq_scr[pl.ds(wq_rows + L, L), :] = q_scr[r, :]
            kw = kk * (jnp.exp(g[L - 1:L, :] - g) * beta_scr[r, :])
            kwt_scr[pl.ds(pl.multiple_of(ci * HW, HW), HW), :] = kw.T.astype(BF16)
        return carry

    _unrolled(nseq * nchunks // group, prepare, 0)

    def advance(c, states):
        cis = [i * nchunks + c for i in range(nseq)]
        rows = [pl.ds(pl.multiple_of(ci * L, L), L) for ci in cis]
        wqs = [_dot(wq_scr[pl.ds(pl.multiple_of(ci * 2 * L, 2 * L), 2 * L), :], s.astype(BF16))
               for ci, s in zip(cis, states)]
        ub = [(uv_scr[r, :] - x[0:L]).astype(BF16) for r, x in zip(rows, wqs)]
        new = [jnp.exp(g_scr[pl.ds(pl.multiple_of(ci * L + L - SUBLANES, SUBLANES), SUBLANES), :][SUBLANES - 1:, :]) * s
               + jnp.where(bd, _dot(kwt_scr[pl.ds(pl.multiple_of(ci * HW, HW), HW), :], u), 0.0)
               for ci, s, u in zip(cis, states, ub)]
        for i, (r, x, u) in enumerate(zip(rows, wqs, ub)):
            o = jnp.exp(g_scr[r, :]) * x[L:2 * L] + _dot(qkm_scr[r, :], _expand_bd(u, bd))
            hc_ref[i, pl.ds(pl.multiple_of(c * L, L), L), :] = (
                _head_rms(o, ones_bd, norm_ref[...]) * _silu(zg_ref[i, pl.ds(pl.multiple_of(c * L, L), L), :]))
        return tuple(new)

    states = _unrolled(nchunks, advance, tuple(s_scr[i] for i in range(nseq)))
    for i in range(nseq):
        s_scr[i] = states[i]

    @pl.when(tb == pl.num_programs(1) - 1)
    def _():
        for i in range(nseq):
            for h in range(HEADS):
                s_ref[i, h] = s_scr[i, h * DH:(h + 1) * DH, h * DH:(h + 1) * DH]


def _gdn_gate_rows(a_log, dt_bias):
    z = jnp.zeros((1, GATE_PAD), F32)
    return (z.at[0, 3 * HEADS:4 * HEADS].set(a_log), z.at[0, 3 * HEADS:4 * HEADS].set(dt_bias))


def gdn_prompt(zc, zg, gates, a_log, dt_bias, norm, B, S, tt):
    assert S % tt == 0 and tt % CHUNK == 0
    nt = S // tt
    nseq = math.gcd(B, GDN_SEQS)
    rows = nseq * tt
    alog, dtb = _gdn_gate_rows(a_log, dt_bias)
    blk = lambda w: pl.BlockSpec((nseq, tt, w), lambda b, t: (b, t, 0))
    row = lambda w: pl.BlockSpec((1, w), lambda b, t: (0, 0))
    hc, s = pl.pallas_call(
        functools.partial(_gdn_prompt_kernel, nseq=nseq, nchunks=tt // CHUNK, group=math.gcd(rows // CHUNK, 16)),
        grid=(B // nseq, nt),
        in_specs=[blk(3 * HW), blk(HW), blk(GATE_PAD), row(GATE_PAD), row(GATE_PAD), row(HW)],
        out_specs=[blk(HW), pl.BlockSpec((nseq, HEADS, DH, DH), lambda b, t: (b, 0, 0, 0))],
        out_shape=[jax.ShapeDtypeStruct((B, S, HW), F32), jax.ShapeDtypeStruct((B, HEADS, DH, DH), F32)],
        scratch_shapes=[pltpu.VMEM((nseq, HW, HW), F32),
                        pltpu.VMEM((rows, HW), BF16), pltpu.VMEM((rows, HW), F32), pltpu.VMEM((rows, HW), F32),
                        pltpu.VMEM((rows, HW), F32), pltpu.VMEM((rows, HW), F32), pltpu.VMEM((rows, HW), F32),
                        pltpu.VMEM((2 * rows, HW), BF16), pltpu.VMEM((rows, HW), BF16),
                        pltpu.VMEM((rows // CHUNK * HW, CHUNK), BF16)],
        compiler_params=_cparams("parallel", "arbitrary"),
    )(zc.reshape(B, S, 3 * HW), zg.reshape(B, S, HW), gates.reshape(B, S, GATE_PAD), alog, dtb,
      jnp.tile(norm, HEADS).reshape(1, HW))
    return hc.reshape(B * S, HW), s


def _s5_prep_kernel(are_ref, aim_ref, ldt_ref, bre_ref, bim_ref, lre_ref, lim_ref, bbre_ref, bbim_ref):
    a_re, a_im = are_ref[...], aim_ref[...]
    dt = jnp.exp(ldt_ref[...])
    mag = jnp.exp(a_re * dt)
    lam_re, lam_im = mag * jnp.cos(a_im * dt), mag * jnp.sin(a_im * dt)
    lre_ref[...] = lam_re
    lim_ref[...] = lam_im
    nr, ni = lam_re - 1.0, lam_im
    den = a_re * a_re + a_im * a_im
    coef_re = (nr * a_re + ni * a_im) / den
    coef_im = (ni * a_re - nr * a_im) / den
    b_re, b_im = bre_ref[...], bim_ref[...]
    bbre_ref[...] = coef_re * b_re - coef_im * b_im
    bbim_ref[...] = coef_re * b_im + coef_im * b_re


def s5_params(lp):
    G, N, P = S5_GROUPS, S5_N, S5_P
    row = lambda a: a.astype(F32).reshape(1, G * N)
    to_pn = lambda b: jnp.transpose(b.astype(F32), (2, 0, 1)).reshape(P, G * N)
    shp = [jax.ShapeDtypeStruct((1, G * N), F32)] * 2 + [jax.ShapeDtypeStruct((P, G * N), F32)] * 2
    lam_re, lam_im, bb_re, bb_im = pl.pallas_call(_s5_prep_kernel, out_shape=shp)(
        row(lp['s5_a_re']), row(lp['s5_a_im']), row(jnp.repeat(lp['s5_log_dt'][:, None], N, axis=1)),
        to_pn(lp['s5_b_re']), to_pn(lp['s5_b_im']))
    eye = jnp.eye(S5_GB, dtype=F32)

    def w_in_blocks(bb):
        b4 = bb.reshape(P, S5_NBLK, S5_GB, N)
        return jnp.einsum('pbgn,gh->bgphn', b4, eye).reshape(S5_NBLK, S5_GB * P, S5_GB * N)

    def w_out_blocks(c):
        c4 = c.astype(F32).reshape(S5_NBLK, S5_GB, P, N)
        return jnp.einsum('bgpn,gh->bgnhp', c4, eye).reshape(S5_NBLK, S5_GB * N, S5_GB * P)

    w_in = jnp.concatenate([w_in_blocks(bb_re), w_in_blocks(bb_im)], axis=-1).astype(BF16)
    return {'lam_re': lam_re, 'lam_im': lam_im, 'w_in': w_in,
            'w_out_re': w_out_blocks(lp['s5_c_re']).astype(BF16),
            'w_out_im': (-w_out_blocks(lp['s5_c_im'])).astype(BF16),
            'd': lp['s5_d'].astype(F32).reshape(1, S5_WIDTH), 'w_glu': lp['s5_w_glu'].astype(BF16)}


def _s5_kernel(u_ref, h0r_ref, h0i_ref, lamr_ref, lami_ref, win_ref, wor_ref, woi_ref, d_ref, wglu_ref,
               ys_ref, h1r_ref, h1i_ref, hr_scr, hi_scr, br_scr, bi_scr, *, nseq, rows, bb, lane_blk):
    tb = pl.program_id(0)
    blk_in, blk_st = S5_GB * S5_P, S5_GB * S5_N

    @pl.when(tb == 0)
    def _():
        hr_scr[...] = h0r_ref[...]
        hi_scr[...] = h0i_ref[...]

    if nseq > 1:
        u = jnp.swapaxes(u_ref[...], 0, 1).reshape(nseq * rows, S5_WIDTH)
    else:
        u = u_ref[0]
    ub = u.astype(BF16)
    lanes = [slice(lb * lane_blk, (lb + 1) * lane_blk) for lb in range(S5_STATE // lane_blk)]
    lam = [(jnp.broadcast_to(lamr_ref[:, ls], (bb, lane_blk)), jnp.broadcast_to(lami_ref[:, ls], (bb, lane_blk)))
           for ls in lanes]
    h = [(hr_scr[:, ls], hi_scr[:, ls]) for ls in lanes]

    slab = min(nseq * rows, S5_SLAB_ROWS)
    for c in range(nseq * rows // slab):
        rs = slice(c * slab, (c + 1) * slab)
        for blk in range(S5_NBLK):
            bu = _dot(ub[rs, blk * blk_in:(blk + 1) * blk_in], win_ref[blk])
            br_scr[rs, blk * blk_st:(blk + 1) * blk_st] = bu[:, 0:blk_st]
            bi_scr[rs, blk * blk_st:(blk + 1) * blk_st] = bu[:, blk_st:2 * blk_st]

        for lb, ls in enumerate(lanes):
            (lr, li), (hr, hi) = lam[lb], h[lb]
            for t in range(slab // bb):
                r = slice(c * slab + t * bb, c * slab + (t + 1) * bb)
                hr, hi = lr * hr - li * hi + br_scr[r, ls], lr * hi + li * hr + bi_scr[r, ls]
                br_scr[r, ls] = hr
                bi_scr[r, ls] = hi
            h[lb] = (hr, hi)

        ys = []
        for blk in range(S5_NBLK):
            st = slice(blk * blk_st, (blk + 1) * blk_st)
            ys.append(_dot(br_scr[rs, st].astype(BF16), wor_ref[blk]) + _dot(bi_scr[rs, st].astype(BF16), woi_ref[blk]))
        gy = jax.nn.gelu(jnp.concatenate(ys, axis=-1) + d_ref[...] * u[rs])
        out = gy * _sigmoid(_dot(gy.astype(BF16), wglu_ref[...]))
        if nseq > 1:
            tpc = slab // nseq
            ys_ref[:, c * tpc:(c + 1) * tpc, :] = jnp.swapaxes(out.reshape(tpc, nseq, S5_WIDTH), 0, 1)
        else:
            ys_ref[0, rs, :] = out

    for ls, (hr, hi) in zip(lanes, h):
        hr_scr[:, ls] = hr
        hi_scr[:, ls] = hi

    @pl.when(tb == pl.num_programs(0) - 1)
    def _():
        h1r_ref[...] = hr_scr[...]
        h1i_ref[...] = hi_scr[...]


def s5_mixer(u, h0_re, h0_im, sp, tt, single_step):
    B = h0_re.shape[0]
    nseq, S = (1, 1) if single_step else (B, u.shape[1])
    rows = B if single_step else tt
    assert S % tt == 0 and B % SUBLANES == 0
    assert u.shape == ((1, B, S5_WIDTH) if single_step else (B, S, S5_WIDTH))
    lane_blk = max(LANES, min(S5_STATE, (SUBLANES * SUBLANES * LANES) // B))
    full = lambda shape: pl.BlockSpec(shape, lambda t: (0,) * len(shape))
    return pl.pallas_call(
        functools.partial(_s5_kernel, nseq=nseq, rows=rows, bb=B, lane_blk=lane_blk),
        grid=(S // tt,),
        in_specs=[pl.BlockSpec((nseq, rows, S5_WIDTH), lambda t: (0, t, 0)),
                  full((B, S5_STATE)), full((B, S5_STATE)), full((1, S5_STATE)), full((1, S5_STATE)),
                  full(sp['w_in'].shape), full(sp['w_out_re'].shape), full(sp['w_out_im'].shape),
                  full((1, S5_WIDTH)), full((S5_WIDTH, S5_WIDTH))],
        out_specs=[pl.BlockSpec((nseq, rows, S5_WIDTH), lambda t: (0, t, 0)),
                   full((B, S5_STATE)), full((B, S5_STATE))],
        out_shape=[jax.ShapeDtypeStruct(u.shape, F32),
                   jax.ShapeDtypeStruct((B, S5_STATE), F32), jax.ShapeDtypeStruct((B, S5_STATE), F32)],
        scratch_shapes=[pltpu.VMEM((B, S5_STATE), F32), pltpu.VMEM((B, S5_STATE), F32),
                        pltpu.VMEM((nseq * rows, S5_STATE), F32), pltpu.VMEM((nseq * rows, S5_STATE), F32)],
        compiler_params=_cparams("arbitrary"),
    )(u, h0_re, h0_im, sp['lam_re'], sp['lam_im'], sp['w_in'], sp['w_out_re'], sp['w_out_im'], sp['d'], sp['w_glu'])


def _ones_bd():
    r = lax.broadcasted_iota(jnp.int32, (HW, HW), 0) // DH
    c = lax.broadcasted_iota(jnp.int32, (HW, HW), 1) // DH
    return (r == c).astype(BF16)


def _mlstm_sample_kernel(za_ref, gt_ref, bias_ref, norm_ref, c_ref, n_ref, m_ref,
                         ha_ref, c_out, n_out, m_out, q_scr, kw_scr, h_scr):
    za = za_ref[...]
    q_scr[...] = (za[:, 0:HW] * (DH ** -0.5)).T
    k_t = za[:, HW:2 * HW].T
    v_t = za[:, 2 * HW:3 * HW].T
    g_t = (gt_ref[...] + bias_ref[...]).T
    m_out[...] = jnp.zeros_like(m_out)
    for h in range(HEADS):
        hs = slice(h * DH, (h + 1) * DH)
        i_h = g_t[h:h + 1, :]
        bm = _log_sigmoid(g_t[HEADS + h:HEADS + h + 1, :]) + m_ref[h:h + 1, :]
        m_t = jnp.maximum(i_h, bm)
        w_in = jnp.exp(i_h - m_t)
        w_st = jnp.exp(bm - m_t)
        q_h, k_h, v_h = q_scr[hs, :], k_t[hs, :], v_t[hs, :]
        s = jnp.sum(q_h * k_h, axis=0, keepdims=True) * w_in
        kw_scr[hs, :] = k_h * w_in

        def body(d, acc):
            r = h * DH + d
            rows = pl.ds(pl.multiple_of(r * DH, DH), DH)
            c_hd = c_ref[rows, :]
            c_out[rows, :] = w_st * c_hd + kw_scr[pl.ds(r, 1), :] * v_h
            return acc + q_scr[pl.ds(r, 1), :] * c_hd

        qc = lax.fori_loop(0, DH, body, jnp.zeros((DH, za.shape[0]), F32), unroll=4)
        n_h = n_ref[hs, :]
        num = w_st * qc + s * v_h
        den = w_st * jnp.sum(q_h * n_h, axis=0, keepdims=True) + s
        h_scr[hs, :] = num / jnp.maximum(jnp.abs(den), jnp.exp(-m_t))
        n_out[hs, :] = w_st * n_h + kw_scr[hs, :]
        m_out[h:h + 1, :] = m_t
    ha_ref[...] = _head_rms(h_scr[...].T, _ones_bd(), norm_ref[...]) * _sigmoid(za[:, 3 * HW:4 * HW])


def mlstm_sample(za, gates, b_i, b_f, norm, c_t, n_t, m_t):
    B = za.shape[0]
    bias = jnp.zeros((1, GATE_PAD), F32).at[0, 0:HEADS].set(b_i).at[0, HEADS:2 * HEADS].set(b_f)
    shp = lambda *s: jax.ShapeDtypeStruct(s, F32)
    return pl.pallas_call(
        _mlstm_sample_kernel,
        out_shape=[shp(B, HW), shp(HW * DH, B), shp(HW, B), shp(SUBLANES, B)],
        scratch_shapes=[pltpu.VMEM((HW, B), F32), pltpu.VMEM((HW, B), F32), pltpu.VMEM((HW, B), F32)],
        compiler_params=pltpu.CompilerParams(vmem_limit_bytes=VMEM_LIMIT),
    )(za, gates, bias, norm.reshape(1, HW), c_t, n_t, m_t)


def _gdn_sample_kernel(zc_ref, zg_ref, gt_ref, buf_ref, cw_ref, alog_ref, dtb_ref, norm_ref, s_ref,
                       hc_ref, s_out, buf_out, q_scr, k_scr, o_scr):
    W3 = 3 * HW
    x = zc_ref[...]
    y = cw_ref[CONV_K - 1:CONV_K, :] * x
    for j in range(CONV_K - 1):
        y = y + cw_ref[j:j + 1, :] * buf_ref[:, j * W3:(j + 1) * W3]
    buf_out[:, 0:(CONV_K - 2) * W3] = buf_ref[:, W3:(CONV_K - 1) * W3]
    buf_out[:, (CONV_K - 2) * W3:(CONV_K - 1) * W3] = x
    y = _silu(y)
    ones_bd = _ones_bd()
    q_raw, k_raw = y[:, 0:HW], y[:, HW:2 * HW]
    q_scr[...] = (q_raw * lax.rsqrt(_dot_sel_r(q_raw * q_raw, ones_bd) + EPS) * (DH ** -0.5)).T
    k_scr[...] = (k_raw * lax.rsqrt(_dot_sel_r(k_raw * k_raw, ones_bd) + EPS)).T
    v_t = y[:, 2 * HW:3 * HW].T
    gt = gt_ref[...]
    beta_t = _sigmoid(gt).T
    la_t = (-jnp.exp(alog_ref[...]) * _softplus(gt + dtb_ref[...])).T
    nb = x.shape[0]
    for h in range(HEADS):
        hs = slice(h * DH, (h + 1) * DH)
        beta = beta_t[2 * HEADS + h:2 * HEADS + h + 1, :]
        eg = jnp.exp(la_t[3 * HEADS + h:3 * HEADS + h + 1, :])
        q_h, k_h, v_h = q_scr[hs, :], k_scr[hs, :], v_t[hs, :]

        def read(d, acc):
            ks, qs = acc
            r = h * DH + d
            s_hd = s_ref[pl.ds(pl.multiple_of(r * DH, DH), DH), :]
            return ks + k_scr[pl.ds(r, 1), :] * s_hd, qs + q_scr[pl.ds(r, 1), :] * s_hd

        zero = jnp.zeros((DH, nb), F32)
        ks, qs = lax.fori_loop(0, DH, read, (zero, zero), unroll=4)
        u = v_h - eg * ks
        o_scr[hs, :] = eg * qs + (jnp.sum(q_h * k_h, axis=0, keepdims=True) * beta) * u

        def write(d, carry):
            r = h * DH + d
            rows = pl.ds(pl.multiple_of(r * DH, DH), DH)
            s_out[rows, :] = eg * s_ref[rows, :] + (beta * k_scr[pl.ds(r, 1), :]) * u
            return carry

        lax.fori_loop(0, DH, write, 0, unroll=4)
    hc_ref[...] = _head_rms(o_scr[...].T, ones_bd, norm_ref[...]) * _silu(zg_ref[...])


def gdn_sample(zc, zg, gates, buf, conv_w, a_log, dt_bias, norm, s_t):
    B = zc.shape[0]
    alog, dtb = _gdn_gate_rows(a_log, dt_bias)
    shp = lambda *s: jax.ShapeDtypeStruct(s, F32)
    return pl.pallas_call(
        _gdn_sample_kernel,
        out_shape=[shp(B, HW), shp(HW * DH, B), shp(B, (CONV_K - 1) * 3 * HW)],
        scratch_shapes=[pltpu.VMEM((HW, B), F32), pltpu.VMEM((HW, B), F32), pltpu.VMEM((HW, B), F32)],
        compiler_params=pltpu.CompilerParams(vmem_limit_bytes=VMEM_LIMIT),
    )(zc, zg, gates, buf, conv_w, alog, dtb, jnp.tile(norm, HEADS).reshape(1, HW), s_t)


FFN_CHUNK = 1408
IN_SEGMENTS = (4 * HW, S5_WIDTH, 3 * HW, HW, GATE_PAD)


def _tile(n, pref):
    return pref if n % pref == 0 else n


def _regroup_w_in_kernel(w_ref, o_ref):
    a, g2 = 4 * HW, 2 * HEADS
    mid = sum(IN_SEGMENTS[1:4])
    w = w_ref[...]
    o_ref[:, 0:a] = w[:, 0:a].astype(BF16)
    o_ref[:, a:a + mid] = w[:, a + g2:a + g2 + mid].astype(BF16)
    gates = jnp.concatenate([w[:, a:a + g2], w[:, a + g2 + mid:a + 2 * g2 + mid],
                             jnp.zeros((w.shape[0], GATE_PAD - 2 * g2), F32)], axis=1)
    o_ref[:, a + mid:a + mid + GATE_PAD] = gates.astype(BF16)


def regroup_w_in(w_in, tk):
    L, D, n_in = w_in.shape
    n_out = sum(IN_SEGMENTS)
    return pl.pallas_call(
        _regroup_w_in_kernel,
        grid=(L, D // tk),
        in_specs=[pl.BlockSpec((None, tk, n_in), lambda l, i: (l, i, 0))],
        out_specs=pl.BlockSpec((None, tk, n_out), lambda l, i: (l, i, 0)),
        out_shape=jax.ShapeDtypeStruct((L, D, n_out), BF16),
        compiler_params=_cparams("parallel", "parallel"),
    )(w_in)


def prep_weights(p):
    w_in = regroup_w_in(p['w_in'], 256)
    bf = lambda n: p[n].astype(BF16)
    return {'w_in': w_in, 'w_out': bf('w_out'), 'w_mq': bf('w_mq'), 'w_mo': bf('w_mo'), 'w_mk': bf('w_mk'),
            'w_mv': bf('w_mv'), 'w_gate': bf('w_gate'), 'w_up': bf('w_up'), 'w_down': bf('w_down')}


def mixer_prompt(x, lp, W, layer, B, S):
    T = B * S
    za, zu, zc, zg, gates, tail = in_proj_prompt(x, lp['norm_mix'], W['w_in'], layer, lp['gdn_conv_w'], S,
                                                 _tile(S, 1024))
    buf1 = tail[:, SUBLANES - (CONV_K - 1):, :]
    tt = _tile(S, 1024)
    ha, c1, n1, m1 = mlstm_prompt(za, gates, lp['mlstm_b_i'], lp['mlstm_b_f'], lp['mlstm_norm'], B, S, tt)
    h0 = jnp.zeros((B, S5_STATE), F32)
    ys3, r1, i1 = s5_mixer(zu.reshape(B, S, S5_WIDTH), h0, h0, lp['s5'], _tile(S, 128), False)
    ys = ys3.reshape(T, S5_WIDTH)
    hc, s1 = gdn_prompt(zc, zg, gates, lp['gdn_a_log'], lp['gdn_dt_bias'], lp['gdn_norm'], B, S, _tile(S, 256))
    return [ha, ys, hc], (c1, n1, m1, r1.reshape(B, S5_GROUPS, S5_N), i1.reshape(B, S5_GROUPS, S5_N), s1, buf1)


def mixer_sample(x, st, lp, W, layer):
    B = x.shape[0]
    c0, n0, m0, r0, i0, s0, buf0 = st
    za, zu, zc, zg, gates = norm_matmul(x, lp['norm_mix'], W['w_in'], layer, IN_SEGMENTS, B)
    m_t = jnp.zeros((SUBLANES, B), F32).at[0:HEADS, :].set(m0.T)
    ha, c1t, n1t, m1t = mlstm_sample(za, gates, lp['mlstm_b_i'], lp['mlstm_b_f'], lp['mlstm_norm'],
                                     c0.reshape(B, HW * DH).T, n0.reshape(B, HW).T, m_t)
    ys3, r1, i1 = s5_mixer(zu.reshape(1, B, S5_WIDTH), r0.reshape(B, S5_STATE), i0.reshape(B, S5_STATE), lp['s5'], 1, True)
    hc, s1t, buf1 = gdn_sample(zc, zg, gates, buf0.reshape(B, (CONV_K - 1) * 3 * HW), lp['gdn_conv_w'],
                               lp['gdn_a_log'], lp['gdn_dt_bias'], lp['gdn_norm'], s0.reshape(B, HW * DH).T)
    x1 = matmul_residual(x, [ha, ys3.reshape(B, S5_WIDTH), hc], W['w_out'], layer, B)
    return x1, (c1t.T.reshape(B, HEADS, DH, DH), n1t.T.reshape(B, HEADS, DH), m1t[0:HEADS, :].T,
                r1.reshape(B, S5_GROUPS, S5_N), i1.reshape(B, S5_GROUPS, S5_N),
                s1t.T.reshape(B, HEADS, DH, DH), buf1.reshape(B, CONV_K - 1, 3 * HW))


def _mem_kv_kernel(x_ref, g_ref, wk_ref, wv_ref, k2_ref, v2_ref, k5_ref, v5_ref):
    xn = _rms(x_ref[...], g_ref[...]).astype(BF16)
    for w_ref, o2_ref, o5_ref in ((wk_ref, k2_ref, k5_ref), (wv_ref, v2_ref, v5_ref)):
        y = _dot(xn, w_ref[...])
        o2_ref[...] = y
        o5_ref[...] = y.reshape(o5_ref.shape)


def mem_kv(mem, norm_mem, wk, wv):
    B, M, D = mem.shape
    L = wk.shape[0]
    dh = D // X_HEADS
    w_spec = pl.BlockSpec((None, D, D), lambda l, b: (l, 0, 0))
    o2_spec = pl.BlockSpec((None, None, M, D), lambda l, b: (l, b, 0, 0))
    o5_spec = pl.BlockSpec((None, None, M, X_HEADS, dh), lambda l, b: (l, b, 0, 0, 0))
    return pl.pallas_call(
        _mem_kv_kernel,
        grid=(L, B),
        in_specs=[pl.BlockSpec((None, M, D), lambda l, b: (b, 0, 0)),
                  pl.BlockSpec((None, 1, D), lambda l, b: (l, 0, 0)), w_spec, w_spec],
        out_specs=[o2_spec, o2_spec, o5_spec, o5_spec],
        out_shape=[jax.ShapeDtypeStruct((L, B, M, D), F32)] * 2
                  + [jax.ShapeDtypeStruct((L, B, M, X_HEADS, dh), F32)] * 2,
        compiler_params=_cparams("parallel", "parallel"),
    )(mem, norm_mem.reshape(L, 1, D), wk, wv)


def xattn_ffn_prompt(x, acts, mk, mv, lp, W, layer, S, norm_final, final):
    T, D = x.shape
    x2 = xattn_prompt(x, acts, W['w_out'], lp['norm_xattn'], W['w_mq'], mk, mv, W['w_mo'], layer, S, _tile(S, 1024))
    return ffn(x2, lp['norm_ffn'], W['w_gate'], W['w_up'], W['w_down'], layer, norm_final, final, _tile(T, 512),
               FFN_CHUNK)


def xattn_ffn_sample(x, ck, cv, lp, W, layer, norm_final, final):
    B, D = x.shape
    (q,) = norm_matmul(x, lp['norm_xattn'], W['w_mq'], layer, (D,), B)
    o = xattn_sample(q, ck, cv, layer, 4)
    x2 = matmul_residual(x, [o], W['w_mo'], layer, B)
    return ffn(x2, lp['norm_ffn'], W['w_gate'], W['w_up'], W['w_down'], layer, norm_final, final, B, FFN_CHUNK)


LAYER_PARAMS = ('norm_mix', 'w_in', 'w_out', 'mlstm_b_i', 'mlstm_b_f', 'mlstm_norm', 's5_a_re', 's5_a_im', 's5_log_dt',
                's5_b_re', 's5_b_im', 's5_c_re', 's5_c_im', 's5_d', 's5_w_glu', 'gdn_conv_w', 'gdn_a_log',
                'gdn_dt_bias', 'gdn_norm', 'norm_xattn', 'norm_mem', 'w_mq', 'w_mk', 'w_mv', 'w_mo', 'norm_ffn',
                'w_gate', 'w_up', 'w_down')


def kernel(x_prompt, x_sample, mem_prompt, cache_mem_k, cache_mem_v, state_mlstm_c, state_mlstm_n, state_mlstm_m, state_s5_re, state_s5_im, state_gdn, state_gdn_conv, norm_mix, w_in, w_out, mlstm_b_i, mlstm_b_f, mlstm_norm, s5_a_re, s5_a_im, s5_log_dt, s5_b_re, s5_b_im, s5_c_re, s5_c_im, s5_d, s5_w_glu, gdn_conv_w, gdn_a_log, gdn_dt_bias, gdn_norm, norm_xattn, norm_mem, w_mq, w_mk, w_mv, w_mo, norm_ffn, w_gate, w_up, w_down, norm_final):
    stacked = dict(norm_mix=norm_mix, w_in=w_in, w_out=w_out, mlstm_b_i=mlstm_b_i, mlstm_b_f=mlstm_b_f,
                   mlstm_norm=mlstm_norm, s5_a_re=s5_a_re, s5_a_im=s5_a_im, s5_log_dt=s5_log_dt, s5_b_re=s5_b_re,
                   s5_b_im=s5_b_im, s5_c_re=s5_c_re, s5_c_im=s5_c_im, s5_d=s5_d, s5_w_glu=s5_w_glu,
                   gdn_conv_w=gdn_conv_w, gdn_a_log=gdn_a_log, gdn_dt_bias=gdn_dt_bias, gdn_norm=gdn_norm,
                   norm_xattn=norm_xattn, norm_mem=norm_mem, w_mq=w_mq, w_mk=w_mk, w_mv=w_mv, w_mo=w_mo,
                   norm_ffn=norm_ffn, w_gate=w_gate, w_up=w_up, w_down=w_down)
    B, S, D = x_prompt.shape
    Bs = x_sample.shape[0]
    M = mem_prompt.shape[1]
    depth = w_in.shape[0]
    xp = x_prompt.reshape(B * S, D)
    xs = x_sample.reshape(Bs, D)
    cache_k, cache_v = cache_mem_k, cache_mem_v
    st_p, st_s = [], []
    W = prep_weights(stacked)
    mk, mv, mem_k, mem_v = mem_kv(mem_prompt, norm_mem, W['w_mk'], W['w_mv'])
    for l in range(depth):
        lp = {n: stacked[n][l] for n in LAYER_PARAMS if n not in W}
        lp['s5'] = s5_params(lp)
        last = l == depth - 1
        acts, sp = mixer_prompt(xp, lp, W, l, B, S)
        xp = xattn_ffn_prompt(xp, acts, mk, mv, lp, W, l, S, norm_final, last)
        st_in = (state_mlstm_c[l], state_mlstm_n[l], state_mlstm_m[l], state_s5_re[l], state_s5_im[l],
                 state_gdn[l], state_gdn_conv[l])
        xs, ss = mixer_sample(xs, st_in, lp, W, l)
        xs = xattn_ffn_sample(xs, cache_k, cache_v, lp, W, l, norm_final, last)
        st_p.append(sp)
        st_s.append(ss)
    stack = lambda lst: [jnp.stack([st[i] for st in lst]) for i in range(7)]
    return (xp.reshape(B, S, D), xs.reshape(Bs, 1, D), mem_k, mem_v,
            *stack(st_p), *stack(st_s))
```

```python
import functools
import math

import jax
import jax.numpy as jnp
from jax import lax
from jax.experimental import pallas as pl
from jax.experimental.pallas import tpu as pltpu

F32 = jnp.float32
BF16 = jnp.bfloat16
EPS = 1e-6

HEADS = 4
DH = 64
HW = HEADS * DH
CHUNK = 64
S5_P = 16
S5_N = 64
S5_GROUPS = 32
S5_WIDTH = S5_GROUPS * S5_P
S5_STATE = S5_GROUPS * S5_N
S5_GB = 8
S5_NBLK = S5_GROUPS // S5_GB
S5_SLAB_ROWS = 256
CONV_K = 4
GDN_SEQS = 4
X_HEADS = 4
GATE_PAD = 128
LANES = 128
SUBLANES = 8
VMEM_LIMIT = 48 * 1024 * 1024


def _cparams(*sem):
    return pltpu.CompilerParams(dimension_semantics=sem, vmem_limit_bytes=VMEM_LIMIT)


def _rms(x, g_row):
    return x * lax.rsqrt(jnp.mean(x * x, axis=-1, keepdims=True) + EPS) * g_row


def _dot(a, b):
    return jnp.dot(a, b, preferred_element_type=F32)


def _dot_nt(a, b):
    return lax.dot_general(a, b, (((1,), (1,)), ((), ())), preferred_element_type=F32)


def _sigmoid(x):
    return 1.0 / (1.0 + jnp.exp(-x))


def _silu(x):
    return x * _sigmoid(x)


def _softplus(x):
    return jnp.maximum(x, 0.0) + jnp.log1p(jnp.exp(-jnp.abs(x)))


def _log_sigmoid(x):
    return jnp.minimum(x, 0.0) - jnp.log1p(jnp.exp(-jnp.abs(x)))


def _norm_matmul_kernel(x_ref, g_ref, w_ref, *out_refs, splits):
    xn = _rms(x_ref[...], g_ref[...]).astype(BF16)
    off = 0
    for o_ref, n in zip(out_refs, splits):
        o_ref[...] = _dot(xn, w_ref[:, off:off + n])
        off += n


def _layer_weight(w_all, layer):
    return pl.BlockSpec((None,) + w_all.shape[1:], lambda *_: (layer, 0, 0), pipeline_mode=pl.Buffered(1))


def norm_matmul(x, g, w_all, layer, splits, tm):
    T, D = x.shape
    N = w_all.shape[2]
    assert sum(splits) == N and T % tm == 0
    return pl.pallas_call(
        functools.partial(_norm_matmul_kernel, splits=tuple(splits)),
        grid=(T // tm,),
        in_specs=[pl.BlockSpec((tm, D), lambda i: (i, 0)),
                  pl.BlockSpec((1, D), lambda i: (0, 0)),
                  _layer_weight(w_all, layer)],
        out_specs=[pl.BlockSpec((tm, n), lambda i: (i, 0)) for n in splits],
        out_shape=[jax.ShapeDtypeStruct((T, n), F32) for n in splits],
        compiler_params=_cparams("parallel"),
    )(x, g.reshape(1, D), w_all)


def _in_proj_prompt_kernel(x_ref, g_ref, w_ref, cw_ref, za_ref, zu_ref, zc_ref, zg_ref, gt_ref, tail_ref, xp_scr,
                           *, tiles_per_seq):
    tm = x_ref.shape[0]
    pad = SUBLANES
    xn = _rms(x_ref[...], g_ref[...]).astype(BF16)
    offs = [sum(IN_SEGMENTS[:s]) for s in range(len(IN_SEGMENTS))]

    @pl.when(pl.program_id(0) % tiles_per_seq == 0)
    def _():
        xp_scr[0:pad, :] = jnp.zeros((pad, xp_scr.shape[1]), F32)

    xp_scr[pad:pad + tm, :] = _dot(xn, w_ref[:, offs[2]:offs[2] + IN_SEGMENTS[2]])
    others = ((0, za_ref), (1, zu_ref), (3, zg_ref), (4, gt_ref))
    rows = tm // len(others)
    for r, (s, o_ref) in enumerate(others):
        y = cw_ref[CONV_K - 1:CONV_K, :] * xp_scr[pl.ds(pad + r * rows, rows), :]
        for j in range(CONV_K - 1):
            y = y + cw_ref[j:j + 1, :] * xp_scr[pl.ds(pad + r * rows - (CONV_K - 1) + j, rows), :]
        zc_ref[pl.ds(r * rows, rows), :] = _silu(y)
        o_ref[...] = _dot(xn, w_ref[:, offs[s]:offs[s] + IN_SEGMENTS[s]])
    tail = xp_scr[tm:tm + pad, :]
    xp_scr[0:pad, :] = tail
    tail_ref[0] = tail


def in_proj_prompt(x, g, w_all, layer, conv_w, seq, tm):
    T, D = x.shape
    assert seq % tm == 0 and T % seq == 0
    wc = IN_SEGMENTS[2]
    outs = pl.pallas_call(
        functools.partial(_in_proj_prompt_kernel, tiles_per_seq=seq // tm),
        grid=(T // tm,),
        in_specs=[pl.BlockSpec((tm, D), lambda i: (i, 0)),
                  pl.BlockSpec((1, D), lambda i: (0, 0)),
                  _layer_weight(w_all, layer),
                  pl.BlockSpec((CONV_K, wc), lambda i: (0, 0))],
        out_specs=[pl.BlockSpec((tm, n), lambda i: (i, 0)) for n in IN_SEGMENTS]
                  + [pl.BlockSpec((1, SUBLANES, wc), lambda i: (i // (seq // tm), 0, 0))],
        out_shape=[jax.ShapeDtypeStruct((T, n), F32) for n in IN_SEGMENTS]
                  + [jax.ShapeDtypeStruct((T // seq, SUBLANES, wc), F32)],
        scratch_shapes=[pltpu.VMEM((tm + SUBLANES, wc), F32)],
        compiler_params=_cparams("arbitrary"),
    )(x, g.reshape(1, D), w_all, conv_w)
    return outs


def _matmul_residual_kernel(x_ref, *refs, ksplits):
    a_refs, w_ref, o_ref = refs[:-2], refs[-2], refs[-1]
    acc = x_ref[...]
    off = 0
    for a_ref, k in zip(a_refs, ksplits):
        acc = acc + _dot(a_ref[...].astype(BF16), w_ref[off:off + k, :])
        off += k
    o_ref[...] = acc


def matmul_residual(x, acts, w_all, layer, tm):
    T, D = x.shape
    ks = tuple(a.shape[1] for a in acts)
    K = w_all.shape[1]
    assert sum(ks) == K and T % tm == 0
    return pl.pallas_call(
        functools.partial(_matmul_residual_kernel, ksplits=ks),
        grid=(T // tm,),
        in_specs=[pl.BlockSpec((tm, D), lambda i: (i, 0))]
                 + [pl.BlockSpec((tm, k), lambda i: (i, 0)) for k in ks]
                 + [_layer_weight(w_all, layer)],
        out_specs=pl.BlockSpec((tm, D), lambda i: (i, 0)),
        out_shape=jax.ShapeDtypeStruct((T, D), F32),
        compiler_params=_cparams("parallel"),
    )(x, *acts, w_all)


def _ffn_kernel(x_ref, g_ref, wg_ref, wu_ref, wd_ref, gf_ref, o_ref, *, final_norm, tf):
    x = x_ref[...]
    h = _rms(x, g_ref[...]).astype(BF16)
    y = x
    for j in range(wg_ref.shape[1] // tf):
        cols = slice(j * tf, (j + 1) * tf)
        a = _dot(h, wg_ref[:, cols])
        b = _dot(h, wu_ref[:, cols])
        y = y + _dot((_silu(a) * b).astype(BF16), wd_ref[cols, :])
    if final_norm:
        y = _rms(y, gf_ref[...])
    o_ref[...] = y


def _resident(shape):
    return pl.BlockSpec(shape, lambda *_: (0,) * len(shape), pipeline_mode=pl.Buffered(1))


def ffn(x, g, wg, wu, wd, layer, g_final, final_norm, tm, tf):
    T, D = x.shape
    F = wg.shape[2]
    assert T % tm == 0 and F % tf == 0
    return pl.pallas_call(
        functools.partial(_ffn_kernel, final_norm=final_norm, tf=tf),
        grid=(T // tm,),
        in_specs=[pl.BlockSpec((tm, D), lambda i: (i, 0)),
                  _resident((1, D)), _layer_weight(wg, layer), _layer_weight(wu, layer), _layer_weight(wd, layer),
                  _resident((1, D))],
        out_specs=pl.BlockSpec((tm, D), lambda i: (i, 0)),
        out_shape=jax.ShapeDtypeStruct((T, D), F32),
        compiler_params=_cparams("parallel"),
    )(x, g.reshape(1, D), wg, wu, wd, g_final.reshape(1, D))


def _softmax_rows(s):
    e = jnp.exp(s - jnp.max(s, axis=-1, keepdims=True))
    return e / jnp.sum(e, axis=-1, keepdims=True)


def _xattn_prompt_kernel(x_ref, *refs, dh, n_acts):
    a_refs = refs[:n_acts]
    wmix_ref, g_ref, wq_ref, k_ref, v_ref, wo_ref, o_ref = refs[n_acts:]
    scale = dh ** -0.5
    x = x_ref[...]
    off = 0
    for a_ref in a_refs:
        x = x + _dot(a_ref[...].astype(BF16), wmix_ref[off:off + a_ref.shape[1], :])
        off += a_ref.shape[1]
    q = _dot(_rms(x, g_ref[...]).astype(BF16), wq_ref[...]).astype(BF16)
    sl = [slice(h * dh, (h + 1) * dh) for h in range(X_HEADS)]
    s = [_dot_nt(q[:, c], k_ref[0, :, c].astype(BF16)) * scale for c in sl]
    p = [_softmax_rows(sh).astype(BF16) for sh in s]
    heads = [_dot(ph, v_ref[0, :, c].astype(BF16)).astype(BF16) for ph, c in zip(p, sl)]
    o_ref[...] = x + _dot(jnp.concatenate(heads, axis=-1), wo_ref[...])


def xattn_prompt(x, acts, wmix, g, wq, mk, mv, wo, layer, seq, tq):
    T, D = x.shape
    _, B, M, _ = mk.shape
    nt = seq // tq
    rows = lambda w: pl.BlockSpec((tq, w), lambda b, t: (b * nt + t, 0))
    return pl.pallas_call(
        functools.partial(_xattn_prompt_kernel, dh=D // X_HEADS, n_acts=len(acts)),
        grid=(B, nt),
        in_specs=[rows(D)] + [rows(a.shape[1]) for a in acts]
                 + [_layer_weight(wmix, layer),
                    pl.BlockSpec((1, D), lambda b, t: (0, 0)),
                    _layer_weight(wq, layer),
                    pl.BlockSpec((None, 1, M, D), lambda b, t: (layer, b, 0, 0)),
                    pl.BlockSpec((None, 1, M, D), lambda b, t: (layer, b, 0, 0)),
                    _layer_weight(wo, layer)],
        out_specs=rows(D),
        out_shape=jax.ShapeDtypeStruct((T, D), F32),
        compiler_params=_cparams("parallel", "parallel"),
    )(x, *acts, wmix, g.reshape(1, D), wq, mk, mv, wo)


def _xattn_sample_kernel(q_ref, k_ref, v_ref, o_ref, *, sb):
    M, H, dh = k_ref.shape[1:]
    scale = dh ** -0.5
    row = lax.broadcasted_iota(jnp.int32, (SUBLANES, M * H), 0)
    col_head = lax.broadcasted_iota(jnp.int32, (SUBLANES, M * H), 1) % H
    own = (row % H) == col_head
    pad = jnp.zeros((SUBLANES - H, dh), F32)
    q8 = [jnp.concatenate([q_ref[i], pad], axis=0).astype(BF16) for i in range(sb)]
    s = [_dot_nt(q8[i], k_ref[i].reshape(M * H, dh).astype(BF16)) * scale for i in range(sb)]
    p = [_softmax_rows(jnp.where(own, si, -jnp.inf)).astype(BF16) for si in s]
    for i in range(sb):
        o_ref[i] = _dot(p[i], v_ref[i].reshape(M * H, dh).astype(BF16))[0:H]


def xattn_sample(q, ck, cv, layer, sb):
    B, D = q.shape
    _, _, M, H, dh = ck.shape
    out = pl.pallas_call(
        functools.partial(_xattn_sample_kernel, sb=sb),
        grid=(B // sb,),
        in_specs=[pl.BlockSpec((sb, H, dh), lambda i: (i, 0, 0)),
                  pl.BlockSpec((None, sb, M, H, dh), lambda i: (layer, i, 0, 0, 0)),
                  pl.BlockSpec((None, sb, M, H, dh), lambda i: (layer, i, 0, 0, 0))],
        out_specs=pl.BlockSpec((sb, H, dh), lambda i: (i, 0, 0)),
        out_shape=jax.ShapeDtypeStruct((B, H, dh), F32),
        compiler_params=_cparams("parallel"),
    )(q.reshape(B, H, dh), ck, cv)
    return out.reshape(B, D)


def _unrolled(n, body, carry):
    for i in range(n):
        carry = body(i, carry)
    return carry


def _lane_cat_masks(L):
    row = lax.broadcasted_iota(jnp.int32, (L, HW), 0)
    j = lax.broadcasted_iota(jnp.int32, (L, HW), 1) % DH
    r2 = lax.broadcasted_iota(jnp.int32, (HW, HW), 0) // DH
    c2 = lax.broadcasted_iota(jnp.int32, (HW, HW), 1) // DH
    return row >= j, row > j, row == j, r2 == c2


def _expand_bd(x, bd):
    return jnp.where(bd, jnp.concatenate([x] * HEADS, axis=0), jnp.zeros((), x.dtype))


def _seg_reduce(x, op, fill):
    lo = lax.broadcasted_iota(jnp.int32, (x.shape[0], LANES), 1) < DH
    parts = []
    for c in range(HW // LANES):
        xh = x[:, c * LANES:(c + 1) * LANES]
        a = op(jnp.where(lo, xh, fill), axis=-1, keepdims=True)
        b = op(jnp.where(lo, fill, xh), axis=-1, keepdims=True)
        parts.append(jnp.where(lo, a, b))
    return jnp.concatenate(parts, axis=-1)


def _head_expander(first_lane):
    r = lax.broadcasted_iota(jnp.int32, (GATE_PAD, HW), 0)
    c = lax.broadcasted_iota(jnp.int32, (GATE_PAD, HW), 1) // DH
    return (r == c + first_lane).astype(BF16)


def _chunk_cumsum(x):
    tt, w = x.shape
    g = SUBLANES
    x3 = x.reshape(tt // g, g, w)
    sub = lax.broadcasted_iota(jnp.int32, x3.shape, 1)
    s = 1
    while s < g:
        x3 = x3 + jnp.where(sub >= s, pltpu.roll(x3, s, 1), 0.0)
        s *= 2
    per = CHUNK // g
    x4 = x3.reshape(tt // CHUNK, per, g, w)
    acc = jnp.zeros((tt // CHUNK, 1, 1, w), F32)
    parts = []
    for i in range(per):
        parts.append(x4[:, i:i + 1] + acc)
        acc = acc + x4[:, i:i + 1, g - 1:g, :]
    return jnp.concatenate(parts, axis=1).reshape(tt, w)


def _split3(x):
    hi = x.astype(BF16)
    r = x - hi.astype(F32)
    mid = r.astype(BF16)
    return hi, mid, (r - mid.astype(F32)).astype(BF16)


def _dot_sel_r(x, sel):
    hi, mid, lo = _split3(x)
    return (_dot(lo, sel) + _dot(mid, sel)) + _dot(hi, sel)


def _col_to_row(x, eye):
    return jnp.sum(jnp.where(eye, x, 0.0), axis=0, keepdims=True)


def _head_rms(x, ones_bd, g_row):
    ms = _dot_sel_r(x * x, ones_bd) * (1.0 / DH)
    return x * lax.rsqrt(ms + EPS) * g_row


def _mlstm_prompt_kernel(za_ref, gt_ref, bias_ref, norm_ref, ha_ref, c_ref, n_ref, m_ref,
                         c_scr, n_scr, m_scr, ix_scr, bx_scr, *, nchunks, group):
    tb = pl.program_id(1)
    L = CHUNK
    tril, _, eye, bd = _lane_cat_masks(L)
    ones_bd = bd.astype(BF16)

    @pl.when(tb == 0)
    def _():
        c_scr[...] = jnp.zeros_like(c_scr)
        n_scr[...] = jnp.zeros_like(n_scr)
        m_scr[...] = jnp.zeros_like(m_scr)

    gt = gt_ref[...] + bias_ref[...]
    b_cols = _chunk_cumsum(_log_sigmoid(gt))
    ix_scr[...] = _dot_sel_r(gt, _head_expander(0))
    bx_scr[...] = _dot_sel_r(b_cols, _head_expander(HEADS))

    def chunks(gi, carry):
        c_bd, n_row, m_x = carry
        rows = [pl.ds(pl.multiple_of((gi * group + j) * L, L), L) for j in range(group)]
        q = [(za_ref[r, 0:HW] * (DH ** -0.5)).astype(BF16) for r in rows]
        k = [za_ref[r, HW:2 * HW] for r in rows]
        v = [za_ref[r, 2 * HW:3 * HW].astype(BF16) for r in rows]
        i_x = [ix_scr[r, :] for r in rows]
        b_x = [bx_scr[r, :] for r in rows]

        m_in, n_in, kw, decay = [], [], [], []
        for kk, ii, bb in zip(k, i_x, b_x):
            b_last = bb[L - 1:L, :]
            g_x = b_last - bb + ii
            m_new = jnp.maximum(b_last + m_x, jnp.max(g_x, axis=0, keepdims=True))
            kw.append(kk * jnp.exp(g_x - m_new))
            decay.append(jnp.exp(b_last + m_x - m_new))
            m_in.append(m_x)
            n_in.append(n_row)
            n_row = decay[-1] * n_row + jnp.sum(kw[-1], axis=0, keepdims=True)
            m_x = m_new

        d_intra = [jnp.where(tril, bb - _col_to_row(bb, eye) + _col_to_row(ii, eye), -jnp.inf) for bb, ii in zip(b_x, i_x)]
        d_inter = [bb + mm for bb, mm in zip(b_x, m_in)]
        m_t = [jnp.maximum(_seg_reduce(di, jnp.max, -jnp.inf), de) for di, de in zip(d_intra, d_inter)]
        w_inter = [jnp.exp(de - mt) for de, mt in zip(d_inter, m_t)]
        s = [_dot_nt(qq, _expand_bd(kk.astype(BF16), bd)) * jnp.exp(di - mt) for qq, kk, di, mt in zip(q, k, d_intra, m_t)]
        kv = [jnp.where(bd, _dot(kwj.T.astype(BF16), vv), 0.0) for kwj, vv in zip(kw, v)]
        c_in = []
        for dj, kvj in zip(decay, kv):
            c_in.append(c_bd)
            c_bd = dj * c_bd + kvj
        num = [wi * _dot(qq, cc.astype(BF16)) + _dot(ss.astype(BF16), _expand_bd(vv, bd))
               for wi, qq, cc, ss, vv in zip(w_inter, q, c_in, s, v)]
        den = [wi * _dot((qq.astype(F32) * nn).astype(BF16), ones_bd) + _seg_reduce(ss, jnp.sum, 0.0)
               for wi, qq, nn, ss in zip(w_inter, q, n_in, s)]
        for r, nu, de, mt in zip(rows, num, den, m_t):
            hh = nu / jnp.maximum(jnp.abs(de), jnp.exp(-mt))
            ha_ref[r, :] = _head_rms(hh, ones_bd, norm_ref[...]) * _sigmoid(za_ref[r, 3 * HW:4 * HW])
        return c_bd, n_row, m_x

    c_bd, n_row, m_x = _unrolled(nchunks // group, chunks, (c_scr[...], n_scr[...], m_scr[...]))
    c_scr[...] = c_bd
    n_scr[...] = n_row
    m_scr[...] = m_x

    @pl.when(tb == pl.num_programs(1) - 1)
    def _():
        for h in range(HEADS):
            c_ref[0, h] = c_scr[h * DH:(h + 1) * DH, h * DH:(h + 1) * DH]
        n_ref[0] = n_scr[...]
        m_ref[0] = m_scr[...]


def mlstm_prompt(za, gates, b_i, b_f, norm, B, S, tt):
    assert S % tt == 0 and tt % CHUNK == 0
    nt = S // tt
    bias = jnp.zeros((1, GATE_PAD), F32).at[0, 0:HEADS].set(b_i).at[0, HEADS:2 * HEADS].set(b_f)
    ha, c, n, m = pl.pallas_call(
        functools.partial(_mlstm_prompt_kernel, nchunks=tt // CHUNK, group=math.gcd(tt // CHUNK, 8)),
        grid=(B, nt),
        in_specs=[pl.BlockSpec((tt, 4 * HW), lambda b, t: (b * nt + t, 0)),
                  pl.BlockSpec((tt, GATE_PAD), lambda b, t: (b * nt + t, 0)),
                  pl.BlockSpec((1, GATE_PAD), lambda b, t: (0, 0)),
                  pl.BlockSpec((1, HW), lambda b, t: (0, 0))],
        out_specs=[pl.BlockSpec((tt, HW), lambda b, t: (b * nt + t, 0)),
                   pl.BlockSpec((1, HEADS, DH, DH), lambda b, t: (b, 0, 0, 0)),
                   pl.BlockSpec((1, 1, HW), lambda b, t: (b, 0, 0)),
                   pl.BlockSpec((1, 1, HW), lambda b, t: (b, 0, 0))],
        out_shape=[jax.ShapeDtypeStruct((B * S, HW), F32),
                   jax.ShapeDtypeStruct((B, HEADS, DH, DH), F32),
                   jax.ShapeDtypeStruct((B, 1, HW), F32),
                   jax.ShapeDtypeStruct((B, 1, HW), F32)],
        scratch_shapes=[pltpu.VMEM((HW, HW), F32), pltpu.VMEM((1, HW), F32), pltpu.VMEM((1, HW), F32),
                        pltpu.VMEM((tt, HW), F32), pltpu.VMEM((tt, HW), F32)],
        compiler_params=_cparams("parallel", "arbitrary"),
    )(za, gates, bias, norm.reshape(1, HW))
    return ha, c, n.reshape(B, HEADS, DH), m[:, 0, ::DH]


def _gdn_prompt_kernel(zc_ref, zg_ref, gt_ref, alog_ref, dtb_ref, norm_ref,
                       hc_ref, s_ref,
                       s_scr, q_scr, k_scr, v_scr, beta_scr, g_scr, uv_scr, wq_scr, qkm_scr, kwt_scr,
                       *, nseq, nchunks, group):
    tb = pl.program_id(1)
    L = CHUNK
    tt = nchunks * L
    tril, strict, eye, bd = _lane_cat_masks(L)
    ones_bd = bd.astype(BF16)

    @pl.when(tb == 0)
    def _():
        s_scr[...] = jnp.zeros_like(s_scr)

    for i in range(nseq):
        sr = slice(i * tt, (i + 1) * tt)
        y = zc_ref[i]
        q_raw, k_raw = y[:, 0:HW], y[:, HW:2 * HW]
        q_scr[sr, :] = (q_raw * lax.rsqrt(_dot_sel_r(q_raw * q_raw, ones_bd) + EPS) * (DH ** -0.5)).astype(BF16)
        k_scr[sr, :] = k_raw * lax.rsqrt(_dot_sel_r(k_raw * k_raw, ones_bd) + EPS)
        v_scr[sr, :] = y[:, 2 * HW:3 * HW]
        gt = gt_ref[i]
        beta_scr[sr, :] = _dot_sel_r(_sigmoid(gt), _head_expander(2 * HEADS))
        la_cols = -jnp.exp(alog_ref[...]) * _softplus(gt + dtb_ref[...])
        g_scr[sr, :] = _dot_sel_r(_chunk_cumsum(la_cols), _head_expander(3 * HEADS))

    def prepare(gi, carry):
        cis = [gi * group + j for j in range(group)]
        rows = [pl.ds(pl.multiple_of(ci * L, L), L) for ci in cis]
        k = [k_scr[r, :] for r in rows]
        g_x = [g_scr[r, :] for r in rows]
        beta_row = [_col_to_row(beta_scr[r, :], eye) for r in rows]
        dec_incl = [jnp.where(tril, jnp.exp(jnp.where(tril, g - _col_to_row(g, eye), 0.0)), 0.0) for g in g_x]
        k_bd = [_expand_bd(kk.astype(BF16), bd) for kk in k]
        n0 = [-(jnp.where(strict, d, 0.0) * _dot_nt(kk.astype(BF16), kbd) * br)
              for d, kk, kbd, br in zip(dec_incl, k, k_bd, beta_row)]
        for r, d, kbd, br in zip(rows, dec_incl, k_bd, beta_row):
            qkm_scr[r, :] = (_dot_nt(q_scr[r, :], kbd) * d * br).astype(BF16)

        p = [_dot(n.astype(BF16), _expand_bd(n.astype(BF16), bd)) for n in n0]
        m = n0
        steps = int(math.log2(L)) - 1
        for i in range(steps):
            p_bd = [_expand_bd(pp.astype(BF16), bd) for pp in p]
            if i < steps - 1:
                pm = [_dot(jnp.concatenate([pp, mm], axis=0).astype(BF16), pbd) for pp, mm, pbd in zip(p, m, p_bd)]
                p_next, mp = [x[0:L] for x in pm], [x[L:2 * L] for x in pm]
            else:
                p_next, mp = None, [_dot(mm.astype(BF16), pbd) for mm, pbd in zip(m, p_bd)]
            m = [mm + pp + x for mm, pp, x in zip(m, p, mp)]
            p = p_next

        for ci, r, kk, g, mm in zip(cis, rows, k, g_x, m):
            v = v_scr[r, :]
            egk = jnp.exp(g) * kk
            rhs_bd = jnp.concatenate([_expand_bd(v.astype(BF16), bd), _expand_bd(egk.astype(BF16), bd)], axis=1)
            mr = _dot(mm.astype(BF16), rhs_bd)
            uv_scr[r, :] = v + mr[:, 0:HW]
            wq_rows = pl.multiple_of(ci * 2 * L, 2 * L)
            wq_scr[pl.ds(wq_rows, L), :] = (egk + mr[:, HW:2 * HW]).astype(BF16)
            wq_scr[pl.ds(wq_rows + L, L), :] = q_scr[r, :]
            kw = kk * (jnp.exp(g[L - 1:L, :] - g) * beta_scr[r, :])
            kwt_scr[pl.ds(pl.multiple_of(ci * HW, HW), HW), :] = kw.T.astype(BF16)
        return carry

    _unrolled(nseq * nchunks // group, prepare, 0)

    def advance(c, states):
        cis = [i * nchunks + c for i in range(nseq)]
        rows = [pl.ds(pl.multiple_of(ci * L, L), L) for ci in cis]
        wqs = [_dot(wq_scr[pl.ds(pl.multiple_of(ci * 2 * L, 2 * L), 2 * L), :], s.astype(BF16))
               for ci, s in zip(cis, states)]
        ub = [(uv_scr[r, :] - x[0:L]).astype(BF16) for r, x in zip(rows, wqs)]
        new = [jnp.exp(g_scr[pl.ds(pl.multiple_of(ci * L + L - SUBLANES, SUBLANES), SUBLANES), :][SUBLANES - 1:, :]) * s
               + jnp.where(bd, _dot(kwt_scr[pl.ds(pl.multiple_of(ci * HW, HW), HW), :], u), 0.0)
               for ci, s, u in zip(cis, states, ub)]
        for i, (r, x, u) in enumerate(zip(rows, wqs, ub)):
            o = jnp.exp(g_scr[r, :]) * x[L:2 * L] + _dot(qkm_scr[r, :], _expand_bd(u, bd))
            hc_ref[i, pl.ds(pl.multiple_of(c * L, L), L), :] = (
                _head_rms(o, ones_bd, norm_ref[...]) * _silu(zg_ref[i, pl.ds(pl.multiple_of(c * L, L), L), :]))
        return tuple(new)

    states = _unrolled(nchunks, advance, tuple(s_scr[i] for i in range(nseq)))
    for i in range(nseq):
        s_scr[i] = states[i]

    @pl.when(tb == pl.num_programs(1) - 1)
    def _():
        for i in range(nseq):
            for h in range(HEADS):
                s_ref[i, h] = s_scr[i, h * DH:(h + 1) * DH, h * DH:(h + 1) * DH]


def _gdn_gate_rows(a_log, dt_bias):
    z = jnp.zeros((1, GATE_PAD), F32)
    return (z.at[0, 3 * HEADS:4 * HEADS].set(a_log), z.at[0, 3 * HEADS:4 * HEADS].set(dt_bias))


def gdn_prompt(zc, zg, gates, a_log, dt_bias, norm, B, S, tt):
    assert S % tt == 0 and tt % CHUNK == 0
    nt = S // tt
    nseq = math.gcd(B, GDN_SEQS)
    rows = nseq * tt
    alog, dtb = _gdn_gate_rows(a_log, dt_bias)
    blk = lambda w: pl.BlockSpec((nseq, tt, w), lambda b, t: (b, t, 0))
    row = lambda w: pl.BlockSpec((1, w), lambda b, t: (0, 0))
    hc, s = pl.pallas_call(
        functools.partial(_gdn_prompt_kernel, nseq=nseq, nchunks=tt // CHUNK, group=math.gcd(rows // CHUNK, 16)),
        grid=(B // nseq, nt),
        in_specs=[blk(3 * HW), blk(HW), blk(GATE_PAD), row(GATE_PAD), row(GATE_PAD), row(HW)],
        out_specs=[blk(HW), pl.BlockSpec((nseq, HEADS, DH, DH), lambda b, t: (b, 0, 0, 0))],
        out_shape=[jax.ShapeDtypeStruct((B, S, HW), F32), jax.ShapeDtypeStruct((B, HEADS, DH, DH), F32)],
        scratch_shapes=[pltpu.VMEM((nseq, HW, HW), F32),
                        pltpu.VMEM((rows, HW), BF16), pltpu.VMEM((rows, HW), F32), pltpu.VMEM((rows, HW), F32),
                        pltpu.VMEM((rows, HW), F32), pltpu.VMEM((rows, HW), F32), pltpu.VMEM((rows, HW), F32),
                        pltpu.VMEM((2 * rows, HW), BF16), pltpu.VMEM((rows, HW), BF16),
                        pltpu.VMEM((rows // CHUNK * HW, CHUNK), BF16)],
        compiler_params=_cparams("parallel", "arbitrary"),
    )(zc.reshape(B, S, 3 * HW), zg.reshape(B, S, HW), gates.reshape(B, S, GATE_PAD), alog, dtb,
      jnp.tile(norm, HEADS).reshape(1, HW))
    return hc.reshape(B * S, HW), s


def _s5_prep_kernel(are_ref, aim_ref, ldt_ref, bre_ref, bim_ref, lre_ref, lim_ref, bbre_ref, bbim_ref):
    a_re, a_im = are_ref[...], aim_ref[...]
    dt = jnp.exp(ldt_ref[...])
    mag = jnp.exp(a_re * dt)
    lam_re, lam_im = mag * jnp.cos(a_im * dt), mag * jnp.sin(a_im * dt)
    lre_ref[...] = lam_re
    lim_ref[...] = lam_im
    nr, ni = lam_re - 1.0, lam_im
    den = a_re * a_re + a_im * a_im
    coef_re = (nr * a_re + ni * a_im) / den
    coef_im = (ni * a_re - nr * a_im) / den
    b_re, b_im = bre_ref[...], bim_ref[...]
    bbre_ref[...] = coef_re * b_re - coef_im * b_im
    bbim_ref[...] = coef_re * b_im + coef_im * b_re


def s5_params(lp):
    G, N, P = S5_GROUPS, S5_N, S5_P
    row = lambda a: a.astype(F32).reshape(1, G * N)
    to_pn = lambda b: jnp.transpose(b.astype(F32), (2, 0, 1)).reshape(P, G * N)
    shp = [jax.ShapeDtypeStruct((1, G * N), F32)] * 2 + [jax.ShapeDtypeStruct((P, G * N), F32)] * 2
    lam_re, lam_im, bb_re, bb_im = pl.pallas_call(_s5_prep_kernel, out_shape=shp)(
        row(lp['s5_a_re']), row(lp['s5_a_im']), row(jnp.repeat(lp['s5_log_dt'][:, None], N, axis=1)),
        to_pn(lp['s5_b_re']), to_pn(lp['s5_b_im']))
    eye = jnp.eye(S5_GB, dtype=F32)

    def w_in_blocks(bb):
        b4 = bb.reshape(P, S5_NBLK, S5_GB, N)
        return jnp.einsum('pbgn,gh->bgphn', b4, eye).reshape(S5_NBLK, S5_GB * P, S5_GB * N)

    def w_out_blocks(c):
        c4 = c.astype(F32).reshape(S5_NBLK, S5_GB, P, N)
        return jnp.einsum('bgpn,gh->bgnhp', c4, eye).reshape(S5_NBLK, S5_GB * N, S5_GB * P)

    w_in = jnp.concatenate([w_in_blocks(bb_re), w_in_blocks(bb_im)], axis=-1).astype(BF16)
    return {'lam_re': lam_re, 'lam_im': lam_im, 'w_in': w_in,
            'w_out_re': w_out_blocks(lp['s5_c_re']).astype(BF16),
            'w_out_im': (-w_out_blocks(lp['s5_c_im'])).astype(BF16),
            'd': lp['s5_d'].astype(F32).reshape(1, S5_WIDTH), 'w_glu': lp['s5_w_glu'].astype(BF16)}


def _s5_kernel(u_ref, h0r_ref, h0i_ref, lamr_ref, lami_ref, win_ref, wor_ref, woi_ref, d_ref, wglu_ref,
               ys_ref, h1r_ref, h1i_ref, hr_scr, hi_scr, br_scr, bi_scr, *, nseq, rows, bb, lane_blk):
    tb = pl.program_id(0)
    blk_in, blk_st = S5_GB * S5_P, S5_GB * S5_N

    @pl.when(tb == 0)
    def _():
        hr_scr[...] = h0r_ref[...]
        hi_scr[...] = h0i_ref[...]

    if nseq > 1:
        u = jnp.swapaxes(u_ref[...], 0, 1).reshape(nseq * rows, S5_WIDTH)
    else:
        u = u_ref[0]
    ub = u.astype(BF16)
    lanes = [slice(lb * lane_blk, (lb + 1) * lane_blk) for lb in range(S5_STATE // lane_blk)]
    lam = [(jnp.broadcast_to(lamr_ref[:, ls], (bb, lane_blk)), jnp.broadcast_to(lami_ref[:, ls], (bb, lane_blk)))
           for ls in lanes]
    h = [(hr_scr[:, ls], hi_scr[:, ls]) for ls in lanes]

    slab = min(nseq * rows, S5_SLAB_ROWS)
    for c in range(nseq * rows // slab):
        rs = slice(c * slab, (c + 1) * slab)
        for blk in range(S5_NBLK):
            bu = _dot(ub[rs, blk * blk_in:(blk + 1) * blk_in], win_ref[blk])
            br_scr[rs, blk * blk_st:(blk + 1) * blk_st] = bu[:, 0:blk_st]
            bi_scr[rs, blk * blk_st:(blk + 1) * blk_st] = bu[:, blk_st:2 * blk_st]

        for lb, ls in enumerate(lanes):
            (lr, li), (hr, hi) = lam[lb], h[lb]
            for t in range(slab // bb):
                r = slice(c * slab + t * bb, c * slab + (t + 1) * bb)
                hr, hi = lr * hr - li * hi + br_scr[r, ls], lr * hi + li * hr + bi_scr[r, ls]
                br_scr[r, ls] = hr
                bi_scr[r, ls] = hi
            h[lb] = (hr, hi)

        ys = []
        for blk in range(S5_NBLK):
            st = slice(blk * blk_st, (blk + 1) * blk_st)
            ys.append(_dot(br_scr[rs, st].astype(BF16), wor_ref[blk]) + _dot(bi_scr[rs, st].astype(BF16), woi_ref[blk]))
        gy = jax.nn.gelu(jnp.concatenate(ys, axis=-1) + d_ref[...] * u[rs])
        out = gy * _sigmoid(_dot(gy.astype(BF16), wglu_ref[...]))
        if nseq > 1:
            tpc = slab // nseq
            ys_ref[:, c * tpc:(c + 1) * tpc, :] = jnp.swapaxes(out.reshape(tpc, nseq, S5_WIDTH), 0, 1)
        else:
            ys_ref[0, rs, :] = out

    for ls, (hr, hi) in zip(lanes, h):
        hr_scr[:, ls] = hr
        hi_scr[:, ls] = hi

    @pl.when(tb == pl.num_programs(0) - 1)
    def _():
        h1r_ref[...] = hr_scr[...]
        h1i_ref[...] = hi_scr[...]


def s5_mixer(u, h0_re, h0_im, sp, tt, single_step):
    B = h0_re.shape[0]
    nseq, S = (1, 1) if single_step else (B, u.shape[1])
    rows = B if single_step else tt
    assert S % tt == 0 and B % SUBLANES == 0
    assert u.shape == ((1, B, S5_WIDTH) if single_step else (B, S, S5_WIDTH))
    lane_blk = max(LANES, min(S5_STATE, (SUBLANES * SUBLANES * LANES) // B))
    full = lambda shape: pl.BlockSpec(shape, lambda t: (0,) * len(shape))
    return pl.pallas_call(
        functools.partial(_s5_kernel, nseq=nseq, rows=rows, bb=B, lane_blk=lane_blk),
        grid=(S // tt,),
        in_specs=[pl.BlockSpec((nseq, rows, S5_WIDTH), lambda t: (0, t, 0)),
                  full((B, S5_STATE)), full((B, S5_STATE)), full((1, S5_STATE)), full((1, S5_STATE)),
                  full(sp['w_in'].shape), full(sp['w_out_re'].shape), full(sp['w_out_im'].shape),
                  full((1, S5_WIDTH)), full((S5_WIDTH, S5_WIDTH))],
        out_specs=[pl.BlockSpec((nseq, rows, S5_WIDTH), lambda t: (0, t, 0)),
                   full((B, S5_STATE)), full((B, S5_STATE))],
        out_shape=[jax.ShapeDtypeStruct(u.shape, F32),
                   jax.ShapeDtypeStruct((B, S5_STATE), F32), jax.ShapeDtypeStruct((B, S5_STATE), F32)],
        scratch_shapes=[pltpu.VMEM((B, S5_STATE), F32), pltpu.VMEM((B, S5_STATE), F32),
                        pltpu.VMEM((nseq * rows, S5_STATE), F32), pltpu.VMEM((nseq * rows, S5_STATE), F32)],
        compiler_params=_cparams("arbitrary"),
    )(u, h0_re, h0_im, sp['lam_re'], sp['lam_im'], sp['w_in'], sp['w_out_re'], sp['w_out_im'], sp['d'], sp['w_glu'])


def _ones_bd():
    r = lax.broadcasted_iota(jnp.int32, (HW, HW), 0) // DH
    c = lax.broadcasted_iota(jnp.int32, (HW, HW), 1) // DH
    return (r == c).astype(BF16)


def _mlstm_sample_kernel(za_ref, gt_ref, bias_ref, norm_ref, c_ref, n_ref, m_ref,
                         ha_ref, c_out, n_out, m_out, q_scr, kw_scr, h_scr):
    za = za_ref[...]
    q_scr[...] = (za[:, 0:HW] * (DH ** -0.5)).T
    k_t = za[:, HW:2 * HW].T
    v_t = za[:, 2 * HW:3 * HW].T
    g_t = (gt_ref[...] + bias_ref[...]).T
    m_out[...] = jnp.zeros_like(m_out)
    for h in range(HEADS):
        hs = slice(h * DH, (h + 1) * DH)
        i_h = g_t[h:h + 1, :]
        bm = _log_sigmoid(g_t[HEADS + h:HEADS + h + 1, :]) + m_ref[h:h + 1, :]
        m_t = jnp.maximum(i_h, bm)
        w_in = jnp.exp(i_h - m_t)
        w_st = jnp.exp(bm - m_t)
        q_h, k_h, v_h = q_scr[hs, :], k_t[hs, :], v_t[hs, :]
        s = jnp.sum(q_h * k_h, axis=0, keepdims=True) * w_in
        kw_scr[hs, :] = k_h * w_in

        def body(d, acc):
            r = h * DH + d
            rows = pl.ds(pl.multiple_of(r * DH, DH), DH)
            c_hd = c_ref[rows, :]
            c_out[rows, :] = w_st * c_hd + kw_scr[pl.ds(r, 1), :] * v_h
            return acc + q_scr[pl.ds(r, 1), :] * c_hd

        qc = lax.fori_loop(0, DH, body, jnp.zeros((DH, za.shape[0]), F32), unroll=4)
        n_h = n_ref[hs, :]
        num = w_st * qc + s * v_h
        den = w_st * jnp.sum(q_h * n_h, axis=0, keepdims=True) + s
        h_scr[hs, :] = num / jnp.maximum(jnp.abs(den), jnp.exp(-m_t))
        n_out[hs, :] = w_st * n_h + kw_scr[hs, :]
        m_out[h:h + 1, :] = m_t
    ha_ref[...] = _head_rms(h_scr[...].T, _ones_bd(), norm_ref[...]) * _sigmoid(za[:, 3 * HW:4 * HW])


def mlstm_sample(za, gates, b_i, b_f, norm, c_t, n_t, m_t):
    B = za.shape[0]
    bias = jnp.zeros((1, GATE_PAD), F32).at[0, 0:HEADS].set(b_i).at[0, HEADS:2 * HEADS].set(b_f)
    shp = lambda *s: jax.ShapeDtypeStruct(s, F32)
    return pl.pallas_call(
        _mlstm_sample_kernel,
        out_shape=[shp(B, HW), shp(HW * DH, B), shp(HW, B), shp(SUBLANES, B)],
        scratch_shapes=[pltpu.VMEM((HW, B), F32), pltpu.VMEM((HW, B), F32), pltpu.VMEM((HW, B), F32)],
        compiler_params=pltpu.CompilerParams(vmem_limit_bytes=VMEM_LIMIT),
    )(za, gates, bias, norm.reshape(1, HW), c_t, n_t, m_t)


def _gdn_sample_kernel(zc_ref, zg_ref, gt_ref, buf_ref, cw_ref, alog_ref, dtb_ref, norm_ref, s_ref,
                       hc_ref, s_out, buf_out, q_scr, k_scr, o_scr):
    W3 = 3 * HW
    x = zc_ref[...]
    y = cw_ref[CONV_K - 1:CONV_K, :] * x
    for j in range(CONV_K - 1):
        y = y + cw_ref[j:j + 1, :] * buf_ref[:, j * W3:(j + 1) * W3]
    buf_out[:, 0:(CONV_K - 2) * W3] = buf_ref[:, W3:(CONV_K - 1) * W3]
    buf_out[:, (CONV_K - 2) * W3:(CONV_K - 1) * W3] = x
    y = _silu(y)
    ones_bd = _ones_bd()
    q_raw, k_raw = y[:, 0:HW], y[:, HW:2 * HW]
    q_scr[...] = (q_raw * lax.rsqrt(_dot_sel_r(q_raw * q_raw, ones_bd) + EPS) * (DH ** -0.5)).T
    k_scr[...] = (k_raw * lax.rsqrt(_dot_sel_r(k_raw * k_raw, ones_bd) + EPS)).T
    v_t = y[:, 2 * HW:3 * HW].T
    gt = gt_ref[...]
    beta_t = _sigmoid(gt).T
    la_t = (-jnp.exp(alog_ref[...]) * _softplus(gt + dtb_ref[...])).T
    nb = x.shape[0]
    for h in range(HEADS):
        hs = slice(h * DH, (h + 1) * DH)
        beta = beta_t[2 * HEADS + h:2 * HEADS + h + 1, :]
        eg = jnp.exp(la_t[3 * HEADS + h:3 * HEADS + h + 1, :])
        q_h, k_h, v_h = q_scr[hs, :], k_scr[hs, :], v_t[hs, :]

        def read(d, acc):
            ks, qs = acc
            r = h * DH + d
            s_hd = s_ref[pl.ds(pl.multiple_of(r * DH, DH), DH), :]
            return ks + k_scr[pl.ds(r, 1), :] * s_hd, qs + q_scr[pl.ds(r, 1), :] * s_hd

        zero = jnp.zeros((DH, nb), F32)
        ks, qs = lax.fori_loop(0, DH, read, (zero, zero), unroll=4)
        u = v_h - eg * ks
        o_scr[hs, :] = eg * qs + (jnp.sum(q_h * k_h, axis=0, keepdims=True) * beta) * u

        def write(d, carry):
            r = h * DH + d
            rows = pl.ds(pl.multiple_of(r * DH, DH), DH)
            s_out[rows, :] = eg * s_ref[rows, :] + (beta * k_scr[pl.ds(r, 1), :]) * u
            return carry

        lax.fori_loop(0, DH, write, 0, unroll=4)
    hc_ref[...] = _head_rms(o_scr[...].T, ones_bd, norm_ref[...]) * _silu(zg_ref[...])


def gdn_sample(zc, zg, gates, buf, conv_w, a_log, dt_bias, norm, s_t):
    B = zc.shape[0]
    alog, dtb = _gdn_gate_rows(a_log, dt_bias)
    shp = lambda *s: jax.ShapeDtypeStruct(s, F32)
    return pl.pallas_call(
        _gdn_sample_kernel,
        out_shape=[shp(B, HW), shp(HW * DH, B), shp(B, (CONV_K - 1) * 3 * HW)],
        scratch_shapes=[pltpu.VMEM((HW, B), F32), pltpu.VMEM((HW, B), F32), pltpu.VMEM((HW, B), F32)],
        compiler_params=pltpu.CompilerParams(vmem_limit_bytes=VMEM_LIMIT),
    )(zc, zg, gates, buf, conv_w, alog, dtb, jnp.tile(norm, HEADS).reshape(1, HW), s_t)


FFN_CHUNK = 704
IN_SEGMENTS = (4 * HW, S5_WIDTH, 3 * HW, HW, GATE_PAD)


def _tile(n, pref):
    return pref if n % pref == 0 else n


def _regroup_w_in_kernel(w_ref, o_ref):
    a, g2 = 4 * HW, 2 * HEADS
    mid = sum(IN_SEGMENTS[1:4])
    w = w_ref[...]
    o_ref[:, 0:a] = w[:, 0:a].astype(BF16)
    o_ref[:, a:a + mid] = w[:, a + g2:a + g2 + mid].astype(BF16)
    gates = jnp.concatenate([w[:, a:a + g2], w[:, a + g2 + mid:a + 2 * g2 + mid],
                             jnp.zeros((w.shape[0], GATE_PAD - 2 * g2), F32)], axis=1)
    o_ref[:, a + mid:a + mid + GATE_PAD] = gates.astype(BF16)


def regroup_w_in(w_in, tk):
    L, D, n_in = w_in.shape
    n_out = sum(IN_SEGMENTS)
    return pl.pallas_call(
        _regroup_w_in_kernel,
        grid=(L, D // tk),
        in_specs=[pl.BlockSpec((None, tk, n_in), lambda l, i: (l, i, 0))],
        out_specs=pl.BlockSpec((None, tk, n_out), lambda l, i: (l, i, 0)),
        out_shape=jax.ShapeDtypeStruct((L, D, n_out), BF16),
        compiler_params=_cparams("parallel", "parallel"),
    )(w_in)


def prep_weights(p):
    w_in = regroup_w_in(p['w_in'], 256)
    bf = lambda n: p[n].astype(BF16)
    return {'w_in': w_in, 'w_out': bf('w_out'), 'w_mq': bf('w_mq'), 'w_mo': bf('w_mo'), 'w_mk': bf('w_mk'),
            'w_mv': bf('w_mv'), 'w_gate': bf('w_gate'), 'w_up': bf('w_up'), 'w_down': bf('w_down')}


def mixer_prompt(x, lp, W, layer, B, S):
    T = B * S
    za, zu, zc, zg, gates, tail = in_proj_prompt(x, lp['norm_mix'], W['w_in'], layer, lp['gdn_conv_w'], S,
                                                 _tile(S, 1024))
    buf1 = tail[:, SUBLANES - (CONV_K - 1):, :]
    tt = _tile(S, 1024)
    ha, c1, n1, m1 = mlstm_prompt(za, gates, lp['mlstm_b_i'], lp['mlstm_b_f'], lp['mlstm_norm'], B, S, tt)
    h0 = jnp.zeros((B, S5_STATE), F32)
    ys3, r1, i1 = s5_mixer(zu.reshape(B, S, S5_WIDTH), h0, h0, lp['s5'], _tile(S, 128), False)
    ys = ys3.reshape(T, S5_WIDTH)
    hc, s1 = gdn_prompt(zc, zg, gates, lp['gdn_a_log'], lp['gdn_dt_bias'], lp['gdn_norm'], B, S, _tile(S, 256))
    return [ha, ys, hc], (c1, n1, m1, r1.reshape(B, S5_GROUPS, S5_N), i1.reshape(B, S5_GROUPS, S5_N), s1, buf1)


def mixer_sample(x, st, lp, W, layer):
    B = x.shape[0]
    c0, n0, m0, r0, i0, s0, buf0 = st
    za, zu, zc, zg, gates = norm_matmul(x, lp['norm_mix'], W['w_in'], layer, IN_SEGMENTS, B)
    m_t = jnp.zeros((SUBLANES, B), F32).at[0:HEADS, :].set(m0.T)
    ha, c1t, n1t, m1t = mlstm_sample(za, gates, lp['mlstm_b_i'], lp['mlstm_b_f'], lp['mlstm_norm'],
                                     c0.reshape(B, HW * DH).T, n0.reshape(B, HW).T, m_t)
    ys3, r1, i1 = s5_mixer(zu.reshape(1, B, S5_WIDTH), r0.reshape(B, S5_STATE), i0.reshape(B, S5_STATE), lp['s5'], 1, True)
    hc, s1t, buf1 = gdn_sample(zc, zg, gates, buf0.reshape(B, (CONV_K - 1) * 3 * HW), lp['gdn_conv_w'],
                               lp['gdn_a_log'], lp['gdn_dt_bias'], lp['gdn_norm'], s0.reshape(B, HW * DH).T)
    x1 = matmul_residual(x, [ha, ys3.reshape(B, S5_WIDTH), hc], W['w_out'], layer, B)
    return x1, (c1t.T.reshape(B, HEADS, DH, DH), n1t.T.reshape(B, HEADS, DH), m1t[0:HEADS, :].T,
                r1.reshape(B, S5_GROUPS, S5_N), i1.reshape(B, S5_GROUPS, S5_N),
                s1t.T.reshape(B, HEADS, DH, DH), buf1.reshape(B, CONV_K - 1, 3 * HW))


def _mem_kv_kernel(x_ref, g_ref, wk_ref, wv_ref, k2_ref, v2_ref, k5_ref, v5_ref):
    xn = _rms(x_ref[...], g_ref[...]).astype(BF16)
    for w_ref, o2_ref, o5_ref in ((wk_ref, k2_ref, k5_ref), (wv_ref, v2_ref, v5_ref)):
        y = _dot(xn, w_ref[...])
        o2_ref[...] = y
        o5_ref[...] = y.reshape(o5_ref.shape)


def mem_kv(mem, norm_mem, wk, wv):
    B, M, D = mem.shape
    L = wk.shape[0]
    dh = D // X_HEADS
    w_spec = pl.BlockSpec((None, D, D), lambda l, b: (l, 0, 0))
    o2_spec = pl.BlockSpec((None, None, M, D), lambda l, b: (l, b, 0, 0))
    o5_spec = pl.BlockSpec((None, None, M, X_HEADS, dh), lambda l, b: (l, b, 0, 0, 0))
    return pl.pallas_call(
        _mem_kv_kernel,
        grid=(L, B),
        in_specs=[pl.BlockSpec((None, M, D), lambda l, b: (b, 0, 0)),
                  pl.BlockSpec((None, 1, D), lambda l, b: (l, 0, 0)), w_spec, w_spec],
        out_specs=[o2_spec, o2_spec, o5_spec, o5_spec],
        out_shape=[jax.ShapeDtypeStruct((L, B, M, D), F32)] * 2
                  + [jax.ShapeDtypeStruct((L, B, M, X_HEADS, dh), F32)] * 2,
        compiler_params=_cparams("parallel", "parallel"),
    )(mem, norm_mem.reshape(L, 1, D), wk, wv)


def xattn_ffn_prompt(x, acts, mk, mv, lp, W, layer, S, norm_final, final):
    T, D = x.shape
    x2 = xattn_prompt(x, acts, W['w_out'], lp['norm_xattn'], W['w_mq'], mk, mv, W['w_mo'], layer, S, _tile(S, 1024))
    return ffn(x2, lp['norm_ffn'], W['w_gate'], W['w_up'], W['w_down'], layer, norm_final, final, _tile(T, 1024),
               FFN_CHUNK)


def xattn_ffn_sample(x, ck, cv, lp, W, layer, norm_final, final):
    B, D = x.shape
    (q,) = norm_matmul(x, lp['norm_xattn'], W['w_mq'], layer, (D,), B)
    o = xattn_sample(q, ck, cv, layer, 8)
    x2 = matmul_residual(x, [o], W['w_mo'], layer, B)
    return ffn(x2, lp['norm_ffn'], W['w_gate'], W['w_up'], W['w_down'], layer, norm_final, final, B, FFN_CHUNK)


LAYER_PARAMS = ('norm_mix', 'w_in', 'w_out', 'mlstm_b_i', 'mlstm_b_f', 'mlstm_norm', 's5_a_re', 's5_a_im', 's5_log_dt',
                's5_b_re', 's5_b_im', 's5_c_re', 's5_c_im', 's5_d', 's5_w_glu', 'gdn_conv_w', 'gdn_a_log',
                'gdn_dt_bias', 'gdn_norm', 'norm_xattn', 'norm_mem', 'w_mq', 'w_mk', 'w_mv', 'w_mo', 'norm_ffn',
                'w_gate', 'w_up', 'w_down')


def kernel(x_prompt, x_sample, mem_prompt, cache_mem_k, cache_mem_v, state_mlstm_c, state_mlstm_n, state_mlstm_m, state_s5_re, state_s5_im, state_gdn, state_gdn_conv, norm_mix, w_in, w_out, mlstm_b_i, mlstm_b_f, mlstm_norm, s5_a_re, s5_a_im, s5_log_dt, s5_b_re, s5_b_im, s5_c_re, s5_c_im, s5_d, s5_w_glu, gdn_conv_w, gdn_a_log, gdn_dt_bias, gdn_norm, norm_xattn, norm_mem, w_mq, w_mk, w_mv, w_mo, norm_ffn, w_gate, w_up, w_down, norm_final):
    stacked = dict(norm_mix=norm_mix, w_in=w_in, w_out=w_out, mlstm_b_i=mlstm_b_i, mlstm_b_f=mlstm_b_f,
                   mlstm_norm=mlstm_norm, s5_a_re=s5_a_re, s5_a_im=s5_a_im, s5_log_dt=s5_log_dt, s5_b_re=s5_b_re,
                   s5_b_im=s5_b_im, s5_c_re=s5_c_re, s5_c_im=s5_c_im, s5_d=s5_d, s5_w_glu=s5_w_glu,
                   gdn_conv_w=gdn_conv_w, gdn_a_log=gdn_a_log, gdn_dt_bias=gdn_dt_bias, gdn_norm=gdn_norm,
                   norm_xattn=norm_xattn, norm_mem=norm_mem, w_mq=w_mq, w_mk=w_mk, w_mv=w_mv, w_mo=w_mo,
                   norm_ffn=norm_ffn, w_gate=w_gate, w_up=w_up, w_down=w_down)
    B, S, D = x_prompt.shape
    Bs = x_sample.shape[0]
    M = mem_prompt.shape[1]
    depth = w_in.shape[0]
    xp = x_prompt.reshape(B * S, D)
    xs = x_sample.reshape(Bs, D)
    cache_k, cache_v = cache_mem_k, cache_mem_v
    st_p, st_s = [], []
    W = prep_weights(stacked)
    mk, mv, mem_k, mem_v = mem_kv(mem_prompt, norm_mem, W['w_mk'], W['w_mv'])
    for l in range(depth):
        lp = {n: stacked[n][l] for n in LAYER_PARAMS if n not in W}
        lp['s5'] = s5_params(lp)
        last = l == depth - 1
        acts, sp = mixer_prompt(xp, lp, W, l, B, S)
        xp = xattn_ffn_prompt(xp, acts, mk, mv, lp, W, l, S, norm_final, last)
        st_in = (state_mlstm_c[l], state_mlstm_n[l], state_mlstm_m[l], state_s5_re[l], state_s5_im[l],
                 state_gdn[l], state_gdn_conv[l])
        xs, ss = mixer_sample(xs, st_in, lp, W, l)
        xs = xattn_ffn_sample(xs, cache_k, cache_v, lp, W, l, norm_final, last)
        st_p.append(sp)
        st_s.append(ss)
    stack = lambda lst: [jnp.stack([st[i] for st in lst]) for i in range(7)]
    return (xp.reshape(B, S, D), xs.reshape(Bs, 1, D), mem_k, mem_v,
            *stack(st_p), *stack(st_s))
```

```python
import functools
import math

import jax
import jax.numpy as jnp
from jax import lax
from jax.experimental import pallas as pl
from jax.experimental.pallas import tpu as pltpu

F32 = jnp.float32
BF16 = jnp.bfloat16
EPS = 1e-6

HEADS = 4
DH = 64
HW = HEADS * DH
CHUNK = 64
S5_P = 16
S5_N = 64
S5_GROUPS = 32
S5_WIDTH = S5_GROUPS * S5_P
S5_STATE = S5_GROUPS * S5_N
S5_GB = 8
S5_NBLK = S5_GROUPS // S5_GB
S5_SLAB_ROWS = 256
CONV_K = 4
GDN_SEQS = 4
X_HEADS = 4
GATE_PAD = 128
LANES = 128
SUBLANES = 8
VMEM_LIMIT = 48 * 1024 * 1024


def _cparams(*sem):
    return pltpu.CompilerParams(dimension_semantics=sem, vmem_limit_bytes=VMEM_LIMIT)


def _rms(x, g_row):
    return x * lax.rsqrt(jnp.mean(x * x, axis=-1, keepdims=True) + EPS) * g_row


def _dot(a, b):
    return jnp.dot(a, b, preferred_element_type=F32)


def _dot_nt(a, b):
    return lax.dot_general(a, b, (((1,), (1,)), ((), ())), preferred_element_type=F32)


def _sigmoid(x):
    return 1.0 / (1.0 + jnp.exp(-x))


def _silu(x):
    return x * _sigmoid(x)


def _softplus(x):
    return jnp.maximum(x, 0.0) + jnp.log1p(jnp.exp(-jnp.abs(x)))


def _log_sigmoid(x):
    return jnp.minimum(x, 0.0) - jnp.log1p(jnp.exp(-jnp.abs(x)))


def _norm_matmul_kernel(x_ref, g_ref, w_ref, *out_refs, splits):
    xn = _rms(x_ref[...], g_ref[...]).astype(BF16)
    off = 0
    for o_ref, n in zip(out_refs, splits):
        o_ref[...] = _dot(xn, w_ref[:, off:off + n])
        off += n


def _layer_weight(w_all, layer):
    return pl.BlockSpec((None,) + w_all.shape[1:], lambda *_: (layer, 0, 0), pipeline_mode=pl.Buffered(1))


def norm_matmul(x, g, w_all, layer, splits, tm):
    T, D = x.shape
    N = w_all.shape[2]
    assert sum(splits) == N and T % tm == 0
    return pl.pallas_call(
        functools.partial(_norm_matmul_kernel, splits=tuple(splits)),
        grid=(T // tm,),
        in_specs=[pl.BlockSpec((tm, D), lambda i: (i, 0)),
                  pl.BlockSpec((1, D), lambda i: (0, 0)),
                  _layer_weight(w_all, layer)],
        out_specs=[pl.BlockSpec((tm, n), lambda i: (i, 0)) for n in splits],
        out_shape=[jax.ShapeDtypeStruct((T, n), F32) for n in splits],
        compiler_params=_cparams("parallel"),
    )(x, g.reshape(1, D), w_all)


def _in_proj_prompt_kernel(x_ref, g_ref, w_ref, cw_ref, za_ref, zu_ref, zc_ref, zg_ref, gt_ref, tail_ref, xp_scr,
                           *, tiles_per_seq):
    tm = x_ref.shape[0]
    pad = SUBLANES
    xn = _rms(x_ref[...], g_ref[...]).astype(BF16)
    offs = [sum(IN_SEGMENTS[:s]) for s in range(len(IN_SEGMENTS))]

    @pl.when(pl.program_id(0) % tiles_per_seq == 0)
    def _():
        xp_scr[0:pad, :] = jnp.zeros((pad, xp_scr.shape[1]), F32)

    xp_scr[pad:pad + tm, :] = _dot(xn, w_ref[:, offs[2]:offs[2] + IN_SEGMENTS[2]])
    others = ((0, za_ref), (1, zu_ref), (3, zg_ref), (4, gt_ref))
    rows = tm // len(others)
    for r, (s, o_ref) in enumerate(others):
        y = cw_ref[CONV_K - 1:CONV_K, :] * xp_scr[pl.ds(pad + r * rows, rows), :]
        for j in range(CONV_K - 1):
            y = y + cw_ref[j:j + 1, :] * xp_scr[pl.ds(pad + r * rows - (CONV_K - 1) + j, rows), :]
        zc_ref[pl.ds(r * rows, rows), :] = _silu(y)
        o_ref[...] = _dot(xn, w_ref[:, offs[s]:offs[s] + IN_SEGMENTS[s]])
    tail = xp_scr[tm:tm + pad, :]
    xp_scr[0:pad, :] = tail
    tail_ref[0] = tail


def in_proj_prompt(x, g, w_all, layer, conv_w, seq, tm):
    T, D = x.shape
    assert seq % tm == 0 and T % seq == 0
    wc = IN_SEGMENTS[2]
    outs = pl.pallas_call(
        functools.partial(_in_proj_prompt_kernel, tiles_per_seq=seq // tm),
        grid=(T // tm,),
        in_specs=[pl.BlockSpec((tm, D), lambda i: (i, 0)),
                  pl.BlockSpec((1, D), lambda i: (0, 0)),
                  _layer_weight(w_all, layer),
                  pl.BlockSpec((CONV_K, wc), lambda i: (0, 0))],
        out_specs=[pl.BlockSpec((tm, n), lambda i: (i, 0)) for n in IN_SEGMENTS]
                  + [pl.BlockSpec((1, SUBLANES, wc), lambda i: (i // (seq // tm), 0, 0))],
        out_shape=[jax.ShapeDtypeStruct((T, n), F32) for n in IN_SEGMENTS]
                  + [jax.ShapeDtypeStruct((T // seq, SUBLANES, wc), F32)],
        scratch_shapes=[pltpu.VMEM((tm + SUBLANES, wc), F32)],
        compiler_params=_cparams("arbitrary"),
    )(x, g.reshape(1, D), w_all, conv_w)
    return outs


def _matmul_residual_kernel(x_ref, *refs, ksplits):
    a_refs, w_ref, o_ref = refs[:-2], refs[-2], refs[-1]
    acc = x_ref[...]
    off = 0
    for a_ref, k in zip(a_refs, ksplits):
        acc = acc + _dot(a_ref[...].astype(BF16), w_ref[off:off + k, :])
        off += k
    o_ref[...] = acc


def matmul_residual(x, acts, w_all, layer, tm):
    T, D = x.shape
    ks = tuple(a.shape[1] for a in acts)
    K = w_all.shape[1]
    assert sum(ks) == K and T % tm == 0
    return pl.pallas_call(
        functools.partial(_matmul_residual_kernel, ksplits=ks),
        grid=(T // tm,),
        in_specs=[pl.BlockSpec((tm, D), lambda i: (i, 0))]
                 + [pl.BlockSpec((tm, k), lambda i: (i, 0)) for k in ks]
                 + [_layer_weight(w_all, layer)],
        out_specs=pl.BlockSpec((tm, D), lambda i: (i, 0)),
        out_shape=jax.ShapeDtypeStruct((T, D), F32),
        compiler_params=_cparams("parallel"),
    )(x, *acts, w_all)


def _ffn_kernel(x_ref, g_ref, wg_ref, wu_ref, wd_ref, gf_ref, o_ref, *, final_norm, tf):
    x = x_ref[...]
    h = _rms(x, g_ref[...]).astype(BF16)
    y = x
    for j in range(wg_ref.shape[1] // tf):
        cols = slice(j * tf, (j + 1) * tf)
        a = _dot(h, wg_ref[:, cols])
        b = _dot(h, wu_ref[:, cols])
        y = y + _dot((_silu(a) * b).astype(BF16), wd_ref[cols, :])
    if final_norm:
        y = _rms(y, gf_ref[...])
    o_ref[...] = y


def _resident(shape):
    return pl.BlockSpec(shape, lambda *_: (0,) * len(shape), pipeline_mode=pl.Buffered(1))


def ffn(x, g, wg, wu, wd, layer, g_final, final_norm, tm, tf):
    T, D = x.shape
    F = wg.shape[2]
    assert T % tm == 0 and F % tf == 0
    return pl.pallas_call(
        functools.partial(_ffn_kernel, final_norm=final_norm, tf=tf),
        grid=(T // tm,),
        in_specs=[pl.BlockSpec((tm, D), lambda i: (i, 0)),
                  _resident((1, D)), _layer_weight(wg, layer), _layer_weight(wu, layer), _layer_weight(wd, layer),
                  _resident((1, D))],
        out_specs=pl.BlockSpec((tm, D), lambda i: (i, 0)),
        out_shape=jax.ShapeDtypeStruct((T, D), F32),
        compiler_params=_cparams("parallel"),
    )(x, g.reshape(1, D), wg, wu, wd, g_final.reshape(1, D))


def _softmax_rows(s):
    e = jnp.exp(s - jnp.max(s, axis=-1, keepdims=True))
    return e / jnp.sum(e, axis=-1, keepdims=True)


def _xattn_prompt_kernel(x_ref, *refs, dh, n_acts):
    a_refs = refs[:n_acts]
    wmix_ref, g_ref, wq_ref, k_ref, v_ref, wo_ref, o_ref = refs[n_acts:]
    scale = dh ** -0.5
    x = x_ref[...]
    off = 0
    for a_ref in a_refs:
        x = x + _dot(a_ref[...].astype(BF16), wmix_ref[off:off + a_ref.shape[1], :])
        off += a_ref.shape[1]
    q = _dot(_rms(x, g_ref[...]).astype(BF16), wq_ref[...]).astype(BF16)
    sl = [slice(h * dh, (h + 1) * dh) for h in range(X_HEADS)]
    s = [_dot_nt(q[:, c], k_ref[0, :, c]) * scale for c in sl]
    p = [_softmax_rows(sh).astype(BF16) for sh in s]
    heads = [_dot(ph, v_ref[0, :, c]).astype(BF16) for ph, c in zip(p, sl)]
    o_ref[...] = x + _dot(jnp.concatenate(heads, axis=-1), wo_ref[...])


def xattn_prompt(x, acts, wmix, g, wq, mk, mv, wo, layer, seq, tq):
    T, D = x.shape
    _, B, M, _ = mk.shape
    nt = seq // tq
    rows = lambda w: pl.BlockSpec((tq, w), lambda b, t: (b * nt + t, 0))
    return pl.pallas_call(
        functools.partial(_xattn_prompt_kernel, dh=D // X_HEADS, n_acts=len(acts)),
        grid=(B, nt),
        in_specs=[rows(D)] + [rows(a.shape[1]) for a in acts]
                 + [_layer_weight(wmix, layer),
                    pl.BlockSpec((1, D), lambda b, t: (0, 0)),
                    _layer_weight(wq, layer),
                    pl.BlockSpec((None, 1, M, D), lambda b, t: (layer, b, 0, 0)),
                    pl.BlockSpec((None, 1, M, D), lambda b, t: (layer, b, 0, 0)),
                    _layer_weight(wo, layer)],
        out_specs=rows(D),
        out_shape=jax.ShapeDtypeStruct((T, D), F32),
        compiler_params=_cparams("parallel", "parallel"),
    )(x, *acts, wmix, g.reshape(1, D), wq, mk, mv, wo)


def _xattn_sample_kernel(q_ref, k_ref, v_ref, o_ref, *, sb):
    M, H, dh = k_ref.shape[1:]
    scale = dh ** -0.5
    row = lax.broadcasted_iota(jnp.int32, (SUBLANES, M * H), 0)
    col_head = lax.broadcasted_iota(jnp.int32, (SUBLANES, M * H), 1) % H
    own = (row % H) == col_head
    pad = jnp.zeros((SUBLANES - H, dh), F32)
    q8 = [jnp.concatenate([q_ref[i], pad], axis=0).astype(BF16) for i in range(sb)]
    s = [_dot_nt(q8[i], k_ref[i].reshape(M * H, dh).astype(BF16)) * scale for i in range(sb)]
    p = [_softmax_rows(jnp.where(own, si, -jnp.inf)).astype(BF16) for si in s]
    for i in range(sb):
        o_ref[i] = _dot(p[i], v_ref[i].reshape(M * H, dh).astype(BF16))[0:H]


def xattn_sample(q, ck, cv, layer, sb):
    B, D = q.shape
    _, _, M, H, dh = ck.shape
    out = pl.pallas_call(
        functools.partial(_xattn_sample_kernel, sb=sb),
        grid=(B // sb,),
        in_specs=[pl.BlockSpec((sb, H, dh), lambda i: (i, 0, 0)),
                  pl.BlockSpec((None, sb, M, H, dh), lambda i: (layer, i, 0, 0, 0)),
                  pl.BlockSpec((None, sb, M, H, dh), lambda i: (layer, i, 0, 0, 0))],
        out_specs=pl.BlockSpec((sb, H, dh), lambda i: (i, 0, 0)),
        out_shape=jax.ShapeDtypeStruct((B, H, dh), F32),
        compiler_params=_cparams("parallel"),
    )(q.reshape(B, H, dh), ck, cv)
    return out.reshape(B, D)


def _unrolled(n, body, carry):
    for i in range(n):
        carry = body(i, carry)
    return carry


def _lane_cat_masks(L):
    row = lax.broadcasted_iota(jnp.int32, (L, HW), 0)
    j = lax.broadcasted_iota(jnp.int32, (L, HW), 1) % DH
    r2 = lax.broadcasted_iota(jnp.int32, (HW, HW), 0) // DH
    c2 = lax.broadcasted_iota(jnp.int32, (HW, HW), 1) // DH
    return row >= j, row > j, row == j, r2 == c2


def _expand_bd(x, bd):
    return jnp.where(bd, jnp.concatenate([x] * HEADS, axis=0), jnp.zeros((), x.dtype))


def _seg_reduce(x, op, fill):
    lo = lax.broadcasted_iota(jnp.int32, (x.shape[0], LANES), 1) < DH
    parts = []
    for c in range(HW // LANES):
        xh = x[:, c * LANES:(c + 1) * LANES]
        a = op(jnp.where(lo, xh, fill), axis=-1, keepdims=True)
        b = op(jnp.where(lo, fill, xh), axis=-1, keepdims=True)
        parts.append(jnp.where(lo, a, b))
    return jnp.concatenate(parts, axis=-1)


def _head_expander(first_lane):
    r = lax.broadcasted_iota(jnp.int32, (GATE_PAD, HW), 0)
    c = lax.broadcasted_iota(jnp.int32, (GATE_PAD, HW), 1) // DH
    return (r == c + first_lane).astype(BF16)


def _chunk_cumsum(x):
    tt, w = x.shape
    g = SUBLANES
    x3 = x.reshape(tt // g, g, w)
    sub = lax.broadcasted_iota(jnp.int32, x3.shape, 1)
    s = 1
    while s < g:
        x3 = x3 + jnp.where(sub >= s, pltpu.roll(x3, s, 1), 0.0)
        s *= 2
    per = CHUNK // g
    x4 = x3.reshape(tt // CHUNK, per, g, w)
    acc = jnp.zeros((tt // CHUNK, 1, 1, w), F32)
    parts = []
    for i in range(per):
        parts.append(x4[:, i:i + 1] + acc)
        acc = acc + x4[:, i:i + 1, g - 1:g, :]
    return jnp.concatenate(parts, axis=1).reshape(tt, w)


def _split3(x):
    hi = x.astype(BF16)
    r = x - hi.astype(F32)
    mid = r.astype(BF16)
    return hi, mid, (r - mid.astype(F32)).astype(BF16)


def _dot_sel_r(x, sel):
    hi, mid, lo = _split3(x)
    return (_dot(lo, sel) + _dot(mid, sel)) + _dot(hi, sel)


def _col_to_row(x, eye):
    return jnp.sum(jnp.where(eye, x, 0.0), axis=0, keepdims=True)


def _head_rms(x, ones_bd, g_row):
    ms = _dot_sel_r(x * x, ones_bd) * (1.0 / DH)
    return x * lax.rsqrt(ms + EPS) * g_row


def _mlstm_prompt_kernel(za_ref, gt_ref, bias_ref, norm_ref, ha_ref, c_ref, n_ref, m_ref,
                         c_scr, n_scr, m_scr, ix_scr, bx_scr, *, nchunks, group):
    tb = pl.program_id(1)
    L = CHUNK
    tril, _, eye, bd = _lane_cat_masks(L)
    ones_bd = bd.astype(BF16)

    @pl.when(tb == 0)
    def _():
        c_scr[...] = jnp.zeros_like(c_scr)
        n_scr[...] = jnp.zeros_like(n_scr)
        m_scr[...] = jnp.zeros_like(m_scr)

    gt = gt_ref[...] + bias_ref[...]
    b_cols = _chunk_cumsum(_log_sigmoid(gt))
    ix_scr[...] = _dot_sel_r(gt, _head_expander(0))
    bx_scr[...] = _dot_sel_r(b_cols, _head_expander(HEADS))

    def chunks(gi, carry):
        c_bd, n_row, m_x = carry
        rows = [pl.ds(pl.multiple_of((gi * group + j) * L, L), L) for j in range(group)]
        q = [(za_ref[r, 0:HW] * (DH ** -0.5)).astype(BF16) for r in rows]
        k = [za_ref[r, HW:2 * HW] for r in rows]
        v = [za_ref[r, 2 * HW:3 * HW].astype(BF16) for r in rows]
        i_x = [ix_scr[r, :] for r in rows]
        b_x = [bx_scr[r, :] for r in rows]

        m_in, n_in, kw, decay = [], [], [], []
        for kk, ii, bb in zip(k, i_x, b_x):
            b_last = bb[L - 1:L, :]
            g_x = b_last - bb + ii
            m_new = jnp.maximum(b_last + m_x, jnp.max(g_x, axis=0, keepdims=True))
            kw.append(kk * jnp.exp(g_x - m_new))
            decay.append(jnp.exp(b_last + m_x - m_new))
            m_in.append(m_x)
            n_in.append(n_row)
            n_row = decay[-1] * n_row + jnp.sum(kw[-1], axis=0, keepdims=True)
            m_x = m_new

        d_intra = [jnp.where(tril, bb - _col_to_row(bb, eye) + _col_to_row(ii, eye), -jnp.inf) for bb, ii in zip(b_x, i_x)]
        d_inter = [bb + mm for bb, mm in zip(b_x, m_in)]
        m_t = [jnp.maximum(_seg_reduce(di, jnp.max, -jnp.inf), de) for di, de in zip(d_intra, d_inter)]
        w_inter = [jnp.exp(de - mt) for de, mt in zip(d_inter, m_t)]
        s = [_dot_nt(qq, _expand_bd(kk.astype(BF16), bd)) * jnp.exp(di - mt) for qq, kk, di, mt in zip(q, k, d_intra, m_t)]
        kv = [jnp.where(bd, _dot(kwj.T.astype(BF16), vv), 0.0) for kwj, vv in zip(kw, v)]
        c_in = []
        for dj, kvj in zip(decay, kv):
            c_in.append(c_bd)
            c_bd = dj * c_bd + kvj
        num = [wi * _dot(qq, cc.astype(BF16)) + _dot(ss.astype(BF16), _expand_bd(vv, bd))
               for wi, qq, cc, ss, vv in zip(w_inter, q, c_in, s, v)]
        den = [wi * _dot((qq.astype(F32) * nn).astype(BF16), ones_bd) + _seg_reduce(ss, jnp.sum, 0.0)
               for wi, qq, nn, ss in zip(w_inter, q, n_in, s)]
        for r, nu, de, mt in zip(rows, num, den, m_t):
            hh = nu / jnp.maximum(jnp.abs(de), jnp.exp(-mt))
            ha_ref[r, :] = _head_rms(hh, ones_bd, norm_ref[...]) * _sigmoid(za_ref[r, 3 * HW:4 * HW])
        return c_bd, n_row, m_x

    c_bd, n_row, m_x = _unrolled(nchunks // group, chunks, (c_scr[...], n_scr[...], m_scr[...]))
    c_scr[...] = c_bd
    n_scr[...] = n_row
    m_scr[...] = m_x

    @pl.when(tb == pl.num_programs(1) - 1)
    def _():
        for h in range(HEADS):
            c_ref[0, h] = c_scr[h * DH:(h + 1) * DH, h * DH:(h + 1) * DH]
        n_ref[0] = n_scr[...]
        m_ref[0] = m_scr[...]


def mlstm_prompt(za, gates, b_i, b_f, norm, B, S, tt):
    assert S % tt == 0 and tt % CHUNK == 0
    nt = S // tt
    bias = jnp.zeros((1, GATE_PAD), F32).at[0, 0:HEADS].set(b_i).at[0, HEADS:2 * HEADS].set(b_f)
    ha, c, n, m = pl.pallas_call(
        functools.partial(_mlstm_prompt_kernel, nchunks=tt // CHUNK, group=math.gcd(tt // CHUNK, 8)),
        grid=(B, nt),
        in_specs=[pl.BlockSpec((tt, 4 * HW), lambda b, t: (b * nt + t, 0)),
                  pl.BlockSpec((tt, GATE_PAD), lambda b, t: (b * nt + t, 0)),
                  pl.BlockSpec((1, GATE_PAD), lambda b, t: (0, 0)),
                  pl.BlockSpec((1, HW), lambda b, t: (0, 0))],
        out_specs=[pl.BlockSpec((tt, HW), lambda b, t: (b * nt + t, 0)),
                   pl.BlockSpec((1, HEADS, DH, DH), lambda b, t: (b, 0, 0, 0)),
                   pl.BlockSpec((1, 1, HW), lambda b, t: (b, 0, 0)),
                   pl.BlockSpec((1, 1, HW), lambda b, t: (b, 0, 0))],
        out_shape=[jax.ShapeDtypeStruct((B * S, HW), F32),
                   jax.ShapeDtypeStruct((B, HEADS, DH, DH), F32),
                   jax.ShapeDtypeStruct((B, 1, HW), F32),
                   jax.ShapeDtypeStruct((B, 1, HW), F32)],
        scratch_shapes=[pltpu.VMEM((HW, HW), F32), pltpu.VMEM((1, HW), F32), pltpu.VMEM((1, HW), F32),
                        pltpu.VMEM((tt, HW), F32), pltpu.VMEM((tt, HW), F32)],
        compiler_params=_cparams("parallel", "arbitrary"),
    )(za, gates, bias, norm.reshape(1, HW))
    return ha, c, n.reshape(B, HEADS, DH), m[:, 0, ::DH]


def _gdn_prompt_kernel(zc_ref, zg_ref, gt_ref, alog_ref, dtb_ref, norm_ref,
                       hc_ref, s_ref,
                       s_scr, q_scr, k_scr, v_scr, beta_scr, g_scr, uv_scr, wq_scr, qkm_scr, kwt_scr,
                       *, nseq, nchunks, group):
    tb = pl.program_id(1)
    L = CHUNK
    tt = nchunks * L
    tril, strict, eye, bd = _lane_cat_masks(L)
    ones_bd = bd.astype(BF16)

    @pl.when(tb == 0)
    def _():
        s_scr[...] = jnp.zeros_like(s_scr)

    for i in range(nseq):
        sr = slice(i * tt, (i + 1) * tt)
        y = zc_ref[i]
        q_raw, k_raw = y[:, 0:HW], y[:, HW:2 * HW]
        q_scr[sr, :] = (q_raw * lax.rsqrt(_dot_sel_r(q_raw * q_raw, ones_bd) + EPS) * (DH ** -0.5)).astype(BF16)
        k_scr[sr, :] = k_raw * lax.rsqrt(_dot_sel_r(k_raw * k_raw, ones_bd) + EPS)
        v_scr[sr, :] = y[:, 2 * HW:3 * HW]
        gt = gt_ref[i]
        beta_scr[sr, :] = _dot_sel_r(_sigmoid(gt), _head_expander(2 * HEADS))
        la_cols = -jnp.exp(alog_ref[...]) * _softplus(gt + dtb_ref[...])
        g_scr[sr, :] = _dot_sel_r(_chunk_cumsum(la_cols), _head_expander(3 * HEADS))

    def prepare(gi, carry):
        cis = [gi * group + j for j in range(group)]
        rows = [pl.ds(pl.multiple_of(ci * L, L), L) for ci in cis]
        k = [k_scr[r, :] for r in rows]
        g_x = [g_scr[r, :] for r in rows]
        beta_row = [_col_to_row(beta_scr[r, :], eye) for r in rows]
        dec_incl = [jnp.where(tril, jnp.exp(jnp.where(tril, g - _col_to_row(g, eye), 0.0)), 0.0) for g in g_x]
        k_bd = [_expand_bd(kk.astype(BF16), bd) for kk in k]
        n0 = [-(jnp.where(strict, d, 0.0) * _dot_nt(kk.astype(BF16), kbd) * br)
              for d, kk, kbd, br in zip(dec_incl, k, k_bd, beta_row)]
        for r, d, kbd, br in zip(rows, dec_incl, k_bd, beta_row):
            qkm_scr[r, :] = (_dot_nt(q_scr[r, :], kbd) * d * br).astype(BF16)

        p = [_dot(n.astype(BF16), _expand_bd(n.astype(BF16), bd)) for n in n0]
        m = n0
        steps = int(math.log2(L)) - 1
        for i in range(steps):
            p_bd = [_expand_bd(pp.astype(BF16), bd) for pp in p]
            if i < steps - 1:
                pm = [_dot(jnp.concatenate([pp, mm], axis=0).astype(BF16), pbd) for pp, mm, pbd in zip(p, m, p_bd)]
                p_next, mp = [x[0:L] for x in pm], [x[L:2 * L] for x in pm]
            else:
                p_next, mp = None, [_dot(mm.astype(BF16), pbd) for mm, pbd in zip(m, p_bd)]
            m = [mm + pp + x for mm, pp, x in zip(m, p, mp)]
            p = p_next

        for ci, r, kk, g, mm in zip(cis, rows, k, g_x, m):
            v = v_scr[r, :]
            egk = jnp.exp(g) * kk
            rhs_bd = jnp.concatenate([_expand_bd(v.astype(BF16), bd), _expand_bd(egk.astype(BF16), bd)], axis=1)
            mr = _dot(mm.astype(BF16), rhs_bd)
            uv_scr[r, :] = v + mr[:, 0:HW]
            wq_rows = pl.multiple_of(ci * 2 * L, 2 * L)
            wq_scr[pl.ds(wq_rows, L), :] = (egk + mr[:, HW:2 * HW]).astype(BF16)
            wq_scr[pl.ds(wq_rows + L, L), :] = q_scr[r, :]
            kw = kk * (jnp.exp(g[L - 1:L, :] - g) * beta_scr[r, :])
            kwt_scr[pl.ds(pl.multiple_of(ci * HW, HW), HW), :] = kw.T.astype(BF16)
        return carry

    _unrolled(nseq * nchunks // group, prepare, 0)

    def advance(c, states):
        cis = [i * nchunks + c for i in range(nseq)]
        rows = [pl.ds(pl.multiple_of(ci * L, L), L) for ci in cis]
        wqs = [_dot(wq_scr[pl.ds(pl.multiple_of(ci * 2 * L, 2 * L), 2 * L), :], s.astype(BF16))
               for ci, s in zip(cis, states)]
        ub = [(uv_scr[r, :] - x[0:L]).astype(BF16) for r, x in zip(rows, wqs)]
        new = [jnp.exp(g_scr[pl.ds(pl.multiple_of(ci * L + L - SUBLANES, SUBLANES), SUBLANES), :][SUBLANES - 1:, :]) * s
               + jnp.where(bd, _dot(kwt_scr[pl.ds(pl.multiple_of(ci * HW, HW), HW), :], u), 0.0)
               for ci, s, u in zip(cis, states, ub)]
        for i, (r, x, u) in enumerate(zip(rows, wqs, ub)):
            o = jnp.exp(g_scr[r, :]) * x[L:2 * L] + _dot(qkm_scr[r, :], _expand_bd(u, bd))
            hc_ref[i, pl.ds(pl.multiple_of(c * L, L), L), :] = (
                _head_rms(o, ones_bd, norm_ref[...]) * _silu(zg_ref[i, pl.ds(pl.multiple_of(c * L, L), L), :]))
        return tuple(new)

    states = _unrolled(nchunks, advance, tuple(s_scr[i] for i in range(nseq)))
    for i in range(nseq):
        s_scr[i] = states[i]

    @pl.when(tb == pl.num_programs(1) - 1)
    def _():
        for i in range(nseq):
            for h in range(HEADS):
                s_ref[i, h] = s_scr[i, h * DH:(h + 1) * DH, h * DH:(h + 1) * DH]


def _gdn_gate_rows(a_log, dt_bias):
    z = jnp.zeros((1, GATE_PAD), F32)
    return (z.at[0, 3 * HEADS:4 * HEADS].set(a_log), z.at[0, 3 * HEADS:4 * HEADS].set(dt_bias))


def gdn_prompt(zc, zg, gates, a_log, dt_bias, norm, B, S, tt):
    assert S % tt == 0 and tt % CHUNK == 0
    nt = S // tt
    nseq = math.gcd(B, GDN_SEQS)
    rows = nseq * tt
    alog, dtb = _gdn_gate_rows(a_log, dt_bias)
    blk = lambda w: pl.BlockSpec((nseq, tt, w), lambda b, t: (b, t, 0))
    row = lambda w: pl.BlockSpec((1, w), lambda b, t: (0, 0))
    hc, s = pl.pallas_call(
        functools.partial(_gdn_prompt_kernel, nseq=nseq, nchunks=tt // CHUNK, group=math.gcd(rows // CHUNK, 16)),
        grid=(B // nseq, nt),
        in_specs=[blk(3 * HW), blk(HW), blk(GATE_PAD), row(GATE_PAD), row(GATE_PAD), row(HW)],
        out_specs=[blk(HW), pl.BlockSpec((nseq, HEADS, DH, DH), lambda b, t: (b, 0, 0, 0))],
        out_shape=[jax.ShapeDtypeStruct((B, S, HW), F32), jax.ShapeDtypeStruct((B, HEADS, DH, DH), F32)],
        scratch_shapes=[pltpu.VMEM((nseq, HW, HW), F32),
                        pltpu.VMEM((rows, HW), BF16), pltpu.VMEM((rows, HW), F32), pltpu.VMEM((rows, HW), F32),
                        pltpu.VMEM((rows, HW), F32), pltpu.VMEM((rows, HW), F32), pltpu.VMEM((rows, HW), F32),
                        pltpu.VMEM((2 * rows, HW), BF16), pltpu.VMEM((rows, HW), BF16),
                        pltpu.VMEM((rows // CHUNK * HW, CHUNK), BF16)],
        compiler_params=_cparams("parallel", "arbitrary"),
    )(zc.reshape(B, S, 3 * HW), zg.reshape(B, S, HW), gates.reshape(B, S, GATE_PAD), alog, dtb,
      jnp.tile(norm, HEADS).reshape(1, HW))
    return hc.reshape(B * S, HW), s


def _s5_prep_kernel(are_ref, aim_ref, ldt_ref, bre_ref, bim_ref, lre_ref, lim_ref, bbre_ref, bbim_ref):
    a_re, a_im = are_ref[...], aim_ref[...]
    dt = jnp.exp(ldt_ref[...])
    mag = jnp.exp(a_re * dt)
    lam_re, lam_im = mag * jnp.cos(a_im * dt), mag * jnp.sin(a_im * dt)
    lre_ref[...] = lam_re
    lim_ref[...] = lam_im
    nr, ni = lam_re - 1.0, lam_im
    den = a_re * a_re + a_im * a_im
    coef_re = (nr * a_re + ni * a_im) / den
    coef_im = (ni * a_re - nr * a_im) / den
    b_re, b_im = bre_ref[...], bim_ref[...]
    bbre_ref[...] = coef_re * b_re - coef_im * b_im
    bbim_ref[...] = coef_re * b_im + coef_im * b_re


def s5_params(lp):
    G, N, P = S5_GROUPS, S5_N, S5_P
    row = lambda a: a.astype(F32).reshape(1, G * N)
    to_pn = lambda b: jnp.transpose(b.astype(F32), (2, 0, 1)).reshape(P, G * N)
    shp = [jax.ShapeDtypeStruct((1, G * N), F32)] * 2 + [jax.ShapeDtypeStruct((P, G * N), F32)] * 2
    lam_re, lam_im, bb_re, bb_im = pl.pallas_call(_s5_prep_kernel, out_shape=shp)(
        row(lp['s5_a_re']), row(lp['s5_a_im']), row(jnp.repeat(lp['s5_log_dt'][:, None], N, axis=1)),
        to_pn(lp['s5_b_re']), to_pn(lp['s5_b_im']))
    eye = jnp.eye(S5_GB, dtype=F32)

    def w_in_blocks(bb):
        b4 = bb.reshape(P, S5_NBLK, S5_GB, N)
        return jnp.einsum('pbgn,gh->bgphn', b4, eye).reshape(S5_NBLK, S5_GB * P, S5_GB * N)

    def w_out_blocks(c):
        c4 = c.astype(F32).reshape(S5_NBLK, S5_GB, P, N)
        return jnp.einsum('bgpn,gh->bgnhp', c4, eye).reshape(S5_NBLK, S5_GB * N, S5_GB * P)

    w_in = jnp.concatenate([w_in_blocks(bb_re), w_in_blocks(bb_im)], axis=-1).astype(BF16)
    return {'lam_re': lam_re, 'lam_im': lam_im, 'w_in': w_in,
            'w_out_re': w_out_blocks(lp['s5_c_re']).astype(BF16),
            'w_out_im': (-w_out_blocks(lp['s5_c_im'])).astype(BF16),
            'd': lp['s5_d'].astype(F32).reshape(1, S5_WIDTH), 'w_glu': lp['s5_w_glu'].astype(BF16)}


def _s5_kernel(u_ref, h0r_ref, h0i_ref, lamr_ref, lami_ref, win_ref, wor_ref, woi_ref, d_ref, wglu_ref,
               ys_ref, h1r_ref, h1i_ref, hr_scr, hi_scr, br_scr, bi_scr, *, nseq, rows, bb, lane_blk):
    tb = pl.program_id(0)
    blk_in, blk_st = S5_GB * S5_P, S5_GB * S5_N

    @pl.when(tb == 0)
    def _():
        hr_scr[...] = h0r_ref[...]
        hi_scr[...] = h0i_ref[...]

    if nseq > 1:
        u = jnp.swapaxes(u_ref[...], 0, 1).reshape(nseq * rows, S5_WIDTH)
    else:
        u = u_ref[0]
    ub = u.astype(BF16)
    lanes = [slice(lb * lane_blk, (lb + 1) * lane_blk) for lb in range(S5_STATE // lane_blk)]
    lam = [(jnp.broadcast_to(lamr_ref[:, ls], (bb, lane_blk)), jnp.broadcast_to(lami_ref[:, ls], (bb, lane_blk)))
           for ls in lanes]
    h = [(hr_scr[:, ls], hi_scr[:, ls]) for ls in lanes]

    slab = min(nseq * rows, S5_SLAB_ROWS)
    for c in range(nseq * rows // slab):
        rs = slice(c * slab, (c + 1) * slab)
        for blk in range(S5_NBLK):
            bu = _dot(ub[rs, blk * blk_in:(blk + 1) * blk_in], win_ref[blk])
            br_scr[rs, blk * blk_st:(blk + 1) * blk_st] = bu[:, 0:blk_st]
            bi_scr[rs, blk * blk_st:(blk + 1) * blk_st] = bu[:, blk_st:2 * blk_st]

        for lb, ls in enumerate(lanes):
            (lr, li), (hr, hi) = lam[lb], h[lb]
            for t in range(slab // bb):
                r = slice(c * slab + t * bb, c * slab + (t + 1) * bb)
                hr, hi = lr * hr - li * hi + br_scr[r, ls], lr * hi + li * hr + bi_scr[r, ls]
                br_scr[r, ls] = hr
                bi_scr[r, ls] = hi
            h[lb] = (hr, hi)

        ys = []
        for blk in range(S5_NBLK):
            st = slice(blk * blk_st, (blk + 1) * blk_st)
            ys.append(_dot(br_scr[rs, st].astype(BF16), wor_ref[blk]) + _dot(bi_scr[rs, st].astype(BF16), woi_ref[blk]))
        gy = jax.nn.gelu(jnp.concatenate(ys, axis=-1) + d_ref[...] * u[rs])
        out = gy * _sigmoid(_dot(gy.astype(BF16), wglu_ref[...]))
        if nseq > 1:
            tpc = slab // nseq
            ys_ref[:, c * tpc:(c + 1) * tpc, :] = jnp.swapaxes(out.reshape(tpc, nseq, S5_WIDTH), 0, 1)
        else:
            ys_ref[0, rs, :] = out

    for ls, (hr, hi) in zip(lanes, h):
        hr_scr[:, ls] = hr
        hi_scr[:, ls] = hi

    @pl.when(tb == pl.num_programs(0) - 1)
    def _():
        h1r_ref[...] = hr_scr[...]
        h1i_ref[...] = hi_scr[...]


def s5_mixer(u, h0_re, h0_im, sp, tt, single_step):
    B = h0_re.shape[0]
    nseq, S = (1, 1) if single_step else (B, u.shape[1])
    rows = B if single_step else tt
    assert S % tt == 0 and B % SUBLANES == 0
    assert u.shape == ((1, B, S5_WIDTH) if single_step else (B, S, S5_WIDTH))
    lane_blk = max(LANES, min(S5_STATE, (SUBLANES * SUBLANES * LANES) // B))
    full = lambda shape: pl.BlockSpec(shape, lambda t: (0,) * len(shape))
    return pl.pallas_call(
        functools.partial(_s5_kernel, nseq=nseq, rows=rows, bb=B, lane_blk=lane_blk),
        grid=(S // tt,),
        in_specs=[pl.BlockSpec((nseq, rows, S5_WIDTH), lambda t: (0, t, 0)),
                  full((B, S5_STATE)), full((B, S5_STATE)), full((1, S5_STATE)), full((1, S5_STATE)),
                  full(sp['w_in'].shape), full(sp['w_out_re'].shape), full(sp['w_out_im'].shape),
                  full((1, S5_WIDTH)), full((S5_WIDTH, S5_WIDTH))],
        out_specs=[pl.BlockSpec((nseq, rows, S5_WIDTH), lambda t: (0, t, 0)),
                   full((B, S5_STATE)), full((B, S5_STATE))],
        out_shape=[jax.ShapeDtypeStruct(u.shape, F32),
                   jax.ShapeDtypeStruct((B, S5_STATE), F32), jax.ShapeDtypeStruct((B, S5_STATE), F32)],
        scratch_shapes=[pltpu.VMEM((B, S5_STATE), F32), pltpu.VMEM((B, S5_STATE), F32),
                        pltpu.VMEM((nseq * rows, S5_STATE), F32), pltpu.VMEM((nseq * rows, S5_STATE), F32)],
        compiler_params=_cparams("arbitrary"),
    )(u, h0_re, h0_im, sp['lam_re'], sp['lam_im'], sp['w_in'], sp['w_out_re'], sp['w_out_im'], sp['d'], sp['w_glu'])


def _ones_bd():
    r = lax.broadcasted_iota(jnp.int32, (HW, HW), 0) // DH
    c = lax.broadcasted_iota(jnp.int32, (HW, HW), 1) // DH
    return (r == c).astype(BF16)


def _mlstm_sample_kernel(za_ref, gt_ref, bias_ref, norm_ref, c_ref, n_ref, m_ref,
                         ha_ref, c_out, n_out, m_out, q_scr, kw_scr, h_scr):
    za = za_ref[...]
    q_scr[...] = (za[:, 0:HW] * (DH ** -0.5)).T
    k_t = za[:, HW:2 * HW].T
    v_t = za[:, 2 * HW:3 * HW].T
    g_t = (gt_ref[...] + bias_ref[...]).T
    m_out[...] = jnp.zeros_like(m_out)
    for h in range(HEADS):
        hs = slice(h * DH, (h + 1) * DH)
        i_h = g_t[h:h + 1, :]
        bm = _log_sigmoid(g_t[HEADS + h:HEADS + h + 1, :]) + m_ref[h:h + 1, :]
        m_t = jnp.maximum(i_h, bm)
        w_in = jnp.exp(i_h - m_t)
        w_st = jnp.exp(bm - m_t)
        q_h, k_h, v_h = q_scr[hs, :], k_t[hs, :], v_t[hs, :]
        s = jnp.sum(q_h * k_h, axis=0, keepdims=True) * w_in
        kw_scr[hs, :] = k_h * w_in

        def body(d, acc):
            r = h * DH + d
            rows = pl.ds(pl.multiple_of(r * DH, DH), DH)
            c_hd = c_ref[rows, :]
            c_out[rows, :] = w_st * c_hd + kw_scr[pl.ds(r, 1), :] * v_h
            return acc + q_scr[pl.ds(r, 1), :] * c_hd

        qc = lax.fori_loop(0, DH, body, jnp.zeros((DH, za.shape[0]), F32), unroll=4)
        n_h = n_ref[hs, :]
        num = w_st * qc + s * v_h
        den = w_st * jnp.sum(q_h * n_h, axis=0, keepdims=True) + s
        h_scr[hs, :] = num / jnp.maximum(jnp.abs(den), jnp.exp(-m_t))
        n_out[hs, :] = w_st * n_h + kw_scr[hs, :]
        m_out[h:h + 1, :] = m_t
    ha_ref[...] = _head_rms(h_scr[...].T, _ones_bd(), norm_ref[...]) * _sigmoid(za[:, 3 * HW:4 * HW])


def mlstm_sample(za, gates, b_i, b_f, norm, c_t, n_t, m_t):
    B = za.shape[0]
    bias = jnp.zeros((1, GATE_PAD), F32).at[0, 0:HEADS].set(b_i).at[0, HEADS:2 * HEADS].set(b_f)
    shp = lambda *s: jax.ShapeDtypeStruct(s, F32)
    return pl.pallas_call(
        _mlstm_sample_kernel,
        out_shape=[shp(B, HW), shp(HW * DH, B), shp(HW, B), shp(SUBLANES, B)],
        scratch_shapes=[pltpu.VMEM((HW, B), F32), pltpu.VMEM((HW, B), F32), pltpu.VMEM((HW, B), F32)],
        compiler_params=pltpu.CompilerParams(vmem_limit_bytes=VMEM_LIMIT),
    )(za, gates, bias, norm.reshape(1, HW), c_t, n_t, m_t)


def _gdn_sample_kernel(zc_ref, zg_ref, gt_ref, buf_ref, cw_ref, alog_ref, dtb_ref, norm_ref, s_ref,
                       hc_ref, s_out, buf_out, q_scr, k_scr, o_scr):
    W3 = 3 * HW
    x = zc_ref[...]
    y = cw_ref[CONV_K - 1:CONV_K, :] * x
    for j in range(CONV_K - 1):
        y = y + cw_ref[j:j + 1, :] * buf_ref[:, j * W3:(j + 1) * W3]
    buf_out[:, 0:(CONV_K - 2) * W3] = buf_ref[:, W3:(CONV_K - 1) * W3]
    buf_out[:, (CONV_K - 2) * W3:(CONV_K - 1) * W3] = x
    y = _silu(y)
    ones_bd = _ones_bd()
    q_raw, k_raw = y[:, 0:HW], y[:, HW:2 * HW]
    q_scr[...] = (q_raw * lax.rsqrt(_dot_sel_r(q_raw * q_raw, ones_bd) + EPS) * (DH ** -0.5)).T
    k_scr[...] = (k_raw * lax.rsqrt(_dot_sel_r(k_raw * k_raw, ones_bd) + EPS)).T
    v_t = y[:, 2 * HW:3 * HW].T
    gt = gt_ref[...]
    beta_t = _sigmoid(gt).T
    la_t = (-jnp.exp(alog_ref[...]) * _softplus(gt + dtb_ref[...])).T
    nb = x.shape[0]
    for h in range(HEADS):
        hs = slice(h * DH, (h + 1) * DH)
        beta = beta_t[2 * HEADS + h:2 * HEADS + h + 1, :]
        eg = jnp.exp(la_t[3 * HEADS + h:3 * HEADS + h + 1, :])
        q_h, k_h, v_h = q_scr[hs, :], k_scr[hs, :], v_t[hs, :]

        def read(d, acc):
            ks, qs = acc
            r = h * DH + d
            s_hd = s_ref[pl.ds(pl.multiple_of(r * DH, DH), DH), :]
            return ks + k_scr[pl.ds(r, 1), :] * s_hd, qs + q_scr[pl.ds(r, 1), :] * s_hd

        zero = jnp.zeros((DH, nb), F32)
        ks, qs = lax.fori_loop(0, DH, read, (zero, zero), unroll=4)
        u = v_h - eg * ks
        o_scr[hs, :] = eg * qs + (jnp.sum(q_h * k_h, axis=0, keepdims=True) * beta) * u

        def write(d, carry):
            r = h * DH + d
            rows = pl.ds(pl.multiple_of(r * DH, DH), DH)
            s_out[rows, :] = eg * s_ref[rows, :] + (beta * k_scr[pl.ds(r, 1), :]) * u
            return carry

        lax.fori_loop(0, DH, write, 0, unroll=4)
    hc_ref[...] = _head_rms(o_scr[...].T, ones_bd, norm_ref[...]) * _silu(zg_ref[...])


def gdn_sample(zc, zg, gates, buf, conv_w, a_log, dt_bias, norm, s_t):
    B = zc.shape[0]
    alog, dtb = _gdn_gate_rows(a_log, dt_bias)
    shp = lambda *s: jax.ShapeDtypeStruct(s, F32)
    return pl.pallas_call(
        _gdn_sample_kernel,
        out_shape=[shp(B, HW), shp(HW * DH, B), shp(B, (CONV_K - 1) * 3 * HW)],
        scratch_shapes=[pltpu.VMEM((HW, B), F32), pltpu.VMEM((HW, B), F32), pltpu.VMEM((HW, B), F32)],
        compiler_params=pltpu.CompilerParams(vmem_limit_bytes=VMEM_LIMIT),
    )(zc, zg, gates, buf, conv_w, alog, dtb, jnp.tile(norm, HEADS).reshape(1, HW), s_t)


FFN_CHUNK = 704
IN_SEGMENTS = (4 * HW, S5_WIDTH, 3 * HW, HW, GATE_PAD)


def _tile(n, pref):
    return pref if n % pref == 0 else n


def _regroup_w_in_kernel(w_ref, o_ref):
    a, g2 = 4 * HW, 2 * HEADS
    mid = sum(IN_SEGMENTS[1:4])
    w = w_ref[...]
    o_ref[:, 0:a] = w[:, 0:a].astype(BF16)
    o_ref[:, a:a + mid] = w[:, a + g2:a + g2 + mid].astype(BF16)
    gates = jnp.concatenate([w[:, a:a + g2], w[:, a + g2 + mid:a + 2 * g2 + mid],
                             jnp.zeros((w.shape[0], GATE_PAD - 2 * g2), F32)], axis=1)
    o_ref[:, a + mid:a + mid + GATE_PAD] = gates.astype(BF16)


def regroup_w_in(w_in, tk):
    L, D, n_in = w_in.shape
    n_out = sum(IN_SEGMENTS)
    return pl.pallas_call(
        _regroup_w_in_kernel,
        grid=(L, D // tk),
        in_specs=[pl.BlockSpec((None, tk, n_in), lambda l, i: (l, i, 0))],
        out_specs=pl.BlockSpec((None, tk, n_out), lambda l, i: (l, i, 0)),
        out_shape=jax.ShapeDtypeStruct((L, D, n_out), BF16),
        compiler_params=_cparams("parallel", "parallel"),
    )(w_in)


def prep_weights(p):
    w_in = regroup_w_in(p['w_in'], 256)
    bf = lambda n: p[n].astype(BF16)
    return {'w_in': w_in, 'w_out': bf('w_out'), 'w_mq': bf('w_mq'), 'w_mo': bf('w_mo'), 'w_mk': bf('w_mk'),
            'w_mv': bf('w_mv'), 'w_gate': bf('w_gate'), 'w_up': bf('w_up'), 'w_down': bf('w_down')}


def mixer_prompt(x, lp, W, layer, B, S):
    T = B * S
    za, zu, zc, zg, gates, tail = in_proj_prompt(x, lp['norm_mix'], W['w_in'], layer, lp['gdn_conv_w'], S,
                                                 _tile(S, 1024))
    buf1 = tail[:, SUBLANES - (CONV_K - 1):, :]
    tt = _tile(S, 1024)
    ha, c1, n1, m1 = mlstm_prompt(za, gates, lp['mlstm_b_i'], lp['mlstm_b_f'], lp['mlstm_norm'], B, S, tt)
    h0 = jnp.zeros((B, S5_STATE), F32)
    ys3, r1, i1 = s5_mixer(zu.reshape(B, S, S5_WIDTH), h0, h0, lp['s5'], _tile(S, 128), False)
    ys = ys3.reshape(T, S5_WIDTH)
    hc, s1 = gdn_prompt(zc, zg, gates, lp['gdn_a_log'], lp['gdn_dt_bias'], lp['gdn_norm'], B, S, _tile(S, 256))
    return [ha, ys, hc], (c1, n1, m1, r1.reshape(B, S5_GROUPS, S5_N), i1.reshape(B, S5_GROUPS, S5_N), s1, buf1)


def mixer_sample(x, st, lp, W, layer):
    B = x.shape[0]
    c0, n0, m0, r0, i0, s0, buf0 = st
    za, zu, zc, zg, gates = norm_matmul(x, lp['norm_mix'], W['w_in'], layer, IN_SEGMENTS, B)
    m_t = jnp.zeros((SUBLANES, B), F32).at[0:HEADS, :].set(m0.T)
    ha, c1t, n1t, m1t = mlstm_sample(za, gates, lp['mlstm_b_i'], lp['mlstm_b_f'], lp['mlstm_norm'],
                                     c0.reshape(B, HW * DH).T, n0.reshape(B, HW).T, m_t)
    ys3, r1, i1 = s5_mixer(zu.reshape(1, B, S5_WIDTH), r0.reshape(B, S5_STATE), i0.reshape(B, S5_STATE), lp['s5'], 1, True)
    hc, s1t, buf1 = gdn_sample(zc, zg, gates, buf0.reshape(B, (CONV_K - 1) * 3 * HW), lp['gdn_conv_w'],
                               lp['gdn_a_log'], lp['gdn_dt_bias'], lp['gdn_norm'], s0.reshape(B, HW * DH).T)
    x1 = matmul_residual(x, [ha, ys3.reshape(B, S5_WIDTH), hc], W['w_out'], layer, B)
    return x1, (c1t.T.reshape(B, HEADS, DH, DH), n1t.T.reshape(B, HEADS, DH), m1t[0:HEADS, :].T,
                r1.reshape(B, S5_GROUPS, S5_N), i1.reshape(B, S5_GROUPS, S5_N),
                s1t.T.reshape(B, HEADS, DH, DH), buf1.reshape(B, CONV_K - 1, 3 * HW))


def _mem_kv_kernel(x_ref, g_ref, wk_ref, wv_ref, k2_ref, v2_ref, k5_ref, v5_ref):
    xn = _rms(x_ref[...], g_ref[...]).astype(BF16)
    for w_ref, o2_ref, o5_ref in ((wk_ref, k2_ref, k5_ref), (wv_ref, v2_ref, v5_ref)):
        y = _dot(xn, w_ref[...])
        o2_ref[...] = y.astype(BF16)
        o5_ref[...] = y.reshape(o5_ref.shape)


def mem_kv(mem, norm_mem, wk, wv):
    B, M, D = mem.shape
    L = wk.shape[0]
    dh = D // X_HEADS
    w_spec = pl.BlockSpec((None, D, D), lambda l, b: (l, 0, 0))
    o2_spec = pl.BlockSpec((None, None, M, D), lambda l, b: (l, b, 0, 0))
    o5_spec = pl.BlockSpec((None, None, M, X_HEADS, dh), lambda l, b: (l, b, 0, 0, 0))
    return pl.pallas_call(
        _mem_kv_kernel,
        grid=(L, B),
        in_specs=[pl.BlockSpec((None, M, D), lambda l, b: (b, 0, 0)),
                  pl.BlockSpec((None, 1, D), lambda l, b: (l, 0, 0)), w_spec, w_spec],
        out_specs=[o2_spec, o2_spec, o5_spec, o5_spec],
        out_shape=[jax.ShapeDtypeStruct((L, B, M, D), BF16)] * 2
                  + [jax.ShapeDtypeStruct((L, B, M, X_HEADS, dh), F32)] * 2,
        compiler_params=_cparams("parallel", "parallel"),
    )(mem, norm_mem.reshape(L, 1, D), wk, wv)


def xattn_ffn_prompt(x, acts, mk, mv, lp, W, layer, S, norm_final, final):
    T, D = x.shape
    x2 = xattn_prompt(x, acts, W['w_out'], lp['norm_xattn'], W['w_mq'], mk, mv, W['w_mo'], layer, S, _tile(S, 1024))
    return ffn(x2, lp['norm_ffn'], W['w_gate'], W['w_up'], W['w_down'], layer, norm_final, final, _tile(T, 1024),
               FFN_CHUNK)


def xattn_ffn_sample(x, ck, cv, lp, W, layer, norm_final, final):
    B, D = x.shape
    (q,) = norm_matmul(x, lp['norm_xattn'], W['w_mq'], layer, (D,), B)
    o = xattn_sample(q, ck, cv, layer, 8)
    x2 = matmul_residual(x, [o], W['w_mo'], layer, B)
    return ffn(x2, lp['norm_ffn'], W['w_gate'], W['w_up'], W['w_down'], layer, norm_final, final, B, FFN_CHUNK)


LAYER_PARAMS = ('norm_mix', 'w_in', 'w_out', 'mlstm_b_i', 'mlstm_b_f', 'mlstm_norm', 's5_a_re', 's5_a_im', 's5_log_dt',
                's5_b_re', 's5_b_im', 's5_c_re', 's5_c_im', 's5_d', 's5_w_glu', 'gdn_conv_w', 'gdn_a_log',
                'gdn_dt_bias', 'gdn_norm', 'norm_xattn', 'norm_mem', 'w_mq', 'w_mk', 'w_mv', 'w_mo', 'norm_ffn',
                'w_gate', 'w_up', 'w_down')


def kernel(x_prompt, x_sample, mem_prompt, cache_mem_k, cache_mem_v, state_mlstm_c, state_mlstm_n, state_mlstm_m, state_s5_re, state_s5_im, state_gdn, state_gdn_conv, norm_mix, w_in, w_out, mlstm_b_i, mlstm_b_f, mlstm_norm, s5_a_re, s5_a_im, s5_log_dt, s5_b_re, s5_b_im, s5_c_re, s5_c_im, s5_d, s5_w_glu, gdn_conv_w, gdn_a_log, gdn_dt_bias, gdn_norm, norm_xattn, norm_mem, w_mq, w_mk, w_mv, w_mo, norm_ffn, w_gate, w_up, w_down, norm_final):
    stacked = dict(norm_mix=norm_mix, w_in=w_in, w_out=w_out, mlstm_b_i=mlstm_b_i, mlstm_b_f=mlstm_b_f,
                   mlstm_norm=mlstm_norm, s5_a_re=s5_a_re, s5_a_im=s5_a_im, s5_log_dt=s5_log_dt, s5_b_re=s5_b_re,
                   s5_b_im=s5_b_im, s5_c_re=s5_c_re, s5_c_im=s5_c_im, s5_d=s5_d, s5_w_glu=s5_w_glu,
                   gdn_conv_w=gdn_conv_w, gdn_a_log=gdn_a_log, gdn_dt_bias=gdn_dt_bias, gdn_norm=gdn_norm,
                   norm_xattn=norm_xattn, norm_mem=norm_mem, w_mq=w_mq, w_mk=w_mk, w_mv=w_mv, w_mo=w_mo,
                   norm_ffn=norm_ffn, w_gate=w_gate, w_up=w_up, w_down=w_down)
    B, S, D = x_prompt.shape
    Bs = x_sample.shape[0]
    M = mem_prompt.shape[1]
    depth = w_in.shape[0]
    xp = x_prompt.reshape(B * S, D)
    xs = x_sample.reshape(Bs, D)
    cache_k, cache_v = cache_mem_k, cache_mem_v
    st_p, st_s = [], []
    W = prep_weights(stacked)
    mk, mv, mem_k, mem_v = mem_kv(mem_prompt, norm_mem, W['w_mk'], W['w_mv'])
    for l in range(depth):
        lp = {n: stacked[n][l] for n in LAYER_PARAMS if n not in W}
        lp['s5'] = s5_params(lp)
        last = l == depth - 1
        acts, sp = mixer_prompt(xp, lp, W, l, B, S)
        xp = xattn_ffn_prompt(xp, acts, mk, mv, lp, W, l, S, norm_final, last)
        st_in = (state_mlstm_c[l], state_mlstm_n[l], state_mlstm_m[l], state_s5_re[l], state_s5_im[l],
                 state_gdn[l], state_gdn_conv[l])
        xs, ss = mixer_sample(xs, st_in, lp, W, l)
        xs = xattn_ffn_sample(xs, cache_k, cache_v, lp, W, l, norm_final, last)
        st_p.append(sp)
        st_s.append(ss)
    stack = lambda lst: [jnp.stack([st[i] for st in lst]) for i in range(7)]
    return (xp.reshape(B, S, D), xs.reshape(Bs, 1, D), mem_k, mem_v,
            *stack(st_p), *stack(st_s))
```
